```python
import math
import jax, jax.numpy as jnp
from jax import lax
import numpy as np

D_MODEL = 1024
BATCH = 8
SEQ = 8192
DEPTH = 1

N_META = 16
CHUNK = 64
GDN_HEADS = 8
GDN_DK = 128
GDN_DV = 128
GDN_CONV = 4
RET_HEADS = 4
RET_DK = 256
RET_DV = 256
D_FF = 2816
FFN_CONV = 3
ROPE_BASE = 10000.0
EPS = 1e-6

GDN_QK = GDN_HEADS * GDN_DK
GDN_V = GDN_HEADS * GDN_DV
RET_QK = RET_HEADS * RET_DK
RET_V = RET_HEADS * RET_DV
IN_WIDTHS = (GDN_QK, GDN_QK, GDN_V, GDN_V, GDN_HEADS, GDN_HEADS,
             RET_QK, RET_QK, RET_V, RET_V, D_MODEL, D_MODEL)
D_IN = sum(IN_WIDTHS)
SPLIT_IDX = tuple(int(i) for i in np.cumsum(IN_WIDTHS)[:-1])

kernel_name = 'hybrid_gdn_retention_convffn_meta'


def rmsnorm(x, g):
    xf = x.astype(jnp.float32)
    y = xf * lax.rsqrt(jnp.mean(xf * xf, axis=-1, keepdims=True) + EPS)
    return (y * g.astype(jnp.float32)).astype(x.dtype)


def causal_dwconv(x, w):
    K, C = w.shape
    return lax.conv_general_dilated(
        x, w[:, None, :].astype(x.dtype), window_strides=(1,), padding=[(K - 1, 0)],
        dimension_numbers=('NWC', 'WIO', 'NWC'), feature_group_count=C)


def l2norm(x):
    return x * lax.rsqrt(jnp.sum(x * x, axis=-1, keepdims=True) + EPS)


def rotary(x, pos):
    half = x.shape[-1] // 2
    inv = 1.0 / (ROPE_BASE ** (jnp.arange(half, dtype=jnp.float32) / half))
    ang = pos[:, None] * inv[None, :]
    cos = jnp.cos(ang)[None, :, None, :]
    sin = jnp.sin(ang)[None, :, None, :]
    x1, x2 = x[..., :half], x[..., half:]
    return jnp.concatenate([x1 * cos - x2 * sin, x2 * cos + x1 * sin], axis=-1)


def to_chunks(t, pad):
    widths = [(0, 0), (pad, 0)] + [(0, 0)] * (t.ndim - 2)
    t = jnp.pad(t, widths)
    B, Lp = t.shape[0], t.shape[1]
    t = t.reshape((B, Lp // CHUNK, CHUNK) + t.shape[2:])
    perm = (0, 3, 1, 2) + tuple(range(4, t.ndim))
    return t.transpose(perm)


def from_chunks(o, pad):
    N, B, H, C, d = o.shape
    o = o.transpose(1, 0, 3, 2, 4).reshape(B, N * C, H, d)
    return o[:, pad:]


def gated_delta_chunked(q, k, v, log_g, beta):
    L = q.shape[1]
    pad = (-L) % CHUNK
    dk = q.shape[-1]
    q = to_chunks(q * (dk ** -0.5), pad)
    k = to_chunks(k, pad)
    v = to_chunks(v, pad)
    beta = to_chunks(beta, pad)
    G = jnp.cumsum(to_chunks(log_g, pad), axis=-1)
    tril = jnp.asarray(np.tril(np.ones((CHUNK, CHUNK), dtype=bool)))
    strict = jnp.asarray(np.tril(np.ones((CHUNK, CHUNK), dtype=bool), -1))
    diff = G[..., :, None] - G[..., None, :]
    decay = jnp.where(tril, jnp.exp(jnp.where(tril, diff, 0.0)), 0.0)
    kb = k * beta[..., None]
    M = jnp.where(strict, jnp.einsum('bhncd,bhnsd->bhncs', kb, k) * decay, 0.0)
    eye = jnp.eye(CHUNK, dtype=M.dtype)
    T = lax.linalg.triangular_solve(eye + M, jnp.broadcast_to(eye, M.shape),
                                    left_side=True, lower=True, unit_diagonal=True)
    u = jnp.einsum('bhncs,bhnsd->bhncd', T, v * beta[..., None])
    w = jnp.einsum('bhncs,bhnsd->bhncd', T, kb * jnp.exp(G)[..., None])
    attn = jnp.einsum('bhncd,bhnsd->bhncs', q, k) * decay
    q_dec = q * jnp.exp(G)[..., None]
    G_last = G[..., -1]
    k_dec = k * jnp.exp(G_last[..., None] - G)[..., None]
    g_last = jnp.exp(G_last)

    def step(S, xs):
        u_c, w_c, a_c, qd_c, kd_c, gl_c = xs
        v_new = u_c - jnp.einsum('bhcd,bhde->bhce', w_c, S)
        o_c = jnp.einsum('bhcd,bhde->bhce', qd_c, S) + jnp.einsum('bhcs,bhse->bhce', a_c, v_new)
        S = S * gl_c[..., None, None] + jnp.einsum('bhcd,bhce->bhde', kd_c, v_new)
        return S, o_c

    xs = tuple(jnp.moveaxis(t, 2, 0) for t in (u, w, attn, q_dec, k_dec, g_last))
    B, H = q.shape[0], q.shape[1]
    S0 = jnp.zeros((B, H, dk, v.shape[-1]), jnp.float32)
    _, o = lax.scan(step, S0, xs)
    return from_chunks(o, pad)


def retention_chunked(q, k, v):
    H = q.shape[2]
    dk = q.shape[-1]
    lg = jnp.log(1.0 - 2.0 ** (-5.0 - jnp.arange(H, dtype=jnp.float32)))
    L = q.shape[1]
    pad = (-L) % CHUNK
    q = to_chunks(q, pad)
    k = to_chunks(k * (dk ** -0.5), pad)
    v = to_chunks(v, pad)
    idx = jnp.arange(CHUNK, dtype=jnp.float32)
    tril = jnp.asarray(np.tril(np.ones((CHUNK, CHUNK), dtype=bool)))
    dmask = jnp.where(tril, jnp.exp((idx[:, None] - idx[None, :]) * lg[:, None, None]), 0.0)
    inner = jnp.einsum('bhncs,bhnse->bhnce',
                       jnp.einsum('bhncd,bhnsd->bhncs', q, k) * dmask[None, :, None], v)
    q_dec = q * jnp.exp((idx[None, :] + 1.0) * lg[:, None])[None, :, None, :, None]
    k_dec = k * jnp.exp((CHUNK - 1.0 - idx[None, :]) * lg[:, None])[None, :, None, :, None]
    g_chunk = jnp.exp(CHUNK * lg)[None, :, None, None]

    def step(S, xs):
        in_c, qd_c, kd_c, v_c = xs
        o_c = in_c + jnp.einsum('bhcd,bhde->bhce', qd_c, S)
        S = S * g_chunk + jnp.einsum('bhcd,bhce->bhde', kd_c, v_c)
        return S, o_c

    xs = tuple(jnp.moveaxis(t, 2, 0) for t in (inner, q_dec, k_dec, v))
    S0 = jnp.zeros((q.shape[0], H, dk, v.shape[-1]), jnp.float32)
    _, o = lax.scan(step, S0, xs)
    return from_chunks(o, pad)


def mixer_block(hn, pos, w_in, conv_w, a_log, dt_bias, gdn_norm, w_out):
    B, L, _ = hn.shape
    f32 = jnp.float32
    proj = hn @ w_in.astype(hn.dtype)
    gq, gk, gv, gz, ga, gb, rq, rk, rv, rg, gate_a, gate_b = jnp.split(proj, SPLIT_IDX, axis=-1)
    qkv = jax.nn.silu(causal_dwconv(jnp.concatenate([gq, gk, gv], axis=-1), conv_w))
    gq, gk, gv = jnp.split(qkv.astype(f32), (GDN_QK, 2 * GDN_QK), axis=-1)
    gq = l2norm(gq.reshape(B, L, GDN_HEADS, GDN_DK))
    gk = l2norm(gk.reshape(B, L, GDN_HEADS, GDN_DK))
    gv = gv.reshape(B, L, GDN_HEADS, GDN_DV)
    beta = jax.nn.sigmoid(gb.astype(f32))
    log_g = -jnp.exp(a_log.astype(f32)) * jax.nn.softplus(ga.astype(f32) + dt_bias.astype(f32))
    o_a = gated_delta_chunked(gq, gk, gv, log_g, beta)
    o_a = o_a * lax.rsqrt(jnp.mean(o_a * o_a, axis=-1, keepdims=True) + EPS) * gdn_norm.astype(f32)
    y_a = (o_a * jax.nn.silu(gz.astype(f32).reshape(B, L, GDN_HEADS, GDN_DV))).reshape(B, L, GDN_V)
    rq = rotary(rq.astype(f32).reshape(B, L, RET_HEADS, RET_DK), pos)
    rk = rotary(rk.astype(f32).reshape(B, L, RET_HEADS, RET_DK), pos)
    o_b = retention_chunked(rq, rk, rv.astype(f32).reshape(B, L, RET_HEADS, RET_DV))
    o_b = o_b * lax.rsqrt(jnp.mean(o_b * o_b, axis=-1, keepdims=True) + EPS)
    y_b = jax.nn.silu(rg.astype(f32)) * o_b.reshape(B, L, RET_V)
    y = jax.nn.sigmoid(gate_a.astype(f32)) * y_a + jax.nn.sigmoid(gate_b.astype(f32)) * y_b
    return y.astype(hn.dtype) @ w_out.astype(hn.dtype)


def conv_ffn(hn, w_up, conv_w, conv_b, w_down):
    u = causal_dwconv(hn @ w_up.astype(hn.dtype), conv_w) + conv_b.astype(hn.dtype)
    a, b = jnp.split(u, 2, axis=-1)
    return (jax.nn.silu(a) * b) @ w_down.astype(hn.dtype)


def _fwd_setup_inputs(seed: int = 0) -> dict:
    key = jax.random.key(seed)
    ks = jax.random.split(key, 16)
    f32 = jnp.float32
    nrm = lambda k, shape, s: jax.random.normal(k, shape, f32) * s
    dt = jnp.exp(jax.random.uniform(ks[6], (DEPTH, GDN_HEADS), f32, math.log(1e-3), math.log(1e-1)))
    return {
        'x': nrm(ks[0], (BATCH, SEQ, D_MODEL), 1.0),
        'meta': nrm(ks[1], (N_META, D_MODEL), 1.0),
        'norm1': 1.0 + nrm(ks[2], (DEPTH, D_MODEL), 0.02),
        'w_in': nrm(ks[3], (DEPTH, D_MODEL, D_IN), D_MODEL ** -0.5),
        'gdn_conv_w': nrm(ks[4], (DEPTH, GDN_CONV, 2 * GDN_QK + GDN_V), GDN_CONV ** -0.5),
        'gdn_a_log': jnp.log(jax.random.uniform(ks[5], (DEPTH, GDN_HEADS), f32, 1.0, 16.0)),
        'gdn_dt_bias': dt + jnp.log(-jnp.expm1(-dt)),
        'gdn_norm': 1.0 + nrm(ks[7], (DEPTH, GDN_DV), 0.02),
        'w_out': nrm(ks[8], (DEPTH, D_MODEL, D_MODEL), D_MODEL ** -0.5),
        'norm2': 1.0 + nrm(ks[9], (DEPTH, D_MODEL), 0.02),
        'w_ffn_up': nrm(ks[10], (DEPTH, D_MODEL, 2 * D_FF), D_MODEL ** -0.5),
        'ffn_conv_w': nrm(ks[11], (DEPTH, FFN_CONV, 2 * D_FF), FFN_CONV ** -0.5),
        'ffn_conv_b': nrm(ks[12], (DEPTH, 2 * D_FF), 0.02),
        'w_ffn_down': nrm(ks[13], (DEPTH, D_FF, D_MODEL), D_FF ** -0.5),
        'norm_f': 1.0 + nrm(ks[14], (D_MODEL,), 0.02),
    }


def _fwd_reference(x, meta, norm1, w_in, gdn_conv_w, gdn_a_log, gdn_dt_bias, gdn_norm, w_out,
              norm2, w_ffn_up, ffn_conv_w, ffn_conv_b, w_ffn_down, norm_f):
    B = x.shape[0]
    h = jnp.concatenate(
        [jnp.broadcast_to(meta.astype(x.dtype)[None], (B, N_META, D_MODEL)), x], axis=1)
    pos = jnp.arange(h.shape[1], dtype=jnp.float32)
    for layer in range(DEPTH):
        h = h + mixer_block(rmsnorm(h, norm1[layer]), pos, w_in[layer], gdn_conv_w[layer],
                            gdn_a_log[layer], gdn_dt_bias[layer], gdn_norm[layer], w_out[layer])
        h = h + conv_ffn(rmsnorm(h, norm2[layer]), w_ffn_up[layer], ffn_conv_w[layer],
                         ffn_conv_b[layer], w_ffn_down[layer])
    h = rmsnorm(h, norm_f)
    return h[:, N_META:]


import jax as _jax
import jax.numpy as _jnp

TWIN_FORMAT = 'train_step'
FWD_PARAMS = ['x', 'meta', 'norm1', 'w_in', 'gdn_conv_w', 'gdn_a_log', 'gdn_dt_bias', 'gdn_norm', 'w_out', 'norm2', 'w_ffn_up', 'ffn_conv_w', 'ffn_conv_b', 'w_ffn_down', 'norm_f']
TWIN_WEIGHTS = ['meta', 'norm1', 'w_in', 'gdn_conv_w', 'gdn_a_log', 'gdn_dt_bias', 'gdn_norm', 'w_out', 'norm2', 'w_ffn_up', 'ffn_conv_w', 'ffn_conv_b', 'w_ffn_down', 'norm_f']
TWIN_DIFF_INPUT = 'x'
TWIN_INPUTS = ['x', 'meta', 'norm1', 'w_in', 'gdn_conv_w', 'gdn_a_log', 'gdn_dt_bias', 'gdn_norm', 'w_out', 'norm2', 'w_ffn_up', 'ffn_conv_w', 'ffn_conv_b', 'w_ffn_down', 'norm_f', 'loss_target', 'm_meta', 'm_norm1', 'm_w_in', 'm_gdn_conv_w', 'm_gdn_a_log', 'm_gdn_dt_bias', 'm_gdn_norm', 'm_w_out', 'm_norm2', 'm_w_ffn_up', 'm_ffn_conv_w', 'm_ffn_conv_b', 'm_w_ffn_down', 'm_norm_f', 'v_meta', 'v_norm1', 'v_w_in', 'v_gdn_conv_w', 'v_gdn_a_log', 'v_gdn_dt_bias', 'v_gdn_norm', 'v_w_out', 'v_norm2', 'v_w_ffn_up', 'v_ffn_conv_w', 'v_ffn_conv_b', 'v_w_ffn_down', 'v_norm_f']
TWIN_OUTPUTS = ['loss', 'grad_x', 'grad_meta', 'grad_norm1', 'grad_w_in', 'grad_gdn_conv_w', 'grad_gdn_a_log', 'grad_gdn_dt_bias', 'grad_gdn_norm', 'grad_w_out', 'grad_norm2', 'grad_w_ffn_up', 'grad_ffn_conv_w', 'grad_ffn_conv_b', 'grad_w_ffn_down', 'grad_norm_f', 'delta_meta', 'delta_norm1', 'delta_w_in', 'delta_gdn_conv_w', 'delta_gdn_a_log', 'delta_gdn_dt_bias', 'delta_gdn_norm', 'delta_w_out', 'delta_norm2', 'delta_w_ffn_up', 'delta_ffn_conv_w', 'delta_ffn_conv_b', 'delta_w_ffn_down', 'delta_norm_f', 'new_m_meta', 'new_m_norm1', 'new_m_w_in', 'new_m_gdn_conv_w', 'new_m_gdn_a_log', 'new_m_gdn_dt_bias', 'new_m_gdn_norm', 'new_m_w_out', 'new_m_norm2', 'new_m_w_ffn_up', 'new_m_ffn_conv_w', 'new_m_ffn_conv_b', 'new_m_w_ffn_down', 'new_m_norm_f', 'new_v_meta', 'new_v_norm1', 'new_v_w_in', 'new_v_gdn_conv_w', 'new_v_gdn_a_log', 'new_v_gdn_dt_bias', 'new_v_gdn_norm', 'new_v_w_out', 'new_v_norm2', 'new_v_w_ffn_up', 'new_v_ffn_conv_w', 'new_v_ffn_conv_b', 'new_v_w_ffn_down', 'new_v_norm_f']
TWIN_LEAF_KINDS = {'loss': 'loss', 'grad_x': 'grad_x', 'grad_meta': 'grad_w', 'grad_norm1': 'grad_w', 'grad_w_in': 'grad_w', 'grad_gdn_conv_w': 'grad_w', 'grad_gdn_a_log': 'grad_w', 'grad_gdn_dt_bias': 'grad_w', 'grad_gdn_norm': 'grad_w', 'grad_w_out': 'grad_w', 'grad_norm2': 'grad_w', 'grad_w_ffn_up': 'grad_w', 'grad_ffn_conv_w': 'grad_w', 'grad_ffn_conv_b': 'grad_w', 'grad_w_ffn_down': 'grad_w', 'grad_norm_f': 'grad_w', 'delta_meta': 'delta_w', 'delta_norm1': 'delta_w', 'delta_w_in': 'delta_w', 'delta_gdn_conv_w': 'delta_w', 'delta_gdn_a_log': 'delta_w', 'delta_gdn_dt_bias': 'delta_w', 'delta_gdn_norm': 'delta_w', 'delta_w_out': 'delta_w', 'delta_norm2': 'delta_w', 'delta_w_ffn_up': 'delta_w', 'delta_ffn_conv_w': 'delta_w', 'delta_ffn_conv_b': 'delta_w', 'delta_w_ffn_down': 'delta_w', 'delta_norm_f': 'delta_w', 'new_m_meta': 'new_m', 'new_m_norm1': 'new_m', 'new_m_w_in': 'new_m', 'new_m_gdn_conv_w': 'new_m', 'new_m_gdn_a_log': 'new_m', 'new_m_gdn_dt_bias': 'new_m', 'new_m_gdn_norm': 'new_m', 'new_m_w_out': 'new_m', 'new_m_norm2': 'new_m', 'new_m_w_ffn_up': 'new_m', 'new_m_ffn_conv_w': 'new_m', 'new_m_ffn_conv_b': 'new_m', 'new_m_w_ffn_down': 'new_m', 'new_m_norm_f': 'new_m', 'new_v_meta': 'new_v', 'new_v_norm1': 'new_v', 'new_v_w_in': 'new_v', 'new_v_gdn_conv_w': 'new_v', 'new_v_gdn_a_log': 'new_v', 'new_v_gdn_dt_bias': 'new_v', 'new_v_gdn_norm': 'new_v', 'new_v_w_out': 'new_v', 'new_v_norm2': 'new_v', 'new_v_w_ffn_up': 'new_v', 'new_v_ffn_conv_w': 'new_v', 'new_v_ffn_conv_b': 'new_v', 'new_v_w_ffn_down': 'new_v', 'new_v_norm_f': 'new_v'}


def _forward(args):
    return _fwd_reference(*[args[k] for k in FWD_PARAMS])


def _output_shape():
    def fwd():
        inp = _fwd_setup_inputs(0)
        return _fwd_reference(*[inp[k] for k in FWD_PARAMS])
    out = _jax.eval_shape(fwd)
    return out.shape, out.dtype

N_MICROBATCH = 1
ADAM_LR = 0.001
ADAM_B1 = 0.9
ADAM_B2 = 0.999
ADAM_EPS = 1e-08
ADAM_WD = 0.01
ADAM_STEP = 10
PER_EXAMPLE_BATCH_AXIS = {'x': 0, 'loss_target': 0}
SHARED_INPUTS = []
_WEIGHT_DTYPES = {'meta': _jnp.float32, 'norm1': _jnp.float32, 'w_in': _jnp.float32, 'gdn_conv_w': _jnp.float32, 'gdn_a_log': _jnp.float32, 'gdn_dt_bias': _jnp.float32, 'gdn_norm': _jnp.float32, 'w_out': _jnp.float32, 'norm2': _jnp.float32, 'w_ffn_up': _jnp.float32, 'ffn_conv_w': _jnp.float32, 'ffn_conv_b': _jnp.float32, 'w_ffn_down': _jnp.float32, 'norm_f': _jnp.float32}
MOMENT_SCALE = {'meta': 1.003528e-02, 'norm1': 2.334984e-01, 'w_in': 6.985341e-02, 'gdn_conv_w': 6.195292e-02, 'gdn_a_log': 3.312046e-01, 'gdn_dt_bias': 3.258522e-01, 'gdn_norm': 2.241694e-01, 'w_out': 1.154965e-01, 'norm2': 1.866556e-01, 'w_ffn_up': 7.662589e-02, 'ffn_conv_w': 7.690545e-02, 'ffn_conv_b': 7.618600e-02, 'w_ffn_down': 1.251218e-01, 'norm_f': 6.404334e+01}


def _to_microbatches(a, axis):
    t = _jnp.moveaxis(a, axis, 0)
    t = t.reshape((N_MICROBATCH, t.shape[0] // N_MICROBATCH) + t.shape[1:])
    return _jnp.moveaxis(t, 1, axis + 1)


def setup_inputs(seed: int = 0) -> dict:
    inp = _fwd_setup_inputs(seed)
    key = _jax.random.fold_in(_jax.random.key(seed), 7919)
    shape, _ = _output_shape()
    out = dict(inp)
    out["loss_target"] = _jax.random.normal(_jax.random.fold_in(key, 0), shape, _jnp.float32)
    for i, name in enumerate(TWIN_WEIGHTS):
        w = inp[name].astype(_jnp.float32)
        if MOMENT_SCALE is None:
            s = _jnp.sqrt(_jnp.mean(_jnp.square(w)) + 1e-30)
        else:
            s = MOMENT_SCALE[name]
        km, kv = _jax.random.split(_jax.random.fold_in(key, i + 1))
        out[name] = w
        out["m_" + name] = s * _jax.random.normal(km, w.shape, _jnp.float32)
        out["v_" + name] = (s * s) * _jax.random.uniform(kv, w.shape, _jnp.float32, 0.5, 1.5)
    if N_MICROBATCH > 1:
        for name, axis in PER_EXAMPLE_BATCH_AXIS.items():
            out[name] = _to_microbatches(out[name], axis)
    return {'x': out['x'], 'meta': out['meta'], 'norm1': out['norm1'], 'w_in': out['w_in'], 'gdn_conv_w': out['gdn_conv_w'], 'gdn_a_log': out['gdn_a_log'], 'gdn_dt_bias': out['gdn_dt_bias'], 'gdn_norm': out['gdn_norm'], 'w_out': out['w_out'], 'norm2': out['norm2'], 'w_ffn_up': out['w_ffn_up'], 'ffn_conv_w': out['ffn_conv_w'], 'ffn_conv_b': out['ffn_conv_b'], 'w_ffn_down': out['w_ffn_down'], 'norm_f': out['norm_f'], 'loss_target': out['loss_target'], 'm_meta': out['m_meta'], 'm_norm1': out['m_norm1'], 'm_w_in': out['m_w_in'], 'm_gdn_conv_w': out['m_gdn_conv_w'], 'm_gdn_a_log': out['m_gdn_a_log'], 'm_gdn_dt_bias': out['m_gdn_dt_bias'], 'm_gdn_norm': out['m_gdn_norm'], 'm_w_out': out['m_w_out'], 'm_norm2': out['m_norm2'], 'm_w_ffn_up': out['m_w_ffn_up'], 'm_ffn_conv_w': out['m_ffn_conv_w'], 'm_ffn_conv_b': out['m_ffn_conv_b'], 'm_w_ffn_down': out['m_w_ffn_down'], 'm_norm_f': out['m_norm_f'], 'v_meta': out['v_meta'], 'v_norm1': out['v_norm1'], 'v_w_in': out['v_w_in'], 'v_gdn_conv_w': out['v_gdn_conv_w'], 'v_gdn_a_log': out['v_gdn_a_log'], 'v_gdn_dt_bias': out['v_gdn_dt_bias'], 'v_gdn_norm': out['v_gdn_norm'], 'v_w_out': out['v_w_out'], 'v_norm2': out['v_norm2'], 'v_w_ffn_up': out['v_w_ffn_up'], 'v_ffn_conv_w': out['v_ffn_conv_w'], 'v_ffn_conv_b': out['v_ffn_conv_b'], 'v_w_ffn_down': out['v_w_ffn_down'], 'v_norm_f': out['v_norm_f']}


def _loss(weights, diff, rest, loss_target):
    with _jax.named_scope("forward"):
        args = {**rest, TWIN_DIFF_INPUT: diff, **{k: w.astype(_WEIGHT_DTYPES[k]) for k, w in weights.items()}}
        y = _forward(args)
    with _jax.named_scope("loss_head"):
        err = _jnp.square(y.astype(_jnp.float32) - loss_target)
        return 0.5 * _jnp.sum(_jnp.mean(err, axis=-1)) if err.ndim else 0.5 * err


def _adamw(w, g, m, v):
    m = ADAM_B1 * m + (1.0 - ADAM_B1) * g
    v = ADAM_B2 * v + (1.0 - ADAM_B2) * _jnp.square(g)
    m_hat = m / (1.0 - ADAM_B1 ** ADAM_STEP)
    v_hat = v / (1.0 - ADAM_B2 ** ADAM_STEP)
    delta = -ADAM_LR * (m_hat / (_jnp.sqrt(v_hat) + ADAM_EPS) + ADAM_WD * w)
    return delta, m, v


def reference(x, meta, norm1, w_in, gdn_conv_w, gdn_a_log, gdn_dt_bias, gdn_norm, w_out, norm2, w_ffn_up, ffn_conv_w, ffn_conv_b, w_ffn_down, norm_f, loss_target, m_meta, m_norm1, m_w_in, m_gdn_conv_w, m_gdn_a_log, m_gdn_dt_bias, m_gdn_norm, m_w_out, m_norm2, m_w_ffn_up, m_ffn_conv_w, m_ffn_conv_b, m_w_ffn_down, m_norm_f, v_meta, v_norm1, v_w_in, v_gdn_conv_w, v_gdn_a_log, v_gdn_dt_bias, v_gdn_norm, v_w_out, v_norm2, v_w_ffn_up, v_ffn_conv_w, v_ffn_conv_b, v_w_ffn_down, v_norm_f):
    given = dict(x=x, meta=meta, norm1=norm1, w_in=w_in, gdn_conv_w=gdn_conv_w, gdn_a_log=gdn_a_log, gdn_dt_bias=gdn_dt_bias, gdn_norm=gdn_norm, w_out=w_out, norm2=norm2, w_ffn_up=w_ffn_up, ffn_conv_w=ffn_conv_w, ffn_conv_b=ffn_conv_b, w_ffn_down=w_ffn_down, norm_f=norm_f, loss_target=loss_target, m_meta=m_meta, m_norm1=m_norm1, m_w_in=m_w_in, m_gdn_conv_w=m_gdn_conv_w, m_gdn_a_log=m_gdn_a_log, m_gdn_dt_bias=m_gdn_dt_bias, m_gdn_norm=m_gdn_norm, m_w_out=m_w_out, m_norm2=m_norm2, m_w_ffn_up=m_w_ffn_up, m_ffn_conv_w=m_ffn_conv_w, m_ffn_conv_b=m_ffn_conv_b, m_w_ffn_down=m_w_ffn_down, m_norm_f=m_norm_f, v_meta=v_meta, v_norm1=v_norm1, v_w_in=v_w_in, v_gdn_conv_w=v_gdn_conv_w, v_gdn_a_log=v_gdn_a_log, v_gdn_dt_bias=v_gdn_dt_bias, v_gdn_norm=v_gdn_norm, v_w_out=v_w_out, v_norm2=v_norm2, v_w_ffn_up=v_w_ffn_up, v_ffn_conv_w=v_ffn_conv_w, v_ffn_conv_b=v_ffn_conv_b, v_w_ffn_down=v_w_ffn_down, v_norm_f=v_norm_f)
    weights = {n: given[n] for n in TWIN_WEIGHTS}
    shared = {n: given[n] for n in SHARED_INPUTS}
    per_example = {n: given[n] for n in ['x']}
    grad_fn = _jax.value_and_grad(_loss, argnums=(0, 1))

    def one_microbatch(ex, loss_target):
        ex = dict(ex)
        diff = ex.pop(TWIN_DIFF_INPUT)
        return grad_fn(weights, diff, {**shared, **ex}, loss_target)

    if N_MICROBATCH == 1:
        loss, (grad_w, grad_x) = one_microbatch(per_example, given["loss_target"])
    else:
        def body(carry, xs):
            loss_sum, grad_sum = carry
            l_k, (gw_k, gx_k) = one_microbatch(xs[0], xs[1])
            with _jax.named_scope("update"):
                return (loss_sum + l_k, _jax.tree.map(_jnp.add, grad_sum, gw_k)), gx_k

        init = (_jnp.zeros((), _jnp.float32), _jax.tree.map(_jnp.zeros_like, weights))
        (loss, grad_w), grad_x = _jax.lax.scan(body, init, (per_example, given["loss_target"]))
    with _jax.named_scope("update"):
        delta_w, new_m, new_v = {}, {}, {}
        for n in TWIN_WEIGHTS:
            delta_w[n], new_m[n], new_v[n] = _adamw(weights[n], grad_w[n], given["m_" + n], given["v_" + n])
    return (loss, grad_x, *[grad_w[n] for n in TWIN_WEIGHTS], *[delta_w[n] for n in TWIN_WEIGHTS],
            *[new_m[n] for n in TWIN_WEIGHTS], *[new_v[n] for n in TWIN_WEIGHTS])
```

```python
import functools
from typing import NamedTuple

import numpy as np
import jax
import jax.numpy as jnp
from jax import lax
from jax.experimental import pallas as pl
from jax.experimental.pallas import tpu as pltpu

F32 = jnp.float32
BF16 = jnp.bfloat16
EPS = 1e-6
CHUNK = 64
GDN_DK = 128
RET_DK = 256
GDN_CONV = 4
FFN_CONV = 3
ROPE_BASE = 10000.0
LANES = 128
N_CHIPS = 4
ADAM_LR, ADAM_B1, ADAM_B2, ADAM_EPS, ADAM_WD, ADAM_STEP = 0.001, 0.9, 0.999, 1e-08, 0.01, 10
MESH = pl.DeviceIdType.MESH
VMEM_LIMIT = 56 * 1024 * 1024


class Cfg(NamedTuple):
    d: int
    seq: int
    n_meta: int
    dff: int
    tr: int
    nb: int
    tm: int
    tf: int

    @property
    def hg(self): return self.d // GDN_DK
    @property
    def hr(self): return self.d // RET_DK
    @property
    def L(self): return self.n_meta + self.seq
    @property
    def rp(self): return -(-self.L // 256) * 256
    @property
    def front(self): return self.rp - self.L
    @property
    def xrow(self): return self.rp - self.seq
    @property
    def nch(self): return self.rp // CHUNK
    @property
    def pw(self): return 10 * self.d + LANES
    @property
    def din(self): return 10 * self.d + 2 * self.hg


REAL = Cfg(d=1024, seq=8192, n_meta=16, dff=2816, tr=256, nb=12, tm=768, tf=1408)


def _pallas(body, **kw):
    return pl.pallas_call(body, **kw)


def _sigmoid(x):
    return 1.0 / (1.0 + jnp.exp(-x))


def _silu(x):
    return x * _sigmoid(x)


def _softplus(x):
    return jnp.maximum(x, 0.0) + jnp.log(1.0 + jnp.exp(-jnp.abs(x)))


def _raw_dot(a, b, ta, tb, hi):
    if not hi:
        a = a.astype(BF16)
        b = b.astype(BF16)
    nbatch = a.ndim - 2
    ca = a.ndim - 2 if ta else a.ndim - 1
    cb = b.ndim - 1 if tb else b.ndim - 2
    batch = tuple(range(nbatch))
    return lax.dot_general(a, b, (((ca,), (cb,)), (batch, batch)),
                           precision=lax.Precision.HIGHEST if hi else None,
                           preferred_element_type=F32)


@functools.partial(jax.custom_vjp, nondiff_argnums=(2, 3, 4))
def _dot_p(a, b, ta, tb, hi):
    return _raw_dot(a, b, ta, tb, hi)


def _dot(a, b, ta=False, tb=False, hi=False):
    return _dot_p(a, b, ta, tb, hi)


def _dot_fwd(a, b, ta, tb, hi):
    return _raw_dot(a, b, ta, tb, hi), (a, b)


def _dot_bwd(ta, tb, hi, res, g):
    a, b = res
    if not ta and not tb:
        da, db = _dot(g, b, False, True, hi), _dot(a, g, True, False, hi)
    elif not ta and tb:
        da, db = _dot(g, b, False, False, hi), _dot(g, a, True, False, hi)
    elif ta and not tb:
        da, db = _dot(b, g, False, True, hi), _dot(a, g, False, False, hi)
    else:
        raise NotImplementedError
    return da.astype(a.dtype), db.astype(b.dtype)


_dot_p.defvjp(_dot_fwd, _dot_bwd)


def _iota2(n, m, axis):
    return lax.broadcasted_iota(jnp.int32, (n, m), axis)


def _bcast(mat, nb):
    return jnp.broadcast_to(mat[None], (nb,) + mat.shape)


def _tri_inv_raw(m):
    nb = m.shape[0]
    r, c = _iota2(CHUNK, CHUNK, 0), _iota2(CHUNK, CHUNK, 1)
    t = _bcast((r == c).astype(F32), nb)
    b = 1
    while b < CHUNK:
        sh = b.bit_length() - 1
        off = ((r >> (sh + 1)) == (c >> (sh + 1))) & ((r >> sh) != (c >> sh)) & (r > c)
        cl = jnp.where(off[None], m, 0.0)
        t = t - _raw_dot(_raw_dot(t, cl, False, False, True), t, False, False, True)
        b *= 2
    return t


@jax.custom_vjp
def _tri_inv(m):
    return _tri_inv_raw(m)


def _tri_inv_fwd(m):
    t = _tri_inv_raw(m)
    return t, t


def _tri_inv_bwd(t, g):
    x = _raw_dot(t, g, True, False, True)
    return (-_raw_dot(x, t, False, True, True),)


_tri_inv.defvjp(_tri_inv_fwd, _tri_inv_bwd)


def _rms(h, g):
    return h * lax.rsqrt(jnp.mean(h * h, axis=-1, keepdims=True) + EPS) * g


class In(NamedTuple):
    arr: jax.Array
    spec: pl.BlockSpec
    grad: object = None
    acc: bool = False
    gshape: object = None
    gspec: object = None


def _params(grid):
    sem = ("arbitrary",) * len(grid)
    return pltpu.CompilerParams(dimension_semantics=sem, vmem_limit_bytes=VMEM_LIMIT)


def _stage_fwd(name, fn, grid, ins, out_shapes, out_specs):
    n_in = len(ins)

    def body(*refs):
        pids = tuple(pl.program_id(k) for k in range(len(grid)))
        vals = [r[...] for r in refs[:n_in]]
        outs = fn(pids, *vals)
        for o_ref, o in zip(refs[n_in:], outs):
            o_ref[...] = o.reshape(o_ref.shape).astype(o_ref.dtype)

    return _pallas(
        body, out_shape=out_shapes, grid=grid, in_specs=[i.spec for i in ins],
        out_specs=out_specs, name=name, compiler_params=_params(grid))(*[i.arr for i in ins])


def _stage_bwd(name, fn, grid, ins, cots):
    n_in, n_ct = len(ins), len(cots)
    didx = [k for k, i in enumerate(ins) if i.grad is not None]

    def body(*refs):
        pids = tuple(pl.program_id(k) for k in range(len(grid)))
        vals = [r[...] for r in refs[:n_in]]
        ct_refs = refs[n_in:n_in + n_ct]
        g_refs = refs[n_in + n_ct:]

        def f(*dv):
            merged = list(vals)
            for k, v in zip(didx, dv):
                merged[k] = v
            return tuple(fn(pids, *merged))

        outs, vjp_fn = jax.vjp(f, *[vals[k].astype(F32) for k in didx])
        cts = tuple(c[...].reshape(o.shape).astype(F32) for c, o in zip(ct_refs, outs))
        grads = vjp_fn(cts)
        first = functools.reduce(jnp.logical_and, [p == 0 for p in pids])
        for k, g_ref, g in zip(didx, g_refs, grads):
            if ins[k].acc:
                @pl.when(first)
                def _(g_ref=g_ref):
                    g_ref[...] = jnp.zeros(g_ref.shape, g_ref.dtype)
                g_ref[...] += g.reshape(g_ref.shape).astype(g_ref.dtype)
            else:
                g_ref[...] = g.reshape(g_ref.shape).astype(g_ref.dtype)

    out_shapes = [jax.ShapeDtypeStruct(ins[k].gshape or ins[k].arr.shape, ins[k].grad) for k in didx]
    out_specs = [ins[k].gspec or ins[k].spec for k in didx]
    return _pallas(
        body, out_shape=out_shapes, grid=grid,
        in_specs=[i.spec for i in ins] + [c[1] for c in cots], out_specs=out_specs,
        name=name, compiler_params=_params(grid))(*[i.arr for i in ins], *[c[0] for c in cots])


def _full(arr):
    nd = arr.ndim
    return pl.BlockSpec(arr.shape, lambda *p: (0,) * nd)


def _rows(tr, width, blk=0):
    return pl.BlockSpec((tr, width), lambda i: (i, blk))


def _mm(name, a, b, *, tm, tn, tk, out_dtype=F32, add=None):
    M, K = a.shape
    N = b.shape[1]
    nk = K // tk
    grid = (M // tm, N // tn, nk)

    def body(*refs):
        a_ref, b_ref = refs[0], refs[1]
        add_ref = refs[2] if add is not None else None
        o_ref = refs[3 if add is not None else 2]
        acc_ref = refs[-1] if nk > 1 else None
        part = _raw_dot(a_ref[...], b_ref[...], False, False, False)

        def finish(total):
            if add_ref is not None:
                total = total + add_ref[...]
            o_ref[...] = total.astype(o_ref.dtype)

        if nk == 1:
            finish(part)
        else:
            k = pl.program_id(2)

            @pl.when(k == 0)
            def _():
                acc_ref[...] = part

            @pl.when(k > 0)
            def _():
                acc_ref[...] += part

            @pl.when(k == nk - 1)
            def _():
                finish(acc_ref[...])

    in_specs = [pl.BlockSpec((tm, tk), lambda i, j, k: (i, k)),
                pl.BlockSpec((tk, tn), lambda i, j, k: (k, j))]
    args = [a, b]
    if add is not None:
        in_specs.append(pl.BlockSpec((tm, tn), lambda i, j, k: (i, j)))
        args.append(add)
    return _pallas(
        body, out_shape=jax.ShapeDtypeStruct((M, N), out_dtype), grid=grid, in_specs=in_specs,
        out_specs=pl.BlockSpec((tm, tn), lambda i, j, k: (i, j)),
        scratch_shapes=[pltpu.VMEM((tm, tn), F32)] if nk > 1 else [],
        name=name, compiler_params=_params(grid))(*args)


def _mm_tn(name, a, b, *, tr, tka, tn):
    R, Ka = a.shape
    N = b.shape[1]
    nr = R // tr
    grid = (Ka // tka, N // tn, nr)

    def body(a_ref, b_ref, o_ref):
        r = pl.program_id(2)
        part = _raw_dot(a_ref[...], b_ref[...], True, False, False)

        @pl.when(r == 0)
        def _():
            o_ref[...] = part

        @pl.when(r > 0)
        def _():
            o_ref[...] += part

    return _pallas(
        body, out_shape=jax.ShapeDtypeStruct((Ka, N), F32), grid=grid,
        in_specs=[pl.BlockSpec((tr, tka), lambda i, j, r: (r, i)),
                  pl.BlockSpec((tr, tn), lambda i, j, r: (r, j))],
        out_specs=pl.BlockSpec((tka, tn), lambda i, j, r: (i, j)),
        name=name, compiler_params=_params(grid))(a, b)


def _conv_fwd(name, x, xcol0, w, b, *, taps, width, tr, tc):
    R = x.shape[0]
    grid = (width // tc, R // tr)
    cb0 = xcol0 // tc
    hb = tr // 8

    def body(*refs):
        x_ref, xp_ref, w_ref = refs[:3]
        b_ref = refs[3] if b is not None else None
        o_ref = refs[-1]
        i = pl.program_id(1)
        xv = x_ref[...]
        prev = jnp.where(i > 0, xp_ref[...], 0.0)
        ext = jnp.concatenate([prev, xv], axis=0)
        acc = xv * w_ref[taps - 1:taps, :]
        for s in range(1, taps):
            acc = acc + pltpu.roll(ext, s, 0)[8:, :] * w_ref[taps - 1 - s:taps - s, :]
        if b_ref is not None:
            acc = acc + b_ref[...]
        o_ref[...] = acc

    in_specs = [pl.BlockSpec((tr, tc), lambda j, i: (i, cb0 + j)),
                pl.BlockSpec((8, tc), lambda j, i: (jnp.maximum(i * hb - 1, 0), cb0 + j)),
                pl.BlockSpec((taps, tc), lambda j, i: (0, j))]
    args = [x, x, w]
    if b is not None:
        in_specs.append(pl.BlockSpec((1, tc), lambda j, i: (0, j)))
        args.append(b)
    return _pallas(
        body, out_shape=jax.ShapeDtypeStruct((R, width), F32), grid=grid, in_specs=in_specs,
        out_specs=pl.BlockSpec((tr, tc), lambda j, i: (i, j)),
        name=name, compiler_params=_params(grid))(*args)


def _conv_bwd(name, x, xcol0, w, dy, *, taps, width, tr, tc, with_bias):
    R = x.shape[0]
    nr = R // tr
    grid = (width // tc, nr)
    cb0 = xcol0 // tc
    hb = tr // 8
    n_ext = tr + 8

    def body(*refs):
        x_ref, xp_ref, w_ref, dy_ref, dyn_ref = refs[:5]
        dx_ref, dw_ref = refs[5], refs[6]
        db_ref = refs[7] if with_bias else None
        i = pl.program_id(1)
        xv = x_ref[...]
        ext = jnp.concatenate([jnp.where(i > 0, xp_ref[...], 0.0), xv], axis=0)
        dyv = dy_ref[...]
        dext = jnp.concatenate([dyv, jnp.where(i < nr - 1, dyn_ref[...], 0.0)], axis=0)
        dx = dyv * w_ref[taps - 1:taps, :]
        dws = [None] * taps
        dws[taps - 1] = jnp.sum(xv * dyv, axis=0, keepdims=True)
        for s in range(1, taps):
            dx = dx + pltpu.roll(dext, n_ext - s, 0)[:tr, :] * w_ref[taps - 1 - s:taps - s, :]
            dws[taps - 1 - s] = jnp.sum(pltpu.roll(ext, s, 0)[8:, :] * dyv, axis=0, keepdims=True)
        dx_ref[...] = dx.astype(dx_ref.dtype)

        @pl.when(i == 0)
        def _():
            for k in range(taps):
                dw_ref[k:k + 1, :] = dws[k]
            if db_ref is not None:
                db_ref[...] = jnp.sum(dyv, axis=0, keepdims=True)

        @pl.when(i > 0)
        def _():
            for k in range(taps):
                dw_ref[k:k + 1, :] += dws[k]
            if db_ref is not None:
                db_ref[...] += jnp.sum(dyv, axis=0, keepdims=True)

    in_specs = [pl.BlockSpec((tr, tc), lambda j, i: (i, cb0 + j)),
                pl.BlockSpec((8, tc), lambda j, i: (jnp.maximum(i * hb - 1, 0), cb0 + j)),
                pl.BlockSpec((taps, tc), lambda j, i: (0, j)),
                pl.BlockSpec((tr, tc), lambda j, i: (i, j)),
                pl.BlockSpec((8, tc), lambda j, i: (jnp.minimum((i + 1) * hb, R // 8 - 1), j))]
    out_shape = [jax.ShapeDtypeStruct((R, width), BF16), jax.ShapeDtypeStruct((taps, width), F32)]
    out_specs = [pl.BlockSpec((tr, tc), lambda j, i: (i, j)), pl.BlockSpec((taps, tc), lambda j, i: (0, j))]
    if with_bias:
        out_shape.append(jax.ShapeDtypeStruct((1, width), F32))
        out_specs.append(pl.BlockSpec((1, tc), lambda j, i: (0, j)))
    return _pallas(
        body, out_shape=out_shape, grid=grid, in_specs=in_specs, out_specs=out_specs,
        name=name, compiler_params=_params(grid))(x, x, w, dy, dy)


def _row_mask(cfg, i, tr):
    rows = i * tr + lax.broadcasted_iota(jnp.int32, (tr, 1), 0)
    return (rows >= cfg.front).astype(F32)


def _make_rms_fn(cfg, tr, with_residual):
    def fn(pids, h, g):
        hm = h * _row_mask(cfg, pids[0], tr)
        if with_residual:
            return _rms(hm, g), hm
        return (_rms(hm, g),)
    return fn


def _make_gdn_prep_fn(cfg, tr):
    d, hg = cfg.d, cfg.hg

    def fn(pids, c, tail, alog, dtb):
        cq, ck, cv = c[:, :d], c[:, d:2 * d], c[:, 2 * d:]
        mask = _row_mask(cfg, pids[0], tr)
        j, col = _iota2(LANES, d, 0), _iota2(LANES, d, 1)
        ea = ((col >> 7) == j).astype(F32)
        eb = ((col >> 7) + hg == j).astype(F32)
        ga = _dot(tail, ea, False, False, True)
        gb = _dot(tail, eb, False, False, True)
        al = jnp.sum(_dot(alog, ea, False, False, True), axis=0, keepdims=True)
        db = jnp.sum(_dot(dtb, ea, False, False, True), axis=0, keepdims=True)
        lg = -jnp.exp(al) * _softplus(ga + db) * mask
        beta = _sigmoid(gb) * mask
        sq, sk, sv = _silu(cq), _silu(ck), _silu(cv)
        qs, ks = [], []
        for h in range(hg):
            sl = slice(h * GDN_DK, (h + 1) * GDN_DK)
            qh, kh = sq[:, sl], sk[:, sl]
            qs.append(qh * lax.rsqrt(jnp.sum(qh * qh, axis=-1, keepdims=True) + EPS) * (GDN_DK ** -0.5))
            ks.append(kh * lax.rsqrt(jnp.sum(kh * kh, axis=-1, keepdims=True) + EPS))
        return jnp.concatenate(qs, axis=1), jnp.concatenate(ks, axis=1), sv, beta, lg
    return fn


def _gdn_intra_fn(pids, q, k, v, bB, lB):
    rows = q.shape[0]
    nb = rows // CHUNK
    q3, k3, v3, b3, l3 = [t.reshape(nb, CHUNK, GDN_DK) for t in (q, k, v, bB, lB)]
    r, c = _iota2(CHUNK, CHUNK, 0), _iota2(CHUNK, CHUNK, 1)
    tril = (r >= c)
    strict = (r > c)
    gcol = _dot(_bcast(tril.astype(F32), nb), l3, False, False, True)
    l64 = l3[:, :, :CHUNK]
    grow = _dot(jnp.ones((nb, CHUNK, CHUNK), F32), l64 * (r <= c).astype(F32)[None], False, False, True)
    diff = gcol[:, :, :CHUNK] - grow
    decay = jnp.where(tril[None], jnp.exp(jnp.where(tril[None], diff, 0.0)), 0.0)
    kb = k3 * b3
    m = jnp.where(strict[None], _dot(kb, k3, False, True) * decay, 0.0)
    t = _tri_inv(m)
    eg = jnp.exp(gcol)
    u = _dot(t, v3 * b3)
    w = _dot(t, kb * eg)
    attn = _dot(q3, k3, False, True) * decay
    qd = q3 * eg
    glast = jnp.sum(l3, axis=1, keepdims=True)
    kd = k3 * jnp.exp(glast - gcol)
    gl = jnp.exp(glast)
    return (u.reshape(rows, GDN_DK), w.reshape(rows, GDN_DK), attn.reshape(1, rows, CHUNK),
            qd.reshape(rows, GDN_DK), kd.reshape(rows, GDN_DK), gl.reshape(1, nb, 1, GDN_DK))


def _make_rot_fn(cfg):
    hr = cfg.hr
    half = RET_DK // 2

    def fn(pids, rq, rk, cos, sin):
        def rot(t, scale):
            outs = []
            for h in range(hr):
                x1 = t[:, h * RET_DK:h * RET_DK + half]
                x2 = t[:, h * RET_DK + half:(h + 1) * RET_DK]
                outs += [(x1 * cos - x2 * sin) * scale, (x2 * cos + x1 * sin) * scale]
            return jnp.concatenate(outs, axis=1)
        return rot(rq, 1.0), rot(rk, RET_DK ** -0.5)
    return fn


def _make_mix_fn(cfg):
    hg, hr = cfg.hg, cfg.hr

    def fn(pids, oa, gz, ob, rg, gate_a, gate_b, gnorm):
        oas = []
        for h in range(hg):
            oh = oa[:, h * GDN_DK:(h + 1) * GDN_DK]
            oas.append(oh * lax.rsqrt(jnp.mean(oh * oh, axis=-1, keepdims=True) + EPS) * gnorm)
        ya = jnp.concatenate(oas, axis=1) * _silu(gz)
        obs = []
        for h in range(hr):
            oh = ob[:, h * RET_DK:(h + 1) * RET_DK]
            obs.append(oh * lax.rsqrt(jnp.mean(oh * oh, axis=-1, keepdims=True) + EPS))
        yb = _silu(rg) * jnp.concatenate(obs, axis=1)
        return (_sigmoid(gate_a) * ya + _sigmoid(gate_b) * yb,)
    return fn


def _act_fn(pids, ua, ub):
    return (_silu(ua) * ub,)


def _gdn_step(s, u, w, a, qd, kd, gl):
    v_new = u - _dot(w, s)
    o = _dot(qd, s) + _dot(a, v_new)
    s2 = s * gl + _dot(kd, v_new, True, False)
    return s2, o


def _ret_step(s, q, k, v, dm, qdc, kdc, g):
    inner = _dot(_dot(q, k, False, True) * dm, v)
    o = inner + _dot(q * qdc, s)
    s2 = s * g + _dot(k * kdc, v, True, False)
    return s2, o


def _gdn_scan_fwd(cfg, u, w, attn, qd, kd, gl):
    d, hg, nch = cfg.d, cfg.hg, cfg.nch

    def body(u_ref, w_ref, a_ref, qd_ref, kd_ref, gl_ref, o_ref, ss_ref, s_ref):
        n = pl.program_id(0)

        @pl.when(n == 0)
        def _():
            s_ref[...] = jnp.zeros(s_ref.shape, F32)

        for h in range(hg):
            sl = slice(h * GDN_DK, (h + 1) * GDN_DK)
            s = s_ref[h]
            ss_ref[0, h] = s
            s2, o = _gdn_step(s, u_ref[:, sl], w_ref[:, sl], a_ref[h], qd_ref[:, sl], kd_ref[:, sl], gl_ref[h, 0])
            s_ref[h] = s2
            o_ref[:, sl] = o

    row = pl.BlockSpec((CHUNK, d), lambda n: (n, 0))
    return _pallas(
        body,
        out_shape=[jax.ShapeDtypeStruct((cfg.rp, d), F32), jax.ShapeDtypeStruct((nch, hg, GDN_DK, GDN_DK), F32)],
        grid=(nch,),
        in_specs=[row, row, pl.BlockSpec((hg, CHUNK, CHUNK), lambda n: (0, n, 0)), row, row,
                  pl.BlockSpec((hg, 1, 1, GDN_DK), lambda n: (0, n, 0, 0))],
        out_specs=[row, pl.BlockSpec((1, hg, GDN_DK, GDN_DK), lambda n: (n, 0, 0, 0))],
        scratch_shapes=[pltpu.VMEM((hg, GDN_DK, GDN_DK), F32)],
        name="gdn_scan_fwd", compiler_params=_params((nch,)))(u, w, attn, qd, kd, gl)


def _gdn_scan_bwd(cfg, do, u, w, attn, qd, kd, gl, ss):
    d, hg, nch = cfg.d, cfg.hg, cfg.nch

    def body(do_ref, u_ref, w_ref, a_ref, qd_ref, kd_ref, gl_ref, ss_ref,
             du_ref, dw_ref, da_ref, dqd_ref, dkd_ref, dgl_ref, ds_ref):
        n = pl.program_id(0)

        @pl.when(n == 0)
        def _():
            ds_ref[...] = jnp.zeros(ds_ref.shape, F32)

        for h in range(hg):
            sl = slice(h * GDN_DK, (h + 1) * GDN_DK)
            args = (ss_ref[0, h], u_ref[:, sl], w_ref[:, sl], a_ref[h], qd_ref[:, sl], kd_ref[:, sl], gl_ref[h, 0])
            _, vjp_fn = jax.vjp(_gdn_step, *args)
            ds, du, dw, da, dqd, dkd, dgl = vjp_fn((ds_ref[h], do_ref[:, sl]))
            ds_ref[h] = ds
            du_ref[:, sl] = du
            dw_ref[:, sl] = dw
            da_ref[h] = da
            dqd_ref[:, sl] = dqd
            dkd_ref[:, sl] = dkd
            dgl_ref[h, 0] = dgl

    row = pl.BlockSpec((CHUNK, d), lambda n: (nch - 1 - n, 0))
    aspec = pl.BlockSpec((hg, CHUNK, CHUNK), lambda n: (0, nch - 1 - n, 0))
    gspec = pl.BlockSpec((hg, 1, 1, GDN_DK), lambda n: (0, nch - 1 - n, 0, 0))
    rowshape = jax.ShapeDtypeStruct((cfg.rp, d), F32)
    return _pallas(
        body,
        out_shape=[rowshape, rowshape, jax.ShapeDtypeStruct(attn.shape, F32), rowshape, rowshape,
                   jax.ShapeDtypeStruct(gl.shape, F32)],
        grid=(nch,),
        in_specs=[row, row, row, aspec, row, row, gspec,
                  pl.BlockSpec((1, hg, GDN_DK, GDN_DK), lambda n: (nch - 1 - n, 0, 0, 0))],
        out_specs=[row, row, aspec, row, row, gspec],
        scratch_shapes=[pltpu.VMEM((hg, GDN_DK, GDN_DK), F32)],
        name="gdn_scan_bwd", compiler_params=_params((nch,)))(do, u, w, attn, qd, kd, gl, ss)


def _ret_consts(cfg):
    hr = cfg.hr
    lg = np.log(1.0 - 2.0 ** (-5.0 - np.arange(hr, dtype=np.float64)))
    idx = np.arange(CHUNK, dtype=np.float64)
    tril = np.tril(np.ones((CHUNK, CHUNK), dtype=bool))
    dm = np.where(tril[None], np.exp((idx[:, None] - idx[None, :])[None] * lg[:, None, None]), 0.0)
    qdc = np.exp((idx[None, :] + 1.0) * lg[:, None])
    kdc = np.exp((CHUNK - 1.0 - idx[None, :]) * lg[:, None])
    gch = np.exp(CHUNK * lg)
    qdc = np.broadcast_to(qdc[:, :, None], (hr, CHUNK, RET_DK))
    kdc = np.broadcast_to(kdc[:, :, None], (hr, CHUNK, RET_DK))
    gch = np.broadcast_to(gch[:, None, None], (hr, 1, RET_DK))
    return tuple(jnp.asarray(np.ascontiguousarray(t), F32) for t in (dm, qdc, kdc, gch))


def _ret_scan_fwd(cfg, qr, kr, proj, consts):
    d, hr, nch = cfg.d, cfg.hr, cfg.nch
    dm, qdc, kdc, gch = consts

    def body(q_ref, k_ref, v_ref, dm_ref, qdc_ref, kdc_ref, g_ref, o_ref, ss_ref, s_ref):
        n = pl.program_id(0)

        @pl.when(n == 0)
        def _():
            s_ref[...] = jnp.zeros(s_ref.shape, F32)

        for h in range(hr):
            sl = slice(h * RET_DK, (h + 1) * RET_DK)
            s = s_ref[h]
            ss_ref[0, h] = s
            s2, o = _ret_step(s, q_ref[:, sl], k_ref[:, sl], v_ref[:, sl], dm_ref[h], qdc_ref[h], kdc_ref[h], g_ref[h])
            s_ref[h] = s2
            o_ref[:, sl] = o

    row = pl.BlockSpec((CHUNK, d), lambda n: (n, 0))
    return _pallas(
        body,
        out_shape=[jax.ShapeDtypeStruct((cfg.rp, d), F32), jax.ShapeDtypeStruct((nch, hr, RET_DK, RET_DK), F32)],
        grid=(nch,),
        in_specs=[row, row, pl.BlockSpec((CHUNK, d), lambda n: (n, 6)), _full(dm), _full(qdc), _full(kdc), _full(gch)],
        out_specs=[row, pl.BlockSpec((1, hr, RET_DK, RET_DK), lambda n: (n, 0, 0, 0))],
        scratch_shapes=[pltpu.VMEM((hr, RET_DK, RET_DK), F32)],
        name="ret_scan_fwd", compiler_params=_params((nch,)))(qr, kr, proj, dm, qdc, kdc, gch)


def _ret_scan_bwd(cfg, do, qr, kr, proj, consts, ss):
    d, hr, nch = cfg.d, cfg.hr, cfg.nch
    dm, qdc, kdc, gch = consts

    def body(do_ref, q_ref, k_ref, v_ref, dm_ref, qdc_ref, kdc_ref, g_ref, ss_ref, dq_ref, dk_ref, dv_ref, ds_ref):
        n = pl.program_id(0)

        @pl.when(n == 0)
        def _():
            ds_ref[...] = jnp.zeros(ds_ref.shape, F32)

        for h in range(hr):
            sl = slice(h * RET_DK, (h + 1) * RET_DK)
            cs = (dm_ref[h], qdc_ref[h], kdc_ref[h], g_ref[h])
            _, vjp_fn = jax.vjp(lambda s, q, k, v: _ret_step(s, q, k, v, *cs),
                                ss_ref[0, h], q_ref[:, sl], k_ref[:, sl], v_ref[:, sl])
            ds, dq, dk, dv = vjp_fn((ds_ref[h], do_ref[:, sl]))
            ds_ref[h] = ds
            dq_ref[:, sl] = dq
            dk_ref[:, sl] = dk
            dv_ref[:, sl] = dv.astype(dv_ref.dtype)

    row = pl.BlockSpec((CHUNK, d), lambda n: (nch - 1 - n, 0))
    rowshape = jax.ShapeDtypeStruct((cfg.rp, d), F32)
    return _pallas(
        body,
        out_shape=[rowshape, rowshape, jax.ShapeDtypeStruct((cfg.rp, d), BF16)],
        grid=(nch,),
        in_specs=[row, row, row, pl.BlockSpec((CHUNK, d), lambda n: (nch - 1 - n, 6)),
                  _full(dm), _full(qdc), _full(kdc), _full(gch),
                  pl.BlockSpec((1, hr, RET_DK, RET_DK), lambda n: (nch - 1 - n, 0, 0, 0))],
        out_specs=[row, row, row],
        scratch_shapes=[pltpu.VMEM((hr, RET_DK, RET_DK), F32)],
        name="ret_scan_bwd", compiler_params=_params((nch,)))(do, qr, kr, proj, dm, qdc, kdc, gch, ss)


def _final(cfg, h2, normf, tgt):
    d, tr = cfg.d, cfg.xrow
    nr = cfg.rp // tr

    def body(h_ref, g_ref, t_ref, dh_ref, dg_ref, loss_ref):
        i = pl.program_id(0)
        y, vjp_fn = jax.vjp(_rms, h_ref[...], g_ref[...])
        err = jnp.where(i >= 1, y - t_ref[...], 0.0)
        dh, dg = vjp_fn(err * (1.0 / d))
        dh_ref[...] = dh
        part = jnp.zeros((8, LANES), F32) + 0.5 * jnp.sum(err * err) * (1.0 / d)

        @pl.when(i == 0)
        def _():
            dg_ref[...] = dg
            loss_ref[...] = part

        @pl.when(i > 0)
        def _():
            dg_ref[...] += dg
            loss_ref[...] += part

    return _pallas(
        body,
        out_shape=[jax.ShapeDtypeStruct((cfg.rp, d), F32), jax.ShapeDtypeStruct((1, d), F32),
                   jax.ShapeDtypeStruct((8, LANES), F32)],
        grid=(nr,),
        in_specs=[_rows(tr, d), _full(normf), pl.BlockSpec((tr, d), lambda i: (jnp.maximum(i - 1, 0), 0))],
        out_specs=[_rows(tr, d), pl.BlockSpec((1, d), lambda i: (0, 0)), pl.BlockSpec((8, LANES), lambda i: (0, 0))],
        name="final_loss", compiler_params=_params((nr,)))(h2, normf, tgt)


ANY = pl.BlockSpec(memory_space=pl.ANY)


def _place():
    x, y, c = lax.axis_index("x"), lax.axis_index("y"), lax.axis_index("c")
    others = [(1 - x, y), (x, 1 - y), (1 - x, 1 - y)]
    return x, y, c, others


def _all_gather_chips(wp):
    R = wp.shape[0]
    half = R // 2

    def body(w_ref, o_ref, send_sems, recv_sems, local_sem):
        x, y, c, others = _place()
        me = 2 * x + y
        sibling = (x, y, 1 - c)
        mine = pltpu.make_async_copy(w_ref, o_ref.at[me], local_sem)
        mine.start()

        def piece(chip, core):
            return o_ref.at[chip, pl.ds(core * half, half), :]

        def copy(k, src, chip, core, to):
            return pltpu.make_async_remote_copy(src_ref=src, dst_ref=piece(chip, core), send_sem=send_sems.at[k],
                                                recv_sem=recv_sems.at[k], device_id=to, device_id_type=MESH)

        first = [copy(j, w_ref.at[pl.ds(c * half, half), :], me, c, (px, py, c)) for j, (px, py) in enumerate(others)]
        for cp in first:
            cp.start()
        passed = [copy(3 + j, piece(2 * px + py, c), 2 * px + py, c, sibling) for j, (px, py) in enumerate(others)]
        for j, (px, py) in enumerate(others):
            copy(j, piece(2 * px + py, c), 2 * px + py, c, (x, y, c)).wait_recv()
            passed[j].start()
        for j, (px, py) in enumerate(others):
            copy(3 + j, piece(2 * px + py, 1 - c), 2 * px + py, 1 - c, (x, y, c)).wait_recv()
        for cp in first + passed:
            cp.wait_send()
        mine.wait()

    return _pallas(
        body, out_shape=jax.ShapeDtypeStruct((N_CHIPS, R, LANES), wp.dtype),
        in_specs=[ANY], out_specs=ANY,
        scratch_shapes=[pltpu.SemaphoreType.DMA((6,)), pltpu.SemaphoreType.DMA((6,)), pltpu.SemaphoreType.DMA],
        name="weights_all_gather")(wp)


def _pair_exchange(g):
    R = g.shape[1]
    half = R // 2

    def body(g_ref, o_ref, send_sem, recv_sem):
        x, y, c, _ = _place()
        cp = pltpu.make_async_remote_copy(
            src_ref=g_ref.at[:, pl.ds((1 - c) * half, half), :], dst_ref=o_ref, send_sem=send_sem, recv_sem=recv_sem,
            device_id=(x, y, 1 - c), device_id_type=MESH)
        cp.start()
        cp.wait()

    return _pallas(
        body, out_shape=jax.ShapeDtypeStruct((N_CHIPS, half, LANES), g.dtype), in_specs=[ANY], out_specs=ANY,
        scratch_shapes=[pltpu.SemaphoreType.DMA, pltpu.SemaphoreType.DMA], name="grad_pair_exchange")(g)


def _pair_sum(g, recv, cidx, *, tr):
    half = recv.shape[1]
    nblk = half // tr

    def body(c_ref, g_ref, r_ref, o_ref):
        o_ref[...] = (g_ref[...] + r_ref[...]).astype(o_ref.dtype)

    grid_spec = pltpu.PrefetchScalarGridSpec(
        num_scalar_prefetch=1, grid=(N_CHIPS, nblk),
        in_specs=[pl.BlockSpec((1, tr, LANES), lambda s, i, c: (s, c[0] * nblk + i, 0)),
                  pl.BlockSpec((1, tr, LANES), lambda s, i, c: (s, i, 0))],
        out_specs=pl.BlockSpec((1, tr, LANES), lambda s, i, c: (s, i, 0)))
    return _pallas(
        body, out_shape=jax.ShapeDtypeStruct((N_CHIPS, half, LANES), BF16), grid_spec=grid_spec,
        name="grad_pair_sum", compiler_params=_params((N_CHIPS, nblk)))(cidx, g, recv)


def _chip_exchange(part):
    def body(p_ref, o_ref, send_sems, recv_sems, local_sem):
        x, y, c, others = _place()
        me = 2 * x + y
        mine = pltpu.make_async_copy(p_ref.at[me], o_ref.at[me], local_sem)
        mine.start()
        sends = []
        for j, (px, py) in enumerate(others):
            cp = pltpu.make_async_remote_copy(
                src_ref=p_ref.at[2 * px + py], dst_ref=o_ref.at[me], send_sem=send_sems.at[j], recv_sem=recv_sems.at[j],
                device_id=(px, py, c), device_id_type=MESH)
            cp.start()
            sends.append(cp)
        for j, (px, py) in enumerate(others):
            pltpu.make_async_remote_copy(
                src_ref=p_ref.at[me], dst_ref=o_ref.at[2 * px + py], send_sem=send_sems.at[j], recv_sem=recv_sems.at[j],
                device_id=(px, py, c), device_id_type=MESH).wait_recv()
        for cp in sends:
            cp.wait_send()
        mine.wait()

    return _pallas(
        body, out_shape=jax.ShapeDtypeStruct(part.shape, part.dtype), in_specs=[ANY], out_specs=ANY,
        scratch_shapes=[pltpu.SemaphoreType.DMA((3,)), pltpu.SemaphoreType.DMA((3,)), pltpu.SemaphoreType.DMA],
        name="grad_chip_exchange")(part)


def _chip_sum(slots, *, tr):
    half = slots.shape[1]

    def body(s_ref, o_ref):
        v = s_ref[...].astype(F32)
        o_ref[...] = ((v[0] + v[1]) + v[2]) + v[3]

    return _pallas(
        body, out_shape=jax.ShapeDtypeStruct((half, LANES), F32), grid=(half // tr,),
        in_specs=[pl.BlockSpec((N_CHIPS, tr, LANES), lambda i: (0, i, 0))],
        out_specs=pl.BlockSpec((tr, LANES), lambda i: (i, 0)),
        name="grad_chip_sum", compiler_params=_params((half // tr,)))(slots)


def _pair_gather(fin):
    half = fin.shape[0]

    def body(f_ref, o_ref, send_sem, recv_sem, local_sem):
        x, y, c, _ = _place()
        mine = pltpu.make_async_copy(f_ref, o_ref.at[pl.ds(c * half, half), :], local_sem)
        mine.start()
        cp = pltpu.make_async_remote_copy(
            src_ref=f_ref, dst_ref=o_ref.at[pl.ds(c * half, half), :], send_sem=send_sem, recv_sem=recv_sem,
            device_id=(x, y, 1 - c), device_id_type=MESH)
        cp.start()
        pltpu.make_async_remote_copy(
            src_ref=f_ref, dst_ref=o_ref.at[pl.ds((1 - c) * half, half), :], send_sem=send_sem, recv_sem=recv_sem,
            device_id=(x, y, 1 - c), device_id_type=MESH).wait_recv()
        cp.wait_send()
        mine.wait()

    return _pallas(
        body, out_shape=jax.ShapeDtypeStruct((2 * half, LANES), fin.dtype), in_specs=[ANY], out_specs=ANY,
        scratch_shapes=[pltpu.SemaphoreType.DMA, pltpu.SemaphoreType.DMA, pltpu.SemaphoreType.DMA],
        name="grad_pair_gather")(fin)


def _adamw(w, g, m, v, *, tr):
    R = w.shape[0]
    c1 = 1.0 - ADAM_B1 ** ADAM_STEP
    c2 = 1.0 - ADAM_B2 ** ADAM_STEP

    def body(w_ref, g_ref, m_ref, v_ref, d_ref, nm_ref, nv_ref):
        gv = g_ref[...]
        nm = ADAM_B1 * m_ref[...] + (1.0 - ADAM_B1) * gv
        nv = ADAM_B2 * v_ref[...] + (1.0 - ADAM_B2) * (gv * gv)
        d_ref[...] = -ADAM_LR * ((nm / c1) / (jnp.sqrt(nv / c2) + ADAM_EPS) + ADAM_WD * w_ref[...])
        nm_ref[...] = nm
        nv_ref[...] = nv

    spec = pl.BlockSpec((tr, LANES), lambda i: (i, 0))
    shape = jax.ShapeDtypeStruct((R, LANES), F32)
    return _pallas(
        body, out_shape=[shape, shape, shape], grid=(R // tr,), in_specs=[spec] * 4, out_specs=[spec] * 3,
        name="adamw", compiler_params=_params((R // tr,)))(w, g, m, v)


PARAMS = (("meta", 1), ("norm1", None), ("w_in", 2), ("gdn_conv_w", 2), ("gdn_a_log", None), ("gdn_dt_bias", None),
          ("gdn_norm", None), ("w_out", 1), ("norm2", None), ("w_ffn_up", 2), ("ffn_conv_w", 2), ("ffn_conv_b", None),
          ("w_ffn_down", 1), ("norm_f", None))
PACK_ALIGN = 1024
PACK_ROWS_ALIGN = 1024


def _pack(arrs, dtype):
    parts, total = [], 0
    for a in arrs:
        f = a.reshape(-1).astype(dtype)
        pad = (-f.shape[0]) % PACK_ALIGN
        parts.append(jnp.pad(f, (0, pad)) if pad else f)
        total += f.shape[0] + pad
    rows = total // LANES
    rpad = (-rows) % PACK_ROWS_ALIGN
    if rpad:
        parts.append(jnp.zeros((rpad * LANES,), dtype))
    return jnp.concatenate(parts).reshape(rows + rpad, LANES)


def _unpack(buf, shapes):
    flat = buf.reshape(-1)
    outs, off = [], 0
    for s in shapes:
        n = int(np.prod(s))
        outs.append(flat[off:off + n].reshape(s))
        off += n + (-n) % PACK_ALIGN
    return outs


def _split4(a, axis):
    n = a.shape[axis] // N_CHIPS
    return [lax.slice_in_dim(a, s * n, (s + 1) * n, axis=axis) for s in range(N_CHIPS)]


def _reorder_w_in(w, cfg):
    d, hg = cfg.d, cfg.hg
    tail = jnp.pad(w[:, 4 * d:4 * d + 2 * hg], ((0, 0), (0, LANES - 2 * hg)))
    return jnp.concatenate([w[:, :4 * d], w[:, 4 * d + 2 * hg:], tail], axis=1)


def _restore_w_in(wr, cfg):
    d, hg = cfg.d, cfg.hg
    return jnp.concatenate([wr[:, :4 * d], wr[:, 10 * d:10 * d + 2 * hg], wr[:, 4 * d:10 * d]], axis=1)


def _step(cfg, x, tgt, shard, m_shard, v_shard):
    d, hg, hr, dff, rp, tr, tm = cfg.d, cfg.hg, cfg.hr, cfg.dff, cfg.rp, cfg.tr, cfg.tm
    nrow = rp // tr
    cidx = lax.axis_index("c").astype(jnp.int32).reshape(1)

    big = ("w_in", "w_out", "w_ffn_up", "w_ffn_down")
    small = ("meta", "gdn_conv_w", "ffn_conv_w")
    small_bits = [lax.bitcast_convert_type(shard[n], BF16) for n in small]
    wp = _pack([shard[n] for n in big] + small_bits, BF16)
    gathered = _all_gather_chips(wp)
    shapes = [shard[n].shape for n in big] + [b.shape for b in small_bits]
    per_chip = [_unpack(gathered[s], shapes) for s in range(N_CHIPS)]
    axis = dict(PARAMS)
    full = {}
    for k, n in enumerate(big):
        full[n] = jnp.concatenate([per_chip[s][k] for s in range(N_CHIPS)], axis=axis[n])[0]
    for k, n in enumerate(small):
        parts = [lax.bitcast_convert_type(per_chip[s][len(big) + k], F32) for s in range(N_CHIPS)]
        full[n] = jnp.concatenate(parts, axis=axis[n])
    w_in_r = _reorder_w_in(full["w_in"], cfg)
    w_out, w_up, w_down = full["w_out"], full["w_ffn_up"], full["w_ffn_down"]
    meta = full["meta"]
    gconv_w = full["gdn_conv_w"][0]
    fconv_w = full["ffn_conv_w"][0]
    norm1, norm2, gnorm = shard["norm1"], shard["norm2"], shard["gdn_norm"]
    normf = shard["norm_f"].reshape(1, d)
    fconv_b = shard["ffn_conv_b"]
    alog = jnp.pad(shard["gdn_a_log"], ((0, 7), (0, LANES - hg)))
    dtb = jnp.pad(shard["gdn_dt_bias"], ((0, 7), (0, LANES - hg)))

    h0 = jnp.concatenate([jnp.zeros((cfg.front, d), F32), meta, x], axis=0)
    pos = jnp.arange(rp, dtype=F32) - float(cfg.front)
    half = RET_DK // 2
    inv = 1.0 / (ROPE_BASE ** (jnp.arange(half, dtype=F32) / half))
    ang = pos[:, None] * inv[None, :]
    cos, sin = jnp.cos(ang), jnp.sin(ang)
    rconsts = _ret_consts(cfg)

    rms_f = _make_rms_fn(cfg, tr, False)
    rms_b = _make_rms_fn(cfg, tr, True)
    rowshape = jax.ShapeDtypeStruct((rp, d), F32)
    rspec = _rows(tr, d)

    def rms_fwd(name, h, g):
        return _stage_fwd(name, rms_f, (nrow,), [In(h, rspec), In(g, _full(g))],
                          [jax.ShapeDtypeStruct((rp, d), BF16)], [rspec])[0]

    tn_in = cfg.pw // 9 if cfg.pw % (9 * LANES) == 0 else LANES
    hn1 = rms_fwd("rms1_fwd", h0, norm1)
    proj = _mm("proj_fwd", hn1, w_in_r, tm=tm, tn=tn_in, tk=d)
    cqkv = _conv_fwd("gdn_conv_fwd", proj, 0, gconv_w, None, taps=GDN_CONV, width=3 * d, tr=tr, tc=d)
    tail_spec = _rows(tr, LANES, 10 * d // LANES)
    prep_fn = _make_gdn_prep_fn(cfg, tr)
    gd = dict(gshape=(rp, d), gspec=rspec)
    prep_ins = [In(cqkv, _rows(tr, 3 * d), F32),
                In(proj, tail_spec, BF16, gshape=(rp, LANES), gspec=_rows(tr, LANES)),
                In(alog, _full(alog), F32, True), In(dtb, _full(dtb), F32, True)]
    qn, kn, vv, bB, lB = _stage_fwd("gdn_prep_fwd", prep_fn, (nrow,), prep_ins, [rowshape] * 5, [rspec] * 5)

    trg = cfg.nb * CHUNK
    gi_grid = (rp // trg, hg)
    hspec = pl.BlockSpec((trg, GDN_DK), lambda i, h: (i, h))
    aspec = pl.BlockSpec((1, trg, CHUNK), lambda i, h: (h, i, 0))
    gspec = pl.BlockSpec((1, cfg.nb, 1, GDN_DK), lambda i, h: (h, i, 0, 0))
    intra_ins = [In(t, hspec, F32) for t in (qn, kn, vv, bB, lB)]
    intra_shapes = [rowshape, rowshape, jax.ShapeDtypeStruct((hg, rp, CHUNK), F32), rowshape, rowshape,
                    jax.ShapeDtypeStruct((hg, cfg.nch, 1, GDN_DK), F32)]
    intra_specs = [hspec, hspec, aspec, hspec, hspec, gspec]
    gu, gw, gattn, gqd, gkd, ggl = _stage_fwd("gdn_intra_fwd", _gdn_intra_fn, gi_grid, intra_ins, intra_shapes, intra_specs)
    oa, gss = _gdn_scan_fwd(cfg, gu, gw, gattn, gqd, gkd, ggl)

    rot_fn = _make_rot_fn(cfg)
    rot_ins = [In(proj, _rows(tr, d, 4), BF16, **gd), In(proj, _rows(tr, d, 5), BF16, **gd),
               In(cos, _rows(tr, half)), In(sin, _rows(tr, half))]
    qr, kr = _stage_fwd("rot_fwd", rot_fn, (nrow,), rot_ins, [rowshape] * 2, [rspec] * 2)
    ob, rss = _ret_scan_fwd(cfg, qr, kr, proj, rconsts)

    mix_fn = _make_mix_fn(cfg)
    mix_ins = [In(oa, rspec, F32), In(proj, _rows(tr, d, 3), BF16, **gd), In(ob, rspec, F32),
               In(proj, _rows(tr, d, 7), BF16, **gd), In(proj, _rows(tr, d, 8), BF16, **gd),
               In(proj, _rows(tr, d, 9), BF16, **gd), In(gnorm, _full(gnorm), F32, True)]
    ymix = _stage_fwd("mix_fwd", mix_fn, (nrow,), mix_ins, [jax.ShapeDtypeStruct((rp, d), BF16)], [rspec])[0]
    h1 = _mm("out_proj_fwd", ymix, w_out, tm=tm, tn=d, tk=d, add=h0)

    hn2 = rms_fwd("rms2_fwd", h1, norm2)
    up = _mm("ffn_up_fwd", hn2, w_up, tm=tm, tn=cfg.tf, tk=d)
    uc = _conv_fwd("ffn_conv_fwd", up, 0, fconv_w, fconv_b, taps=FFN_CONV, width=2 * dff, tr=tr, tc=cfg.tf)
    nfb = dff // cfg.tf
    act_grid = (nfb, nrow)
    act_spec = pl.BlockSpec((tr, cfg.tf), lambda j, i: (i, j))
    ga = dict(gshape=(rp, dff), gspec=act_spec)
    act_ins = [In(uc, act_spec, F32, **ga), In(uc, pl.BlockSpec((tr, cfg.tf), lambda j, i: (i, nfb + j)), F32, **ga)]
    act = _stage_fwd("ffn_act_fwd", _act_fn, act_grid, act_ins, [jax.ShapeDtypeStruct((rp, dff), BF16)], [act_spec])[0]
    h2 = _mm("ffn_down_fwd", act, w_down, tm=tm, tn=d, tk=cfg.tf, add=h1)

    dh2, g_normf, loss_blk = _final(cfg, h2, normf, tgt)
    loss = lax.psum(loss_blk[0, 0], ("x", "y", "c"))

    g_w_down = _mm_tn("ffn_down_dw", act, dh2, tr=tm, tka=cfg.tf, tn=d)
    dact = _mm("ffn_down_dx", dh2, w_down.T, tm=tm, tn=cfg.tf, tk=d)
    duc_a, duc_b = _stage_bwd("ffn_act_bwd", _act_fn, act_grid, act_ins, [(dact, act_spec)])
    duc = jnp.concatenate([duc_a, duc_b], axis=1)
    dup, g_fconv_w, g_fconv_b = _conv_bwd("ffn_conv_bwd", up, 0, fconv_w, duc, taps=FFN_CONV, width=2 * dff,
                                          tr=tr, tc=cfg.tf, with_bias=True)
    g_w_up = _mm_tn("ffn_up_dw", hn2, dup, tr=tm, tka=d, tn=cfg.tf)
    dhn2 = _mm("ffn_up_dx", dup, w_up.T, tm=tm, tn=d, tk=cfg.tf)

    def rms_bwd(name, h, g, dhn, dres):
        ins = [In(h, rspec, F32), In(g, _full(g), F32, True)]
        return _stage_bwd(name, rms_b, (nrow,), ins, [(dhn, rspec), (dres, rspec)])

    dh1, g_norm2 = rms_bwd("rms2_bwd", h1, norm2, dhn2, dh2)
    g_w_out = _mm_tn("out_proj_dw", ymix, dh1, tr=tm, tka=d, tn=d)
    dymix = _mm("out_proj_dx", dh1, w_out.T, tm=tm, tn=d, tk=d)
    doa, dgz, dob, drg, dgate_a, dgate_b, g_gnorm = _stage_bwd("mix_bwd", mix_fn, (nrow,), mix_ins, [(dymix, rspec)])

    dqr, dkr, drv = _ret_scan_bwd(cfg, dob, qr, kr, proj, rconsts, rss)
    drq, drk = _stage_bwd("rot_bwd", rot_fn, (nrow,), rot_ins, [(dqr, rspec), (dkr, rspec)])

    dgu, dgw, dgattn, dgqd, dgkd, dggl = _gdn_scan_bwd(cfg, doa, gu, gw, gattn, gqd, gkd, ggl, gss)
    intra_cots = [(dgu, hspec), (dgw, hspec), (dgattn, aspec), (dgqd, hspec), (dgkd, hspec), (dggl, gspec)]
    dqn, dkn, dvv, dbB, dlB = _stage_bwd("gdn_intra_bwd", _gdn_intra_fn, gi_grid, intra_ins, intra_cots)
    dcqkv, dtail, g_alog, g_dtb = _stage_bwd(
        "gdn_prep_bwd", prep_fn, (nrow,), prep_ins, [(t, rspec) for t in (dqn, dkn, dvv, dbB, dlB)])
    dqkv, g_gconv_w = _conv_bwd("gdn_conv_bwd", proj, 0, gconv_w, dcqkv, taps=GDN_CONV, width=3 * d,
                                tr=tr, tc=d, with_bias=False)
    dproj = jnp.concatenate([dqkv, dgz, drq, drk, drv, drg, dgate_a, dgate_b, dtail], axis=1)
    g_w_in_r = _mm_tn("proj_dw", hn1, dproj, tr=tm, tka=d, tn=tn_in)
    dhn1 = _mm("proj_dx", dproj, w_in_r.T, tm=tm, tn=d, tk=tn_in)
    dh0, g_norm1 = rms_bwd("rms1_bwd", h0, norm1, dhn1, dh1)

    grad_x = dh0[cfg.xrow:]
    grads = {
        "meta": dh0[cfg.front:cfg.xrow], "norm1": g_norm1, "w_in": _restore_w_in(g_w_in_r, cfg)[None],
        "gdn_conv_w": g_gconv_w[None], "gdn_a_log": g_alog[0:1, :hg], "gdn_dt_bias": g_dtb[0:1, :hg],
        "gdn_norm": g_gnorm, "w_out": g_w_out[None], "norm2": g_norm2, "w_ffn_up": g_w_up[None],
        "ffn_conv_w": g_fconv_w[None], "ffn_conv_b": g_fconv_b, "w_ffn_down": g_w_down[None],
        "norm_f": g_normf.reshape(d),
    }

    names = [n for n, _ in PARAMS]
    per = []
    for s in range(N_CHIPS):
        per.append(_pack([grads[n] if ax is None else _split4(grads[n], ax)[s] for n, ax in PARAMS], F32))
    gpack = jnp.stack(per)
    tra = 512
    recv = _pair_exchange(gpack)
    part = _pair_sum(gpack, recv, cidx, tr=tra)
    slots = _chip_exchange(part)
    fin = _chip_sum(slots, tr=tra)
    gsum = _pair_gather(fin)

    wpk = _pack([shard[n] for n in names], F32)
    mpk = _pack([m_shard[n] for n in names], F32)
    vpk = _pack([v_shard[n] for n in names], F32)
    delta, new_m, new_v = _adamw(wpk, gsum, mpk, vpk, tr=tra)
    shapes = [shard[n].shape for n in names]
    out_g = _unpack(gsum, shapes)
    out_d = _unpack(delta, shapes)
    out_m = _unpack(new_m, shapes)
    out_v = _unpack(new_v, shapes)
    return (loss, grad_x[None], *out_g, *out_d, *out_m, *out_v)


def kernel(x, meta, norm1, w_in, gdn_conv_w, gdn_a_log, gdn_dt_bias, gdn_norm, w_out, norm2, w_ffn_up, ffn_conv_w, ffn_conv_b, w_ffn_down, norm_f, loss_target, m_meta, m_norm1, m_w_in, m_gdn_conv_w, m_gdn_a_log, m_gdn_dt_bias, m_gdn_norm, m_w_out, m_norm2, m_w_ffn_up, m_ffn_conv_w, m_ffn_conv_b, m_w_ffn_down, m_norm_f, v_meta, v_norm1, v_w_in, v_gdn_conv_w, v_gdn_a_log, v_gdn_dt_bias, v_gdn_norm, v_w_out, v_norm2, v_w_ffn_up, v_ffn_conv_w, v_ffn_conv_b, v_w_ffn_down, v_norm_f):
    names = [n for n, _ in PARAMS]
    shard = dict(zip(names, (meta, norm1, w_in, gdn_conv_w, gdn_a_log, gdn_dt_bias, gdn_norm, w_out, norm2, w_ffn_up,
                             ffn_conv_w, ffn_conv_b, w_ffn_down, norm_f)))
    m_shard = dict(zip(names, (m_meta, m_norm1, m_w_in, m_gdn_conv_w, m_gdn_a_log, m_gdn_dt_bias, m_gdn_norm, m_w_out,
                               m_norm2, m_w_ffn_up, m_ffn_conv_w, m_ffn_conv_b, m_w_ffn_down, m_norm_f)))
    v_shard = dict(zip(names, (v_meta, v_norm1, v_w_in, v_gdn_conv_w, v_gdn_a_log, v_gdn_dt_bias, v_gdn_norm, v_w_out,
                               v_norm2, v_w_ffn_up, v_ffn_conv_w, v_ffn_conv_b, v_w_ffn_down, v_norm_f)))
    return _step(REAL, x[0], loss_target[0], shard, m_shard, v_shard)
```

```python
import functools
from typing import NamedTuple

import numpy as np
import jax
import jax.numpy as jnp
from jax import lax
from jax.experimental import pallas as pl
from jax.experimental.pallas import tpu as pltpu

F32 = jnp.float32
BF16 = jnp.bfloat16
EPS = 1e-6
CHUNK = 64
GDN_DK = 128
RET_DK = 256
GDN_CONV = 4
FFN_CONV = 3
ROPE_BASE = 10000.0
LANES = 128
N_CHIPS = 4
ADAM_LR, ADAM_B1, ADAM_B2, ADAM_EPS, ADAM_WD, ADAM_STEP = 0.001, 0.9, 0.999, 1e-08, 0.01, 10
MIX_COL, CONV_COL, RV_BLOCK, ROT_COL, TAIL_COL = 0, 4, 7, 8, 10
MESH = pl.DeviceIdType.MESH
VMEM_LIMIT = 56 * 1024 * 1024


class Cfg(NamedTuple):
    d: int
    seq: int
    n_meta: int
    dff: int
    tr: int
    nb: int
    tm: int
    tf: int

    @property
    def hg(self): return self.d // GDN_DK
    @property
    def hr(self): return self.d // RET_DK
    @property
    def L(self): return self.n_meta + self.seq
    @property
    def rp(self): return -(-self.L // 256) * 256
    @property
    def front(self): return self.rp - self.L
    @property
    def xrow(self): return self.rp - self.seq
    @property
    def nch(self): return self.rp // CHUNK
    @property
    def pw(self): return 10 * self.d + LANES
    @property
    def din(self): return 10 * self.d + 2 * self.hg


REAL = Cfg(d=1024, seq=8192, n_meta=16, dff=2816, tr=256, nb=12, tm=768, tf=1408)


def _pallas(body, **kw):
    return pl.pallas_call(body, **kw)


def _sigmoid(x):
    return 1.0 / (1.0 + jnp.exp(-x))


def _silu(x):
    return x * _sigmoid(x)


def _softplus(x):
    return jnp.maximum(x, 0.0) + jnp.log(1.0 + jnp.exp(-jnp.abs(x)))


def _raw_dot(a, b, ta, tb, hi):
    if not hi:
        a = a.astype(BF16)
        b = b.astype(BF16)
    nbatch = a.ndim - 2
    ca = a.ndim - 2 if ta else a.ndim - 1
    cb = b.ndim - 1 if tb else b.ndim - 2
    batch = tuple(range(nbatch))
    return lax.dot_general(a, b, (((ca,), (cb,)), (batch, batch)),
                           precision=lax.Precision.HIGHEST if hi else None,
                           preferred_element_type=F32)


@functools.partial(jax.custom_vjp, nondiff_argnums=(2, 3, 4))
def _dot_p(a, b, ta, tb, hi):
    return _raw_dot(a, b, ta, tb, hi)


def _dot(a, b, ta=False, tb=False, hi=False):
    return _dot_p(a, b, ta, tb, hi)


def _dot_fwd(a, b, ta, tb, hi):
    return _raw_dot(a, b, ta, tb, hi), (a, b)


def _dot_bwd(ta, tb, hi, res, g):
    a, b = res
    if not ta and not tb:
        da, db = _dot(g, b, False, True, hi), _dot(a, g, True, False, hi)
    elif not ta and tb:
        da, db = _dot(g, b, False, False, hi), _dot(g, a, True, False, hi)
    elif ta and not tb:
        da, db = _dot(b, g, False, True, hi), _dot(a, g, False, False, hi)
    else:
        raise NotImplementedError
    return da.astype(a.dtype), db.astype(b.dtype)


_dot_p.defvjp(_dot_fwd, _dot_bwd)


def _iota2(n, m, axis):
    return lax.broadcasted_iota(jnp.int32, (n, m), axis)


def _bcast(mat, nb):
    return jnp.broadcast_to(mat[None], (nb,) + mat.shape)


def _split(a):
    hi = a.astype(BF16)
    return hi, (a - hi.astype(F32)).astype(BF16)


def _dot3(a, b, ta=False, tb=False):
    ah, al = _split(a)
    bh, bl = _split(b)
    return _raw_dot(ah, bh, ta, tb, False) + (_raw_dot(ah, bl, ta, tb, False) + _raw_dot(al, bh, ta, tb, False))


def _tri_inv_raw(m):
    nb = m.shape[0]
    r, c = _iota2(CHUNK, CHUNK, 0), _iota2(CHUNK, CHUNK, 1)
    t = _bcast((r == c).astype(F32), nb)
    b = 1
    while b < CHUNK:
        sh = b.bit_length() - 1
        off = ((r >> (sh + 1)) == (c >> (sh + 1))) & ((r >> sh) != (c >> sh)) & (r > c)
        cl = jnp.where(off[None], m, 0.0)
        t = t - _dot3(_dot3(t, cl), t)
        b *= 2
    return t


@jax.custom_vjp
def _tri_inv_given(m, t):
    return t


def _tri_inv_fwd(m, t):
    return t, t


def _tri_inv_bwd(t, g):
    return -_dot3(_dot3(t, g, True, False), t, False, True), jnp.zeros_like(t)


_tri_inv_given.defvjp(_tri_inv_fwd, _tri_inv_bwd)


def _rms(h, g):
    return h * lax.rsqrt(jnp.mean(h * h, axis=-1, keepdims=True) + EPS) * g


class In(NamedTuple):
    arr: jax.Array
    spec: pl.BlockSpec
    grad: object = None
    acc: bool = False
    gshape: object = None
    gspec: object = None
    galias: object = None


def _params(grid):
    sem = ("arbitrary",) * len(grid)
    return pltpu.CompilerParams(dimension_semantics=sem, vmem_limit_bytes=VMEM_LIMIT)


def _stage_fwd(name, fn, grid, ins, out_shapes, out_specs):
    n_in = len(ins)

    def body(*refs):
        pids = tuple(pl.program_id(k) for k in range(len(grid)))
        vals = [r[...] for r in refs[:n_in]]
        outs = fn(pids, *vals)
        for o_ref, o in zip(refs[n_in:], outs):
            o_ref[...] = o.reshape(o_ref.shape).astype(o_ref.dtype)

    return _pallas(
        body, out_shape=out_shapes, grid=grid, in_specs=[i.spec for i in ins],
        out_specs=out_specs, name=name, compiler_params=_params(grid))(*[i.arr for i in ins])


def _stage_bwd(name, fn, grid, ins, cots):
    n_in, n_ct = len(ins), len(cots)
    didx = [k for k, i in enumerate(ins) if i.grad is not None]
    aliased = [(o, ins[k].galias) for o, k in enumerate(didx) if ins[k].galias is not None]
    n_al = len(aliased)

    def body(*refs):
        pids = tuple(pl.program_id(k) for k in range(len(grid)))
        vals = [r[...] for r in refs[:n_in]]
        ct_refs = refs[n_in:n_in + n_ct]
        g_refs = refs[n_in + n_ct + n_al:]

        def f(*dv):
            merged = list(vals)
            for k, v in zip(didx, dv):
                merged[k] = v
            return tuple(fn(pids, *merged))

        outs, vjp_fn = jax.vjp(f, *[vals[k].astype(F32) for k in didx])
        cts = tuple(c[...].reshape(o.shape).astype(F32) for c, o in zip(ct_refs, outs))
        grads = vjp_fn(cts)
        first = functools.reduce(jnp.logical_and, [p == 0 for p in pids])
        for k, g_ref, g in zip(didx, g_refs, grads):
            if ins[k].acc:
                @pl.when(first)
                def _(g_ref=g_ref):
                    g_ref[...] = jnp.zeros(g_ref.shape, g_ref.dtype)
                g_ref[...] += g.reshape(g_ref.shape).astype(g_ref.dtype)
            else:
                g_ref[...] = g.reshape(g_ref.shape).astype(g_ref.dtype)

    out_shapes = [jax.ShapeDtypeStruct(ins[k].gshape or ins[k].arr.shape, ins[k].grad) for k in didx]
    out_specs = [ins[k].gspec or ins[k].spec for k in didx]
    return _pallas(
        body, out_shape=out_shapes, grid=grid,
        in_specs=[i.spec for i in ins] + [c[1] for c in cots] + [ANY] * n_al, out_specs=out_specs,
        input_output_aliases={n_in + n_ct + a: o for a, (o, _) in enumerate(aliased)},
        name=name, compiler_params=_params(grid))(*[i.arr for i in ins], *[c[0] for c in cots], *[a for _, a in aliased])


def _full(arr):
    nd = arr.ndim
    return pl.BlockSpec(arr.shape, lambda *p: (0,) * nd)


def _rows(tr, width, blk=0):
    return pl.BlockSpec((tr, width), lambda i: (i, blk))


def _mm(name, a, b, *, tm, tn, tk, out_dtype=F32, add=None):
    M, K = a.shape
    N = b.shape[1] if b.ndim == 2 else b.shape[0] * b.shape[2]
    nk = K // tk
    grid = (M // tm, N // tn, nk)

    def body(*refs):
        a_ref, b_ref = refs[0], refs[1]
        add_ref = refs[2] if add is not None else None
        o_ref = refs[3 if add is not None else 2]
        acc_ref = refs[-1] if nk > 1 else None
        part = _raw_dot(a_ref[...], b_ref[...], False, False, False)

        def finish(total):
            if add_ref is not None:
                total = total + add_ref[...]
            o_ref[...] = total.astype(o_ref.dtype)

        if nk == 1:
            finish(part)
        else:
            k = pl.program_id(2)

            @pl.when(k == 0)
            def _():
                acc_ref[...] = part

            @pl.when(k > 0)
            def _():
                acc_ref[...] += part

            @pl.when(k == nk - 1)
            def _():
                finish(acc_ref[...])

    b_spec = (pl.BlockSpec((tk, tn), lambda i, j, k: (k, j)) if b.ndim == 2 else
              pl.BlockSpec((None, tk, tn), lambda i, j, k: (j, k, 0)))
    in_specs = [pl.BlockSpec((tm, tk), lambda i, j, k: (i, k)), b_spec]
    args = [a, b]
    if add is not None:
        in_specs.append(pl.BlockSpec((tm, tn), lambda i, j, k: (i, j)))
        args.append(add)
    return _pallas(
        body, out_shape=jax.ShapeDtypeStruct((M, N), out_dtype), grid=grid, in_specs=in_specs,
        out_specs=pl.BlockSpec((tm, tn), lambda i, j, k: (i, j)),
        scratch_shapes=[pltpu.VMEM((tm, tn), F32)] if nk > 1 else [],
        name=name, compiler_params=_params(grid))(*args)


def _mm_tn(name, a, b, *, tr, tka, tn, blocked=False):
    R, Ka = a.shape
    N = b.shape[1]
    nr = R // tr
    grid = (Ka // tka, N // tn, nr)
    if blocked:
        out_shape = jax.ShapeDtypeStruct((N // tn, Ka, tn), F32)
        out_spec = pl.BlockSpec((None, tka, tn), lambda i, j, r: (j, i, 0))
    else:
        out_shape = jax.ShapeDtypeStruct((Ka, N), F32)
        out_spec = pl.BlockSpec((tka, tn), lambda i, j, r: (i, j))

    def body(a_ref, b_ref, o_ref):
        r = pl.program_id(2)
        part = _raw_dot(a_ref[...], b_ref[...], True, False, False)

        @pl.when(r == 0)
        def _():
            o_ref[...] = part

        @pl.when(r > 0)
        def _():
            o_ref[...] += part

    return _pallas(
        body, out_shape=out_shape, grid=grid,
        in_specs=[pl.BlockSpec((tr, tka), lambda i, j, r: (r, i)),
                  pl.BlockSpec((tr, tn), lambda i, j, r: (r, j))],
        out_specs=out_spec, name=name, compiler_params=_params(grid))(a, b)


def _conv_fwd(name, x, xcol0, w, b, *, taps, width, tr, tc):
    R = x.shape[0]
    grid = (width // tc, R // tr)
    cb0 = xcol0 // tc
    hb = tr // 8

    def body(*refs):
        x_ref, xp_ref, w_ref = refs[:3]
        b_ref = refs[3] if b is not None else None
        o_ref = refs[-1]
        i = pl.program_id(1)
        xv = x_ref[...]
        prev = jnp.where(i > 0, xp_ref[...], 0.0)
        ext = jnp.concatenate([prev, xv], axis=0)
        acc = xv * w_ref[taps - 1:taps, :]
        for s in range(1, taps):
            acc = acc + pltpu.roll(ext, s, 0)[8:, :] * w_ref[taps - 1 - s:taps - s, :]
        if b_ref is not None:
            acc = acc + b_ref[...]
        o_ref[...] = acc

    in_specs = [pl.BlockSpec((tr, tc), lambda j, i: (i, cb0 + j)),
                pl.BlockSpec((8, tc), lambda j, i: (jnp.maximum(i * hb - 1, 0), cb0 + j)),
                pl.BlockSpec((taps, tc), lambda j, i: (0, j))]
    args = [x, x, w]
    if b is not None:
        in_specs.append(pl.BlockSpec((1, tc), lambda j, i: (0, j)))
        args.append(b)
    return _pallas(
        body, out_shape=jax.ShapeDtypeStruct((R, width), F32), grid=grid, in_specs=in_specs,
        out_specs=pl.BlockSpec((tr, tc), lambda j, i: (i, j)),
        name=name, compiler_params=_params(grid))(*args)


def _conv_bwd(name, x, xcol0, w, dy, *, taps, width, tr, tc, with_bias, dx_into=None):
    R = x.shape[0]
    nr = R // tr
    grid = (width // tc, nr)
    cb0 = xcol0 // tc
    hb = tr // 8
    n_ext = tr + 8
    n_al = 0 if dx_into is None else 1

    def body(*refs):
        x_ref, xp_ref, w_ref, dy_ref, dyn_ref = refs[:5]
        dx_ref, dw_ref = refs[5 + n_al], refs[6 + n_al]
        db_ref = refs[7 + n_al] if with_bias else None
        i = pl.program_id(1)
        xv = x_ref[...]
        ext = jnp.concatenate([jnp.where(i > 0, xp_ref[...], 0.0), xv], axis=0)
        dyv = dy_ref[...]
        dext = jnp.concatenate([dyv, jnp.where(i < nr - 1, dyn_ref[...], 0.0)], axis=0)
        dx = dyv * w_ref[taps - 1:taps, :]
        dws = [None] * taps
        dws[taps - 1] = jnp.sum(xv * dyv, axis=0, keepdims=True)
        for s in range(1, taps):
            dx = dx + pltpu.roll(dext, n_ext - s, 0)[:tr, :] * w_ref[taps - 1 - s:taps - s, :]
            dws[taps - 1 - s] = jnp.sum(pltpu.roll(ext, s, 0)[8:, :] * dyv, axis=0, keepdims=True)
        dx_ref[...] = dx.astype(dx_ref.dtype)

        @pl.when(i == 0)
        def _():
            for k in range(taps):
                dw_ref[k:k + 1, :] = dws[k]
            if db_ref is not None:
                db_ref[...] = jnp.sum(dyv, axis=0, keepdims=True)

        @pl.when(i > 0)
        def _():
            for k in range(taps):
                dw_ref[k:k + 1, :] += dws[k]
            if db_ref is not None:
                db_ref[...] += jnp.sum(dyv, axis=0, keepdims=True)

    in_specs = [pl.BlockSpec((tr, tc), lambda j, i: (i, cb0 + j)),
                pl.BlockSpec((8, tc), lambda j, i: (jnp.maximum(i * hb - 1, 0), cb0 + j)),
                pl.BlockSpec((taps, tc), lambda j, i: (0, j)),
                pl.BlockSpec((tr, tc), lambda j, i: (i, j)),
                pl.BlockSpec((8, tc), lambda j, i: (jnp.minimum((i + 1) * hb, R // 8 - 1), j))]
    args = [x, x, w, dy, dy]
    if dx_into is None:
        dx_shape, dx_spec, aliases = jax.ShapeDtypeStruct((R, width), BF16), pl.BlockSpec((tr, tc), lambda j, i: (i, j)), {}
    else:
        dx_shape = jax.ShapeDtypeStruct(dx_into.shape, dx_into.dtype)
        dx_spec, aliases = pl.BlockSpec((tr, tc), lambda j, i: (i, cb0 + j)), {5: 0}
        in_specs.append(ANY)
        args.append(dx_into)
    out_shape = [dx_shape, jax.ShapeDtypeStruct((taps, width), F32)]
    out_specs = [dx_spec, pl.BlockSpec((taps, tc), lambda j, i: (0, j))]
    if with_bias:
        out_shape.append(jax.ShapeDtypeStruct((1, width), F32))
        out_specs.append(pl.BlockSpec((1, tc), lambda j, i: (0, j)))
    return _pallas(
        body, out_shape=out_shape, grid=grid, in_specs=in_specs, out_specs=out_specs, input_output_aliases=aliases,
        name=name, compiler_params=_params(grid))(*args)


def _row_mask(cfg, i, tr):
    rows = i * tr + lax.broadcasted_iota(jnp.int32, (tr, 1), 0)
    return (rows >= cfg.front).astype(F32)


def _make_rms_fn(cfg, tr, with_residual):
    def fn(pids, h, g):
        hm = h * _row_mask(cfg, pids[0], tr)
        if with_residual:
            return _rms(hm, g), hm
        return (_rms(hm, g),)
    return fn


def _make_gdn_prep_fn(cfg, tr):
    d, hg = cfg.d, cfg.hg

    def fn(pids, c, tail, alog, dtb):
        cq, ck, cv = c[:, :d], c[:, d:2 * d], c[:, 2 * d:]
        mask = _row_mask(cfg, pids[0], tr)
        j, col = _iota2(LANES, d, 0), _iota2(LANES, d, 1)
        ea = ((col >> 7) == j).astype(F32)
        eb = ((col >> 7) + hg == j).astype(F32)
        ga = _dot(tail, ea, False, False, True)
        gb = _dot(tail, eb, False, False, True)
        al = jnp.sum(_dot(alog, ea, False, False, True), axis=0, keepdims=True)
        db = jnp.sum(_dot(dtb, ea, False, False, True), axis=0, keepdims=True)
        lg = -jnp.exp(al) * _softplus(ga + db) * mask
        beta = _sigmoid(gb) * mask
        sq, sk, sv = _silu(cq), _silu(ck), _silu(cv)
        qs, ks = [], []
        for h in range(hg):
            sl = slice(h * GDN_DK, (h + 1) * GDN_DK)
            qh, kh = sq[:, sl], sk[:, sl]
            qs.append(qh * lax.rsqrt(jnp.sum(qh * qh, axis=-1, keepdims=True) + EPS) * (GDN_DK ** -0.5))
            ks.append(kh * lax.rsqrt(jnp.sum(kh * kh, axis=-1, keepdims=True) + EPS))
        return jnp.concatenate(qs, axis=1), jnp.concatenate(ks, axis=1), sv, beta, lg
    return fn


def _gdn_intra_fn(pids, q, k, v, bB, lB, t_saved=None):
    rows = q.shape[0]
    nb = rows // CHUNK
    q3, k3, v3, b3, l3 = [t.reshape(nb, CHUNK, GDN_DK) for t in (q, k, v, bB, lB)]
    r, c = _iota2(CHUNK, CHUNK, 0), _iota2(CHUNK, CHUNK, 1)
    tril = (r >= c)
    strict = (r > c)
    gcol = _dot(_bcast(tril.astype(F32), nb), l3, False, False, True)
    l64 = l3[:, :, :CHUNK]
    grow = _dot(jnp.ones((nb, CHUNK, CHUNK), F32), l64 * (r <= c).astype(F32)[None], False, False, True)
    diff = gcol[:, :, :CHUNK] - grow
    decay = jnp.where(tril[None], jnp.exp(jnp.where(tril[None], diff, 0.0)), 0.0)
    kb = k3 * b3
    m = jnp.where(strict[None], _dot(kb, k3, False, True) * decay, 0.0)
    t = _tri_inv_raw(m) if t_saved is None else _tri_inv_given(m, t_saved.reshape(nb, CHUNK, CHUNK))
    eg = jnp.exp(gcol)
    u = _dot(t, v3 * b3)
    w = _dot(t, kb * eg)
    attn = _dot(q3, k3, False, True) * decay
    qd = q3 * eg
    glast = jnp.sum(l3, axis=1, keepdims=True)
    kd = k3 * jnp.exp(glast - gcol)
    gl = jnp.exp(glast)
    outs = (u.reshape(rows, GDN_DK), w.reshape(rows, GDN_DK), attn.reshape(1, rows, CHUNK),
            qd.reshape(rows, GDN_DK), kd.reshape(rows, GDN_DK), gl.reshape(1, nb, 1, GDN_DK))
    return outs + (t.reshape(1, rows, CHUNK),) if t_saved is None else outs


def _make_rot_fn(cfg):
    hr = cfg.hr
    half = RET_DK // 2

    def fn(pids, rqk, cos, sin):
        rq, rk = rqk[:, :cfg.d], rqk[:, cfg.d:]

        def rot(t, scale):
            outs = []
            for h in range(hr):
                x1 = t[:, h * RET_DK:h * RET_DK + half]
                x2 = t[:, h * RET_DK + half:(h + 1) * RET_DK]
                outs += [(x1 * cos - x2 * sin) * scale, (x2 * cos + x1 * sin) * scale]
            return jnp.concatenate(outs, axis=1)
        return rot(rq, 1.0), rot(rk, RET_DK ** -0.5)
    return fn


def _make_mix_fn(cfg):
    hg, hr = cfg.hg, cfg.hr

    def fn(pids, oa, ob, pm, gnorm):
        d = cfg.d
        gz, rg, gate_a, gate_b = pm[:, :d], pm[:, d:2 * d], pm[:, 2 * d:3 * d], pm[:, 3 * d:]
        oas = []
        for h in range(hg):
            oh = oa[:, h * GDN_DK:(h + 1) * GDN_DK]
            oas.append(oh * lax.rsqrt(jnp.mean(oh * oh, axis=-1, keepdims=True) + EPS) * gnorm)
        ya = jnp.concatenate(oas, axis=1) * _silu(gz)
        obs = []
        for h in range(hr):
            oh = ob[:, h * RET_DK:(h + 1) * RET_DK]
            obs.append(oh * lax.rsqrt(jnp.mean(oh * oh, axis=-1, keepdims=True) + EPS))
        yb = _silu(rg) * jnp.concatenate(obs, axis=1)
        return (_sigmoid(gate_a) * ya + _sigmoid(gate_b) * yb,)
    return fn


def _act_fn(pids, u):
    f = u.shape[1] // 2
    return (_silu(u[:, :f]) * u[:, f:],)


def _gdn_step(s, u, w, a, qd, kd, gl):
    v_new = u - _dot(w, s)
    o = _dot(qd, s) + _dot(a, v_new)
    s2 = s * gl + _dot(kd, v_new, True, False)
    return s2, o


def _ret_step(s, q, k, v, dm, qdc, kdc, g):
    inner = _dot(_dot(q, k, False, True) * dm, v)
    o = inner + _dot(q * qdc, s)
    s2 = s * g + _dot(k * kdc, v, True, False)
    return s2, o


def _gdn_scan_fwd(cfg, u, w, attn, qd, kd, gl):
    d, hg, nch = cfg.d, cfg.hg, cfg.nch

    def body(u_ref, w_ref, a_ref, qd_ref, kd_ref, gl_ref, o_ref, ss_ref, s_ref):
        n = pl.program_id(0)

        @pl.when(n == 0)
        def _():
            s_ref[...] = jnp.zeros(s_ref.shape, F32)

        for h in range(hg):
            sl = slice(h * GDN_DK, (h + 1) * GDN_DK)
            s = s_ref[h]
            ss_ref[0, h] = s
            s2, o = _gdn_step(s, u_ref[:, sl], w_ref[:, sl], a_ref[h], qd_ref[:, sl], kd_ref[:, sl], gl_ref[h, 0])
            s_ref[h] = s2
            o_ref[:, sl] = o

    row = pl.BlockSpec((CHUNK, d), lambda n: (n, 0))
    return _pallas(
        body,
        out_shape=[jax.ShapeDtypeStruct((cfg.rp, d), F32), jax.ShapeDtypeStruct((nch, hg, GDN_DK, GDN_DK), F32)],
        grid=(nch,),
        in_specs=[row, row, pl.BlockSpec((hg, CHUNK, CHUNK), lambda n: (0, n, 0)), row, row,
                  pl.BlockSpec((hg, 1, 1, GDN_DK), lambda n: (0, n, 0, 0))],
        out_specs=[row, pl.BlockSpec((1, hg, GDN_DK, GDN_DK), lambda n: (n, 0, 0, 0))],
        scratch_shapes=[pltpu.VMEM((hg, GDN_DK, GDN_DK), F32)],
        name="gdn_scan_fwd", compiler_params=_params((nch,)))(u, w, attn, qd, kd, gl)


def _gdn_scan_bwd(cfg, do, u, w, attn, qd, kd, gl, ss):
    d, hg, nch = cfg.d, cfg.hg, cfg.nch

    def body(do_ref, u_ref, w_ref, a_ref, qd_ref, kd_ref, gl_ref, ss_ref,
             du_ref, dw_ref, da_ref, dqd_ref, dkd_ref, dgl_ref, ds_ref):
        n = pl.program_id(0)

        @pl.when(n == 0)
        def _():
            ds_ref[...] = jnp.zeros(ds_ref.shape, F32)

        for h in range(hg):
            sl = slice(h * GDN_DK, (h + 1) * GDN_DK)
            args = (ss_ref[0, h], u_ref[:, sl], w_ref[:, sl], a_ref[h], qd_ref[:, sl], kd_ref[:, sl], gl_ref[h, 0])
            _, vjp_fn = jax.vjp(_gdn_step, *args)
            ds, du, dw, da, dqd, dkd, dgl = vjp_fn((ds_ref[h], do_ref[:, sl]))
            ds_ref[h] = ds
            du_ref[:, sl] = du
            dw_ref[:, sl] = dw
            da_ref[h] = da
            dqd_ref[:, sl] = dqd
            dkd_ref[:, sl] = dkd
            dgl_ref[h, 0] = dgl

    row = pl.BlockSpec((CHUNK, d), lambda n: (nch - 1 - n, 0))
    aspec = pl.BlockSpec((hg, CHUNK, CHUNK), lambda n: (0, nch - 1 - n, 0))
    gspec = pl.BlockSpec((hg, 1, 1, GDN_DK), lambda n: (0, nch - 1 - n, 0, 0))
    rowshape = jax.ShapeDtypeStruct((cfg.rp, d), F32)
    return _pallas(
        body,
        out_shape=[rowshape, rowshape, jax.ShapeDtypeStruct(attn.shape, F32), rowshape, rowshape,
                   jax.ShapeDtypeStruct(gl.shape, F32)],
        grid=(nch,),
        in_specs=[row, row, row, aspec, row, row, gspec,
                  pl.BlockSpec((1, hg, GDN_DK, GDN_DK), lambda n: (nch - 1 - n, 0, 0, 0))],
        out_specs=[row, row, aspec, row, row, gspec],
        scratch_shapes=[pltpu.VMEM((hg, GDN_DK, GDN_DK), F32)],
        name="gdn_scan_bwd", compiler_params=_params((nch,)))(do, u, w, attn, qd, kd, gl, ss)


def _ret_consts(cfg):
    hr = cfg.hr
    lg = np.log(1.0 - 2.0 ** (-5.0 - np.arange(hr, dtype=np.float64)))
    idx = np.arange(CHUNK, dtype=np.float64)
    tril = np.tril(np.ones((CHUNK, CHUNK), dtype=bool))
    dm = np.where(tril[None], np.exp((idx[:, None] - idx[None, :])[None] * lg[:, None, None]), 0.0)
    qdc = np.exp((idx[None, :] + 1.0) * lg[:, None])
    kdc = np.exp((CHUNK - 1.0 - idx[None, :]) * lg[:, None])
    gch = np.exp(CHUNK * lg)
    qdc = np.broadcast_to(qdc[:, :, None], (hr, CHUNK, RET_DK))
    kdc = np.broadcast_to(kdc[:, :, None], (hr, CHUNK, RET_DK))
    gch = np.broadcast_to(gch[:, None, None], (hr, 1, RET_DK))
    return tuple(jnp.asarray(np.ascontiguousarray(t), F32) for t in (dm, qdc, kdc, gch))


def _ret_scan_fwd(cfg, qr, kr, proj, consts):
    d, hr, nch = cfg.d, cfg.hr, cfg.nch
    dm, qdc, kdc, gch = consts

    def body(q_ref, k_ref, v_ref, dm_ref, qdc_ref, kdc_ref, g_ref, o_ref, ss_ref, s_ref):
        n = pl.program_id(0)

        @pl.when(n == 0)
        def _():
            s_ref[...] = jnp.zeros(s_ref.shape, F32)

        for h in range(hr):
            sl = slice(h * RET_DK, (h + 1) * RET_DK)
            s = s_ref[h]
            ss_ref[0, h] = s
            s2, o = _ret_step(s, q_ref[:, sl], k_ref[:, sl], v_ref[:, sl], dm_ref[h], qdc_ref[h], kdc_ref[h], g_ref[h])
            s_ref[h] = s2
            o_ref[:, sl] = o

    row = pl.BlockSpec((CHUNK, d), lambda n: (n, 0))
    return _pallas(
        body,
        out_shape=[jax.ShapeDtypeStruct((cfg.rp, d), F32), jax.ShapeDtypeStruct((nch, hr, RET_DK, RET_DK), F32)],
        grid=(nch,),
        in_specs=[row, row, pl.BlockSpec((CHUNK, d), lambda n: (n, RV_BLOCK)), _full(dm), _full(qdc), _full(kdc), _full(gch)],
        out_specs=[row, pl.BlockSpec((1, hr, RET_DK, RET_DK), lambda n: (n, 0, 0, 0))],
        scratch_shapes=[pltpu.VMEM((hr, RET_DK, RET_DK), F32)],
        name="ret_scan_fwd", compiler_params=_params((nch,)))(qr, kr, proj, dm, qdc, kdc, gch)


def _ret_scan_bwd(cfg, do, qr, kr, proj, consts, ss, dproj):
    d, hr, nch = cfg.d, cfg.hr, cfg.nch
    dm, qdc, kdc, gch = consts

    def body(do_ref, q_ref, k_ref, v_ref, dm_ref, qdc_ref, kdc_ref, g_ref, ss_ref, _, dq_ref, dk_ref, dv_ref, ds_ref):
        n = pl.program_id(0)

        @pl.when(n == 0)
        def _():
            ds_ref[...] = jnp.zeros(ds_ref.shape, F32)

        for h in range(hr):
            sl = slice(h * RET_DK, (h + 1) * RET_DK)
            cs = (dm_ref[h], qdc_ref[h], kdc_ref[h], g_ref[h])
            _, vjp_fn = jax.vjp(lambda s, q, k, v: _ret_step(s, q, k, v, *cs),
                                ss_ref[0, h], q_ref[:, sl], k_ref[:, sl], v_ref[:, sl])
            ds, dq, dk, dv = vjp_fn((ds_ref[h], do_ref[:, sl]))
            ds_ref[h] = ds
            dq_ref[:, sl] = dq
            dk_ref[:, sl] = dk
            dv_ref[:, sl] = dv.astype(dv_ref.dtype)

    row = pl.BlockSpec((CHUNK, d), lambda n: (nch - 1 - n, 0))
    rowshape = jax.ShapeDtypeStruct((cfg.rp, d), F32)
    vspec = pl.BlockSpec((CHUNK, d), lambda n: (nch - 1 - n, RV_BLOCK))
    return _pallas(
        body,
        out_shape=[rowshape, rowshape, jax.ShapeDtypeStruct(dproj.shape, dproj.dtype)],
        grid=(nch,),
        in_specs=[row, row, row, vspec, _full(dm), _full(qdc), _full(kdc), _full(gch),
                  pl.BlockSpec((1, hr, RET_DK, RET_DK), lambda n: (nch - 1 - n, 0, 0, 0)), ANY],
        out_specs=[row, row, vspec], input_output_aliases={9: 2},
        scratch_shapes=[pltpu.VMEM((hr, RET_DK, RET_DK), F32)],
        name="ret_scan_bwd", compiler_params=_params((nch,)))(do, qr, kr, proj, dm, qdc, kdc, gch, ss, dproj)


def _final(cfg, h2, normf, tgt):
    d, tr = cfg.d, cfg.xrow
    nr = cfg.rp // tr

    def body(h_ref, g_ref, t_ref, dh_ref, dg_ref, loss_ref):
        i = pl.program_id(0)
        y, vjp_fn = jax.vjp(_rms, h_ref[...], g_ref[...])
        err = jnp.where(i >= 1, y - t_ref[...], 0.0)
        dh, dg = vjp_fn(err * (1.0 / d))
        dh_ref[...] = dh
        part = jnp.zeros((8, LANES), F32) + 0.5 * jnp.sum(err * err) * (1.0 / d)

        @pl.when(i == 0)
        def _():
            dg_ref[...] = dg
            loss_ref[...] = part

        @pl.when(i > 0)
        def _():
            dg_ref[...] += dg
            loss_ref[...] += part

    return _pallas(
        body,
        out_shape=[jax.ShapeDtypeStruct((cfg.rp, d), F32), jax.ShapeDtypeStruct((1, d), F32),
                   jax.ShapeDtypeStruct((8, LANES), F32)],
        grid=(nr,),
        in_specs=[_rows(tr, d), _full(normf), pl.BlockSpec((tr, d), lambda i: (jnp.maximum(i - 1, 0), 0))],
        out_specs=[_rows(tr, d), pl.BlockSpec((1, d), lambda i: (0, 0)), pl.BlockSpec((8, LANES), lambda i: (0, 0))],
        name="final_loss", compiler_params=_params((nr,)))(h2, normf, tgt)


ANY = pl.BlockSpec(memory_space=pl.ANY)


def _place():
    x, y, c = lax.axis_index("x"), lax.axis_index("y"), lax.axis_index("c")
    others = [(1 - x, y), (x, 1 - y), (1 - x, 1 - y)]
    return x, y, c, others


def _row_tile(rows, cap=256):
    return max(t for t in range(16, min(rows, cap) + 1, 16) if rows % t == 0)


def _all_gather_chips(ws):
    n = len(ws)

    def body(*refs):
        w_refs, o_refs = refs[:n], refs[n:2 * n]
        send_sems, recv_sems, local_sems = refs[2 * n:]
        x, y, c, others = _place()
        me = 2 * x + y
        sibling = (x, y, 1 - c)
        chips = [2 * px + py for px, py in others]
        halves = [w.shape[0] // 2 for w in ws]

        def piece(a, chip, core):
            return o_refs[a].at[chip, pl.ds(core * halves[a], halves[a]), :]

        def copy(a, k, src, chip, core, to):
            return pltpu.make_async_remote_copy(src_ref=src, dst_ref=piece(a, chip, core), send_sem=send_sems.at[6 * a + k],
                                                recv_sem=recv_sems.at[6 * a + k], device_id=to, device_id_type=MESH)

        mine = [pltpu.make_async_copy(w_refs[a], o_refs[a].at[me], local_sems.at[a]) for a in range(n)]
        for cp in mine:
            cp.start()
        first = [copy(a, j, w_refs[a].at[pl.ds(c * halves[a], halves[a]), :], me, c, (px, py, c))
                 for j, (px, py) in enumerate(others) for a in range(n)]
        for cp in first:
            cp.start()
        passed = []
        for j in range(3):
            for a in range(n):
                copy(a, j, piece(a, chips[j], c), chips[j], c, (x, y, c)).wait_recv()
                cp = copy(a, 3 + j, piece(a, chips[j], c), chips[j], c, sibling)
                cp.start()
                passed.append(cp)
        for j in range(3):
            for a in range(n):
                copy(a, 3 + j, piece(a, chips[j], 1 - c), chips[j], 1 - c, (x, y, c)).wait_recv()
        for cp in first + passed:
            cp.wait_send()
        for cp in mine:
            cp.wait()

    return _pallas(
        body, out_shape=[jax.ShapeDtypeStruct((N_CHIPS,) + w.shape, w.dtype) for w in ws],
        in_specs=[ANY] * n, out_specs=[ANY] * n,
        scratch_shapes=[pltpu.SemaphoreType.DMA((6 * n,)), pltpu.SemaphoreType.DMA((6 * n,)), pltpu.SemaphoreType.DMA((n,))],
        name="weights_all_gather")(*ws)


def _pair_exchange(gs):
    n = len(gs)

    def body(*refs):
        g_refs, o_refs = refs[:n], refs[n:2 * n]
        send_sems, recv_sems = refs[2 * n:]
        x, y, c, _ = _place()
        cps = []
        for a in range(n):
            half = gs[a].shape[1] // 2
            cp = pltpu.make_async_remote_copy(
                src_ref=g_refs[a].at[:, pl.ds((1 - c) * half, half), :], dst_ref=o_refs[a], send_sem=send_sems.at[a],
                recv_sem=recv_sems.at[a], device_id=(x, y, 1 - c), device_id_type=MESH)
            cp.start()
            cps.append(cp)
        for cp in cps:
            cp.wait()

    return _pallas(
        body, out_shape=[jax.ShapeDtypeStruct((N_CHIPS, g.shape[1] // 2, g.shape[2]), g.dtype) for g in gs],
        in_specs=[ANY] * n, out_specs=[ANY] * n,
        scratch_shapes=[pltpu.SemaphoreType.DMA((n,)), pltpu.SemaphoreType.DMA((n,))], name="grad_pair_exchange")(*gs)


def _pair_sum(name, g, recv, cidx):
    half, cols = recv.shape[1], recv.shape[2]
    tr = _row_tile(half)
    nblk = half // tr

    def body(c_ref, g_ref, r_ref, o_ref):
        o_ref[...] = (g_ref[...] + r_ref[...]).astype(o_ref.dtype)

    grid_spec = pltpu.PrefetchScalarGridSpec(
        num_scalar_prefetch=1, grid=(N_CHIPS, nblk),
        in_specs=[pl.BlockSpec((1, tr, cols), lambda s, i, c: (s, c[0] * nblk + i, 0)),
                  pl.BlockSpec((1, tr, cols), lambda s, i, c: (s, i, 0))],
        out_specs=pl.BlockSpec((1, tr, cols), lambda s, i, c: (s, i, 0)))
    return _pallas(
        body, out_shape=jax.ShapeDtypeStruct((N_CHIPS, half, cols), BF16), grid_spec=grid_spec,
        name=name, compiler_params=_params((N_CHIPS, nblk)))(cidx, g, recv)


def _chip_exchange(parts):
    n = len(parts)

    def body(*refs):
        p_refs, o_refs = refs[:n], refs[n:2 * n]
        send_sems, recv_sems, local_sems = refs[2 * n:]
        x, y, c, others = _place()
        me = 2 * x + y

        def copy(a, j, src_chip, dst_chip):
            px, py = others[j]
            return pltpu.make_async_remote_copy(
                src_ref=p_refs[a].at[src_chip], dst_ref=o_refs[a].at[dst_chip], send_sem=send_sems.at[3 * a + j],
                recv_sem=recv_sems.at[3 * a + j], device_id=(px, py, c), device_id_type=MESH)

        mine = [pltpu.make_async_copy(p_refs[a].at[me], o_refs[a].at[me], local_sems.at[a]) for a in range(n)]
        for cp in mine:
            cp.start()
        sends = [copy(a, j, 2 * px + py, me) for j, (px, py) in enumerate(others) for a in range(n)]
        for cp in sends:
            cp.start()
        for j, (px, py) in enumerate(others):
            for a in range(n):
                copy(a, j, me, 2 * px + py).wait_recv()
        for cp in sends:
            cp.wait_send()
        for cp in mine:
            cp.wait()

    return _pallas(
        body, out_shape=[jax.ShapeDtypeStruct(p.shape, p.dtype) for p in parts], in_specs=[ANY] * n, out_specs=[ANY] * n,
        scratch_shapes=[pltpu.SemaphoreType.DMA((3 * n,)), pltpu.SemaphoreType.DMA((3 * n,)), pltpu.SemaphoreType.DMA((n,))],
        name="grad_chip_exchange")(*parts)


def _chip_sum(name, slots):
    half, cols = slots.shape[1], slots.shape[2]
    tr = _row_tile(half)

    def body(s_ref, o_ref):
        v = s_ref[...].astype(F32)
        o_ref[...] = ((v[0] + v[1]) + v[2]) + v[3]

    return _pallas(
        body, out_shape=jax.ShapeDtypeStruct((half, cols), F32), grid=(half // tr,),
        in_specs=[pl.BlockSpec((N_CHIPS, tr, cols), lambda i: (0, i, 0))],
        out_specs=pl.BlockSpec((tr, cols), lambda i: (i, 0)),
        name=name, compiler_params=_params((half // tr,)))(slots)


def _pair_gather(fins):
    n = len(fins)

    def body(*refs):
        f_refs, o_refs = refs[:n], refs[n:2 * n]
        send_sems, recv_sems, local_sems = refs[2 * n:]
        x, y, c, _ = _place()

        def copy(a, core):
            half = fins[a].shape[0]
            return pltpu.make_async_remote_copy(
                src_ref=f_refs[a], dst_ref=o_refs[a].at[pl.ds(core * half, half), :], send_sem=send_sems.at[a],
                recv_sem=recv_sems.at[a], device_id=(x, y, 1 - c), device_id_type=MESH)

        mine = [pltpu.make_async_copy(f_refs[a], o_refs[a].at[pl.ds(c * fins[a].shape[0], fins[a].shape[0]), :],
                                      local_sems.at[a]) for a in range(n)]
        for cp in mine:
            cp.start()
        sends = [copy(a, c) for a in range(n)]
        for cp in sends:
            cp.start()
        for a in range(n):
            copy(a, 1 - c).wait_recv()
        for cp in sends:
            cp.wait_send()
        for cp in mine:
            cp.wait()

    return _pallas(
        body, out_shape=[jax.ShapeDtypeStruct((2 * f.shape[0], f.shape[1]), f.dtype) for f in fins],
        in_specs=[ANY] * n, out_specs=[ANY] * n,
        scratch_shapes=[pltpu.SemaphoreType.DMA((n,)), pltpu.SemaphoreType.DMA((n,)), pltpu.SemaphoreType.DMA((n,))],
        name="grad_pair_gather")(*fins)


def _adamw(name, w, g, m, v):
    R, cols = w.shape
    tr = _row_tile(R) if R >= 16 else R
    c1 = 1.0 - ADAM_B1 ** ADAM_STEP
    c2 = 1.0 - ADAM_B2 ** ADAM_STEP

    def body(w_ref, g_ref, m_ref, v_ref, d_ref, nm_ref, nv_ref):
        gv = g_ref[...]
        nm = ADAM_B1 * m_ref[...] + (1.0 - ADAM_B1) * gv
        nv = ADAM_B2 * v_ref[...] + (1.0 - ADAM_B2) * (gv * gv)
        d_ref[...] = -ADAM_LR * ((nm / c1) / (jnp.sqrt(nv / c2) + ADAM_EPS) + ADAM_WD * w_ref[...])
        nm_ref[...] = nm
        nv_ref[...] = nv

    spec = pl.BlockSpec((tr, cols), lambda i: (i, 0))
    shape = jax.ShapeDtypeStruct((R, cols), F32)
    return _pallas(
        body, out_shape=[shape, shape, shape], grid=(R // tr,), in_specs=[spec] * 4, out_specs=[spec] * 3,
        name=name, compiler_params=_params((R // tr,)))(w, g, m, v)


PARAMS = (("meta", 1), ("norm1", None), ("w_in", 2), ("gdn_conv_w", 2), ("gdn_a_log", None), ("gdn_dt_bias", None),
          ("gdn_norm", None), ("w_out", 1), ("norm2", None), ("w_ffn_up", 2), ("ffn_conv_w", 2), ("ffn_conv_b", None),
          ("w_ffn_down", 1), ("norm_f", None))
BIG = ("w_in", "w_out", "w_ffn_up", "w_ffn_down")
PACK_ALIGN = 1024
PACK_ROWS_ALIGN = 32


def _pack(arrs, dtype):
    parts, total = [], 0
    for a in arrs:
        f = a.reshape(-1).astype(dtype)
        pad = (-f.shape[0]) % PACK_ALIGN
        parts.append(jnp.pad(f, (0, pad)) if pad else f)
        total += f.shape[0] + pad
    rows = total // LANES
    rpad = (-rows) % PACK_ROWS_ALIGN
    if rpad:
        parts.append(jnp.zeros((rpad * LANES,), dtype))
    return jnp.concatenate(parts).reshape(rows + rpad, LANES)


def _unpack(buf, shapes):
    flat = buf.reshape(-1)
    outs, off = [], 0
    for s in shapes:
        n = int(np.prod(s))
        outs.append(flat[off:off + n].reshape(s))
        off += n + (-n) % PACK_ALIGN
    return outs


def _split4(a, axis):
    n = a.shape[axis] // N_CHIPS
    return [lax.slice_in_dim(a, s * n, (s + 1) * n, axis=axis) for s in range(N_CHIPS)]


PROJ_ORDER = (3, 7, 8, 9, 0, 1, 2, 6, 4, 5)


def _reorder_w_in(w, cfg):
    d, hg = cfg.d, cfg.hg

    def block(k):
        off = k * d + (2 * hg if k >= 4 else 0)
        return w[:, off:off + d]

    tail = jnp.pad(w[:, 4 * d:4 * d + 2 * hg], ((0, 0), (0, LANES - 2 * hg)))
    return jnp.concatenate([block(k) for k in PROJ_ORDER] + [tail], axis=1)


def _restore_w_in(wr, cfg):
    d, hg = cfg.d, cfg.hg
    at = {k: i for i, k in enumerate(PROJ_ORDER)}
    block = lambda k: wr[:, at[k] * d:(at[k] + 1) * d]
    return jnp.concatenate([block(k) for k in range(4)] + [wr[:, 10 * d:10 * d + 2 * hg]] +
                           [block(k) for k in range(4, 10)], axis=1)


def _step(cfg, x, tgt, shard, m_shard, v_shard):
    d, hg, hr, dff, rp, tr, tm = cfg.d, cfg.hg, cfg.hr, cfg.dff, cfg.rp, cfg.tr, cfg.tm
    nrow = rp // tr
    assert cfg.tf * N_CHIPS == 2 * dff and cfg.din % N_CHIPS == 0
    cidx = lax.axis_index("c").astype(jnp.int32).reshape(1)

    axis = dict(PARAMS)
    small = ("meta", "gdn_conv_w", "ffn_conv_w")
    small_shapes = [shard[n].shape for n in small]
    gathered = _all_gather_chips([shard[n][0].astype(BF16) for n in BIG] + [_pack([shard[n] for n in small], F32)])
    g_in, g_out, g_up, g_down, g_small = gathered
    w_in_r = _reorder_w_in(jnp.concatenate([g_in[s] for s in range(N_CHIPS)], axis=1), cfg)
    w_out = g_out.reshape(d, d)
    w_up = g_up
    w_up_t = jnp.swapaxes(g_up, 1, 2).reshape(2 * dff, d)
    w_down = g_down.reshape(dff, d)
    per_chip = [_unpack(g_small[s], small_shapes) for s in range(N_CHIPS)]
    full = {n: jnp.concatenate([per_chip[s][k] for s in range(N_CHIPS)], axis=axis[n]) for k, n in enumerate(small)}
    meta = full["meta"]
    gconv_w = full["gdn_conv_w"][0]
    fconv_w = full["ffn_conv_w"][0]
    norm1, norm2, gnorm = shard["norm1"], shard["norm2"], shard["gdn_norm"]
    normf = shard["norm_f"].reshape(1, d)
    fconv_b = shard["ffn_conv_b"]
    alog = jnp.pad(shard["gdn_a_log"], ((0, 7), (0, LANES - hg)))
    dtb = jnp.pad(shard["gdn_dt_bias"], ((0, 7), (0, LANES - hg)))

    h0 = jnp.concatenate([jnp.zeros((cfg.front, d), F32), meta, x], axis=0)
    pos = jnp.arange(rp, dtype=F32) - float(cfg.front)
    half = RET_DK // 2
    inv = 1.0 / (ROPE_BASE ** (jnp.arange(half, dtype=F32) / half))
    ang = pos[:, None] * inv[None, :]
    cos, sin = jnp.cos(ang), jnp.sin(ang)
    rconsts = _ret_consts(cfg)

    rms_f = _make_rms_fn(cfg, tr, False)
    rms_b = _make_rms_fn(cfg, tr, True)
    rowshape = jax.ShapeDtypeStruct((rp, d), F32)
    rspec = _rows(tr, d)

    def rms_fwd(name, h, g):
        return _stage_fwd(name, rms_f, (nrow,), [In(h, rspec), In(g, _full(g))],
                          [jax.ShapeDtypeStruct((rp, d), BF16)], [rspec])[0]

    tn_in = cfg.pw // 9 if cfg.pw % (9 * LANES) == 0 else LANES
    hn1 = rms_fwd("rms1_fwd", h0, norm1)
    proj = _mm("proj_fwd", hn1, w_in_r, tm=tm, tn=tn_in, tk=d)
    cqkv = _conv_fwd("gdn_conv_fwd", proj, CONV_COL * d, gconv_w, None, taps=GDN_CONV, width=3 * d, tr=tr, tc=d)
    tail_spec = _rows(tr, LANES, TAIL_COL * d // LANES)
    prep_fn = _make_gdn_prep_fn(cfg, tr)

    def prep_ins(dproj=None):
        return [In(cqkv, _rows(tr, 3 * d), F32), In(proj, tail_spec, BF16, galias=dproj, gshape=(rp, cfg.pw)),
                In(alog, _full(alog), F32, True), In(dtb, _full(dtb), F32, True)]

    qn, kn, vv, bB, lB = _stage_fwd("gdn_prep_fwd", prep_fn, (nrow,), prep_ins(), [rowshape] * 5, [rspec] * 5)

    trg = cfg.nb * CHUNK
    gi_grid = (rp // trg, hg)
    hspec = pl.BlockSpec((trg, GDN_DK), lambda i, h: (i, h))
    aspec = pl.BlockSpec((1, trg, CHUNK), lambda i, h: (h, i, 0))
    gspec = pl.BlockSpec((1, cfg.nb, 1, GDN_DK), lambda i, h: (h, i, 0, 0))
    intra_ins = [In(t, hspec, F32) for t in (qn, kn, vv, bB, lB)]
    ashape = jax.ShapeDtypeStruct((hg, rp, CHUNK), F32)
    intra_shapes = [rowshape, rowshape, ashape, rowshape, rowshape, jax.ShapeDtypeStruct((hg, cfg.nch, 1, GDN_DK), F32), ashape]
    intra_specs = [hspec, hspec, aspec, hspec, hspec, gspec, aspec]
    gu, gw, gattn, gqd, gkd, ggl, gtinv = _stage_fwd("gdn_intra_fwd", _gdn_intra_fn, gi_grid, intra_ins, intra_shapes,
                                                     intra_specs)
    oa, gss = _gdn_scan_fwd(cfg, gu, gw, gattn, gqd, gkd, ggl)

    rot_fn = _make_rot_fn(cfg)

    def rot_ins(dproj=None):
        return [In(proj, _rows(tr, 2 * d, ROT_COL // 2), BF16, galias=dproj, gshape=(rp, cfg.pw)),
                In(cos, _rows(tr, half)), In(sin, _rows(tr, half))]

    qr, kr = _stage_fwd("rot_fwd", rot_fn, (nrow,), rot_ins(), [rowshape] * 2, [rspec] * 2)
    ob, rss = _ret_scan_fwd(cfg, qr, kr, proj, rconsts)

    mix_fn = _make_mix_fn(cfg)
    mix_ins = [In(oa, rspec, F32), In(ob, rspec, F32), In(proj, _rows(tr, 4 * d, MIX_COL // 4), BF16, gshape=(rp, cfg.pw)),
               In(gnorm, _full(gnorm), F32, True)]
    ymix = _stage_fwd("mix_fwd", mix_fn, (nrow,), mix_ins, [jax.ShapeDtypeStruct((rp, d), BF16)], [rspec])[0]
    h1 = _mm("out_proj_fwd", ymix, w_out, tm=tm, tn=d, tk=d, add=h0)

    hn2 = rms_fwd("rms2_fwd", h1, norm2)
    up = _mm("ffn_up_fwd", hn2, w_up, tm=tm, tn=cfg.tf, tk=d)
    uc = _conv_fwd("ffn_conv_fwd", up, 0, fconv_w, fconv_b, taps=FFN_CONV, width=2 * dff, tr=tr, tc=cfg.tf)
    tra = tr // 2
    act_ins = [In(uc, _rows(tra, 2 * dff), F32)]
    act_spec = _rows(tra, dff)
    act = _stage_fwd("ffn_act_fwd", _act_fn, (rp // tra,), act_ins, [jax.ShapeDtypeStruct((rp, dff), BF16)], [act_spec])[0]
    h2 = _mm("ffn_down_fwd", act, w_down, tm=tm, tn=d, tk=cfg.tf, add=h1)

    dh2, g_normf, loss_blk = _final(cfg, h2, normf, tgt)
    loss = lax.psum(loss_blk[0, 0], ("x", "y", "c"))

    g_w_down = _mm_tn("ffn_down_dw", act, dh2, tr=tm, tka=cfg.tf, tn=d)
    dact = _mm("ffn_down_dx", dh2, w_down.T, tm=tm, tn=cfg.tf, tk=d)
    duc, = _stage_bwd("ffn_act_bwd", _act_fn, (rp // tra,), act_ins, [(dact, act_spec)])
    dup, g_fconv_w, g_fconv_b = _conv_bwd("ffn_conv_bwd", up, 0, fconv_w, duc, taps=FFN_CONV, width=2 * dff,
                                          tr=tr, tc=cfg.tf, with_bias=True)
    g_w_up = _mm_tn("ffn_up_dw", hn2, dup, tr=tm, tka=d, tn=cfg.tf, blocked=True)
    dhn2 = _mm("ffn_up_dx", dup, w_up_t, tm=tm, tn=d, tk=cfg.tf)

    def rms_bwd(name, h, g, dhn, dres):
        ins = [In(h, rspec, F32), In(g, _full(g), F32, True)]
        return _stage_bwd(name, rms_b, (nrow,), ins, [(dhn, rspec), (dres, rspec)])

    dh1, g_norm2 = rms_bwd("rms2_bwd", h1, norm2, dhn2, dh2)
    g_w_out = _mm_tn("out_proj_dw", ymix, dh1, tr=tm, tka=d, tn=d)
    dymix = _mm("out_proj_dx", dh1, w_out.T, tm=tm, tn=d, tk=d)
    doa, dob, dproj, g_gnorm = _stage_bwd("mix_bwd", mix_fn, (nrow,), mix_ins, [(dymix, rspec)])

    dqr, dkr, dproj = _ret_scan_bwd(cfg, dob, qr, kr, proj, rconsts, rss, dproj)
    dproj, = _stage_bwd("rot_bwd", rot_fn, (nrow,), rot_ins(dproj), [(dqr, rspec), (dkr, rspec)])

    dgu, dgw, dgattn, dgqd, dgkd, dggl = _gdn_scan_bwd(cfg, doa, gu, gw, gattn, gqd, gkd, ggl, gss)
    intra_cots = [(dgu, hspec), (dgw, hspec), (dgattn, aspec), (dgqd, hspec), (dgkd, hspec), (dggl, gspec)]
    dqn, dkn, dvv, dbB, dlB = _stage_bwd("gdn_intra_bwd", _gdn_intra_fn, gi_grid, intra_ins + [In(gtinv, aspec)], intra_cots)
    dcqkv, dproj, g_alog, g_dtb = _stage_bwd(
        "gdn_prep_bwd", prep_fn, (nrow,), prep_ins(dproj), [(t, rspec) for t in (dqn, dkn, dvv, dbB, dlB)])
    dproj, g_gconv_w = _conv_bwd("gdn_conv_bwd", proj, CONV_COL * d, gconv_w, dcqkv, taps=GDN_CONV, width=3 * d,
                                 tr=tr, tc=d, with_bias=False, dx_into=dproj)
    g_w_in_r = _mm_tn("proj_dw", hn1, dproj, tr=tm, tka=d, tn=tn_in)
    dhn1 = _mm("proj_dx", dproj, w_in_r.T, tm=tm, tn=d, tk=tn_in)
    dh0, g_norm1 = rms_bwd("rms1_bwd", h0, norm1, dhn1, dh1)

    grad_x = dh0[cfg.xrow:]
    small_grads = {
        "meta": dh0[cfg.front:cfg.xrow], "norm1": g_norm1, "gdn_conv_w": g_gconv_w[None],
        "gdn_a_log": g_alog[0:1, :hg], "gdn_dt_bias": g_dtb[0:1, :hg], "gdn_norm": g_gnorm, "norm2": g_norm2,
        "ffn_conv_w": g_fconv_w[None], "ffn_conv_b": g_fconv_b, "norm_f": g_normf.reshape(d),
    }

    small_names = [n for n, _ in PARAMS if n not in BIG]
    g_small = jnp.stack([_pack([small_grads[n] if axis[n] is None else _split4(small_grads[n], axis[n])[s]
                                for n in small_names], F32) for s in range(N_CHIPS)])
    g_in4 = _restore_w_in(g_w_in_r, cfg).reshape(d, N_CHIPS, cfg.din // N_CHIPS).transpose(1, 0, 2)
    groups = [g_in4, g_w_out.reshape(N_CHIPS, d // N_CHIPS, d), g_w_up, g_w_down.reshape(N_CHIPS, dff // N_CHIPS, d), g_small]
    tags = ["w_in", "w_out", "w_ffn_up", "w_ffn_down", "small"]
    recvs = _pair_exchange(groups)
    parts = [_pair_sum("grad_pair_sum_" + t, g, r, cidx) for t, g, r in zip(tags, groups, recvs)]
    slots = _chip_exchange(parts)
    fins = [_chip_sum("grad_chip_sum_" + t, s) for t, s in zip(tags, slots)]
    gsums = _pair_gather(fins)

    def flat2(a):
        return a.reshape(-1, a.shape[-1])

    outs = {}
    for t, gsum in zip(BIG, gsums[:4]):
        res = _adamw("adamw_" + t, flat2(shard[t]), gsum, flat2(m_shard[t]), flat2(v_shard[t]))
        outs[t] = [r.reshape(shard[t].shape) for r in (gsum, *res)]
    small_shapes_all = [shard[n].shape for n in small_names]
    pk = lambda src: _pack([src[n] for n in small_names], F32)
    res = _adamw("adamw_small", pk(shard), gsums[4], pk(m_shard), pk(v_shard))
    for k, r in enumerate((gsums[4], *res)):
        for n, a in zip(small_names, _unpack(r, small_shapes_all)):
            outs.setdefault(n, [None] * 4)[k] = a
    names = [n for n, _ in PARAMS]
    return (loss, grad_x[None], *[outs[n][k] for k in range(4) for n in names])


def kernel(x, meta, norm1, w_in, gdn_conv_w, gdn_a_log, gdn_dt_bias, gdn_norm, w_out, norm2, w_ffn_up, ffn_conv_w, ffn_conv_b, w_ffn_down, norm_f, loss_target, m_meta, m_norm1, m_w_in, m_gdn_conv_w, m_gdn_a_log, m_gdn_dt_bias, m_gdn_norm, m_w_out, m_norm2, m_w_ffn_up, m_ffn_conv_w, m_ffn_conv_b, m_w_ffn_down, m_norm_f, v_meta, v_norm1, v_w_in, v_gdn_conv_w, v_gdn_a_log, v_gdn_dt_bias, v_gdn_norm, v_w_out, v_norm2, v_w_ffn_up, v_ffn_conv_w, v_ffn_conv_b, v_w_ffn_down, v_norm_f):
    names = [n for n, _ in PARAMS]
    shard = dict(zip(names, (meta, norm1, w_in, gdn_conv_w, gdn_a_log, gdn_dt_bias, gdn_norm, w_out, norm2, w_ffn_up,
                             ffn_conv_w, ffn_conv_b, w_ffn_down, norm_f)))
    m_shard = dict(zip(names, (m_meta, m_norm1, m_w_in, m_gdn_conv_w, m_gdn_a_log, m_gdn_dt_bias, m_gdn_norm, m_w_out,
                               m_norm2, m_w_ffn_up, m_ffn_conv_w, m_ffn_conv_b, m_w_ffn_down, m_norm_f)))
    v_shard = dict(zip(names, (v_meta, v_norm1, v_w_in, v_gdn_conv_w, v_gdn_a_log, v_gdn_dt_bias, v_gdn_norm, v_w_out,
                               v_norm2, v_w_ffn_up, v_ffn_conv_w, v_ffn_conv_b, v_w_ffn_down, v_norm_f)))
    return _step(REAL, x[0], loss_target[0], shard, m_shard, v_shard)
```

```python
import functools
from typing import NamedTuple

import numpy as np
import jax
import jax.numpy as jnp
from jax import lax
from jax.experimental import pallas as pl
from jax.experimental.pallas import tpu as pltpu

F32 = jnp.float32
BF16 = jnp.bfloat16
EPS = 1e-6
CHUNK = 64
GDN_DK = 128
RET_DK = 256
GDN_CONV = 4
FFN_CONV = 3
ROPE_BASE = 10000.0
LANES = 128
N_CHIPS = 4
ADAM_LR, ADAM_B1, ADAM_B2, ADAM_EPS, ADAM_WD, ADAM_STEP = 0.001, 0.9, 0.999, 1e-08, 0.01, 10
MIX_COL, CONV_COL, RV_BLOCK, ROT_COL, TAIL_COL = 0, 4, 7, 8, 10
MESH = pl.DeviceIdType.MESH
VMEM_LIMIT = 56 * 1024 * 1024


class Cfg(NamedTuple):
    d: int
    seq: int
    n_meta: int
    dff: int
    tr: int
    nb: int
    tm: int
    tf: int
    sc: int

    @property
    def hg(self): return self.d // GDN_DK
    @property
    def hr(self): return self.d // RET_DK
    @property
    def L(self): return self.n_meta + self.seq
    @property
    def rp(self): return -(-self.L // 256) * 256
    @property
    def front(self): return self.rp - self.L
    @property
    def xrow(self): return self.rp - self.seq
    @property
    def nch(self): return self.rp // CHUNK
    @property
    def pw(self): return 10 * self.d + LANES
    @property
    def din(self): return 10 * self.d + 2 * self.hg


REAL = Cfg(d=1024, seq=8192, n_meta=16, dff=2816, tr=256, nb=12, tm=768, tf=1408, sc=4)


def _pallas(body, **kw):
    return pl.pallas_call(body, **kw)


def _sigmoid(x):
    return 1.0 / (1.0 + jnp.exp(-x))


def _silu(x):
    return x * _sigmoid(x)


def _softplus(x):
    return jnp.maximum(x, 0.0) + jnp.log(1.0 + jnp.exp(-jnp.abs(x)))


def _raw_dot(a, b, ta, tb, hi):
    if not hi:
        a = a.astype(BF16)
        b = b.astype(BF16)
    nbatch = a.ndim - 2
    ca = a.ndim - 2 if ta else a.ndim - 1
    cb = b.ndim - 1 if tb else b.ndim - 2
    batch = tuple(range(nbatch))
    return lax.dot_general(a, b, (((ca,), (cb,)), (batch, batch)),
                           precision=lax.Precision.HIGHEST if hi else None,
                           preferred_element_type=F32)


@functools.partial(jax.custom_vjp, nondiff_argnums=(2, 3, 4))
def _dot_p(a, b, ta, tb, hi):
    return _raw_dot(a, b, ta, tb, hi)


def _dot(a, b, ta=False, tb=False, hi=False):
    return _dot_p(a, b, ta, tb, hi)


def _dot_fwd(a, b, ta, tb, hi):
    return _raw_dot(a, b, ta, tb, hi), (a, b)


def _dot_bwd(ta, tb, hi, res, g):
    a, b = res
    if not ta and not tb:
        da, db = _dot(g, b, False, True, hi), _dot(a, g, True, False, hi)
    elif not ta and tb:
        da, db = _dot(g, b, False, False, hi), _dot(g, a, True, False, hi)
    elif ta and not tb:
        da, db = _dot(b, g, False, True, hi), _dot(a, g, False, False, hi)
    else:
        raise NotImplementedError
    return da.astype(a.dtype), db.astype(b.dtype)


_dot_p.defvjp(_dot_fwd, _dot_bwd)


def _iota2(n, m, axis):
    return lax.broadcasted_iota(jnp.int32, (n, m), axis)


def _bcast(mat, nb):
    return jnp.broadcast_to(mat[None], (nb,) + mat.shape)


def _split(a):
    hi = a.astype(BF16)
    return hi, (a - hi.astype(F32)).astype(BF16)


def _dot3(a, b, ta=False, tb=False):
    ah, al = _split(a)
    bh, bl = _split(b)
    return _raw_dot(ah, bh, ta, tb, False) + (_raw_dot(ah, bl, ta, tb, False) + _raw_dot(al, bh, ta, tb, False))


def _split3(a):
    a0 = a.astype(BF16)
    r1 = a - a0.astype(F32)
    a1 = r1.astype(BF16)
    return a0, a1, (r1 - a1.astype(F32)).astype(BF16)


@functools.partial(jax.custom_vjp, nondiff_argnums=(2,))
def _dot_sel(a, e, te):
    eb = e.astype(BF16)
    p0, p1, p2 = (_raw_dot(p, eb, False, te, False) for p in _split3(a))
    return p0 + (p1 + p2)


def _dot_sel_fwd(a, e, te):
    return _dot_sel(a, e, te), e


def _dot_sel_bwd(te, e, g):
    return _dot_sel(g, e, not te), jnp.zeros_like(e)


_dot_sel.defvjp(_dot_sel_fwd, _dot_sel_bwd)


def _tri_inv_raw(m):
    nb = m.shape[0]
    r, c = _iota2(CHUNK, CHUNK, 0), _iota2(CHUNK, CHUNK, 1)
    t = _bcast((r == c).astype(F32), nb)
    b = 1
    while b < CHUNK:
        sh = b.bit_length() - 1
        off = ((r >> (sh + 1)) == (c >> (sh + 1))) & ((r >> sh) != (c >> sh)) & (r > c)
        cl = jnp.where(off[None], m, 0.0)
        t = t - _dot3(_dot3(t, cl), t)
        b *= 2
    return t


@jax.custom_vjp
def _tri_inv_given(m, t):
    return t


def _tri_inv_fwd(m, t):
    return t, t


def _tri_inv_bwd(t, g):
    return -_dot3(_dot3(t, g, True, False), t, False, True), jnp.zeros_like(t)


_tri_inv_given.defvjp(_tri_inv_fwd, _tri_inv_bwd)


def _rms(h, g):
    return h * lax.rsqrt(jnp.mean(h * h, axis=-1, keepdims=True) + EPS) * g


class In(NamedTuple):
    arr: jax.Array
    spec: pl.BlockSpec
    grad: object = None
    acc: bool = False
    gshape: object = None
    gspec: object = None
    galias: object = None


def _params(grid):
    sem = ("arbitrary",) * len(grid)
    return pltpu.CompilerParams(dimension_semantics=sem, vmem_limit_bytes=VMEM_LIMIT)


def _stage_fwd(name, fn, grid, ins, out_shapes, out_specs):
    n_in = len(ins)

    def body(*refs):
        pids = tuple(pl.program_id(k) for k in range(len(grid)))
        vals = [r[...] for r in refs[:n_in]]
        outs = fn(pids, *vals)
        for o_ref, o in zip(refs[n_in:], outs):
            o_ref[...] = o.reshape(o_ref.shape).astype(o_ref.dtype)

    return _pallas(
        body, out_shape=out_shapes, grid=grid, in_specs=[i.spec for i in ins],
        out_specs=out_specs, name=name, compiler_params=_params(grid))(*[i.arr for i in ins])


def _stage_bwd(name, fn, grid, ins, cots):
    n_in, n_ct = len(ins), len(cots)
    didx = [k for k, i in enumerate(ins) if i.grad is not None]
    aliased = [(o, ins[k].galias) for o, k in enumerate(didx) if ins[k].galias is not None]
    n_al = len(aliased)

    def body(*refs):
        pids = tuple(pl.program_id(k) for k in range(len(grid)))
        vals = [r[...] for r in refs[:n_in]]
        ct_refs = refs[n_in:n_in + n_ct]
        g_refs = refs[n_in + n_ct + n_al:]

        def f(*dv):
            merged = list(vals)
            for k, v in zip(didx, dv):
                merged[k] = v
            return tuple(fn(pids, *merged))

        outs, vjp_fn = jax.vjp(f, *[vals[k].astype(F32) for k in didx])
        cts = tuple(c[...].reshape(o.shape).astype(F32) for c, o in zip(ct_refs, outs))
        grads = vjp_fn(cts)
        first = functools.reduce(jnp.logical_and, [p == 0 for p in pids])
        for k, g_ref, g in zip(didx, g_refs, grads):
            if ins[k].acc:
                @pl.when(first)
                def _(g_ref=g_ref):
                    g_ref[...] = jnp.zeros(g_ref.shape, g_ref.dtype)
                g_ref[...] += g.reshape(g_ref.shape).astype(g_ref.dtype)
            else:
                g_ref[...] = g.reshape(g_ref.shape).astype(g_ref.dtype)

    out_shapes = [jax.ShapeDtypeStruct(ins[k].gshape or ins[k].arr.shape, ins[k].grad) for k in didx]
    out_specs = [ins[k].gspec or ins[k].spec for k in didx]
    return _pallas(
        body, out_shape=out_shapes, grid=grid,
        in_specs=[i.spec for i in ins] + [c[1] for c in cots] + [ANY] * n_al, out_specs=out_specs,
        input_output_aliases={n_in + n_ct + a: o for a, (o, _) in enumerate(aliased)},
        name=name, compiler_params=_params(grid))(*[i.arr for i in ins], *[c[0] for c in cots], *[a for _, a in aliased])


def _full(arr):
    nd = arr.ndim
    return pl.BlockSpec(arr.shape, lambda *p: (0,) * nd)


def _rows(tr, width, blk=0):
    return pl.BlockSpec((tr, width), lambda i: (i, blk))


def _mm(name, a, b, *, tm, tn, tk, out_dtype=F32, add=None):
    M, K = a.shape
    N = b.shape[1] if b.ndim == 2 else b.shape[0] * b.shape[2]
    nk = K // tk
    grid = (M // tm, N // tn, nk)

    def body(*refs):
        a_ref, b_ref = refs[0], refs[1]
        add_ref = refs[2] if add is not None else None
        o_ref = refs[3 if add is not None else 2]
        acc_ref = refs[-1] if nk > 1 else None
        part = _raw_dot(a_ref[...], b_ref[...], False, False, False)

        def finish(total):
            if add_ref is not None:
                total = total + add_ref[...]
            o_ref[...] = total.astype(o_ref.dtype)

        if nk == 1:
            finish(part)
        else:
            k = pl.program_id(2)

            @pl.when(k == 0)
            def _():
                acc_ref[...] = part

            @pl.when(k > 0)
            def _():
                acc_ref[...] += part

            @pl.when(k == nk - 1)
            def _():
                finish(acc_ref[...])

    b_spec = (pl.BlockSpec((tk, tn), lambda i, j, k: (k, j)) if b.ndim == 2 else
              pl.BlockSpec((None, tk, tn), lambda i, j, k: (j, k, 0)))
    in_specs = [pl.BlockSpec((tm, tk), lambda i, j, k: (i, k)), b_spec]
    args = [a, b]
    if add is not None:
        in_specs.append(pl.BlockSpec((tm, tn), lambda i, j, k: (i, j)))
        args.append(add)
    return _pallas(
        body, out_shape=jax.ShapeDtypeStruct((M, N), out_dtype), grid=grid, in_specs=in_specs,
        out_specs=pl.BlockSpec((tm, tn), lambda i, j, k: (i, j)),
        scratch_shapes=[pltpu.VMEM((tm, tn), F32)] if nk > 1 else [],
        name=name, compiler_params=_params(grid))(*args)


def _mm_tn(name, a, b, *, tr, tka, tn, blocked=False):
    R, Ka = a.shape
    N = b.shape[1]
    nr = R // tr
    grid = (Ka // tka, N // tn, nr)
    if blocked:
        out_shape = jax.ShapeDtypeStruct((N // tn, Ka, tn), F32)
        out_spec = pl.BlockSpec((None, tka, tn), lambda i, j, r: (j, i, 0))
    else:
        out_shape = jax.ShapeDtypeStruct((Ka, N), F32)
        out_spec = pl.BlockSpec((tka, tn), lambda i, j, r: (i, j))

    def body(a_ref, b_ref, o_ref):
        r = pl.program_id(2)
        part = _raw_dot(a_ref[...], b_ref[...], True, False, False)

        @pl.when(r == 0)
        def _():
            o_ref[...] = part

        @pl.when(r > 0)
        def _():
            o_ref[...] += part

    return _pallas(
        body, out_shape=out_shape, grid=grid,
        in_specs=[pl.BlockSpec((tr, tka), lambda i, j, r: (r, i)),
                  pl.BlockSpec((tr, tn), lambda i, j, r: (r, j))],
        out_specs=out_spec, name=name, compiler_params=_params(grid))(a, b)


def _conv_fwd(name, x, xcol0, w, b, *, taps, width, tr, tc):
    R = x.shape[0]
    grid = (width // tc, R // tr)
    cb0 = xcol0 // tc
    hb = tr // 8

    def body(*refs):
        x_ref, xp_ref, w_ref = refs[:3]
        b_ref = refs[3] if b is not None else None
        o_ref = refs[-1]
        i = pl.program_id(1)
        xv = x_ref[...]
        prev = jnp.where(i > 0, xp_ref[...], 0.0)
        ext = jnp.concatenate([prev, xv], axis=0)
        acc = xv * w_ref[taps - 1:taps, :]
        for s in range(1, taps):
            acc = acc + pltpu.roll(ext, s, 0)[8:, :] * w_ref[taps - 1 - s:taps - s, :]
        if b_ref is not None:
            acc = acc + b_ref[...]
        o_ref[...] = acc

    in_specs = [pl.BlockSpec((tr, tc), lambda j, i: (i, cb0 + j)),
                pl.BlockSpec((8, tc), lambda j, i: (jnp.maximum(i * hb - 1, 0), cb0 + j)),
                pl.BlockSpec((taps, tc), lambda j, i: (0, j))]
    args = [x, x, w]
    if b is not None:
        in_specs.append(pl.BlockSpec((1, tc), lambda j, i: (0, j)))
        args.append(b)
    return _pallas(
        body, out_shape=jax.ShapeDtypeStruct((R, width), F32), grid=grid, in_specs=in_specs,
        out_specs=pl.BlockSpec((tr, tc), lambda j, i: (i, j)),
        name=name, compiler_params=_params(grid))(*args)


def _conv_bwd(name, x, xcol0, w, dy, *, taps, width, tr, tc, with_bias, dx_into=None):
    R = x.shape[0]
    nr = R // tr
    grid = (width // tc, nr)
    cb0 = xcol0 // tc
    hb = tr // 8
    n_ext = tr + 8
    n_al = 0 if dx_into is None else 1

    def body(*refs):
        x_ref, xp_ref, w_ref, dy_ref, dyn_ref = refs[:5]
        dx_ref, dw_ref = refs[5 + n_al], refs[6 + n_al]
        db_ref = refs[7 + n_al] if with_bias else None
        i = pl.program_id(1)
        xv = x_ref[...]
        ext = jnp.concatenate([jnp.where(i > 0, xp_ref[...], 0.0), xv], axis=0)
        dyv = dy_ref[...]
        dext = jnp.concatenate([dyv, jnp.where(i < nr - 1, dyn_ref[...], 0.0)], axis=0)
        dx = dyv * w_ref[taps - 1:taps, :]
        dws = [None] * taps
        dws[taps - 1] = jnp.sum(xv * dyv, axis=0, keepdims=True)
        for s in range(1, taps):
            dx = dx + pltpu.roll(dext, n_ext - s, 0)[:tr, :] * w_ref[taps - 1 - s:taps - s, :]
            dws[taps - 1 - s] = jnp.sum(pltpu.roll(ext, s, 0)[8:, :] * dyv, axis=0, keepdims=True)
        dx_ref[...] = dx.astype(dx_ref.dtype)

        @pl.when(i == 0)
        def _():
            for k in range(taps):
                dw_ref[k:k + 1, :] = dws[k]
            if db_ref is not None:
                db_ref[...] = jnp.sum(dyv, axis=0, keepdims=True)

        @pl.when(i > 0)
        def _():
            for k in range(taps):
                dw_ref[k:k + 1, :] += dws[k]
            if db_ref is not None:
                db_ref[...] += jnp.sum(dyv, axis=0, keepdims=True)

    in_specs = [pl.BlockSpec((tr, tc), lambda j, i: (i, cb0 + j)),
                pl.BlockSpec((8, tc), lambda j, i: (jnp.maximum(i * hb - 1, 0), cb0 + j)),
                pl.BlockSpec((taps, tc), lambda j, i: (0, j)),
                pl.BlockSpec((tr, tc), lambda j, i: (i, j)),
                pl.BlockSpec((8, tc), lambda j, i: (jnp.minimum((i + 1) * hb, R // 8 - 1), j))]
    args = [x, x, w, dy, dy]
    if dx_into is None:
        dx_shape, dx_spec, aliases = jax.ShapeDtypeStruct((R, width), BF16), pl.BlockSpec((tr, tc), lambda j, i: (i, j)), {}
    else:
        dx_shape = jax.ShapeDtypeStruct(dx_into.shape, dx_into.dtype)
        dx_spec, aliases = pl.BlockSpec((tr, tc), lambda j, i: (i, cb0 + j)), {5: 0}
        in_specs.append(ANY)
        args.append(dx_into)
    out_shape = [dx_shape, jax.ShapeDtypeStruct((taps, width), F32)]
    out_specs = [dx_spec, pl.BlockSpec((taps, tc), lambda j, i: (0, j))]
    if with_bias:
        out_shape.append(jax.ShapeDtypeStruct((1, width), F32))
        out_specs.append(pl.BlockSpec((1, tc), lambda j, i: (0, j)))
    return _pallas(
        body, out_shape=out_shape, grid=grid, in_specs=in_specs, out_specs=out_specs, input_output_aliases=aliases,
        name=name, compiler_params=_params(grid))(*args)


def _row_mask(cfg, i, tr):
    rows = i * tr + lax.broadcasted_iota(jnp.int32, (tr, 1), 0)
    return (rows >= cfg.front).astype(F32)


def _make_rms_fn(cfg, tr, with_residual):
    def fn(pids, h, g):
        hm = h * _row_mask(cfg, pids[0], tr)
        if with_residual:
            return _rms(hm, g), hm
        return (_rms(hm, g),)
    return fn


def _make_gdn_prep_fn(cfg, tr):
    d, hg = cfg.d, cfg.hg

    def fn(pids, c, tail, alog, dtb):
        cq, ck, cv = c[:, :d], c[:, d:2 * d], c[:, 2 * d:]
        mask = _row_mask(cfg, pids[0], tr)
        j, col = _iota2(LANES, d, 0), _iota2(LANES, d, 1)
        ea = ((col >> 7) == j).astype(F32)
        eb = ((col >> 7) + hg == j).astype(F32)
        al = jnp.sum(alog, axis=0, keepdims=True)
        db = jnp.sum(dtb, axis=0, keepdims=True)
        lg = _dot_sel(-jnp.exp(al) * _softplus(tail + db) * mask, ea, False)
        beta = _dot_sel(_sigmoid(tail) * mask, eb, False)
        sq, sk, sv = _silu(cq), _silu(ck), _silu(cv)
        qs, ks = [], []
        for h in range(hg):
            sl = slice(h * GDN_DK, (h + 1) * GDN_DK)
            qh, kh = sq[:, sl], sk[:, sl]
            qs.append(qh * lax.rsqrt(jnp.sum(qh * qh, axis=-1, keepdims=True) + EPS) * (GDN_DK ** -0.5))
            ks.append(kh * lax.rsqrt(jnp.sum(kh * kh, axis=-1, keepdims=True) + EPS))
        return jnp.concatenate(qs, axis=1), jnp.concatenate(ks, axis=1), sv, beta, lg
    return fn


def _gdn_intra_fn(pids, q, k, v, bB, lB, t_saved=None):
    rows = q.shape[0]
    nb = rows // CHUNK
    q3, k3, v3, b3, l3 = [t.reshape(nb, CHUNK, GDN_DK) for t in (q, k, v, bB, lB)]
    r, c = _iota2(CHUNK, CHUNK, 0), _iota2(CHUNK, CHUNK, 1)
    tril = (r >= c)
    strict = (r > c)
    gcol = _dot(_bcast(tril.astype(F32), nb), l3, False, False, True)
    l64 = l3[:, :, :CHUNK]
    grow = _dot(jnp.ones((nb, CHUNK, CHUNK), F32), l64 * (r <= c).astype(F32)[None], False, False, True)
    diff = gcol[:, :, :CHUNK] - grow
    decay = jnp.where(tril[None], jnp.exp(jnp.where(tril[None], diff, 0.0)), 0.0)
    kb = k3 * b3
    m = jnp.where(strict[None], _dot(kb, k3, False, True) * decay, 0.0)
    t = _tri_inv_raw(m) if t_saved is None else _tri_inv_given(m, t_saved.reshape(nb, CHUNK, CHUNK))
    eg = jnp.exp(gcol)
    u = _dot(t, v3 * b3)
    w = _dot(t, kb * eg)
    attn = _dot(q3, k3, False, True) * decay
    qd = q3 * eg
    glast = jnp.sum(l3, axis=1, keepdims=True)
    kd = k3 * jnp.exp(glast - gcol)
    gl = jnp.exp(glast)
    outs = (u.reshape(rows, GDN_DK), w.reshape(rows, GDN_DK), attn.reshape(1, rows, CHUNK),
            qd.reshape(rows, GDN_DK), kd.reshape(rows, GDN_DK), gl.reshape(1, nb, 1, GDN_DK))
    return outs + (t.reshape(1, rows, CHUNK),) if t_saved is None else outs


def _make_rot_fn(cfg):
    hr = cfg.hr
    half = RET_DK // 2

    def fn(pids, rqk, cos, sin):
        rq, rk = rqk[:, :cfg.d], rqk[:, cfg.d:]

        def rot(t, scale):
            outs = []
            for h in range(hr):
                x1 = t[:, h * RET_DK:h * RET_DK + half]
                x2 = t[:, h * RET_DK + half:(h + 1) * RET_DK]
                outs += [(x1 * cos - x2 * sin) * scale, (x2 * cos + x1 * sin) * scale]
            return jnp.concatenate(outs, axis=1)
        return rot(rq, 1.0), rot(rk, RET_DK ** -0.5)
    return fn


def _make_mix_fn(cfg):
    hg, hr = cfg.hg, cfg.hr

    def fn(pids, oa, ob, pm, gnorm):
        d = cfg.d
        gz, rg, gate_a, gate_b = pm[:, :d], pm[:, d:2 * d], pm[:, 2 * d:3 * d], pm[:, 3 * d:]
        oas = []
        for h in range(hg):
            oh = oa[:, h * GDN_DK:(h + 1) * GDN_DK]
            oas.append(oh * lax.rsqrt(jnp.mean(oh * oh, axis=-1, keepdims=True) + EPS) * gnorm)
        ya = jnp.concatenate(oas, axis=1) * _silu(gz)
        obs = []
        for h in range(hr):
            oh = ob[:, h * RET_DK:(h + 1) * RET_DK]
            obs.append(oh * lax.rsqrt(jnp.mean(oh * oh, axis=-1, keepdims=True) + EPS))
        yb = _silu(rg) * jnp.concatenate(obs, axis=1)
        return (_sigmoid(gate_a) * ya + _sigmoid(gate_b) * yb,)
    return fn


def _act_fn(pids, u):
    f = u.shape[1] // 2
    return (_silu(u[:, :f]) * u[:, f:],)


def _gdn_step(s, u, w, a, qd, kd, gl):
    v_new = u - _dot(w, s)
    o = _dot(qd, s) + _dot(a, v_new)
    s2 = s * gl + _dot(kd, v_new, True, False)
    return s2, o


def _ret_step(s, q, k, v, dm, qdc, kdc, g):
    inner = _dot(_dot(q, k, False, True) * dm, v)
    o = inner + _dot(q * qdc, s)
    s2 = s * g + _dot(k * kdc, v, True, False)
    return s2, o


def _gdn_scan_fwd(cfg, u, w, attn, qd, kd, gl):
    d, hg, nch, sc = cfg.d, cfg.hg, cfg.nch, cfg.sc
    nst = nch // sc

    def body(u_ref, w_ref, a_ref, qd_ref, kd_ref, gl_ref, o_ref, ss_ref, s_ref):
        @pl.when(pl.program_id(0) == 0)
        def _():
            s_ref[...] = jnp.zeros(s_ref.shape, F32)

        states = [s_ref[h] for h in range(hg)]
        for j in range(sc):
            rows = slice(j * CHUNK, (j + 1) * CHUNK)
            outs = []
            for h in range(hg):
                sl = slice(h * GDN_DK, (h + 1) * GDN_DK)
                ss_ref[j, h] = states[h]
                states[h], o = _gdn_step(states[h], u_ref[rows, sl], w_ref[rows, sl], a_ref[h, rows, :],
                                         qd_ref[rows, sl], kd_ref[rows, sl], gl_ref[h, j])
                outs.append(o)
            o_ref[rows, :] = jnp.concatenate(outs, axis=1)
        for h in range(hg):
            s_ref[h] = states[h]

    row = pl.BlockSpec((sc * CHUNK, d), lambda n: (n, 0))
    return _pallas(
        body,
        out_shape=[jax.ShapeDtypeStruct((cfg.rp, d), F32), jax.ShapeDtypeStruct((nch, hg, GDN_DK, GDN_DK), F32)],
        grid=(nst,),
        in_specs=[row, row, pl.BlockSpec((hg, sc * CHUNK, CHUNK), lambda n: (0, n, 0)), row, row,
                  pl.BlockSpec((hg, sc, 1, GDN_DK), lambda n: (0, n, 0, 0))],
        out_specs=[row, pl.BlockSpec((sc, hg, GDN_DK, GDN_DK), lambda n: (n, 0, 0, 0))],
        scratch_shapes=[pltpu.VMEM((hg, GDN_DK, GDN_DK), F32)],
        name="gdn_scan_fwd", compiler_params=_params((nst,)))(u, w, attn, qd, kd, gl)


def _gdn_scan_bwd(cfg, do, u, w, attn, qd, kd, gl, ss):
    d, hg, nch, sc = cfg.d, cfg.hg, cfg.nch, cfg.sc
    nst = nch // sc

    def body(do_ref, u_ref, w_ref, a_ref, qd_ref, kd_ref, gl_ref, ss_ref,
             du_ref, dw_ref, da_ref, dqd_ref, dkd_ref, dgl_ref, ds_ref):
        @pl.when(pl.program_id(0) == 0)
        def _():
            ds_ref[...] = jnp.zeros(ds_ref.shape, F32)

        dstates = [ds_ref[h] for h in range(hg)]
        for j in reversed(range(sc)):
            rows = slice(j * CHUNK, (j + 1) * CHUNK)
            dus, dws, dqds, dkds = [], [], [], []
            for h in range(hg):
                sl = slice(h * GDN_DK, (h + 1) * GDN_DK)
                args = (ss_ref[j, h], u_ref[rows, sl], w_ref[rows, sl], a_ref[h, rows, :], qd_ref[rows, sl],
                        kd_ref[rows, sl], gl_ref[h, j])
                _, vjp_fn = jax.vjp(_gdn_step, *args)
                dstates[h], du, dw, da, dqd, dkd, dgl = vjp_fn((dstates[h], do_ref[rows, sl]))
                da_ref[h, rows, :] = da
                dgl_ref[h, j] = dgl
                dus.append(du)
                dws.append(dw)
                dqds.append(dqd)
                dkds.append(dkd)
            du_ref[rows, :] = jnp.concatenate(dus, axis=1)
            dw_ref[rows, :] = jnp.concatenate(dws, axis=1)
            dqd_ref[rows, :] = jnp.concatenate(dqds, axis=1)
            dkd_ref[rows, :] = jnp.concatenate(dkds, axis=1)
        for h in range(hg):
            ds_ref[h] = dstates[h]

    row = pl.BlockSpec((sc * CHUNK, d), lambda n: (nst - 1 - n, 0))
    aspec = pl.BlockSpec((hg, sc * CHUNK, CHUNK), lambda n: (0, nst - 1 - n, 0))
    gspec = pl.BlockSpec((hg, sc, 1, GDN_DK), lambda n: (0, nst - 1 - n, 0, 0))
    rowshape = jax.ShapeDtypeStruct((cfg.rp, d), F32)
    return _pallas(
        body,
        out_shape=[rowshape, rowshape, jax.ShapeDtypeStruct(attn.shape, F32), rowshape, rowshape,
                   jax.ShapeDtypeStruct(gl.shape, F32)],
        grid=(nst,),
        in_specs=[row, row, row, aspec, row, row, gspec,
                  pl.BlockSpec((sc, hg, GDN_DK, GDN_DK), lambda n: (nst - 1 - n, 0, 0, 0))],
        out_specs=[row, row, aspec, row, row, gspec],
        scratch_shapes=[pltpu.VMEM((hg, GDN_DK, GDN_DK), F32)],
        name="gdn_scan_bwd", compiler_params=_params((nst,)))(do, u, w, attn, qd, kd, gl, ss)


def _ret_consts(cfg):
    hr = cfg.hr
    lg = np.log(1.0 - 2.0 ** (-5.0 - np.arange(hr, dtype=np.float64)))
    idx = np.arange(CHUNK, dtype=np.float64)
    tril = np.tril(np.ones((CHUNK, CHUNK), dtype=bool))
    dm = np.where(tril[None], np.exp((idx[:, None] - idx[None, :])[None] * lg[:, None, None]), 0.0)
    qdc = np.exp((idx[None, :] + 1.0) * lg[:, None])
    kdc = np.exp((CHUNK - 1.0 - idx[None, :]) * lg[:, None])
    gch = np.exp(CHUNK * lg)
    qdc = np.broadcast_to(qdc[:, :, None], (hr, CHUNK, RET_DK))
    kdc = np.broadcast_to(kdc[:, :, None], (hr, CHUNK, RET_DK))
    gch = np.broadcast_to(gch[:, None, None], (hr, 1, RET_DK))
    return tuple(jnp.asarray(np.ascontiguousarray(t), F32) for t in (dm, qdc, kdc, gch))


def _ret_scan_fwd(cfg, qr, kr, proj, consts):
    d, hr, nch, sc = cfg.d, cfg.hr, cfg.nch, cfg.sc
    nst = nch // sc
    dm, qdc, kdc, gch = consts

    def body(q_ref, k_ref, v_ref, dm_ref, qdc_ref, kdc_ref, g_ref, o_ref, ss_ref, s_ref):
        @pl.when(pl.program_id(0) == 0)
        def _():
            s_ref[...] = jnp.zeros(s_ref.shape, F32)

        states = [s_ref[h] for h in range(hr)]
        for j in range(sc):
            rows = slice(j * CHUNK, (j + 1) * CHUNK)
            outs = []
            for h in range(hr):
                sl = slice(h * RET_DK, (h + 1) * RET_DK)
                ss_ref[j, h] = states[h]
                states[h], o = _ret_step(states[h], q_ref[rows, sl], k_ref[rows, sl], v_ref[rows, sl], dm_ref[h],
                                         qdc_ref[h], kdc_ref[h], g_ref[h])
                outs.append(o)
            o_ref[rows, :] = jnp.concatenate(outs, axis=1)
        for h in range(hr):
            s_ref[h] = states[h]

    row = pl.BlockSpec((sc * CHUNK, d), lambda n: (n, 0))
    return _pallas(
        body,
        out_shape=[jax.ShapeDtypeStruct((cfg.rp, d), F32), jax.ShapeDtypeStruct((nch, hr, RET_DK, RET_DK), F32)],
        grid=(nst,),
        in_specs=[row, row, pl.BlockSpec((sc * CHUNK, d), lambda n: (n, RV_BLOCK)), _full(dm), _full(qdc), _full(kdc),
                  _full(gch)],
        out_specs=[row, pl.BlockSpec((sc, hr, RET_DK, RET_DK), lambda n: (n, 0, 0, 0))],
        scratch_shapes=[pltpu.VMEM((hr, RET_DK, RET_DK), F32)],
        name="ret_scan_fwd", compiler_params=_params((nst,)))(qr, kr, proj, dm, qdc, kdc, gch)


def _ret_scan_bwd(cfg, do, qr, kr, proj, consts, ss, dproj):
    d, hr, nch, sc = cfg.d, cfg.hr, cfg.nch, cfg.sc
    nst = nch // sc
    dm, qdc, kdc, gch = consts

    def body(do_ref, q_ref, k_ref, v_ref, dm_ref, qdc_ref, kdc_ref, g_ref, ss_ref, _, dq_ref, dk_ref, dv_ref, ds_ref):
        @pl.when(pl.program_id(0) == 0)
        def _():
            ds_ref[...] = jnp.zeros(ds_ref.shape, F32)

        dstates = [ds_ref[h] for h in range(hr)]
        for j in reversed(range(sc)):
            rows = slice(j * CHUNK, (j + 1) * CHUNK)
            dqs, dks, dvs = [], [], []
            for h in range(hr):
                sl = slice(h * RET_DK, (h + 1) * RET_DK)
                cs = (dm_ref[h], qdc_ref[h], kdc_ref[h], g_ref[h])
                _, vjp_fn = jax.vjp(lambda s, q, k, v, cs=cs: _ret_step(s, q, k, v, *cs),
                                    ss_ref[j, h], q_ref[rows, sl], k_ref[rows, sl], v_ref[rows, sl])
                dstates[h], dq, dk, dv = vjp_fn((dstates[h], do_ref[rows, sl]))
                dqs.append(dq)
                dks.append(dk)
                dvs.append(dv)
            dq_ref[rows, :] = jnp.concatenate(dqs, axis=1)
            dk_ref[rows, :] = jnp.concatenate(dks, axis=1)
            dv_ref[rows, :] = jnp.concatenate(dvs, axis=1).astype(dv_ref.dtype)
        for h in range(hr):
            ds_ref[h] = dstates[h]

    row = pl.BlockSpec((sc * CHUNK, d), lambda n: (nst - 1 - n, 0))
    rowshape = jax.ShapeDtypeStruct((cfg.rp, d), F32)
    vspec = pl.BlockSpec((sc * CHUNK, d), lambda n: (nst - 1 - n, RV_BLOCK))
    return _pallas(
        body,
        out_shape=[rowshape, rowshape, jax.ShapeDtypeStruct(dproj.shape, dproj.dtype)],
        grid=(nst,),
        in_specs=[row, row, row, vspec, _full(dm), _full(qdc), _full(kdc), _full(gch),
                  pl.BlockSpec((sc, hr, RET_DK, RET_DK), lambda n: (nst - 1 - n, 0, 0, 0)), ANY],
        out_specs=[row, row, vspec], input_output_aliases={9: 2},
        scratch_shapes=[pltpu.VMEM((hr, RET_DK, RET_DK), F32)],
        name="ret_scan_bwd", compiler_params=_params((nst,)))(do, qr, kr, proj, dm, qdc, kdc, gch, ss, dproj)


def _final(cfg, h2, normf, tgt):
    d, tr = cfg.d, cfg.xrow
    nr = cfg.rp // tr

    def body(h_ref, g_ref, t_ref, dh_ref, dg_ref, loss_ref):
        i = pl.program_id(0)
        y, vjp_fn = jax.vjp(_rms, h_ref[...], g_ref[...])
        err = jnp.where(i >= 1, y - t_ref[...], 0.0)
        dh, dg = vjp_fn(err * (1.0 / d))
        dh_ref[...] = dh
        part = jnp.zeros((8, LANES), F32) + 0.5 * jnp.sum(err * err) * (1.0 / d)

        @pl.when(i == 0)
        def _():
            dg_ref[...] = dg
            loss_ref[...] = part

        @pl.when(i > 0)
        def _():
            dg_ref[...] += dg
            loss_ref[...] += part

    return _pallas(
        body,
        out_shape=[jax.ShapeDtypeStruct((cfg.rp, d), F32), jax.ShapeDtypeStruct((1, d), F32),
                   jax.ShapeDtypeStruct((8, LANES), F32)],
        grid=(nr,),
        in_specs=[_rows(tr, d), _full(normf), pl.BlockSpec((tr, d), lambda i: (jnp.maximum(i - 1, 0), 0))],
        out_specs=[_rows(tr, d), pl.BlockSpec((1, d), lambda i: (0, 0)), pl.BlockSpec((8, LANES), lambda i: (0, 0))],
        name="final_loss", compiler_params=_params((nr,)))(h2, normf, tgt)


ANY = pl.BlockSpec(memory_space=pl.ANY)


def _place():
    x, y, c = lax.axis_index("x"), lax.axis_index("y"), lax.axis_index("c")
    others = [(1 - x, y), (x, 1 - y), (1 - x, 1 - y)]
    return x, y, c, others


def _row_tile(rows, cap=256):
    return max(t for t in range(16, min(rows, cap) + 1, 16) if rows % t == 0)


def _all_gather_chips(ws):
    n = len(ws)

    def body(*refs):
        w_refs, o_refs = refs[:n], refs[n:2 * n]
        send_sems, recv_sems = refs[2 * n:]
        x, y, c, others = _place()
        me = 2 * x + y
        sibling = (x, y, 1 - c)
        chips = [2 * px + py for px, py in others]
        halves = [w.shape[0] // 2 for w in ws]

        def piece(a, chip, core):
            return o_refs[a].at[chip, pl.ds(core * halves[a], halves[a]), :]

        def copy(a, k, src, chip, core, to):
            return pltpu.make_async_remote_copy(src_ref=src, dst_ref=piece(a, chip, core), send_sem=send_sems.at[6 * a + k],
                                                recv_sem=recv_sems.at[6 * a + k], device_id=to, device_id_type=MESH)

        first = [copy(a, j, w_refs[a].at[pl.ds(c * halves[a], halves[a]), :], me, c, (px, py, c))
                 for j, (px, py) in enumerate(others) for a in range(n)]
        for cp in first:
            cp.start()
        passed = []
        for j in range(3):
            for a in range(n):
                copy(a, j, piece(a, chips[j], c), chips[j], c, (x, y, c)).wait_recv()
                cp = copy(a, 3 + j, piece(a, chips[j], c), chips[j], c, sibling)
                cp.start()
                passed.append(cp)
        for j in range(3):
            for a in range(n):
                copy(a, 3 + j, piece(a, chips[j], 1 - c), chips[j], 1 - c, (x, y, c)).wait_recv()
        for cp in first + passed:
            cp.wait_send()

    return _pallas(
        body, out_shape=[jax.ShapeDtypeStruct((N_CHIPS,) + w.shape, w.dtype) for w in ws],
        in_specs=[ANY] * n, out_specs=[ANY] * n,
        scratch_shapes=[pltpu.SemaphoreType.DMA((6 * n,)), pltpu.SemaphoreType.DMA((6 * n,))],
        name="weights_all_gather")(*ws)


def _pair_exchange(gs):
    n = len(gs)

    def body(*refs):
        g_refs, o_refs = refs[:n], refs[n:2 * n]
        send_sems, recv_sems = refs[2 * n:]
        x, y, c, _ = _place()
        cps = []
        for a in range(n):
            half = gs[a].shape[1] // 2
            cp = pltpu.make_async_remote_copy(
                src_ref=g_refs[a].at[:, pl.ds((1 - c) * half, half), :], dst_ref=o_refs[a], send_sem=send_sems.at[a],
                recv_sem=recv_sems.at[a], device_id=(x, y, 1 - c), device_id_type=MESH)
            cp.start()
            cps.append(cp)
        for cp in cps:
            cp.wait()

    return _pallas(
        body, out_shape=[jax.ShapeDtypeStruct((N_CHIPS, g.shape[1] // 2, g.shape[2]), g.dtype) for g in gs],
        in_specs=[ANY] * n, out_specs=[ANY] * n,
        scratch_shapes=[pltpu.SemaphoreType.DMA((n,)), pltpu.SemaphoreType.DMA((n,))], name="grad_pair_exchange")(*gs)


def _pair_sum(name, g, recv, cidx):
    half, cols = recv.shape[1], recv.shape[2]
    tr = _row_tile(half)
    nblk = half // tr

    def body(c_ref, g_ref, r_ref, o_ref):
        o_ref[...] = (g_ref[...] + r_ref[...]).astype(o_ref.dtype)

    grid_spec = pltpu.PrefetchScalarGridSpec(
        num_scalar_prefetch=1, grid=(N_CHIPS, nblk),
        in_specs=[pl.BlockSpec((1, tr, cols), lambda s, i, c: (s, c[0] * nblk + i, 0)),
                  pl.BlockSpec((1, tr, cols), lambda s, i, c: (s, i, 0))],
        out_specs=pl.BlockSpec((1, tr, cols), lambda s, i, c: (s, i, 0)))
    return _pallas(
        body, out_shape=jax.ShapeDtypeStruct((N_CHIPS, half, cols), BF16), grid_spec=grid_spec,
        name=name, compiler_params=_params((N_CHIPS, nblk)))(cidx, g, recv)


def _chip_exchange(parts):
    n = len(parts)

    def body(*refs):
        p_refs, o_refs = refs[:n], refs[n:2 * n]
        send_sems, recv_sems = refs[2 * n:]
        x, y, c, others = _place()
        me = 2 * x + y

        def copy(a, j, src_chip, dst_chip):
            px, py = others[j]
            return pltpu.make_async_remote_copy(
                src_ref=p_refs[a].at[src_chip], dst_ref=o_refs[a].at[dst_chip], send_sem=send_sems.at[3 * a + j],
                recv_sem=recv_sems.at[3 * a + j], device_id=(px, py, c), device_id_type=MESH)

        sends = [copy(a, j, 2 * px + py, me) for j, (px, py) in enumerate(others) for a in range(n)]
        for cp in sends:
            cp.start()
        for j, (px, py) in enumerate(others):
            for a in range(n):
                copy(a, j, me, 2 * px + py).wait_recv()
        for cp in sends:
            cp.wait_send()

    return _pallas(
        body, out_shape=[jax.ShapeDtypeStruct(p.shape, p.dtype) for p in parts], in_specs=[ANY] * n, out_specs=[ANY] * n,
        scratch_shapes=[pltpu.SemaphoreType.DMA((3 * n,)), pltpu.SemaphoreType.DMA((3 * n,))],
        name="grad_chip_exchange")(*parts)


def _chip_sum(name, part, slots, chip):
    half, cols = slots.shape[1], slots.shape[2]
    tr = _row_tile(half)

    def body(me_ref, p_ref, *rest):
        s_refs, o_ref = rest[:N_CHIPS], rest[N_CHIPS]
        own = p_ref[...].astype(F32)
        v = [jnp.where(me_ref[0] == k, own, s_refs[k][...].astype(F32)) for k in range(N_CHIPS)]
        o_ref[...] = ((v[0] + v[1]) + v[2]) + v[3]

    def slot_spec(k):
        return pl.BlockSpec((None, tr, cols), lambda i, me: (jnp.where(me[0] == k, (k + 1) % N_CHIPS, k), i, 0))

    grid_spec = pltpu.PrefetchScalarGridSpec(
        num_scalar_prefetch=1, grid=(half // tr,),
        in_specs=[pl.BlockSpec((None, tr, cols), lambda i, me: (me[0], i, 0))] + [slot_spec(k) for k in range(N_CHIPS)],
        out_specs=pl.BlockSpec((tr, cols), lambda i, me: (i, 0)))
    return _pallas(
        body, out_shape=jax.ShapeDtypeStruct((half, cols), F32), grid_spec=grid_spec,
        name=name, compiler_params=_params((half // tr,)))(chip, part, *([slots] * N_CHIPS))


def _pair_swap(fins):
    n = len(fins)

    def body(*refs):
        f_refs, o_refs = refs[:n], refs[n:2 * n]
        send_sems, recv_sems = refs[2 * n:]
        x, y, c, _ = _place()
        cps = [pltpu.make_async_remote_copy(src_ref=f_refs[a], dst_ref=o_refs[a], send_sem=send_sems.at[a],
                                            recv_sem=recv_sems.at[a], device_id=(x, y, 1 - c), device_id_type=MESH)
               for a in range(n)]
        for cp in cps:
            cp.start()
        for cp in cps:
            cp.wait()

    return _pallas(
        body, out_shape=[jax.ShapeDtypeStruct(f.shape, f.dtype) for f in fins], in_specs=[ANY] * n, out_specs=[ANY] * n,
        scratch_shapes=[pltpu.SemaphoreType.DMA((n,)), pltpu.SemaphoreType.DMA((n,))], name="grad_pair_swap")(*fins)


def _adamw(name, w, g_own, g_other, m, v, cidx):
    R, cols = w.shape
    half = R // 2
    tr = _row_tile(half, 128)
    nblk = half // tr
    c1 = 1.0 - ADAM_B1 ** ADAM_STEP
    c2 = 1.0 - ADAM_B2 ** ADAM_STEP

    def body(c_ref, w_ref, go_ref, gs_ref, m_ref, v_ref, g_ref, d_ref, nm_ref, nv_ref):
        mine = (pl.program_id(0) // nblk) == c_ref[0]
        gv = jnp.where(mine, go_ref[...], gs_ref[...])
        nm = ADAM_B1 * m_ref[...] + (1.0 - ADAM_B1) * gv
        nv = ADAM_B2 * v_ref[...] + (1.0 - ADAM_B2) * (gv * gv)
        g_ref[...] = gv
        d_ref[...] = -ADAM_LR * ((nm / c1) / (jnp.sqrt(nv / c2) + ADAM_EPS) + ADAM_WD * w_ref[...])
        nm_ref[...] = nm
        nv_ref[...] = nv

    spec = pl.BlockSpec((tr, cols), lambda i, c: (i, 0))
    hspec = pl.BlockSpec((tr, cols), lambda i, c: (i % nblk, 0))
    shape = jax.ShapeDtypeStruct((R, cols), F32)
    grid_spec = pltpu.PrefetchScalarGridSpec(num_scalar_prefetch=1, grid=(R // tr,),
                                             in_specs=[spec, hspec, hspec, spec, spec], out_specs=[spec] * 4)
    return _pallas(
        body, out_shape=[shape] * 4, grid_spec=grid_spec,
        name=name, compiler_params=_params((R // tr,)))(cidx, w, g_own, g_other, m, v)


PARAMS = (("meta", 1), ("norm1", None), ("w_in", 2), ("gdn_conv_w", 2), ("gdn_a_log", None), ("gdn_dt_bias", None),
          ("gdn_norm", None), ("w_out", 1), ("norm2", None), ("w_ffn_up", 2), ("ffn_conv_w", 2), ("ffn_conv_b", None),
          ("w_ffn_down", 1), ("norm_f", None))
BIG = ("w_in", "w_out", "w_ffn_up", "w_ffn_down")
PACK_ALIGN = 1024
PACK_ROWS_ALIGN = 32


def _pack(arrs, dtype):
    parts, total = [], 0
    for a in arrs:
        f = a.reshape(-1).astype(dtype)
        pad = (-f.shape[0]) % PACK_ALIGN
        parts.append(jnp.pad(f, (0, pad)) if pad else f)
        total += f.shape[0] + pad
    rows = total // LANES
    rpad = (-rows) % PACK_ROWS_ALIGN
    if rpad:
        parts.append(jnp.zeros((rpad * LANES,), dtype))
    return jnp.concatenate(parts).reshape(rows + rpad, LANES)


def _unpack(buf, shapes):
    flat = buf.reshape(-1)
    outs, off = [], 0
    for s in shapes:
        n = int(np.prod(s))
        outs.append(flat[off:off + n].reshape(s))
        off += n + (-n) % PACK_ALIGN
    return outs


def _split4(a, axis):
    n = a.shape[axis] // N_CHIPS
    return [lax.slice_in_dim(a, s * n, (s + 1) * n, axis=axis) for s in range(N_CHIPS)]


PROJ_ORDER = (3, 7, 8, 9, 0, 1, 2, 6, 4, 5)


def _reorder_w_in(w, cfg):
    d, hg = cfg.d, cfg.hg

    def block(k):
        off = k * d + (2 * hg if k >= 4 else 0)
        return w[:, off:off + d]

    tail = jnp.pad(w[:, 4 * d:4 * d + 2 * hg], ((0, 0), (0, LANES - 2 * hg)))
    return jnp.concatenate([block(k) for k in PROJ_ORDER] + [tail], axis=1)


def _restore_w_in(wr, cfg):
    d, hg = cfg.d, cfg.hg
    at = {k: i for i, k in enumerate(PROJ_ORDER)}
    block = lambda k: wr[:, at[k] * d:(at[k] + 1) * d]
    return jnp.concatenate([block(k) for k in range(4)] + [wr[:, 10 * d:10 * d + 2 * hg]] +
                           [block(k) for k in range(4, 10)], axis=1)


def _step(cfg, x, tgt, shard, m_shard, v_shard):
    d, hg, hr, dff, rp, tr, tm = cfg.d, cfg.hg, cfg.hr, cfg.dff, cfg.rp, cfg.tr, cfg.tm
    nrow = rp // tr
    assert cfg.tf * N_CHIPS == 2 * dff and cfg.din % N_CHIPS == 0
    cidx = lax.axis_index("c").astype(jnp.int32).reshape(1)
    chip = (2 * lax.axis_index("x") + lax.axis_index("y")).astype(jnp.int32).reshape(1)

    axis = dict(PARAMS)
    small = ("meta", "gdn_conv_w", "ffn_conv_w")
    small_shapes = [shard[n].shape for n in small]
    mine = [shard[n][0].astype(BF16) for n in BIG] + [_pack([shard[n] for n in small], F32)]
    gathered = _all_gather_chips(mine)
    g_in, g_out, g_up, g_down, g_small = [lax.dynamic_update_slice(g, w[None], (chip[0], 0, 0)) for g, w in zip(gathered, mine)]
    w_in_r = _reorder_w_in(jnp.concatenate([g_in[s] for s in range(N_CHIPS)], axis=1), cfg)
    w_out = g_out.reshape(d, d)
    w_up = g_up
    w_up_t = jnp.swapaxes(g_up, 1, 2).reshape(2 * dff, d)
    w_down = g_down.reshape(dff, d)
    per_chip = [_unpack(g_small[s], small_shapes) for s in range(N_CHIPS)]
    full = {n: jnp.concatenate([per_chip[s][k] for s in range(N_CHIPS)], axis=axis[n]) for k, n in enumerate(small)}
    meta = full["meta"]
    gconv_w = full["gdn_conv_w"][0]
    fconv_w = full["ffn_conv_w"][0]
    norm1, norm2, gnorm = shard["norm1"], shard["norm2"], shard["gdn_norm"]
    normf = shard["norm_f"].reshape(1, d)
    fconv_b = shard["ffn_conv_b"]
    alog = jnp.pad(shard["gdn_a_log"], ((0, 7), (0, LANES - hg)))
    dtb = jnp.pad(shard["gdn_dt_bias"], ((0, 7), (0, LANES - hg)))

    h0 = jnp.concatenate([jnp.zeros((cfg.front, d), F32), meta, x], axis=0)
    pos = jnp.arange(rp, dtype=F32) - float(cfg.front)
    half = RET_DK // 2
    inv = 1.0 / (ROPE_BASE ** (jnp.arange(half, dtype=F32) / half))
    ang = pos[:, None] * inv[None, :]
    cos, sin = jnp.cos(ang), jnp.sin(ang)
    rconsts = _ret_consts(cfg)

    rms_f = _make_rms_fn(cfg, tr, False)
    rms_b = _make_rms_fn(cfg, tr, True)
    rowshape = jax.ShapeDtypeStruct((rp, d), F32)
    rspec = _rows(tr, d)

    def rms_fwd(name, h, g):
        return _stage_fwd(name, rms_f, (nrow,), [In(h, rspec), In(g, _full(g))],
                          [jax.ShapeDtypeStruct((rp, d), BF16)], [rspec])[0]

    tn_in = cfg.pw // 9 if cfg.pw % (9 * LANES) == 0 else LANES
    hn1 = rms_fwd("rms1_fwd", h0, norm1)
    proj = _mm("proj_fwd", hn1, w_in_r, tm=tm, tn=tn_in, tk=d)
    cqkv = _conv_fwd("gdn_conv_fwd", proj, CONV_COL * d, gconv_w, None, taps=GDN_CONV, width=3 * d, tr=tr, tc=d)
    tail_spec = _rows(tr, LANES, TAIL_COL * d // LANES)
    prep_fn = _make_gdn_prep_fn(cfg, tr)

    def prep_ins(dproj=None):
        return [In(cqkv, _rows(tr, 3 * d), F32), In(proj, tail_spec, BF16, galias=dproj, gshape=(rp, cfg.pw)),
                In(alog, _full(alog), F32, True), In(dtb, _full(dtb), F32, True)]

    qn, kn, vv, bB, lB = _stage_fwd("gdn_prep_fwd", prep_fn, (nrow,), prep_ins(), [rowshape] * 5, [rspec] * 5)

    trg = cfg.nb * CHUNK
    gi_grid = (rp // trg, hg)
    hspec = pl.BlockSpec((trg, GDN_DK), lambda i, h: (i, h))
    aspec = pl.BlockSpec((1, trg, CHUNK), lambda i, h: (h, i, 0))
    gspec = pl.BlockSpec((1, cfg.nb, 1, GDN_DK), lambda i, h: (h, i, 0, 0))
    intra_ins = [In(t, hspec, F32) for t in (qn, kn, vv, bB, lB)]
    ashape = jax.ShapeDtypeStruct((hg, rp, CHUNK), F32)
    intra_shapes = [rowshape, rowshape, ashape, rowshape, rowshape, jax.ShapeDtypeStruct((hg, cfg.nch, 1, GDN_DK), F32), ashape]
    intra_specs = [hspec, hspec, aspec, hspec, hspec, gspec, aspec]
    gu, gw, gattn, gqd, gkd, ggl, gtinv = _stage_fwd("gdn_intra_fwd", _gdn_intra_fn, gi_grid, intra_ins, intra_shapes,
                                                     intra_specs)
    oa, gss = _gdn_scan_fwd(cfg, gu, gw, gattn, gqd, gkd, ggl)

    rot_fn = _make_rot_fn(cfg)

    def rot_ins(dproj=None):
        return [In(proj, _rows(tr, 2 * d, ROT_COL // 2), BF16, galias=dproj, gshape=(rp, cfg.pw)),
                In(cos, _rows(tr, half)), In(sin, _rows(tr, half))]

    qr, kr = _stage_fwd("rot_fwd", rot_fn, (nrow,), rot_ins(), [rowshape] * 2, [rspec] * 2)
    ob, rss = _ret_scan_fwd(cfg, qr, kr, proj, rconsts)

    mix_fn = _make_mix_fn(cfg)
    mix_ins = [In(oa, rspec, F32), In(ob, rspec, F32), In(proj, _rows(tr, 4 * d, MIX_COL // 4), BF16, gshape=(rp, cfg.pw)),
               In(gnorm, _full(gnorm), F32, True)]
    ymix = _stage_fwd("mix_fwd", mix_fn, (nrow,), mix_ins, [jax.ShapeDtypeStruct((rp, d), BF16)], [rspec])[0]
    h1 = _mm("out_proj_fwd", ymix, w_out, tm=tm, tn=d, tk=d, add=h0)

    hn2 = rms_fwd("rms2_fwd", h1, norm2)
    up = _mm("ffn_up_fwd", hn2, w_up, tm=tm, tn=cfg.tf, tk=d)
    uc = _conv_fwd("ffn_conv_fwd", up, 0, fconv_w, fconv_b, taps=FFN_CONV, width=2 * dff, tr=tr, tc=cfg.tf)
    tra = tr // 2
    act_ins = [In(uc, _rows(tra, 2 * dff), F32)]
    act_spec = _rows(tra, dff)
    act = _stage_fwd("ffn_act_fwd", _act_fn, (rp // tra,), act_ins, [jax.ShapeDtypeStruct((rp, dff), BF16)], [act_spec])[0]
    h2 = _mm("ffn_down_fwd", act, w_down, tm=tm, tn=d, tk=cfg.tf, add=h1)

    dh2, g_normf, loss_blk = _final(cfg, h2, normf, tgt)
    loss = lax.psum(loss_blk[0, 0], ("x", "y", "c"))

    g_w_down = _mm_tn("ffn_down_dw", act, dh2, tr=tm, tka=cfg.tf, tn=d)
    dact = _mm("ffn_down_dx", dh2, w_down.T, tm=tm, tn=cfg.tf, tk=d)
    duc, = _stage_bwd("ffn_act_bwd", _act_fn, (rp // tra,), act_ins, [(dact, act_spec)])
    dup, g_fconv_w, g_fconv_b = _conv_bwd("ffn_conv_bwd", up, 0, fconv_w, duc, taps=FFN_CONV, width=2 * dff,
                                          tr=tr, tc=cfg.tf, with_bias=True)
    g_w_up = _mm_tn("ffn_up_dw", hn2, dup, tr=tm, tka=d, tn=cfg.tf, blocked=True)
    dhn2 = _mm("ffn_up_dx", dup, w_up_t, tm=tm, tn=d, tk=cfg.tf)

    def rms_bwd(name, h, g, dhn, dres):
        ins = [In(h, rspec, F32), In(g, _full(g), F32, True)]
        return _stage_bwd(name, rms_b, (nrow,), ins, [(dhn, rspec), (dres, rspec)])

    dh1, g_norm2 = rms_bwd("rms2_bwd", h1, norm2, dhn2, dh2)
    g_w_out = _mm_tn("out_proj_dw", ymix, dh1, tr=tm, tka=d, tn=d)
    dymix = _mm("out_proj_dx", dh1, w_out.T, tm=tm, tn=d, tk=d)
    doa, dob, dproj, g_gnorm = _stage_bwd("mix_bwd", mix_fn, (nrow,), mix_ins, [(dymix, rspec)])

    dqr, dkr, dproj = _ret_scan_bwd(cfg, dob, qr, kr, proj, rconsts, rss, dproj)
    dproj, = _stage_bwd("rot_bwd", rot_fn, (nrow,), rot_ins(dproj), [(dqr, rspec), (dkr, rspec)])

    dgu, dgw, dgattn, dgqd, dgkd, dggl = _gdn_scan_bwd(cfg, doa, gu, gw, gattn, gqd, gkd, ggl, gss)
    intra_cots = [(dgu, hspec), (dgw, hspec), (dgattn, aspec), (dgqd, hspec), (dgkd, hspec), (dggl, gspec)]
    dqn, dkn, dvv, dbB, dlB = _stage_bwd("gdn_intra_bwd", _gdn_intra_fn, gi_grid, intra_ins + [In(gtinv, aspec)], intra_cots)
    dcqkv, dproj, g_alog, g_dtb = _stage_bwd(
        "gdn_prep_bwd", prep_fn, (nrow,), prep_ins(dproj), [(t, rspec) for t in (dqn, dkn, dvv, dbB, dlB)])
    dproj, g_gconv_w = _conv_bwd("gdn_conv_bwd", proj, CONV_COL * d, gconv_w, dcqkv, taps=GDN_CONV, width=3 * d,
                                 tr=tr, tc=d, with_bias=False, dx_into=dproj)
    g_w_in_r = _mm_tn("proj_dw", hn1, dproj, tr=tm, tka=d, tn=tn_in)
    dhn1 = _mm("proj_dx", dproj, w_in_r.T, tm=tm, tn=d, tk=tn_in)
    dh0, g_norm1 = rms_bwd("rms1_bwd", h0, norm1, dhn1, dh1)

    grad_x = dh0[cfg.xrow:]
    small_grads = {
        "meta": dh0[cfg.front:cfg.xrow], "norm1": g_norm1, "gdn_conv_w": g_gconv_w[None],
        "gdn_a_log": g_alog[0:1, :hg], "gdn_dt_bias": g_dtb[0:1, :hg], "gdn_norm": g_gnorm, "norm2": g_norm2,
        "ffn_conv_w": g_fconv_w[None], "ffn_conv_b": g_fconv_b, "norm_f": g_normf.reshape(d),
    }

    small_names = [n for n, _ in PARAMS if n not in BIG]
    g_small = jnp.stack([_pack([small_grads[n] if axis[n] is None else _split4(small_grads[n], axis[n])[s]
                                for n in small_names], F32) for s in range(N_CHIPS)])
    g_in4 = _restore_w_in(g_w_in_r, cfg).reshape(d, N_CHIPS, cfg.din // N_CHIPS).transpose(1, 0, 2)
    groups = [g_in4, g_w_out.reshape(N_CHIPS, d // N_CHIPS, d), g_w_up, g_w_down.reshape(N_CHIPS, dff // N_CHIPS, d), g_small]
    tags = ["w_in", "w_out", "w_ffn_up", "w_ffn_down", "small"]
    recvs = _pair_exchange(groups)
    parts = [_pair_sum("grad_pair_sum_" + t, g, r, cidx) for t, g, r in zip(tags, groups, recvs)]
    slots = _chip_exchange(parts)
    fins = [_chip_sum("grad_chip_sum_" + t, p, s, chip) for t, p, s in zip(tags, parts, slots)]
    sibs = _pair_swap(fins)

    def flat2(a):
        return a.reshape(-1, a.shape[-1])

    outs = {}
    for k, t in enumerate(BIG):
        res = _adamw("adamw_" + t, flat2(shard[t]), fins[k], sibs[k], flat2(m_shard[t]), flat2(v_shard[t]), cidx)
        outs[t] = [r.reshape(shard[t].shape) for r in res]
    small_shapes_all = [shard[n].shape for n in small_names]
    pk = lambda src: _pack([src[n] for n in small_names], F32)
    res = _adamw("adamw_small", pk(shard), fins[4], sibs[4], pk(m_shard), pk(v_shard), cidx)
    for k, r in enumerate(res):
        for n, a in zip(small_names, _unpack(r, small_shapes_all)):
            outs.setdefault(n, [None] * 4)[k] = a
    names = [n for n, _ in PARAMS]
    return (loss, grad_x[None], *[outs[n][k] for k in range(4) for n in names])


def kernel(x, meta, norm1, w_in, gdn_conv_w, gdn_a_log, gdn_dt_bias, gdn_norm, w_out, norm2, w_ffn_up, ffn_conv_w, ffn_conv_b, w_ffn_down, norm_f, loss_target, m_meta, m_norm1, m_w_in, m_gdn_conv_w, m_gdn_a_log, m_gdn_dt_bias, m_gdn_norm, m_w_out, m_norm2, m_w_ffn_up, m_ffn_conv_w, m_ffn_conv_b, m_w_ffn_down, m_norm_f, v_meta, v_norm1, v_w_in, v_gdn_conv_w, v_gdn_a_log, v_gdn_dt_bias, v_gdn_norm, v_w_out, v_norm2, v_w_ffn_up, v_ffn_conv_w, v_ffn_conv_b, v_w_ffn_down, v_norm_f):
    names = [n for n, _ in PARAMS]
    shard = dict(zip(names, (meta, norm1, w_in, gdn_conv_w, gdn_a_log, gdn_dt_bias, gdn_norm, w_out, norm2, w_ffn_up,
                             ffn_conv_w, ffn_conv_b, w_ffn_down, norm_f)))
    m_shard = dict(zip(names, (m_meta, m_norm1, m_w_in, m_gdn_conv_w, m_gdn_a_log, m_gdn_dt_bias, m_gdn_norm, m_w_out,
                               m_norm2, m_w_ffn_up, m_ffn_conv_w, m_ffn_conv_b, m_w_ffn_down, m_norm_f)))
    v_shard = dict(zip(names, (v_meta, v_norm1, v_w_in, v_gdn_conv_w, v_gdn_a_log, v_gdn_dt_bias, v_gdn_norm, v_w_out,
                               v_norm2, v_w_ffn_up, v_ffn_conv_w, v_ffn_conv_b, v_w_ffn_down, v_norm_f)))
    return _step(REAL, x[0], loss_target[0], shard, m_shard, v_shard)
```

```python
import functools
from typing import NamedTuple

import numpy as np
import jax
import jax.numpy as jnp
from jax import lax
from jax.experimental import pallas as pl
from jax.experimental.pallas import tpu as pltpu

F32 = jnp.float32
BF16 = jnp.bfloat16
EPS = 1e-6
CHUNK = 64
GDN_DK = 128
RET_DK = 256
GDN_CONV = 4
FFN_CONV = 3
ROPE_BASE = 10000.0
LANES = 128
N_CHIPS = 4
ADAM_LR, ADAM_B1, ADAM_B2, ADAM_EPS, ADAM_WD, ADAM_STEP = 0.001, 0.9, 0.999, 1e-08, 0.01, 10
MIX_COL, CONV_COL, RV_BLOCK, ROT_COL, TAIL_COL = 0, 4, 7, 8, 10
MESH = pl.DeviceIdType.MESH
VMEM_LIMIT = 56 * 1024 * 1024


class Cfg(NamedTuple):
    d: int
    seq: int
    n_meta: int
    dff: int
    tr: int
    nb: int
    tm: int
    tf: int
    sc: int

    @property
    def hg(self): return self.d // GDN_DK
    @property
    def hr(self): return self.d // RET_DK
    @property
    def L(self): return self.n_meta + self.seq
    @property
    def rp(self): return -(-self.L // 256) * 256
    @property
    def front(self): return self.rp - self.L
    @property
    def xrow(self): return self.rp - self.seq
    @property
    def nch(self): return self.rp // CHUNK
    @property
    def pw(self): return 10 * self.d + LANES
    @property
    def din(self): return 10 * self.d + 2 * self.hg


REAL = Cfg(d=1024, seq=8192, n_meta=16, dff=2816, tr=256, nb=12, tm=768, tf=1408, sc=4)


def _pallas(body, **kw):
    return pl.pallas_call(body, **kw)


def _sigmoid(x):
    return 1.0 / (1.0 + jnp.exp(-x))


def _silu(x):
    return x * _sigmoid(x)


def _softplus(x):
    return jnp.maximum(x, 0.0) + jnp.log(1.0 + jnp.exp(-jnp.abs(x)))


def _raw_dot(a, b, ta, tb, hi):
    if not hi:
        a = a.astype(BF16)
        b = b.astype(BF16)
    nbatch = a.ndim - 2
    ca = a.ndim - 2 if ta else a.ndim - 1
    cb = b.ndim - 1 if tb else b.ndim - 2
    batch = tuple(range(nbatch))
    return lax.dot_general(a, b, (((ca,), (cb,)), (batch, batch)),
                           precision=lax.Precision.HIGHEST if hi else None,
                           preferred_element_type=F32)


@functools.partial(jax.custom_vjp, nondiff_argnums=(2, 3, 4))
def _dot_p(a, b, ta, tb, hi):
    return _raw_dot(a, b, ta, tb, hi)


def _dot(a, b, ta=False, tb=False, hi=False):
    return _dot_p(a, b, ta, tb, hi)


def _dot_fwd(a, b, ta, tb, hi):
    return _raw_dot(a, b, ta, tb, hi), (a, b)


def _dot_bwd(ta, tb, hi, res, g):
    a, b = res
    if not ta and not tb:
        da, db = _dot(g, b, False, True, hi), _dot(a, g, True, False, hi)
    elif not ta and tb:
        da, db = _dot(g, b, False, False, hi), _dot(g, a, True, False, hi)
    elif ta and not tb:
        da, db = _dot(b, g, False, True, hi), _dot(a, g, False, False, hi)
    else:
        raise NotImplementedError
    return da.astype(a.dtype), db.astype(b.dtype)


_dot_p.defvjp(_dot_fwd, _dot_bwd)


def _iota2(n, m, axis):
    return lax.broadcasted_iota(jnp.int32, (n, m), axis)


def _bcast(mat, nb):
    return jnp.broadcast_to(mat[None], (nb,) + mat.shape)


def _split3(a):
    a0 = a.astype(BF16)
    r1 = a - a0.astype(F32)
    a1 = r1.astype(BF16)
    return a0, a1, (r1 - a1.astype(F32)).astype(BF16)


@functools.partial(jax.custom_vjp, nondiff_argnums=(2,))
def _dot_sel(a, e, te):
    eb = e.astype(BF16)
    p0, p1, p2 = (_raw_dot(p, eb, False, te, False) for p in _split3(a))
    return p0 + (p1 + p2)


def _dot_sel_fwd(a, e, te):
    return _dot_sel(a, e, te), e


def _dot_sel_bwd(te, e, g):
    return _dot_sel(g, e, not te), jnp.zeros_like(e)


_dot_sel.defvjp(_dot_sel_fwd, _dot_sel_bwd)


@jax.custom_vjp
def _sel_dot(e, x):
    eb = e.astype(BF16)
    p0, p1, p2 = (_raw_dot(eb, p, False, False, False) for p in _split3(x))
    return p0 + (p1 + p2)


def _sel_dot_fwd(e, x):
    return _sel_dot(e, x), e


def _sel_dot_bwd(e, g):
    eb = e.astype(BF16)
    p0, p1, p2 = (_raw_dot(eb, p, True, False, False) for p in _split3(g))
    return jnp.zeros_like(e), p0 + (p1 + p2)


_sel_dot.defvjp(_sel_dot_fwd, _sel_dot_bwd)


def _tri_inv_raw(m):
    nb = m.shape[0]
    r, c = _iota2(CHUNK, CHUNK, 0), _iota2(CHUNK, CHUNK, 1)
    t = _bcast((r == c).astype(F32), nb)
    b = 1
    while b < CHUNK:
        sh = b.bit_length() - 1
        off = ((r >> (sh + 1)) == (c >> (sh + 1))) & ((r >> sh) != (c >> sh)) & (r > c)
        cl = jnp.where(off[None], m, 0.0)
        t = t - _raw_dot(_raw_dot(t, cl, False, False, False), t, False, False, False)
        b *= 2
    return t


@jax.custom_vjp
def _tri_inv_given(m, t):
    return t


def _tri_inv_fwd(m, t):
    return t, t


def _tri_inv_bwd(t, g):
    return -_raw_dot(_raw_dot(t, g, True, False, False), t, False, True, False), jnp.zeros_like(t)


_tri_inv_given.defvjp(_tri_inv_fwd, _tri_inv_bwd)


def _rms(h, g):
    return h * lax.rsqrt(jnp.mean(h * h, axis=-1, keepdims=True) + EPS) * g


class In(NamedTuple):
    arr: jax.Array
    spec: pl.BlockSpec
    grad: object = None
    acc: bool = False
    gshape: object = None
    gspec: object = None
    galias: object = None


def _params(grid):
    sem = ("arbitrary",) * len(grid)
    return pltpu.CompilerParams(dimension_semantics=sem, vmem_limit_bytes=VMEM_LIMIT)


def _stage_fwd(name, fn, grid, ins, out_shapes, out_specs):
    n_in = len(ins)

    def body(*refs):
        pids = tuple(pl.program_id(k) for k in range(len(grid)))
        vals = [r[...] for r in refs[:n_in]]
        outs = fn(pids, *vals)
        for o_ref, o in zip(refs[n_in:], outs):
            o_ref[...] = o.reshape(o_ref.shape).astype(o_ref.dtype)

    return _pallas(
        body, out_shape=out_shapes, grid=grid, in_specs=[i.spec for i in ins],
        out_specs=out_specs, name=name, compiler_params=_params(grid))(*[i.arr for i in ins])


def _stage_bwd(name, fn, grid, ins, cots):
    n_in, n_ct = len(ins), len(cots)
    didx = [k for k, i in enumerate(ins) if i.grad is not None]
    aliased = [(o, ins[k].galias) for o, k in enumerate(didx) if ins[k].galias is not None]
    n_al = len(aliased)

    def body(*refs):
        pids = tuple(pl.program_id(k) for k in range(len(grid)))
        vals = [r[...] for r in refs[:n_in]]
        ct_refs = refs[n_in:n_in + n_ct]
        g_refs = refs[n_in + n_ct + n_al:]

        def f(*dv):
            merged = list(vals)
            for k, v in zip(didx, dv):
                merged[k] = v
            return tuple(fn(pids, *merged))

        outs, vjp_fn = jax.vjp(f, *[vals[k].astype(F32) for k in didx])
        cts = tuple(c[...].reshape(o.shape).astype(F32) for c, o in zip(ct_refs, outs))
        grads = vjp_fn(cts)
        first = functools.reduce(jnp.logical_and, [p == 0 for p in pids])
        for k, g_ref, g in zip(didx, g_refs, grads):
            if ins[k].acc:
                @pl.when(first)
                def _(g_ref=g_ref):
                    g_ref[...] = jnp.zeros(g_ref.shape, g_ref.dtype)
                g_ref[...] += g.reshape(g_ref.shape).astype(g_ref.dtype)
            else:
                g_ref[...] = g.reshape(g_ref.shape).astype(g_ref.dtype)

    out_shapes = [jax.ShapeDtypeStruct(ins[k].gshape or ins[k].arr.shape, ins[k].grad) for k in didx]
    out_specs = [ins[k].gspec or ins[k].spec for k in didx]
    return _pallas(
        body, out_shape=out_shapes, grid=grid,
        in_specs=[i.spec for i in ins] + [c[1] for c in cots] + [ANY] * n_al, out_specs=out_specs,
        input_output_aliases={n_in + n_ct + a: o for a, (o, _) in enumerate(aliased)},
        name=name, compiler_params=_params(grid))(*[i.arr for i in ins], *[c[0] for c in cots], *[a for _, a in aliased])


def _full(arr):
    nd = arr.ndim
    return pl.BlockSpec(arr.shape, lambda *p: (0,) * nd)


def _rows(tr, width, blk=0):
    return pl.BlockSpec((tr, width), lambda i: (i, blk))


def _mm(name, a, b, *, tm, tn, tk, out_dtype=F32, add=None, comm=None):
    M, K = a.shape
    N = b.shape[1] if b.ndim == 2 else b.shape[0] * b.shape[2]
    nk = K // tk
    grid = (M // tm, N // tn, nk)
    n_in = 3 if add is not None else 2
    n_ci, n_co = (len(comm.ins), len(comm.outs)) if comm is not None else (0, 0)

    def body(*refs):
        a_ref, b_ref = refs[0], refs[1]
        add_ref = refs[2] if add is not None else None
        c_ins = refs[n_in:n_in + n_ci]
        o_ref = refs[n_in + n_ci]
        c_outs = refs[n_in + n_ci + 1:n_in + n_ci + 1 + n_co]
        scratch = refs[n_in + n_ci + 1 + n_co:]
        acc_ref = scratch[0] if nk > 1 else None
        sems = scratch[1 if nk > 1 else 0:]
        step = (pl.program_id(0) * grid[1] + pl.program_id(1)) * nk + pl.program_id(2)
        if comm is not None:
            @pl.when(step == 0)
            def _():
                comm.start(c_ins, c_outs, sems)

        part = _raw_dot(a_ref[...], b_ref[...], False, False, False)

        def finish(total):
            if add_ref is not None:
                total = total + add_ref[...]
            o_ref[...] = total.astype(o_ref.dtype)

        if nk == 1:
            finish(part)
        else:
            k = pl.program_id(2)

            @pl.when(k == 0)
            def _():
                acc_ref[...] = part

            @pl.when(k > 0)
            def _():
                acc_ref[...] += part

            @pl.when(k == nk - 1)
            def _():
                finish(acc_ref[...])

        if comm is not None:
            @pl.when(step == grid[0] * grid[1] * nk - 1)
            def _():
                comm.finish(c_ins, c_outs, sems)

    b_spec = (pl.BlockSpec((tk, tn), lambda i, j, k: (k, j)) if b.ndim == 2 else
              pl.BlockSpec((None, tk, tn), lambda i, j, k: (j, k, 0)))
    in_specs = [pl.BlockSpec((tm, tk), lambda i, j, k: (i, k)), b_spec]
    args = [a, b]
    if add is not None:
        in_specs.append(pl.BlockSpec((tm, tn), lambda i, j, k: (i, j)))
        args.append(add)
    out_shape = jax.ShapeDtypeStruct((M, N), out_dtype)
    out_spec = pl.BlockSpec((tm, tn), lambda i, j, k: (i, j))
    scratch = [pltpu.VMEM((tm, tn), F32)] if nk > 1 else []
    if comm is None:
        return _pallas(body, out_shape=out_shape, grid=grid, in_specs=in_specs, out_specs=out_spec,
                       scratch_shapes=scratch, name=name, compiler_params=_params(grid))(*args)
    res = _pallas(body, out_shape=[out_shape] + comm.outs, grid=grid, in_specs=in_specs + [ANY] * n_ci,
                  out_specs=[out_spec] + [ANY] * n_co, scratch_shapes=scratch + comm.sems, name=name,
                  compiler_params=_params(grid))(*args, *comm.ins)
    return res[0], res[1:]


def _mm_tn(name, a, b, *, tr, tka, tn, blocked=False):
    R, Ka = a.shape
    N = b.shape[1]
    nr = R // tr
    grid = (Ka // tka, N // tn, nr)
    if blocked:
        out_shape = jax.ShapeDtypeStruct((N // tn, Ka, tn), F32)
        out_spec = pl.BlockSpec((None, tka, tn), lambda i, j, r: (j, i, 0))
    else:
        out_shape = jax.ShapeDtypeStruct((Ka, N), F32)
        out_spec = pl.BlockSpec((tka, tn), lambda i, j, r: (i, j))

    def body(a_ref, b_ref, o_ref):
        r = pl.program_id(2)
        part = _raw_dot(a_ref[...], b_ref[...], True, False, False)

        @pl.when(r == 0)
        def _():
            o_ref[...] = part

        @pl.when(r > 0)
        def _():
            o_ref[...] += part

    return _pallas(
        body, out_shape=out_shape, grid=grid,
        in_specs=[pl.BlockSpec((tr, tka), lambda i, j, r: (r, i)),
                  pl.BlockSpec((tr, tn), lambda i, j, r: (r, j))],
        out_specs=out_spec, name=name, compiler_params=_params(grid))(a, b)


def _conv_fwd(name, x, xcol0, w, b, *, taps, width, tr, tc):
    R = x.shape[0]
    grid = (width // tc, R // tr)
    cb0 = xcol0 // tc
    hb = tr // 8

    def body(*refs):
        x_ref, xp_ref, w_ref = refs[:3]
        b_ref = refs[3] if b is not None else None
        o_ref = refs[-1]
        i = pl.program_id(1)
        xv = x_ref[...]
        prev = jnp.where(i > 0, xp_ref[...], 0.0)
        ext = jnp.concatenate([prev, xv], axis=0)
        acc = xv * w_ref[taps - 1:taps, :]
        for s in range(1, taps):
            acc = acc + pltpu.roll(ext, s, 0)[8:, :] * w_ref[taps - 1 - s:taps - s, :]
        if b_ref is not None:
            acc = acc + b_ref[...]
        o_ref[...] = acc

    in_specs = [pl.BlockSpec((tr, tc), lambda j, i: (i, cb0 + j)),
                pl.BlockSpec((8, tc), lambda j, i: (jnp.maximum(i * hb - 1, 0), cb0 + j)),
                pl.BlockSpec((taps, tc), lambda j, i: (0, j))]
    args = [x, x, w]
    if b is not None:
        in_specs.append(pl.BlockSpec((1, tc), lambda j, i: (0, j)))
        args.append(b)
    return _pallas(
        body, out_shape=jax.ShapeDtypeStruct((R, width), F32), grid=grid, in_specs=in_specs,
        out_specs=pl.BlockSpec((tr, tc), lambda j, i: (i, j)),
        name=name, compiler_params=_params(grid))(*args)


def _conv_bwd(name, x, xcol0, w, dy, *, taps, width, tr, tc, with_bias, dx_into=None):
    R = x.shape[0]
    nr = R // tr
    grid = (width // tc, nr)
    cb0 = xcol0 // tc
    hb = tr // 8
    n_ext = tr + 8
    n_al = 0 if dx_into is None else 1

    def body(*refs):
        x_ref, xp_ref, w_ref, dy_ref, dyn_ref = refs[:5]
        dx_ref, dw_ref = refs[5 + n_al], refs[6 + n_al]
        db_ref = refs[7 + n_al] if with_bias else None
        i = pl.program_id(1)
        xv = x_ref[...]
        ext = jnp.concatenate([jnp.where(i > 0, xp_ref[...], 0.0), xv], axis=0)
        dyv = dy_ref[...]
        dext = jnp.concatenate([dyv, jnp.where(i < nr - 1, dyn_ref[...], 0.0)], axis=0)
        dx = dyv * w_ref[taps - 1:taps, :]
        dws = [None] * taps
        dws[taps - 1] = jnp.sum(xv * dyv, axis=0, keepdims=True)
        for s in range(1, taps):
            dx = dx + pltpu.roll(dext, n_ext - s, 0)[:tr, :] * w_ref[taps - 1 - s:taps - s, :]
            dws[taps - 1 - s] = jnp.sum(pltpu.roll(ext, s, 0)[8:, :] * dyv, axis=0, keepdims=True)
        dx_ref[...] = dx.astype(dx_ref.dtype)

        @pl.when(i == 0)
        def _():
            for k in range(taps):
                dw_ref[k:k + 1, :] = dws[k]
            if db_ref is not None:
                db_ref[...] = jnp.sum(dyv, axis=0, keepdims=True)

        @pl.when(i > 0)
        def _():
            for k in range(taps):
                dw_ref[k:k + 1, :] += dws[k]
            if db_ref is not None:
                db_ref[...] += jnp.sum(dyv, axis=0, keepdims=True)

    in_specs = [pl.BlockSpec((tr, tc), lambda j, i: (i, cb0 + j)),
                pl.BlockSpec((8, tc), lambda j, i: (jnp.maximum(i * hb - 1, 0), cb0 + j)),
                pl.BlockSpec((taps, tc), lambda j, i: (0, j)),
                pl.BlockSpec((tr, tc), lambda j, i: (i, j)),
                pl.BlockSpec((8, tc), lambda j, i: (jnp.minimum((i + 1) * hb, R // 8 - 1), j))]
    args = [x, x, w, dy, dy]
    if dx_into is None:
        dx_shape, dx_spec, aliases = jax.ShapeDtypeStruct((R, width), BF16), pl.BlockSpec((tr, tc), lambda j, i: (i, j)), {}
    else:
        dx_shape = jax.ShapeDtypeStruct(dx_into.shape, dx_into.dtype)
        dx_spec, aliases = pl.BlockSpec((tr, tc), lambda j, i: (i, cb0 + j)), {5: 0}
        in_specs.append(ANY)
        args.append(dx_into)
    out_shape = [dx_shape, jax.ShapeDtypeStruct((taps, width), F32)]
    out_specs = [dx_spec, pl.BlockSpec((taps, tc), lambda j, i: (0, j))]
    if with_bias:
        out_shape.append(jax.ShapeDtypeStruct((1, width), F32))
        out_specs.append(pl.BlockSpec((1, tc), lambda j, i: (0, j)))
    return _pallas(
        body, out_shape=out_shape, grid=grid, in_specs=in_specs, out_specs=out_specs, input_output_aliases=aliases,
        name=name, compiler_params=_params(grid))(*args)


def _row_mask(cfg, i, tr):
    rows = i * tr + lax.broadcasted_iota(jnp.int32, (tr, 1), 0)
    return (rows >= cfg.front).astype(F32)


def _make_rms_fn(cfg, tr, with_residual):
    def fn(pids, h, g):
        hm = h * _row_mask(cfg, pids[0], tr)
        if with_residual:
            return _rms(hm, g), hm
        return (_rms(hm, g),)
    return fn


def _make_gdn_prep_fn(cfg, tr):
    d, hg = cfg.d, cfg.hg

    def fn(pids, c, tail, alog, dtb):
        cq, ck, cv = c[:, :d], c[:, d:2 * d], c[:, 2 * d:]
        mask = _row_mask(cfg, pids[0], tr)
        j, col = _iota2(LANES, d, 0), _iota2(LANES, d, 1)
        ea = ((col >> 7) == j).astype(F32)
        eb = ((col >> 7) + hg == j).astype(F32)
        al = jnp.sum(alog, axis=0, keepdims=True)
        db = jnp.sum(dtb, axis=0, keepdims=True)
        lg = _dot_sel(-jnp.exp(al) * _softplus(tail + db) * mask, ea, False)
        beta = _dot_sel(_sigmoid(tail) * mask, eb, False)
        sq, sk, sv = _silu(cq), _silu(ck), _silu(cv)
        qs, ks = [], []
        for h in range(hg):
            sl = slice(h * GDN_DK, (h + 1) * GDN_DK)
            qh, kh = sq[:, sl], sk[:, sl]
            qs.append(qh * lax.rsqrt(jnp.sum(qh * qh, axis=-1, keepdims=True) + EPS) * (GDN_DK ** -0.5))
            ks.append(kh * lax.rsqrt(jnp.sum(kh * kh, axis=-1, keepdims=True) + EPS))
        return jnp.concatenate(qs, axis=1), jnp.concatenate(ks, axis=1), sv, beta, lg
    return fn


def _gdn_intra_fn(pids, q, k, v, bB, lB, t_saved=None):
    rows = q.shape[0]
    nb = rows // CHUNK
    q3, k3, v3, b3, l3 = [t.reshape(nb, CHUNK, GDN_DK) for t in (q, k, v, bB, lB)]
    r, c = _iota2(CHUNK, CHUNK, 0), _iota2(CHUNK, CHUNK, 1)
    tril = (r >= c)
    strict = (r > c)
    gcol = _sel_dot(_bcast(tril.astype(F32), nb), l3)
    l64 = l3[:, :, :CHUNK]
    grow = _sel_dot(jnp.ones((nb, CHUNK, CHUNK), F32), l64 * (r <= c).astype(F32)[None])
    diff = gcol[:, :, :CHUNK] - grow
    decay = jnp.where(tril[None], jnp.exp(jnp.where(tril[None], diff, 0.0)), 0.0)
    kb = k3 * b3
    m = jnp.where(strict[None], _dot(kb, k3, False, True) * decay, 0.0)
    t = _tri_inv_raw(m) if t_saved is None else _tri_inv_given(m, t_saved.reshape(nb, CHUNK, CHUNK))
    eg = jnp.exp(gcol)
    u = _dot(t, v3 * b3)
    w = _dot(t, kb * eg)
    attn = _dot(q3, k3, False, True) * decay
    qd = q3 * eg
    glast = jnp.sum(l3, axis=1, keepdims=True)
    kd = k3 * jnp.exp(glast - gcol)
    gl = jnp.exp(glast)
    outs = (u.reshape(rows, GDN_DK), w.reshape(rows, GDN_DK), attn.reshape(1, rows, CHUNK),
            qd.reshape(rows, GDN_DK), kd.reshape(rows, GDN_DK), gl.reshape(1, nb, 1, GDN_DK))
    return outs + (t.reshape(1, rows, CHUNK),) if t_saved is None else outs


def _make_rot_fn(cfg):
    hr = cfg.hr
    half = RET_DK // 2

    def fn(pids, rqk, cos, sin):
        rq, rk = rqk[:, :cfg.d], rqk[:, cfg.d:]

        def rot(t, scale):
            outs = []
            for h in range(hr):
                x1 = t[:, h * RET_DK:h * RET_DK + half]
                x2 = t[:, h * RET_DK + half:(h + 1) * RET_DK]
                outs += [(x1 * cos - x2 * sin) * scale, (x2 * cos + x1 * sin) * scale]
            return jnp.concatenate(outs, axis=1)
        return rot(rq, 1.0), rot(rk, RET_DK ** -0.5)
    return fn


def _make_mix_fn(cfg):
    hg, hr = cfg.hg, cfg.hr

    def fn(pids, oa, ob, pm, gnorm):
        d = cfg.d
        gz, rg, gate_a, gate_b = pm[:, :d], pm[:, d:2 * d], pm[:, 2 * d:3 * d], pm[:, 3 * d:]
        oas = []
        for h in range(hg):
            oh = oa[:, h * GDN_DK:(h + 1) * GDN_DK]
            oas.append(oh * lax.rsqrt(jnp.mean(oh * oh, axis=-1, keepdims=True) + EPS) * gnorm)
        ya = jnp.concatenate(oas, axis=1) * _silu(gz)
        obs = []
        for h in range(hr):
            oh = ob[:, h * RET_DK:(h + 1) * RET_DK]
            obs.append(oh * lax.rsqrt(jnp.mean(oh * oh, axis=-1, keepdims=True) + EPS))
        yb = _silu(rg) * jnp.concatenate(obs, axis=1)
        return (_sigmoid(gate_a) * ya + _sigmoid(gate_b) * yb,)
    return fn


def _act_fn(pids, u):
    f = u.shape[1] // 2
    return (_silu(u[:, :f]) * u[:, f:],)


def _gdn_step(s, u, w, a, qd, kd, gl):
    v_new = u - _dot(w, s)
    o = _dot(qd, s) + _dot(a, v_new)
    s2 = s * gl + _dot(kd, v_new, True, False)
    return s2, o


def _ret_step(s, q, k, v, dm, qdc, kdc, g):
    inner = _dot(_dot(q, k, False, True) * dm, v)
    o = inner + _dot(q * qdc, s)
    s2 = s * g + _dot(k * kdc, v, True, False)
    return s2, o


def _gdn_scan_fwd(cfg, u, w, attn, qd, kd, gl):
    d, hg, nch, sc = cfg.d, cfg.hg, cfg.nch, cfg.sc
    nst = nch // sc

    def body(u_ref, w_ref, a_ref, qd_ref, kd_ref, gl_ref, o_ref, ss_ref, s_ref):
        @pl.when(pl.program_id(0) == 0)
        def _():
            s_ref[...] = jnp.zeros(s_ref.shape, F32)

        states = [s_ref[h] for h in range(hg)]
        for j in range(sc):
            rows = slice(j * CHUNK, (j + 1) * CHUNK)
            outs = []
            for h in range(hg):
                sl = slice(h * GDN_DK, (h + 1) * GDN_DK)
                ss_ref[j, h] = states[h]
                states[h], o = _gdn_step(states[h], u_ref[rows, sl], w_ref[rows, sl], a_ref[h, rows, :],
                                         qd_ref[rows, sl], kd_ref[rows, sl], gl_ref[h, j])
                outs.append(o)
            o_ref[rows, :] = jnp.concatenate(outs, axis=1)
        for h in range(hg):
            s_ref[h] = states[h]

    row = pl.BlockSpec((sc * CHUNK, d), lambda n: (n, 0))
    return _pallas(
        body,
        out_shape=[jax.ShapeDtypeStruct((cfg.rp, d), F32), jax.ShapeDtypeStruct((nch, hg, GDN_DK, GDN_DK), F32)],
        grid=(nst,),
        in_specs=[row, row, pl.BlockSpec((hg, sc * CHUNK, CHUNK), lambda n: (0, n, 0)), row, row,
                  pl.BlockSpec((hg, sc, 1, GDN_DK), lambda n: (0, n, 0, 0))],
        out_specs=[row, pl.BlockSpec((sc, hg, GDN_DK, GDN_DK), lambda n: (n, 0, 0, 0))],
        scratch_shapes=[pltpu.VMEM((hg, GDN_DK, GDN_DK), F32)],
        name="gdn_scan_fwd", compiler_params=_params((nst,)))(u, w, attn, qd, kd, gl)


def _gdn_scan_bwd(cfg, do, u, w, attn, qd, kd, gl, ss):
    d, hg, nch, sc = cfg.d, cfg.hg, cfg.nch, cfg.sc
    nst = nch // sc

    def body(do_ref, u_ref, w_ref, a_ref, qd_ref, kd_ref, gl_ref, ss_ref,
             du_ref, dw_ref, da_ref, dqd_ref, dkd_ref, dgl_ref, ds_ref):
        @pl.when(pl.program_id(0) == 0)
        def _():
            ds_ref[...] = jnp.zeros(ds_ref.shape, F32)

        dstates = [ds_ref[h] for h in range(hg)]
        for j in reversed(range(sc)):
            rows = slice(j * CHUNK, (j + 1) * CHUNK)
            dus, dws, dqds, dkds = [], [], [], []
            for h in range(hg):
                sl = slice(h * GDN_DK, (h + 1) * GDN_DK)
                args = (ss_ref[j, h], u_ref[rows, sl], w_ref[rows, sl], a_ref[h, rows, :], qd_ref[rows, sl],
                        kd_ref[rows, sl], gl_ref[h, j])
                _, vjp_fn = jax.vjp(_gdn_step, *args)
                dstates[h], du, dw, da, dqd, dkd, dgl = vjp_fn((dstates[h], do_ref[rows, sl]))
                da_ref[h, rows, :] = da
                dgl_ref[h, j] = dgl
                dus.append(du)
                dws.append(dw)
                dqds.append(dqd)
                dkds.append(dkd)
            du_ref[rows, :] = jnp.concatenate(dus, axis=1)
            dw_ref[rows, :] = jnp.concatenate(dws, axis=1)
            dqd_ref[rows, :] = jnp.concatenate(dqds, axis=1)
            dkd_ref[rows, :] = jnp.concatenate(dkds, axis=1)
        for h in range(hg):
            ds_ref[h] = dstates[h]

    row = pl.BlockSpec((sc * CHUNK, d), lambda n: (nst - 1 - n, 0))
    aspec = pl.BlockSpec((hg, sc * CHUNK, CHUNK), lambda n: (0, nst - 1 - n, 0))
    gspec = pl.BlockSpec((hg, sc, 1, GDN_DK), lambda n: (0, nst - 1 - n, 0, 0))
    rowshape = jax.ShapeDtypeStruct((cfg.rp, d), F32)
    return _pallas(
        body,
        out_shape=[rowshape, rowshape, jax.ShapeDtypeStruct(attn.shape, F32), rowshape, rowshape,
                   jax.ShapeDtypeStruct(gl.shape, F32)],
        grid=(nst,),
        in_specs=[row, row, row, aspec, row, row, gspec,
                  pl.BlockSpec((sc, hg, GDN_DK, GDN_DK), lambda n: (nst - 1 - n, 0, 0, 0))],
        out_specs=[row, row, aspec, row, row, gspec],
        scratch_shapes=[pltpu.VMEM((hg, GDN_DK, GDN_DK), F32)],
        name="gdn_scan_bwd", compiler_params=_params((nst,)))(do, u, w, attn, qd, kd, gl, ss)


def _ret_consts(cfg):
    hr = cfg.hr
    lg = np.log(1.0 - 2.0 ** (-5.0 - np.arange(hr, dtype=np.float64)))
    idx = np.arange(CHUNK, dtype=np.float64)
    tril = np.tril(np.ones((CHUNK, CHUNK), dtype=bool))
    dm = np.where(tril[None], np.exp((idx[:, None] - idx[None, :])[None] * lg[:, None, None]), 0.0)
    qdc = np.exp((idx[None, :] + 1.0) * lg[:, None])
    kdc = np.exp((CHUNK - 1.0 - idx[None, :]) * lg[:, None])
    gch = np.exp(CHUNK * lg)
    qdc = np.broadcast_to(qdc[:, :, None], (hr, CHUNK, RET_DK))
    kdc = np.broadcast_to(kdc[:, :, None], (hr, CHUNK, RET_DK))
    gch = np.broadcast_to(gch[:, None, None], (hr, 1, RET_DK))
    return tuple(jnp.asarray(np.ascontiguousarray(t), F32) for t in (dm, qdc, kdc, gch))


def _ret_scan_fwd(cfg, qr, kr, proj, consts):
    d, hr, nch, sc = cfg.d, cfg.hr, cfg.nch, cfg.sc
    nst = nch // sc
    dm, qdc, kdc, gch = consts

    def body(q_ref, k_ref, v_ref, dm_ref, qdc_ref, kdc_ref, g_ref, o_ref, ss_ref, s_ref):
        @pl.when(pl.program_id(0) == 0)
        def _():
            s_ref[...] = jnp.zeros(s_ref.shape, F32)

        states = [s_ref[h] for h in range(hr)]
        for j in range(sc):
            rows = slice(j * CHUNK, (j + 1) * CHUNK)
            outs = []
            for h in range(hr):
                sl = slice(h * RET_DK, (h + 1) * RET_DK)
                ss_ref[j, h] = states[h]
                states[h], o = _ret_step(states[h], q_ref[rows, sl], k_ref[rows, sl], v_ref[rows, sl], dm_ref[h],
                                         qdc_ref[h], kdc_ref[h], g_ref[h])
                outs.append(o)
            o_ref[rows, :] = jnp.concatenate(outs, axis=1)
        for h in range(hr):
            s_ref[h] = states[h]

    row = pl.BlockSpec((sc * CHUNK, d), lambda n: (n, 0))
    return _pallas(
        body,
        out_shape=[jax.ShapeDtypeStruct((cfg.rp, d), F32), jax.ShapeDtypeStruct((nch, hr, RET_DK, RET_DK), F32)],
        grid=(nst,),
        in_specs=[row, row, pl.BlockSpec((sc * CHUNK, d), lambda n: (n, RV_BLOCK)), _full(dm), _full(qdc), _full(kdc),
                  _full(gch)],
        out_specs=[row, pl.BlockSpec((sc, hr, RET_DK, RET_DK), lambda n: (n, 0, 0, 0))],
        scratch_shapes=[pltpu.VMEM((hr, RET_DK, RET_DK), F32)],
        name="ret_scan_fwd", compiler_params=_params((nst,)))(qr, kr, proj, dm, qdc, kdc, gch)


def _ret_scan_bwd(cfg, do, qr, kr, proj, consts, ss, dproj):
    d, hr, nch, sc = cfg.d, cfg.hr, cfg.nch, cfg.sc
    nst = nch // sc
    dm, qdc, kdc, gch = consts

    def body(do_ref, q_ref, k_ref, v_ref, dm_ref, qdc_ref, kdc_ref, g_ref, ss_ref, _, dq_ref, dk_ref, dv_ref, ds_ref):
        @pl.when(pl.program_id(0) == 0)
        def _():
            ds_ref[...] = jnp.zeros(ds_ref.shape, F32)

        dstates = [ds_ref[h] for h in range(hr)]
        for j in reversed(range(sc)):
            rows = slice(j * CHUNK, (j + 1) * CHUNK)
            dqs, dks, dvs = [], [], []
            for h in range(hr):
                sl = slice(h * RET_DK, (h + 1) * RET_DK)
                cs = (dm_ref[h], qdc_ref[h], kdc_ref[h], g_ref[h])
                _, vjp_fn = jax.vjp(lambda s, q, k, v, cs=cs: _ret_step(s, q, k, v, *cs),
                                    ss_ref[j, h], q_ref[rows, sl], k_ref[rows, sl], v_ref[rows, sl])
                dstates[h], dq, dk, dv = vjp_fn((dstates[h], do_ref[rows, sl]))
                dqs.append(dq)
                dks.append(dk)
                dvs.append(dv)
            dq_ref[rows, :] = jnp.concatenate(dqs, axis=1)
            dk_ref[rows, :] = jnp.concatenate(dks, axis=1)
            dv_ref[rows, :] = jnp.concatenate(dvs, axis=1).astype(dv_ref.dtype)
        for h in range(hr):
            ds_ref[h] = dstates[h]

    row = pl.BlockSpec((sc * CHUNK, d), lambda n: (nst - 1 - n, 0))
    rowshape = jax.ShapeDtypeStruct((cfg.rp, d), F32)
    vspec = pl.BlockSpec((sc * CHUNK, d), lambda n: (nst - 1 - n, RV_BLOCK))
    return _pallas(
        body,
        out_shape=[rowshape, rowshape, jax.ShapeDtypeStruct(dproj.shape, dproj.dtype)],
        grid=(nst,),
        in_specs=[row, row, row, vspec, _full(dm), _full(qdc), _full(kdc), _full(gch),
                  pl.BlockSpec((sc, hr, RET_DK, RET_DK), lambda n: (nst - 1 - n, 0, 0, 0)), ANY],
        out_specs=[row, row, vspec], input_output_aliases={9: 2},
        scratch_shapes=[pltpu.VMEM((hr, RET_DK, RET_DK), F32)],
        name="ret_scan_bwd", compiler_params=_params((nst,)))(do, qr, kr, proj, dm, qdc, kdc, gch, ss, dproj)


def _final(cfg, h2, normf, tgt):
    d, tr = cfg.d, cfg.xrow
    nr = cfg.rp // tr

    def body(h_ref, g_ref, t_ref, dh_ref, dg_ref, loss_ref):
        i = pl.program_id(0)
        y, vjp_fn = jax.vjp(_rms, h_ref[...], g_ref[...])
        err = jnp.where(i >= 1, y - t_ref[...], 0.0)
        dh, dg = vjp_fn(err * (1.0 / d))
        dh_ref[...] = dh
        part = jnp.zeros((8, LANES), F32) + 0.5 * jnp.sum(err * err) * (1.0 / d)

        @pl.when(i == 0)
        def _():
            dg_ref[...] = dg
            loss_ref[...] = part

        @pl.when(i > 0)
        def _():
            dg_ref[...] += dg
            loss_ref[...] += part

    return _pallas(
        body,
        out_shape=[jax.ShapeDtypeStruct((cfg.rp, d), F32), jax.ShapeDtypeStruct((1, d), F32),
                   jax.ShapeDtypeStruct((8, LANES), F32)],
        grid=(nr,),
        in_specs=[_rows(tr, d), _full(normf), pl.BlockSpec((tr, d), lambda i: (jnp.maximum(i - 1, 0), 0))],
        out_specs=[_rows(tr, d), pl.BlockSpec((1, d), lambda i: (0, 0)), pl.BlockSpec((8, LANES), lambda i: (0, 0))],
        name="final_loss", compiler_params=_params((nr,)))(h2, normf, tgt)


ANY = pl.BlockSpec(memory_space=pl.ANY)


def _place():
    x, y, c = lax.axis_index("x"), lax.axis_index("y"), lax.axis_index("c")
    others = [(1 - x, y), (x, 1 - y), (1 - x, 1 - y)]
    return x, y, c, others


def _row_tile(rows, cap=256):
    return max(t for t in range(16, min(rows, cap) + 1, 16) if rows % t == 0)


class Comm(NamedTuple):
    ins: list
    outs: list
    sems: list
    start: object
    finish: object


def _run_comm(name, comm):
    n_in, n_out = len(comm.ins), len(comm.outs)

    def body(*refs):
        ins, outs, sems = refs[:n_in], refs[n_in:n_in + n_out], refs[n_in + n_out:]
        comm.start(ins, outs, sems)
        comm.finish(ins, outs, sems)

    return _pallas(body, out_shape=comm.outs, in_specs=[ANY] * n_in, out_specs=[ANY] * n_out,
                   scratch_shapes=comm.sems, name=name)(*comm.ins)


def _gather_comm(ws):
    n = len(ws)
    halves = [w.shape[0] // 2 for w in ws]

    def copies(w_refs, o_refs, sems):
        send_sems, recv_sems = sems
        x, y, c, others = _place()
        me = 2 * x + y
        chips = [2 * px + py for px, py in others]

        def piece(a, chip, core):
            return o_refs[a].at[chip, pl.ds(core * halves[a], halves[a]), :]

        def copy(a, k, src, chip, core, to):
            return pltpu.make_async_remote_copy(src_ref=src, dst_ref=piece(a, chip, core), send_sem=send_sems.at[6 * a + k],
                                                recv_sem=recv_sems.at[6 * a + k], device_id=to, device_id_type=MESH)

        def first(j, a):
            return copy(a, j, w_refs[a].at[pl.ds(c * halves[a], halves[a]), :], me, c, (*others[j], c))

        def landed(j, a):
            return copy(a, j, piece(a, chips[j], c), chips[j], c, (x, y, c))

        def passed(j, a):
            return copy(a, 3 + j, piece(a, chips[j], c), chips[j], c, (x, y, 1 - c))

        def from_sibling(j, a):
            return copy(a, 3 + j, piece(a, chips[j], 1 - c), chips[j], 1 - c, (x, y, c))

        return first, landed, passed, from_sibling

    pairs = [(j, a) for j in range(3) for a in range(n)]

    def start(w_refs, o_refs, sems):
        first, _, _, _ = copies(w_refs, o_refs, sems)
        for j, a in pairs:
            first(j, a).start()

    def finish(w_refs, o_refs, sems):
        first, landed, passed, from_sibling = copies(w_refs, o_refs, sems)
        for j, a in pairs:
            landed(j, a).wait_recv()
            passed(j, a).start()
        for j, a in pairs:
            from_sibling(j, a).wait_recv()
        for j, a in pairs:
            first(j, a).wait_send()
            passed(j, a).wait_send()

    return Comm(list(ws), [jax.ShapeDtypeStruct((N_CHIPS,) + w.shape, w.dtype) for w in ws],
                [pltpu.SemaphoreType.DMA((6 * n,)), pltpu.SemaphoreType.DMA((6 * n,))], start, finish)


def _pair_exchange(name, gs):
    n = len(gs)

    def body(*refs):
        g_refs, o_refs = refs[:n], refs[n:2 * n]
        send_sems, recv_sems = refs[2 * n:]
        x, y, c, _ = _place()
        cps = []
        for a in range(n):
            half = gs[a].shape[1] // 2
            cp = pltpu.make_async_remote_copy(
                src_ref=g_refs[a].at[:, pl.ds((1 - c) * half, half), :], dst_ref=o_refs[a], send_sem=send_sems.at[a],
                recv_sem=recv_sems.at[a], device_id=(x, y, 1 - c), device_id_type=MESH)
            cp.start()
            cps.append(cp)
        for cp in cps:
            cp.wait()

    return _pallas(
        body, out_shape=[jax.ShapeDtypeStruct((N_CHIPS, g.shape[1] // 2, g.shape[2]), g.dtype) for g in gs],
        in_specs=[ANY] * n, out_specs=[ANY] * n,
        scratch_shapes=[pltpu.SemaphoreType.DMA((n,)), pltpu.SemaphoreType.DMA((n,))], name=name)(*gs)


def _pair_sum(name, g, recv, cidx):
    half, cols = recv.shape[1], recv.shape[2]
    tr = _row_tile(half)
    nblk = half // tr

    def body(c_ref, g_ref, r_ref, o_ref):
        o_ref[...] = (g_ref[...] + r_ref[...]).astype(o_ref.dtype)

    grid_spec = pltpu.PrefetchScalarGridSpec(
        num_scalar_prefetch=1, grid=(N_CHIPS, nblk),
        in_specs=[pl.BlockSpec((1, tr, cols), lambda s, i, c: (s, c[0] * nblk + i, 0)),
                  pl.BlockSpec((1, tr, cols), lambda s, i, c: (s, i, 0))],
        out_specs=pl.BlockSpec((1, tr, cols), lambda s, i, c: (s, i, 0)))
    return _pallas(
        body, out_shape=jax.ShapeDtypeStruct((N_CHIPS, half, cols), BF16), grid_spec=grid_spec,
        name=name, compiler_params=_params((N_CHIPS, nblk)))(cidx, g, recv)


def _exchange_comm(parts):
    n = len(parts)

    def copies(p_refs, o_refs, sems):
        send_sems, recv_sems = sems
        x, y, c, others = _place()
        me = 2 * x + y

        def copy(a, j, src_chip, dst_chip):
            px, py = others[j]
            return pltpu.make_async_remote_copy(
                src_ref=p_refs[a].at[src_chip], dst_ref=o_refs[a].at[dst_chip], send_sem=send_sems.at[3 * a + j],
                recv_sem=recv_sems.at[3 * a + j], device_id=(px, py, c), device_id_type=MESH)

        def send(j, a):
            return copy(a, j, 2 * others[j][0] + others[j][1], me)

        def arrival(j, a):
            return copy(a, j, me, 2 * others[j][0] + others[j][1])

        return send, arrival

    pairs = [(j, a) for j in range(3) for a in range(n)]

    def start(p_refs, o_refs, sems):
        send, _ = copies(p_refs, o_refs, sems)
        for j, a in pairs:
            send(j, a).start()

    def finish(p_refs, o_refs, sems):
        send, arrival = copies(p_refs, o_refs, sems)
        for j, a in pairs:
            arrival(j, a).wait_recv()
        for j, a in pairs:
            send(j, a).wait_send()

    return Comm(list(parts), [jax.ShapeDtypeStruct(p.shape, p.dtype) for p in parts],
                [pltpu.SemaphoreType.DMA((3 * n,)), pltpu.SemaphoreType.DMA((3 * n,))], start, finish)


def _chip_sum(name, part, slots, chip):
    half, cols = slots.shape[1], slots.shape[2]
    tr = _row_tile(half)

    def body(me_ref, p_ref, *rest):
        s_refs, o_ref = rest[:N_CHIPS], rest[N_CHIPS]
        own = p_ref[...].astype(F32)
        v = [jnp.where(me_ref[0] == k, own, s_refs[k][...].astype(F32)) for k in range(N_CHIPS)]
        o_ref[...] = ((v[0] + v[1]) + v[2]) + v[3]

    def slot_spec(k):
        return pl.BlockSpec((None, tr, cols), lambda i, me: (jnp.where(me[0] == k, (k + 1) % N_CHIPS, k), i, 0))

    grid_spec = pltpu.PrefetchScalarGridSpec(
        num_scalar_prefetch=1, grid=(half // tr,),
        in_specs=[pl.BlockSpec((None, tr, cols), lambda i, me: (me[0], i, 0))] + [slot_spec(k) for k in range(N_CHIPS)],
        out_specs=pl.BlockSpec((tr, cols), lambda i, me: (i, 0)))
    return _pallas(
        body, out_shape=jax.ShapeDtypeStruct((half, cols), F32), grid_spec=grid_spec,
        name=name, compiler_params=_params((half // tr,)))(chip, part, *([slots] * N_CHIPS))


def _pair_swap(fins):
    n = len(fins)

    def body(*refs):
        f_refs, o_refs = refs[:n], refs[n:2 * n]
        send_sems, recv_sems = refs[2 * n:]
        x, y, c, _ = _place()
        cps = [pltpu.make_async_remote_copy(src_ref=f_refs[a], dst_ref=o_refs[a], send_sem=send_sems.at[a],
                                            recv_sem=recv_sems.at[a], device_id=(x, y, 1 - c), device_id_type=MESH)
               for a in range(n)]
        for cp in cps:
            cp.start()
        for cp in cps:
            cp.wait()

    return _pallas(
        body, out_shape=[jax.ShapeDtypeStruct(f.shape, f.dtype) for f in fins], in_specs=[ANY] * n, out_specs=[ANY] * n,
        scratch_shapes=[pltpu.SemaphoreType.DMA((n,)), pltpu.SemaphoreType.DMA((n,))], name="grad_pair_swap")(*fins)


def _adamw(name, w, g_own, g_other, m, v, cidx):
    R, cols = w.shape
    half = R // 2
    tr = _row_tile(half, 128)
    nblk = half // tr
    c1 = 1.0 - ADAM_B1 ** ADAM_STEP
    c2 = 1.0 - ADAM_B2 ** ADAM_STEP

    def body(c_ref, w_ref, go_ref, gs_ref, m_ref, v_ref, g_ref, d_ref, nm_ref, nv_ref):
        mine = (pl.program_id(0) // nblk) == c_ref[0]
        gv = jnp.where(mine, go_ref[...], gs_ref[...])
        nm = ADAM_B1 * m_ref[...] + (1.0 - ADAM_B1) * gv
        nv = ADAM_B2 * v_ref[...] + (1.0 - ADAM_B2) * (gv * gv)
        g_ref[...] = gv
        d_ref[...] = -ADAM_LR * ((nm / c1) / (jnp.sqrt(nv / c2) + ADAM_EPS) + ADAM_WD * w_ref[...])
        nm_ref[...] = nm
        nv_ref[...] = nv

    spec = pl.BlockSpec((tr, cols), lambda i, c: (i, 0))
    hspec = pl.BlockSpec((tr, cols), lambda i, c: (i % nblk, 0))
    shape = jax.ShapeDtypeStruct((R, cols), F32)
    grid_spec = pltpu.PrefetchScalarGridSpec(num_scalar_prefetch=1, grid=(R // tr,),
                                             in_specs=[spec, hspec, hspec, spec, spec], out_specs=[spec] * 4)
    return _pallas(
        body, out_shape=[shape] * 4, grid_spec=grid_spec,
        name=name, compiler_params=_params((R // tr,)))(cidx, w, g_own, g_other, m, v)


PARAMS = (("meta", 1), ("norm1", None), ("w_in", 2), ("gdn_conv_w", 2), ("gdn_a_log", None), ("gdn_dt_bias", None),
          ("gdn_norm", None), ("w_out", 1), ("norm2", None), ("w_ffn_up", 2), ("ffn_conv_w", 2), ("ffn_conv_b", None),
          ("w_ffn_down", 1), ("norm_f", None))
BIG = ("w_in", "w_out", "w_ffn_up", "w_ffn_down")
PACK_ALIGN = 1024
PACK_ROWS_ALIGN = 32


def _pack(arrs, dtype):
    parts, total = [], 0
    for a in arrs:
        f = a.reshape(-1).astype(dtype)
        pad = (-f.shape[0]) % PACK_ALIGN
        parts.append(jnp.pad(f, (0, pad)) if pad else f)
        total += f.shape[0] + pad
    rows = total // LANES
    rpad = (-rows) % PACK_ROWS_ALIGN
    if rpad:
        parts.append(jnp.zeros((rpad * LANES,), dtype))
    return jnp.concatenate(parts).reshape(rows + rpad, LANES)


def _unpack(buf, shapes):
    flat = buf.reshape(-1)
    outs, off = [], 0
    for s in shapes:
        n = int(np.prod(s))
        outs.append(flat[off:off + n].reshape(s))
        off += n + (-n) % PACK_ALIGN
    return outs


def _split4(a, axis):
    n = a.shape[axis] // N_CHIPS
    return [lax.slice_in_dim(a, s * n, (s + 1) * n, axis=axis) for s in range(N_CHIPS)]


PROJ_ORDER = (3, 7, 8, 9, 0, 1, 2, 6, 4, 5)


def _reorder_w_in(w, cfg):
    d, hg = cfg.d, cfg.hg

    def block(k):
        off = k * d + (2 * hg if k >= 4 else 0)
        return w[:, off:off + d]

    tail = jnp.pad(w[:, 4 * d:4 * d + 2 * hg], ((0, 0), (0, LANES - 2 * hg)))
    return jnp.concatenate([block(k) for k in PROJ_ORDER] + [tail], axis=1)


def _restore_w_in(wr, cfg):
    d, hg = cfg.d, cfg.hg
    at = {k: i for i, k in enumerate(PROJ_ORDER)}
    block = lambda k: wr[:, at[k] * d:(at[k] + 1) * d]
    return jnp.concatenate([block(k) for k in range(4)] + [wr[:, 10 * d:10 * d + 2 * hg]] +
                           [block(k) for k in range(4, 10)], axis=1)


def _step(cfg, x, tgt, shard, m_shard, v_shard):
    d, hg, hr, dff, rp, tr, tm = cfg.d, cfg.hg, cfg.hr, cfg.dff, cfg.rp, cfg.tr, cfg.tm
    nrow = rp // tr
    assert cfg.tf * N_CHIPS == 2 * dff and cfg.din % N_CHIPS == 0
    cidx = lax.axis_index("c").astype(jnp.int32).reshape(1)
    chip = (2 * lax.axis_index("x") + lax.axis_index("y")).astype(jnp.int32).reshape(1)

    axis = dict(PARAMS)
    small = ("meta", "gdn_conv_w", "ffn_conv_w")
    small_shapes = [shard[n].shape for n in small]
    mine = [shard[n][0].astype(BF16) for n in BIG] + [_pack([shard[n] for n in small], F32)]

    def with_own(gathered, own):
        return [lax.dynamic_update_slice(g, w[None], (chip[0], 0, 0)) for g, w in zip(gathered, own)]

    g_in, g_small = with_own(_run_comm("weights_gather_first", _gather_comm([mine[0], mine[4]])), [mine[0], mine[4]])
    w_in_r = _reorder_w_in(jnp.concatenate([g_in[s] for s in range(N_CHIPS)], axis=1), cfg)
    per_chip = [_unpack(g_small[s], small_shapes) for s in range(N_CHIPS)]
    full = {n: jnp.concatenate([per_chip[s][k] for s in range(N_CHIPS)], axis=axis[n]) for k, n in enumerate(small)}
    meta = full["meta"]
    gconv_w = full["gdn_conv_w"][0]
    fconv_w = full["ffn_conv_w"][0]
    norm1, norm2, gnorm = shard["norm1"], shard["norm2"], shard["gdn_norm"]
    normf = shard["norm_f"].reshape(1, d)
    fconv_b = shard["ffn_conv_b"]
    alog = jnp.pad(shard["gdn_a_log"], ((0, 7), (0, LANES - hg)))
    dtb = jnp.pad(shard["gdn_dt_bias"], ((0, 7), (0, LANES - hg)))

    h0 = jnp.concatenate([jnp.zeros((cfg.front, d), F32), meta, x], axis=0)
    pos = jnp.arange(rp, dtype=F32) - float(cfg.front)
    half = RET_DK // 2
    inv = 1.0 / (ROPE_BASE ** (jnp.arange(half, dtype=F32) / half))
    ang = pos[:, None] * inv[None, :]
    cos, sin = jnp.cos(ang), jnp.sin(ang)
    rconsts = _ret_consts(cfg)

    rms_f = _make_rms_fn(cfg, tr, False)
    rms_b = _make_rms_fn(cfg, tr, True)
    rowshape = jax.ShapeDtypeStruct((rp, d), F32)
    rspec = _rows(tr, d)

    def rms_fwd(name, h, g):
        return _stage_fwd(name, rms_f, (nrow,), [In(h, rspec), In(g, _full(g))],
                          [jax.ShapeDtypeStruct((rp, d), BF16)], [rspec])[0]

    tn_in = cfg.pw // 9 if cfg.pw % (9 * LANES) == 0 else LANES
    hn1 = rms_fwd("rms1_fwd", h0, norm1)
    proj, rest = _mm("proj_fwd", hn1, w_in_r, tm=tm, tn=tn_in, tk=d, comm=_gather_comm(mine[1:4]))
    g_out, g_up, g_down = with_own(rest, mine[1:4])
    w_out = g_out.reshape(d, d)
    w_up = g_up
    w_up_t = jnp.swapaxes(g_up, 1, 2).reshape(2 * dff, d)
    w_down = g_down.reshape(dff, d)
    cqkv = _conv_fwd("gdn_conv_fwd", proj, CONV_COL * d, gconv_w, None, taps=GDN_CONV, width=3 * d, tr=tr, tc=d)
    tail_spec = _rows(tr, LANES, TAIL_COL * d // LANES)
    prep_fn = _make_gdn_prep_fn(cfg, tr)

    def prep_ins(dproj=None):
        return [In(cqkv, _rows(tr, 3 * d), F32), In(proj, tail_spec, BF16, galias=dproj, gshape=(rp, cfg.pw)),
                In(alog, _full(alog), F32, True), In(dtb, _full(dtb), F32, True)]

    qn, kn, vv, bB, lB = _stage_fwd("gdn_prep_fwd", prep_fn, (nrow,), prep_ins(), [rowshape] * 5, [rspec] * 5)

    trg = cfg.nb * CHUNK
    gi_grid = (rp // trg, hg)
    hspec = pl.BlockSpec((trg, GDN_DK), lambda i, h: (i, h))
    aspec = pl.BlockSpec((1, trg, CHUNK), lambda i, h: (h, i, 0))
    gspec = pl.BlockSpec((1, cfg.nb, 1, GDN_DK), lambda i, h: (h, i, 0, 0))
    intra_ins = [In(t, hspec, F32) for t in (qn, kn, vv, bB, lB)]
    ashape = jax.ShapeDtypeStruct((hg, rp, CHUNK), F32)
    intra_shapes = [rowshape, rowshape, ashape, rowshape, rowshape, jax.ShapeDtypeStruct((hg, cfg.nch, 1, GDN_DK), F32), ashape]
    intra_specs = [hspec, hspec, aspec, hspec, hspec, gspec, aspec]
    gu, gw, gattn, gqd, gkd, ggl, gtinv = _stage_fwd("gdn_intra_fwd", _gdn_intra_fn, gi_grid, intra_ins, intra_shapes,
                                                     intra_specs)
    oa, gss = _gdn_scan_fwd(cfg, gu, gw, gattn, gqd, gkd, ggl)

    rot_fn = _make_rot_fn(cfg)

    def rot_ins(dproj=None):
        return [In(proj, _rows(tr, 2 * d, ROT_COL // 2), BF16, galias=dproj, gshape=(rp, cfg.pw)),
                In(cos, _rows(tr, half)), In(sin, _rows(tr, half))]

    qr, kr = _stage_fwd("rot_fwd", rot_fn, (nrow,), rot_ins(), [rowshape] * 2, [rspec] * 2)
    ob, rss = _ret_scan_fwd(cfg, qr, kr, proj, rconsts)

    mix_fn = _make_mix_fn(cfg)
    mix_ins = [In(oa, rspec, F32), In(ob, rspec, F32), In(proj, _rows(tr, 4 * d, MIX_COL // 4), BF16, gshape=(rp, cfg.pw)),
               In(gnorm, _full(gnorm), F32, True)]
    ymix = _stage_fwd("mix_fwd", mix_fn, (nrow,), mix_ins, [jax.ShapeDtypeStruct((rp, d), BF16)], [rspec])[0]
    h1 = _mm("out_proj_fwd", ymix, w_out, tm=tm, tn=d, tk=d, add=h0)

    hn2 = rms_fwd("rms2_fwd", h1, norm2)
    up = _mm("ffn_up_fwd", hn2, w_up, tm=tm, tn=cfg.tf, tk=d)
    uc = _conv_fwd("ffn_conv_fwd", up, 0, fconv_w, fconv_b, taps=FFN_CONV, width=2 * dff, tr=tr, tc=cfg.tf)
    tra = tr // 2
    act_ins = [In(uc, _rows(tra, 2 * dff), F32)]
    act_spec = _rows(tra, dff)
    act = _stage_fwd("ffn_act_fwd", _act_fn, (rp // tra,), act_ins, [jax.ShapeDtypeStruct((rp, dff), BF16)], [act_spec])[0]
    h2 = _mm("ffn_down_fwd", act, w_down, tm=tm, tn=d, tk=cfg.tf, add=h1)

    dh2, g_normf, loss_blk = _final(cfg, h2, normf, tgt)
    loss = lax.psum(loss_blk[0, 0], ("x", "y", "c"))

    g_w_down = _mm_tn("ffn_down_dw", act, dh2, tr=tm, tka=cfg.tf, tn=d)
    dact = _mm("ffn_down_dx", dh2, w_down.T, tm=tm, tn=cfg.tf, tk=d)
    duc, = _stage_bwd("ffn_act_bwd", _act_fn, (rp // tra,), act_ins, [(dact, act_spec)])
    dup, g_fconv_w, g_fconv_b = _conv_bwd("ffn_conv_bwd", up, 0, fconv_w, duc, taps=FFN_CONV, width=2 * dff,
                                          tr=tr, tc=cfg.tf, with_bias=True)
    g_w_up = _mm_tn("ffn_up_dw", hn2, dup, tr=tm, tka=d, tn=cfg.tf, blocked=True)

    def pair_reduce(tag, names, arrs):
        recvs = _pair_exchange("grad_pair_exchange_" + tag, arrs)
        return [_pair_sum("grad_pair_sum_" + n, g, r, cidx) for n, g, r in zip(names, arrs, recvs)]

    parts_ffn = pair_reduce("ffn", ["w_ffn_down", "w_ffn_up"], [g_w_down.reshape(N_CHIPS, dff // N_CHIPS, d), g_w_up])
    dhn2, slots_ffn = _mm("ffn_up_dx", dup, w_up_t, tm=tm, tn=d, tk=cfg.tf, comm=_exchange_comm(parts_ffn))

    def rms_bwd(name, h, g, dhn, dres):
        ins = [In(h, rspec, F32), In(g, _full(g), F32, True)]
        return _stage_bwd(name, rms_b, (nrow,), ins, [(dhn, rspec), (dres, rspec)])

    dh1, g_norm2 = rms_bwd("rms2_bwd", h1, norm2, dhn2, dh2)
    g_w_out = _mm_tn("out_proj_dw", ymix, dh1, tr=tm, tka=d, tn=d)
    dymix = _mm("out_proj_dx", dh1, w_out.T, tm=tm, tn=d, tk=d)
    doa, dob, dproj, g_gnorm = _stage_bwd("mix_bwd", mix_fn, (nrow,), mix_ins, [(dymix, rspec)])

    dqr, dkr, dproj = _ret_scan_bwd(cfg, dob, qr, kr, proj, rconsts, rss, dproj)
    dproj, = _stage_bwd("rot_bwd", rot_fn, (nrow,), rot_ins(dproj), [(dqr, rspec), (dkr, rspec)])

    dgu, dgw, dgattn, dgqd, dgkd, dggl = _gdn_scan_bwd(cfg, doa, gu, gw, gattn, gqd, gkd, ggl, gss)
    intra_cots = [(dgu, hspec), (dgw, hspec), (dgattn, aspec), (dgqd, hspec), (dgkd, hspec), (dggl, gspec)]
    dqn, dkn, dvv, dbB, dlB = _stage_bwd("gdn_intra_bwd", _gdn_intra_fn, gi_grid, intra_ins + [In(gtinv, aspec)], intra_cots)
    dcqkv, dproj, g_alog, g_dtb = _stage_bwd(
        "gdn_prep_bwd", prep_fn, (nrow,), prep_ins(dproj), [(t, rspec) for t in (dqn, dkn, dvv, dbB, dlB)])
    dproj, g_gconv_w = _conv_bwd("gdn_conv_bwd", proj, CONV_COL * d, gconv_w, dcqkv, taps=GDN_CONV, width=3 * d,
                                 tr=tr, tc=d, with_bias=False, dx_into=dproj)
    g_w_in_r = _mm_tn("proj_dw", hn1, dproj, tr=tm, tka=d, tn=tn_in)
    g_in4 = _restore_w_in(g_w_in_r, cfg).reshape(d, N_CHIPS, cfg.din // N_CHIPS).transpose(1, 0, 2)
    parts_mix = pair_reduce("mix", ["w_out", "w_in"], [g_w_out.reshape(N_CHIPS, d // N_CHIPS, d), g_in4])
    dhn1, slots_mix = _mm("proj_dx", dproj, w_in_r.T, tm=tm, tn=d, tk=tn_in, comm=_exchange_comm(parts_mix))
    dh0, g_norm1 = rms_bwd("rms1_bwd", h0, norm1, dhn1, dh1)

    grad_x = dh0[cfg.xrow:]
    small_grads = {
        "meta": dh0[cfg.front:cfg.xrow], "norm1": g_norm1, "gdn_conv_w": g_gconv_w[None],
        "gdn_a_log": g_alog[0:1, :hg], "gdn_dt_bias": g_dtb[0:1, :hg], "gdn_norm": g_gnorm, "norm2": g_norm2,
        "ffn_conv_w": g_fconv_w[None], "ffn_conv_b": g_fconv_b, "norm_f": g_normf.reshape(d),
    }

    small_names = [n for n, _ in PARAMS if n not in BIG]
    g_small = jnp.stack([_pack([small_grads[n] if axis[n] is None else _split4(small_grads[n], axis[n])[s]
                                for n in small_names], F32) for s in range(N_CHIPS)])
    parts_small = pair_reduce("small", ["small"], [g_small])
    slots_small = _run_comm("grad_exchange_small", _exchange_comm(parts_small))
    tags = ["w_in", "w_out", "w_ffn_up", "w_ffn_down", "small"]
    parts = [parts_mix[1], parts_mix[0], parts_ffn[1], parts_ffn[0], parts_small[0]]
    slots = [slots_mix[1], slots_mix[0], slots_ffn[1], slots_ffn[0], slots_small[0]]
    fins = [_chip_sum("grad_chip_sum_" + t, p, s, chip) for t, p, s in zip(tags, parts, slots)]
    sibs = _pair_swap(fins)

    def flat2(a):
        return a.reshape(-1, a.shape[-1])

    outs = {}
    for k, t in enumerate(BIG):
        res = _adamw("adamw_" + t, flat2(shard[t]), fins[k], sibs[k], flat2(m_shard[t]), flat2(v_shard[t]), cidx)
        outs[t] = [r.reshape(shard[t].shape) for r in res]
    small_shapes_all = [shard[n].shape for n in small_names]
    pk = lambda src: _pack([src[n] for n in small_names], F32)
    res = _adamw("adamw_small", pk(shard), fins[4], sibs[4], pk(m_shard), pk(v_shard), cidx)
    for k, r in enumerate(res):
        for n, a in zip(small_names, _unpack(r, small_shapes_all)):
            outs.setdefault(n, [None] * 4)[k] = a
    names = [n for n, _ in PARAMS]
    return (loss, grad_x[None], *[outs[n][k] for k in range(4) for n in names])


def kernel(x, meta, norm1, w_in, gdn_conv_w, gdn_a_log, gdn_dt_bias, gdn_norm, w_out, norm2, w_ffn_up, ffn_conv_w, ffn_conv_b, w_ffn_down, norm_f, loss_target, m_meta, m_norm1, m_w_in, m_gdn_conv_w, m_gdn_a_log, m_gdn_dt_bias, m_gdn_norm, m_w_out, m_norm2, m_w_ffn_up, m_ffn_conv_w, m_ffn_conv_b, m_w_ffn_down, m_norm_f, v_meta, v_norm1, v_w_in, v_gdn_conv_w, v_gdn_a_log, v_gdn_dt_bias, v_gdn_norm, v_w_out, v_norm2, v_w_ffn_up, v_ffn_conv_w, v_ffn_conv_b, v_w_ffn_down, v_norm_f):
    names = [n for n, _ in PARAMS]
    shard = dict(zip(names, (meta, norm1, w_in, gdn_conv_w, gdn_a_log, gdn_dt_bias, gdn_norm, w_out, norm2, w_ffn_up,
                             ffn_conv_w, ffn_conv_b, w_ffn_down, norm_f)))
    m_shard = dict(zip(names, (m_meta, m_norm1, m_w_in, m_gdn_conv_w, m_gdn_a_log, m_gdn_dt_bias, m_gdn_norm, m_w_out,
                               m_norm2, m_w_ffn_up, m_ffn_conv_w, m_ffn_conv_b, m_w_ffn_down, m_norm_f)))
    v_shard = dict(zip(names, (v_meta, v_norm1, v_w_in, v_gdn_conv_w, v_gdn_a_log, v_gdn_dt_bias, v_gdn_norm, v_w_out,
                               v_norm2, v_w_ffn_up, v_ffn_conv_w, v_ffn_conv_b, v_w_ffn_down, v_norm_f)))
    return _step(REAL, x[0], loss_target[0], shard, m_shard, v_shard)
```

```python
import functools
from typing import NamedTuple

import numpy as np
import jax
import jax.numpy as jnp
from jax import lax
from jax.experimental import pallas as pl
from jax.experimental.pallas import tpu as pltpu

F32 = jnp.float32
BF16 = jnp.bfloat16
EPS = 1e-6
CHUNK = 64
GDN_DK = 128
RET_DK = 256
GDN_CONV = 4
FFN_CONV = 3
ROPE_BASE = 10000.0
LANES = 128
N_CHIPS = 4
ADAM_LR, ADAM_B1, ADAM_B2, ADAM_EPS, ADAM_WD, ADAM_STEP = 0.001, 0.9, 0.999, 1e-08, 0.01, 10
MIX_COL, CONV_COL, RV_BLOCK, ROT_COL, TAIL_COL = 0, 4, 7, 8, 10
MESH = pl.DeviceIdType.MESH
VMEM_LIMIT = 56 * 1024 * 1024


class Cfg(NamedTuple):
    d: int
    seq: int
    n_meta: int
    dff: int
    tr: int
    nb: int
    tm: int
    tf: int
    sc: int

    @property
    def hg(self): return self.d // GDN_DK
    @property
    def hr(self): return self.d // RET_DK
    @property
    def L(self): return self.n_meta + self.seq
    @property
    def rp(self): return -(-self.L // 256) * 256
    @property
    def front(self): return self.rp - self.L
    @property
    def xrow(self): return self.rp - self.seq
    @property
    def nch(self): return self.rp // CHUNK
    @property
    def pw(self): return 10 * self.d + LANES
    @property
    def din(self): return 10 * self.d + 2 * self.hg


REAL = Cfg(d=1024, seq=8192, n_meta=16, dff=2816, tr=256, nb=12, tm=1408, tf=1408, sc=4)


def _pallas(body, **kw):
    return pl.pallas_call(body, **kw)


def _sigmoid(x):
    return 1.0 / (1.0 + jnp.exp(-x))


def _silu(x):
    return x * _sigmoid(x)


def _softplus(x):
    return jnp.maximum(x, 0.0) + jnp.log(1.0 + jnp.exp(-jnp.abs(x)))


def _raw_dot(a, b, ta, tb, hi):
    if not hi:
        a = a.astype(BF16)
        b = b.astype(BF16)
    nbatch = a.ndim - 2
    ca = a.ndim - 2 if ta else a.ndim - 1
    cb = b.ndim - 1 if tb else b.ndim - 2
    batch = tuple(range(nbatch))
    return lax.dot_general(a, b, (((ca,), (cb,)), (batch, batch)),
                           precision=lax.Precision.HIGHEST if hi else None,
                           preferred_element_type=F32)


@functools.partial(jax.custom_vjp, nondiff_argnums=(2, 3, 4))
def _dot_p(a, b, ta, tb, hi):
    return _raw_dot(a, b, ta, tb, hi)


def _dot(a, b, ta=False, tb=False, hi=False):
    return _dot_p(a, b, ta, tb, hi)


def _dot_fwd(a, b, ta, tb, hi):
    return _raw_dot(a, b, ta, tb, hi), (a, b)


def _dot_bwd(ta, tb, hi, res, g):
    a, b = res
    if not ta and not tb:
        da, db = _dot(g, b, False, True, hi), _dot(a, g, True, False, hi)
    elif not ta and tb:
        da, db = _dot(g, b, False, False, hi), _dot(g, a, True, False, hi)
    elif ta and not tb:
        da, db = _dot(b, g, False, True, hi), _dot(a, g, False, False, hi)
    else:
        raise NotImplementedError
    return da.astype(a.dtype), db.astype(b.dtype)


_dot_p.defvjp(_dot_fwd, _dot_bwd)


def _iota2(n, m, axis):
    return lax.broadcasted_iota(jnp.int32, (n, m), axis)


def _bcast(mat, nb):
    return jnp.broadcast_to(mat[None], (nb,) + mat.shape)


def _split3(a):
    a0 = a.astype(BF16)
    r1 = a - a0.astype(F32)
    a1 = r1.astype(BF16)
    return a0, a1, (r1 - a1.astype(F32)).astype(BF16)


@functools.partial(jax.custom_vjp, nondiff_argnums=(2,))
def _dot_sel(a, e, te):
    eb = e.astype(BF16)
    p0, p1, p2 = (_raw_dot(p, eb, False, te, False) for p in _split3(a))
    return p0 + (p1 + p2)


def _dot_sel_fwd(a, e, te):
    return _dot_sel(a, e, te), e


def _dot_sel_bwd(te, e, g):
    return _dot_sel(g, e, not te), jnp.zeros_like(e)


_dot_sel.defvjp(_dot_sel_fwd, _dot_sel_bwd)


@jax.custom_vjp
def _sel_dot(e, x):
    eb = e.astype(BF16)
    p0, p1, p2 = (_raw_dot(eb, p, False, False, False) for p in _split3(x))
    return p0 + (p1 + p2)


def _sel_dot_fwd(e, x):
    return _sel_dot(e, x), e


def _sel_dot_bwd(e, g):
    eb = e.astype(BF16)
    p0, p1, p2 = (_raw_dot(eb, p, True, False, False) for p in _split3(g))
    return jnp.zeros_like(e), p0 + (p1 + p2)


_sel_dot.defvjp(_sel_dot_fwd, _sel_dot_bwd)


def _tri_inv_raw(m):
    nb = m.shape[0]
    r, c = _iota2(CHUNK, CHUNK, 0), _iota2(CHUNK, CHUNK, 1)
    t = _bcast((r == c).astype(F32), nb)
    b = 1
    while b < CHUNK:
        sh = b.bit_length() - 1
        off = ((r >> (sh + 1)) == (c >> (sh + 1))) & ((r >> sh) != (c >> sh)) & (r > c)
        cl = jnp.where(off[None], m, 0.0)
        t = t - _raw_dot(_raw_dot(t, cl, False, False, False), t, False, False, False)
        b *= 2
    return t


@jax.custom_vjp
def _tri_inv_given(m, t):
    return t


def _tri_inv_fwd(m, t):
    return t, t


def _tri_inv_bwd(t, g):
    return -_raw_dot(_raw_dot(t, g, True, False, False), t, False, True, False), jnp.zeros_like(t)


_tri_inv_given.defvjp(_tri_inv_fwd, _tri_inv_bwd)


def _rms(h, g):
    return h * lax.rsqrt(jnp.mean(h * h, axis=-1, keepdims=True) + EPS) * g


class In(NamedTuple):
    arr: jax.Array
    spec: pl.BlockSpec
    grad: object = None
    acc: bool = False
    gshape: object = None
    gspec: object = None
    galias: object = None


def _params(grid):
    sem = ("arbitrary",) * len(grid)
    return pltpu.CompilerParams(dimension_semantics=sem, vmem_limit_bytes=VMEM_LIMIT)


def _stage_fwd(name, fn, grid, ins, out_shapes, out_specs):
    n_in = len(ins)

    def body(*refs):
        pids = tuple(pl.program_id(k) for k in range(len(grid)))
        vals = [r[...] for r in refs[:n_in]]
        outs = fn(pids, *vals)
        for o_ref, o in zip(refs[n_in:], outs):
            o_ref[...] = o.reshape(o_ref.shape).astype(o_ref.dtype)

    return _pallas(
        body, out_shape=out_shapes, grid=grid, in_specs=[i.spec for i in ins],
        out_specs=out_specs, name=name, compiler_params=_params(grid))(*[i.arr for i in ins])


def _stage_bwd(name, fn, grid, ins, cots):
    n_in, n_ct = len(ins), len(cots)
    didx = [k for k, i in enumerate(ins) if i.grad is not None]
    aliased = [(o, ins[k].galias) for o, k in enumerate(didx) if ins[k].galias is not None]
    n_al = len(aliased)

    def body(*refs):
        pids = tuple(pl.program_id(k) for k in range(len(grid)))
        vals = [r[...] for r in refs[:n_in]]
        ct_refs = refs[n_in:n_in + n_ct]
        g_refs = refs[n_in + n_ct + n_al:]

        def f(*dv):
            merged = list(vals)
            for k, v in zip(didx, dv):
                merged[k] = v
            return tuple(fn(pids, *merged))

        outs, vjp_fn = jax.vjp(f, *[vals[k].astype(F32) for k in didx])
        cts = tuple(c[...].reshape(o.shape).astype(F32) for c, o in zip(ct_refs, outs))
        grads = vjp_fn(cts)
        first = functools.reduce(jnp.logical_and, [p == 0 for p in pids])
        for k, g_ref, g in zip(didx, g_refs, grads):
            if ins[k].acc:
                @pl.when(first)
                def _(g_ref=g_ref):
                    g_ref[...] = jnp.zeros(g_ref.shape, g_ref.dtype)
                g_ref[...] += g.reshape(g_ref.shape).astype(g_ref.dtype)
            else:
                g_ref[...] = g.reshape(g_ref.shape).astype(g_ref.dtype)

    out_shapes = [jax.ShapeDtypeStruct(ins[k].gshape or ins[k].arr.shape, ins[k].grad) for k in didx]
    out_specs = [ins[k].gspec or ins[k].spec for k in didx]
    return _pallas(
        body, out_shape=out_shapes, grid=grid,
        in_specs=[i.spec for i in ins] + [c[1] for c in cots] + [ANY] * n_al, out_specs=out_specs,
        input_output_aliases={n_in + n_ct + a: o for a, (o, _) in enumerate(aliased)},
        name=name, compiler_params=_params(grid))(*[i.arr for i in ins], *[c[0] for c in cots], *[a for _, a in aliased])


def _full(arr):
    nd = arr.ndim
    return pl.BlockSpec(arr.shape, lambda *p: (0,) * nd)


def _rows(tr, width, blk=0):
    return pl.BlockSpec((tr, width), lambda i: (i, blk))


def _mm(name, a, b, *, tm, tn, tk, out_dtype=F32, add=None, comm=None):
    M, K = a.shape
    N = b.shape[1] if b.ndim == 2 else b.shape[0] * b.shape[2]
    nk = K // tk
    grid = (M // tm, N // tn, nk)
    n_in = 3 if add is not None else 2
    n_ci, n_co = (len(comm.ins), len(comm.outs)) if comm is not None else (0, 0)

    def body(*refs):
        a_ref, b_ref = refs[0], refs[1]
        add_ref = refs[2] if add is not None else None
        c_ins = refs[n_in:n_in + n_ci]
        o_ref = refs[n_in + n_ci]
        c_outs = refs[n_in + n_ci + 1:n_in + n_ci + 1 + n_co]
        scratch = refs[n_in + n_ci + 1 + n_co:]
        acc_ref = scratch[0] if nk > 1 else None
        sems = scratch[1 if nk > 1 else 0:]
        step = (pl.program_id(0) * grid[1] + pl.program_id(1)) * nk + pl.program_id(2)
        if comm is not None:
            @pl.when(step == 0)
            def _():
                comm.start(c_ins, c_outs, sems)

        part = _raw_dot(a_ref[...], b_ref[...], False, False, False)

        def finish(total):
            if add_ref is not None:
                total = total + add_ref[...]
            o_ref[...] = total.astype(o_ref.dtype)

        if nk == 1:
            finish(part)
        else:
            k = pl.program_id(2)

            @pl.when(k == 0)
            def _():
                acc_ref[...] = part

            @pl.when(k > 0)
            def _():
                acc_ref[...] += part

            @pl.when(k == nk - 1)
            def _():
                finish(acc_ref[...])

        if comm is not None:
            @pl.when(step == grid[0] * grid[1] * nk - 1)
            def _():
                comm.finish(c_ins, c_outs, sems)

    b_spec = (pl.BlockSpec((tk, tn), lambda i, j, k: (k, j)) if b.ndim == 2 else
              pl.BlockSpec((None, tk, tn), lambda i, j, k: (j, k, 0)))
    in_specs = [pl.BlockSpec((tm, tk), lambda i, j, k: (i, k)), b_spec]
    args = [a, b]
    if add is not None:
        in_specs.append(pl.BlockSpec((tm, tn), lambda i, j, k: (i, j)))
        args.append(add)
    out_shape = jax.ShapeDtypeStruct((M, N), out_dtype)
    out_spec = pl.BlockSpec((tm, tn), lambda i, j, k: (i, j))
    scratch = [pltpu.VMEM((tm, tn), F32)] if nk > 1 else []
    if comm is None:
        return _pallas(body, out_shape=out_shape, grid=grid, in_specs=in_specs, out_specs=out_spec,
                       scratch_shapes=scratch, name=name, compiler_params=_params(grid))(*args)
    res = _pallas(body, out_shape=[out_shape] + comm.outs, grid=grid, in_specs=in_specs + [ANY] * n_ci,
                  out_specs=[out_spec] + [ANY] * n_co, scratch_shapes=scratch + comm.sems, name=name,
                  compiler_params=_params(grid))(*args, *comm.ins)
    return res[0], res[1:]


def _mm_tn(name, a, b, *, tr, tka, tn, blocked=False):
    R, Ka = a.shape
    N = b.shape[1]
    nr = R // tr
    grid = (Ka // tka, N // tn, nr)
    if blocked:
        out_shape = jax.ShapeDtypeStruct((N // tn, Ka, tn), F32)
        out_spec = pl.BlockSpec((None, tka, tn), lambda i, j, r: (j, i, 0))
    else:
        out_shape = jax.ShapeDtypeStruct((Ka, N), F32)
        out_spec = pl.BlockSpec((tka, tn), lambda i, j, r: (i, j))

    def body(a_ref, b_ref, o_ref):
        r = pl.program_id(2)
        part = _raw_dot(a_ref[...], b_ref[...], True, False, False)

        @pl.when(r == 0)
        def _():
            o_ref[...] = part

        @pl.when(r > 0)
        def _():
            o_ref[...] += part

    return _pallas(
        body, out_shape=out_shape, grid=grid,
        in_specs=[pl.BlockSpec((tr, tka), lambda i, j, r: (r, i)),
                  pl.BlockSpec((tr, tn), lambda i, j, r: (r, j))],
        out_specs=out_spec, name=name, compiler_params=_params(grid))(a, b)


def _conv_fwd(name, x, xcol0, w, b, *, taps, width, tr, tc):
    R = x.shape[0]
    grid = (width // tc, R // tr)
    cb0 = xcol0 // tc
    hb = tr // 8

    def body(*refs):
        x_ref, xp_ref, w_ref = refs[:3]
        b_ref = refs[3] if b is not None else None
        o_ref = refs[-1]
        i = pl.program_id(1)
        xv = x_ref[...]
        prev = jnp.where(i > 0, xp_ref[...], 0.0)
        ext = jnp.concatenate([prev, xv], axis=0)
        acc = xv * w_ref[taps - 1:taps, :]
        for s in range(1, taps):
            acc = acc + pltpu.roll(ext, s, 0)[8:, :] * w_ref[taps - 1 - s:taps - s, :]
        if b_ref is not None:
            acc = acc + b_ref[...]
        o_ref[...] = acc

    in_specs = [pl.BlockSpec((tr, tc), lambda j, i: (i, cb0 + j)),
                pl.BlockSpec((8, tc), lambda j, i: (jnp.maximum(i * hb - 1, 0), cb0 + j)),
                pl.BlockSpec((taps, tc), lambda j, i: (0, j))]
    args = [x, x, w]
    if b is not None:
        in_specs.append(pl.BlockSpec((1, tc), lambda j, i: (0, j)))
        args.append(b)
    return _pallas(
        body, out_shape=jax.ShapeDtypeStruct((R, width), F32), grid=grid, in_specs=in_specs,
        out_specs=pl.BlockSpec((tr, tc), lambda j, i: (i, j)),
        name=name, compiler_params=_params(grid))(*args)


def _conv_bwd(name, x, xcol0, w, dy, *, taps, width, tr, tc, with_bias, dx_into=None):
    R = x.shape[0]
    nr = R // tr
    grid = (width // tc, nr)
    cb0 = xcol0 // tc
    hb = tr // 8
    n_ext = tr + 8
    n_al = 0 if dx_into is None else 1

    def body(*refs):
        x_ref, w_ref, dy_ref, dyn_ref = refs[:4]
        dx_ref, dw_ref = refs[4 + n_al], refs[5 + n_al]
        db_ref = refs[6 + n_al] if with_bias else None
        i = pl.program_id(1)
        xv = x_ref[...]
        dyv = dy_ref[...]
        dext = jnp.concatenate([dyv, jnp.where(i < nr - 1, dyn_ref[...], 0.0)], axis=0)
        dx = dyv * w_ref[taps - 1:taps, :]
        dws = [None] * taps
        dws[taps - 1] = jnp.sum(xv * dyv, axis=0, keepdims=True)
        for s in range(1, taps):
            ahead = pltpu.roll(dext, n_ext - s, 0)[:tr, :]
            dx = dx + ahead * w_ref[taps - 1 - s:taps - s, :]
            dws[taps - 1 - s] = jnp.sum(xv * ahead, axis=0, keepdims=True)
        dx_ref[...] = dx.astype(dx_ref.dtype)

        @pl.when(i == 0)
        def _():
            for k in range(taps):
                dw_ref[k:k + 1, :] = dws[k]
            if db_ref is not None:
                db_ref[...] = jnp.sum(dyv, axis=0, keepdims=True)

        @pl.when(i > 0)
        def _():
            for k in range(taps):
                dw_ref[k:k + 1, :] += dws[k]
            if db_ref is not None:
                db_ref[...] += jnp.sum(dyv, axis=0, keepdims=True)

    in_specs = [pl.BlockSpec((tr, tc), lambda j, i: (i, cb0 + j)),
                pl.BlockSpec((taps, tc), lambda j, i: (0, j)),
                pl.BlockSpec((tr, tc), lambda j, i: (i, j)),
                pl.BlockSpec((8, tc), lambda j, i: (jnp.minimum((i + 1) * hb, R // 8 - 1), j))]
    args = [x, w, dy, dy]
    if dx_into is None:
        dx_shape, dx_spec, aliases = jax.ShapeDtypeStruct((R, width), BF16), pl.BlockSpec((tr, tc), lambda j, i: (i, j)), {}
    else:
        dx_shape = jax.ShapeDtypeStruct(dx_into.shape, dx_into.dtype)
        dx_spec, aliases = pl.BlockSpec((tr, tc), lambda j, i: (i, cb0 + j)), {4: 0}
        in_specs.append(ANY)
        args.append(dx_into)
    out_shape = [dx_shape, jax.ShapeDtypeStruct((taps, width), F32)]
    out_specs = [dx_spec, pl.BlockSpec((taps, tc), lambda j, i: (0, j))]
    if with_bias:
        out_shape.append(jax.ShapeDtypeStruct((1, width), F32))
        out_specs.append(pl.BlockSpec((1, tc), lambda j, i: (0, j)))
    return _pallas(
        body, out_shape=out_shape, grid=grid, in_specs=in_specs, out_specs=out_specs, input_output_aliases=aliases,
        name=name, compiler_params=_params(grid))(*args)


def _row_mask(cfg, i, tr):
    rows = i * tr + lax.broadcasted_iota(jnp.int32, (tr, 1), 0)
    return (rows >= cfg.front).astype(F32)


def _make_rms_fn(cfg, tr, with_residual):
    def fn(pids, h, g):
        hm = h * _row_mask(cfg, pids[0], tr)
        if with_residual:
            return _rms(hm, g), hm
        return (_rms(hm, g),)
    return fn


def _make_gdn_prep_fn(cfg, tr):
    d, hg = cfg.d, cfg.hg

    def fn(pids, c, tail, alog, dtb):
        cq, ck, cv = c[:, :d], c[:, d:2 * d], c[:, 2 * d:]
        mask = _row_mask(cfg, pids[0], tr)
        j, col = _iota2(LANES, d, 0), _iota2(LANES, d, 1)
        ea = ((col >> 7) == j).astype(F32)
        eb = ((col >> 7) + hg == j).astype(F32)
        al = jnp.sum(alog, axis=0, keepdims=True)
        db = jnp.sum(dtb, axis=0, keepdims=True)
        lg = _dot_sel(-jnp.exp(al) * _softplus(tail + db) * mask, ea, False)
        beta = _dot_sel(_sigmoid(tail) * mask, eb, False)
        sq, sk, sv = _silu(cq), _silu(ck), _silu(cv)
        qs, ks = [], []
        for h in range(hg):
            sl = slice(h * GDN_DK, (h + 1) * GDN_DK)
            qh, kh = sq[:, sl], sk[:, sl]
            qs.append(qh * lax.rsqrt(jnp.sum(qh * qh, axis=-1, keepdims=True) + EPS) * (GDN_DK ** -0.5))
            ks.append(kh * lax.rsqrt(jnp.sum(kh * kh, axis=-1, keepdims=True) + EPS))
        return jnp.concatenate(qs, axis=1), jnp.concatenate(ks, axis=1), sv, beta, lg
    return fn


def _gdn_intra_fn(pids, q, k, v, bB, lB, t_saved=None):
    rows = q.shape[0]
    nb = rows // CHUNK
    q3, k3, v3, b3, l3 = [t.reshape(nb, CHUNK, GDN_DK) for t in (q, k, v, bB, lB)]
    r, c = _iota2(CHUNK, CHUNK, 0), _iota2(CHUNK, CHUNK, 1)
    tril = (r >= c)
    strict = (r > c)
    gcol = _sel_dot(_bcast(tril.astype(F32), nb), l3)
    l64 = l3[:, :, :CHUNK]
    grow = _sel_dot(jnp.ones((nb, CHUNK, CHUNK), F32), l64 * (r <= c).astype(F32)[None])
    diff = gcol[:, :, :CHUNK] - grow
    decay = jnp.where(tril[None], jnp.exp(jnp.where(tril[None], diff, 0.0)), 0.0)
    kb = k3 * b3
    m = jnp.where(strict[None], _dot(kb, k3, False, True) * decay, 0.0)
    t = _tri_inv_raw(m) if t_saved is None else _tri_inv_given(m, t_saved.reshape(nb, CHUNK, CHUNK))
    eg = jnp.exp(gcol)
    u = _dot(t, v3 * b3)
    w = _dot(t, kb * eg)
    attn = _dot(q3, k3, False, True) * decay
    qd = q3 * eg
    glast = jnp.sum(l3, axis=1, keepdims=True)
    kd = k3 * jnp.exp(glast - gcol)
    gl = jnp.exp(glast)
    outs = (u.reshape(rows, GDN_DK), w.reshape(rows, GDN_DK), attn.reshape(1, rows, CHUNK),
            qd.reshape(rows, GDN_DK), kd.reshape(rows, GDN_DK), gl.reshape(1, nb, 1, GDN_DK))
    return outs + (t.reshape(1, rows, CHUNK),) if t_saved is None else outs


def _make_rot_fn(cfg):
    hr = cfg.hr
    half = RET_DK // 2

    def fn(pids, rqk, cos, sin):
        rq, rk = rqk[:, :cfg.d], rqk[:, cfg.d:]

        def rot(t, scale):
            outs = []
            for h in range(hr):
                x1 = t[:, h * RET_DK:h * RET_DK + half]
                x2 = t[:, h * RET_DK + half:(h + 1) * RET_DK]
                outs += [(x1 * cos - x2 * sin) * scale, (x2 * cos + x1 * sin) * scale]
            return jnp.concatenate(outs, axis=1)
        return rot(rq, 1.0), rot(rk, RET_DK ** -0.5)
    return fn


def _make_mix_fn(cfg):
    hg, hr = cfg.hg, cfg.hr

    def fn(pids, oa, ob, pm, gnorm):
        d = cfg.d
        gz, rg, gate_a, gate_b = pm[:, :d], pm[:, d:2 * d], pm[:, 2 * d:3 * d], pm[:, 3 * d:]
        oas = []
        for h in range(hg):
            oh = oa[:, h * GDN_DK:(h + 1) * GDN_DK]
            oas.append(oh * lax.rsqrt(jnp.mean(oh * oh, axis=-1, keepdims=True) + EPS) * gnorm)
        ya = jnp.concatenate(oas, axis=1) * _silu(gz)
        obs = []
        for h in range(hr):
            oh = ob[:, h * RET_DK:(h + 1) * RET_DK]
            obs.append(oh * lax.rsqrt(jnp.mean(oh * oh, axis=-1, keepdims=True) + EPS))
        yb = _silu(rg) * jnp.concatenate(obs, axis=1)
        return (_sigmoid(gate_a) * ya + _sigmoid(gate_b) * yb,)
    return fn


def _act_fn(pids, u):
    f = u.shape[1] // 2
    return (_silu(u[:, :f]) * u[:, f:],)


def _gdn_step(s, u, w, a, qd, kd, gl):
    v_new = u - _dot(w, s)
    o = _dot(qd, s) + _dot(a, v_new)
    s2 = s * gl + _dot(kd, v_new, True, False)
    return s2, o


def _ret_step(s, q, k, v, dm, qdc, kdc, g):
    inner = _dot(_dot(q, k, False, True) * dm, v)
    o = inner + _dot(q * qdc, s)
    s2 = s * g + _dot(k * kdc, v, True, False)
    return s2, o


def _gdn_scan_fwd(cfg, u, w, attn, qd, kd, gl):
    d, hg, nch, sc = cfg.d, cfg.hg, cfg.nch, cfg.sc
    nst = nch // sc

    def body(u_ref, w_ref, a_ref, qd_ref, kd_ref, gl_ref, o_ref, ss_ref, s_ref):
        @pl.when(pl.program_id(0) == 0)
        def _():
            s_ref[...] = jnp.zeros(s_ref.shape, F32)

        states = [s_ref[h] for h in range(hg)]
        for j in range(sc):
            rows = slice(j * CHUNK, (j + 1) * CHUNK)
            outs = []
            for h in range(hg):
                sl = slice(h * GDN_DK, (h + 1) * GDN_DK)
                ss_ref[j, h] = states[h]
                states[h], o = _gdn_step(states[h], u_ref[rows, sl], w_ref[rows, sl], a_ref[h, rows, :],
                                         qd_ref[rows, sl], kd_ref[rows, sl], gl_ref[h, j])
                outs.append(o)
            o_ref[rows, :] = jnp.concatenate(outs, axis=1)
        for h in range(hg):
            s_ref[h] = states[h]

    row = pl.BlockSpec((sc * CHUNK, d), lambda n: (n, 0))
    return _pallas(
        body,
        out_shape=[jax.ShapeDtypeStruct((cfg.rp, d), F32), jax.ShapeDtypeStruct((nch, hg, GDN_DK, GDN_DK), F32)],
        grid=(nst,),
        in_specs=[row, row, pl.BlockSpec((hg, sc * CHUNK, CHUNK), lambda n: (0, n, 0)), row, row,
                  pl.BlockSpec((hg, sc, 1, GDN_DK), lambda n: (0, n, 0, 0))],
        out_specs=[row, pl.BlockSpec((sc, hg, GDN_DK, GDN_DK), lambda n: (n, 0, 0, 0))],
        scratch_shapes=[pltpu.VMEM((hg, GDN_DK, GDN_DK), F32)],
        name="gdn_scan_fwd", compiler_params=_params((nst,)))(u, w, attn, qd, kd, gl)


def _gdn_scan_bwd(cfg, do, u, w, attn, qd, kd, gl, ss):
    d, hg, nch, sc = cfg.d, cfg.hg, cfg.nch, cfg.sc
    nst = nch // sc

    def body(do_ref, u_ref, w_ref, a_ref, qd_ref, kd_ref, gl_ref, ss_ref,
             du_ref, dw_ref, da_ref, dqd_ref, dkd_ref, dgl_ref, ds_ref):
        @pl.when(pl.program_id(0) == 0)
        def _():
            ds_ref[...] = jnp.zeros(ds_ref.shape, F32)

        dstates = [ds_ref[h] for h in range(hg)]
        for j in reversed(range(sc)):
            rows = slice(j * CHUNK, (j + 1) * CHUNK)
            dus, dws, dqds, dkds = [], [], [], []
            for h in range(hg):
                sl = slice(h * GDN_DK, (h + 1) * GDN_DK)
                args = (ss_ref[j, h], u_ref[rows, sl], w_ref[rows, sl], a_ref[h, rows, :], qd_ref[rows, sl],
                        kd_ref[rows, sl], gl_ref[h, j])
                _, vjp_fn = jax.vjp(_gdn_step, *args)
                dstates[h], du, dw, da, dqd, dkd, dgl = vjp_fn((dstates[h], do_ref[rows, sl]))
                da_ref[h, rows, :] = da
                dgl_ref[h, j] = dgl
                dus.append(du)
                dws.append(dw)
                dqds.append(dqd)
                dkds.append(dkd)
            du_ref[rows, :] = jnp.concatenate(dus, axis=1)
            dw_ref[rows, :] = jnp.concatenate(dws, axis=1)
            dqd_ref[rows, :] = jnp.concatenate(dqds, axis=1)
            dkd_ref[rows, :] = jnp.concatenate(dkds, axis=1)
        for h in range(hg):
            ds_ref[h] = dstates[h]

    row = pl.BlockSpec((sc * CHUNK, d), lambda n: (nst - 1 - n, 0))
    aspec = pl.BlockSpec((hg, sc * CHUNK, CHUNK), lambda n: (0, nst - 1 - n, 0))
    gspec = pl.BlockSpec((hg, sc, 1, GDN_DK), lambda n: (0, nst - 1 - n, 0, 0))
    rowshape = jax.ShapeDtypeStruct((cfg.rp, d), F32)
    return _pallas(
        body,
        out_shape=[rowshape, rowshape, jax.ShapeDtypeStruct(attn.shape, F32), rowshape, rowshape,
                   jax.ShapeDtypeStruct(gl.shape, F32)],
        grid=(nst,),
        in_specs=[row, row, row, aspec, row, row, gspec,
                  pl.BlockSpec((sc, hg, GDN_DK, GDN_DK), lambda n: (nst - 1 - n, 0, 0, 0))],
        out_specs=[row, row, aspec, row, row, gspec],
        scratch_shapes=[pltpu.VMEM((hg, GDN_DK, GDN_DK), F32)],
        name="gdn_scan_bwd", compiler_params=_params((nst,)))(do, u, w, attn, qd, kd, gl, ss)


def _ret_consts(cfg):
    hr = cfg.hr
    lg = np.log(1.0 - 2.0 ** (-5.0 - np.arange(hr, dtype=np.float64)))
    idx = np.arange(CHUNK, dtype=np.float64)
    tril = np.tril(np.ones((CHUNK, CHUNK), dtype=bool))
    dm = np.where(tril[None], np.exp((idx[:, None] - idx[None, :])[None] * lg[:, None, None]), 0.0)
    qdc = np.exp((idx[None, :] + 1.0) * lg[:, None])
    kdc = np.exp((CHUNK - 1.0 - idx[None, :]) * lg[:, None])
    gch = np.exp(CHUNK * lg)
    qdc = np.broadcast_to(qdc[:, :, None], (hr, CHUNK, RET_DK))
    kdc = np.broadcast_to(kdc[:, :, None], (hr, CHUNK, RET_DK))
    gch = np.broadcast_to(gch[:, None, None], (hr, 1, RET_DK))
    return tuple(jnp.asarray(np.ascontiguousarray(t), F32) for t in (dm, qdc, kdc, gch))


def _ret_scan_fwd(cfg, qr, kr, proj, consts):
    d, hr, nch, sc = cfg.d, cfg.hr, cfg.nch, cfg.sc
    nst = nch // sc
    dm, qdc, kdc, gch = consts

    def body(q_ref, k_ref, v_ref, dm_ref, qdc_ref, kdc_ref, g_ref, o_ref, ss_ref, s_ref):
        @pl.when(pl.program_id(0) == 0)
        def _():
            s_ref[...] = jnp.zeros(s_ref.shape, F32)

        states = [s_ref[h] for h in range(hr)]
        for j in range(sc):
            rows = slice(j * CHUNK, (j + 1) * CHUNK)
            outs = []
            for h in range(hr):
                sl = slice(h * RET_DK, (h + 1) * RET_DK)
                ss_ref[j, h] = states[h]
                states[h], o = _ret_step(states[h], q_ref[rows, sl], k_ref[rows, sl], v_ref[rows, sl], dm_ref[h],
                                         qdc_ref[h], kdc_ref[h], g_ref[h])
                outs.append(o)
            o_ref[rows, :] = jnp.concatenate(outs, axis=1)
        for h in range(hr):
            s_ref[h] = states[h]

    row = pl.BlockSpec((sc * CHUNK, d), lambda n: (n, 0))
    return _pallas(
        body,
        out_shape=[jax.ShapeDtypeStruct((cfg.rp, d), F32), jax.ShapeDtypeStruct((nch, hr, RET_DK, RET_DK), F32)],
        grid=(nst,),
        in_specs=[row, row, pl.BlockSpec((sc * CHUNK, d), lambda n: (n, RV_BLOCK)), _full(dm), _full(qdc), _full(kdc),
                  _full(gch)],
        out_specs=[row, pl.BlockSpec((sc, hr, RET_DK, RET_DK), lambda n: (n, 0, 0, 0))],
        scratch_shapes=[pltpu.VMEM((hr, RET_DK, RET_DK), F32)],
        name="ret_scan_fwd", compiler_params=_params((nst,)))(qr, kr, proj, dm, qdc, kdc, gch)


def _ret_scan_bwd(cfg, do, qr, kr, proj, consts, ss, dproj):
    d, hr, nch, sc = cfg.d, cfg.hr, cfg.nch, cfg.sc
    nst = nch // sc
    dm, qdc, kdc, gch = consts

    def body(do_ref, q_ref, k_ref, v_ref, dm_ref, qdc_ref, kdc_ref, g_ref, ss_ref, _, dq_ref, dk_ref, dv_ref, ds_ref):
        @pl.when(pl.program_id(0) == 0)
        def _():
            ds_ref[...] = jnp.zeros(ds_ref.shape, F32)

        dstates = [ds_ref[h] for h in range(hr)]
        for j in reversed(range(sc)):
            rows = slice(j * CHUNK, (j + 1) * CHUNK)
            dqs, dks, dvs = [], [], []
            for h in range(hr):
                sl = slice(h * RET_DK, (h + 1) * RET_DK)
                cs = (dm_ref[h], qdc_ref[h], kdc_ref[h], g_ref[h])
                _, vjp_fn = jax.vjp(lambda s, q, k, v, cs=cs: _ret_step(s, q, k, v, *cs),
                                    ss_ref[j, h], q_ref[rows, sl], k_ref[rows, sl], v_ref[rows, sl])
                dstates[h], dq, dk, dv = vjp_fn((dstates[h], do_ref[rows, sl]))
                dqs.append(dq)
                dks.append(dk)
                dvs.append(dv)
            dq_ref[rows, :] = jnp.concatenate(dqs, axis=1)
            dk_ref[rows, :] = jnp.concatenate(dks, axis=1)
            dv_ref[rows, :] = jnp.concatenate(dvs, axis=1).astype(dv_ref.dtype)
        for h in range(hr):
            ds_ref[h] = dstates[h]

    row = pl.BlockSpec((sc * CHUNK, d), lambda n: (nst - 1 - n, 0))
    rowshape = jax.ShapeDtypeStruct((cfg.rp, d), F32)
    vspec = pl.BlockSpec((sc * CHUNK, d), lambda n: (nst - 1 - n, RV_BLOCK))
    return _pallas(
        body,
        out_shape=[rowshape, rowshape, jax.ShapeDtypeStruct(dproj.shape, dproj.dtype)],
        grid=(nst,),
        in_specs=[row, row, row, vspec, _full(dm), _full(qdc), _full(kdc), _full(gch),
                  pl.BlockSpec((sc, hr, RET_DK, RET_DK), lambda n: (nst - 1 - n, 0, 0, 0)), ANY],
        out_specs=[row, row, vspec], input_output_aliases={9: 2},
        scratch_shapes=[pltpu.VMEM((hr, RET_DK, RET_DK), F32)],
        name="ret_scan_bwd", compiler_params=_params((nst,)))(do, qr, kr, proj, dm, qdc, kdc, gch, ss, dproj)


def _final(cfg, h2, normf, tgt):
    d, tr = cfg.d, cfg.xrow
    nr = cfg.rp // tr

    def body(h_ref, g_ref, t_ref, dh_ref, dg_ref, loss_ref):
        i = pl.program_id(0)
        y, vjp_fn = jax.vjp(_rms, h_ref[...], g_ref[...])
        err = jnp.where(i >= 1, y - t_ref[...], 0.0)
        dh, dg = vjp_fn(err * (1.0 / d))
        dh_ref[...] = dh
        part = jnp.zeros((8, LANES), F32) + 0.5 * jnp.sum(err * err) * (1.0 / d)

        @pl.when(i == 0)
        def _():
            dg_ref[...] = dg
            loss_ref[...] = part

        @pl.when(i > 0)
        def _():
            dg_ref[...] += dg
            loss_ref[...] += part

    return _pallas(
        body,
        out_shape=[jax.ShapeDtypeStruct((cfg.rp, d), F32), jax.ShapeDtypeStruct((1, d), F32),
                   jax.ShapeDtypeStruct((8, LANES), F32)],
        grid=(nr,),
        in_specs=[_rows(tr, d), _full(normf), pl.BlockSpec((tr, d), lambda i: (jnp.maximum(i - 1, 0), 0))],
        out_specs=[_rows(tr, d), pl.BlockSpec((1, d), lambda i: (0, 0)), pl.BlockSpec((8, LANES), lambda i: (0, 0))],
        name="final_loss", compiler_params=_params((nr,)))(h2, normf, tgt)


ANY = pl.BlockSpec(memory_space=pl.ANY)


def _place():
    x, y, c = lax.axis_index("x"), lax.axis_index("y"), lax.axis_index("c")
    others = [(1 - x, y), (x, 1 - y), (1 - x, 1 - y)]
    return x, y, c, others


def _row_tile(rows, cap=256):
    return max(t for t in range(16, min(rows, cap) + 1, 16) if rows % t == 0)


class Comm(NamedTuple):
    ins: list
    outs: list
    sems: list
    start: object
    finish: object


def _run_comm(name, comm):
    n_in, n_out = len(comm.ins), len(comm.outs)

    def body(*refs):
        ins, outs, sems = refs[:n_in], refs[n_in:n_in + n_out], refs[n_in + n_out:]
        comm.start(ins, outs, sems)
        comm.finish(ins, outs, sems)

    return _pallas(body, out_shape=comm.outs, in_specs=[ANY] * n_in, out_specs=[ANY] * n_out,
                   scratch_shapes=comm.sems, name=name)(*comm.ins)


def _gather_comm(ws):
    n = len(ws)
    halves = [w.shape[0] // 2 for w in ws]

    def copies(w_refs, o_refs, sems):
        send_sems, recv_sems = sems
        x, y, c, others = _place()
        me = 2 * x + y
        chips = [2 * px + py for px, py in others]

        def piece(a, chip, core):
            return o_refs[a].at[chip, pl.ds(core * halves[a], halves[a]), :]

        def copy(a, k, src, chip, core, to):
            return pltpu.make_async_remote_copy(src_ref=src, dst_ref=piece(a, chip, core), send_sem=send_sems.at[6 * a + k],
                                                recv_sem=recv_sems.at[6 * a + k], device_id=to, device_id_type=MESH)

        def first(j, a):
            return copy(a, j, w_refs[a].at[pl.ds(c * halves[a], halves[a]), :], me, c, (*others[j], c))

        def landed(j, a):
            return copy(a, j, piece(a, chips[j], c), chips[j], c, (x, y, c))

        def passed(j, a):
            return copy(a, 3 + j, piece(a, chips[j], c), chips[j], c, (x, y, 1 - c))

        def from_sibling(j, a):
            return copy(a, 3 + j, piece(a, chips[j], 1 - c), chips[j], 1 - c, (x, y, c))

        return first, landed, passed, from_sibling

    pairs = [(j, a) for j in range(3) for a in range(n)]

    def start(w_refs, o_refs, sems):
        first, _, _, _ = copies(w_refs, o_refs, sems)
        for j, a in pairs:
            first(j, a).start()

    def finish(w_refs, o_refs, sems):
        first, landed, passed, from_sibling = copies(w_refs, o_refs, sems)
        for j, a in pairs:
            landed(j, a).wait_recv()
            passed(j, a).start()
        for j, a in pairs:
            from_sibling(j, a).wait_recv()
        for j, a in pairs:
            first(j, a).wait_send()
            passed(j, a).wait_send()

    return Comm(list(ws), [jax.ShapeDtypeStruct((N_CHIPS,) + w.shape, w.dtype) for w in ws],
                [pltpu.SemaphoreType.DMA((6 * n,)), pltpu.SemaphoreType.DMA((6 * n,))], start, finish)


def _pair_exchange(name, gs):
    n = len(gs)

    def body(*refs):
        g_refs, o_refs = refs[:n], refs[n:2 * n]
        send_sems, recv_sems = refs[2 * n:]
        x, y, c, _ = _place()
        cps = []
        for a in range(n):
            half = gs[a].shape[1] // 2
            cp = pltpu.make_async_remote_copy(
                src_ref=g_refs[a].at[:, pl.ds((1 - c) * half, half), :], dst_ref=o_refs[a], send_sem=send_sems.at[a],
                recv_sem=recv_sems.at[a], device_id=(x, y, 1 - c), device_id_type=MESH)
            cp.start()
            cps.append(cp)
        for cp in cps:
            cp.wait()

    return _pallas(
        body, out_shape=[jax.ShapeDtypeStruct((N_CHIPS, g.shape[1] // 2, g.shape[2]), g.dtype) for g in gs],
        in_specs=[ANY] * n, out_specs=[ANY] * n,
        scratch_shapes=[pltpu.SemaphoreType.DMA((n,)), pltpu.SemaphoreType.DMA((n,))], name=name)(*gs)


def _pair_sum(name, g, recv, cidx):
    half, cols = recv.shape[1], recv.shape[2]
    tr = _row_tile(half)
    nblk = half // tr

    def body(c_ref, g_ref, r_ref, o_ref):
        o_ref[...] = (g_ref[...] + r_ref[...]).astype(o_ref.dtype)

    grid_spec = pltpu.PrefetchScalarGridSpec(
        num_scalar_prefetch=1, grid=(N_CHIPS, nblk),
        in_specs=[pl.BlockSpec((1, tr, cols), lambda s, i, c: (s, c[0] * nblk + i, 0)),
                  pl.BlockSpec((1, tr, cols), lambda s, i, c: (s, i, 0))],
        out_specs=pl.BlockSpec((1, tr, cols), lambda s, i, c: (s, i, 0)))
    return _pallas(
        body, out_shape=jax.ShapeDtypeStruct((N_CHIPS, half, cols), BF16), grid_spec=grid_spec,
        name=name, compiler_params=_params((N_CHIPS, nblk)))(cidx, g, recv)


def _exchange_comm(parts):
    n = len(parts)

    def copies(p_refs, o_refs, sems):
        send_sems, recv_sems = sems
        x, y, c, others = _place()
        me = 2 * x + y

        def copy(a, j, src_chip, dst_chip):
            px, py = others[j]
            return pltpu.make_async_remote_copy(
                src_ref=p_refs[a].at[src_chip], dst_ref=o_refs[a].at[dst_chip], send_sem=send_sems.at[3 * a + j],
                recv_sem=recv_sems.at[3 * a + j], device_id=(px, py, c), device_id_type=MESH)

        def send(j, a):
            return copy(a, j, 2 * others[j][0] + others[j][1], me)

        def arrival(j, a):
            return copy(a, j, me, 2 * others[j][0] + others[j][1])

        return send, arrival

    pairs = [(j, a) for j in range(3) for a in range(n)]

    def start(p_refs, o_refs, sems):
        send, _ = copies(p_refs, o_refs, sems)
        for j, a in pairs:
            send(j, a).start()

    def finish(p_refs, o_refs, sems):
        send, arrival = copies(p_refs, o_refs, sems)
        for j, a in pairs:
            arrival(j, a).wait_recv()
        for j, a in pairs:
            send(j, a).wait_send()

    return Comm(list(parts), [jax.ShapeDtypeStruct(p.shape, p.dtype) for p in parts],
                [pltpu.SemaphoreType.DMA((3 * n,)), pltpu.SemaphoreType.DMA((3 * n,))], start, finish)


def _chip_sum(name, part, slots, chip):
    half, cols = slots.shape[1], slots.shape[2]
    tr = _row_tile(half)

    def body(me_ref, p_ref, *rest):
        s_refs, o_ref = rest[:N_CHIPS], rest[N_CHIPS]
        own = p_ref[...].astype(F32)
        v = [jnp.where(me_ref[0] == k, own, s_refs[k][...].astype(F32)) for k in range(N_CHIPS)]
        o_ref[...] = ((v[0] + v[1]) + v[2]) + v[3]

    def slot_spec(k):
        return pl.BlockSpec((None, tr, cols), lambda i, me: (jnp.where(me[0] == k, (k + 1) % N_CHIPS, k), i, 0))

    grid_spec = pltpu.PrefetchScalarGridSpec(
        num_scalar_prefetch=1, grid=(half // tr,),
        in_specs=[pl.BlockSpec((None, tr, cols), lambda i, me: (me[0], i, 0))] + [slot_spec(k) for k in range(N_CHIPS)],
        out_specs=pl.BlockSpec((tr, cols), lambda i, me: (i, 0)))
    return _pallas(
        body, out_shape=jax.ShapeDtypeStruct((half, cols), F32), grid_spec=grid_spec,
        name=name, compiler_params=_params((half // tr,)))(chip, part, *([slots] * N_CHIPS))


def _pair_swap(fins):
    n = len(fins)

    def body(*refs):
        f_refs, o_refs = refs[:n], refs[n:2 * n]
        send_sems, recv_sems = refs[2 * n:]
        x, y, c, _ = _place()
        cps = [pltpu.make_async_remote_copy(src_ref=f_refs[a], dst_ref=o_refs[a], send_sem=send_sems.at[a],
                                            recv_sem=recv_sems.at[a], device_id=(x, y, 1 - c), device_id_type=MESH)
               for a in range(n)]
        for cp in cps:
            cp.start()
        for cp in cps:
            cp.wait()

    return _pallas(
        body, out_shape=[jax.ShapeDtypeStruct(f.shape, f.dtype) for f in fins], in_specs=[ANY] * n, out_specs=[ANY] * n,
        scratch_shapes=[pltpu.SemaphoreType.DMA((n,)), pltpu.SemaphoreType.DMA((n,))], name="grad_pair_swap")(*fins)


def _adamw(name, w, g_own, g_other, m, v, cidx):
    R, cols = w.shape[-2:]
    lead = (None,) * (w.ndim - 2)
    zeros = (0,) * (w.ndim - 2)
    half = R // 2
    tr = _row_tile(half, 128)
    nblk = half // tr
    c1 = 1.0 - ADAM_B1 ** ADAM_STEP
    c2 = 1.0 - ADAM_B2 ** ADAM_STEP

    def body(c_ref, w_ref, go_ref, gs_ref, m_ref, v_ref, g_ref, d_ref, nm_ref, nv_ref):
        mine = (pl.program_id(0) // nblk) == c_ref[0]
        gv = jnp.where(mine, go_ref[...], gs_ref[...])
        nm = ADAM_B1 * m_ref[...] + (1.0 - ADAM_B1) * gv
        nv = ADAM_B2 * v_ref[...] + (1.0 - ADAM_B2) * (gv * gv)
        g_ref[...] = gv
        d_ref[...] = -ADAM_LR * ((nm / c1) / (jnp.sqrt(nv / c2) + ADAM_EPS) + ADAM_WD * w_ref[...])
        nm_ref[...] = nm
        nv_ref[...] = nv

    spec = pl.BlockSpec(lead + (tr, cols), lambda i, c: zeros + (i, 0))
    hspec = pl.BlockSpec((tr, cols), lambda i, c: (i % nblk, 0))
    shape = jax.ShapeDtypeStruct(w.shape, F32)
    grid_spec = pltpu.PrefetchScalarGridSpec(num_scalar_prefetch=1, grid=(R // tr,),
                                             in_specs=[spec, hspec, hspec, spec, spec], out_specs=[spec] * 4)
    return _pallas(
        body, out_shape=[shape] * 4, grid_spec=grid_spec,
        name=name, compiler_params=_params((R // tr,)))(cidx, w, g_own, g_other, m, v)


PARAMS = (("meta", 1), ("norm1", None), ("w_in", 2), ("gdn_conv_w", 2), ("gdn_a_log", None), ("gdn_dt_bias", None),
          ("gdn_norm", None), ("w_out", 1), ("norm2", None), ("w_ffn_up", 2), ("ffn_conv_w", 2), ("ffn_conv_b", None),
          ("w_ffn_down", 1), ("norm_f", None))
BIG = ("w_in", "w_out", "w_ffn_up", "w_ffn_down")
PACK_ALIGN = 1024
PACK_ROWS_ALIGN = 32


def _pack(arrs, dtype):
    parts, total = [], 0
    for a in arrs:
        f = a.reshape(-1).astype(dtype)
        pad = (-f.shape[0]) % PACK_ALIGN
        parts.append(jnp.pad(f, (0, pad)) if pad else f)
        total += f.shape[0] + pad
    rows = total // LANES
    rpad = (-rows) % PACK_ROWS_ALIGN
    if rpad:
        parts.append(jnp.zeros((rpad * LANES,), dtype))
    return jnp.concatenate(parts).reshape(rows + rpad, LANES)


def _unpack(buf, shapes):
    flat = buf.reshape(-1)
    outs, off = [], 0
    for s in shapes:
        n = int(np.prod(s))
        outs.append(flat[off:off + n].reshape(s))
        off += n + (-n) % PACK_ALIGN
    return outs


def _split4(a, axis):
    n = a.shape[axis] // N_CHIPS
    return [lax.slice_in_dim(a, s * n, (s + 1) * n, axis=axis) for s in range(N_CHIPS)]


PROJ_ORDER = (3, 7, 8, 9, 0, 1, 2, 6, 4, 5)


def _reorder_w_in(w, cfg):
    d, hg = cfg.d, cfg.hg

    def block(k):
        off = k * d + (2 * hg if k >= 4 else 0)
        return w[:, off:off + d]

    tail = jnp.pad(w[:, 4 * d:4 * d + 2 * hg], ((0, 0), (0, LANES - 2 * hg)))
    return jnp.concatenate([block(k) for k in PROJ_ORDER] + [tail], axis=1)


def _restore_w_in(wr, cfg):
    d, hg = cfg.d, cfg.hg
    at = {k: i for i, k in enumerate(PROJ_ORDER)}
    block = lambda k: wr[:, at[k] * d:(at[k] + 1) * d]
    return jnp.concatenate([block(k) for k in range(4)] + [wr[:, 10 * d:10 * d + 2 * hg]] +
                           [block(k) for k in range(4, 10)], axis=1)


def _step(cfg, x, tgt, shard, m_shard, v_shard):
    d, hg, hr, dff, rp, tr, tm = cfg.d, cfg.hg, cfg.hr, cfg.dff, cfg.rp, cfg.tr, cfg.tm
    nrow = rp // tr
    assert cfg.tf * N_CHIPS == 2 * dff and cfg.din % N_CHIPS == 0
    cidx = lax.axis_index("c").astype(jnp.int32).reshape(1)
    chip = (2 * lax.axis_index("x") + lax.axis_index("y")).astype(jnp.int32).reshape(1)

    axis = dict(PARAMS)
    small = ("meta", "gdn_conv_w", "ffn_conv_w")
    small_shapes = [shard[n].shape for n in small]
    mine = [shard[n][0].astype(BF16) for n in BIG] + [_pack([shard[n] for n in small], F32)]

    def with_own(gathered, own):
        return [lax.dynamic_update_slice(g, w[None], (chip[0], 0, 0)) for g, w in zip(gathered, own)]

    g_in, g_small = with_own(_run_comm("weights_gather_first", _gather_comm([mine[0], mine[4]])), [mine[0], mine[4]])
    w_in_r = _reorder_w_in(jnp.concatenate([g_in[s] for s in range(N_CHIPS)], axis=1), cfg)
    per_chip = [_unpack(g_small[s], small_shapes) for s in range(N_CHIPS)]
    full = {n: jnp.concatenate([per_chip[s][k] for s in range(N_CHIPS)], axis=axis[n]) for k, n in enumerate(small)}
    meta = full["meta"]
    gconv_w = full["gdn_conv_w"][0]
    fconv_w = full["ffn_conv_w"][0]
    norm1, norm2, gnorm = shard["norm1"], shard["norm2"], shard["gdn_norm"]
    normf = shard["norm_f"].reshape(1, d)
    fconv_b = shard["ffn_conv_b"]
    alog = jnp.pad(shard["gdn_a_log"], ((0, 7), (0, LANES - hg)))
    dtb = jnp.pad(shard["gdn_dt_bias"], ((0, 7), (0, LANES - hg)))

    h0 = jnp.concatenate([jnp.zeros((cfg.front, d), F32), meta, x], axis=0)
    pos = jnp.arange(rp, dtype=F32) - float(cfg.front)
    half = RET_DK // 2
    inv = 1.0 / (ROPE_BASE ** (jnp.arange(half, dtype=F32) / half))
    ang = pos[:, None] * inv[None, :]
    cos, sin = jnp.cos(ang), jnp.sin(ang)
    rconsts = _ret_consts(cfg)

    rms_f = _make_rms_fn(cfg, tr, False)
    rms_b = _make_rms_fn(cfg, tr, True)
    rowshape = jax.ShapeDtypeStruct((rp, d), F32)
    rspec = _rows(tr, d)

    def rms_fwd(name, h, g):
        return _stage_fwd(name, rms_f, (nrow,), [In(h, rspec), In(g, _full(g))],
                          [jax.ShapeDtypeStruct((rp, d), BF16)], [rspec])[0]

    tn_in = cfg.pw // 9 if cfg.pw % (9 * LANES) == 0 else LANES
    hn1 = rms_fwd("rms1_fwd", h0, norm1)
    proj, rest = _mm("proj_fwd", hn1, w_in_r, tm=tm, tn=tn_in, tk=d, comm=_gather_comm(mine[1:4]))
    g_out, g_up, g_down = with_own(rest, mine[1:4])
    w_out = g_out.reshape(d, d)
    w_up = g_up
    w_up_t = jnp.swapaxes(g_up, 1, 2).reshape(2 * dff, d)
    w_down = g_down.reshape(dff, d)
    cqkv = _conv_fwd("gdn_conv_fwd", proj, CONV_COL * d, gconv_w, None, taps=GDN_CONV, width=3 * d, tr=tr, tc=d)
    tail_spec = _rows(tr, LANES, TAIL_COL * d // LANES)
    prep_fn = _make_gdn_prep_fn(cfg, tr)

    def prep_ins(dproj=None):
        return [In(cqkv, _rows(tr, 3 * d), F32), In(proj, tail_spec, BF16, galias=dproj, gshape=(rp, cfg.pw)),
                In(alog, _full(alog), F32, True), In(dtb, _full(dtb), F32, True)]

    qn, kn, vv, bB, lB = _stage_fwd("gdn_prep_fwd", prep_fn, (nrow,), prep_ins(), [rowshape] * 5, [rspec] * 5)

    trg = cfg.nb * CHUNK
    gi_grid = (rp // trg, hg)
    hspec = pl.BlockSpec((trg, GDN_DK), lambda i, h: (i, h))
    aspec = pl.BlockSpec((1, trg, CHUNK), lambda i, h: (h, i, 0))
    gspec = pl.BlockSpec((1, cfg.nb, 1, GDN_DK), lambda i, h: (h, i, 0, 0))
    intra_ins = [In(t, hspec, F32) for t in (qn, kn, vv, bB, lB)]
    ashape = jax.ShapeDtypeStruct((hg, rp, CHUNK), F32)
    intra_shapes = [rowshape, rowshape, ashape, rowshape, rowshape, jax.ShapeDtypeStruct((hg, cfg.nch, 1, GDN_DK), F32), ashape]
    intra_specs = [hspec, hspec, aspec, hspec, hspec, gspec, aspec]
    gu, gw, gattn, gqd, gkd, ggl, gtinv = _stage_fwd("gdn_intra_fwd", _gdn_intra_fn, gi_grid, intra_ins, intra_shapes,
                                                     intra_specs)
    oa, gss = _gdn_scan_fwd(cfg, gu, gw, gattn, gqd, gkd, ggl)

    rot_fn = _make_rot_fn(cfg)

    def rot_ins(dproj=None):
        return [In(proj, _rows(tr, 2 * d, ROT_COL // 2), BF16, galias=dproj, gshape=(rp, cfg.pw)),
                In(cos, _rows(tr, half)), In(sin, _rows(tr, half))]

    qr, kr = _stage_fwd("rot_fwd", rot_fn, (nrow,), rot_ins(), [rowshape] * 2, [rspec] * 2)
    ob, rss = _ret_scan_fwd(cfg, qr, kr, proj, rconsts)

    mix_fn = _make_mix_fn(cfg)
    mix_ins = [In(oa, rspec, F32), In(ob, rspec, F32), In(proj, _rows(tr, 4 * d, MIX_COL // 4), BF16, gshape=(rp, cfg.pw)),
               In(gnorm, _full(gnorm), F32, True)]
    ymix = _stage_fwd("mix_fwd", mix_fn, (nrow,), mix_ins, [jax.ShapeDtypeStruct((rp, d), BF16)], [rspec])[0]
    h1 = _mm("out_proj_fwd", ymix, w_out, tm=tm, tn=d, tk=d, add=h0)

    hn2 = rms_fwd("rms2_fwd", h1, norm2)
    up = _mm("ffn_up_fwd", hn2, w_up, tm=tm, tn=cfg.tf, tk=d)
    uc = _conv_fwd("ffn_conv_fwd", up, 0, fconv_w, fconv_b, taps=FFN_CONV, width=2 * dff, tr=tr, tc=cfg.tf)
    tra = tr // 2
    act_ins = [In(uc, _rows(tra, 2 * dff), F32)]
    act_spec = _rows(tra, dff)
    act = _stage_fwd("ffn_act_fwd", _act_fn, (rp // tra,), act_ins, [jax.ShapeDtypeStruct((rp, dff), BF16)], [act_spec])[0]
    h2 = _mm("ffn_down_fwd", act, w_down, tm=tm, tn=d, tk=cfg.tf, add=h1)

    dh2, g_normf, loss_blk = _final(cfg, h2, normf, tgt)
    loss = lax.psum(loss_blk[0, 0], ("x", "y", "c"))

    g_w_down = _mm_tn("ffn_down_dw", act, dh2, tr=tm, tka=cfg.tf, tn=d)
    dact = _mm("ffn_down_dx", dh2, w_down.T, tm=tm, tn=cfg.tf, tk=d)
    duc, = _stage_bwd("ffn_act_bwd", _act_fn, (rp // tra,), act_ins, [(dact, act_spec)])
    dup, g_fconv_w, g_fconv_b = _conv_bwd("ffn_conv_bwd", up, 0, fconv_w, duc, taps=FFN_CONV, width=2 * dff,
                                          tr=tr, tc=cfg.tf, with_bias=True)
    g_w_up = _mm_tn("ffn_up_dw", hn2, dup, tr=tm, tka=d, tn=cfg.tf, blocked=True)

    def pair_reduce(tag, names, arrs):
        recvs = _pair_exchange("grad_pair_exchange_" + tag, arrs)
        return [_pair_sum("grad_pair_sum_" + n, g, r, cidx) for n, g, r in zip(names, arrs, recvs)]

    parts_ffn = pair_reduce("ffn", ["w_ffn_down", "w_ffn_up"], [g_w_down.reshape(N_CHIPS, dff // N_CHIPS, d), g_w_up])
    dhn2, slots_ffn = _mm("ffn_up_dx", dup, w_up_t, tm=tm, tn=d, tk=cfg.tf, comm=_exchange_comm(parts_ffn))

    def rms_bwd(name, h, g, dhn, dres):
        ins = [In(h, rspec, F32), In(g, _full(g), F32, True)]
        return _stage_bwd(name, rms_b, (nrow,), ins, [(dhn, rspec), (dres, rspec)])

    dh1, g_norm2 = rms_bwd("rms2_bwd", h1, norm2, dhn2, dh2)
    g_w_out = _mm_tn("out_proj_dw", ymix, dh1, tr=tm, tka=d, tn=d)
    dymix = _mm("out_proj_dx", dh1, w_out.T, tm=tm, tn=d, tk=d)
    doa, dob, dproj, g_gnorm = _stage_bwd("mix_bwd", mix_fn, (nrow,), mix_ins, [(dymix, rspec)])

    dqr, dkr, dproj = _ret_scan_bwd(cfg, dob, qr, kr, proj, rconsts, rss, dproj)
    dproj, = _stage_bwd("rot_bwd", rot_fn, (nrow,), rot_ins(dproj), [(dqr, rspec), (dkr, rspec)])

    dgu, dgw, dgattn, dgqd, dgkd, dggl = _gdn_scan_bwd(cfg, doa, gu, gw, gattn, gqd, gkd, ggl, gss)
    intra_cots = [(dgu, hspec), (dgw, hspec), (dgattn, aspec), (dgqd, hspec), (dgkd, hspec), (dggl, gspec)]
    dqn, dkn, dvv, dbB, dlB = _stage_bwd("gdn_intra_bwd", _gdn_intra_fn, gi_grid, intra_ins + [In(gtinv, aspec)], intra_cots)
    dcqkv, dproj, g_alog, g_dtb = _stage_bwd(
        "gdn_prep_bwd", prep_fn, (nrow,), prep_ins(dproj), [(t, rspec) for t in (dqn, dkn, dvv, dbB, dlB)])
    dproj, g_gconv_w = _conv_bwd("gdn_conv_bwd", proj, CONV_COL * d, gconv_w, dcqkv, taps=GDN_CONV, width=3 * d,
                                 tr=tr, tc=d, with_bias=False, dx_into=dproj)
    g_w_in_r = _mm_tn("proj_dw", hn1, dproj, tr=tm, tka=d, tn=tn_in)
    g_in4 = _restore_w_in(g_w_in_r, cfg).reshape(d, N_CHIPS, cfg.din // N_CHIPS).transpose(1, 0, 2)
    parts_mix = pair_reduce("mix", ["w_out", "w_in"], [g_w_out.reshape(N_CHIPS, d // N_CHIPS, d), g_in4])
    dhn1, slots_mix = _mm("proj_dx", dproj, w_in_r.T, tm=tm, tn=d, tk=tn_in, comm=_exchange_comm(parts_mix))
    dh0, g_norm1 = rms_bwd("rms1_bwd", h0, norm1, dhn1, dh1)

    grad_x = dh0[cfg.xrow:]
    small_grads = {
        "meta": dh0[cfg.front:cfg.xrow], "norm1": g_norm1, "gdn_conv_w": g_gconv_w[None],
        "gdn_a_log": g_alog[0:1, :hg], "gdn_dt_bias": g_dtb[0:1, :hg], "gdn_norm": g_gnorm, "norm2": g_norm2,
        "ffn_conv_w": g_fconv_w[None], "ffn_conv_b": g_fconv_b, "norm_f": g_normf.reshape(d),
    }

    small_names = [n for n, _ in PARAMS if n not in BIG]
    g_small = jnp.stack([_pack([small_grads[n] if axis[n] is None else _split4(small_grads[n], axis[n])[s]
                                for n in small_names], F32) for s in range(N_CHIPS)])
    parts_small = pair_reduce("small", ["small"], [g_small])
    slots_small = _run_comm("grad_exchange_small", _exchange_comm(parts_small))
    tags = ["w_in", "w_out", "w_ffn_up", "w_ffn_down", "small"]
    parts = [parts_mix[1], parts_mix[0], parts_ffn[1], parts_ffn[0], parts_small[0]]
    slots = [slots_mix[1], slots_mix[0], slots_ffn[1], slots_ffn[0], slots_small[0]]
    fins = [_chip_sum("grad_chip_sum_" + t, p, s, chip) for t, p, s in zip(tags, parts, slots)]
    sibs = _pair_swap(fins)

    outs = {}
    for k, t in enumerate(BIG):
        outs[t] = _adamw("adamw_" + t, shard[t], fins[k], sibs[k], m_shard[t], v_shard[t], cidx)
    small_shapes_all = [shard[n].shape for n in small_names]
    pk = lambda src: _pack([src[n] for n in small_names], F32)
    res = _adamw("adamw_small", pk(shard), fins[4], sibs[4], pk(m_shard), pk(v_shard), cidx)
    for k, r in enumerate(res):
        for n, a in zip(small_names, _unpack(r, small_shapes_all)):
            outs.setdefault(n, [None] * 4)[k] = a
    names = [n for n, _ in PARAMS]
    return (loss, grad_x[None], *[outs[n][k] for k in range(4) for n in names])


def kernel(x, meta, norm1, w_in, gdn_conv_w, gdn_a_log, gdn_dt_bias, gdn_norm, w_out, norm2, w_ffn_up, ffn_conv_w, ffn_conv_b, w_ffn_down, norm_f, loss_target, m_meta, m_norm1, m_w_in, m_gdn_conv_w, m_gdn_a_log, m_gdn_dt_bias, m_gdn_norm, m_w_out, m_norm2, m_w_ffn_up, m_ffn_conv_w, m_ffn_conv_b, m_w_ffn_down, m_norm_f, v_meta, v_norm1, v_w_in, v_gdn_conv_w, v_gdn_a_log, v_gdn_dt_bias, v_gdn_norm, v_w_out, v_norm2, v_w_ffn_up, v_ffn_conv_w, v_ffn_conv_b, v_w_ffn_down, v_norm_f):
    names = [n for n, _ in PARAMS]
    shard = dict(zip(names, (meta, norm1, w_in, gdn_conv_w, gdn_a_log, gdn_dt_bias, gdn_norm, w_out, norm2, w_ffn_up,
                             ffn_conv_w, ffn_conv_b, w_ffn_down, norm_f)))
    m_shard = dict(zip(names, (m_meta, m_norm1, m_w_in, m_gdn_conv_w, m_gdn_a_log, m_gdn_dt_bias, m_gdn_norm, m_w_out,
                               m_norm2, m_w_ffn_up, m_ffn_conv_w, m_ffn_conv_b, m_w_ffn_down, m_norm_f)))
    v_shard = dict(zip(names, (v_meta, v_norm1, v_w_in, v_gdn_conv_w, v_gdn_a_log, v_gdn_dt_bias, v_gdn_norm, v_w_out,
                               v_norm2, v_w_ffn_up, v_ffn_conv_w, v_ffn_conv_b, v_w_ffn_down, v_norm_f)))
    return _step(REAL, x[0], loss_target[0], shard, m_shard, v_shard)
```

```python
import functools
from typing import NamedTuple

import numpy as np
import jax
import jax.numpy as jnp
from jax import lax
from jax.experimental import pallas as pl
from jax.experimental.pallas import tpu as pltpu

F32 = jnp.float32
BF16 = jnp.bfloat16
EPS = 1e-6
CHUNK = 64
GDN_DK = 128
RET_DK = 256
GDN_CONV = 4
FFN_CONV = 3
ROPE_BASE = 10000.0
LANES = 128
N_CHIPS = 4
ADAM_LR, ADAM_B1, ADAM_B2, ADAM_EPS, ADAM_WD, ADAM_STEP = 0.001, 0.9, 0.999, 1e-08, 0.01, 10
MIX_COL, CONV_COL, RV_BLOCK, ROT_COL, TAIL_COL = 0, 4, 7, 8, 10
MESH = pl.DeviceIdType.MESH
VMEM_LIMIT = 56 * 1024 * 1024


class Cfg(NamedTuple):
    d: int
    seq: int
    n_meta: int
    dff: int
    tr: int
    nb: int
    tm: int
    tf: int
    sc: int

    @property
    def hg(self): return self.d // GDN_DK
    @property
    def hr(self): return self.d // RET_DK
    @property
    def L(self): return self.n_meta + self.seq
    @property
    def rp(self): return -(-self.L // 256) * 256
    @property
    def front(self): return self.rp - self.L
    @property
    def xrow(self): return self.rp - self.seq
    @property
    def nch(self): return self.rp // CHUNK
    @property
    def pw(self): return 10 * self.d + LANES
    @property
    def din(self): return 10 * self.d + 2 * self.hg


REAL = Cfg(d=1024, seq=8192, n_meta=16, dff=2816, tr=256, nb=12, tm=1408, tf=1408, sc=4)


def _pallas(body, **kw):
    return pl.pallas_call(body, **kw)


def _sigmoid(x):
    return 1.0 / (1.0 + jnp.exp(-x))


def _silu(x):
    return x * _sigmoid(x)


def _softplus(x):
    return jnp.maximum(x, 0.0) + jnp.log(1.0 + jnp.exp(-jnp.abs(x)))


def _raw_dot(a, b, ta, tb, hi):
    if not hi:
        a = a.astype(BF16)
        b = b.astype(BF16)
    nbatch = a.ndim - 2
    ca = a.ndim - 2 if ta else a.ndim - 1
    cb = b.ndim - 1 if tb else b.ndim - 2
    batch = tuple(range(nbatch))
    return lax.dot_general(a, b, (((ca,), (cb,)), (batch, batch)),
                           precision=lax.Precision.HIGHEST if hi else None,
                           preferred_element_type=F32)


@functools.partial(jax.custom_vjp, nondiff_argnums=(2, 3, 4))
def _dot_p(a, b, ta, tb, hi):
    return _raw_dot(a, b, ta, tb, hi)


def _dot(a, b, ta=False, tb=False, hi=False):
    return _dot_p(a, b, ta, tb, hi)


def _dot_fwd(a, b, ta, tb, hi):
    return _raw_dot(a, b, ta, tb, hi), (a, b)


def _dot_bwd(ta, tb, hi, res, g):
    a, b = res
    if not ta and not tb:
        da, db = _dot(g, b, False, True, hi), _dot(a, g, True, False, hi)
    elif not ta and tb:
        da, db = _dot(g, b, False, False, hi), _dot(g, a, True, False, hi)
    elif ta and not tb:
        da, db = _dot(b, g, False, True, hi), _dot(a, g, False, False, hi)
    else:
        raise NotImplementedError
    return da.astype(a.dtype), db.astype(b.dtype)


_dot_p.defvjp(_dot_fwd, _dot_bwd)


def _iota2(n, m, axis):
    return lax.broadcasted_iota(jnp.int32, (n, m), axis)


def _bcast(mat, nb):
    return jnp.broadcast_to(mat[None], (nb,) + mat.shape)


def _split3(a):
    a0 = a.astype(BF16)
    r1 = a - a0.astype(F32)
    a1 = r1.astype(BF16)
    return a0, a1, (r1 - a1.astype(F32)).astype(BF16)


@functools.partial(jax.custom_vjp, nondiff_argnums=(2,))
def _dot_sel(a, e, te):
    eb = e.astype(BF16)
    p0, p1, p2 = (_raw_dot(p, eb, False, te, False) for p in _split3(a))
    return p0 + (p1 + p2)


def _dot_sel_fwd(a, e, te):
    return _dot_sel(a, e, te), e


def _dot_sel_bwd(te, e, g):
    return _dot_sel(g, e, not te), jnp.zeros_like(e)


_dot_sel.defvjp(_dot_sel_fwd, _dot_sel_bwd)


@jax.custom_vjp
def _sel_dot(e, x):
    eb = e.astype(BF16)
    p0, p1, p2 = (_raw_dot(eb, p, False, False, False) for p in _split3(x))
    return p0 + (p1 + p2)


def _sel_dot_fwd(e, x):
    return _sel_dot(e, x), e


def _sel_dot_bwd(e, g):
    eb = e.astype(BF16)
    p0, p1, p2 = (_raw_dot(eb, p, True, False, False) for p in _split3(g))
    return jnp.zeros_like(e), p0 + (p1 + p2)


_sel_dot.defvjp(_sel_dot_fwd, _sel_dot_bwd)


def _tri_inv_raw(m):
    nb = m.shape[0]
    r, c = _iota2(CHUNK, CHUNK, 0), _iota2(CHUNK, CHUNK, 1)
    t = _bcast((r == c).astype(F32), nb)
    b = 1
    while b < CHUNK:
        sh = b.bit_length() - 1
        off = ((r >> (sh + 1)) == (c >> (sh + 1))) & ((r >> sh) != (c >> sh)) & (r > c)
        cl = jnp.where(off[None], m, 0.0)
        t = t - _raw_dot(_raw_dot(t, cl, False, False, False), t, False, False, False)
        b *= 2
    return t


@jax.custom_vjp
def _tri_inv_given(m, t):
    return t


def _tri_inv_fwd(m, t):
    return t, t


def _tri_inv_bwd(t, g):
    return -_raw_dot(_raw_dot(t, g, True, False, False), t, False, True, False), jnp.zeros_like(t)


_tri_inv_given.defvjp(_tri_inv_fwd, _tri_inv_bwd)


def _rms(h, g):
    return h * lax.rsqrt(jnp.mean(h * h, axis=-1, keepdims=True) + EPS) * g


class In(NamedTuple):
    arr: jax.Array
    spec: pl.BlockSpec
    grad: object = None
    acc: bool = False
    gshape: object = None
    gspec: object = None
    galias: object = None


def _params(grid):
    sem = ("arbitrary",) * len(grid)
    return pltpu.CompilerParams(dimension_semantics=sem, vmem_limit_bytes=VMEM_LIMIT)


def _stage_fwd(name, fn, grid, ins, out_shapes, out_specs):
    n_in = len(ins)

    def body(*refs):
        pids = tuple(pl.program_id(k) for k in range(len(grid)))
        vals = [r[...] for r in refs[:n_in]]
        outs = fn(pids, *vals)
        for o_ref, o in zip(refs[n_in:], outs):
            o_ref[...] = o.reshape(o_ref.shape).astype(o_ref.dtype)

    return _pallas(
        body, out_shape=out_shapes, grid=grid, in_specs=[i.spec for i in ins],
        out_specs=out_specs, name=name, compiler_params=_params(grid))(*[i.arr for i in ins])


def _stage_bwd(name, fn, grid, ins, cots):
    n_in, n_ct = len(ins), len(cots)
    didx = [k for k, i in enumerate(ins) if i.grad is not None]
    aliased = [(o, ins[k].galias) for o, k in enumerate(didx) if ins[k].galias is not None]
    n_al = len(aliased)

    def body(*refs):
        pids = tuple(pl.program_id(k) for k in range(len(grid)))
        vals = [r[...] for r in refs[:n_in]]
        ct_refs = refs[n_in:n_in + n_ct]
        g_refs = refs[n_in + n_ct + n_al:]

        def f(*dv):
            merged = list(vals)
            for k, v in zip(didx, dv):
                merged[k] = v
            return tuple(fn(pids, *merged))

        outs, vjp_fn = jax.vjp(f, *[vals[k].astype(F32) for k in didx])
        cts = tuple(c[...].reshape(o.shape).astype(F32) for c, o in zip(ct_refs, outs))
        grads = vjp_fn(cts)
        first = functools.reduce(jnp.logical_and, [p == 0 for p in pids])
        for k, g_ref, g in zip(didx, g_refs, grads):
            if ins[k].acc:
                @pl.when(first)
                def _(g_ref=g_ref):
                    g_ref[...] = jnp.zeros(g_ref.shape, g_ref.dtype)
                g_ref[...] += g.reshape(g_ref.shape).astype(g_ref.dtype)
            else:
                g_ref[...] = g.reshape(g_ref.shape).astype(g_ref.dtype)

    out_shapes = [jax.ShapeDtypeStruct(ins[k].gshape or ins[k].arr.shape, ins[k].grad) for k in didx]
    out_specs = [ins[k].gspec or ins[k].spec for k in didx]
    return _pallas(
        body, out_shape=out_shapes, grid=grid,
        in_specs=[i.spec for i in ins] + [c[1] for c in cots] + [ANY] * n_al, out_specs=out_specs,
        input_output_aliases={n_in + n_ct + a: o for a, (o, _) in enumerate(aliased)},
        name=name, compiler_params=_params(grid))(*[i.arr for i in ins], *[c[0] for c in cots], *[a for _, a in aliased])


def _full(arr):
    nd = arr.ndim
    return pl.BlockSpec(arr.shape, lambda *p: (0,) * nd)


def _rows(tr, width, blk=0):
    return pl.BlockSpec((tr, width), lambda i: (i, blk))


def _mm(name, a, b, *, tm, tn, tk, out_dtype=F32, add=None, comm=None):
    M, K = a.shape
    N = b.shape[1] if b.ndim == 2 else b.shape[0] * b.shape[2]
    nk = K // tk
    grid = (M // tm, N // tn, nk)
    n_in = 3 if add is not None else 2
    n_ci, n_co = (len(comm.ins), len(comm.outs)) if comm is not None else (0, 0)

    def body(*refs):
        a_ref, b_ref = refs[0], refs[1]
        add_ref = refs[2] if add is not None else None
        c_ins = refs[n_in:n_in + n_ci]
        o_ref = refs[n_in + n_ci]
        c_outs = refs[n_in + n_ci + 1:n_in + n_ci + 1 + n_co]
        scratch = refs[n_in + n_ci + 1 + n_co:]
        acc_ref = scratch[0] if nk > 1 else None
        sems = scratch[1 if nk > 1 else 0:]
        step = (pl.program_id(0) * grid[1] + pl.program_id(1)) * nk + pl.program_id(2)
        if comm is not None:
            @pl.when(step == 0)
            def _():
                comm.start(c_ins, c_outs, sems)

        part = _raw_dot(a_ref[...], b_ref[...], False, False, False)

        def finish(total):
            if add_ref is not None:
                total = total + add_ref[...]
            o_ref[...] = total.astype(o_ref.dtype)

        if nk == 1:
            finish(part)
        else:
            k = pl.program_id(2)

            @pl.when(k == 0)
            def _():
                acc_ref[...] = part

            @pl.when(k > 0)
            def _():
                acc_ref[...] += part

            @pl.when(k == nk - 1)
            def _():
                finish(acc_ref[...])

        if comm is not None:
            @pl.when(step == grid[0] * grid[1] * nk - 1)
            def _():
                comm.finish(c_ins, c_outs, sems)

    b_spec = (pl.BlockSpec((tk, tn), lambda i, j, k: (k, j)) if b.ndim == 2 else
              pl.BlockSpec((None, tk, tn), lambda i, j, k: (j, k, 0)))
    in_specs = [pl.BlockSpec((tm, tk), lambda i, j, k: (i, k)), b_spec]
    args = [a, b]
    if add is not None:
        in_specs.append(pl.BlockSpec((tm, tn), lambda i, j, k: (i, j)))
        args.append(add)
    out_shape = jax.ShapeDtypeStruct((M, N), out_dtype)
    out_spec = pl.BlockSpec((tm, tn), lambda i, j, k: (i, j))
    scratch = [pltpu.VMEM((tm, tn), F32)] if nk > 1 else []
    if comm is None:
        return _pallas(body, out_shape=out_shape, grid=grid, in_specs=in_specs, out_specs=out_spec,
                       scratch_shapes=scratch, name=name, compiler_params=_params(grid))(*args)
    res = _pallas(body, out_shape=[out_shape] + comm.outs, grid=grid, in_specs=in_specs + [ANY] * n_ci,
                  out_specs=[out_spec] + [ANY] * n_co, scratch_shapes=scratch + comm.sems, name=name,
                  compiler_params=_params(grid))(*args, *comm.ins)
    return res[0], res[1:]


def _mm_tn(name, a, b, *, tr, tka, tn, blocked=False):
    R, Ka = a.shape
    N = b.shape[1]
    nr = R // tr
    grid = (Ka // tka, N // tn, nr)
    if blocked:
        out_shape = jax.ShapeDtypeStruct((N // tn, Ka, tn), F32)
        out_spec = pl.BlockSpec((None, tka, tn), lambda i, j, r: (j, i, 0))
    else:
        out_shape = jax.ShapeDtypeStruct((Ka, N), F32)
        out_spec = pl.BlockSpec((tka, tn), lambda i, j, r: (i, j))

    def body(a_ref, b_ref, o_ref):
        r = pl.program_id(2)
        part = _raw_dot(a_ref[...], b_ref[...], True, False, False)

        @pl.when(r == 0)
        def _():
            o_ref[...] = part

        @pl.when(r > 0)
        def _():
            o_ref[...] += part

    return _pallas(
        body, out_shape=out_shape, grid=grid,
        in_specs=[pl.BlockSpec((tr, tka), lambda i, j, r: (r, i)),
                  pl.BlockSpec((tr, tn), lambda i, j, r: (r, j))],
        out_specs=out_spec, name=name, compiler_params=_params(grid))(a, b)


def _conv_fwd(name, x, xcol0, w, b, *, taps, width, tr, tc):
    R = x.shape[0]
    grid = (width // tc, R // tr)
    cb0 = xcol0 // tc
    hb = tr // 8

    def body(*refs):
        x_ref, xp_ref, w_ref = refs[:3]
        b_ref = refs[3] if b is not None else None
        o_ref = refs[-1]
        i = pl.program_id(1)
        xv = x_ref[...]
        prev = jnp.where(i > 0, xp_ref[...], 0.0)
        ext = jnp.concatenate([prev, xv], axis=0)
        acc = xv * w_ref[taps - 1:taps, :]
        for s in range(1, taps):
            acc = acc + pltpu.roll(ext, s, 0)[8:, :] * w_ref[taps - 1 - s:taps - s, :]
        if b_ref is not None:
            acc = acc + b_ref[...]
        o_ref[...] = acc

    in_specs = [pl.BlockSpec((tr, tc), lambda j, i: (i, cb0 + j)),
                pl.BlockSpec((8, tc), lambda j, i: (jnp.maximum(i * hb - 1, 0), cb0 + j)),
                pl.BlockSpec((taps, tc), lambda j, i: (0, j))]
    args = [x, x, w]
    if b is not None:
        in_specs.append(pl.BlockSpec((1, tc), lambda j, i: (0, j)))
        args.append(b)
    return _pallas(
        body, out_shape=jax.ShapeDtypeStruct((R, width), F32), grid=grid, in_specs=in_specs,
        out_specs=pl.BlockSpec((tr, tc), lambda j, i: (i, j)),
        name=name, compiler_params=_params(grid))(*args)


def _conv_bwd(name, x, xcol0, w, dy, *, taps, width, tr, tc, with_bias, dx_into=None):
    R = x.shape[0]
    nr = R // tr
    grid = (width // tc, nr)
    cb0 = xcol0 // tc
    hb = tr // 8
    n_ext = tr + 8
    n_al = 0 if dx_into is None else 1

    def body(*refs):
        x_ref, w_ref, dy_ref, dyn_ref = refs[:4]
        dx_ref, dw_ref = refs[4 + n_al], refs[5 + n_al]
        db_ref = refs[6 + n_al] if with_bias else None
        i = pl.program_id(1)
        xv = x_ref[...]
        dyv = dy_ref[...]
        dext = jnp.concatenate([dyv, jnp.where(i < nr - 1, dyn_ref[...], 0.0)], axis=0)
        dx = dyv * w_ref[taps - 1:taps, :]
        dws = [None] * taps
        dws[taps - 1] = jnp.sum(xv * dyv, axis=0, keepdims=True)
        for s in range(1, taps):
            ahead = pltpu.roll(dext, n_ext - s, 0)[:tr, :]
            dx = dx + ahead * w_ref[taps - 1 - s:taps - s, :]
            dws[taps - 1 - s] = jnp.sum(xv * ahead, axis=0, keepdims=True)
        dx_ref[...] = dx.astype(dx_ref.dtype)

        @pl.when(i == 0)
        def _():
            for k in range(taps):
                dw_ref[k:k + 1, :] = dws[k]
            if db_ref is not None:
                db_ref[...] = jnp.sum(dyv, axis=0, keepdims=True)

        @pl.when(i > 0)
        def _():
            for k in range(taps):
                dw_ref[k:k + 1, :] += dws[k]
            if db_ref is not None:
                db_ref[...] += jnp.sum(dyv, axis=0, keepdims=True)

    in_specs = [pl.BlockSpec((tr, tc), lambda j, i: (i, cb0 + j)),
                pl.BlockSpec((taps, tc), lambda j, i: (0, j)),
                pl.BlockSpec((tr, tc), lambda j, i: (i, j)),
                pl.BlockSpec((8, tc), lambda j, i: (jnp.minimum((i + 1) * hb, R // 8 - 1), j))]
    args = [x, w, dy, dy]
    if dx_into is None:
        dx_shape, dx_spec, aliases = jax.ShapeDtypeStruct((R, width), BF16), pl.BlockSpec((tr, tc), lambda j, i: (i, j)), {}
    else:
        dx_shape = jax.ShapeDtypeStruct(dx_into.shape, dx_into.dtype)
        dx_spec, aliases = pl.BlockSpec((tr, tc), lambda j, i: (i, cb0 + j)), {4: 0}
        in_specs.append(ANY)
        args.append(dx_into)
    out_shape = [dx_shape, jax.ShapeDtypeStruct((taps, width), F32)]
    out_specs = [dx_spec, pl.BlockSpec((taps, tc), lambda j, i: (0, j))]
    if with_bias:
        out_shape.append(jax.ShapeDtypeStruct((1, width), F32))
        out_specs.append(pl.BlockSpec((1, tc), lambda j, i: (0, j)))
    return _pallas(
        body, out_shape=out_shape, grid=grid, in_specs=in_specs, out_specs=out_specs, input_output_aliases=aliases,
        name=name, compiler_params=_params(grid))(*args)


def _row_mask(cfg, i, tr):
    rows = i * tr + lax.broadcasted_iota(jnp.int32, (tr, 1), 0)
    return (rows >= cfg.front).astype(F32)


def _make_rms_fn(cfg, tr, with_residual):
    def fn(pids, h, g):
        hm = h * _row_mask(cfg, pids[0], tr)
        if with_residual:
            return _rms(hm, g), hm
        return (_rms(hm, g),)
    return fn


def _make_gdn_prep_fn(cfg, tr):
    d, hg = cfg.d, cfg.hg

    def fn(pids, c, tail, alog, dtb):
        cq, ck, cv = c[:, :d], c[:, d:2 * d], c[:, 2 * d:]
        mask = _row_mask(cfg, pids[0], tr)
        j, col = _iota2(LANES, d, 0), _iota2(LANES, d, 1)
        ea = ((col >> 7) == j).astype(F32)
        eb = ((col >> 7) + hg == j).astype(F32)
        al = jnp.sum(alog, axis=0, keepdims=True)
        db = jnp.sum(dtb, axis=0, keepdims=True)
        lg = _dot_sel(-jnp.exp(al) * _softplus(tail + db) * mask, ea, False)
        beta = _dot_sel(_sigmoid(tail) * mask, eb, False)
        sq, sk, sv = _silu(cq), _silu(ck), _silu(cv)
        qs, ks = [], []
        for h in range(hg):
            sl = slice(h * GDN_DK, (h + 1) * GDN_DK)
            qh, kh = sq[:, sl], sk[:, sl]
            qs.append(qh * lax.rsqrt(jnp.sum(qh * qh, axis=-1, keepdims=True) + EPS) * (GDN_DK ** -0.5))
            ks.append(kh * lax.rsqrt(jnp.sum(kh * kh, axis=-1, keepdims=True) + EPS))
        return jnp.concatenate(qs, axis=1), jnp.concatenate(ks, axis=1), sv, beta, lg
    return fn


def _gdn_intra_fn(pids, q, k, v, bB, lB, t_saved=None):
    rows = q.shape[0]
    nb = rows // CHUNK
    q3, k3, v3, b3, l3 = [t.reshape(nb, CHUNK, GDN_DK) for t in (q, k, v, bB, lB)]
    r, c = _iota2(CHUNK, CHUNK, 0), _iota2(CHUNK, CHUNK, 1)
    tril = (r >= c)
    strict = (r > c)
    gcol = _sel_dot(_bcast(tril.astype(F32), nb), l3)
    l64 = l3[:, :, :CHUNK]
    grow = _sel_dot(jnp.ones((nb, CHUNK, CHUNK), F32), l64 * (r <= c).astype(F32)[None])
    diff = gcol[:, :, :CHUNK] - grow
    decay = jnp.where(tril[None], jnp.exp(jnp.where(tril[None], diff, 0.0)), 0.0)
    kb = k3 * b3
    m = jnp.where(strict[None], _dot(kb, k3, False, True) * decay, 0.0)
    t = _tri_inv_raw(m) if t_saved is None else _tri_inv_given(m, t_saved.reshape(nb, CHUNK, CHUNK))
    eg = jnp.exp(gcol)
    u = _dot(t, v3 * b3)
    w = _dot(t, kb * eg)
    attn = _dot(q3, k3, False, True) * decay
    qd = q3 * eg
    glast = jnp.sum(l3, axis=1, keepdims=True)
    kd = k3 * jnp.exp(glast - gcol)
    gl = jnp.exp(glast)
    outs = (u.reshape(rows, GDN_DK), w.reshape(rows, GDN_DK), attn.reshape(1, rows, CHUNK),
            qd.reshape(rows, GDN_DK), kd.reshape(rows, GDN_DK), gl.reshape(1, nb, 1, GDN_DK))
    return outs + (t.reshape(1, rows, CHUNK),) if t_saved is None else outs


def _make_rot_fn(cfg):
    hr = cfg.hr
    half = RET_DK // 2

    def fn(pids, rqk, cos, sin):
        rq, rk = rqk[:, :cfg.d], rqk[:, cfg.d:]

        def rot(t, scale):
            outs = []
            for h in range(hr):
                x1 = t[:, h * RET_DK:h * RET_DK + half]
                x2 = t[:, h * RET_DK + half:(h + 1) * RET_DK]
                outs += [(x1 * cos - x2 * sin) * scale, (x2 * cos + x1 * sin) * scale]
            return jnp.concatenate(outs, axis=1)
        return rot(rq, 1.0), rot(rk, RET_DK ** -0.5)
    return fn


def _make_mix_fn(cfg):
    hg, hr = cfg.hg, cfg.hr

    def fn(pids, oa, ob, pm, gnorm):
        d = cfg.d
        gz, rg, gate_a, gate_b = pm[:, :d], pm[:, d:2 * d], pm[:, 2 * d:3 * d], pm[:, 3 * d:]
        oas = []
        for h in range(hg):
            oh = oa[:, h * GDN_DK:(h + 1) * GDN_DK]
            oas.append(oh * lax.rsqrt(jnp.mean(oh * oh, axis=-1, keepdims=True) + EPS) * gnorm)
        ya = jnp.concatenate(oas, axis=1) * _silu(gz)
        obs = []
        for h in range(hr):
            oh = ob[:, h * RET_DK:(h + 1) * RET_DK]
            obs.append(oh * lax.rsqrt(jnp.mean(oh * oh, axis=-1, keepdims=True) + EPS))
        yb = _silu(rg) * jnp.concatenate(obs, axis=1)
        return (_sigmoid(gate_a) * ya + _sigmoid(gate_b) * yb,)
    return fn


def _act_fn(pids, u):
    f = u.shape[1] // 2
    return (_silu(u[:, :f]) * u[:, f:],)


def _gdn_step(s, u, w, a, qd, kd, gl):
    top = _dot(jnp.concatenate([w, qd], axis=0), s)
    v_new = u - top[:CHUNK]
    bot = _dot(jnp.concatenate([a, kd.T], axis=0), v_new)
    o = top[CHUNK:] + bot[:CHUNK]
    s2 = s * gl + bot[CHUNK:]
    return s2, o


def _ret_step(s, q, k, v, dm, qdc, kdc, g):
    att = _dot(q, k, False, True) * dm
    bot = _dot(jnp.concatenate([att, (k * kdc).T], axis=0), v)
    o = bot[:CHUNK] + _dot(q * qdc, s)
    s2 = s * g + bot[CHUNK:]
    return s2, o


def _gdn_scan_fwd(cfg, u, w, attn, qd, kd, gl):
    d, hg, nch, sc = cfg.d, cfg.hg, cfg.nch, cfg.sc
    nst = nch // sc

    def body(u_ref, w_ref, a_ref, qd_ref, kd_ref, gl_ref, o_ref, ss_ref, s_ref):
        @pl.when(pl.program_id(0) == 0)
        def _():
            s_ref[...] = jnp.zeros(s_ref.shape, F32)

        states = [s_ref[h] for h in range(hg)]
        for j in range(sc):
            rows = slice(j * CHUNK, (j + 1) * CHUNK)
            outs = []
            for h in range(hg):
                sl = slice(h * GDN_DK, (h + 1) * GDN_DK)
                ss_ref[j, h] = states[h]
                states[h], o = _gdn_step(states[h], u_ref[rows, sl], w_ref[rows, sl], a_ref[h, rows, :],
                                         qd_ref[rows, sl], kd_ref[rows, sl], gl_ref[h, j])
                outs.append(o)
            o_ref[rows, :] = jnp.concatenate(outs, axis=1)
        for h in range(hg):
            s_ref[h] = states[h]

    row = pl.BlockSpec((sc * CHUNK, d), lambda n: (n, 0))
    return _pallas(
        body,
        out_shape=[jax.ShapeDtypeStruct((cfg.rp, d), F32), jax.ShapeDtypeStruct((nch, hg, GDN_DK, GDN_DK), F32)],
        grid=(nst,),
        in_specs=[row, row, pl.BlockSpec((hg, sc * CHUNK, CHUNK), lambda n: (0, n, 0)), row, row,
                  pl.BlockSpec((hg, sc, 1, GDN_DK), lambda n: (0, n, 0, 0))],
        out_specs=[row, pl.BlockSpec((sc, hg, GDN_DK, GDN_DK), lambda n: (n, 0, 0, 0))],
        scratch_shapes=[pltpu.VMEM((hg, GDN_DK, GDN_DK), F32)],
        name="gdn_scan_fwd", compiler_params=_params((nst,)))(u, w, attn, qd, kd, gl)


def _gdn_scan_bwd(cfg, do, u, w, attn, qd, kd, gl, ss):
    d, hg, nch, sc = cfg.d, cfg.hg, cfg.nch, cfg.sc
    nst = nch // sc

    def body(do_ref, u_ref, w_ref, a_ref, qd_ref, kd_ref, gl_ref, ss_ref,
             du_ref, dw_ref, da_ref, dqd_ref, dkd_ref, dgl_ref, ds_ref):
        @pl.when(pl.program_id(0) == 0)
        def _():
            ds_ref[...] = jnp.zeros(ds_ref.shape, F32)

        dstates = [ds_ref[h] for h in range(hg)]
        for j in reversed(range(sc)):
            rows = slice(j * CHUNK, (j + 1) * CHUNK)
            dus, dws, dqds, dkds = [], [], [], []
            for h in range(hg):
                sl = slice(h * GDN_DK, (h + 1) * GDN_DK)
                args = (ss_ref[j, h], u_ref[rows, sl], w_ref[rows, sl], a_ref[h, rows, :], qd_ref[rows, sl],
                        kd_ref[rows, sl], gl_ref[h, j])
                _, vjp_fn = jax.vjp(_gdn_step, *args)
                dstates[h], du, dw, da, dqd, dkd, dgl = vjp_fn((dstates[h], do_ref[rows, sl]))
                da_ref[h, rows, :] = da
                dgl_ref[h, j] = dgl
                dus.append(du)
                dws.append(dw)
                dqds.append(dqd)
                dkds.append(dkd)
            du_ref[rows, :] = jnp.concatenate(dus, axis=1)
            dw_ref[rows, :] = jnp.concatenate(dws, axis=1)
            dqd_ref[rows, :] = jnp.concatenate(dqds, axis=1)
            dkd_ref[rows, :] = jnp.concatenate(dkds, axis=1)
        for h in range(hg):
            ds_ref[h] = dstates[h]

    row = pl.BlockSpec((sc * CHUNK, d), lambda n: (nst - 1 - n, 0))
    aspec = pl.BlockSpec((hg, sc * CHUNK, CHUNK), lambda n: (0, nst - 1 - n, 0))
    gspec = pl.BlockSpec((hg, sc, 1, GDN_DK), lambda n: (0, nst - 1 - n, 0, 0))
    rowshape = jax.ShapeDtypeStruct((cfg.rp, d), F32)
    return _pallas(
        body,
        out_shape=[rowshape, rowshape, jax.ShapeDtypeStruct(attn.shape, F32), rowshape, rowshape,
                   jax.ShapeDtypeStruct(gl.shape, F32)],
        grid=(nst,),
        in_specs=[row, row, row, aspec, row, row, gspec,
                  pl.BlockSpec((sc, hg, GDN_DK, GDN_DK), lambda n: (nst - 1 - n, 0, 0, 0))],
        out_specs=[row, row, aspec, row, row, gspec],
        scratch_shapes=[pltpu.VMEM((hg, GDN_DK, GDN_DK), F32)],
        name="gdn_scan_bwd", compiler_params=_params((nst,)))(do, u, w, attn, qd, kd, gl, ss)


def _ret_consts(cfg):
    hr = cfg.hr
    lg = np.log(1.0 - 2.0 ** (-5.0 - np.arange(hr, dtype=np.float64)))
    idx = np.arange(CHUNK, dtype=np.float64)
    tril = np.tril(np.ones((CHUNK, CHUNK), dtype=bool))
    dm = np.where(tril[None], np.exp((idx[:, None] - idx[None, :])[None] * lg[:, None, None]), 0.0)
    qdc = np.exp((idx[None, :] + 1.0) * lg[:, None])
    kdc = np.exp((CHUNK - 1.0 - idx[None, :]) * lg[:, None])
    gch = np.exp(CHUNK * lg)
    qdc = np.broadcast_to(qdc[:, :, None], (hr, CHUNK, RET_DK))
    kdc = np.broadcast_to(kdc[:, :, None], (hr, CHUNK, RET_DK))
    gch = np.broadcast_to(gch[:, None, None], (hr, 1, RET_DK))
    return tuple(jnp.asarray(np.ascontiguousarray(t), F32) for t in (dm, qdc, kdc, gch))


def _ret_scan_fwd(cfg, qr, kr, proj, consts):
    d, hr, nch, sc = cfg.d, cfg.hr, cfg.nch, cfg.sc
    nst = nch // sc
    dm, qdc, kdc, gch = consts

    def body(q_ref, k_ref, v_ref, dm_ref, qdc_ref, kdc_ref, g_ref, o_ref, ss_ref, s_ref):
        @pl.when(pl.program_id(0) == 0)
        def _():
            s_ref[...] = jnp.zeros(s_ref.shape, F32)

        states = [s_ref[h] for h in range(hr)]
        for j in range(sc):
            rows = slice(j * CHUNK, (j + 1) * CHUNK)
            outs = []
            for h in range(hr):
                sl = slice(h * RET_DK, (h + 1) * RET_DK)
                ss_ref[j, h] = states[h]
                states[h], o = _ret_step(states[h], q_ref[rows, sl], k_ref[rows, sl], v_ref[rows, sl], dm_ref[h],
                                         qdc_ref[h], kdc_ref[h], g_ref[h])
                outs.append(o)
            o_ref[rows, :] = jnp.concatenate(outs, axis=1)
        for h in range(hr):
            s_ref[h] = states[h]

    row = pl.BlockSpec((sc * CHUNK, d), lambda n: (n, 0))
    return _pallas(
        body,
        out_shape=[jax.ShapeDtypeStruct((cfg.rp, d), F32), jax.ShapeDtypeStruct((nch, hr, RET_DK, RET_DK), F32)],
        grid=(nst,),
        in_specs=[row, row, pl.BlockSpec((sc * CHUNK, d), lambda n: (n, RV_BLOCK)), _full(dm), _full(qdc), _full(kdc),
                  _full(gch)],
        out_specs=[row, pl.BlockSpec((sc, hr, RET_DK, RET_DK), lambda n: (n, 0, 0, 0))],
        scratch_shapes=[pltpu.VMEM((hr, RET_DK, RET_DK), F32)],
        name="ret_scan_fwd", compiler_params=_params((nst,)))(qr, kr, proj, dm, qdc, kdc, gch)


def _ret_scan_bwd(cfg, do, qr, kr, proj, consts, ss, dproj):
    d, hr, nch, sc = cfg.d, cfg.hr, cfg.nch, cfg.sc
    nst = nch // sc
    dm, qdc, kdc, gch = consts

    def body(do_ref, q_ref, k_ref, v_ref, dm_ref, qdc_ref, kdc_ref, g_ref, ss_ref, _, dq_ref, dk_ref, dv_ref, ds_ref):
        @pl.when(pl.program_id(0) == 0)
        def _():
            ds_ref[...] = jnp.zeros(ds_ref.shape, F32)

        dstates = [ds_ref[h] for h in range(hr)]
        for j in reversed(range(sc)):
            rows = slice(j * CHUNK, (j + 1) * CHUNK)
            dqs, dks, dvs = [], [], []
            for h in range(hr):
                sl = slice(h * RET_DK, (h + 1) * RET_DK)
                cs = (dm_ref[h], qdc_ref[h], kdc_ref[h], g_ref[h])
                _, vjp_fn = jax.vjp(lambda s, q, k, v, cs=cs: _ret_step(s, q, k, v, *cs),
                                    ss_ref[j, h], q_ref[rows, sl], k_ref[rows, sl], v_ref[rows, sl])
                dstates[h], dq, dk, dv = vjp_fn((dstates[h], do_ref[rows, sl]))
                dqs.append(dq)
                dks.append(dk)
                dvs.append(dv)
            dq_ref[rows, :] = jnp.concatenate(dqs, axis=1)
            dk_ref[rows, :] = jnp.concatenate(dks, axis=1)
            dv_ref[rows, :] = jnp.concatenate(dvs, axis=1).astype(dv_ref.dtype)
        for h in range(hr):
            ds_ref[h] = dstates[h]

    row = pl.BlockSpec((sc * CHUNK, d), lambda n: (nst - 1 - n, 0))
    rowshape = jax.ShapeDtypeStruct((cfg.rp, d), F32)
    vspec = pl.BlockSpec((sc * CHUNK, d), lambda n: (nst - 1 - n, RV_BLOCK))
    return _pallas(
        body,
        out_shape=[rowshape, rowshape, jax.ShapeDtypeStruct(dproj.shape, dproj.dtype)],
        grid=(nst,),
        in_specs=[row, row, row, vspec, _full(dm), _full(qdc), _full(kdc), _full(gch),
                  pl.BlockSpec((sc, hr, RET_DK, RET_DK), lambda n: (nst - 1 - n, 0, 0, 0)), ANY],
        out_specs=[row, row, vspec], input_output_aliases={9: 2},
        scratch_shapes=[pltpu.VMEM((hr, RET_DK, RET_DK), F32)],
        name="ret_scan_bwd", compiler_params=_params((nst,)))(do, qr, kr, proj, dm, qdc, kdc, gch, ss, dproj)


def _final(cfg, h2, normf, tgt):
    d, tr = cfg.d, cfg.xrow
    nr = cfg.rp // tr

    def body(h_ref, g_ref, t_ref, dh_ref, dg_ref, loss_ref):
        i = pl.program_id(0)
        y, vjp_fn = jax.vjp(_rms, h_ref[...], g_ref[...])
        err = jnp.where(i >= 1, y - t_ref[...], 0.0)
        dh, dg = vjp_fn(err * (1.0 / d))
        dh_ref[...] = dh
        part = jnp.zeros((8, LANES), F32) + 0.5 * jnp.sum(err * err) * (1.0 / d)

        @pl.when(i == 0)
        def _():
            dg_ref[...] = dg
            loss_ref[...] = part

        @pl.when(i > 0)
        def _():
            dg_ref[...] += dg
            loss_ref[...] += part

    return _pallas(
        body,
        out_shape=[jax.ShapeDtypeStruct((cfg.rp, d), F32), jax.ShapeDtypeStruct((1, d), F32),
                   jax.ShapeDtypeStruct((8, LANES), F32)],
        grid=(nr,),
        in_specs=[_rows(tr, d), _full(normf), pl.BlockSpec((tr, d), lambda i: (jnp.maximum(i - 1, 0), 0))],
        out_specs=[_rows(tr, d), pl.BlockSpec((1, d), lambda i: (0, 0)), pl.BlockSpec((8, LANES), lambda i: (0, 0))],
        name="final_loss", compiler_params=_params((nr,)))(h2, normf, tgt)


ANY = pl.BlockSpec(memory_space=pl.ANY)


def _place():
    x, y, c = lax.axis_index("x"), lax.axis_index("y"), lax.axis_index("c")
    others = [(1 - x, y), (x, 1 - y), (1 - x, 1 - y)]
    return x, y, c, others


def _row_tile(rows, cap=256):
    return max(t for t in range(16, min(rows, cap) + 1, 16) if rows % t == 0)


class Comm(NamedTuple):
    ins: list
    outs: list
    sems: list
    start: object
    finish: object


def _run_comm(name, comm):
    n_in, n_out = len(comm.ins), len(comm.outs)

    def body(*refs):
        ins, outs, sems = refs[:n_in], refs[n_in:n_in + n_out], refs[n_in + n_out:]
        comm.start(ins, outs, sems)
        comm.finish(ins, outs, sems)

    return _pallas(body, out_shape=comm.outs, in_specs=[ANY] * n_in, out_specs=[ANY] * n_out,
                   scratch_shapes=comm.sems, name=name)(*comm.ins)


def _gather_comm(ws):
    n = len(ws)
    halves = [w.shape[0] // 2 for w in ws]

    def copies(w_refs, o_refs, sems):
        send_sems, recv_sems = sems
        x, y, c, others = _place()
        me = 2 * x + y
        chips = [2 * px + py for px, py in others]

        def piece(a, chip, core):
            return o_refs[a].at[chip, pl.ds(core * halves[a], halves[a]), :]

        def copy(a, k, src, chip, core, to):
            return pltpu.make_async_remote_copy(src_ref=src, dst_ref=piece(a, chip, core), send_sem=send_sems.at[6 * a + k],
                                                recv_sem=recv_sems.at[6 * a + k], device_id=to, device_id_type=MESH)

        def first(j, a):
            return copy(a, j, w_refs[a].at[pl.ds(c * halves[a], halves[a]), :], me, c, (*others[j], c))

        def landed(j, a):
            return copy(a, j, piece(a, chips[j], c), chips[j], c, (x, y, c))

        def passed(j, a):
            return copy(a, 3 + j, piece(a, chips[j], c), chips[j], c, (x, y, 1 - c))

        def from_sibling(j, a):
            return copy(a, 3 + j, piece(a, chips[j], 1 - c), chips[j], 1 - c, (x, y, c))

        return first, landed, passed, from_sibling

    pairs = [(j, a) for j in range(3) for a in range(n)]

    def start(w_refs, o_refs, sems):
        first, _, _, _ = copies(w_refs, o_refs, sems)
        for j, a in pairs:
            first(j, a).start()

    def finish(w_refs, o_refs, sems):
        first, landed, passed, from_sibling = copies(w_refs, o_refs, sems)
        for j, a in pairs:
            landed(j, a).wait_recv()
            passed(j, a).start()
        for j, a in pairs:
            from_sibling(j, a).wait_recv()
        for j, a in pairs:
            first(j, a).wait_send()
            passed(j, a).wait_send()

    return Comm(list(ws), [jax.ShapeDtypeStruct((N_CHIPS,) + w.shape, w.dtype) for w in ws],
                [pltpu.SemaphoreType.DMA((6 * n,)), pltpu.SemaphoreType.DMA((6 * n,))], start, finish)


def _pair_exchange(name, gs):
    n = len(gs)

    def body(*refs):
        g_refs, o_refs = refs[:n], refs[n:2 * n]
        send_sems, recv_sems = refs[2 * n:]
        x, y, c, _ = _place()
        cps = []
        for a in range(n):
            half = gs[a].shape[1] // 2
            cp = pltpu.make_async_remote_copy(
                src_ref=g_refs[a].at[:, pl.ds((1 - c) * half, half), :], dst_ref=o_refs[a], send_sem=send_sems.at[a],
                recv_sem=recv_sems.at[a], device_id=(x, y, 1 - c), device_id_type=MESH)
            cp.start()
            cps.append(cp)
        for cp in cps:
            cp.wait()

    return _pallas(
        body, out_shape=[jax.ShapeDtypeStruct((N_CHIPS, g.shape[1] // 2, g.shape[2]), g.dtype) for g in gs],
        in_specs=[ANY] * n, out_specs=[ANY] * n,
        scratch_shapes=[pltpu.SemaphoreType.DMA((n,)), pltpu.SemaphoreType.DMA((n,))], name=name)(*gs)


def _pair_sum(name, g, recv, cidx):
    half, cols = recv.shape[1], recv.shape[2]
    tr = _row_tile(half)
    nblk = half // tr

    def body(c_ref, g_ref, r_ref, o_ref):
        o_ref[...] = (g_ref[...] + r_ref[...]).astype(o_ref.dtype)

    grid_spec = pltpu.PrefetchScalarGridSpec(
        num_scalar_prefetch=1, grid=(N_CHIPS, nblk),
        in_specs=[pl.BlockSpec((1, tr, cols), lambda s, i, c: (s, c[0] * nblk + i, 0)),
                  pl.BlockSpec((1, tr, cols), lambda s, i, c: (s, i, 0))],
        out_specs=pl.BlockSpec((1, tr, cols), lambda s, i, c: (s, i, 0)))
    return _pallas(
        body, out_shape=jax.ShapeDtypeStruct((N_CHIPS, half, cols), BF16), grid_spec=grid_spec,
        name=name, compiler_params=_params((N_CHIPS, nblk)))(cidx, g, recv)


def _exchange_comm(parts):
    n = len(parts)

    def copies(p_refs, o_refs, sems):
        send_sems, recv_sems = sems
        x, y, c, others = _place()
        me = 2 * x + y

        def copy(a, j, src_chip, dst_chip):
            px, py = others[j]
            return pltpu.make_async_remote_copy(
                src_ref=p_refs[a].at[src_chip], dst_ref=o_refs[a].at[dst_chip], send_sem=send_sems.at[3 * a + j],
                recv_sem=recv_sems.at[3 * a + j], device_id=(px, py, c), device_id_type=MESH)

        def send(j, a):
            return copy(a, j, 2 * others[j][0] + others[j][1], me)

        def arrival(j, a):
            return copy(a, j, me, 2 * others[j][0] + others[j][1])

        return send, arrival

    pairs = [(j, a) for j in range(3) for a in range(n)]

    def start(p_refs, o_refs, sems):
        send, _ = copies(p_refs, o_refs, sems)
        for j, a in pairs:
            send(j, a).start()

    def finish(p_refs, o_refs, sems):
        send, arrival = copies(p_refs, o_refs, sems)
        for j, a in pairs:
            arrival(j, a).wait_recv()
        for j, a in pairs:
            send(j, a).wait_send()

    return Comm(list(parts), [jax.ShapeDtypeStruct(p.shape, p.dtype) for p in parts],
                [pltpu.SemaphoreType.DMA((3 * n,)), pltpu.SemaphoreType.DMA((3 * n,))], start, finish)


def _chip_sum(name, part, slots, chip):
    half, cols = slots.shape[1], slots.shape[2]
    tr = _row_tile(half)

    def body(me_ref, p_ref, *rest):
        s_refs, o_ref = rest[:N_CHIPS], rest[N_CHIPS]
        own = p_ref[...].astype(F32)
        v = [jnp.where(me_ref[0] == k, own, s_refs[k][...].astype(F32)) for k in range(N_CHIPS)]
        o_ref[...] = ((v[0] + v[1]) + v[2]) + v[3]

    def slot_spec(k):
        return pl.BlockSpec((None, tr, cols), lambda i, me: (jnp.where(me[0] == k, (k + 1) % N_CHIPS, k), i, 0))

    grid_spec = pltpu.PrefetchScalarGridSpec(
        num_scalar_prefetch=1, grid=(half // tr,),
        in_specs=[pl.BlockSpec((None, tr, cols), lambda i, me: (me[0], i, 0))] + [slot_spec(k) for k in range(N_CHIPS)],
        out_specs=pl.BlockSpec((tr, cols), lambda i, me: (i, 0)))
    return _pallas(
        body, out_shape=jax.ShapeDtypeStruct((half, cols), F32), grid_spec=grid_spec,
        name=name, compiler_params=_params((half // tr,)))(chip, part, *([slots] * N_CHIPS))


def _pair_swap(fins):
    n = len(fins)

    def body(*refs):
        f_refs, o_refs = refs[:n], refs[n:2 * n]
        send_sems, recv_sems = refs[2 * n:]
        x, y, c, _ = _place()
        cps = [pltpu.make_async_remote_copy(src_ref=f_refs[a], dst_ref=o_refs[a], send_sem=send_sems.at[a],
                                            recv_sem=recv_sems.at[a], device_id=(x, y, 1 - c), device_id_type=MESH)
               for a in range(n)]
        for cp in cps:
            cp.start()
        for cp in cps:
            cp.wait()

    return _pallas(
        body, out_shape=[jax.ShapeDtypeStruct(f.shape, f.dtype) for f in fins], in_specs=[ANY] * n, out_specs=[ANY] * n,
        scratch_shapes=[pltpu.SemaphoreType.DMA((n,)), pltpu.SemaphoreType.DMA((n,))], name="grad_pair_swap")(*fins)


def _adamw(name, w, g_own, g_other, m, v, cidx):
    R, cols = w.shape[-2:]
    lead = (None,) * (w.ndim - 2)
    zeros = (0,) * (w.ndim - 2)
    half = R // 2
    tr = _row_tile(half, 128)
    nblk = half // tr
    c1 = 1.0 - ADAM_B1 ** ADAM_STEP
    c2 = 1.0 - ADAM_B2 ** ADAM_STEP

    def body(c_ref, w_ref, go_ref, gs_ref, m_ref, v_ref, g_ref, d_ref, nm_ref, nv_ref):
        mine = (pl.program_id(0) // nblk) == c_ref[0]
        gv = jnp.where(mine, go_ref[...], gs_ref[...])
        nm = ADAM_B1 * m_ref[...] + (1.0 - ADAM_B1) * gv
        nv = ADAM_B2 * v_ref[...] + (1.0 - ADAM_B2) * (gv * gv)
        g_ref[...] = gv
        d_ref[...] = -ADAM_LR * ((nm / c1) / (jnp.sqrt(nv / c2) + ADAM_EPS) + ADAM_WD * w_ref[...])
        nm_ref[...] = nm
        nv_ref[...] = nv

    spec = pl.BlockSpec(lead + (tr, cols), lambda i, c: zeros + (i, 0))
    hspec = pl.BlockSpec((tr, cols), lambda i, c: (i % nblk, 0))
    shape = jax.ShapeDtypeStruct(w.shape, F32)
    grid_spec = pltpu.PrefetchScalarGridSpec(num_scalar_prefetch=1, grid=(R // tr,),
                                             in_specs=[spec, hspec, hspec, spec, spec], out_specs=[spec] * 4)
    return _pallas(
        body, out_shape=[shape] * 4, grid_spec=grid_spec,
        name=name, compiler_params=_params((R // tr,)))(cidx, w, g_own, g_other, m, v)


PARAMS = (("meta", 1), ("norm1", None), ("w_in", 2), ("gdn_conv_w", 2), ("gdn_a_log", None), ("gdn_dt_bias", None),
          ("gdn_norm", None), ("w_out", 1), ("norm2", None), ("w_ffn_up", 2), ("ffn_conv_w", 2), ("ffn_conv_b", None),
          ("w_ffn_down", 1), ("norm_f", None))
BIG = ("w_in", "w_out", "w_ffn_up", "w_ffn_down")
PACK_ALIGN = 1024
PACK_ROWS_ALIGN = 32


def _pack(arrs, dtype):
    parts, total = [], 0
    for a in arrs:
        f = a.reshape(-1).astype(dtype)
        pad = (-f.shape[0]) % PACK_ALIGN
        parts.append(jnp.pad(f, (0, pad)) if pad else f)
        total += f.shape[0] + pad
    rows = total // LANES
    rpad = (-rows) % PACK_ROWS_ALIGN
    if rpad:
        parts.append(jnp.zeros((rpad * LANES,), dtype))
    return jnp.concatenate(parts).reshape(rows + rpad, LANES)


def _unpack(buf, shapes):
    flat = buf.reshape(-1)
    outs, off = [], 0
    for s in shapes:
        n = int(np.prod(s))
        outs.append(flat[off:off + n].reshape(s))
        off += n + (-n) % PACK_ALIGN
    return outs


def _split4(a, axis):
    n = a.shape[axis] // N_CHIPS
    return [lax.slice_in_dim(a, s * n, (s + 1) * n, axis=axis) for s in range(N_CHIPS)]


PROJ_ORDER = (3, 7, 8, 9, 0, 1, 2, 6, 4, 5)


def _reorder_w_in(w, cfg):
    d, hg = cfg.d, cfg.hg

    def block(k):
        off = k * d + (2 * hg if k >= 4 else 0)
        return w[:, off:off + d]

    tail = jnp.pad(w[:, 4 * d:4 * d + 2 * hg], ((0, 0), (0, LANES - 2 * hg)))
    return jnp.concatenate([block(k) for k in PROJ_ORDER] + [tail], axis=1)


def _restore_w_in(wr, cfg):
    d, hg = cfg.d, cfg.hg
    at = {k: i for i, k in enumerate(PROJ_ORDER)}
    block = lambda k: wr[:, at[k] * d:(at[k] + 1) * d]
    return jnp.concatenate([block(k) for k in range(4)] + [wr[:, 10 * d:10 * d + 2 * hg]] +
                           [block(k) for k in range(4, 10)], axis=1)


def _step(cfg, x, tgt, shard, m_shard, v_shard):
    d, hg, hr, dff, rp, tr, tm = cfg.d, cfg.hg, cfg.hr, cfg.dff, cfg.rp, cfg.tr, cfg.tm
    nrow = rp // tr
    assert cfg.tf * N_CHIPS == 2 * dff and cfg.din % N_CHIPS == 0
    cidx = lax.axis_index("c").astype(jnp.int32).reshape(1)
    chip = (2 * lax.axis_index("x") + lax.axis_index("y")).astype(jnp.int32).reshape(1)

    axis = dict(PARAMS)
    small = ("meta", "gdn_conv_w", "ffn_conv_w")
    small_shapes = [shard[n].shape for n in small]
    mine = [shard[n][0].astype(BF16) for n in BIG] + [_pack([shard[n] for n in small], F32)]

    def with_own(gathered, own):
        return [lax.dynamic_update_slice(g, w[None], (chip[0], 0, 0)) for g, w in zip(gathered, own)]

    g_in, g_small = with_own(_run_comm("weights_gather_first", _gather_comm([mine[0], mine[4]])), [mine[0], mine[4]])
    w_in_r = _reorder_w_in(jnp.concatenate([g_in[s] for s in range(N_CHIPS)], axis=1), cfg)
    per_chip = [_unpack(g_small[s], small_shapes) for s in range(N_CHIPS)]
    full = {n: jnp.concatenate([per_chip[s][k] for s in range(N_CHIPS)], axis=axis[n]) for k, n in enumerate(small)}
    meta = full["meta"]
    gconv_w = full["gdn_conv_w"][0]
    fconv_w = full["ffn_conv_w"][0]
    norm1, norm2, gnorm = shard["norm1"], shard["norm2"], shard["gdn_norm"]
    normf = shard["norm_f"].reshape(1, d)
    fconv_b = shard["ffn_conv_b"]
    alog = jnp.pad(shard["gdn_a_log"], ((0, 7), (0, LANES - hg)))
    dtb = jnp.pad(shard["gdn_dt_bias"], ((0, 7), (0, LANES - hg)))

    h0 = jnp.concatenate([jnp.zeros((cfg.front, d), F32), meta, x], axis=0)
    pos = jnp.arange(rp, dtype=F32) - float(cfg.front)
    half = RET_DK // 2
    inv = 1.0 / (ROPE_BASE ** (jnp.arange(half, dtype=F32) / half))
    ang = pos[:, None] * inv[None, :]
    cos, sin = jnp.cos(ang), jnp.sin(ang)
    rconsts = _ret_consts(cfg)

    rms_f = _make_rms_fn(cfg, tr, False)
    rms_b = _make_rms_fn(cfg, tr, True)
    rowshape = jax.ShapeDtypeStruct((rp, d), F32)
    rspec = _rows(tr, d)

    def rms_fwd(name, h, g):
        return _stage_fwd(name, rms_f, (nrow,), [In(h, rspec), In(g, _full(g))],
                          [jax.ShapeDtypeStruct((rp, d), BF16)], [rspec])[0]

    tn_in = cfg.pw // 9 if cfg.pw % (9 * LANES) == 0 else LANES
    hn1 = rms_fwd("rms1_fwd", h0, norm1)
    proj, rest = _mm("proj_fwd", hn1, w_in_r, tm=tm, tn=tn_in, tk=d, comm=_gather_comm(mine[1:4]))
    g_out, g_up, g_down = with_own(rest, mine[1:4])
    w_out = g_out.reshape(d, d)
    w_up = g_up
    w_up_t = jnp.swapaxes(g_up, 1, 2).reshape(2 * dff, d)
    w_down = g_down.reshape(dff, d)
    cqkv = _conv_fwd("gdn_conv_fwd", proj, CONV_COL * d, gconv_w, None, taps=GDN_CONV, width=3 * d, tr=tr, tc=d)
    tail_spec = _rows(tr, LANES, TAIL_COL * d // LANES)
    prep_fn = _make_gdn_prep_fn(cfg, tr)

    def prep_ins(dproj=None):
        return [In(cqkv, _rows(tr, 3 * d), F32), In(proj, tail_spec, BF16, galias=dproj, gshape=(rp, cfg.pw)),
                In(alog, _full(alog), F32, True), In(dtb, _full(dtb), F32, True)]

    qn, kn, vv, bB, lB = _stage_fwd("gdn_prep_fwd", prep_fn, (nrow,), prep_ins(), [rowshape] * 5, [rspec] * 5)

    trg = cfg.nb * CHUNK
    gi_grid = (rp // trg, hg)
    hspec = pl.BlockSpec((trg, GDN_DK), lambda i, h: (i, h))
    aspec = pl.BlockSpec((1, trg, CHUNK), lambda i, h: (h, i, 0))
    gspec = pl.BlockSpec((1, cfg.nb, 1, GDN_DK), lambda i, h: (h, i, 0, 0))
    intra_ins = [In(t, hspec, F32) for t in (qn, kn, vv, bB, lB)]
    ashape = jax.ShapeDtypeStruct((hg, rp, CHUNK), F32)
    intra_shapes = [rowshape, rowshape, ashape, rowshape, rowshape, jax.ShapeDtypeStruct((hg, cfg.nch, 1, GDN_DK), F32), ashape]
    intra_specs = [hspec, hspec, aspec, hspec, hspec, gspec, aspec]
    gu, gw, gattn, gqd, gkd, ggl, gtinv = _stage_fwd("gdn_intra_fwd", _gdn_intra_fn, gi_grid, intra_ins, intra_shapes,
                                                     intra_specs)
    oa, gss = _gdn_scan_fwd(cfg, gu, gw, gattn, gqd, gkd, ggl)

    rot_fn = _make_rot_fn(cfg)

    def rot_ins(dproj=None):
        return [In(proj, _rows(tr, 2 * d, ROT_COL // 2), BF16, galias=dproj, gshape=(rp, cfg.pw)),
                In(cos, _rows(tr, half)), In(sin, _rows(tr, half))]

    qr, kr = _stage_fwd("rot_fwd", rot_fn, (nrow,), rot_ins(), [rowshape] * 2, [rspec] * 2)
    ob, rss = _ret_scan_fwd(cfg, qr, kr, proj, rconsts)

    mix_fn = _make_mix_fn(cfg)
    mix_ins = [In(oa, rspec, F32), In(ob, rspec, F32), In(proj, _rows(tr, 4 * d, MIX_COL // 4), BF16, gshape=(rp, cfg.pw)),
               In(gnorm, _full(gnorm), F32, True)]
    ymix = _stage_fwd("mix_fwd", mix_fn, (nrow,), mix_ins, [jax.ShapeDtypeStruct((rp, d), BF16)], [rspec])[0]
    h1 = _mm("out_proj_fwd", ymix, w_out, tm=tm, tn=d, tk=d, add=h0)

    hn2 = rms_fwd("rms2_fwd", h1, norm2)
    up = _mm("ffn_up_fwd", hn2, w_up, tm=tm, tn=cfg.tf, tk=d)
    uc = _conv_fwd("ffn_conv_fwd", up, 0, fconv_w, fconv_b, taps=FFN_CONV, width=2 * dff, tr=tr, tc=cfg.tf)
    tra = tr // 2
    act_ins = [In(uc, _rows(tra, 2 * dff), F32)]
    act_spec = _rows(tra, dff)
    act = _stage_fwd("ffn_act_fwd", _act_fn, (rp // tra,), act_ins, [jax.ShapeDtypeStruct((rp, dff), BF16)], [act_spec])[0]
    h2 = _mm("ffn_down_fwd", act, w_down, tm=tm, tn=d, tk=cfg.tf, add=h1)

    dh2, g_normf, loss_blk = _final(cfg, h2, normf, tgt)
    loss = lax.psum(loss_blk[0, 0], ("x", "y", "c"))

    g_w_down = _mm_tn("ffn_down_dw", act, dh2, tr=tm, tka=cfg.tf, tn=d)
    dact = _mm("ffn_down_dx", dh2, w_down.T, tm=tm, tn=cfg.tf, tk=d)
    duc, = _stage_bwd("ffn_act_bwd", _act_fn, (rp // tra,), act_ins, [(dact, act_spec)])
    dup, g_fconv_w, g_fconv_b = _conv_bwd("ffn_conv_bwd", up, 0, fconv_w, duc, taps=FFN_CONV, width=2 * dff,
                                          tr=tr, tc=cfg.tf, with_bias=True)
    g_w_up = _mm_tn("ffn_up_dw", hn2, dup, tr=tm, tka=d, tn=cfg.tf, blocked=True)

    def pair_reduce(tag, names, arrs):
        recvs = _pair_exchange("grad_pair_exchange_" + tag, arrs)
        return [_pair_sum("grad_pair_sum_" + n, g, r, cidx) for n, g, r in zip(names, arrs, recvs)]

    parts_ffn = pair_reduce("ffn", ["w_ffn_down", "w_ffn_up"], [g_w_down.reshape(N_CHIPS, dff // N_CHIPS, d), g_w_up])
    dhn2, slots_ffn = _mm("ffn_up_dx", dup, w_up_t, tm=tm, tn=d, tk=cfg.tf, comm=_exchange_comm(parts_ffn))

    def rms_bwd(name, h, g, dhn, dres):
        ins = [In(h, rspec, F32), In(g, _full(g), F32, True)]
        return _stage_bwd(name, rms_b, (nrow,), ins, [(dhn, rspec), (dres, rspec)])

    dh1, g_norm2 = rms_bwd("rms2_bwd", h1, norm2, dhn2, dh2)
    g_w_out = _mm_tn("out_proj_dw", ymix, dh1, tr=tm, tka=d, tn=d)
    dymix = _mm("out_proj_dx", dh1, w_out.T, tm=tm, tn=d, tk=d)
    doa, dob, dproj, g_gnorm = _stage_bwd("mix_bwd", mix_fn, (nrow,), mix_ins, [(dymix, rspec)])

    dqr, dkr, dproj = _ret_scan_bwd(cfg, dob, qr, kr, proj, rconsts, rss, dproj)
    dproj, = _stage_bwd("rot_bwd", rot_fn, (nrow,), rot_ins(dproj), [(dqr, rspec), (dkr, rspec)])

    dgu, dgw, dgattn, dgqd, dgkd, dggl = _gdn_scan_bwd(cfg, doa, gu, gw, gattn, gqd, gkd, ggl, gss)
    intra_cots = [(dgu, hspec), (dgw, hspec), (dgattn, aspec), (dgqd, hspec), (dgkd, hspec), (dggl, gspec)]
    dqn, dkn, dvv, dbB, dlB = _stage_bwd("gdn_intra_bwd", _gdn_intra_fn, gi_grid, intra_ins + [In(gtinv, aspec)], intra_cots)
    dcqkv, dproj, g_alog, g_dtb = _stage_bwd(
        "gdn_prep_bwd", prep_fn, (nrow,), prep_ins(dproj), [(t, rspec) for t in (dqn, dkn, dvv, dbB, dlB)])
    dproj, g_gconv_w = _conv_bwd("gdn_conv_bwd", proj, CONV_COL * d, gconv_w, dcqkv, taps=GDN_CONV, width=3 * d,
                                 tr=tr, tc=d, with_bias=False, dx_into=dproj)
    g_w_in_r = _mm_tn("proj_dw", hn1, dproj, tr=tm, tka=d, tn=tn_in)
    g_in4 = jnp.stack(_split4(_restore_w_in(g_w_in_r, cfg), 1))
    parts_mix = pair_reduce("mix", ["w_out", "w_in"], [g_w_out.reshape(N_CHIPS, d // N_CHIPS, d), g_in4])
    dhn1, slots_mix = _mm("proj_dx", dproj, w_in_r.T, tm=tm, tn=d, tk=tn_in, comm=_exchange_comm(parts_mix))
    dh0, g_norm1 = rms_bwd("rms1_bwd", h0, norm1, dhn1, dh1)

    grad_x = dh0[cfg.xrow:]
    small_grads = {
        "meta": dh0[cfg.front:cfg.xrow], "norm1": g_norm1, "gdn_conv_w": g_gconv_w[None],
        "gdn_a_log": g_alog[0:1, :hg], "gdn_dt_bias": g_dtb[0:1, :hg], "gdn_norm": g_gnorm, "norm2": g_norm2,
        "ffn_conv_w": g_fconv_w[None], "ffn_conv_b": g_fconv_b, "norm_f": g_normf.reshape(d),
    }

    small_names = [n for n, _ in PARAMS if n not in BIG]
    g_small = jnp.stack([_pack([small_grads[n] if axis[n] is None else _split4(small_grads[n], axis[n])[s]
                                for n in small_names], F32) for s in range(N_CHIPS)])
    parts_small = pair_reduce("small", ["small"], [g_small])
    slots_small = _run_comm("grad_exchange_small", _exchange_comm(parts_small))
    tags = ["w_in", "w_out", "w_ffn_up", "w_ffn_down", "small"]
    parts = [parts_mix[1], parts_mix[0], parts_ffn[1], parts_ffn[0], parts_small[0]]
    slots = [slots_mix[1], slots_mix[0], slots_ffn[1], slots_ffn[0], slots_small[0]]
    fins = [_chip_sum("grad_chip_sum_" + t, p, s, chip) for t, p, s in zip(tags, parts, slots)]
    sibs = _pair_swap(fins)

    def flat2(a):
        return a.reshape(-1, a.shape[-1])

    outs = {}
    for k, t in enumerate(BIG):
        res = _adamw("adamw_" + t, flat2(shard[t]), fins[k], sibs[k], flat2(m_shard[t]), flat2(v_shard[t]), cidx)
        outs[t] = [r.reshape(shard[t].shape) for r in res]
    small_shapes_all = [shard[n].shape for n in small_names]
    pk = lambda src: _pack([src[n] for n in small_names], F32)
    res = _adamw("adamw_small", pk(shard), fins[4], sibs[4], pk(m_shard), pk(v_shard), cidx)
    for k, r in enumerate(res):
        for n, a in zip(small_names, _unpack(r, small_shapes_all)):
            outs.setdefault(n, [None] * 4)[k] = a
    names = [n for n, _ in PARAMS]
    return (loss, grad_x[None], *[outs[n][k] for k in range(4) for n in names])


def kernel(x, meta, norm1, w_in, gdn_conv_w, gdn_a_log, gdn_dt_bias, gdn_norm, w_out, norm2, w_ffn_up, ffn_conv_w, ffn_conv_b, w_ffn_down, norm_f, loss_target, m_meta, m_norm1, m_w_in, m_gdn_conv_w, m_gdn_a_log, m_gdn_dt_bias, m_gdn_norm, m_w_out, m_norm2, m_w_ffn_up, m_ffn_conv_w, m_ffn_conv_b, m_w_ffn_down, m_norm_f, v_meta, v_norm1, v_w_in, v_gdn_conv_w, v_gdn_a_log, v_gdn_dt_bias, v_gdn_norm, v_w_out, v_norm2, v_w_ffn_up, v_ffn_conv_w, v_ffn_conv_b, v_w_ffn_down, v_norm_f):
    names = [n for n, _ in PARAMS]
    shard = dict(zip(names, (meta, norm1, w_in, gdn_conv_w, gdn_a_log, gdn_dt_bias, gdn_norm, w_out, norm2, w_ffn_up,
                             ffn_conv_w, ffn_conv_b, w_ffn_down, norm_f)))
    m_shard = dict(zip(names, (m_meta, m_norm1, m_w_in, m_gdn_conv_w, m_gdn_a_log, m_gdn_dt_bias, m_gdn_norm, m_w_out,
                               m_norm2, m_w_ffn_up, m_ffn_conv_w, m_ffn_conv_b, m_w_ffn_down, m_norm_f)))
    v_shard = dict(zip(names, (v_meta, v_norm1, v_w_in, v_gdn_conv_w, v_gdn_a_log, v_gdn_dt_bias, v_gdn_norm, v_w_out,
                               v_norm2, v_w_ffn_up, v_ffn_conv_w, v_ffn_conv_b, v_w_ffn_down, v_norm_f)))
    return _step(REAL, x[0], loss_target[0], shard, m_shard, v_shard)
```

```python
import functools
from typing import NamedTuple

import numpy as np
import jax
import jax.numpy as jnp
from jax import lax
from jax.experimental import pallas as pl
from jax.experimental.pallas import tpu as pltpu

F32 = jnp.float32
BF16 = jnp.bfloat16
EPS = 1e-6
CHUNK = 64
GDN_DK = 128
RET_DK = 256
GDN_CONV = 4
FFN_CONV = 3
ROPE_BASE = 10000.0
LANES = 128
N_CHIPS = 4
ADAM_LR, ADAM_B1, ADAM_B2, ADAM_EPS, ADAM_WD, ADAM_STEP = 0.001, 0.9, 0.999, 1e-08, 0.01, 10
MIX_COL, CONV_COL, RV_BLOCK, ROT_COL, TAIL_COL = 0, 4, 7, 8, 10
MESH = pl.DeviceIdType.MESH
VMEM_LIMIT = 56 * 1024 * 1024


class Cfg(NamedTuple):
    d: int
    seq: int
    n_meta: int
    dff: int
    tr: int
    nb: int
    tm: int
    tf: int
    sc: int

    @property
    def hg(self): return self.d // GDN_DK
    @property
    def hr(self): return self.d // RET_DK
    @property
    def L(self): return self.n_meta + self.seq
    @property
    def rp(self): return -(-self.L // 256) * 256
    @property
    def front(self): return self.rp - self.L
    @property
    def xrow(self): return self.rp - self.seq
    @property
    def nch(self): return self.rp // CHUNK
    @property
    def pw(self): return 10 * self.d + LANES
    @property
    def din(self): return 10 * self.d + 2 * self.hg


REAL = Cfg(d=1024, seq=8192, n_meta=16, dff=2816, tr=256, nb=12, tm=1408, tf=1408, sc=4)


def _pallas(body, **kw):
    return pl.pallas_call(body, **kw)


def _sigmoid(x):
    return 1.0 / (1.0 + jnp.exp(-x))


def _silu(x):
    return x * _sigmoid(x)


def _softplus(x):
    return jnp.maximum(x, 0.0) + jnp.log(1.0 + jnp.exp(-jnp.abs(x)))


def _raw_dot(a, b, ta, tb, hi):
    if not hi:
        a = a.astype(BF16)
        b = b.astype(BF16)
    nbatch = a.ndim - 2
    ca = a.ndim - 2 if ta else a.ndim - 1
    cb = b.ndim - 1 if tb else b.ndim - 2
    batch = tuple(range(nbatch))
    return lax.dot_general(a, b, (((ca,), (cb,)), (batch, batch)),
                           precision=lax.Precision.HIGHEST if hi else None,
                           preferred_element_type=F32)


@functools.partial(jax.custom_vjp, nondiff_argnums=(2, 3, 4))
def _dot_p(a, b, ta, tb, hi):
    return _raw_dot(a, b, ta, tb, hi)


def _dot(a, b, ta=False, tb=False, hi=False):
    return _dot_p(a, b, ta, tb, hi)


def _dot_fwd(a, b, ta, tb, hi):
    return _raw_dot(a, b, ta, tb, hi), (a, b)


def _dot_bwd(ta, tb, hi, res, g):
    a, b = res
    if not ta and not tb:
        da, db = _dot(g, b, False, True, hi), _dot(a, g, True, False, hi)
    elif not ta and tb:
        da, db = _dot(g, b, False, False, hi), _dot(g, a, True, False, hi)
    elif ta and not tb:
        da, db = _dot(b, g, False, True, hi), _dot(a, g, False, False, hi)
    else:
        raise NotImplementedError
    return da.astype(a.dtype), db.astype(b.dtype)


_dot_p.defvjp(_dot_fwd, _dot_bwd)


def _iota2(n, m, axis):
    return lax.broadcasted_iota(jnp.int32, (n, m), axis)


def _bcast(mat, nb):
    return jnp.broadcast_to(mat[None], (nb,) + mat.shape)


def _split3(a):
    a0 = a.astype(BF16)
    r1 = a - a0.astype(F32)
    a1 = r1.astype(BF16)
    return a0, a1, (r1 - a1.astype(F32)).astype(BF16)


@functools.partial(jax.custom_vjp, nondiff_argnums=(2,))
def _dot_sel(a, e, te):
    eb = e.astype(BF16)
    p0, p1, p2 = (_raw_dot(p, eb, False, te, False) for p in _split3(a))
    return p0 + (p1 + p2)


def _dot_sel_fwd(a, e, te):
    return _dot_sel(a, e, te), e


def _dot_sel_bwd(te, e, g):
    return _dot_sel(g, e, not te), jnp.zeros_like(e)


_dot_sel.defvjp(_dot_sel_fwd, _dot_sel_bwd)


@jax.custom_vjp
def _sel_dot(e, x):
    eb = e.astype(BF16)
    p0, p1, p2 = (_raw_dot(eb, p, False, False, False) for p in _split3(x))
    return p0 + (p1 + p2)


def _sel_dot_fwd(e, x):
    return _sel_dot(e, x), e


def _sel_dot_bwd(e, g):
    eb = e.astype(BF16)
    p0, p1, p2 = (_raw_dot(eb, p, True, False, False) for p in _split3(g))
    return jnp.zeros_like(e), p0 + (p1 + p2)


_sel_dot.defvjp(_sel_dot_fwd, _sel_dot_bwd)


def _tri_inv_raw(m):
    nb = m.shape[0]
    r, c = _iota2(CHUNK, CHUNK, 0), _iota2(CHUNK, CHUNK, 1)
    t = _bcast((r == c).astype(F32), nb)
    b = 1
    while b < CHUNK:
        sh = b.bit_length() - 1
        off = ((r >> (sh + 1)) == (c >> (sh + 1))) & ((r >> sh) != (c >> sh)) & (r > c)
        cl = jnp.where(off[None], m, 0.0)
        t = t - _raw_dot(_raw_dot(t, cl, False, False, False), t, False, False, False)
        b *= 2
    return t


@jax.custom_vjp
def _tri_inv_given(m, t):
    return t


def _tri_inv_fwd(m, t):
    return t, t


def _tri_inv_bwd(t, g):
    return -_raw_dot(_raw_dot(t, g, True, False, False), t, False, True, False), jnp.zeros_like(t)


_tri_inv_given.defvjp(_tri_inv_fwd, _tri_inv_bwd)


def _rms(h, g):
    return h * lax.rsqrt(jnp.mean(h * h, axis=-1, keepdims=True) + EPS) * g


class In(NamedTuple):
    arr: jax.Array
    spec: pl.BlockSpec
    grad: object = None
    acc: bool = False
    gshape: object = None
    gspec: object = None
    galias: object = None


def _params(grid):
    sem = ("arbitrary",) * len(grid)
    return pltpu.CompilerParams(dimension_semantics=sem, vmem_limit_bytes=VMEM_LIMIT)


def _stage_fwd(name, fn, grid, ins, out_shapes, out_specs):
    n_in = len(ins)

    def body(*refs):
        pids = tuple(pl.program_id(k) for k in range(len(grid)))
        vals = [r[...] for r in refs[:n_in]]
        outs = fn(pids, *vals)
        for o_ref, o in zip(refs[n_in:], outs):
            o_ref[...] = o.reshape(o_ref.shape).astype(o_ref.dtype)

    return _pallas(
        body, out_shape=out_shapes, grid=grid, in_specs=[i.spec for i in ins],
        out_specs=out_specs, name=name, compiler_params=_params(grid))(*[i.arr for i in ins])


def _stage_bwd(name, fn, grid, ins, cots):
    n_in, n_ct = len(ins), len(cots)
    didx = [k for k, i in enumerate(ins) if i.grad is not None]
    aliased = [(o, ins[k].galias) for o, k in enumerate(didx) if ins[k].galias is not None]
    n_al = len(aliased)

    def body(*refs):
        pids = tuple(pl.program_id(k) for k in range(len(grid)))
        vals = [r[...] for r in refs[:n_in]]
        ct_refs = refs[n_in:n_in + n_ct]
        g_refs = refs[n_in + n_ct + n_al:]

        def f(*dv):
            merged = list(vals)
            for k, v in zip(didx, dv):
                merged[k] = v
            return tuple(fn(pids, *merged))

        outs, vjp_fn = jax.vjp(f, *[vals[k].astype(F32) for k in didx])
        cts = tuple(c[...].reshape(o.shape).astype(F32) for c, o in zip(ct_refs, outs))
        grads = vjp_fn(cts)
        first = functools.reduce(jnp.logical_and, [p == 0 for p in pids])
        for k, g_ref, g in zip(didx, g_refs, grads):
            if ins[k].acc:
                @pl.when(first)
                def _(g_ref=g_ref):
                    g_ref[...] = jnp.zeros(g_ref.shape, g_ref.dtype)
                g_ref[...] += g.reshape(g_ref.shape).astype(g_ref.dtype)
            else:
                g_ref[...] = g.reshape(g_ref.shape).astype(g_ref.dtype)

    out_shapes = [jax.ShapeDtypeStruct(ins[k].gshape or ins[k].arr.shape, ins[k].grad) for k in didx]
    out_specs = [ins[k].gspec or ins[k].spec for k in didx]
    return _pallas(
        body, out_shape=out_shapes, grid=grid,
        in_specs=[i.spec for i in ins] + [c[1] for c in cots] + [ANY] * n_al, out_specs=out_specs,
        input_output_aliases={n_in + n_ct + a: o for a, (o, _) in enumerate(aliased)},
        name=name, compiler_params=_params(grid))(*[i.arr for i in ins], *[c[0] for c in cots], *[a for _, a in aliased])


def _full(arr):
    nd = arr.ndim
    return pl.BlockSpec(arr.shape, lambda *p: (0,) * nd)


def _rows(tr, width, blk=0):
    return pl.BlockSpec((tr, width), lambda i: (i, blk))


def _mm(name, a, b, *, tm, tn, tk, out_dtype=F32, add=None, comm=None):
    M, K = a.shape
    N = b.shape[1] if b.ndim == 2 else b.shape[0] * b.shape[2]
    nk = K // tk
    grid = (M // tm, N // tn, nk)
    n_in = 3 if add is not None else 2
    n_ci, n_co = (len(comm.ins), len(comm.outs)) if comm is not None else (0, 0)

    def body(*refs):
        a_ref, b_ref = refs[0], refs[1]
        add_ref = refs[2] if add is not None else None
        c_ins = refs[n_in:n_in + n_ci]
        o_ref = refs[n_in + n_ci]
        c_outs = refs[n_in + n_ci + 1:n_in + n_ci + 1 + n_co]
        scratch = refs[n_in + n_ci + 1 + n_co:]
        acc_ref = scratch[0] if nk > 1 else None
        sems = scratch[1 if nk > 1 else 0:]
        step = (pl.program_id(0) * grid[1] + pl.program_id(1)) * nk + pl.program_id(2)
        if comm is not None:
            @pl.when(step == 0)
            def _():
                comm.start(c_ins, c_outs, sems)

        part = _raw_dot(a_ref[...], b_ref[...], False, False, False)

        def finish(total):
            if add_ref is not None:
                total = total + add_ref[...]
            o_ref[...] = total.astype(o_ref.dtype)

        if nk == 1:
            finish(part)
        else:
            k = pl.program_id(2)

            @pl.when(k == 0)
            def _():
                acc_ref[...] = part

            @pl.when(k > 0)
            def _():
                acc_ref[...] += part

            @pl.when(k == nk - 1)
            def _():
                finish(acc_ref[...])

        if comm is not None:
            @pl.when(step == grid[0] * grid[1] * nk - 1)
            def _():
                comm.finish(c_ins, c_outs, sems)

    b_spec = (pl.BlockSpec((tk, tn), lambda i, j, k: (k, j)) if b.ndim == 2 else
              pl.BlockSpec((None, tk, tn), lambda i, j, k: (j, k, 0)))
    in_specs = [pl.BlockSpec((tm, tk), lambda i, j, k: (i, k)), b_spec]
    args = [a, b]
    if add is not None:
        in_specs.append(pl.BlockSpec((tm, tn), lambda i, j, k: (i, j)))
        args.append(add)
    out_shape = jax.ShapeDtypeStruct((M, N), out_dtype)
    out_spec = pl.BlockSpec((tm, tn), lambda i, j, k: (i, j))
    scratch = [pltpu.VMEM((tm, tn), F32)] if nk > 1 else []
    if comm is None:
        return _pallas(body, out_shape=out_shape, grid=grid, in_specs=in_specs, out_specs=out_spec,
                       scratch_shapes=scratch, name=name, compiler_params=_params(grid))(*args)
    res = _pallas(body, out_shape=[out_shape] + comm.outs, grid=grid, in_specs=in_specs + [ANY] * n_ci,
                  out_specs=[out_spec] + [ANY] * n_co, scratch_shapes=scratch + comm.sems, name=name,
                  compiler_params=_params(grid))(*args, *comm.ins)
    return res[0], res[1:]


def _mm_tn(name, a, b, *, tr, tka, tn, blocked=False):
    R, Ka = a.shape
    N = b.shape[1]
    nr = R // tr
    grid = (Ka // tka, N // tn, nr)
    if blocked:
        out_shape = jax.ShapeDtypeStruct((N // tn, Ka, tn), F32)
        out_spec = pl.BlockSpec((None, tka, tn), lambda i, j, r: (j, i, 0))
    else:
        out_shape = jax.ShapeDtypeStruct((Ka, N), F32)
        out_spec = pl.BlockSpec((tka, tn), lambda i, j, r: (i, j))

    def body(a_ref, b_ref, o_ref):
        r = pl.program_id(2)
        part = _raw_dot(a_ref[...], b_ref[...], True, False, False)

        @pl.when(r == 0)
        def _():
            o_ref[...] = part

        @pl.when(r > 0)
        def _():
            o_ref[...] += part

    return _pallas(
        body, out_shape=out_shape, grid=grid,
        in_specs=[pl.BlockSpec((tr, tka), lambda i, j, r: (r, i)),
                  pl.BlockSpec((tr, tn), lambda i, j, r: (r, j))],
        out_specs=out_spec, name=name, compiler_params=_params(grid))(a, b)


def _conv_fwd(name, x, xcol0, w, b, *, taps, width, tr, tc):
    R = x.shape[0]
    grid = (width // tc, R // tr)
    cb0 = xcol0 // tc
    hb = tr // 8

    def body(*refs):
        x_ref, xp_ref, w_ref = refs[:3]
        b_ref = refs[3] if b is not None else None
        o_ref = refs[-1]
        i = pl.program_id(1)
        xv = x_ref[...]
        prev = jnp.where(i > 0, xp_ref[...], 0.0)
        ext = jnp.concatenate([prev, xv], axis=0)
        acc = xv * w_ref[taps - 1:taps, :]
        for s in range(1, taps):
            acc = acc + pltpu.roll(ext, s, 0)[8:, :] * w_ref[taps - 1 - s:taps - s, :]
        if b_ref is not None:
            acc = acc + b_ref[...]
        o_ref[...] = acc

    in_specs = [pl.BlockSpec((tr, tc), lambda j, i: (i, cb0 + j)),
                pl.BlockSpec((8, tc), lambda j, i: (jnp.maximum(i * hb - 1, 0), cb0 + j)),
                pl.BlockSpec((taps, tc), lambda j, i: (0, j))]
    args = [x, x, w]
    if b is not None:
        in_specs.append(pl.BlockSpec((1, tc), lambda j, i: (0, j)))
        args.append(b)
    return _pallas(
        body, out_shape=jax.ShapeDtypeStruct((R, width), F32), grid=grid, in_specs=in_specs,
        out_specs=pl.BlockSpec((tr, tc), lambda j, i: (i, j)),
        name=name, compiler_params=_params(grid))(*args)


def _conv_bwd(name, x, xcol0, w, dy, *, taps, width, tr, tc, with_bias, dx_into=None):
    R = x.shape[0]
    nr = R // tr
    grid = (width // tc, nr)
    cb0 = xcol0 // tc
    hrows = 16 if dy.dtype == BF16 else 8
    hb = tr // hrows
    n_ext = tr + 8
    n_al = 0 if dx_into is None else 1

    def body(*refs):
        x_ref, w_ref, dy_ref, dyn_ref = refs[:4]
        dx_ref, dw_ref = refs[4 + n_al], refs[5 + n_al]
        db_ref = refs[6 + n_al] if with_bias else None
        i = pl.program_id(1)
        xv = x_ref[...]
        dyv = dy_ref[...].astype(F32)
        nxt = dyn_ref[...].astype(F32)[:8, :]
        dext = jnp.concatenate([dyv, jnp.where(i < nr - 1, nxt, 0.0)], axis=0)
        dx = dyv * w_ref[taps - 1:taps, :]
        dws = [None] * taps
        dws[taps - 1] = jnp.sum(xv * dyv, axis=0, keepdims=True)
        for s in range(1, taps):
            ahead = pltpu.roll(dext, n_ext - s, 0)[:tr, :]
            dx = dx + ahead * w_ref[taps - 1 - s:taps - s, :]
            dws[taps - 1 - s] = jnp.sum(xv * ahead, axis=0, keepdims=True)
        dx_ref[...] = dx.astype(dx_ref.dtype)

        @pl.when(i == 0)
        def _():
            for k in range(taps):
                dw_ref[k:k + 1, :] = dws[k]
            if db_ref is not None:
                db_ref[...] = jnp.sum(dyv, axis=0, keepdims=True)

        @pl.when(i > 0)
        def _():
            for k in range(taps):
                dw_ref[k:k + 1, :] += dws[k]
            if db_ref is not None:
                db_ref[...] += jnp.sum(dyv, axis=0, keepdims=True)

    in_specs = [pl.BlockSpec((tr, tc), lambda j, i: (i, cb0 + j)),
                pl.BlockSpec((taps, tc), lambda j, i: (0, j)),
                pl.BlockSpec((tr, tc), lambda j, i: (i, j)),
                pl.BlockSpec((hrows, tc), lambda j, i: (jnp.minimum((i + 1) * hb, R // hrows - 1), j))]
    args = [x, w, dy, dy]
    if dx_into is None:
        dx_shape, dx_spec, aliases = jax.ShapeDtypeStruct((R, width), BF16), pl.BlockSpec((tr, tc), lambda j, i: (i, j)), {}
    else:
        dx_shape = jax.ShapeDtypeStruct(dx_into.shape, dx_into.dtype)
        dx_spec, aliases = pl.BlockSpec((tr, tc), lambda j, i: (i, cb0 + j)), {4: 0}
        in_specs.append(ANY)
        args.append(dx_into)
    out_shape = [dx_shape, jax.ShapeDtypeStruct((taps, width), F32)]
    out_specs = [dx_spec, pl.BlockSpec((taps, tc), lambda j, i: (0, j))]
    if with_bias:
        out_shape.append(jax.ShapeDtypeStruct((1, width), F32))
        out_specs.append(pl.BlockSpec((1, tc), lambda j, i: (0, j)))
    return _pallas(
        body, out_shape=out_shape, grid=grid, in_specs=in_specs, out_specs=out_specs, input_output_aliases=aliases,
        name=name, compiler_params=_params(grid))(*args)


def _row_mask(cfg, i, tr):
    rows = i * tr + lax.broadcasted_iota(jnp.int32, (tr, 1), 0)
    return (rows >= cfg.front).astype(F32)


def _make_rms_fn(cfg, tr, with_residual):
    def fn(pids, h, g):
        hm = h * _row_mask(cfg, pids[0], tr)
        if with_residual:
            return _rms(hm, g), hm
        return (_rms(hm, g),)
    return fn


def _make_gdn_prep_fn(cfg, tr):
    d, hg = cfg.d, cfg.hg

    def fn(pids, c, tail, alog, dtb):
        cq, ck, cv = c[:, :d], c[:, d:2 * d], c[:, 2 * d:]
        mask = _row_mask(cfg, pids[0], tr)
        j, col = _iota2(LANES, d, 0), _iota2(LANES, d, 1)
        ea = ((col >> 7) == j).astype(F32)
        eb = ((col >> 7) + hg == j).astype(F32)
        al = jnp.sum(alog, axis=0, keepdims=True)
        db = jnp.sum(dtb, axis=0, keepdims=True)
        lg = _dot_sel(-jnp.exp(al) * _softplus(tail + db) * mask, ea, False)
        beta = _dot_sel(_sigmoid(tail) * mask, eb, False)
        sq, sk, sv = _silu(cq), _silu(ck), _silu(cv)
        qs, ks = [], []
        for h in range(hg):
            sl = slice(h * GDN_DK, (h + 1) * GDN_DK)
            qh, kh = sq[:, sl], sk[:, sl]
            qs.append(qh * lax.rsqrt(jnp.sum(qh * qh, axis=-1, keepdims=True) + EPS) * (GDN_DK ** -0.5))
            ks.append(kh * lax.rsqrt(jnp.sum(kh * kh, axis=-1, keepdims=True) + EPS))
        return jnp.concatenate(qs, axis=1), jnp.concatenate(ks, axis=1), sv, beta, lg
    return fn


def _gdn_intra_fn(pids, q, k, v, bB, lB, t_saved=None):
    rows = q.shape[0]
    nb = rows // CHUNK
    q3, k3, v3, b3, l3 = [t.reshape(nb, CHUNK, GDN_DK) for t in (q, k, v, bB, lB)]
    r, c = _iota2(CHUNK, CHUNK, 0), _iota2(CHUNK, CHUNK, 1)
    tril = (r >= c)
    strict = (r > c)
    gcol = _sel_dot(_bcast(tril.astype(F32), nb), l3)
    l64 = l3[:, :, :CHUNK]
    grow = _sel_dot(jnp.ones((nb, CHUNK, CHUNK), F32), l64 * (r <= c).astype(F32)[None])
    diff = gcol[:, :, :CHUNK] - grow
    decay = jnp.where(tril[None], jnp.exp(jnp.where(tril[None], diff, 0.0)), 0.0)
    kb = k3 * b3
    m = jnp.where(strict[None], _dot(kb, k3, False, True) * decay, 0.0)
    t = _tri_inv_raw(m) if t_saved is None else _tri_inv_given(m, t_saved.reshape(nb, CHUNK, CHUNK))
    eg = jnp.exp(gcol)
    u = _dot(t, v3 * b3)
    w = _dot(t, kb * eg)
    attn = _dot(q3, k3, False, True) * decay
    qd = q3 * eg
    glast = jnp.sum(l3, axis=1, keepdims=True)
    kd = k3 * jnp.exp(glast - gcol)
    gl = jnp.exp(glast)
    outs = (u.reshape(rows, GDN_DK), w.reshape(rows, GDN_DK), attn.reshape(1, rows, CHUNK),
            qd.reshape(rows, GDN_DK), kd.reshape(rows, GDN_DK), gl.reshape(1, nb, 1, GDN_DK))
    return outs + (t.reshape(1, rows, CHUNK),) if t_saved is None else outs


def _make_rot_fn(cfg):
    hr = cfg.hr
    half = RET_DK // 2

    def fn(pids, rqk, cos, sin):
        rq, rk = rqk[:, :cfg.d], rqk[:, cfg.d:]

        def rot(t, scale):
            outs = []
            for h in range(hr):
                x1 = t[:, h * RET_DK:h * RET_DK + half]
                x2 = t[:, h * RET_DK + half:(h + 1) * RET_DK]
                outs += [(x1 * cos - x2 * sin) * scale, (x2 * cos + x1 * sin) * scale]
            return jnp.concatenate(outs, axis=1)
        return rot(rq, 1.0), rot(rk, RET_DK ** -0.5)
    return fn


def _make_mix_fn(cfg):
    hg, hr = cfg.hg, cfg.hr

    def fn(pids, oa, ob, pm, gnorm):
        d = cfg.d
        gz, rg, gate_a, gate_b = pm[:, :d], pm[:, d:2 * d], pm[:, 2 * d:3 * d], pm[:, 3 * d:]
        oas = []
        for h in range(hg):
            oh = oa[:, h * GDN_DK:(h + 1) * GDN_DK]
            oas.append(oh * lax.rsqrt(jnp.mean(oh * oh, axis=-1, keepdims=True) + EPS) * gnorm)
        ya = jnp.concatenate(oas, axis=1) * _silu(gz)
        obs = []
        for h in range(hr):
            oh = ob[:, h * RET_DK:(h + 1) * RET_DK]
            obs.append(oh * lax.rsqrt(jnp.mean(oh * oh, axis=-1, keepdims=True) + EPS))
        yb = _silu(rg) * jnp.concatenate(obs, axis=1)
        return (_sigmoid(gate_a) * ya + _sigmoid(gate_b) * yb,)
    return fn


def _act_fn(pids, u):
    f = u.shape[1] // 2
    return (_silu(u[:, :f]) * u[:, f:],)


def _gdn_step(s, u, w, a, qd, kd, gl):
    top = _dot(jnp.concatenate([w, qd], axis=0), s)
    v_new = u - top[:CHUNK]
    bot = _dot(jnp.concatenate([a, kd.T], axis=0), v_new)
    o = top[CHUNK:] + bot[:CHUNK]
    s2 = s * gl + bot[CHUNK:]
    return s2, o


def _ret_step(s, q, k, v, dm, qdc, kdc, g):
    att = _dot(q, k, False, True) * dm
    bot = _dot(jnp.concatenate([att, (k * kdc).T], axis=0), v)
    o = bot[:CHUNK] + _dot(q * qdc, s)
    s2 = s * g + bot[CHUNK:]
    return s2, o


def _gdn_scan_fwd(cfg, u, w, attn, qd, kd, gl):
    d, hg, nch, sc = cfg.d, cfg.hg, cfg.nch, cfg.sc
    nst = nch // sc

    def body(u_ref, w_ref, a_ref, qd_ref, kd_ref, gl_ref, o_ref, ss_ref, s_ref):
        @pl.when(pl.program_id(0) == 0)
        def _():
            s_ref[...] = jnp.zeros(s_ref.shape, F32)

        states = [s_ref[h] for h in range(hg)]
        for j in range(sc):
            rows = slice(j * CHUNK, (j + 1) * CHUNK)
            outs = []
            for h in range(hg):
                sl = slice(h * GDN_DK, (h + 1) * GDN_DK)
                ss_ref[j, h] = states[h]
                states[h], o = _gdn_step(states[h], u_ref[rows, sl], w_ref[rows, sl], a_ref[h, rows, :],
                                         qd_ref[rows, sl], kd_ref[rows, sl], gl_ref[h, j])
                outs.append(o)
            o_ref[rows, :] = jnp.concatenate(outs, axis=1)
        for h in range(hg):
            s_ref[h] = states[h]

    row = pl.BlockSpec((sc * CHUNK, d), lambda n: (n, 0))
    return _pallas(
        body,
        out_shape=[jax.ShapeDtypeStruct((cfg.rp, d), F32), jax.ShapeDtypeStruct((nch, hg, GDN_DK, GDN_DK), F32)],
        grid=(nst,),
        in_specs=[row, row, pl.BlockSpec((hg, sc * CHUNK, CHUNK), lambda n: (0, n, 0)), row, row,
                  pl.BlockSpec((hg, sc, 1, GDN_DK), lambda n: (0, n, 0, 0))],
        out_specs=[row, pl.BlockSpec((sc, hg, GDN_DK, GDN_DK), lambda n: (n, 0, 0, 0))],
        scratch_shapes=[pltpu.VMEM((hg, GDN_DK, GDN_DK), F32)],
        name="gdn_scan_fwd", compiler_params=_params((nst,)))(u, w, attn, qd, kd, gl)


def _gdn_scan_bwd(cfg, do, u, w, attn, qd, kd, gl, ss):
    d, hg, nch, sc = cfg.d, cfg.hg, cfg.nch, cfg.sc
    nst = nch // sc

    def body(do_ref, u_ref, w_ref, a_ref, qd_ref, kd_ref, gl_ref, ss_ref,
             du_ref, dw_ref, da_ref, dqd_ref, dkd_ref, dgl_ref, ds_ref):
        @pl.when(pl.program_id(0) == 0)
        def _():
            ds_ref[...] = jnp.zeros(ds_ref.shape, F32)

        dstates = [ds_ref[h] for h in range(hg)]
        for j in reversed(range(sc)):
            rows = slice(j * CHUNK, (j + 1) * CHUNK)
            dus, dws, dqds, dkds = [], [], [], []
            for h in range(hg):
                sl = slice(h * GDN_DK, (h + 1) * GDN_DK)
                args = (ss_ref[j, h], u_ref[rows, sl], w_ref[rows, sl], a_ref[h, rows, :], qd_ref[rows, sl],
                        kd_ref[rows, sl], gl_ref[h, j])
                _, vjp_fn = jax.vjp(_gdn_step, *args)
                dstates[h], du, dw, da, dqd, dkd, dgl = vjp_fn((dstates[h], do_ref[rows, sl]))
                da_ref[h, rows, :] = da
                dgl_ref[h, j] = dgl
                dus.append(du)
                dws.append(dw)
                dqds.append(dqd)
                dkds.append(dkd)
            du_ref[rows, :] = jnp.concatenate(dus, axis=1)
            dw_ref[rows, :] = jnp.concatenate(dws, axis=1)
            dqd_ref[rows, :] = jnp.concatenate(dqds, axis=1)
            dkd_ref[rows, :] = jnp.concatenate(dkds, axis=1)
        for h in range(hg):
            ds_ref[h] = dstates[h]

    row = pl.BlockSpec((sc * CHUNK, d), lambda n: (nst - 1 - n, 0))
    aspec = pl.BlockSpec((hg, sc * CHUNK, CHUNK), lambda n: (0, nst - 1 - n, 0))
    gspec = pl.BlockSpec((hg, sc, 1, GDN_DK), lambda n: (0, nst - 1 - n, 0, 0))
    rowshape = jax.ShapeDtypeStruct((cfg.rp, d), F32)
    return _pallas(
        body,
        out_shape=[rowshape, rowshape, jax.ShapeDtypeStruct(attn.shape, F32), rowshape, rowshape,
                   jax.ShapeDtypeStruct(gl.shape, F32)],
        grid=(nst,),
        in_specs=[row, row, row, aspec, row, row, gspec,
                  pl.BlockSpec((sc, hg, GDN_DK, GDN_DK), lambda n: (nst - 1 - n, 0, 0, 0))],
        out_specs=[row, row, aspec, row, row, gspec],
        scratch_shapes=[pltpu.VMEM((hg, GDN_DK, GDN_DK), F32)],
        name="gdn_scan_bwd", compiler_params=_params((nst,)))(do, u, w, attn, qd, kd, gl, ss)


def _ret_consts(cfg):
    hr = cfg.hr
    lg = np.log(1.0 - 2.0 ** (-5.0 - np.arange(hr, dtype=np.float64)))
    idx = np.arange(CHUNK, dtype=np.float64)
    tril = np.tril(np.ones((CHUNK, CHUNK), dtype=bool))
    dm = np.where(tril[None], np.exp((idx[:, None] - idx[None, :])[None] * lg[:, None, None]), 0.0)
    qdc = np.exp((idx[None, :] + 1.0) * lg[:, None])
    kdc = np.exp((CHUNK - 1.0 - idx[None, :]) * lg[:, None])
    gch = np.exp(CHUNK * lg)
    qdc = np.broadcast_to(qdc[:, :, None], (hr, CHUNK, RET_DK))
    kdc = np.broadcast_to(kdc[:, :, None], (hr, CHUNK, RET_DK))
    gch = np.broadcast_to(gch[:, None, None], (hr, 1, RET_DK))
    return tuple(jnp.asarray(np.ascontiguousarray(t), F32) for t in (dm, qdc, kdc, gch))


def _ret_scan_fwd(cfg, qr, kr, proj, consts):
    d, hr, nch, sc = cfg.d, cfg.hr, cfg.nch, cfg.sc
    nst = nch // sc
    dm, qdc, kdc, gch = consts

    def body(q_ref, k_ref, v_ref, dm_ref, qdc_ref, kdc_ref, g_ref, o_ref, ss_ref, s_ref):
        @pl.when(pl.program_id(0) == 0)
        def _():
            s_ref[...] = jnp.zeros(s_ref.shape, F32)

        states = [s_ref[h] for h in range(hr)]
        for j in range(sc):
            rows = slice(j * CHUNK, (j + 1) * CHUNK)
            outs = []
            for h in range(hr):
                sl = slice(h * RET_DK, (h + 1) * RET_DK)
                ss_ref[j, h] = states[h]
                states[h], o = _ret_step(states[h], q_ref[rows, sl], k_ref[rows, sl], v_ref[rows, sl], dm_ref[h],
                                         qdc_ref[h], kdc_ref[h], g_ref[h])
                outs.append(o)
            o_ref[rows, :] = jnp.concatenate(outs, axis=1)
        for h in range(hr):
            s_ref[h] = states[h]

    row = pl.BlockSpec((sc * CHUNK, d), lambda n: (n, 0))
    return _pallas(
        body,
        out_shape=[jax.ShapeDtypeStruct((cfg.rp, d), F32), jax.ShapeDtypeStruct((nch, hr, RET_DK, RET_DK), F32)],
        grid=(nst,),
        in_specs=[row, row, pl.BlockSpec((sc * CHUNK, d), lambda n: (n, RV_BLOCK)), _full(dm), _full(qdc), _full(kdc),
                  _full(gch)],
        out_specs=[row, pl.BlockSpec((sc, hr, RET_DK, RET_DK), lambda n: (n, 0, 0, 0))],
        scratch_shapes=[pltpu.VMEM((hr, RET_DK, RET_DK), F32)],
        name="ret_scan_fwd", compiler_params=_params((nst,)))(qr, kr, proj, dm, qdc, kdc, gch)


def _ret_scan_bwd(cfg, do, qr, kr, proj, consts, ss, dproj):
    d, hr, nch, sc = cfg.d, cfg.hr, cfg.nch, cfg.sc
    nst = nch // sc
    dm, qdc, kdc, gch = consts

    def body(do_ref, q_ref, k_ref, v_ref, dm_ref, qdc_ref, kdc_ref, g_ref, ss_ref, _, dq_ref, dk_ref, dv_ref, ds_ref):
        @pl.when(pl.program_id(0) == 0)
        def _():
            ds_ref[...] = jnp.zeros(ds_ref.shape, F32)

        dstates = [ds_ref[h] for h in range(hr)]
        for j in reversed(range(sc)):
            rows = slice(j * CHUNK, (j + 1) * CHUNK)
            dqs, dks, dvs = [], [], []
            for h in range(hr):
                sl = slice(h * RET_DK, (h + 1) * RET_DK)
                cs = (dm_ref[h], qdc_ref[h], kdc_ref[h], g_ref[h])
                _, vjp_fn = jax.vjp(lambda s, q, k, v, cs=cs: _ret_step(s, q, k, v, *cs),
                                    ss_ref[j, h], q_ref[rows, sl], k_ref[rows, sl], v_ref[rows, sl])
                dstates[h], dq, dk, dv = vjp_fn((dstates[h], do_ref[rows, sl]))
                dqs.append(dq)
                dks.append(dk)
                dvs.append(dv)
            dq_ref[rows, :] = jnp.concatenate(dqs, axis=1)
            dk_ref[rows, :] = jnp.concatenate(dks, axis=1)
            dv_ref[rows, :] = jnp.concatenate(dvs, axis=1).astype(dv_ref.dtype)
        for h in range(hr):
            ds_ref[h] = dstates[h]

    row = pl.BlockSpec((sc * CHUNK, d), lambda n: (nst - 1 - n, 0))
    rowshape = jax.ShapeDtypeStruct((cfg.rp, d), F32)
    vspec = pl.BlockSpec((sc * CHUNK, d), lambda n: (nst - 1 - n, RV_BLOCK))
    return _pallas(
        body,
        out_shape=[rowshape, rowshape, jax.ShapeDtypeStruct(dproj.shape, dproj.dtype)],
        grid=(nst,),
        in_specs=[row, row, row, vspec, _full(dm), _full(qdc), _full(kdc), _full(gch),
                  pl.BlockSpec((sc, hr, RET_DK, RET_DK), lambda n: (nst - 1 - n, 0, 0, 0)), ANY],
        out_specs=[row, row, vspec], input_output_aliases={9: 2},
        scratch_shapes=[pltpu.VMEM((hr, RET_DK, RET_DK), F32)],
        name="ret_scan_bwd", compiler_params=_params((nst,)))(do, qr, kr, proj, dm, qdc, kdc, gch, ss, dproj)


def _final(cfg, h2, normf, tgt):
    d, tr = cfg.d, cfg.xrow
    nr = cfg.rp // tr

    def body(h_ref, g_ref, t_ref, dh_ref, dg_ref, loss_ref):
        i = pl.program_id(0)
        y, vjp_fn = jax.vjp(_rms, h_ref[...], g_ref[...])
        err = jnp.where(i >= 1, y - t_ref[...], 0.0)
        dh, dg = vjp_fn(err * (1.0 / d))
        dh_ref[...] = dh
        part = jnp.zeros((8, LANES), F32) + 0.5 * jnp.sum(err * err) * (1.0 / d)

        @pl.when(i == 0)
        def _():
            dg_ref[...] = dg
            loss_ref[...] = part

        @pl.when(i > 0)
        def _():
            dg_ref[...] += dg
            loss_ref[...] += part

    return _pallas(
        body,
        out_shape=[jax.ShapeDtypeStruct((cfg.rp, d), F32), jax.ShapeDtypeStruct((1, d), F32),
                   jax.ShapeDtypeStruct((8, LANES), F32)],
        grid=(nr,),
        in_specs=[_rows(tr, d), _full(normf), pl.BlockSpec((tr, d), lambda i: (jnp.maximum(i - 1, 0), 0))],
        out_specs=[_rows(tr, d), pl.BlockSpec((1, d), lambda i: (0, 0)), pl.BlockSpec((8, LANES), lambda i: (0, 0))],
        name="final_loss", compiler_params=_params((nr,)))(h2, normf, tgt)


ANY = pl.BlockSpec(memory_space=pl.ANY)


def _place():
    x, y, c = lax.axis_index("x"), lax.axis_index("y"), lax.axis_index("c")
    others = [(1 - x, y), (x, 1 - y), (1 - x, 1 - y)]
    return x, y, c, others


def _row_tile(rows, cap=256):
    return max(t for t in range(16, min(rows, cap) + 1, 16) if rows % t == 0)


class Comm(NamedTuple):
    ins: list
    outs: list
    sems: list
    start: object
    finish: object


def _run_comm(name, comm):
    n_in, n_out = len(comm.ins), len(comm.outs)

    def body(*refs):
        ins, outs, sems = refs[:n_in], refs[n_in:n_in + n_out], refs[n_in + n_out:]
        comm.start(ins, outs, sems)
        comm.finish(ins, outs, sems)

    return _pallas(body, out_shape=comm.outs, in_specs=[ANY] * n_in, out_specs=[ANY] * n_out,
                   scratch_shapes=comm.sems, name=name)(*comm.ins)


def _gather_comm(ws):
    n = len(ws)
    halves = [w.shape[0] // 2 for w in ws]

    def copies(w_refs, o_refs, sems):
        send_sems, recv_sems = sems
        x, y, c, others = _place()
        me = 2 * x + y
        chips = [2 * px + py for px, py in others]

        def piece(a, chip, core):
            return o_refs[a].at[chip, pl.ds(core * halves[a], halves[a]), :]

        def copy(a, k, src, chip, core, to):
            return pltpu.make_async_remote_copy(src_ref=src, dst_ref=piece(a, chip, core), send_sem=send_sems.at[6 * a + k],
                                                recv_sem=recv_sems.at[6 * a + k], device_id=to, device_id_type=MESH)

        def first(j, a):
            return copy(a, j, w_refs[a].at[pl.ds(c * halves[a], halves[a]), :], me, c, (*others[j], c))

        def landed(j, a):
            return copy(a, j, piece(a, chips[j], c), chips[j], c, (x, y, c))

        def passed(j, a):
            return copy(a, 3 + j, piece(a, chips[j], c), chips[j], c, (x, y, 1 - c))

        def from_sibling(j, a):
            return copy(a, 3 + j, piece(a, chips[j], 1 - c), chips[j], 1 - c, (x, y, c))

        return first, landed, passed, from_sibling

    pairs = [(j, a) for j in range(3) for a in range(n)]

    def start(w_refs, o_refs, sems):
        first, _, _, _ = copies(w_refs, o_refs, sems)
        for j, a in pairs:
            first(j, a).start()

    def finish(w_refs, o_refs, sems):
        first, landed, passed, from_sibling = copies(w_refs, o_refs, sems)
        for j, a in pairs:
            landed(j, a).wait_recv()
            passed(j, a).start()
        for j, a in pairs:
            from_sibling(j, a).wait_recv()
        for j, a in pairs:
            first(j, a).wait_send()
            passed(j, a).wait_send()

    return Comm(list(ws), [jax.ShapeDtypeStruct((N_CHIPS,) + w.shape, w.dtype) for w in ws],
                [pltpu.SemaphoreType.DMA((6 * n,)), pltpu.SemaphoreType.DMA((6 * n,))], start, finish)


def _pair_exchange(name, gs):
    n = len(gs)

    def body(*refs):
        g_refs, o_refs = refs[:n], refs[n:2 * n]
        send_sems, recv_sems = refs[2 * n:]
        x, y, c, _ = _place()
        cps = []
        for a in range(n):
            half = gs[a].shape[1] // 2
            cp = pltpu.make_async_remote_copy(
                src_ref=g_refs[a].at[:, pl.ds((1 - c) * half, half), :], dst_ref=o_refs[a], send_sem=send_sems.at[a],
                recv_sem=recv_sems.at[a], device_id=(x, y, 1 - c), device_id_type=MESH)
            cp.start()
            cps.append(cp)
        for cp in cps:
            cp.wait()

    return _pallas(
        body, out_shape=[jax.ShapeDtypeStruct((N_CHIPS, g.shape[1] // 2, g.shape[2]), g.dtype) for g in gs],
        in_specs=[ANY] * n, out_specs=[ANY] * n,
        scratch_shapes=[pltpu.SemaphoreType.DMA((n,)), pltpu.SemaphoreType.DMA((n,))], name=name)(*gs)


def _pair_sum(name, g, recv, cidx):
    half, cols = recv.shape[1], recv.shape[2]
    tr = _row_tile(half)
    nblk = half // tr

    def body(c_ref, g_ref, r_ref, o_ref):
        o_ref[...] = (g_ref[...] + r_ref[...]).astype(o_ref.dtype)

    grid_spec = pltpu.PrefetchScalarGridSpec(
        num_scalar_prefetch=1, grid=(N_CHIPS, nblk),
        in_specs=[pl.BlockSpec((1, tr, cols), lambda s, i, c: (s, c[0] * nblk + i, 0)),
                  pl.BlockSpec((1, tr, cols), lambda s, i, c: (s, i, 0))],
        out_specs=pl.BlockSpec((1, tr, cols), lambda s, i, c: (s, i, 0)))
    return _pallas(
        body, out_shape=jax.ShapeDtypeStruct((N_CHIPS, half, cols), BF16), grid_spec=grid_spec,
        name=name, compiler_params=_params((N_CHIPS, nblk)))(cidx, g, recv)


def _exchange_comm(parts):
    n = len(parts)

    def copies(p_refs, o_refs, sems):
        send_sems, recv_sems = sems
        x, y, c, others = _place()
        me = 2 * x + y

        def copy(a, j, src_chip, dst_chip):
            px, py = others[j]
            return pltpu.make_async_remote_copy(
                src_ref=p_refs[a].at[src_chip], dst_ref=o_refs[a].at[dst_chip], send_sem=send_sems.at[3 * a + j],
                recv_sem=recv_sems.at[3 * a + j], device_id=(px, py, c), device_id_type=MESH)

        def send(j, a):
            return copy(a, j, 2 * others[j][0] + others[j][1], me)

        def arrival(j, a):
            return copy(a, j, me, 2 * others[j][0] + others[j][1])

        return send, arrival

    pairs = [(j, a) for j in range(3) for a in range(n)]

    def start(p_refs, o_refs, sems):
        send, _ = copies(p_refs, o_refs, sems)
        for j, a in pairs:
            send(j, a).start()

    def finish(p_refs, o_refs, sems):
        send, arrival = copies(p_refs, o_refs, sems)
        for j, a in pairs:
            arrival(j, a).wait_recv()
        for j, a in pairs:
            send(j, a).wait_send()

    return Comm(list(parts), [jax.ShapeDtypeStruct(p.shape, p.dtype) for p in parts],
                [pltpu.SemaphoreType.DMA((3 * n,)), pltpu.SemaphoreType.DMA((3 * n,))], start, finish)


def _chip_sum(name, part, slots, chip):
    half, cols = slots.shape[1], slots.shape[2]
    tr = _row_tile(half)

    def body(me_ref, p_ref, *rest):
        s_refs, o_ref = rest[:N_CHIPS], rest[N_CHIPS]
        own = p_ref[...].astype(F32)
        v = [jnp.where(me_ref[0] == k, own, s_refs[k][...].astype(F32)) for k in range(N_CHIPS)]
        o_ref[...] = ((v[0] + v[1]) + v[2]) + v[3]

    def slot_spec(k):
        return pl.BlockSpec((None, tr, cols), lambda i, me: (jnp.where(me[0] == k, (k + 1) % N_CHIPS, k), i, 0))

    grid_spec = pltpu.PrefetchScalarGridSpec(
        num_scalar_prefetch=1, grid=(half // tr,),
        in_specs=[pl.BlockSpec((None, tr, cols), lambda i, me: (me[0], i, 0))] + [slot_spec(k) for k in range(N_CHIPS)],
        out_specs=pl.BlockSpec((tr, cols), lambda i, me: (i, 0)))
    return _pallas(
        body, out_shape=jax.ShapeDtypeStruct((half, cols), F32), grid_spec=grid_spec,
        name=name, compiler_params=_params((half // tr,)))(chip, part, *([slots] * N_CHIPS))


def _pair_swap(fins):
    n = len(fins)

    def body(*refs):
        f_refs, o_refs = refs[:n], refs[n:2 * n]
        send_sems, recv_sems = refs[2 * n:]
        x, y, c, _ = _place()
        cps = [pltpu.make_async_remote_copy(src_ref=f_refs[a], dst_ref=o_refs[a], send_sem=send_sems.at[a],
                                            recv_sem=recv_sems.at[a], device_id=(x, y, 1 - c), device_id_type=MESH)
               for a in range(n)]
        for cp in cps:
            cp.start()
        for cp in cps:
            cp.wait()

    return _pallas(
        body, out_shape=[jax.ShapeDtypeStruct(f.shape, f.dtype) for f in fins], in_specs=[ANY] * n, out_specs=[ANY] * n,
        scratch_shapes=[pltpu.SemaphoreType.DMA((n,)), pltpu.SemaphoreType.DMA((n,))], name="grad_pair_swap")(*fins)


def _adamw(name, w, g_own, g_other, m, v, cidx):
    R, cols = w.shape[-2:]
    lead = (None,) * (w.ndim - 2)
    zeros = (0,) * (w.ndim - 2)
    half = R // 2
    tr = _row_tile(half, 128)
    nblk = half // tr
    c1 = 1.0 - ADAM_B1 ** ADAM_STEP
    c2 = 1.0 - ADAM_B2 ** ADAM_STEP

    def body(c_ref, w_ref, go_ref, gs_ref, m_ref, v_ref, g_ref, d_ref, nm_ref, nv_ref):
        mine = (pl.program_id(0) // nblk) == c_ref[0]
        gv = jnp.where(mine, go_ref[...], gs_ref[...])
        nm = ADAM_B1 * m_ref[...] + (1.0 - ADAM_B1) * gv
        nv = ADAM_B2 * v_ref[...] + (1.0 - ADAM_B2) * (gv * gv)
        g_ref[...] = gv
        d_ref[...] = -ADAM_LR * ((nm / c1) / (jnp.sqrt(nv / c2) + ADAM_EPS) + ADAM_WD * w_ref[...])
        nm_ref[...] = nm
        nv_ref[...] = nv

    spec = pl.BlockSpec(lead + (tr, cols), lambda i, c: zeros + (i, 0))
    hspec = pl.BlockSpec((tr, cols), lambda i, c: (i % nblk, 0))
    shape = jax.ShapeDtypeStruct(w.shape, F32)
    grid_spec = pltpu.PrefetchScalarGridSpec(num_scalar_prefetch=1, grid=(R // tr,),
                                             in_specs=[spec, hspec, hspec, spec, spec], out_specs=[spec] * 4)
    return _pallas(
        body, out_shape=[shape] * 4, grid_spec=grid_spec,
        name=name, compiler_params=_params((R // tr,)))(cidx, w, g_own, g_other, m, v)


PARAMS = (("meta", 1), ("norm1", None), ("w_in", 2), ("gdn_conv_w", 2), ("gdn_a_log", None), ("gdn_dt_bias", None),
          ("gdn_norm", None), ("w_out", 1), ("norm2", None), ("w_ffn_up", 2), ("ffn_conv_w", 2), ("ffn_conv_b", None),
          ("w_ffn_down", 1), ("norm_f", None))
BIG = ("w_in", "w_out", "w_ffn_up", "w_ffn_down")
PACK_ALIGN = 1024
PACK_ROWS_ALIGN = 32


def _pack(arrs, dtype):
    parts, total = [], 0
    for a in arrs:
        f = a.reshape(-1).astype(dtype)
        pad = (-f.shape[0]) % PACK_ALIGN
        parts.append(jnp.pad(f, (0, pad)) if pad else f)
        total += f.shape[0] + pad
    rows = total // LANES
    rpad = (-rows) % PACK_ROWS_ALIGN
    if rpad:
        parts.append(jnp.zeros((rpad * LANES,), dtype))
    return jnp.concatenate(parts).reshape(rows + rpad, LANES)


def _unpack(buf, shapes):
    flat = buf.reshape(-1)
    outs, off = [], 0
    for s in shapes:
        n = int(np.prod(s))
        outs.append(flat[off:off + n].reshape(s))
        off += n + (-n) % PACK_ALIGN
    return outs


def _split4(a, axis):
    n = a.shape[axis] // N_CHIPS
    return [lax.slice_in_dim(a, s * n, (s + 1) * n, axis=axis) for s in range(N_CHIPS)]


PROJ_ORDER = (3, 7, 8, 9, 0, 1, 2, 6, 4, 5)


def _reorder_w_in(w, cfg):
    d, hg = cfg.d, cfg.hg

    def block(k):
        off = k * d + (2 * hg if k >= 4 else 0)
        return w[:, off:off + d]

    tail = jnp.pad(w[:, 4 * d:4 * d + 2 * hg], ((0, 0), (0, LANES - 2 * hg)))
    return jnp.concatenate([block(k) for k in PROJ_ORDER] + [tail], axis=1)


def _restore_w_in(wr, cfg):
    d, hg = cfg.d, cfg.hg
    at = {k: i for i, k in enumerate(PROJ_ORDER)}
    block = lambda k: wr[:, at[k] * d:(at[k] + 1) * d]
    return jnp.concatenate([block(k) for k in range(4)] + [wr[:, 10 * d:10 * d + 2 * hg]] +
                           [block(k) for k in range(4, 10)], axis=1)


def _step(cfg, x, tgt, shard, m_shard, v_shard):
    d, hg, hr, dff, rp, tr, tm = cfg.d, cfg.hg, cfg.hr, cfg.dff, cfg.rp, cfg.tr, cfg.tm
    nrow = rp // tr
    assert cfg.tf * N_CHIPS == 2 * dff and cfg.din % N_CHIPS == 0
    cidx = lax.axis_index("c").astype(jnp.int32).reshape(1)
    chip = (2 * lax.axis_index("x") + lax.axis_index("y")).astype(jnp.int32).reshape(1)

    axis = dict(PARAMS)
    small = ("meta", "gdn_conv_w", "ffn_conv_w")
    small_shapes = [shard[n].shape for n in small]
    mine = [shard[n][0].astype(BF16) for n in BIG] + [_pack([shard[n] for n in small], F32)]

    def with_own(gathered, own):
        return [lax.dynamic_update_slice(g, w[None], (chip[0], 0, 0)) for g, w in zip(gathered, own)]

    g_in, g_small = with_own(_run_comm("weights_gather_first", _gather_comm([mine[0], mine[4]])), [mine[0], mine[4]])
    w_in_r = _reorder_w_in(jnp.concatenate([g_in[s] for s in range(N_CHIPS)], axis=1), cfg)
    per_chip = [_unpack(g_small[s], small_shapes) for s in range(N_CHIPS)]
    full = {n: jnp.concatenate([per_chip[s][k] for s in range(N_CHIPS)], axis=axis[n]) for k, n in enumerate(small)}
    meta = full["meta"]
    gconv_w = full["gdn_conv_w"][0]
    fconv_w = full["ffn_conv_w"][0]
    norm1, norm2, gnorm = shard["norm1"], shard["norm2"], shard["gdn_norm"]
    normf = shard["norm_f"].reshape(1, d)
    fconv_b = shard["ffn_conv_b"]
    alog = jnp.pad(shard["gdn_a_log"], ((0, 7), (0, LANES - hg)))
    dtb = jnp.pad(shard["gdn_dt_bias"], ((0, 7), (0, LANES - hg)))

    h0 = jnp.concatenate([jnp.zeros((cfg.front, d), F32), meta, x], axis=0)
    half = RET_DK // 2
    pos = np.arange(rp, dtype=np.float32) - np.float32(cfg.front)
    inv = (np.float32(1.0) / np.float32(ROPE_BASE) ** (np.arange(half, dtype=np.float32) / np.float32(half))).astype(np.float32)
    ang = pos[:, None] * inv[None, :]
    cos, sin = jnp.asarray(np.cos(ang), F32), jnp.asarray(np.sin(ang), F32)
    rconsts = _ret_consts(cfg)

    tr_n = 3 * tr if rp % (3 * tr) == 0 else tr
    rms_f = _make_rms_fn(cfg, tr_n, False)
    rms_b = _make_rms_fn(cfg, tr_n, True)
    rowshape = jax.ShapeDtypeStruct((rp, d), F32)
    rspec = _rows(tr, d)
    nspec = _rows(tr_n, d)

    def rms_fwd(name, h, g):
        return _stage_fwd(name, rms_f, (rp // tr_n,), [In(h, nspec), In(g, _full(g))],
                          [jax.ShapeDtypeStruct((rp, d), BF16)], [nspec])[0]

    tn_in = cfg.pw // 9 if cfg.pw % (9 * LANES) == 0 else LANES
    hn1 = rms_fwd("rms1_fwd", h0, norm1)
    proj, rest = _mm("proj_fwd", hn1, w_in_r, tm=tm, tn=tn_in, tk=d, comm=_gather_comm(mine[1:4]))
    g_out, g_up, g_down = with_own(rest, mine[1:4])
    w_out = g_out.reshape(d, d)
    w_up = g_up
    w_up_t = jnp.swapaxes(g_up, 1, 2).reshape(2 * dff, d)
    w_down = g_down.reshape(dff, d)
    cqkv = _conv_fwd("gdn_conv_fwd", proj, CONV_COL * d, gconv_w, None, taps=GDN_CONV, width=3 * d, tr=tr, tc=d)
    tail_spec = _rows(tr, LANES, TAIL_COL * d // LANES)
    prep_fn = _make_gdn_prep_fn(cfg, tr)

    def prep_ins(dproj=None):
        return [In(cqkv, _rows(tr, 3 * d), BF16), In(proj, tail_spec, BF16, galias=dproj, gshape=(rp, cfg.pw)),
                In(alog, _full(alog), F32, True), In(dtb, _full(dtb), F32, True)]

    qn, kn, vv, bB, lB = _stage_fwd("gdn_prep_fwd", prep_fn, (nrow,), prep_ins(), [rowshape] * 5, [rspec] * 5)

    trg = cfg.nb * CHUNK
    gi_grid = (rp // trg, hg)
    hspec = pl.BlockSpec((trg, GDN_DK), lambda i, h: (i, h))
    aspec = pl.BlockSpec((1, trg, CHUNK), lambda i, h: (h, i, 0))
    gspec = pl.BlockSpec((1, cfg.nb, 1, GDN_DK), lambda i, h: (h, i, 0, 0))
    intra_ins = [In(t, hspec, F32) for t in (qn, kn, vv, bB, lB)]
    ashape = jax.ShapeDtypeStruct((hg, rp, CHUNK), F32)
    intra_shapes = [rowshape, rowshape, ashape, rowshape, rowshape, jax.ShapeDtypeStruct((hg, cfg.nch, 1, GDN_DK), F32), ashape]
    intra_specs = [hspec, hspec, aspec, hspec, hspec, gspec, aspec]
    gu, gw, gattn, gqd, gkd, ggl, gtinv = _stage_fwd("gdn_intra_fwd", _gdn_intra_fn, gi_grid, intra_ins, intra_shapes,
                                                     intra_specs)
    oa, gss = _gdn_scan_fwd(cfg, gu, gw, gattn, gqd, gkd, ggl)

    rot_fn = _make_rot_fn(cfg)

    def rot_ins(dproj=None):
        return [In(proj, _rows(tr, 2 * d, ROT_COL // 2), BF16, galias=dproj, gshape=(rp, cfg.pw)),
                In(cos, _rows(tr, half)), In(sin, _rows(tr, half))]

    qr, kr = _stage_fwd("rot_fwd", rot_fn, (nrow,), rot_ins(), [rowshape] * 2, [rspec] * 2)
    ob, rss = _ret_scan_fwd(cfg, qr, kr, proj, rconsts)

    mix_fn = _make_mix_fn(cfg)
    mix_ins = [In(oa, rspec, F32), In(ob, rspec, F32), In(proj, _rows(tr, 4 * d, MIX_COL // 4), BF16, gshape=(rp, cfg.pw)),
               In(gnorm, _full(gnorm), F32, True)]
    ymix = _stage_fwd("mix_fwd", mix_fn, (nrow,), mix_ins, [jax.ShapeDtypeStruct((rp, d), BF16)], [rspec])[0]
    h1 = _mm("out_proj_fwd", ymix, w_out, tm=tm, tn=d, tk=d, add=h0)

    hn2 = rms_fwd("rms2_fwd", h1, norm2)
    up = _mm("ffn_up_fwd", hn2, w_up, tm=tm, tn=cfg.tf, tk=d)
    uc = _conv_fwd("ffn_conv_fwd", up, 0, fconv_w, fconv_b, taps=FFN_CONV, width=2 * dff, tr=tr, tc=cfg.tf)
    tra = tr // 2
    act_ins = [In(uc, _rows(tra, 2 * dff), BF16)]
    act_spec = _rows(tra, dff)
    act = _stage_fwd("ffn_act_fwd", _act_fn, (rp // tra,), act_ins, [jax.ShapeDtypeStruct((rp, dff), BF16)], [act_spec])[0]
    h2 = _mm("ffn_down_fwd", act, w_down, tm=tm, tn=d, tk=cfg.tf, add=h1)

    dh2, g_normf, loss_blk = _final(cfg, h2, normf, tgt)
    loss = lax.psum(loss_blk[0, 0], ("x", "y", "c"))

    g_w_down = _mm_tn("ffn_down_dw", act, dh2, tr=tm, tka=cfg.tf, tn=d)
    dact = _mm("ffn_down_dx", dh2, w_down.T, tm=tm, tn=cfg.tf, tk=d)
    duc, = _stage_bwd("ffn_act_bwd", _act_fn, (rp // tra,), act_ins, [(dact, act_spec)])
    dup, g_fconv_w, g_fconv_b = _conv_bwd("ffn_conv_bwd", up, 0, fconv_w, duc, taps=FFN_CONV, width=2 * dff,
                                          tr=tr, tc=cfg.tf, with_bias=True)
    g_w_up = _mm_tn("ffn_up_dw", hn2, dup, tr=tm, tka=d, tn=cfg.tf, blocked=True)

    def pair_reduce(tag, names, arrs):
        recvs = _pair_exchange("grad_pair_exchange_" + tag, arrs)
        return [_pair_sum("grad_pair_sum_" + n, g, r, cidx) for n, g, r in zip(names, arrs, recvs)]

    parts_ffn = pair_reduce("ffn", ["w_ffn_down", "w_ffn_up"], [g_w_down.reshape(N_CHIPS, dff // N_CHIPS, d), g_w_up])
    dhn2, slots_ffn = _mm("ffn_up_dx", dup, w_up_t, tm=tm, tn=d, tk=cfg.tf, comm=_exchange_comm(parts_ffn))

    def rms_bwd(name, h, g, dhn, dres):
        ins = [In(h, nspec, F32), In(g, _full(g), F32, True)]
        return _stage_bwd(name, rms_b, (rp // tr_n,), ins, [(dhn, nspec), (dres, nspec)])

    dh1, g_norm2 = rms_bwd("rms2_bwd", h1, norm2, dhn2, dh2)
    g_w_out = _mm_tn("out_proj_dw", ymix, dh1, tr=tm, tka=d, tn=d)
    dymix = _mm("out_proj_dx", dh1, w_out.T, tm=tm, tn=d, tk=d)
    doa, dob, dproj, g_gnorm = _stage_bwd("mix_bwd", mix_fn, (nrow,), mix_ins, [(dymix, rspec)])

    dqr, dkr, dproj = _ret_scan_bwd(cfg, dob, qr, kr, proj, rconsts, rss, dproj)
    dproj, = _stage_bwd("rot_bwd", rot_fn, (nrow,), rot_ins(dproj), [(dqr, rspec), (dkr, rspec)])

    dgu, dgw, dgattn, dgqd, dgkd, dggl = _gdn_scan_bwd(cfg, doa, gu, gw, gattn, gqd, gkd, ggl, gss)
    intra_cots = [(dgu, hspec), (dgw, hspec), (dgattn, aspec), (dgqd, hspec), (dgkd, hspec), (dggl, gspec)]
    dqn, dkn, dvv, dbB, dlB = _stage_bwd("gdn_intra_bwd", _gdn_intra_fn, gi_grid, intra_ins + [In(gtinv, aspec)], intra_cots)
    dcqkv, dproj, g_alog, g_dtb = _stage_bwd(
        "gdn_prep_bwd", prep_fn, (nrow,), prep_ins(dproj), [(t, rspec) for t in (dqn, dkn, dvv, dbB, dlB)])
    dproj, g_gconv_w = _conv_bwd("gdn_conv_bwd", proj, CONV_COL * d, gconv_w, dcqkv, taps=GDN_CONV, width=3 * d,
                                 tr=tr, tc=d, with_bias=False, dx_into=dproj)
    g_w_in_r = _mm_tn("proj_dw", hn1, dproj, tr=tm, tka=d, tn=tn_in)
    g_in4 = jnp.stack(_split4(_restore_w_in(g_w_in_r, cfg), 1))
    parts_mix = pair_reduce("mix", ["w_out", "w_in"], [g_w_out.reshape(N_CHIPS, d // N_CHIPS, d), g_in4])
    dhn1, slots_mix = _mm("proj_dx", dproj, w_in_r.T, tm=tm, tn=d, tk=tn_in, comm=_exchange_comm(parts_mix))
    dh0, g_norm1 = rms_bwd("rms1_bwd", h0, norm1, dhn1, dh1)

    grad_x = dh0[cfg.xrow:]
    small_grads = {
        "meta": dh0[cfg.front:cfg.xrow], "norm1": g_norm1, "gdn_conv_w": g_gconv_w[None],
        "gdn_a_log": g_alog[0:1, :hg], "gdn_dt_bias": g_dtb[0:1, :hg], "gdn_norm": g_gnorm, "norm2": g_norm2,
        "ffn_conv_w": g_fconv_w[None], "ffn_conv_b": g_fconv_b, "norm_f": g_normf.reshape(d),
    }

    small_names = [n for n, _ in PARAMS if n not in BIG]
    g_small = jnp.stack([_pack([small_grads[n] if axis[n] is None else _split4(small_grads[n], axis[n])[s]
                                for n in small_names], F32) for s in range(N_CHIPS)])
    parts_small = pair_reduce("small", ["small"], [g_small])
    slots_small = _run_comm("grad_exchange_small", _exchange_comm(parts_small))
    tags = ["w_in", "w_out", "w_ffn_up", "w_ffn_down", "small"]
    parts = [parts_mix[1], parts_mix[0], parts_ffn[1], parts_ffn[0], parts_small[0]]
    slots = [slots_mix[1], slots_mix[0], slots_ffn[1], slots_ffn[0], slots_small[0]]
    fins = [_chip_sum("grad_chip_sum_" + t, p, s, chip) for t, p, s in zip(tags, parts, slots)]
    sibs = _pair_swap(fins)

    def flat2(a):
        return a.reshape(-1, a.shape[-1])

    outs = {}
    for k, t in enumerate(BIG):
        res = _adamw("adamw_" + t, flat2(shard[t]), fins[k], sibs[k], flat2(m_shard[t]), flat2(v_shard[t]), cidx)
        outs[t] = [r.reshape(shard[t].shape) for r in res]
    small_shapes_all = [shard[n].shape for n in small_names]
    pk = lambda src: _pack([src[n] for n in small_names], F32)
    res = _adamw("adamw_small", pk(shard), fins[4], sibs[4], pk(m_shard), pk(v_shard), cidx)
    for k, r in enumerate(res):
        for n, a in zip(small_names, _unpack(r, small_shapes_all)):
            outs.setdefault(n, [None] * 4)[k] = a
    names = [n for n, _ in PARAMS]
    return (loss, grad_x[None], *[outs[n][k] for k in range(4) for n in names])


def kernel(x, meta, norm1, w_in, gdn_conv_w, gdn_a_log, gdn_dt_bias, gdn_norm, w_out, norm2, w_ffn_up, ffn_conv_w, ffn_conv_b, w_ffn_down, norm_f, loss_target, m_meta, m_norm1, m_w_in, m_gdn_conv_w, m_gdn_a_log, m_gdn_dt_bias, m_gdn_norm, m_w_out, m_norm2, m_w_ffn_up, m_ffn_conv_w, m_ffn_conv_b, m_w_ffn_down, m_norm_f, v_meta, v_norm1, v_w_in, v_gdn_conv_w, v_gdn_a_log, v_gdn_dt_bias, v_gdn_norm, v_w_out, v_norm2, v_w_ffn_up, v_ffn_conv_w, v_ffn_conv_b, v_w_ffn_down, v_norm_f):
    names = [n for n, _ in PARAMS]
    shard = dict(zip(names, (meta, norm1, w_in, gdn_conv_w, gdn_a_log, gdn_dt_bias, gdn_norm, w_out, norm2, w_ffn_up,
                             ffn_conv_w, ffn_conv_b, w_ffn_down, norm_f)))
    m_shard = dict(zip(names, (m_meta, m_norm1, m_w_in, m_gdn_conv_w, m_gdn_a_log, m_gdn_dt_bias, m_gdn_norm, m_w_out,
                               m_norm2, m_w_ffn_up, m_ffn_conv_w, m_ffn_conv_b, m_w_ffn_down, m_norm_f)))
    v_shard = dict(zip(names, (v_meta, v_norm1, v_w_in, v_gdn_conv_w, v_gdn_a_log, v_gdn_dt_bias, v_gdn_norm, v_w_out,
                               v_norm2, v_w_ffn_up, v_ffn_conv_w, v_ffn_conv_b, v_w_ffn_down, v_norm_f)))
    return _step(REAL, x[0], loss_target[0], shard, m_shard, v_shard)
```

```python
import functools
from typing import NamedTuple

import numpy as np
import jax
import jax.numpy as jnp
from jax import lax
from jax.experimental import pallas as pl
from jax.experimental.pallas import tpu as pltpu

F32 = jnp.float32
BF16 = jnp.bfloat16
EPS = 1e-6
CHUNK = 64
GDN_DK = 128
RET_DK = 256
GDN_CONV = 4
FFN_CONV = 3
ROPE_BASE = 10000.0
LANES = 128
N_CHIPS = 4
ADAM_LR, ADAM_B1, ADAM_B2, ADAM_EPS, ADAM_WD, ADAM_STEP = 0.001, 0.9, 0.999, 1e-08, 0.01, 10
MIX_COL, CONV_COL, RV_BLOCK, ROT_COL, TAIL_COL = 0, 4, 7, 8, 10
MESH = pl.DeviceIdType.MESH
VMEM_LIMIT = 56 * 1024 * 1024


class Cfg(NamedTuple):
    d: int
    seq: int
    n_meta: int
    dff: int
    tr: int
    nb: int
    tm: int
    tf: int
    sc: int

    @property
    def hg(self): return self.d // GDN_DK
    @property
    def hr(self): return self.d // RET_DK
    @property
    def L(self): return self.n_meta + self.seq
    @property
    def rp(self): return -(-self.L // 256) * 256
    @property
    def front(self): return self.rp - self.L
    @property
    def xrow(self): return self.rp - self.seq
    @property
    def nch(self): return self.rp // CHUNK
    @property
    def pw(self): return 10 * self.d + LANES
    @property
    def din(self): return 10 * self.d + 2 * self.hg


REAL = Cfg(d=1024, seq=8192, n_meta=16, dff=2816, tr=256, nb=12, tm=1408, tf=1408, sc=4)


def _pallas(body, **kw):
    return pl.pallas_call(body, **kw)


def _sigmoid(x):
    return 1.0 / (1.0 + jnp.exp(-x))


def _silu(x):
    return x * _sigmoid(x)


def _softplus(x):
    return jnp.maximum(x, 0.0) + jnp.log(1.0 + jnp.exp(-jnp.abs(x)))


def _raw_dot(a, b, ta, tb, hi):
    if not hi:
        a = a.astype(BF16)
        b = b.astype(BF16)
    nbatch = a.ndim - 2
    ca = a.ndim - 2 if ta else a.ndim - 1
    cb = b.ndim - 1 if tb else b.ndim - 2
    batch = tuple(range(nbatch))
    return lax.dot_general(a, b, (((ca,), (cb,)), (batch, batch)),
                           precision=lax.Precision.HIGHEST if hi else None,
                           preferred_element_type=F32)


@functools.partial(jax.custom_vjp, nondiff_argnums=(2, 3, 4))
def _dot_p(a, b, ta, tb, hi):
    return _raw_dot(a, b, ta, tb, hi)


def _dot(a, b, ta=False, tb=False, hi=False):
    return _dot_p(a, b, ta, tb, hi)


def _dot_fwd(a, b, ta, tb, hi):
    return _raw_dot(a, b, ta, tb, hi), (a, b)


def _dot_bwd(ta, tb, hi, res, g):
    a, b = res
    if not ta and not tb:
        da, db = _dot(g, b, False, True, hi), _dot(a, g, True, False, hi)
    elif not ta and tb:
        da, db = _dot(g, b, False, False, hi), _dot(g, a, True, False, hi)
    elif ta and not tb:
        da, db = _dot(b, g, False, True, hi), _dot(a, g, False, False, hi)
    else:
        raise NotImplementedError
    return da.astype(a.dtype), db.astype(b.dtype)


_dot_p.defvjp(_dot_fwd, _dot_bwd)


def _iota2(n, m, axis):
    return lax.broadcasted_iota(jnp.int32, (n, m), axis)


def _bcast(mat, nb):
    return jnp.broadcast_to(mat[None], (nb,) + mat.shape)


def _split3(a):
    a0 = a.astype(BF16)
    r1 = a - a0.astype(F32)
    a1 = r1.astype(BF16)
    return a0, a1, (r1 - a1.astype(F32)).astype(BF16)


@functools.partial(jax.custom_vjp, nondiff_argnums=(2,))
def _dot_sel(a, e, te):
    eb = e.astype(BF16)
    p0, p1, p2 = (_raw_dot(p, eb, False, te, False) for p in _split3(a))
    return p0 + (p1 + p2)


def _dot_sel_fwd(a, e, te):
    return _dot_sel(a, e, te), e


def _dot_sel_bwd(te, e, g):
    return _dot_sel(g, e, not te), jnp.zeros_like(e)


_dot_sel.defvjp(_dot_sel_fwd, _dot_sel_bwd)


@jax.custom_vjp
def _sel_dot(e, x):
    eb = e.astype(BF16)
    p0, p1, p2 = (_raw_dot(eb, p, False, False, False) for p in _split3(x))
    return p0 + (p1 + p2)


def _sel_dot_fwd(e, x):
    return _sel_dot(e, x), e


def _sel_dot_bwd(e, g):
    eb = e.astype(BF16)
    p0, p1, p2 = (_raw_dot(eb, p, True, False, False) for p in _split3(g))
    return jnp.zeros_like(e), p0 + (p1 + p2)


_sel_dot.defvjp(_sel_dot_fwd, _sel_dot_bwd)


def _tri_inv_raw(m):
    nb = m.shape[0]
    r, c = _iota2(CHUNK, CHUNK, 0), _iota2(CHUNK, CHUNK, 1)
    t = _bcast((r == c).astype(F32), nb)
    b = 1
    while b < CHUNK:
        sh = b.bit_length() - 1
        off = ((r >> (sh + 1)) == (c >> (sh + 1))) & ((r >> sh) != (c >> sh)) & (r > c)
        cl = jnp.where(off[None], m, 0.0)
        t = t - _raw_dot(_raw_dot(t, cl, False, False, False), t, False, False, False)
        b *= 2
    return t


@jax.custom_vjp
def _tri_inv_given(m, t):
    return t


def _tri_inv_fwd(m, t):
    return t, t


def _tri_inv_bwd(t, g):
    return -_raw_dot(_raw_dot(t, g, True, False, False), t, False, True, False), jnp.zeros_like(t)


_tri_inv_given.defvjp(_tri_inv_fwd, _tri_inv_bwd)


def _rms(h, g):
    return h * lax.rsqrt(jnp.mean(h * h, axis=-1, keepdims=True) + EPS) * g


class In(NamedTuple):
    arr: jax.Array
    spec: pl.BlockSpec
    grad: object = None
    acc: bool = False
    gshape: object = None
    gspec: object = None
    galias: object = None


def _params(grid):
    sem = ("arbitrary",) * len(grid)
    return pltpu.CompilerParams(dimension_semantics=sem, vmem_limit_bytes=VMEM_LIMIT)


def _stage_fwd(name, fn, grid, ins, out_shapes, out_specs):
    n_in = len(ins)

    def body(*refs):
        pids = tuple(pl.program_id(k) for k in range(len(grid)))
        vals = [r[...].astype(F32) for r in refs[:n_in]]
        outs = fn(pids, *vals)
        for o_ref, o in zip(refs[n_in:], outs):
            o_ref[...] = o.reshape(o_ref.shape).astype(o_ref.dtype)

    return _pallas(
        body, out_shape=out_shapes, grid=grid, in_specs=[i.spec for i in ins],
        out_specs=out_specs, name=name, compiler_params=_params(grid))(*[i.arr for i in ins])


def _stage_bwd(name, fn, grid, ins, cots):
    n_in, n_ct = len(ins), len(cots)
    didx = [k for k, i in enumerate(ins) if i.grad is not None]
    aliased = [(o, ins[k].galias) for o, k in enumerate(didx) if ins[k].galias is not None]
    n_al = len(aliased)

    def body(*refs):
        pids = tuple(pl.program_id(k) for k in range(len(grid)))
        vals = [r[...].astype(F32) for r in refs[:n_in]]
        ct_refs = refs[n_in:n_in + n_ct]
        g_refs = refs[n_in + n_ct + n_al:]

        def f(*dv):
            merged = list(vals)
            for k, v in zip(didx, dv):
                merged[k] = v
            return tuple(fn(pids, *merged))

        outs, vjp_fn = jax.vjp(f, *[vals[k].astype(F32) for k in didx])
        cts = tuple(c[...].reshape(o.shape).astype(F32) for c, o in zip(ct_refs, outs))
        grads = vjp_fn(cts)
        first = functools.reduce(jnp.logical_and, [p == 0 for p in pids])
        for k, g_ref, g in zip(didx, g_refs, grads):
            if ins[k].acc:
                @pl.when(first)
                def _(g_ref=g_ref):
                    g_ref[...] = jnp.zeros(g_ref.shape, g_ref.dtype)
                g_ref[...] += g.reshape(g_ref.shape).astype(g_ref.dtype)
            else:
                g_ref[...] = g.reshape(g_ref.shape).astype(g_ref.dtype)

    out_shapes = [jax.ShapeDtypeStruct(ins[k].gshape or ins[k].arr.shape, ins[k].grad) for k in didx]
    out_specs = [ins[k].gspec or ins[k].spec for k in didx]
    return _pallas(
        body, out_shape=out_shapes, grid=grid,
        in_specs=[i.spec for i in ins] + [c[1] for c in cots] + [ANY] * n_al, out_specs=out_specs,
        input_output_aliases={n_in + n_ct + a: o for a, (o, _) in enumerate(aliased)},
        name=name, compiler_params=_params(grid))(*[i.arr for i in ins], *[c[0] for c in cots], *[a for _, a in aliased])


def _full(arr):
    nd = arr.ndim
    return pl.BlockSpec(arr.shape, lambda *p: (0,) * nd)


def _rows(tr, width, blk=0):
    return pl.BlockSpec((tr, width), lambda i: (i, blk))


def _mm(name, a, b, *, tm, tn, tk, out_dtype=F32, add=None, comm=None):
    M, K = a.shape
    N = b.shape[1] if b.ndim == 2 else b.shape[0] * b.shape[2]
    nk = K // tk
    grid = (M // tm, N // tn, nk)
    n_in = 3 if add is not None else 2
    n_ci, n_co = (len(comm.ins), len(comm.outs)) if comm is not None else (0, 0)

    def body(*refs):
        a_ref, b_ref = refs[0], refs[1]
        add_ref = refs[2] if add is not None else None
        c_ins = refs[n_in:n_in + n_ci]
        o_ref = refs[n_in + n_ci]
        c_outs = refs[n_in + n_ci + 1:n_in + n_ci + 1 + n_co]
        scratch = refs[n_in + n_ci + 1 + n_co:]
        acc_ref = scratch[0] if nk > 1 else None
        sems = scratch[1 if nk > 1 else 0:]
        step = (pl.program_id(0) * grid[1] + pl.program_id(1)) * nk + pl.program_id(2)
        if comm is not None:
            @pl.when(step == 0)
            def _():
                comm.start(c_ins, c_outs, sems)

        part = _raw_dot(a_ref[...], b_ref[...], False, False, False)

        def finish(total):
            if add_ref is not None:
                total = total + add_ref[...]
            o_ref[...] = total.astype(o_ref.dtype)

        if nk == 1:
            finish(part)
        else:
            k = pl.program_id(2)

            @pl.when(k == 0)
            def _():
                acc_ref[...] = part

            @pl.when(k > 0)
            def _():
                acc_ref[...] += part

            @pl.when(k == nk - 1)
            def _():
                finish(acc_ref[...])

        if comm is not None:
            @pl.when(step == grid[0] * grid[1] * nk - 1)
            def _():
                comm.finish(c_ins, c_outs, sems)

    b_spec = (pl.BlockSpec((tk, tn), lambda i, j, k: (k, j)) if b.ndim == 2 else
              pl.BlockSpec((None, tk, tn), lambda i, j, k: (j, k, 0)))
    in_specs = [pl.BlockSpec((tm, tk), lambda i, j, k: (i, k)), b_spec]
    args = [a, b]
    if add is not None:
        in_specs.append(pl.BlockSpec((tm, tn), lambda i, j, k: (i, j)))
        args.append(add)
    out_shape = jax.ShapeDtypeStruct((M, N), out_dtype)
    out_spec = pl.BlockSpec((tm, tn), lambda i, j, k: (i, j))
    scratch = [pltpu.VMEM((tm, tn), F32)] if nk > 1 else []
    if comm is None:
        return _pallas(body, out_shape=out_shape, grid=grid, in_specs=in_specs, out_specs=out_spec,
                       scratch_shapes=scratch, name=name, compiler_params=_params(grid))(*args)
    res = _pallas(body, out_shape=[out_shape] + comm.outs, grid=grid, in_specs=in_specs + [ANY] * n_ci,
                  out_specs=[out_spec] + [ANY] * n_co, scratch_shapes=scratch + comm.sems, name=name,
                  compiler_params=_params(grid))(*args, *comm.ins)
    return res[0], res[1:]


def _mm_tn(name, a, b, *, tr, tka, tn, blocked=False):
    R, Ka = a.shape
    N = b.shape[1]
    nr = R // tr
    grid = (Ka // tka, N // tn, nr)
    if blocked:
        out_shape = jax.ShapeDtypeStruct((N // tn, Ka, tn), F32)
        out_spec = pl.BlockSpec((None, tka, tn), lambda i, j, r: (j, i, 0))
    else:
        out_shape = jax.ShapeDtypeStruct((Ka, N), F32)
        out_spec = pl.BlockSpec((tka, tn), lambda i, j, r: (i, j))

    def body(a_ref, b_ref, o_ref):
        r = pl.program_id(2)
        part = _raw_dot(a_ref[...], b_ref[...], True, False, False)

        @pl.when(r == 0)
        def _():
            o_ref[...] = part

        @pl.when(r > 0)
        def _():
            o_ref[...] += part

    return _pallas(
        body, out_shape=out_shape, grid=grid,
        in_specs=[pl.BlockSpec((tr, tka), lambda i, j, r: (r, i)),
                  pl.BlockSpec((tr, tn), lambda i, j, r: (r, j))],
        out_specs=out_spec, name=name, compiler_params=_params(grid))(a, b)


def _conv_fwd(name, x, xcol0, w, b, *, taps, width, tr, tc):
    R = x.shape[0]
    grid = (width // tc, R // tr)
    cb0 = xcol0 // tc
    hrows = 16 if x.dtype == BF16 else 8
    hb = tr // hrows

    def body(*refs):
        x_ref, xp_ref, w_ref = refs[:3]
        b_ref = refs[3] if b is not None else None
        o_ref = refs[-1]
        i = pl.program_id(1)
        xv = x_ref[...].astype(F32)
        prev = jnp.where(i > 0, xp_ref[...].astype(F32)[hrows - 8:, :], 0.0)
        ext = jnp.concatenate([prev, xv], axis=0)
        acc = xv * w_ref[taps - 1:taps, :]
        for s in range(1, taps):
            acc = acc + pltpu.roll(ext, s, 0)[8:, :] * w_ref[taps - 1 - s:taps - s, :]
        if b_ref is not None:
            acc = acc + b_ref[...]
        o_ref[...] = acc.astype(o_ref.dtype)

    in_specs = [pl.BlockSpec((tr, tc), lambda j, i: (i, cb0 + j)),
                pl.BlockSpec((hrows, tc), lambda j, i: (jnp.maximum(i * hb - 1, 0), cb0 + j)),
                pl.BlockSpec((taps, tc), lambda j, i: (0, j))]
    args = [x, x, w]
    if b is not None:
        in_specs.append(pl.BlockSpec((1, tc), lambda j, i: (0, j)))
        args.append(b)
    return _pallas(
        body, out_shape=jax.ShapeDtypeStruct((R, width), BF16), grid=grid, in_specs=in_specs,
        out_specs=pl.BlockSpec((tr, tc), lambda j, i: (i, j)),
        name=name, compiler_params=_params(grid))(*args)


def _conv_bwd(name, x, xcol0, w, dy, *, taps, width, tr, tc, with_bias, dx_into=None):
    R = x.shape[0]
    nr = R // tr
    grid = (width // tc, nr)
    cb0 = xcol0 // tc
    hrows = 16 if dy.dtype == BF16 else 8
    hb = tr // hrows
    n_ext = tr + 8
    n_al = 0 if dx_into is None else 1

    def body(*refs):
        x_ref, w_ref, dy_ref, dyn_ref = refs[:4]
        dx_ref, dw_ref = refs[4 + n_al], refs[5 + n_al]
        db_ref = refs[6 + n_al] if with_bias else None
        i = pl.program_id(1)
        xv = x_ref[...].astype(F32)
        dyv = dy_ref[...].astype(F32)
        nxt = dyn_ref[...].astype(F32)[:8, :]
        dext = jnp.concatenate([dyv, jnp.where(i < nr - 1, nxt, 0.0)], axis=0)
        dx = dyv * w_ref[taps - 1:taps, :]
        dws = [None] * taps
        dws[taps - 1] = jnp.sum(xv * dyv, axis=0, keepdims=True)
        for s in range(1, taps):
            ahead = pltpu.roll(dext, n_ext - s, 0)[:tr, :]
            dx = dx + ahead * w_ref[taps - 1 - s:taps - s, :]
            dws[taps - 1 - s] = jnp.sum(xv * ahead, axis=0, keepdims=True)
        dx_ref[...] = dx.astype(dx_ref.dtype)

        @pl.when(i == 0)
        def _():
            for k in range(taps):
                dw_ref[k:k + 1, :] = dws[k]
            if db_ref is not None:
                db_ref[...] = jnp.sum(dyv, axis=0, keepdims=True)

        @pl.when(i > 0)
        def _():
            for k in range(taps):
                dw_ref[k:k + 1, :] += dws[k]
            if db_ref is not None:
                db_ref[...] += jnp.sum(dyv, axis=0, keepdims=True)

    in_specs = [pl.BlockSpec((tr, tc), lambda j, i: (i, cb0 + j)),
                pl.BlockSpec((taps, tc), lambda j, i: (0, j)),
                pl.BlockSpec((tr, tc), lambda j, i: (i, j)),
                pl.BlockSpec((hrows, tc), lambda j, i: (jnp.minimum((i + 1) * hb, R // hrows - 1), j))]
    args = [x, w, dy, dy]
    if dx_into is None:
        dx_shape, dx_spec, aliases = jax.ShapeDtypeStruct((R, width), BF16), pl.BlockSpec((tr, tc), lambda j, i: (i, j)), {}
    else:
        dx_shape = jax.ShapeDtypeStruct(dx_into.shape, dx_into.dtype)
        dx_spec, aliases = pl.BlockSpec((tr, tc), lambda j, i: (i, cb0 + j)), {4: 0}
        in_specs.append(ANY)
        args.append(dx_into)
    out_shape = [dx_shape, jax.ShapeDtypeStruct((taps, width), F32)]
    out_specs = [dx_spec, pl.BlockSpec((taps, tc), lambda j, i: (0, j))]
    if with_bias:
        out_shape.append(jax.ShapeDtypeStruct((1, width), F32))
        out_specs.append(pl.BlockSpec((1, tc), lambda j, i: (0, j)))
    return _pallas(
        body, out_shape=out_shape, grid=grid, in_specs=in_specs, out_specs=out_specs, input_output_aliases=aliases,
        name=name, compiler_params=_params(grid))(*args)


def _row_mask(cfg, i, tr):
    rows = i * tr + lax.broadcasted_iota(jnp.int32, (tr, 1), 0)
    return (rows >= cfg.front).astype(F32)


def _make_rms_fn(cfg, tr, with_residual):
    def fn(pids, h, g):
        hm = h * _row_mask(cfg, pids[0], tr)
        if with_residual:
            return _rms(hm, g), hm
        return (_rms(hm, g),)
    return fn


def _make_gdn_prep_fn(cfg, tr):
    d, hg = cfg.d, cfg.hg

    def fn(pids, c, tail, alog, dtb):
        cq, ck, cv = c[:, :d], c[:, d:2 * d], c[:, 2 * d:]
        mask = _row_mask(cfg, pids[0], tr)
        j, col = _iota2(LANES, d, 0), _iota2(LANES, d, 1)
        ea = ((col >> 7) == j).astype(F32)
        eb = ((col >> 7) + hg == j).astype(F32)
        al = jnp.sum(alog, axis=0, keepdims=True)
        db = jnp.sum(dtb, axis=0, keepdims=True)
        lg = _dot_sel(-jnp.exp(al) * _softplus(tail + db) * mask, ea, False)
        beta = _dot_sel(_sigmoid(tail) * mask, eb, False)
        sq, sk, sv = _silu(cq), _silu(ck), _silu(cv)
        qs, ks = [], []
        for h in range(hg):
            sl = slice(h * GDN_DK, (h + 1) * GDN_DK)
            qh, kh = sq[:, sl], sk[:, sl]
            qs.append(qh * lax.rsqrt(jnp.sum(qh * qh, axis=-1, keepdims=True) + EPS) * (GDN_DK ** -0.5))
            ks.append(kh * lax.rsqrt(jnp.sum(kh * kh, axis=-1, keepdims=True) + EPS))
        return jnp.concatenate(qs, axis=1), jnp.concatenate(ks, axis=1), sv, beta, lg
    return fn


def _gdn_intra_fn(pids, q, k, v, bB, lB, t_saved=None):
    rows = q.shape[0]
    nb = rows // CHUNK
    q3, k3, v3, b3, l3 = [t.reshape(nb, CHUNK, GDN_DK) for t in (q, k, v, bB, lB)]
    r, c = _iota2(CHUNK, CHUNK, 0), _iota2(CHUNK, CHUNK, 1)
    tril = (r >= c)
    strict = (r > c)
    gcol = _sel_dot(_bcast(tril.astype(F32), nb), l3)
    l64 = l3[:, :, :CHUNK]
    grow = _sel_dot(jnp.ones((nb, CHUNK, CHUNK), F32), l64 * (r <= c).astype(F32)[None])
    diff = gcol[:, :, :CHUNK] - grow
    decay = jnp.where(tril[None], jnp.exp(jnp.where(tril[None], diff, 0.0)), 0.0)
    kb = k3 * b3
    m = jnp.where(strict[None], _dot(kb, k3, False, True) * decay, 0.0)
    t = _tri_inv_raw(m) if t_saved is None else _tri_inv_given(m, t_saved.reshape(nb, CHUNK, CHUNK))
    eg = jnp.exp(gcol)
    u = _dot(t, v3 * b3)
    w = _dot(t, kb * eg)
    attn = _dot(q3, k3, False, True) * decay
    qd = q3 * eg
    glast = jnp.sum(l3, axis=1, keepdims=True)
    kd = k3 * jnp.exp(glast - gcol)
    gl = jnp.exp(glast)
    outs = (u.reshape(rows, GDN_DK), w.reshape(rows, GDN_DK), attn.reshape(1, rows, CHUNK),
            qd.reshape(rows, GDN_DK), kd.reshape(rows, GDN_DK), gl.reshape(1, nb, 1, GDN_DK))
    return outs + (t.reshape(1, rows, CHUNK),) if t_saved is None else outs


def _make_rot_fn(cfg):
    hr = cfg.hr
    half = RET_DK // 2

    def fn(pids, rqk, cos, sin):
        rq, rk = rqk[:, :cfg.d], rqk[:, cfg.d:]

        def rot(t, scale):
            outs = []
            for h in range(hr):
                x1 = t[:, h * RET_DK:h * RET_DK + half]
                x2 = t[:, h * RET_DK + half:(h + 1) * RET_DK]
                outs += [(x1 * cos - x2 * sin) * scale, (x2 * cos + x1 * sin) * scale]
            return jnp.concatenate(outs, axis=1)
        return rot(rq, 1.0), rot(rk, RET_DK ** -0.5)
    return fn


def _make_mix_fn(cfg):
    hg, hr = cfg.hg, cfg.hr

    def fn(pids, oa, ob, pm, gnorm):
        d = cfg.d
        gz, rg, gate_a, gate_b = pm[:, :d], pm[:, d:2 * d], pm[:, 2 * d:3 * d], pm[:, 3 * d:]
        oas = []
        for h in range(hg):
            oh = oa[:, h * GDN_DK:(h + 1) * GDN_DK]
            oas.append(oh * lax.rsqrt(jnp.mean(oh * oh, axis=-1, keepdims=True) + EPS) * gnorm)
        ya = jnp.concatenate(oas, axis=1) * _silu(gz)
        obs = []
        for h in range(hr):
            oh = ob[:, h * RET_DK:(h + 1) * RET_DK]
            obs.append(oh * lax.rsqrt(jnp.mean(oh * oh, axis=-1, keepdims=True) + EPS))
        yb = _silu(rg) * jnp.concatenate(obs, axis=1)
        return (_sigmoid(gate_a) * ya + _sigmoid(gate_b) * yb,)
    return fn


def _act_fn(pids, u):
    f = u.shape[1] // 2
    return (_silu(u[:, :f]) * u[:, f:],)


def _gdn_step(s, u, w, a, qd, kd, gl):
    top = _dot(jnp.concatenate([w, qd], axis=0), s)
    v_new = u - top[:CHUNK]
    bot = _dot(jnp.concatenate([a, kd.T], axis=0), v_new)
    o = top[CHUNK:] + bot[:CHUNK]
    s2 = s * gl + bot[CHUNK:]
    return s2, o


def _ret_step(s, q, k, v, dm, qdc, kdc, g):
    att = _dot(q, k, False, True) * dm
    bot = _dot(jnp.concatenate([att, (k * kdc).T], axis=0), v)
    o = bot[:CHUNK] + _dot(q * qdc, s)
    s2 = s * g + bot[CHUNK:]
    return s2, o


def _gdn_scan_fwd(cfg, u, w, attn, qd, kd, gl):
    d, hg, nch, sc = cfg.d, cfg.hg, cfg.nch, cfg.sc
    nst = nch // sc

    def body(u_ref, w_ref, a_ref, qd_ref, kd_ref, gl_ref, o_ref, ss_ref, s_ref):
        @pl.when(pl.program_id(0) == 0)
        def _():
            s_ref[...] = jnp.zeros(s_ref.shape, F32)

        states = [s_ref[h] for h in range(hg)]
        for j in range(sc):
            rows = slice(j * CHUNK, (j + 1) * CHUNK)
            outs = []
            for h in range(hg):
                sl = slice(h * GDN_DK, (h + 1) * GDN_DK)
                ss_ref[j, h] = states[h]
                states[h], o = _gdn_step(states[h], u_ref[rows, sl], w_ref[rows, sl], a_ref[h, rows, :],
                                         qd_ref[rows, sl], kd_ref[rows, sl], gl_ref[h, j])
                outs.append(o)
            o_ref[rows, :] = jnp.concatenate(outs, axis=1)
        for h in range(hg):
            s_ref[h] = states[h]

    row = pl.BlockSpec((sc * CHUNK, d), lambda n: (n, 0))
    return _pallas(
        body,
        out_shape=[jax.ShapeDtypeStruct((cfg.rp, d), F32), jax.ShapeDtypeStruct((nch, hg, GDN_DK, GDN_DK), F32)],
        grid=(nst,),
        in_specs=[row, row, pl.BlockSpec((hg, sc * CHUNK, CHUNK), lambda n: (0, n, 0)), row, row,
                  pl.BlockSpec((hg, sc, 1, GDN_DK), lambda n: (0, n, 0, 0))],
        out_specs=[row, pl.BlockSpec((sc, hg, GDN_DK, GDN_DK), lambda n: (n, 0, 0, 0))],
        scratch_shapes=[pltpu.VMEM((hg, GDN_DK, GDN_DK), F32)],
        name="gdn_scan_fwd", compiler_params=_params((nst,)))(u, w, attn, qd, kd, gl)


def _gdn_scan_bwd(cfg, do, u, w, attn, qd, kd, gl, ss):
    d, hg, nch, sc = cfg.d, cfg.hg, cfg.nch, cfg.sc
    nst = nch // sc

    def body(do_ref, u_ref, w_ref, a_ref, qd_ref, kd_ref, gl_ref, ss_ref,
             du_ref, dw_ref, da_ref, dqd_ref, dkd_ref, dgl_ref, ds_ref):
        @pl.when(pl.program_id(0) == 0)
        def _():
            ds_ref[...] = jnp.zeros(ds_ref.shape, F32)

        dstates = [ds_ref[h] for h in range(hg)]
        for j in reversed(range(sc)):
            rows = slice(j * CHUNK, (j + 1) * CHUNK)
            dus, dws, dqds, dkds = [], [], [], []
            for h in range(hg):
                sl = slice(h * GDN_DK, (h + 1) * GDN_DK)
                args = (ss_ref[j, h], u_ref[rows, sl], w_ref[rows, sl], a_ref[h, rows, :], qd_ref[rows, sl],
                        kd_ref[rows, sl], gl_ref[h, j])
                _, vjp_fn = jax.vjp(_gdn_step, *args)
                dstates[h], du, dw, da, dqd, dkd, dgl = vjp_fn((dstates[h], do_ref[rows, sl]))
                da_ref[h, rows, :] = da
                dgl_ref[h, j] = dgl
                dus.append(du)
                dws.append(dw)
                dqds.append(dqd)
                dkds.append(dkd)
            du_ref[rows, :] = jnp.concatenate(dus, axis=1)
            dw_ref[rows, :] = jnp.concatenate(dws, axis=1)
            dqd_ref[rows, :] = jnp.concatenate(dqds, axis=1)
            dkd_ref[rows, :] = jnp.concatenate(dkds, axis=1)
        for h in range(hg):
            ds_ref[h] = dstates[h]

    row = pl.BlockSpec((sc * CHUNK, d), lambda n: (nst - 1 - n, 0))
    aspec = pl.BlockSpec((hg, sc * CHUNK, CHUNK), lambda n: (0, nst - 1 - n, 0))
    gspec = pl.BlockSpec((hg, sc, 1, GDN_DK), lambda n: (0, nst - 1 - n, 0, 0))
    rowshape = jax.ShapeDtypeStruct((cfg.rp, d), F32)
    return _pallas(
        body,
        out_shape=[rowshape, rowshape, jax.ShapeDtypeStruct(attn.shape, F32), rowshape, rowshape,
                   jax.ShapeDtypeStruct(gl.shape, F32)],
        grid=(nst,),
        in_specs=[row, row, row, aspec, row, row, gspec,
                  pl.BlockSpec((sc, hg, GDN_DK, GDN_DK), lambda n: (nst - 1 - n, 0, 0, 0))],
        out_specs=[row, row, aspec, row, row, gspec],
        scratch_shapes=[pltpu.VMEM((hg, GDN_DK, GDN_DK), F32)],
        name="gdn_scan_bwd", compiler_params=_params((nst,)))(do, u, w, attn, qd, kd, gl, ss)


def _ret_consts(cfg):
    hr = cfg.hr
    lg = np.log(1.0 - 2.0 ** (-5.0 - np.arange(hr, dtype=np.float64)))
    idx = np.arange(CHUNK, dtype=np.float64)
    tril = np.tril(np.ones((CHUNK, CHUNK), dtype=bool))
    dm = np.where(tril[None], np.exp((idx[:, None] - idx[None, :])[None] * lg[:, None, None]), 0.0)
    qdc = np.exp((idx[None, :] + 1.0) * lg[:, None])
    kdc = np.exp((CHUNK - 1.0 - idx[None, :]) * lg[:, None])
    gch = np.exp(CHUNK * lg)
    qdc = np.broadcast_to(qdc[:, :, None], (hr, CHUNK, RET_DK))
    kdc = np.broadcast_to(kdc[:, :, None], (hr, CHUNK, RET_DK))
    gch = np.broadcast_to(gch[:, None, None], (hr, 1, RET_DK))
    return tuple(jnp.asarray(np.ascontiguousarray(t), F32) for t in (dm, qdc, kdc, gch))


def _ret_scan_fwd(cfg, qr, kr, proj, consts):
    d, hr, nch, sc = cfg.d, cfg.hr, cfg.nch, cfg.sc
    nst = nch // sc
    dm, qdc, kdc, gch = consts

    def body(q_ref, k_ref, v_ref, dm_ref, qdc_ref, kdc_ref, g_ref, o_ref, ss_ref, s_ref):
        @pl.when(pl.program_id(0) == 0)
        def _():
            s_ref[...] = jnp.zeros(s_ref.shape, F32)

        states = [s_ref[h] for h in range(hr)]
        for j in range(sc):
            rows = slice(j * CHUNK, (j + 1) * CHUNK)
            outs = []
            for h in range(hr):
                sl = slice(h * RET_DK, (h + 1) * RET_DK)
                ss_ref[j, h] = states[h]
                states[h], o = _ret_step(states[h], q_ref[rows, sl], k_ref[rows, sl], v_ref[rows, sl], dm_ref[h],
                                         qdc_ref[h], kdc_ref[h], g_ref[h])
                outs.append(o)
            o_ref[rows, :] = jnp.concatenate(outs, axis=1)
        for h in range(hr):
            s_ref[h] = states[h]

    row = pl.BlockSpec((sc * CHUNK, d), lambda n: (n, 0))
    return _pallas(
        body,
        out_shape=[jax.ShapeDtypeStruct((cfg.rp, d), F32), jax.ShapeDtypeStruct((nch, hr, RET_DK, RET_DK), F32)],
        grid=(nst,),
        in_specs=[row, row, pl.BlockSpec((sc * CHUNK, d), lambda n: (n, RV_BLOCK)), _full(dm), _full(qdc), _full(kdc),
                  _full(gch)],
        out_specs=[row, pl.BlockSpec((sc, hr, RET_DK, RET_DK), lambda n: (n, 0, 0, 0))],
        scratch_shapes=[pltpu.VMEM((hr, RET_DK, RET_DK), F32)],
        name="ret_scan_fwd", compiler_params=_params((nst,)))(qr, kr, proj, dm, qdc, kdc, gch)


def _ret_scan_bwd(cfg, do, qr, kr, proj, consts, ss, dproj):
    d, hr, nch, sc = cfg.d, cfg.hr, cfg.nch, cfg.sc
    nst = nch // sc
    dm, qdc, kdc, gch = consts

    def body(do_ref, q_ref, k_ref, v_ref, dm_ref, qdc_ref, kdc_ref, g_ref, ss_ref, _, dq_ref, dk_ref, dv_ref, ds_ref):
        @pl.when(pl.program_id(0) == 0)
        def _():
            ds_ref[...] = jnp.zeros(ds_ref.shape, F32)

        dstates = [ds_ref[h] for h in range(hr)]
        for j in reversed(range(sc)):
            rows = slice(j * CHUNK, (j + 1) * CHUNK)
            dqs, dks, dvs = [], [], []
            for h in range(hr):
                sl = slice(h * RET_DK, (h + 1) * RET_DK)
                cs = (dm_ref[h], qdc_ref[h], kdc_ref[h], g_ref[h])
                _, vjp_fn = jax.vjp(lambda s, q, k, v, cs=cs: _ret_step(s, q, k, v, *cs),
                                    ss_ref[j, h], q_ref[rows, sl], k_ref[rows, sl], v_ref[rows, sl])
                dstates[h], dq, dk, dv = vjp_fn((dstates[h], do_ref[rows, sl]))
                dqs.append(dq)
                dks.append(dk)
                dvs.append(dv)
            dq_ref[rows, :] = jnp.concatenate(dqs, axis=1)
            dk_ref[rows, :] = jnp.concatenate(dks, axis=1)
            dv_ref[rows, :] = jnp.concatenate(dvs, axis=1).astype(dv_ref.dtype)
        for h in range(hr):
            ds_ref[h] = dstates[h]

    row = pl.BlockSpec((sc * CHUNK, d), lambda n: (nst - 1 - n, 0))
    rowshape = jax.ShapeDtypeStruct((cfg.rp, d), F32)
    vspec = pl.BlockSpec((sc * CHUNK, d), lambda n: (nst - 1 - n, RV_BLOCK))
    return _pallas(
        body,
        out_shape=[rowshape, rowshape, jax.ShapeDtypeStruct(dproj.shape, dproj.dtype)],
        grid=(nst,),
        in_specs=[row, row, row, vspec, _full(dm), _full(qdc), _full(kdc), _full(gch),
                  pl.BlockSpec((sc, hr, RET_DK, RET_DK), lambda n: (nst - 1 - n, 0, 0, 0)), ANY],
        out_specs=[row, row, vspec], input_output_aliases={9: 2},
        scratch_shapes=[pltpu.VMEM((hr, RET_DK, RET_DK), F32)],
        name="ret_scan_bwd", compiler_params=_params((nst,)))(do, qr, kr, proj, dm, qdc, kdc, gch, ss, dproj)


def _final(cfg, h2, normf, tgt):
    d, tr = cfg.d, cfg.xrow
    nr = cfg.rp // tr

    def body(h_ref, g_ref, t_ref, dh_ref, dg_ref, loss_ref):
        i = pl.program_id(0)
        y, vjp_fn = jax.vjp(_rms, h_ref[...], g_ref[...])
        err = jnp.where(i >= 1, y - t_ref[...], 0.0)
        dh, dg = vjp_fn(err * (1.0 / d))
        dh_ref[...] = dh
        part = jnp.zeros((8, LANES), F32) + 0.5 * jnp.sum(err * err) * (1.0 / d)

        @pl.when(i == 0)
        def _():
            dg_ref[...] = dg
            loss_ref[...] = part

        @pl.when(i > 0)
        def _():
            dg_ref[...] += dg
            loss_ref[...] += part

    return _pallas(
        body,
        out_shape=[jax.ShapeDtypeStruct((cfg.rp, d), F32), jax.ShapeDtypeStruct((1, d), F32),
                   jax.ShapeDtypeStruct((8, LANES), F32)],
        grid=(nr,),
        in_specs=[_rows(tr, d), _full(normf), pl.BlockSpec((tr, d), lambda i: (jnp.maximum(i - 1, 0), 0))],
        out_specs=[_rows(tr, d), pl.BlockSpec((1, d), lambda i: (0, 0)), pl.BlockSpec((8, LANES), lambda i: (0, 0))],
        name="final_loss", compiler_params=_params((nr,)))(h2, normf, tgt)


ANY = pl.BlockSpec(memory_space=pl.ANY)


def _place():
    x, y, c = lax.axis_index("x"), lax.axis_index("y"), lax.axis_index("c")
    others = [(1 - x, y), (x, 1 - y), (1 - x, 1 - y)]
    return x, y, c, others


def _row_tile(rows, cap=256):
    return max(t for t in range(16, min(rows, cap) + 1, 16) if rows % t == 0)


class Comm(NamedTuple):
    ins: list
    outs: list
    sems: list
    start: object
    finish: object


def _run_comm(name, comm):
    n_in, n_out = len(comm.ins), len(comm.outs)

    def body(*refs):
        ins, outs, sems = refs[:n_in], refs[n_in:n_in + n_out], refs[n_in + n_out:]
        comm.start(ins, outs, sems)
        comm.finish(ins, outs, sems)

    return _pallas(body, out_shape=comm.outs, in_specs=[ANY] * n_in, out_specs=[ANY] * n_out,
                   scratch_shapes=comm.sems, name=name)(*comm.ins)


def _gather_comm(ws):
    n = len(ws)
    halves = [w.shape[0] // 2 for w in ws]

    def copies(w_refs, o_refs, sems):
        send_sems, recv_sems = sems
        x, y, c, others = _place()
        me = 2 * x + y
        chips = [2 * px + py for px, py in others]

        def piece(a, chip, core):
            return o_refs[a].at[chip, pl.ds(core * halves[a], halves[a]), :]

        def copy(a, k, src, chip, core, to):
            return pltpu.make_async_remote_copy(src_ref=src, dst_ref=piece(a, chip, core), send_sem=send_sems.at[6 * a + k],
                                                recv_sem=recv_sems.at[6 * a + k], device_id=to, device_id_type=MESH)

        def first(j, a):
            return copy(a, j, w_refs[a].at[pl.ds(c * halves[a], halves[a]), :], me, c, (*others[j], c))

        def landed(j, a):
            return copy(a, j, piece(a, chips[j], c), chips[j], c, (x, y, c))

        def passed(j, a):
            return copy(a, 3 + j, piece(a, chips[j], c), chips[j], c, (x, y, 1 - c))

        def from_sibling(j, a):
            return copy(a, 3 + j, piece(a, chips[j], 1 - c), chips[j], 1 - c, (x, y, c))

        return first, landed, passed, from_sibling

    pairs = [(j, a) for j in range(3) for a in range(n)]

    def start(w_refs, o_refs, sems):
        first, _, _, _ = copies(w_refs, o_refs, sems)
        for j, a in pairs:
            first(j, a).start()

    def finish(w_refs, o_refs, sems):
        first, landed, passed, from_sibling = copies(w_refs, o_refs, sems)
        for j, a in pairs:
            landed(j, a).wait_recv()
            passed(j, a).start()
        for j, a in pairs:
            from_sibling(j, a).wait_recv()
        for j, a in pairs:
            first(j, a).wait_send()
            passed(j, a).wait_send()

    return Comm(list(ws), [jax.ShapeDtypeStruct((N_CHIPS,) + w.shape, w.dtype) for w in ws],
                [pltpu.SemaphoreType.DMA((6 * n,)), pltpu.SemaphoreType.DMA((6 * n,))], start, finish)


def _pair_exchange(name, gs):
    n = len(gs)

    def body(*refs):
        g_refs, o_refs = refs[:n], refs[n:2 * n]
        send_sems, recv_sems = refs[2 * n:]
        x, y, c, _ = _place()
        cps = []
        for a in range(n):
            half = gs[a].shape[1] // 2
            cp = pltpu.make_async_remote_copy(
                src_ref=g_refs[a].at[:, pl.ds((1 - c) * half, half), :], dst_ref=o_refs[a], send_sem=send_sems.at[a],
                recv_sem=recv_sems.at[a], device_id=(x, y, 1 - c), device_id_type=MESH)
            cp.start()
            cps.append(cp)
        for cp in cps:
            cp.wait()

    return _pallas(
        body, out_shape=[jax.ShapeDtypeStruct((N_CHIPS, g.shape[1] // 2, g.shape[2]), g.dtype) for g in gs],
        in_specs=[ANY] * n, out_specs=[ANY] * n,
        scratch_shapes=[pltpu.SemaphoreType.DMA((n,)), pltpu.SemaphoreType.DMA((n,))], name=name)(*gs)


def _pair_sum(name, g, recv, cidx):
    half, cols = recv.shape[1], recv.shape[2]
    tr = _row_tile(half)
    nblk = half // tr

    def body(c_ref, g_ref, r_ref, o_ref):
        o_ref[...] = (g_ref[...] + r_ref[...]).astype(o_ref.dtype)

    grid_spec = pltpu.PrefetchScalarGridSpec(
        num_scalar_prefetch=1, grid=(N_CHIPS, nblk),
        in_specs=[pl.BlockSpec((1, tr, cols), lambda s, i, c: (s, c[0] * nblk + i, 0)),
                  pl.BlockSpec((1, tr, cols), lambda s, i, c: (s, i, 0))],
        out_specs=pl.BlockSpec((1, tr, cols), lambda s, i, c: (s, i, 0)))
    return _pallas(
        body, out_shape=jax.ShapeDtypeStruct((N_CHIPS, half, cols), BF16), grid_spec=grid_spec,
        name=name, compiler_params=_params((N_CHIPS, nblk)))(cidx, g, recv)


def _exchange_comm(parts):
    n = len(parts)

    def copies(p_refs, o_refs, sems):
        send_sems, recv_sems = sems
        x, y, c, others = _place()
        me = 2 * x + y

        def copy(a, j, src_chip, dst_chip):
            px, py = others[j]
            return pltpu.make_async_remote_copy(
                src_ref=p_refs[a].at[src_chip], dst_ref=o_refs[a].at[dst_chip], send_sem=send_sems.at[3 * a + j],
                recv_sem=recv_sems.at[3 * a + j], device_id=(px, py, c), device_id_type=MESH)

        def send(j, a):
            return copy(a, j, 2 * others[j][0] + others[j][1], me)

        def arrival(j, a):
            return copy(a, j, me, 2 * others[j][0] + others[j][1])

        return send, arrival

    pairs = [(j, a) for j in range(3) for a in range(n)]

    def start(p_refs, o_refs, sems):
        send, _ = copies(p_refs, o_refs, sems)
        for j, a in pairs:
            send(j, a).start()

    def finish(p_refs, o_refs, sems):
        send, arrival = copies(p_refs, o_refs, sems)
        for j, a in pairs:
            arrival(j, a).wait_recv()
        for j, a in pairs:
            send(j, a).wait_send()

    return Comm(list(parts), [jax.ShapeDtypeStruct(p.shape, p.dtype) for p in parts],
                [pltpu.SemaphoreType.DMA((3 * n,)), pltpu.SemaphoreType.DMA((3 * n,))], start, finish)


def _chip_sum(name, part, slots, chip):
    half, cols = slots.shape[1], slots.shape[2]
    tr = _row_tile(half)

    def body(me_ref, p_ref, *rest):
        s_refs, o_ref = rest[:N_CHIPS], rest[N_CHIPS]
        own = p_ref[...].astype(F32)
        v = [jnp.where(me_ref[0] == k, own, s_refs[k][...].astype(F32)) for k in range(N_CHIPS)]
        o_ref[...] = ((v[0] + v[1]) + v[2]) + v[3]

    def slot_spec(k):
        return pl.BlockSpec((None, tr, cols), lambda i, me: (jnp.where(me[0] == k, (k + 1) % N_CHIPS, k), i, 0))

    grid_spec = pltpu.PrefetchScalarGridSpec(
        num_scalar_prefetch=1, grid=(half // tr,),
        in_specs=[pl.BlockSpec((None, tr, cols), lambda i, me: (me[0], i, 0))] + [slot_spec(k) for k in range(N_CHIPS)],
        out_specs=pl.BlockSpec((tr, cols), lambda i, me: (i, 0)))
    return _pallas(
        body, out_shape=jax.ShapeDtypeStruct((half, cols), F32), grid_spec=grid_spec,
        name=name, compiler_params=_params((half // tr,)))(chip, part, *([slots] * N_CHIPS))


def _pair_swap(fins):
    n = len(fins)

    def body(*refs):
        f_refs, o_refs = refs[:n], refs[n:2 * n]
        send_sems, recv_sems = refs[2 * n:]
        x, y, c, _ = _place()
        cps = [pltpu.make_async_remote_copy(src_ref=f_refs[a], dst_ref=o_refs[a], send_sem=send_sems.at[a],
                                            recv_sem=recv_sems.at[a], device_id=(x, y, 1 - c), device_id_type=MESH)
               for a in range(n)]
        for cp in cps:
            cp.start()
        for cp in cps:
            cp.wait()

    return _pallas(
        body, out_shape=[jax.ShapeDtypeStruct(f.shape, f.dtype) for f in fins], in_specs=[ANY] * n, out_specs=[ANY] * n,
        scratch_shapes=[pltpu.SemaphoreType.DMA((n,)), pltpu.SemaphoreType.DMA((n,))], name="grad_pair_swap")(*fins)


def _adamw(name, w, g_own, g_other, m, v, cidx):
    R, cols = w.shape[-2:]
    lead = (None,) * (w.ndim - 2)
    zeros = (0,) * (w.ndim - 2)
    half = R // 2
    tr = _row_tile(half, 128)
    nblk = half // tr
    c1 = 1.0 - ADAM_B1 ** ADAM_STEP
    c2 = 1.0 - ADAM_B2 ** ADAM_STEP

    def body(c_ref, w_ref, go_ref, gs_ref, m_ref, v_ref, g_ref, d_ref, nm_ref, nv_ref):
        mine = (pl.program_id(0) // nblk) == c_ref[0]
        gv = jnp.where(mine, go_ref[...], gs_ref[...])
        nm = ADAM_B1 * m_ref[...] + (1.0 - ADAM_B1) * gv
        nv = ADAM_B2 * v_ref[...] + (1.0 - ADAM_B2) * (gv * gv)
        g_ref[...] = gv
        d_ref[...] = -ADAM_LR * ((nm / c1) / (jnp.sqrt(nv / c2) + ADAM_EPS) + ADAM_WD * w_ref[...])
        nm_ref[...] = nm
        nv_ref[...] = nv

    spec = pl.BlockSpec(lead + (tr, cols), lambda i, c: zeros + (i, 0))
    hspec = pl.BlockSpec((tr, cols), lambda i, c: (i % nblk, 0))
    shape = jax.ShapeDtypeStruct(w.shape, F32)
    grid_spec = pltpu.PrefetchScalarGridSpec(num_scalar_prefetch=1, grid=(R // tr,),
                                             in_specs=[spec, hspec, hspec, spec, spec], out_specs=[spec] * 4)
    return _pallas(
        body, out_shape=[shape] * 4, grid_spec=grid_spec,
        name=name, compiler_params=_params((R // tr,)))(cidx, w, g_own, g_other, m, v)


PARAMS = (("meta", 1), ("norm1", None), ("w_in", 2), ("gdn_conv_w", 2), ("gdn_a_log", None), ("gdn_dt_bias", None),
          ("gdn_norm", None), ("w_out", 1), ("norm2", None), ("w_ffn_up", 2), ("ffn_conv_w", 2), ("ffn_conv_b", None),
          ("w_ffn_down", 1), ("norm_f", None))
BIG = ("w_in", "w_out", "w_ffn_up", "w_ffn_down")
PACK_ALIGN = 1024
PACK_ROWS_ALIGN = 32


def _pack(arrs, dtype):
    parts, total = [], 0
    for a in arrs:
        f = a.reshape(-1).astype(dtype)
        pad = (-f.shape[0]) % PACK_ALIGN
        parts.append(jnp.pad(f, (0, pad)) if pad else f)
        total += f.shape[0] + pad
    rows = total // LANES
    rpad = (-rows) % PACK_ROWS_ALIGN
    if rpad:
        parts.append(jnp.zeros((rpad * LANES,), dtype))
    return jnp.concatenate(parts).reshape(rows + rpad, LANES)


def _unpack(buf, shapes):
    flat = buf.reshape(-1)
    outs, off = [], 0
    for s in shapes:
        n = int(np.prod(s))
        outs.append(flat[off:off + n].reshape(s))
        off += n + (-n) % PACK_ALIGN
    return outs


def _split4(a, axis):
    n = a.shape[axis] // N_CHIPS
    return [lax.slice_in_dim(a, s * n, (s + 1) * n, axis=axis) for s in range(N_CHIPS)]


PROJ_ORDER = (3, 7, 8, 9, 0, 1, 2, 6, 4, 5)


def _reorder_w_in(w, cfg):
    d, hg = cfg.d, cfg.hg

    def block(k):
        off = k * d + (2 * hg if k >= 4 else 0)
        return w[:, off:off + d]

    tail = jnp.pad(w[:, 4 * d:4 * d + 2 * hg], ((0, 0), (0, LANES - 2 * hg)))
    return jnp.concatenate([block(k) for k in PROJ_ORDER] + [tail], axis=1)


def _restore_w_in(wr, cfg):
    d, hg = cfg.d, cfg.hg
    at = {k: i for i, k in enumerate(PROJ_ORDER)}
    block = lambda k: wr[:, at[k] * d:(at[k] + 1) * d]
    return jnp.concatenate([block(k) for k in range(4)] + [wr[:, 10 * d:10 * d + 2 * hg]] +
                           [block(k) for k in range(4, 10)], axis=1)


def _step(cfg, x, tgt, shard, m_shard, v_shard):
    d, hg, hr, dff, rp, tr, tm = cfg.d, cfg.hg, cfg.hr, cfg.dff, cfg.rp, cfg.tr, cfg.tm
    nrow = rp // tr
    assert cfg.tf * N_CHIPS == 2 * dff and cfg.din % N_CHIPS == 0
    cidx = lax.axis_index("c").astype(jnp.int32).reshape(1)
    chip = (2 * lax.axis_index("x") + lax.axis_index("y")).astype(jnp.int32).reshape(1)

    axis = dict(PARAMS)
    small = ("meta", "gdn_conv_w", "ffn_conv_w")
    small_shapes = [shard[n].shape for n in small]
    mine = [shard[n][0].astype(BF16) for n in BIG] + [_pack([shard[n] for n in small], F32)]

    def with_own(gathered, own):
        return [lax.dynamic_update_slice(g, w[None], (chip[0], 0, 0)) for g, w in zip(gathered, own)]

    g_in, g_small = with_own(_run_comm("weights_gather_first", _gather_comm([mine[0], mine[4]])), [mine[0], mine[4]])
    w_in_r = _reorder_w_in(jnp.concatenate([g_in[s] for s in range(N_CHIPS)], axis=1), cfg)
    per_chip = [_unpack(g_small[s], small_shapes) for s in range(N_CHIPS)]
    full = {n: jnp.concatenate([per_chip[s][k] for s in range(N_CHIPS)], axis=axis[n]) for k, n in enumerate(small)}
    meta = full["meta"]
    gconv_w = full["gdn_conv_w"][0]
    fconv_w = full["ffn_conv_w"][0]
    norm1, norm2, gnorm = shard["norm1"], shard["norm2"], shard["gdn_norm"]
    normf = shard["norm_f"].reshape(1, d)
    fconv_b = shard["ffn_conv_b"]
    alog = jnp.pad(shard["gdn_a_log"], ((0, 7), (0, LANES - hg)))
    dtb = jnp.pad(shard["gdn_dt_bias"], ((0, 7), (0, LANES - hg)))

    h0 = jnp.concatenate([jnp.zeros((cfg.front, d), F32), meta, x], axis=0)
    half = RET_DK // 2
    pos = np.arange(rp, dtype=np.float32) - np.float32(cfg.front)
    inv = (np.float32(1.0) / np.float32(ROPE_BASE) ** (np.arange(half, dtype=np.float32) / np.float32(half))).astype(np.float32)
    ang = pos[:, None] * inv[None, :]
    cos, sin = jnp.asarray(np.cos(ang), F32), jnp.asarray(np.sin(ang), F32)
    rconsts = _ret_consts(cfg)

    tr_n = 3 * tr if rp % (3 * tr) == 0 else tr
    rms_f = _make_rms_fn(cfg, tr_n, False)
    rms_b = _make_rms_fn(cfg, tr_n, True)
    rowshape = jax.ShapeDtypeStruct((rp, d), F32)
    rspec = _rows(tr, d)
    nspec = _rows(tr_n, d)

    def rms_fwd(name, h, g):
        return _stage_fwd(name, rms_f, (rp // tr_n,), [In(h, nspec), In(g, _full(g))],
                          [jax.ShapeDtypeStruct((rp, d), BF16)], [nspec])[0]

    wm = 10 * d
    w_main, w_tail = w_in_r[:, :wm], w_in_r[:, wm:]
    tn_in = 1280 if wm % 1280 == 0 else LANES
    hn1 = rms_fwd("rms1_fwd", h0, norm1)
    proj, rest = _mm("proj_fwd", hn1, w_main, tm=tm, tn=tn_in, tk=d, out_dtype=BF16, comm=_gather_comm(mine[1:4]))
    ptail = _mm("proj_tail_fwd", hn1, w_tail, tm=tm, tn=LANES, tk=d)
    g_out, g_up, g_down = with_own(rest, mine[1:4])
    w_out = g_out.reshape(d, d)
    w_up = g_up
    w_up_t = jnp.swapaxes(g_up, 1, 2).reshape(2 * dff, d)
    w_down = g_down.reshape(dff, d)
    cqkv = _conv_fwd("gdn_conv_fwd", proj, CONV_COL * d, gconv_w, None, taps=GDN_CONV, width=3 * d, tr=tr, tc=d)
    prep_fn = _make_gdn_prep_fn(cfg, tr)
    prep_ins = [In(cqkv, _rows(tr, 3 * d), BF16), In(ptail, _rows(tr, LANES), BF16),
                In(alog, _full(alog), F32, True), In(dtb, _full(dtb), F32, True)]
    qn, kn, vv, bB, lB = _stage_fwd("gdn_prep_fwd", prep_fn, (nrow,), prep_ins, [rowshape] * 5, [rspec] * 5)

    trg = cfg.nb * CHUNK
    gi_grid = (rp // trg, hg)
    hspec = pl.BlockSpec((trg, GDN_DK), lambda i, h: (i, h))
    aspec = pl.BlockSpec((1, trg, CHUNK), lambda i, h: (h, i, 0))
    gspec = pl.BlockSpec((1, cfg.nb, 1, GDN_DK), lambda i, h: (h, i, 0, 0))
    intra_ins = [In(t, hspec, F32) for t in (qn, kn, vv, bB, lB)]
    ashape = jax.ShapeDtypeStruct((hg, rp, CHUNK), F32)
    intra_shapes = [rowshape, rowshape, ashape, rowshape, rowshape, jax.ShapeDtypeStruct((hg, cfg.nch, 1, GDN_DK), F32), ashape]
    intra_specs = [hspec, hspec, aspec, hspec, hspec, gspec, aspec]
    gu, gw, gattn, gqd, gkd, ggl, gtinv = _stage_fwd("gdn_intra_fwd", _gdn_intra_fn, gi_grid, intra_ins, intra_shapes,
                                                     intra_specs)
    oa, gss = _gdn_scan_fwd(cfg, gu, gw, gattn, gqd, gkd, ggl)

    rot_fn = _make_rot_fn(cfg)

    def rot_ins(dproj=None):
        return [In(proj, _rows(tr, 2 * d, ROT_COL // 2), BF16, galias=dproj, gshape=(rp, wm)),
                In(cos, _rows(tr, half)), In(sin, _rows(tr, half))]

    qr, kr = _stage_fwd("rot_fwd", rot_fn, (nrow,), rot_ins(), [rowshape] * 2, [rspec] * 2)
    ob, rss = _ret_scan_fwd(cfg, qr, kr, proj, rconsts)

    mix_fn = _make_mix_fn(cfg)
    mix_ins = [In(oa, rspec, F32), In(ob, rspec, F32), In(proj, _rows(tr, 4 * d, MIX_COL // 4), BF16, gshape=(rp, wm)),
               In(gnorm, _full(gnorm), F32, True)]
    ymix = _stage_fwd("mix_fwd", mix_fn, (nrow,), mix_ins, [jax.ShapeDtypeStruct((rp, d), BF16)], [rspec])[0]
    h1 = _mm("out_proj_fwd", ymix, w_out, tm=tm, tn=d, tk=d, add=h0)

    hn2 = rms_fwd("rms2_fwd", h1, norm2)
    up = _mm("ffn_up_fwd", hn2, w_up, tm=tm, tn=cfg.tf, tk=d, out_dtype=BF16)
    uc = _conv_fwd("ffn_conv_fwd", up, 0, fconv_w, fconv_b, taps=FFN_CONV, width=2 * dff, tr=tr, tc=cfg.tf)
    tra = tr
    act_ins = [In(uc, _rows(tra, 2 * dff), BF16)]
    act_spec = _rows(tra, dff)
    act = _stage_fwd("ffn_act_fwd", _act_fn, (rp // tra,), act_ins, [jax.ShapeDtypeStruct((rp, dff), BF16)], [act_spec])[0]
    h2 = _mm("ffn_down_fwd", act, w_down, tm=tm, tn=d, tk=cfg.tf, add=h1)

    dh2, g_normf, loss_blk = _final(cfg, h2, normf, tgt)
    loss = lax.psum(loss_blk[0, 0], ("x", "y", "c"))

    g_w_down = _mm_tn("ffn_down_dw", act, dh2, tr=tm, tka=cfg.tf, tn=d)
    dact = _mm("ffn_down_dx", dh2, w_down.T, tm=tm, tn=cfg.tf, tk=d)
    duc, = _stage_bwd("ffn_act_bwd", _act_fn, (rp // tra,), act_ins, [(dact, act_spec)])
    dup, g_fconv_w, g_fconv_b = _conv_bwd("ffn_conv_bwd", up, 0, fconv_w, duc, taps=FFN_CONV, width=2 * dff,
                                          tr=tr, tc=cfg.tf, with_bias=True)
    g_w_up = _mm_tn("ffn_up_dw", hn2, dup, tr=tm, tka=d, tn=cfg.tf, blocked=True)

    def pair_reduce(tag, names, arrs):
        recvs = _pair_exchange("grad_pair_exchange_" + tag, arrs)
        return [_pair_sum("grad_pair_sum_" + n, g, r, cidx) for n, g, r in zip(names, arrs, recvs)]

    parts_ffn = pair_reduce("ffn", ["w_ffn_down", "w_ffn_up"], [g_w_down.reshape(N_CHIPS, dff // N_CHIPS, d), g_w_up])
    dhn2, slots_ffn = _mm("ffn_up_dx", dup, w_up_t, tm=tm, tn=d, tk=cfg.tf, comm=_exchange_comm(parts_ffn))

    def rms_bwd(name, h, g, dhn, dres):
        ins = [In(h, nspec, F32), In(g, _full(g), F32, True)]
        return _stage_bwd(name, rms_b, (rp // tr_n,), ins, [(dhn, nspec), (dres, nspec)])

    dh1, g_norm2 = rms_bwd("rms2_bwd", h1, norm2, dhn2, dh2)
    g_w_out = _mm_tn("out_proj_dw", ymix, dh1, tr=tm, tka=d, tn=d)
    dymix = _mm("out_proj_dx", dh1, w_out.T, tm=tm, tn=d, tk=d)
    doa, dob, dproj, g_gnorm = _stage_bwd("mix_bwd", mix_fn, (nrow,), mix_ins, [(dymix, rspec)])

    dqr, dkr, dproj = _ret_scan_bwd(cfg, dob, qr, kr, proj, rconsts, rss, dproj)
    dproj, = _stage_bwd("rot_bwd", rot_fn, (nrow,), rot_ins(dproj), [(dqr, rspec), (dkr, rspec)])

    dgu, dgw, dgattn, dgqd, dgkd, dggl = _gdn_scan_bwd(cfg, doa, gu, gw, gattn, gqd, gkd, ggl, gss)
    intra_cots = [(dgu, hspec), (dgw, hspec), (dgattn, aspec), (dgqd, hspec), (dgkd, hspec), (dggl, gspec)]
    dqn, dkn, dvv, dbB, dlB = _stage_bwd("gdn_intra_bwd", _gdn_intra_fn, gi_grid, intra_ins + [In(gtinv, aspec)], intra_cots)
    dcqkv, dtail, g_alog, g_dtb = _stage_bwd(
        "gdn_prep_bwd", prep_fn, (nrow,), prep_ins, [(t, rspec) for t in (dqn, dkn, dvv, dbB, dlB)])
    dproj, g_gconv_w = _conv_bwd("gdn_conv_bwd", proj, CONV_COL * d, gconv_w, dcqkv, taps=GDN_CONV, width=3 * d,
                                 tr=tr, tc=d, with_bias=False, dx_into=dproj)
    g_w_in_r = jnp.concatenate([_mm_tn("proj_dw", hn1, dproj, tr=tm, tka=d, tn=tn_in),
                                _mm_tn("proj_tail_dw", hn1, dtail, tr=tm, tka=d, tn=LANES)], axis=1)
    g_in4 = jnp.stack(_split4(_restore_w_in(g_w_in_r, cfg), 1))
    parts_mix = pair_reduce("mix", ["w_out", "w_in"], [g_w_out.reshape(N_CHIPS, d // N_CHIPS, d), g_in4])
    dhn1_tail = _mm("proj_tail_dx", dtail, w_tail.T, tm=tm, tn=d, tk=LANES)
    dhn1, slots_mix = _mm("proj_dx", dproj, w_main.T, tm=tm, tn=d, tk=tn_in, add=dhn1_tail,
                          comm=_exchange_comm(parts_mix))
    dh0, g_norm1 = rms_bwd("rms1_bwd", h0, norm1, dhn1, dh1)

    grad_x = dh0[cfg.xrow:]
    small_grads = {
        "meta": dh0[cfg.front:cfg.xrow], "norm1": g_norm1, "gdn_conv_w": g_gconv_w[None],
        "gdn_a_log": g_alog[0:1, :hg], "gdn_dt_bias": g_dtb[0:1, :hg], "gdn_norm": g_gnorm, "norm2": g_norm2,
        "ffn_conv_w": g_fconv_w[None], "ffn_conv_b": g_fconv_b, "norm_f": g_normf.reshape(d),
    }

    small_names = [n for n, _ in PARAMS if n not in BIG]
    g_small = jnp.stack([_pack([small_grads[n] if axis[n] is None else _split4(small_grads[n], axis[n])[s]
                                for n in small_names], F32) for s in range(N_CHIPS)])
    parts_small = pair_reduce("small", ["small"], [g_small])
    slots_small = _run_comm("grad_exchange_small", _exchange_comm(parts_small))
    tags = ["w_in", "w_out", "w_ffn_up", "w_ffn_down", "small"]
    parts = [parts_mix[1], parts_mix[0], parts_ffn[1], parts_ffn[0], parts_small[0]]
    slots = [slots_mix[1], slots_mix[0], slots_ffn[1], slots_ffn[0], slots_small[0]]
    fins = [_chip_sum("grad_chip_sum_" + t, p, s, chip) for t, p, s in zip(tags, parts, slots)]
    sibs = _pair_swap(fins)

    def flat2(a):
        return a.reshape(-1, a.shape[-1])

    outs = {}
    for k, t in enumerate(BIG):
        res = _adamw("adamw_" + t, flat2(shard[t]), fins[k], sibs[k], flat2(m_shard[t]), flat2(v_shard[t]), cidx)
        outs[t] = [r.reshape(shard[t].shape) for r in res]
    small_shapes_all = [shard[n].shape for n in small_names]
    pk = lambda src: _pack([src[n] for n in small_names], F32)
    res = _adamw("adamw_small", pk(shard), fins[4], sibs[4], pk(m_shard), pk(v_shard), cidx)
    for k, r in enumerate(res):
        for n, a in zip(small_names, _unpack(r, small_shapes_all)):
            outs.setdefault(n, [None] * 4)[k] = a
    names = [n for n, _ in PARAMS]
    return (loss, grad_x[None], *[outs[n][k] for k in range(4) for n in names])


def kernel(x, meta, norm1, w_in, gdn_conv_w, gdn_a_log, gdn_dt_bias, gdn_norm, w_out, norm2, w_ffn_up, ffn_conv_w, ffn_conv_b, w_ffn_down, norm_f, loss_target, m_meta, m_norm1, m_w_in, m_gdn_conv_w, m_gdn_a_log, m_gdn_dt_bias, m_gdn_norm, m_w_out, m_norm2, m_w_ffn_up, m_ffn_conv_w, m_ffn_conv_b, m_w_ffn_down, m_norm_f, v_meta, v_norm1, v_w_in, v_gdn_conv_w, v_gdn_a_log, v_gdn_dt_bias, v_gdn_norm, v_w_out, v_norm2, v_w_ffn_up, v_ffn_conv_w, v_ffn_conv_b, v_w_ffn_down, v_norm_f):
    names = [n for n, _ in PARAMS]
    shard = dict(zip(names, (meta, norm1, w_in, gdn_conv_w, gdn_a_log, gdn_dt_bias, gdn_norm, w_out, norm2, w_ffn_up,
                             ffn_conv_w, ffn_conv_b, w_ffn_down, norm_f)))
    m_shard = dict(zip(names, (m_meta, m_norm1, m_w_in, m_gdn_conv_w, m_gdn_a_log, m_gdn_dt_bias, m_gdn_norm, m_w_out,
                               m_norm2, m_w_ffn_up, m_ffn_conv_w, m_ffn_conv_b, m_w_ffn_down, m_norm_f)))
    v_shard = dict(zip(names, (v_meta, v_norm1, v_w_in, v_gdn_conv_w, v_gdn_a_log, v_gdn_dt_bias, v_gdn_norm, v_w_out,
                               v_norm2, v_w_ffn_up, v_ffn_conv_w, v_ffn_conv_b, v_w_ffn_down, v_norm_f)))
    return _step(REAL, x[0], loss_target[0], shard, m_shard, v_shard)
```

```python
import functools
from typing import NamedTuple

import numpy as np
import jax
import jax.numpy as jnp
from jax import lax
from jax.experimental import pallas as pl
from jax.experimental.pallas import tpu as pltpu

F32 = jnp.float32
BF16 = jnp.bfloat16
EPS = 1e-6
CHUNK = 64
GDN_DK = 128
RET_DK = 256
GDN_CONV = 4
FFN_CONV = 3
ROPE_BASE = 10000.0
LANES = 128
N_CHIPS = 4
ADAM_LR, ADAM_B1, ADAM_B2, ADAM_EPS, ADAM_WD, ADAM_STEP = 0.001, 0.9, 0.999, 1e-08, 0.01, 10
MIX_COL, CONV_COL, RV_BLOCK, ROT_COL, TAIL_COL = 0, 4, 7, 8, 10
MESH = pl.DeviceIdType.MESH
VMEM_LIMIT = 56 * 1024 * 1024


class Cfg(NamedTuple):
    d: int
    seq: int
    n_meta: int
    dff: int
    tr: int
    nb: int
    tm: int
    tf: int
    sc: int

    @property
    def hg(self): return self.d // GDN_DK
    @property
    def hr(self): return self.d // RET_DK
    @property
    def L(self): return self.n_meta + self.seq
    @property
    def rp(self): return -(-self.L // 256) * 256
    @property
    def front(self): return self.rp - self.L
    @property
    def xrow(self): return self.rp - self.seq
    @property
    def nch(self): return self.rp // CHUNK
    @property
    def pw(self): return 10 * self.d + LANES
    @property
    def din(self): return 10 * self.d + 2 * self.hg


REAL = Cfg(d=1024, seq=8192, n_meta=16, dff=2816, tr=256, nb=12, tm=1408, tf=1408, sc=4)


def _pallas(body, **kw):
    return pl.pallas_call(body, **kw)


def _sigmoid(x):
    return 1.0 / (1.0 + jnp.exp(-x))


def _silu(x):
    return x * _sigmoid(x)


def _softplus(x):
    return jnp.maximum(x, 0.0) + jnp.log(1.0 + jnp.exp(-jnp.abs(x)))


def _raw_dot(a, b, ta, tb, hi):
    if not hi:
        a = a.astype(BF16)
        b = b.astype(BF16)
    nbatch = a.ndim - 2
    ca = a.ndim - 2 if ta else a.ndim - 1
    cb = b.ndim - 1 if tb else b.ndim - 2
    batch = tuple(range(nbatch))
    return lax.dot_general(a, b, (((ca,), (cb,)), (batch, batch)),
                           precision=lax.Precision.HIGHEST if hi else None,
                           preferred_element_type=F32)


@functools.partial(jax.custom_vjp, nondiff_argnums=(2, 3, 4))
def _dot_p(a, b, ta, tb, hi):
    return _raw_dot(a, b, ta, tb, hi)


def _dot(a, b, ta=False, tb=False, hi=False):
    return _dot_p(a, b, ta, tb, hi)


def _dot_fwd(a, b, ta, tb, hi):
    return _raw_dot(a, b, ta, tb, hi), (a, b)


def _dot_bwd(ta, tb, hi, res, g):
    a, b = res
    if not ta and not tb:
        da, db = _dot(g, b, False, True, hi), _dot(a, g, True, False, hi)
    elif not ta and tb:
        da, db = _dot(g, b, False, False, hi), _dot(g, a, True, False, hi)
    elif ta and not tb:
        da, db = _dot(b, g, False, True, hi), _dot(a, g, False, False, hi)
    else:
        raise NotImplementedError
    return da.astype(a.dtype), db.astype(b.dtype)


_dot_p.defvjp(_dot_fwd, _dot_bwd)


def _iota2(n, m, axis):
    return lax.broadcasted_iota(jnp.int32, (n, m), axis)


def _bcast(mat, nb):
    return jnp.broadcast_to(mat[None], (nb,) + mat.shape)


def _split3(a):
    a0 = a.astype(BF16)
    r1 = a - a0.astype(F32)
    a1 = r1.astype(BF16)
    return a0, a1, (r1 - a1.astype(F32)).astype(BF16)


@functools.partial(jax.custom_vjp, nondiff_argnums=(2,))
def _dot_sel(a, e, te):
    eb = e.astype(BF16)
    p0, p1, p2 = (_raw_dot(p, eb, False, te, False) for p in _split3(a))
    return p0 + (p1 + p2)


def _dot_sel_fwd(a, e, te):
    return _dot_sel(a, e, te), e


def _dot_sel_bwd(te, e, g):
    return _dot_sel(g, e, not te), jnp.zeros_like(e)


_dot_sel.defvjp(_dot_sel_fwd, _dot_sel_bwd)


@jax.custom_vjp
def _sel_dot(e, x):
    eb = e.astype(BF16)
    p0, p1, p2 = (_raw_dot(eb, p, False, False, False) for p in _split3(x))
    return p0 + (p1 + p2)


def _sel_dot_fwd(e, x):
    return _sel_dot(e, x), e


def _sel_dot_bwd(e, g):
    eb = e.astype(BF16)
    p0, p1, p2 = (_raw_dot(eb, p, True, False, False) for p in _split3(g))
    return jnp.zeros_like(e), p0 + (p1 + p2)


_sel_dot.defvjp(_sel_dot_fwd, _sel_dot_bwd)


def _tri_inv_raw(m):
    nb = m.shape[0]
    r, c = _iota2(CHUNK, CHUNK, 0), _iota2(CHUNK, CHUNK, 1)
    t = _bcast((r == c).astype(F32), nb)
    b = 1
    while b < CHUNK:
        sh = b.bit_length() - 1
        off = ((r >> (sh + 1)) == (c >> (sh + 1))) & ((r >> sh) != (c >> sh)) & (r > c)
        cl = jnp.where(off[None], m, 0.0)
        t = t - _raw_dot(_raw_dot(t, cl, False, False, False), t, False, False, False)
        b *= 2
    return t


@jax.custom_vjp
def _tri_inv_given(m, t):
    return t


def _tri_inv_fwd(m, t):
    return t, t


def _tri_inv_bwd(t, g):
    return -_raw_dot(_raw_dot(t, g, True, False, False), t, False, True, False), jnp.zeros_like(t)


_tri_inv_given.defvjp(_tri_inv_fwd, _tri_inv_bwd)


def _rms(h, g):
    return h * lax.rsqrt(jnp.mean(h * h, axis=-1, keepdims=True) + EPS) * g


class In(NamedTuple):
    arr: jax.Array
    spec: pl.BlockSpec
    grad: object = None
    acc: bool = False
    gshape: object = None
    gspec: object = None
    galias: object = None


def _params(grid):
    sem = ("arbitrary",) * len(grid)
    return pltpu.CompilerParams(dimension_semantics=sem, vmem_limit_bytes=VMEM_LIMIT)


def _stage_fwd(name, fn, grid, ins, out_shapes, out_specs):
    n_in = len(ins)

    def body(*refs):
        pids = tuple(pl.program_id(k) for k in range(len(grid)))
        vals = [r[...].astype(F32) for r in refs[:n_in]]
        outs = fn(pids, *vals)
        for o_ref, o in zip(refs[n_in:], outs):
            o_ref[...] = o.reshape(o_ref.shape).astype(o_ref.dtype)

    return _pallas(
        body, out_shape=out_shapes, grid=grid, in_specs=[i.spec for i in ins],
        out_specs=out_specs, name=name, compiler_params=_params(grid))(*[i.arr for i in ins])


def _stage_bwd(name, fn, grid, ins, cots):
    n_in, n_ct = len(ins), len(cots)
    didx = [k for k, i in enumerate(ins) if i.grad is not None]
    aliased = [(o, ins[k].galias) for o, k in enumerate(didx) if ins[k].galias is not None]
    n_al = len(aliased)

    def body(*refs):
        pids = tuple(pl.program_id(k) for k in range(len(grid)))
        vals = [r[...].astype(F32) for r in refs[:n_in]]
        ct_refs = refs[n_in:n_in + n_ct]
        g_refs = refs[n_in + n_ct + n_al:]

        def f(*dv):
            merged = list(vals)
            for k, v in zip(didx, dv):
                merged[k] = v
            return tuple(fn(pids, *merged))

        outs, vjp_fn = jax.vjp(f, *[vals[k].astype(F32) for k in didx])
        cts = tuple(c[...].reshape(o.shape).astype(F32) for c, o in zip(ct_refs, outs))
        grads = vjp_fn(cts)
        first = functools.reduce(jnp.logical_and, [p == 0 for p in pids])
        for k, g_ref, g in zip(didx, g_refs, grads):
            if ins[k].acc:
                @pl.when(first)
                def _(g_ref=g_ref):
                    g_ref[...] = jnp.zeros(g_ref.shape, g_ref.dtype)
                g_ref[...] += g.reshape(g_ref.shape).astype(g_ref.dtype)
            else:
                g_ref[...] = g.reshape(g_ref.shape).astype(g_ref.dtype)

    out_shapes = [jax.ShapeDtypeStruct(ins[k].gshape or ins[k].arr.shape, ins[k].grad) for k in didx]
    out_specs = [ins[k].gspec or ins[k].spec for k in didx]
    return _pallas(
        body, out_shape=out_shapes, grid=grid,
        in_specs=[i.spec for i in ins] + [c[1] for c in cots] + [ANY] * n_al, out_specs=out_specs,
        input_output_aliases={n_in + n_ct + a: o for a, (o, _) in enumerate(aliased)},
        name=name, compiler_params=_params(grid))(*[i.arr for i in ins], *[c[0] for c in cots], *[a for _, a in aliased])


def _full(arr):
    nd = arr.ndim
    return pl.BlockSpec(arr.shape, lambda *p: (0,) * nd)


def _rows(tr, width, blk=0):
    return pl.BlockSpec((tr, width), lambda i: (i, blk))


def _mm(name, a, b, *, tm, tn, tk, out_dtype=F32, add=None, comm=None):
    M, K = a.shape
    N = b.shape[1] if b.ndim == 2 else b.shape[0] * b.shape[2]
    nk = K // tk
    grid = (M // tm, N // tn, nk)
    n_in = 3 if add is not None else 2
    n_ci, n_co = (len(comm.ins), len(comm.outs)) if comm is not None else (0, 0)

    def body(*refs):
        a_ref, b_ref = refs[0], refs[1]
        add_ref = refs[2] if add is not None else None
        c_ins = refs[n_in:n_in + n_ci]
        o_ref = refs[n_in + n_ci]
        c_outs = refs[n_in + n_ci + 1:n_in + n_ci + 1 + n_co]
        scratch = refs[n_in + n_ci + 1 + n_co:]
        acc_ref = scratch[0] if nk > 1 else None
        sems = scratch[1 if nk > 1 else 0:]
        step = (pl.program_id(0) * grid[1] + pl.program_id(1)) * nk + pl.program_id(2)
        if comm is not None:
            @pl.when(step == 0)
            def _():
                comm.start(c_ins, c_outs, sems)

        part = _raw_dot(a_ref[...], b_ref[...], False, False, False)

        def finish(total):
            if add_ref is not None:
                total = total + add_ref[...]
            o_ref[...] = total.astype(o_ref.dtype)

        if nk == 1:
            finish(part)
        else:
            k = pl.program_id(2)

            @pl.when(k == 0)
            def _():
                acc_ref[...] = part

            @pl.when(k > 0)
            def _():
                acc_ref[...] += part

            @pl.when(k == nk - 1)
            def _():
                finish(acc_ref[...])

        if comm is not None:
            @pl.when(step == grid[0] * grid[1] * nk - 1)
            def _():
                comm.finish(c_ins, c_outs, sems)

    b_spec = (pl.BlockSpec((tk, tn), lambda i, j, k: (k, j)) if b.ndim == 2 else
              pl.BlockSpec((None, tk, tn), lambda i, j, k: (j, k, 0)))
    in_specs = [pl.BlockSpec((tm, tk), lambda i, j, k: (i, k)), b_spec]
    args = [a, b]
    if add is not None:
        in_specs.append(pl.BlockSpec((tm, tn), lambda i, j, k: (i, j)))
        args.append(add)
    out_shape = jax.ShapeDtypeStruct((M, N), out_dtype)
    out_spec = pl.BlockSpec((tm, tn), lambda i, j, k: (i, j))
    scratch = [pltpu.VMEM((tm, tn), F32)] if nk > 1 else []
    if comm is None:
        return _pallas(body, out_shape=out_shape, grid=grid, in_specs=in_specs, out_specs=out_spec,
                       scratch_shapes=scratch, name=name, compiler_params=_params(grid))(*args)
    res = _pallas(body, out_shape=[out_shape] + comm.outs, grid=grid, in_specs=in_specs + [ANY] * n_ci,
                  out_specs=[out_spec] + [ANY] * n_co, scratch_shapes=scratch + comm.sems, name=name,
                  compiler_params=_params(grid))(*args, *comm.ins)
    return res[0], res[1:]


def _mm_tn(name, a, b, *, tr, tka, tn, blocked=False):
    R, Ka = a.shape
    N = b.shape[1]
    nr = R // tr
    grid = (Ka // tka, N // tn, nr)
    if blocked:
        out_shape = jax.ShapeDtypeStruct((N // tn, Ka, tn), F32)
        out_spec = pl.BlockSpec((None, tka, tn), lambda i, j, r: (j, i, 0))
    else:
        out_shape = jax.ShapeDtypeStruct((Ka, N), F32)
        out_spec = pl.BlockSpec((tka, tn), lambda i, j, r: (i, j))

    def body(a_ref, b_ref, o_ref):
        r = pl.program_id(2)
        part = _raw_dot(a_ref[...], b_ref[...], True, False, False)

        @pl.when(r == 0)
        def _():
            o_ref[...] = part

        @pl.when(r > 0)
        def _():
            o_ref[...] += part

    return _pallas(
        body, out_shape=out_shape, grid=grid,
        in_specs=[pl.BlockSpec((tr, tka), lambda i, j, r: (r, i)),
                  pl.BlockSpec((tr, tn), lambda i, j, r: (r, j))],
        out_specs=out_spec, name=name, compiler_params=_params(grid))(a, b)


def _conv_fwd(name, x, xcol0, w, b, *, taps, width, tr, tc):
    R = x.shape[0]
    grid = (width // tc, R // tr)
    cb0 = xcol0 // tc
    hrows = 16 if x.dtype == BF16 else 8
    hb = tr // hrows

    def body(*refs):
        x_ref, xp_ref, w_ref = refs[:3]
        b_ref = refs[3] if b is not None else None
        o_ref = refs[-1]
        i = pl.program_id(1)
        xv = x_ref[...].astype(F32)
        prev = jnp.where(i > 0, xp_ref[...].astype(F32)[hrows - 8:, :], 0.0)
        ext = jnp.concatenate([prev, xv], axis=0)
        acc = xv * w_ref[taps - 1:taps, :]
        for s in range(1, taps):
            acc = acc + pltpu.roll(ext, s, 0)[8:, :] * w_ref[taps - 1 - s:taps - s, :]
        if b_ref is not None:
            acc = acc + b_ref[...]
        o_ref[...] = acc.astype(o_ref.dtype)

    in_specs = [pl.BlockSpec((tr, tc), lambda j, i: (i, cb0 + j)),
                pl.BlockSpec((hrows, tc), lambda j, i: (jnp.maximum(i * hb - 1, 0), cb0 + j)),
                pl.BlockSpec((taps, tc), lambda j, i: (0, j))]
    args = [x, x, w]
    if b is not None:
        in_specs.append(pl.BlockSpec((1, tc), lambda j, i: (0, j)))
        args.append(b)
    return _pallas(
        body, out_shape=jax.ShapeDtypeStruct((R, width), BF16), grid=grid, in_specs=in_specs,
        out_specs=pl.BlockSpec((tr, tc), lambda j, i: (i, j)),
        name=name, compiler_params=_params(grid))(*args)


def _conv_bwd(name, x, xcol0, w, dy, *, taps, width, tr, tc, with_bias, dx_into=None):
    R = x.shape[0]
    nr = R // tr
    grid = (width // tc, nr)
    cb0 = xcol0 // tc
    hrows = 16 if dy.dtype == BF16 else 8
    hb = tr // hrows
    n_ext = tr + 8
    n_al = 0 if dx_into is None else 1

    def body(*refs):
        x_ref, w_ref, dy_ref, dyn_ref = refs[:4]
        dx_ref, dw_ref = refs[4 + n_al], refs[5 + n_al]
        db_ref = refs[6 + n_al] if with_bias else None
        i = pl.program_id(1)
        xv = x_ref[...].astype(F32)
        dyv = dy_ref[...].astype(F32)
        nxt = dyn_ref[...].astype(F32)[:8, :]
        dext = jnp.concatenate([dyv, jnp.where(i < nr - 1, nxt, 0.0)], axis=0)
        dx = dyv * w_ref[taps - 1:taps, :]
        dws = [None] * taps
        dws[taps - 1] = jnp.sum(xv * dyv, axis=0, keepdims=True)
        for s in range(1, taps):
            ahead = pltpu.roll(dext, n_ext - s, 0)[:tr, :]
            dx = dx + ahead * w_ref[taps - 1 - s:taps - s, :]
            dws[taps - 1 - s] = jnp.sum(xv * ahead, axis=0, keepdims=True)
        dx_ref[...] = dx.astype(dx_ref.dtype)

        @pl.when(i == 0)
        def _():
            for k in range(taps):
                dw_ref[k:k + 1, :] = dws[k]
            if db_ref is not None:
                db_ref[...] = jnp.sum(dyv, axis=0, keepdims=True)

        @pl.when(i > 0)
        def _():
            for k in range(taps):
                dw_ref[k:k + 1, :] += dws[k]
            if db_ref is not None:
                db_ref[...] += jnp.sum(dyv, axis=0, keepdims=True)

    in_specs = [pl.BlockSpec((tr, tc), lambda j, i: (i, cb0 + j)),
                pl.BlockSpec((taps, tc), lambda j, i: (0, j)),
                pl.BlockSpec((tr, tc), lambda j, i: (i, j)),
                pl.BlockSpec((hrows, tc), lambda j, i: (jnp.minimum((i + 1) * hb, R // hrows - 1), j))]
    args = [x, w, dy, dy]
    if dx_into is None:
        dx_shape, dx_spec, aliases = jax.ShapeDtypeStruct((R, width), BF16), pl.BlockSpec((tr, tc), lambda j, i: (i, j)), {}
    else:
        dx_shape = jax.ShapeDtypeStruct(dx_into.shape, dx_into.dtype)
        dx_spec, aliases = pl.BlockSpec((tr, tc), lambda j, i: (i, cb0 + j)), {4: 0}
        in_specs.append(ANY)
        args.append(dx_into)
    out_shape = [dx_shape, jax.ShapeDtypeStruct((taps, width), F32)]
    out_specs = [dx_spec, pl.BlockSpec((taps, tc), lambda j, i: (0, j))]
    if with_bias:
        out_shape.append(jax.ShapeDtypeStruct((1, width), F32))
        out_specs.append(pl.BlockSpec((1, tc), lambda j, i: (0, j)))
    return _pallas(
        body, out_shape=out_shape, grid=grid, in_specs=in_specs, out_specs=out_specs, input_output_aliases=aliases,
        name=name, compiler_params=_params(grid))(*args)


def _row_mask(cfg, i, tr):
    rows = i * tr + lax.broadcasted_iota(jnp.int32, (tr, 1), 0)
    return (rows >= cfg.front).astype(F32)


def _make_rms_fn(cfg, tr, with_residual):
    def fn(pids, h, g):
        hm = h * _row_mask(cfg, pids[0], tr)
        if with_residual:
            return _rms(hm, g), hm
        return (_rms(hm, g),)
    return fn


def _make_gdn_prep_fn(cfg, tr):
    d, hg = cfg.d, cfg.hg

    def fn(pids, c, tail, alog, dtb):
        cq, ck, cv = c[:, :d], c[:, d:2 * d], c[:, 2 * d:]
        mask = _row_mask(cfg, pids[0], tr)
        j, col = _iota2(LANES, d, 0), _iota2(LANES, d, 1)
        ea = ((col >> 7) == j).astype(F32)
        eb = ((col >> 7) + hg == j).astype(F32)
        al = jnp.sum(alog, axis=0, keepdims=True)
        db = jnp.sum(dtb, axis=0, keepdims=True)
        lg = _dot_sel(-jnp.exp(al) * _softplus(tail + db) * mask, ea, False)
        beta = _dot_sel(_sigmoid(tail) * mask, eb, False)
        sq, sk, sv = _silu(cq), _silu(ck), _silu(cv)
        qs, ks = [], []
        for h in range(hg):
            sl = slice(h * GDN_DK, (h + 1) * GDN_DK)
            qh, kh = sq[:, sl], sk[:, sl]
            qs.append(qh * lax.rsqrt(jnp.sum(qh * qh, axis=-1, keepdims=True) + EPS) * (GDN_DK ** -0.5))
            ks.append(kh * lax.rsqrt(jnp.sum(kh * kh, axis=-1, keepdims=True) + EPS))
        return jnp.concatenate(qs, axis=1), jnp.concatenate(ks, axis=1), sv, beta, lg
    return fn


def _gdn_intra_fn(pids, q, k, v, bB, lB, t_saved=None):
    rows = q.shape[0]
    nb = rows // CHUNK
    q3, k3, v3, b3, l3 = [t.reshape(nb, CHUNK, GDN_DK) for t in (q, k, v, bB, lB)]
    r, c = _iota2(CHUNK, CHUNK, 0), _iota2(CHUNK, CHUNK, 1)
    tril = (r >= c)
    strict = (r > c)
    gcol = _sel_dot(_bcast(tril.astype(F32), nb), l3)
    grow = jnp.swapaxes(gcol, 1, 2)[:, :CHUNK, :]
    diff = gcol[:, :, :CHUNK] - grow
    decay = jnp.where(tril[None], jnp.exp(jnp.where(tril[None], diff, 0.0)), 0.0)
    kb = k3 * b3
    m = jnp.where(strict[None], _dot(kb, k3, False, True) * decay, 0.0)
    t = _tri_inv_raw(m) if t_saved is None else _tri_inv_given(m, t_saved.reshape(nb, CHUNK, CHUNK))
    eg = jnp.exp(gcol)
    u = _dot(t, v3 * b3)
    w = _dot(t, kb * eg)
    attn = _dot(q3, k3, False, True) * decay
    qd = q3 * eg
    glast = jnp.sum(l3, axis=1, keepdims=True)
    kd = k3 * jnp.exp(glast - gcol)
    gl = jnp.exp(glast)
    outs = (u.reshape(rows, GDN_DK), w.reshape(rows, GDN_DK), attn.reshape(1, rows, CHUNK),
            qd.reshape(rows, GDN_DK), kd.reshape(rows, GDN_DK), gl.reshape(1, nb, 1, GDN_DK))
    return outs + (t.reshape(1, rows, CHUNK),) if t_saved is None else outs


def _make_rot_fn(cfg):
    hr = cfg.hr
    half = RET_DK // 2

    def fn(pids, rqk, cos, sin):
        rq, rk = rqk[:, :cfg.d], rqk[:, cfg.d:]

        def rot(t, scale):
            outs = []
            for h in range(hr):
                x1 = t[:, h * RET_DK:h * RET_DK + half]
                x2 = t[:, h * RET_DK + half:(h + 1) * RET_DK]
                outs += [(x1 * cos - x2 * sin) * scale, (x2 * cos + x1 * sin) * scale]
            return jnp.concatenate(outs, axis=1)
        return rot(rq, 1.0), rot(rk, RET_DK ** -0.5)
    return fn


def _make_mix_fn(cfg):
    hg, hr = cfg.hg, cfg.hr

    def fn(pids, oa, ob, pm, gnorm):
        d = cfg.d
        gz, rg, gate_a, gate_b = pm[:, :d], pm[:, d:2 * d], pm[:, 2 * d:3 * d], pm[:, 3 * d:]
        oas = []
        for h in range(hg):
            oh = oa[:, h * GDN_DK:(h + 1) * GDN_DK]
            oas.append(oh * lax.rsqrt(jnp.mean(oh * oh, axis=-1, keepdims=True) + EPS) * gnorm)
        ya = jnp.concatenate(oas, axis=1) * _silu(gz)
        obs = []
        for h in range(hr):
            oh = ob[:, h * RET_DK:(h + 1) * RET_DK]
            obs.append(oh * lax.rsqrt(jnp.mean(oh * oh, axis=-1, keepdims=True) + EPS))
        yb = _silu(rg) * jnp.concatenate(obs, axis=1)
        return (_sigmoid(gate_a) * ya + _sigmoid(gate_b) * yb,)
    return fn


def _act_fn(pids, u):
    f = u.shape[1] // 2
    return (_silu(u[:, :f]) * u[:, f:],)


def _gdn_step(s, u, w, a, qd, kd, gl):
    top = _dot(jnp.concatenate([w, qd], axis=0), s)
    v_new = u - top[:CHUNK]
    bot = _dot(jnp.concatenate([a, kd.T], axis=0), v_new)
    o = top[CHUNK:] + bot[:CHUNK]
    s2 = s * gl + bot[CHUNK:]
    return s2, o


def _ret_step(s, q, k, v, dm, qdc, kdc, g):
    att = _dot(q, k, False, True) * dm
    bot = _dot(jnp.concatenate([att, (k * kdc).T], axis=0), v)
    o = bot[:CHUNK] + _dot(q * qdc, s)
    s2 = s * g + bot[CHUNK:]
    return s2, o


def _gdn_scan_fwd(cfg, u, w, attn, qd, kd, gl):
    d, hg, nch, sc = cfg.d, cfg.hg, cfg.nch, cfg.sc
    nst = nch // sc

    def body(u_ref, w_ref, a_ref, qd_ref, kd_ref, gl_ref, o_ref, ss_ref, s_ref):
        @pl.when(pl.program_id(0) == 0)
        def _():
            s_ref[...] = jnp.zeros(s_ref.shape, F32)

        states = [s_ref[h] for h in range(hg)]
        for j in range(sc):
            rows = slice(j * CHUNK, (j + 1) * CHUNK)
            outs = []
            for h in range(hg):
                sl = slice(h * GDN_DK, (h + 1) * GDN_DK)
                ss_ref[j, h] = states[h]
                states[h], o = _gdn_step(states[h], u_ref[rows, sl], w_ref[rows, sl], a_ref[h, rows, :],
                                         qd_ref[rows, sl], kd_ref[rows, sl], gl_ref[h, j])
                outs.append(o)
            o_ref[rows, :] = jnp.concatenate(outs, axis=1)
        for h in range(hg):
            s_ref[h] = states[h]

    row = pl.BlockSpec((sc * CHUNK, d), lambda n: (n, 0))
    return _pallas(
        body,
        out_shape=[jax.ShapeDtypeStruct((cfg.rp, d), F32), jax.ShapeDtypeStruct((nch, hg, GDN_DK, GDN_DK), F32)],
        grid=(nst,),
        in_specs=[row, row, pl.BlockSpec((hg, sc * CHUNK, CHUNK), lambda n: (0, n, 0)), row, row,
                  pl.BlockSpec((hg, sc, 1, GDN_DK), lambda n: (0, n, 0, 0))],
        out_specs=[row, pl.BlockSpec((sc, hg, GDN_DK, GDN_DK), lambda n: (n, 0, 0, 0))],
        scratch_shapes=[pltpu.VMEM((hg, GDN_DK, GDN_DK), F32)],
        name="gdn_scan_fwd", compiler_params=_params((nst,)))(u, w, attn, qd, kd, gl)


def _gdn_scan_bwd(cfg, do, u, w, attn, qd, kd, gl, ss):
    d, hg, nch, sc = cfg.d, cfg.hg, cfg.nch, cfg.sc
    nst = nch // sc

    def body(do_ref, u_ref, w_ref, a_ref, qd_ref, kd_ref, gl_ref, ss_ref,
             du_ref, dw_ref, da_ref, dqd_ref, dkd_ref, dgl_ref, ds_ref):
        @pl.when(pl.program_id(0) == 0)
        def _():
            ds_ref[...] = jnp.zeros(ds_ref.shape, F32)

        dstates = [ds_ref[h] for h in range(hg)]
        for j in reversed(range(sc)):
            rows = slice(j * CHUNK, (j + 1) * CHUNK)
            dus, dws, dqds, dkds = [], [], [], []
            for h in range(hg):
                sl = slice(h * GDN_DK, (h + 1) * GDN_DK)
                args = (ss_ref[j, h], u_ref[rows, sl], w_ref[rows, sl], a_ref[h, rows, :], qd_ref[rows, sl],
                        kd_ref[rows, sl], gl_ref[h, j])
                _, vjp_fn = jax.vjp(_gdn_step, *args)
                dstates[h], du, dw, da, dqd, dkd, dgl = vjp_fn((dstates[h], do_ref[rows, sl]))
                da_ref[h, rows, :] = da
                dgl_ref[h, j] = dgl
                dus.append(du)
                dws.append(dw)
                dqds.append(dqd)
                dkds.append(dkd)
            du_ref[rows, :] = jnp.concatenate(dus, axis=1)
            dw_ref[rows, :] = jnp.concatenate(dws, axis=1)
            dqd_ref[rows, :] = jnp.concatenate(dqds, axis=1)
            dkd_ref[rows, :] = jnp.concatenate(dkds, axis=1)
        for h in range(hg):
            ds_ref[h] = dstates[h]

    row = pl.BlockSpec((sc * CHUNK, d), lambda n: (nst - 1 - n, 0))
    aspec = pl.BlockSpec((hg, sc * CHUNK, CHUNK), lambda n: (0, nst - 1 - n, 0))
    gspec = pl.BlockSpec((hg, sc, 1, GDN_DK), lambda n: (0, nst - 1 - n, 0, 0))
    rowshape = jax.ShapeDtypeStruct((cfg.rp, d), F32)
    return _pallas(
        body,
        out_shape=[rowshape, rowshape, jax.ShapeDtypeStruct(attn.shape, F32), rowshape, rowshape,
                   jax.ShapeDtypeStruct(gl.shape, F32)],
        grid=(nst,),
        in_specs=[row, row, row, aspec, row, row, gspec,
                  pl.BlockSpec((sc, hg, GDN_DK, GDN_DK), lambda n: (nst - 1 - n, 0, 0, 0))],
        out_specs=[row, row, aspec, row, row, gspec],
        scratch_shapes=[pltpu.VMEM((hg, GDN_DK, GDN_DK), F32)],
        name="gdn_scan_bwd", compiler_params=_params((nst,)))(do, u, w, attn, qd, kd, gl, ss)


def _ret_consts(cfg):
    hr = cfg.hr
    lg = np.log(1.0 - 2.0 ** (-5.0 - np.arange(hr, dtype=np.float64)))
    idx = np.arange(CHUNK, dtype=np.float64)
    tril = np.tril(np.ones((CHUNK, CHUNK), dtype=bool))
    dm = np.where(tril[None], np.exp((idx[:, None] - idx[None, :])[None] * lg[:, None, None]), 0.0)
    qdc = np.exp((idx[None, :] + 1.0) * lg[:, None])
    kdc = np.exp((CHUNK - 1.0 - idx[None, :]) * lg[:, None])
    gch = np.exp(CHUNK * lg)
    qdc = np.broadcast_to(qdc[:, :, None], (hr, CHUNK, RET_DK))
    kdc = np.broadcast_to(kdc[:, :, None], (hr, CHUNK, RET_DK))
    gch = np.broadcast_to(gch[:, None, None], (hr, 1, RET_DK))
    return tuple(jnp.asarray(np.ascontiguousarray(t), F32) for t in (dm, qdc, kdc, gch))


def _ret_scan_fwd(cfg, qr, kr, proj, consts):
    d, hr, nch, sc = cfg.d, cfg.hr, cfg.nch, cfg.sc
    nst = nch // sc
    dm, qdc, kdc, gch = consts

    def body(q_ref, k_ref, v_ref, dm_ref, qdc_ref, kdc_ref, g_ref, o_ref, ss_ref, s_ref):
        @pl.when(pl.program_id(0) == 0)
        def _():
            s_ref[...] = jnp.zeros(s_ref.shape, F32)

        states = [s_ref[h] for h in range(hr)]
        for j in range(sc):
            rows = slice(j * CHUNK, (j + 1) * CHUNK)
            outs = []
            for h in range(hr):
                sl = slice(h * RET_DK, (h + 1) * RET_DK)
                ss_ref[j, h] = states[h]
                states[h], o = _ret_step(states[h], q_ref[rows, sl], k_ref[rows, sl], v_ref[rows, sl], dm_ref[h],
                                         qdc_ref[h], kdc_ref[h], g_ref[h])
                outs.append(o)
            o_ref[rows, :] = jnp.concatenate(outs, axis=1)
        for h in range(hr):
            s_ref[h] = states[h]

    row = pl.BlockSpec((sc * CHUNK, d), lambda n: (n, 0))
    return _pallas(
        body,
        out_shape=[jax.ShapeDtypeStruct((cfg.rp, d), F32), jax.ShapeDtypeStruct((nch, hr, RET_DK, RET_DK), F32)],
        grid=(nst,),
        in_specs=[row, row, pl.BlockSpec((sc * CHUNK, d), lambda n: (n, RV_BLOCK)), _full(dm), _full(qdc), _full(kdc),
                  _full(gch)],
        out_specs=[row, pl.BlockSpec((sc, hr, RET_DK, RET_DK), lambda n: (n, 0, 0, 0))],
        scratch_shapes=[pltpu.VMEM((hr, RET_DK, RET_DK), F32)],
        name="ret_scan_fwd", compiler_params=_params((nst,)))(qr, kr, proj, dm, qdc, kdc, gch)


def _ret_scan_bwd(cfg, do, qr, kr, proj, consts, ss, dproj):
    d, hr, nch, sc = cfg.d, cfg.hr, cfg.nch, cfg.sc
    nst = nch // sc
    dm, qdc, kdc, gch = consts

    def body(do_ref, q_ref, k_ref, v_ref, dm_ref, qdc_ref, kdc_ref, g_ref, ss_ref, _, dq_ref, dk_ref, dv_ref, ds_ref):
        @pl.when(pl.program_id(0) == 0)
        def _():
            ds_ref[...] = jnp.zeros(ds_ref.shape, F32)

        dstates = [ds_ref[h] for h in range(hr)]
        for j in reversed(range(sc)):
            rows = slice(j * CHUNK, (j + 1) * CHUNK)
            dqs, dks, dvs = [], [], []
            for h in range(hr):
                sl = slice(h * RET_DK, (h + 1) * RET_DK)
                cs = (dm_ref[h], qdc_ref[h], kdc_ref[h], g_ref[h])
                _, vjp_fn = jax.vjp(lambda s, q, k, v, cs=cs: _ret_step(s, q, k, v, *cs),
                                    ss_ref[j, h], q_ref[rows, sl], k_ref[rows, sl], v_ref[rows, sl])
                dstates[h], dq, dk, dv = vjp_fn((dstates[h], do_ref[rows, sl]))
                dqs.append(dq)
                dks.append(dk)
                dvs.append(dv)
            dq_ref[rows, :] = jnp.concatenate(dqs, axis=1)
            dk_ref[rows, :] = jnp.concatenate(dks, axis=1)
            dv_ref[rows, :] = jnp.concatenate(dvs, axis=1).astype(dv_ref.dtype)
        for h in range(hr):
            ds_ref[h] = dstates[h]

    row = pl.BlockSpec((sc * CHUNK, d), lambda n: (nst - 1 - n, 0))
    rowshape = jax.ShapeDtypeStruct((cfg.rp, d), F32)
    vspec = pl.BlockSpec((sc * CHUNK, d), lambda n: (nst - 1 - n, RV_BLOCK))
    return _pallas(
        body,
        out_shape=[rowshape, rowshape, jax.ShapeDtypeStruct(dproj.shape, dproj.dtype)],
        grid=(nst,),
        in_specs=[row, row, row, vspec, _full(dm), _full(qdc), _full(kdc), _full(gch),
                  pl.BlockSpec((sc, hr, RET_DK, RET_DK), lambda n: (nst - 1 - n, 0, 0, 0)), ANY],
        out_specs=[row, row, vspec], input_output_aliases={9: 2},
        scratch_shapes=[pltpu.VMEM((hr, RET_DK, RET_DK), F32)],
        name="ret_scan_bwd", compiler_params=_params((nst,)))(do, qr, kr, proj, dm, qdc, kdc, gch, ss, dproj)


def _final(cfg, h2, normf, tgt):
    d, tr = cfg.d, cfg.xrow
    nr = cfg.rp // tr

    def body(h_ref, g_ref, t_ref, dh_ref, dg_ref, loss_ref):
        i = pl.program_id(0)
        y, vjp_fn = jax.vjp(_rms, h_ref[...], g_ref[...])
        err = jnp.where(i >= 1, y - t_ref[...], 0.0)
        dh, dg = vjp_fn(err * (1.0 / d))
        dh_ref[...] = dh
        part = jnp.zeros((8, LANES), F32) + 0.5 * jnp.sum(err * err) * (1.0 / d)

        @pl.when(i == 0)
        def _():
            dg_ref[...] = dg
            loss_ref[...] = part

        @pl.when(i > 0)
        def _():
            dg_ref[...] += dg
            loss_ref[...] += part

    return _pallas(
        body,
        out_shape=[jax.ShapeDtypeStruct((cfg.rp, d), F32), jax.ShapeDtypeStruct((1, d), F32),
                   jax.ShapeDtypeStruct((8, LANES), F32)],
        grid=(nr,),
        in_specs=[_rows(tr, d), _full(normf), pl.BlockSpec((tr, d), lambda i: (jnp.maximum(i - 1, 0), 0))],
        out_specs=[_rows(tr, d), pl.BlockSpec((1, d), lambda i: (0, 0)), pl.BlockSpec((8, LANES), lambda i: (0, 0))],
        name="final_loss", compiler_params=_params((nr,)))(h2, normf, tgt)


ANY = pl.BlockSpec(memory_space=pl.ANY)


def _place():
    x, y, c = lax.axis_index("x"), lax.axis_index("y"), lax.axis_index("c")
    others = [(1 - x, y), (x, 1 - y), (1 - x, 1 - y)]
    return x, y, c, others


def _row_tile(rows, cap=256):
    return max(t for t in range(16, min(rows, cap) + 1, 16) if rows % t == 0)


class Comm(NamedTuple):
    ins: list
    outs: list
    sems: list
    start: object
    finish: object


def _run_comm(name, comm):
    n_in, n_out = len(comm.ins), len(comm.outs)

    def body(*refs):
        ins, outs, sems = refs[:n_in], refs[n_in:n_in + n_out], refs[n_in + n_out:]
        comm.start(ins, outs, sems)
        comm.finish(ins, outs, sems)

    return _pallas(body, out_shape=comm.outs, in_specs=[ANY] * n_in, out_specs=[ANY] * n_out,
                   scratch_shapes=comm.sems, name=name)(*comm.ins)


def _gather_comm(ws):
    n = len(ws)
    halves = [w.shape[0] // 2 for w in ws]

    def copies(w_refs, o_refs, sems):
        send_sems, recv_sems = sems
        x, y, c, others = _place()
        me = 2 * x + y
        chips = [2 * px + py for px, py in others]

        def piece(a, chip, core):
            return o_refs[a].at[chip, pl.ds(core * halves[a], halves[a]), :]

        def copy(a, k, src, chip, core, to):
            return pltpu.make_async_remote_copy(src_ref=src, dst_ref=piece(a, chip, core), send_sem=send_sems.at[6 * a + k],
                                                recv_sem=recv_sems.at[6 * a + k], device_id=to, device_id_type=MESH)

        def first(j, a):
            return copy(a, j, w_refs[a].at[pl.ds(c * halves[a], halves[a]), :], me, c, (*others[j], c))

        def landed(j, a):
            return copy(a, j, piece(a, chips[j], c), chips[j], c, (x, y, c))

        def passed(j, a):
            return copy(a, 3 + j, piece(a, chips[j], c), chips[j], c, (x, y, 1 - c))

        def from_sibling(j, a):
            return copy(a, 3 + j, piece(a, chips[j], 1 - c), chips[j], 1 - c, (x, y, c))

        return first, landed, passed, from_sibling

    pairs = [(j, a) for j in range(3) for a in range(n)]

    def start(w_refs, o_refs, sems):
        first, _, _, _ = copies(w_refs, o_refs, sems)
        for j, a in pairs:
            first(j, a).start()

    def finish(w_refs, o_refs, sems):
        first, landed, passed, from_sibling = copies(w_refs, o_refs, sems)
        for j, a in pairs:
            landed(j, a).wait_recv()
            passed(j, a).start()
        for j, a in pairs:
            from_sibling(j, a).wait_recv()
        for j, a in pairs:
            first(j, a).wait_send()
            passed(j, a).wait_send()

    return Comm(list(ws), [jax.ShapeDtypeStruct((N_CHIPS,) + w.shape, w.dtype) for w in ws],
                [pltpu.SemaphoreType.DMA((6 * n,)), pltpu.SemaphoreType.DMA((6 * n,))], start, finish)


def _pair_exchange(name, gs):
    n = len(gs)

    def body(*refs):
        g_refs, o_refs = refs[:n], refs[n:2 * n]
        send_sems, recv_sems = refs[2 * n:]
        x, y, c, _ = _place()
        cps = []
        for a in range(n):
            half = gs[a].shape[1] // 2
            cp = pltpu.make_async_remote_copy(
                src_ref=g_refs[a].at[:, pl.ds((1 - c) * half, half), :], dst_ref=o_refs[a], send_sem=send_sems.at[a],
                recv_sem=recv_sems.at[a], device_id=(x, y, 1 - c), device_id_type=MESH)
            cp.start()
            cps.append(cp)
        for cp in cps:
            cp.wait()

    return _pallas(
        body, out_shape=[jax.ShapeDtypeStruct((N_CHIPS, g.shape[1] // 2, g.shape[2]), g.dtype) for g in gs],
        in_specs=[ANY] * n, out_specs=[ANY] * n,
        scratch_shapes=[pltpu.SemaphoreType.DMA((n,)), pltpu.SemaphoreType.DMA((n,))], name=name)(*gs)


def _pair_sum(name, g, recv, cidx):
    half, cols = recv.shape[1], recv.shape[2]
    tr = _row_tile(half)
    nblk = half // tr

    def body(c_ref, g_ref, r_ref, o_ref):
        o_ref[...] = (g_ref[...] + r_ref[...]).astype(o_ref.dtype)

    grid_spec = pltpu.PrefetchScalarGridSpec(
        num_scalar_prefetch=1, grid=(N_CHIPS, nblk),
        in_specs=[pl.BlockSpec((1, tr, cols), lambda s, i, c: (s, c[0] * nblk + i, 0)),
                  pl.BlockSpec((1, tr, cols), lambda s, i, c: (s, i, 0))],
        out_specs=pl.BlockSpec((1, tr, cols), lambda s, i, c: (s, i, 0)))
    return _pallas(
        body, out_shape=jax.ShapeDtypeStruct((N_CHIPS, half, cols), BF16), grid_spec=grid_spec,
        name=name, compiler_params=_params((N_CHIPS, nblk)))(cidx, g, recv)


def _exchange_comm(parts):
    n = len(parts)

    def copies(p_refs, o_refs, sems):
        send_sems, recv_sems = sems
        x, y, c, others = _place()
        me = 2 * x + y

        def copy(a, j, src_chip, dst_chip):
            px, py = others[j]
            return pltpu.make_async_remote_copy(
                src_ref=p_refs[a].at[src_chip], dst_ref=o_refs[a].at[dst_chip], send_sem=send_sems.at[3 * a + j],
                recv_sem=recv_sems.at[3 * a + j], device_id=(px, py, c), device_id_type=MESH)

        def send(j, a):
            return copy(a, j, 2 * others[j][0] + others[j][1], me)

        def arrival(j, a):
            return copy(a, j, me, 2 * others[j][0] + others[j][1])

        return send, arrival

    pairs = [(j, a) for j in range(3) for a in range(n)]

    def start(p_refs, o_refs, sems):
        send, _ = copies(p_refs, o_refs, sems)
        for j, a in pairs:
            send(j, a).start()

    def finish(p_refs, o_refs, sems):
        send, arrival = copies(p_refs, o_refs, sems)
        for j, a in pairs:
            arrival(j, a).wait_recv()
        for j, a in pairs:
            send(j, a).wait_send()

    return Comm(list(parts), [jax.ShapeDtypeStruct(p.shape, p.dtype) for p in parts],
                [pltpu.SemaphoreType.DMA((3 * n,)), pltpu.SemaphoreType.DMA((3 * n,))], start, finish)


def _chip_sum(name, part, slots, chip):
    half, cols = slots.shape[1], slots.shape[2]
    tr = _row_tile(half)

    def body(me_ref, p_ref, *rest):
        s_refs, o_ref = rest[:N_CHIPS], rest[N_CHIPS]
        own = p_ref[...].astype(F32)
        v = [jnp.where(me_ref[0] == k, own, s_refs[k][...].astype(F32)) for k in range(N_CHIPS)]
        o_ref[...] = ((v[0] + v[1]) + v[2]) + v[3]

    def slot_spec(k):
        return pl.BlockSpec((None, tr, cols), lambda i, me: (jnp.where(me[0] == k, (k + 1) % N_CHIPS, k), i, 0))

    grid_spec = pltpu.PrefetchScalarGridSpec(
        num_scalar_prefetch=1, grid=(half // tr,),
        in_specs=[pl.BlockSpec((None, tr, cols), lambda i, me: (me[0], i, 0))] + [slot_spec(k) for k in range(N_CHIPS)],
        out_specs=pl.BlockSpec((tr, cols), lambda i, me: (i, 0)))
    return _pallas(
        body, out_shape=jax.ShapeDtypeStruct((half, cols), F32), grid_spec=grid_spec,
        name=name, compiler_params=_params((half // tr,)))(chip, part, *([slots] * N_CHIPS))


def _pair_swap(fins):
    n = len(fins)

    def body(*refs):
        f_refs, o_refs = refs[:n], refs[n:2 * n]
        send_sems, recv_sems = refs[2 * n:]
        x, y, c, _ = _place()
        cps = [pltpu.make_async_remote_copy(src_ref=f_refs[a], dst_ref=o_refs[a], send_sem=send_sems.at[a],
                                            recv_sem=recv_sems.at[a], device_id=(x, y, 1 - c), device_id_type=MESH)
               for a in range(n)]
        for cp in cps:
            cp.start()
        for cp in cps:
            cp.wait()

    return _pallas(
        body, out_shape=[jax.ShapeDtypeStruct(f.shape, f.dtype) for f in fins], in_specs=[ANY] * n, out_specs=[ANY] * n,
        scratch_shapes=[pltpu.SemaphoreType.DMA((n,)), pltpu.SemaphoreType.DMA((n,))], name="grad_pair_swap")(*fins)


def _adamw(name, w, g_own, g_other, m, v, cidx):
    R, cols = w.shape[-2:]
    lead = (None,) * (w.ndim - 2)
    zeros = (0,) * (w.ndim - 2)
    half = R // 2
    tr = _row_tile(half, 128)
    nblk = half // tr
    c1 = 1.0 - ADAM_B1 ** ADAM_STEP
    c2 = 1.0 - ADAM_B2 ** ADAM_STEP

    def body(c_ref, w_ref, go_ref, gs_ref, m_ref, v_ref, g_ref, d_ref, nm_ref, nv_ref):
        mine = (pl.program_id(0) // nblk) == c_ref[0]
        gv = jnp.where(mine, go_ref[...], gs_ref[...])
        nm = ADAM_B1 * m_ref[...] + (1.0 - ADAM_B1) * gv
        nv = ADAM_B2 * v_ref[...] + (1.0 - ADAM_B2) * (gv * gv)
        g_ref[...] = gv
        d_ref[...] = -ADAM_LR * ((nm / c1) / (jnp.sqrt(nv / c2) + ADAM_EPS) + ADAM_WD * w_ref[...])
        nm_ref[...] = nm
        nv_ref[...] = nv

    spec = pl.BlockSpec(lead + (tr, cols), lambda i, c: zeros + (i, 0))
    hspec = pl.BlockSpec((tr, cols), lambda i, c: (i % nblk, 0))
    shape = jax.ShapeDtypeStruct(w.shape, F32)
    grid_spec = pltpu.PrefetchScalarGridSpec(num_scalar_prefetch=1, grid=(R // tr,),
                                             in_specs=[spec, hspec, hspec, spec, spec], out_specs=[spec] * 4)
    return _pallas(
        body, out_shape=[shape] * 4, grid_spec=grid_spec,
        name=name, compiler_params=_params((R // tr,)))(cidx, w, g_own, g_other, m, v)


PARAMS = (("meta", 1), ("norm1", None), ("w_in", 2), ("gdn_conv_w", 2), ("gdn_a_log", None), ("gdn_dt_bias", None),
          ("gdn_norm", None), ("w_out", 1), ("norm2", None), ("w_ffn_up", 2), ("ffn_conv_w", 2), ("ffn_conv_b", None),
          ("w_ffn_down", 1), ("norm_f", None))
BIG = ("w_in", "w_out", "w_ffn_up", "w_ffn_down")
PACK_ALIGN = 1024
PACK_ROWS_ALIGN = 32


def _pack(arrs, dtype):
    parts, total = [], 0
    for a in arrs:
        f = a.reshape(-1).astype(dtype)
        pad = (-f.shape[0]) % PACK_ALIGN
        parts.append(jnp.pad(f, (0, pad)) if pad else f)
        total += f.shape[0] + pad
    rows = total // LANES
    rpad = (-rows) % PACK_ROWS_ALIGN
    if rpad:
        parts.append(jnp.zeros((rpad * LANES,), dtype))
    return jnp.concatenate(parts).reshape(rows + rpad, LANES)


def _unpack(buf, shapes):
    flat = buf.reshape(-1)
    outs, off = [], 0
    for s in shapes:
        n = int(np.prod(s))
        outs.append(flat[off:off + n].reshape(s))
        off += n + (-n) % PACK_ALIGN
    return outs


def _split4(a, axis):
    n = a.shape[axis] // N_CHIPS
    return [lax.slice_in_dim(a, s * n, (s + 1) * n, axis=axis) for s in range(N_CHIPS)]


PROJ_ORDER = (3, 7, 8, 9, 0, 1, 2, 6, 4, 5)


def _reorder_w_in(w, cfg):
    d, hg = cfg.d, cfg.hg

    def block(k):
        off = k * d + (2 * hg if k >= 4 else 0)
        return w[:, off:off + d]

    tail = jnp.pad(w[:, 4 * d:4 * d + 2 * hg], ((0, 0), (0, LANES - 2 * hg)))
    return jnp.concatenate([block(k) for k in PROJ_ORDER] + [tail], axis=1)


def _restore_w_in(wr, cfg):
    d, hg = cfg.d, cfg.hg
    at = {k: i for i, k in enumerate(PROJ_ORDER)}
    block = lambda k: wr[:, at[k] * d:(at[k] + 1) * d]
    return jnp.concatenate([block(k) for k in range(4)] + [wr[:, 10 * d:10 * d + 2 * hg]] +
                           [block(k) for k in range(4, 10)], axis=1)


def _step(cfg, x, tgt, shard, m_shard, v_shard):
    d, hg, hr, dff, rp, tr, tm = cfg.d, cfg.hg, cfg.hr, cfg.dff, cfg.rp, cfg.tr, cfg.tm
    nrow = rp // tr
    assert cfg.tf * N_CHIPS == 2 * dff and cfg.din % N_CHIPS == 0
    cidx = lax.axis_index("c").astype(jnp.int32).reshape(1)
    chip = (2 * lax.axis_index("x") + lax.axis_index("y")).astype(jnp.int32).reshape(1)

    axis = dict(PARAMS)
    small = ("meta", "gdn_conv_w", "ffn_conv_w")
    small_shapes = [shard[n].shape for n in small]
    mine = [shard[n][0].astype(BF16) for n in BIG] + [_pack([shard[n] for n in small], F32)]

    def with_own(gathered, own):
        return [lax.dynamic_update_slice(g, w[None], (chip[0], 0, 0)) for g, w in zip(gathered, own)]

    g_in, g_small = with_own(_run_comm("weights_gather_first", _gather_comm([mine[0], mine[4]])), [mine[0], mine[4]])
    w_in_r = _reorder_w_in(jnp.concatenate([g_in[s] for s in range(N_CHIPS)], axis=1), cfg)
    per_chip = [_unpack(g_small[s], small_shapes) for s in range(N_CHIPS)]
    full = {n: jnp.concatenate([per_chip[s][k] for s in range(N_CHIPS)], axis=axis[n]) for k, n in enumerate(small)}
    meta = full["meta"]
    gconv_w = full["gdn_conv_w"][0]
    fconv_w = full["ffn_conv_w"][0]
    norm1, norm2, gnorm = shard["norm1"], shard["norm2"], shard["gdn_norm"]
    normf = shard["norm_f"].reshape(1, d)
    fconv_b = shard["ffn_conv_b"]
    alog = jnp.pad(shard["gdn_a_log"], ((0, 7), (0, LANES - hg)))
    dtb = jnp.pad(shard["gdn_dt_bias"], ((0, 7), (0, LANES - hg)))

    h0 = jnp.concatenate([jnp.zeros((cfg.front, d), F32), meta, x], axis=0)
    half = RET_DK // 2
    pos = np.arange(rp, dtype=np.float32) - np.float32(cfg.front)
    inv = (np.float32(1.0) / np.float32(ROPE_BASE) ** (np.arange(half, dtype=np.float32) / np.float32(half))).astype(np.float32)
    ang = pos[:, None] * inv[None, :]
    cos, sin = jnp.asarray(np.cos(ang), F32), jnp.asarray(np.sin(ang), F32)
    rconsts = _ret_consts(cfg)

    tr_n = 3 * tr if rp % (3 * tr) == 0 else tr
    rms_f = _make_rms_fn(cfg, tr_n, False)
    rms_b = _make_rms_fn(cfg, tr_n, True)
    rowshape = jax.ShapeDtypeStruct((rp, d), F32)
    rspec = _rows(tr, d)
    nspec = _rows(tr_n, d)

    def rms_fwd(name, h, g):
        return _stage_fwd(name, rms_f, (rp // tr_n,), [In(h, nspec), In(g, _full(g))],
                          [jax.ShapeDtypeStruct((rp, d), BF16)], [nspec])[0]

    wm = 10 * d
    w_main, w_tail = w_in_r[:, :wm], w_in_r[:, wm:]
    tn_in = 2560 if wm % 2560 == 0 else LANES
    hn1 = rms_fwd("rms1_fwd", h0, norm1)
    proj, rest = _mm("proj_fwd", hn1, w_main, tm=tm, tn=tn_in, tk=d, out_dtype=BF16, comm=_gather_comm(mine[1:4]))
    ptail = _mm("proj_tail_fwd", hn1, w_tail, tm=tm, tn=LANES, tk=d)
    g_out, g_up, g_down = with_own(rest, mine[1:4])
    w_out = g_out.reshape(d, d)
    w_up = g_up
    w_up_t = jnp.swapaxes(g_up, 1, 2).reshape(2 * dff, d)
    w_down = g_down.reshape(dff, d)
    cqkv = _conv_fwd("gdn_conv_fwd", proj, CONV_COL * d, gconv_w, None, taps=GDN_CONV, width=3 * d, tr=tr, tc=d)
    prep_fn = _make_gdn_prep_fn(cfg, tr)
    prep_ins = [In(cqkv, _rows(tr, 3 * d), BF16), In(ptail, _rows(tr, LANES), BF16),
                In(alog, _full(alog), F32, True), In(dtb, _full(dtb), F32, True)]
    qn, kn, vv, bB, lB = _stage_fwd("gdn_prep_fwd", prep_fn, (nrow,), prep_ins, [rowshape] * 5, [rspec] * 5)

    trg = cfg.nb * CHUNK
    gi_grid = (rp // trg, hg)
    hspec = pl.BlockSpec((trg, GDN_DK), lambda i, h: (i, h))
    aspec = pl.BlockSpec((1, trg, CHUNK), lambda i, h: (h, i, 0))
    gspec = pl.BlockSpec((1, cfg.nb, 1, GDN_DK), lambda i, h: (h, i, 0, 0))
    intra_ins = [In(t, hspec, F32) for t in (qn, kn, vv, bB, lB)]
    ashape = jax.ShapeDtypeStruct((hg, rp, CHUNK), F32)
    intra_shapes = [rowshape, rowshape, ashape, rowshape, rowshape, jax.ShapeDtypeStruct((hg, cfg.nch, 1, GDN_DK), F32), ashape]
    intra_specs = [hspec, hspec, aspec, hspec, hspec, gspec, aspec]
    gu, gw, gattn, gqd, gkd, ggl, gtinv = _stage_fwd("gdn_intra_fwd", _gdn_intra_fn, gi_grid, intra_ins, intra_shapes,
                                                     intra_specs)
    oa, gss = _gdn_scan_fwd(cfg, gu, gw, gattn, gqd, gkd, ggl)

    rot_fn = _make_rot_fn(cfg)

    def rot_ins(dproj=None):
        return [In(proj, _rows(tr_n, 2 * d, ROT_COL // 2), BF16, galias=dproj, gshape=(rp, wm)),
                In(cos, _rows(tr_n, half)), In(sin, _rows(tr_n, half))]

    qr, kr = _stage_fwd("rot_fwd", rot_fn, (rp // tr_n,), rot_ins(), [rowshape] * 2, [nspec] * 2)
    ob, rss = _ret_scan_fwd(cfg, qr, kr, proj, rconsts)

    mix_fn = _make_mix_fn(cfg)
    mix_ins = [In(oa, rspec, F32), In(ob, rspec, F32), In(proj, _rows(tr, 4 * d, MIX_COL // 4), BF16, gshape=(rp, wm)),
               In(gnorm, _full(gnorm), F32, True)]
    ymix = _stage_fwd("mix_fwd", mix_fn, (nrow,), mix_ins, [jax.ShapeDtypeStruct((rp, d), BF16)], [rspec])[0]
    h1 = _mm("out_proj_fwd", ymix, w_out, tm=tm, tn=d, tk=d, add=h0)

    hn2 = rms_fwd("rms2_fwd", h1, norm2)
    up = _mm("ffn_up_fwd", hn2, w_up, tm=tm, tn=cfg.tf, tk=d, out_dtype=BF16)
    uc = _conv_fwd("ffn_conv_fwd", up, 0, fconv_w, fconv_b, taps=FFN_CONV, width=2 * dff, tr=tr, tc=cfg.tf)
    tra = tr
    act_ins = [In(uc, _rows(tra, 2 * dff), BF16)]
    act_spec = _rows(tra, dff)
    act = _stage_fwd("ffn_act_fwd", _act_fn, (rp // tra,), act_ins, [jax.ShapeDtypeStruct((rp, dff), BF16)], [act_spec])[0]
    h2 = _mm("ffn_down_fwd", act, w_down, tm=tm, tn=d, tk=cfg.tf, add=h1)

    dh2, g_normf, loss_blk = _final(cfg, h2, normf, tgt)
    loss = lax.psum(loss_blk[0, 0], ("x", "y", "c"))

    g_w_down = _mm_tn("ffn_down_dw", act, dh2, tr=tm, tka=cfg.tf, tn=d)
    dact = _mm("ffn_down_dx", dh2, w_down.T, tm=tm, tn=cfg.tf, tk=d)
    duc, = _stage_bwd("ffn_act_bwd", _act_fn, (rp // tra,), act_ins, [(dact, act_spec)])
    dup, g_fconv_w, g_fconv_b = _conv_bwd("ffn_conv_bwd", up, 0, fconv_w, duc, taps=FFN_CONV, width=2 * dff,
                                          tr=tr, tc=cfg.tf, with_bias=True)
    g_w_up = _mm_tn("ffn_up_dw", hn2, dup, tr=tm, tka=d, tn=cfg.tf, blocked=True)

    def pair_reduce(tag, names, arrs):
        recvs = _pair_exchange("grad_pair_exchange_" + tag, arrs)
        return [_pair_sum("grad_pair_sum_" + n, g, r, cidx) for n, g, r in zip(names, arrs, recvs)]

    parts_ffn = pair_reduce("ffn", ["w_ffn_down", "w_ffn_up"], [g_w_down.reshape(N_CHIPS, dff // N_CHIPS, d), g_w_up])
    dhn2, slots_ffn = _mm("ffn_up_dx", dup, w_up_t, tm=tm, tn=d, tk=cfg.tf, comm=_exchange_comm(parts_ffn))

    def rms_bwd(name, h, g, dhn, dres):
        ins = [In(h, nspec, F32), In(g, _full(g), F32, True)]
        return _stage_bwd(name, rms_b, (rp // tr_n,), ins, [(dhn, nspec), (dres, nspec)])

    dh1, g_norm2 = rms_bwd("rms2_bwd", h1, norm2, dhn2, dh2)
    g_w_out = _mm_tn("out_proj_dw", ymix, dh1, tr=tm, tka=d, tn=d)
    dymix = _mm("out_proj_dx", dh1, w_out.T, tm=tm, tn=d, tk=d)
    doa, dob, dproj, g_gnorm = _stage_bwd("mix_bwd", mix_fn, (nrow,), mix_ins, [(dymix, rspec)])

    dqr, dkr, dproj = _ret_scan_bwd(cfg, dob, qr, kr, proj, rconsts, rss, dproj)
    dproj, = _stage_bwd("rot_bwd", rot_fn, (rp // tr_n,), rot_ins(dproj), [(dqr, nspec), (dkr, nspec)])

    dgu, dgw, dgattn, dgqd, dgkd, dggl = _gdn_scan_bwd(cfg, doa, gu, gw, gattn, gqd, gkd, ggl, gss)
    intra_cots = [(dgu, hspec), (dgw, hspec), (dgattn, aspec), (dgqd, hspec), (dgkd, hspec), (dggl, gspec)]
    dqn, dkn, dvv, dbB, dlB = _stage_bwd("gdn_intra_bwd", _gdn_intra_fn, gi_grid, intra_ins + [In(gtinv, aspec)], intra_cots)
    dcqkv, dtail, g_alog, g_dtb = _stage_bwd(
        "gdn_prep_bwd", prep_fn, (nrow,), prep_ins, [(t, rspec) for t in (dqn, dkn, dvv, dbB, dlB)])
    dproj, g_gconv_w = _conv_bwd("gdn_conv_bwd", proj, CONV_COL * d, gconv_w, dcqkv, taps=GDN_CONV, width=3 * d,
                                 tr=tr, tc=d, with_bias=False, dx_into=dproj)
    g_w_in_r = jnp.concatenate([_mm_tn("proj_dw", hn1, dproj, tr=tm, tka=d, tn=tn_in),
                                _mm_tn("proj_tail_dw", hn1, dtail, tr=tm, tka=d, tn=LANES)], axis=1)
    g_in4 = jnp.stack(_split4(_restore_w_in(g_w_in_r, cfg), 1))
    parts_mix = pair_reduce("mix", ["w_out", "w_in"], [g_w_out.reshape(N_CHIPS, d // N_CHIPS, d), g_in4])
    dhn1_tail = _mm("proj_tail_dx", dtail, w_tail.T, tm=tm, tn=d, tk=LANES)
    dhn1, slots_mix = _mm("proj_dx", dproj, w_main.T, tm=tm, tn=d, tk=tn_in // 2 if tn_in > LANES else LANES, add=dhn1_tail,
                          comm=_exchange_comm(parts_mix))
    dh0, g_norm1 = rms_bwd("rms1_bwd", h0, norm1, dhn1, dh1)

    grad_x = dh0[cfg.xrow:]
    small_grads = {
        "meta": dh0[cfg.front:cfg.xrow], "norm1": g_norm1, "gdn_conv_w": g_gconv_w[None],
        "gdn_a_log": g_alog[0:1, :hg], "gdn_dt_bias": g_dtb[0:1, :hg], "gdn_norm": g_gnorm, "norm2": g_norm2,
        "ffn_conv_w": g_fconv_w[None], "ffn_conv_b": g_fconv_b, "norm_f": g_normf.reshape(d),
    }

    small_names = [n for n, _ in PARAMS if n not in BIG]
    g_small = jnp.stack([_pack([small_grads[n] if axis[n] is None else _split4(small_grads[n], axis[n])[s]
                                for n in small_names], F32) for s in range(N_CHIPS)])
    parts_small = pair_reduce("small", ["small"], [g_small])
    slots_small = _run_comm("grad_exchange_small", _exchange_comm(parts_small))
    tags = ["w_in", "w_out", "w_ffn_up", "w_ffn_down", "small"]
    parts = [parts_mix[1], parts_mix[0], parts_ffn[1], parts_ffn[0], parts_small[0]]
    slots = [slots_mix[1], slots_mix[0], slots_ffn[1], slots_ffn[0], slots_small[0]]
    fins = [_chip_sum("grad_chip_sum_" + t, p, s, chip) for t, p, s in zip(tags, parts, slots)]
    sibs = _pair_swap(fins)

    def flat2(a):
        return a.reshape(-1, a.shape[-1])

    outs = {}
    for k, t in enumerate(BIG):
        res = _adamw("adamw_" + t, flat2(shard[t]), fins[k], sibs[k], flat2(m_shard[t]), flat2(v_shard[t]), cidx)
        outs[t] = [r.reshape(shard[t].shape) for r in res]
    small_shapes_all = [shard[n].shape for n in small_names]
    pk = lambda src: _pack([src[n] for n in small_names], F32)
    res = _adamw("adamw_small", pk(shard), fins[4], sibs[4], pk(m_shard), pk(v_shard), cidx)
    for k, r in enumerate(res):
        for n, a in zip(small_names, _unpack(r, small_shapes_all)):
            outs.setdefault(n, [None] * 4)[k] = a
    names = [n for n, _ in PARAMS]
    return (loss, grad_x[None], *[outs[n][k] for k in range(4) for n in names])


def kernel(x, meta, norm1, w_in, gdn_conv_w, gdn_a_log, gdn_dt_bias, gdn_norm, w_out, norm2, w_ffn_up, ffn_conv_w, ffn_conv_b, w_ffn_down, norm_f, loss_target, m_meta, m_norm1, m_w_in, m_gdn_conv_w, m_gdn_a_log, m_gdn_dt_bias, m_gdn_norm, m_w_out, m_norm2, m_w_ffn_up, m_ffn_conv_w, m_ffn_conv_b, m_w_ffn_down, m_norm_f, v_meta, v_norm1, v_w_in, v_gdn_conv_w, v_gdn_a_log, v_gdn_dt_bias, v_gdn_norm, v_w_out, v_norm2, v_w_ffn_up, v_ffn_conv_w, v_ffn_conv_b, v_w_ffn_down, v_norm_f):
    names = [n for n, _ in PARAMS]
    shard = dict(zip(names, (meta, norm1, w_in, gdn_conv_w, gdn_a_log, gdn_dt_bias, gdn_norm, w_out, norm2, w_ffn_up,
                             ffn_conv_w, ffn_conv_b, w_ffn_down, norm_f)))
    m_shard = dict(zip(names, (m_meta, m_norm1, m_w_in, m_gdn_conv_w, m_gdn_a_log, m_gdn_dt_bias, m_gdn_norm, m_w_out,
                               m_norm2, m_w_ffn_up, m_ffn_conv_w, m_ffn_conv_b, m_w_ffn_down, m_norm_f)))
    v_shard = dict(zip(names, (v_meta, v_norm1, v_w_in, v_gdn_conv_w, v_gdn_a_log, v_gdn_dt_bias, v_gdn_norm, v_w_out,
                               v_norm2, v_w_ffn_up, v_ffn_conv_w, v_ffn_conv_b, v_w_ffn_down, v_norm_f)))
    return _step(REAL, x[0], loss_target[0], shard, m_shard, v_shard)
```

```python
import functools
from typing import NamedTuple

import numpy as np
import jax
import jax.numpy as jnp
from jax import lax
from jax.experimental import pallas as pl
from jax.experimental.pallas import tpu as pltpu

F32 = jnp.float32
BF16 = jnp.bfloat16
EPS = 1e-6
CHUNK = 64
GDN_DK = 128
RET_DK = 256
GDN_CONV = 4
FFN_CONV = 3
ROPE_BASE = 10000.0
LANES = 128
N_CHIPS = 4
ADAM_LR, ADAM_B1, ADAM_B2, ADAM_EPS, ADAM_WD, ADAM_STEP = 0.001, 0.9, 0.999, 1e-08, 0.01, 10
MIX_COL, CONV_COL, RV_BLOCK, ROT_COL = 0, 4, 7, 8
MESH = pl.DeviceIdType.MESH
VMEM_LIMIT = 56 * 1024 * 1024


class Cfg(NamedTuple):
    d: int
    seq: int
    n_meta: int
    dff: int
    tr: int
    nb: int
    tm: int
    tf: int
    sc: int

    @property
    def hg(self): return self.d // GDN_DK
    @property
    def hr(self): return self.d // RET_DK
    @property
    def L(self): return self.n_meta + self.seq
    @property
    def rp(self): return -(-self.L // 256) * 256
    @property
    def front(self): return self.rp - self.L
    @property
    def xrow(self): return self.rp - self.seq
    @property
    def nch(self): return self.rp // CHUNK
    @property
    def din(self): return 10 * self.d + 2 * self.hg


REAL = Cfg(d=1024, seq=8192, n_meta=16, dff=2816, tr=256, nb=12, tm=1408, tf=1408, sc=4)


def _pallas(body, **kw):
    return pl.pallas_call(body, **kw)


def _sigmoid(x):
    return 1.0 / (1.0 + jnp.exp(-x))


def _silu(x):
    return x * _sigmoid(x)


def _softplus(x):
    return jnp.maximum(x, 0.0) + jnp.log(1.0 + jnp.exp(-jnp.abs(x)))


def _raw_dot(a, b, ta, tb, hi):
    if not hi:
        a = a.astype(BF16)
        b = b.astype(BF16)
    nbatch = a.ndim - 2
    ca = a.ndim - 2 if ta else a.ndim - 1
    cb = b.ndim - 1 if tb else b.ndim - 2
    batch = tuple(range(nbatch))
    return lax.dot_general(a, b, (((ca,), (cb,)), (batch, batch)),
                           precision=lax.Precision.HIGHEST if hi else None,
                           preferred_element_type=F32)


@functools.partial(jax.custom_vjp, nondiff_argnums=(2, 3, 4))
def _dot_p(a, b, ta, tb, hi):
    return _raw_dot(a, b, ta, tb, hi)


def _dot(a, b, ta=False, tb=False, hi=False):
    return _dot_p(a, b, ta, tb, hi)


def _dot_fwd(a, b, ta, tb, hi):
    return _raw_dot(a, b, ta, tb, hi), (a, b)


def _dot_bwd(ta, tb, hi, res, g):
    a, b = res
    if not ta and not tb:
        da, db = _dot(g, b, False, True, hi), _dot(a, g, True, False, hi)
    elif not ta and tb:
        da, db = _dot(g, b, False, False, hi), _dot(g, a, True, False, hi)
    elif ta and not tb:
        da, db = _dot(b, g, False, True, hi), _dot(a, g, False, False, hi)
    else:
        raise NotImplementedError
    return da.astype(a.dtype), db.astype(b.dtype)


_dot_p.defvjp(_dot_fwd, _dot_bwd)


def _iota2(n, m, axis):
    return lax.broadcasted_iota(jnp.int32, (n, m), axis)


def _bcast(mat, nb):
    return jnp.broadcast_to(mat[None], (nb,) + mat.shape)


def _split3(a):
    a0 = a.astype(BF16)
    r1 = a - a0.astype(F32)
    a1 = r1.astype(BF16)
    return a0, a1, (r1 - a1.astype(F32)).astype(BF16)


@functools.partial(jax.custom_vjp, nondiff_argnums=(2,))
def _dot_sel(a, e, te):
    eb = e.astype(BF16)
    p0, p1, p2 = (_raw_dot(p, eb, False, te, False) for p in _split3(a))
    return p0 + (p1 + p2)


def _dot_sel_fwd(a, e, te):
    return _dot_sel(a, e, te), e


def _dot_sel_bwd(te, e, g):
    return _dot_sel(g, e, not te), jnp.zeros_like(e)


_dot_sel.defvjp(_dot_sel_fwd, _dot_sel_bwd)


@jax.custom_vjp
def _sel_dot(e, x):
    eb = e.astype(BF16)
    p0, p1, p2 = (_raw_dot(eb, p, False, False, False) for p in _split3(x))
    return p0 + (p1 + p2)


def _sel_dot_fwd(e, x):
    return _sel_dot(e, x), e


def _sel_dot_bwd(e, g):
    eb = e.astype(BF16)
    p0, p1, p2 = (_raw_dot(eb, p, True, False, False) for p in _split3(g))
    return jnp.zeros_like(e), p0 + (p1 + p2)


_sel_dot.defvjp(_sel_dot_fwd, _sel_dot_bwd)


def _tri_inv_raw(m):
    nb = m.shape[0]
    r, c = _iota2(CHUNK, CHUNK, 0), _iota2(CHUNK, CHUNK, 1)
    t = _bcast((r == c).astype(F32), nb)
    b = 1
    while b < CHUNK:
        sh = b.bit_length() - 1
        off = ((r >> (sh + 1)) == (c >> (sh + 1))) & ((r >> sh) != (c >> sh)) & (r > c)
        cl = jnp.where(off[None], m, 0.0)
        t = t - _raw_dot(_raw_dot(t, cl, False, False, False), t, False, False, False)
        b *= 2
    return t


@jax.custom_vjp
def _tri_inv_given(m, t):
    return t


def _tri_inv_fwd(m, t):
    return t, t


def _tri_inv_bwd(t, g):
    return -_raw_dot(_raw_dot(t, g, True, False, False), t, False, True, False), jnp.zeros_like(t)


_tri_inv_given.defvjp(_tri_inv_fwd, _tri_inv_bwd)


def _rms(h, g):
    return h * lax.rsqrt(jnp.mean(h * h, axis=-1, keepdims=True) + EPS) * g


class In(NamedTuple):
    arr: jax.Array
    spec: pl.BlockSpec
    grad: object = None
    acc: bool = False
    gshape: object = None
    gspec: object = None
    galias: object = None


def _params(grid):
    sem = ("arbitrary",) * len(grid)
    return pltpu.CompilerParams(dimension_semantics=sem, vmem_limit_bytes=VMEM_LIMIT)


def _stage_fwd(name, fn, grid, ins, out_shapes, out_specs):
    n_in = len(ins)

    def body(*refs):
        pids = tuple(pl.program_id(k) for k in range(len(grid)))
        vals = [r[...].astype(F32) for r in refs[:n_in]]
        outs = fn(pids, *vals)
        for o_ref, o in zip(refs[n_in:], outs):
            o_ref[...] = o.reshape(o_ref.shape).astype(o_ref.dtype)

    return _pallas(
        body, out_shape=out_shapes, grid=grid, in_specs=[i.spec for i in ins],
        out_specs=out_specs, name=name, compiler_params=_params(grid))(*[i.arr for i in ins])


def _stage_bwd(name, fn, grid, ins, cots):
    n_in, n_ct = len(ins), len(cots)
    didx = [k for k, i in enumerate(ins) if i.grad is not None]
    aliased = [(o, ins[k].galias) for o, k in enumerate(didx) if ins[k].galias is not None]
    n_al = len(aliased)

    def body(*refs):
        pids = tuple(pl.program_id(k) for k in range(len(grid)))
        vals = [r[...].astype(F32) for r in refs[:n_in]]
        ct_refs = refs[n_in:n_in + n_ct]
        g_refs = refs[n_in + n_ct + n_al:]

        def f(*dv):
            merged = list(vals)
            for k, v in zip(didx, dv):
                merged[k] = v
            return tuple(fn(pids, *merged))

        outs, vjp_fn = jax.vjp(f, *[vals[k].astype(F32) for k in didx])
        cts = tuple(c[...].reshape(o.shape).astype(F32) for c, o in zip(ct_refs, outs))
        grads = vjp_fn(cts)
        first = functools.reduce(jnp.logical_and, [p == 0 for p in pids])
        for k, g_ref, g in zip(didx, g_refs, grads):
            if ins[k].acc:
                @pl.when(first)
                def _(g_ref=g_ref):
                    g_ref[...] = jnp.zeros(g_ref.shape, g_ref.dtype)
                g_ref[...] += g.reshape(g_ref.shape).astype(g_ref.dtype)
            else:
                g_ref[...] = g.reshape(g_ref.shape).astype(g_ref.dtype)

    out_shapes = [jax.ShapeDtypeStruct(ins[k].gshape or ins[k].arr.shape, ins[k].grad) for k in didx]
    out_specs = [ins[k].gspec or ins[k].spec for k in didx]
    return _pallas(
        body, out_shape=out_shapes, grid=grid,
        in_specs=[i.spec for i in ins] + [c[1] for c in cots] + [ANY] * n_al, out_specs=out_specs,
        input_output_aliases={n_in + n_ct + a: o for a, (o, _) in enumerate(aliased)},
        name=name, compiler_params=_params(grid))(*[i.arr for i in ins], *[c[0] for c in cots], *[a for _, a in aliased])


def _full(arr):
    nd = arr.ndim
    return pl.BlockSpec(arr.shape, lambda *p: (0,) * nd)


def _rows(tr, width, blk=0):
    return pl.BlockSpec((tr, width), lambda i: (i, blk))


def _mm(name, a, b, *, tm, tn, tk, out_dtype=F32, add=None, comm=None):
    M, K = a.shape
    N = b.shape[1] if b.ndim == 2 else b.shape[0] * b.shape[2]
    nk = K // tk
    grid = (M // tm, N // tn, nk)
    n_in = 3 if add is not None else 2
    n_ci, n_co = (len(comm.ins), len(comm.outs)) if comm is not None else (0, 0)

    def body(*refs):
        a_ref, b_ref = refs[0], refs[1]
        add_ref = refs[2] if add is not None else None
        c_ins = refs[n_in:n_in + n_ci]
        o_ref = refs[n_in + n_ci]
        c_outs = refs[n_in + n_ci + 1:n_in + n_ci + 1 + n_co]
        scratch = refs[n_in + n_ci + 1 + n_co:]
        acc_ref = scratch[0] if nk > 1 else None
        sems = scratch[1 if nk > 1 else 0:]
        step = (pl.program_id(0) * grid[1] + pl.program_id(1)) * nk + pl.program_id(2)
        if comm is not None:
            @pl.when(step == 0)
            def _():
                comm.start(c_ins, c_outs, sems)

        part = _raw_dot(a_ref[...], b_ref[...], False, False, False)

        def finish(total):
            if add_ref is not None:
                total = total + add_ref[...]
            o_ref[...] = total.astype(o_ref.dtype)

        if nk == 1:
            finish(part)
        else:
            k = pl.program_id(2)

            @pl.when(k == 0)
            def _():
                acc_ref[...] = part

            @pl.when(k > 0)
            def _():
                acc_ref[...] += part

            @pl.when(k == nk - 1)
            def _():
                finish(acc_ref[...])

        if comm is not None:
            @pl.when(step == grid[0] * grid[1] * nk - 1)
            def _():
                comm.finish(c_ins, c_outs, sems)

    b_spec = (pl.BlockSpec((tk, tn), lambda i, j, k: (k, j)) if b.ndim == 2 else
              pl.BlockSpec((None, tk, tn), lambda i, j, k: (j, k, 0)))
    in_specs = [pl.BlockSpec((tm, tk), lambda i, j, k: (i, k)), b_spec]
    args = [a, b]
    if add is not None:
        in_specs.append(pl.BlockSpec((tm, tn), lambda i, j, k: (i, j)))
        args.append(add)
    out_shape = jax.ShapeDtypeStruct((M, N), out_dtype)
    out_spec = pl.BlockSpec((tm, tn), lambda i, j, k: (i, j))
    scratch = [pltpu.VMEM((tm, tn), F32)] if nk > 1 else []
    if comm is None:
        return _pallas(body, out_shape=out_shape, grid=grid, in_specs=in_specs, out_specs=out_spec,
                       scratch_shapes=scratch, name=name, compiler_params=_params(grid))(*args)
    res = _pallas(body, out_shape=[out_shape] + comm.outs, grid=grid, in_specs=in_specs + [ANY] * n_ci,
                  out_specs=[out_spec] + [ANY] * n_co, scratch_shapes=scratch + comm.sems, name=name,
                  compiler_params=_params(grid))(*args, *comm.ins)
    return res[0], res[1:]


def _mm_tn(name, a, b, *, tr, tka, tn, blocked=False):
    R, Ka = a.shape
    N = b.shape[1]
    nr = R // tr
    grid = (Ka // tka, N // tn, nr)
    if blocked:
        out_shape = jax.ShapeDtypeStruct((N // tn, Ka, tn), F32)
        out_spec = pl.BlockSpec((None, tka, tn), lambda i, j, r: (j, i, 0))
    else:
        out_shape = jax.ShapeDtypeStruct((Ka, N), F32)
        out_spec = pl.BlockSpec((tka, tn), lambda i, j, r: (i, j))

    def body(a_ref, b_ref, o_ref):
        r = pl.program_id(2)
        part = _raw_dot(a_ref[...], b_ref[...], True, False, False)

        @pl.when(r == 0)
        def _():
            o_ref[...] = part

        @pl.when(r > 0)
        def _():
            o_ref[...] += part

    return _pallas(
        body, out_shape=out_shape, grid=grid,
        in_specs=[pl.BlockSpec((tr, tka), lambda i, j, r: (r, i)),
                  pl.BlockSpec((tr, tn), lambda i, j, r: (r, j))],
        out_specs=out_spec, name=name, compiler_params=_params(grid))(a, b)


def _conv_fwd(name, x, xcol0, w, b, *, taps, width, tr, tc):
    R = x.shape[0]
    grid = (width // tc, R // tr)
    cb0 = xcol0 // tc
    hrows = 16 if x.dtype == BF16 else 8
    hb = tr // hrows

    def body(*refs):
        x_ref, xp_ref, w_ref = refs[:3]
        b_ref = refs[3] if b is not None else None
        o_ref = refs[-1]
        i = pl.program_id(1)
        xv = x_ref[...].astype(F32)
        prev = jnp.where(i > 0, xp_ref[...].astype(F32)[hrows - 8:, :], 0.0)
        ext = jnp.concatenate([prev, xv], axis=0)
        acc = xv * w_ref[taps - 1:taps, :]
        for s in range(1, taps):
            acc = acc + pltpu.roll(ext, s, 0)[8:, :] * w_ref[taps - 1 - s:taps - s, :]
        if b_ref is not None:
            acc = acc + b_ref[...]
        o_ref[...] = acc.astype(o_ref.dtype)

    in_specs = [pl.BlockSpec((tr, tc), lambda j, i: (i, cb0 + j)),
                pl.BlockSpec((hrows, tc), lambda j, i: (jnp.maximum(i * hb - 1, 0), cb0 + j)),
                pl.BlockSpec((taps, tc), lambda j, i: (0, j))]
    args = [x, x, w]
    if b is not None:
        in_specs.append(pl.BlockSpec((1, tc), lambda j, i: (0, j)))
        args.append(b)
    return _pallas(
        body, out_shape=jax.ShapeDtypeStruct((R, width), BF16), grid=grid, in_specs=in_specs,
        out_specs=pl.BlockSpec((tr, tc), lambda j, i: (i, j)),
        name=name, compiler_params=_params(grid))(*args)


def _conv_bwd(name, x, xcol0, w, dy, *, taps, width, tr, tc, with_bias, dx_into=None):
    R = x.shape[0]
    nr = R // tr
    grid = (width // tc, nr)
    cb0 = xcol0 // tc
    hrows = 16 if dy.dtype == BF16 else 8
    hb = tr // hrows
    n_ext = tr + 8
    n_al = 0 if dx_into is None else 1

    def body(*refs):
        x_ref, w_ref, dy_ref, dyn_ref = refs[:4]
        dx_ref, dw_ref = refs[4 + n_al], refs[5 + n_al]
        db_ref = refs[6 + n_al] if with_bias else None
        i = pl.program_id(1)
        xv = x_ref[...].astype(F32)
        dyv = dy_ref[...].astype(F32)
        nxt = dyn_ref[...].astype(F32)[:8, :]
        dext = jnp.concatenate([dyv, jnp.where(i < nr - 1, nxt, 0.0)], axis=0)
        dx = dyv * w_ref[taps - 1:taps, :]
        dws = [None] * taps
        dws[taps - 1] = jnp.sum(xv * dyv, axis=0, keepdims=True)
        for s in range(1, taps):
            ahead = pltpu.roll(dext, n_ext - s, 0)[:tr, :]
            dx = dx + ahead * w_ref[taps - 1 - s:taps - s, :]
            dws[taps - 1 - s] = jnp.sum(xv * ahead, axis=0, keepdims=True)
        dx_ref[...] = dx.astype(dx_ref.dtype)

        @pl.when(i == 0)
        def _():
            for k in range(taps):
                dw_ref[k:k + 1, :] = dws[k]
            if db_ref is not None:
                db_ref[...] = jnp.sum(dyv, axis=0, keepdims=True)

        @pl.when(i > 0)
        def _():
            for k in range(taps):
                dw_ref[k:k + 1, :] += dws[k]
            if db_ref is not None:
                db_ref[...] += jnp.sum(dyv, axis=0, keepdims=True)

    in_specs = [pl.BlockSpec((tr, tc), lambda j, i: (i, cb0 + j)),
                pl.BlockSpec((taps, tc), lambda j, i: (0, j)),
                pl.BlockSpec((tr, tc), lambda j, i: (i, j)),
                pl.BlockSpec((hrows, tc), lambda j, i: (jnp.minimum((i + 1) * hb, R // hrows - 1), j))]
    args = [x, w, dy, dy]
    if dx_into is None:
        dx_shape, dx_spec, aliases = jax.ShapeDtypeStruct((R, width), BF16), pl.BlockSpec((tr, tc), lambda j, i: (i, j)), {}
    else:
        dx_shape = jax.ShapeDtypeStruct(dx_into.shape, dx_into.dtype)
        dx_spec, aliases = pl.BlockSpec((tr, tc), lambda j, i: (i, cb0 + j)), {4: 0}
        in_specs.append(ANY)
        args.append(dx_into)
    out_shape = [dx_shape, jax.ShapeDtypeStruct((taps, width), F32)]
    out_specs = [dx_spec, pl.BlockSpec((taps, tc), lambda j, i: (0, j))]
    if with_bias:
        out_shape.append(jax.ShapeDtypeStruct((1, width), F32))
        out_specs.append(pl.BlockSpec((1, tc), lambda j, i: (0, j)))
    return _pallas(
        body, out_shape=out_shape, grid=grid, in_specs=in_specs, out_specs=out_specs, input_output_aliases=aliases,
        name=name, compiler_params=_params(grid))(*args)


def _row_mask(cfg, i, tr):
    rows = i * tr + lax.broadcasted_iota(jnp.int32, (tr, 1), 0)
    return (rows >= cfg.front).astype(F32)


def _make_rms_fn(cfg, tr, with_residual):
    def fn(pids, h, g):
        hm = h * _row_mask(cfg, pids[0], tr)
        if with_residual:
            return _rms(hm, g), hm
        return (_rms(hm, g),)
    return fn


def _make_gdn_prep_fn(cfg, tr):
    d, hg = cfg.d, cfg.hg

    def fn(pids, c, tail, alog, dtb):
        cq, ck, cv = c[:, :d], c[:, d:2 * d], c[:, 2 * d:]
        mask = _row_mask(cfg, pids[0], tr)
        j, col = _iota2(LANES, d, 0), _iota2(LANES, d, 1)
        ea = ((col >> 7) == j).astype(F32)
        eb = ((col >> 7) + hg == j).astype(F32)
        al = jnp.sum(alog, axis=0, keepdims=True)
        db = jnp.sum(dtb, axis=0, keepdims=True)
        lg = _dot_sel(-jnp.exp(al) * _softplus(tail + db) * mask, ea, False)
        beta = _dot_sel(_sigmoid(tail) * mask, eb, False)
        sq, sk, sv = _silu(cq), _silu(ck), _silu(cv)
        qs, ks = [], []
        for h in range(hg):
            sl = slice(h * GDN_DK, (h + 1) * GDN_DK)
            qh, kh = sq[:, sl], sk[:, sl]
            qs.append(qh * lax.rsqrt(jnp.sum(qh * qh, axis=-1, keepdims=True) + EPS) * (GDN_DK ** -0.5))
            ks.append(kh * lax.rsqrt(jnp.sum(kh * kh, axis=-1, keepdims=True) + EPS))
        return jnp.concatenate(qs, axis=1), jnp.concatenate(ks, axis=1), sv, beta, lg
    return fn


def _gdn_intra_fn(pids, q, k, v, bB, lB, t_saved=None):
    rows = q.shape[0]
    nb = rows // CHUNK
    q3, k3, v3, b3, l3 = [t.reshape(nb, CHUNK, GDN_DK) for t in (q, k, v, bB, lB)]
    r, c = _iota2(CHUNK, CHUNK, 0), _iota2(CHUNK, CHUNK, 1)
    tril = (r >= c)
    strict = (r > c)
    gcol = _sel_dot(_bcast(tril.astype(F32), nb), l3)
    grow = jnp.swapaxes(gcol, 1, 2)[:, :CHUNK, :]
    diff = gcol[:, :, :CHUNK] - grow
    decay = jnp.where(tril[None], jnp.exp(jnp.where(tril[None], diff, 0.0)), 0.0)
    kb = k3 * b3
    m = jnp.where(strict[None], _dot(kb, k3, False, True) * decay, 0.0)
    t = _tri_inv_raw(m) if t_saved is None else _tri_inv_given(m, t_saved.reshape(nb, CHUNK, CHUNK))
    eg = jnp.exp(gcol)
    u = _dot(t, v3 * b3)
    w = _dot(t, kb * eg)
    attn = _dot(q3, k3, False, True) * decay
    qd = q3 * eg
    glast = jnp.sum(l3, axis=1, keepdims=True)
    kd = k3 * jnp.exp(glast - gcol)
    gl = jnp.exp(glast)
    outs = (u.reshape(rows, GDN_DK), w.reshape(rows, GDN_DK), attn.reshape(1, rows, CHUNK),
            qd.reshape(rows, GDN_DK), kd.reshape(rows, GDN_DK), gl.reshape(1, nb, 1, GDN_DK))
    return outs + (t.reshape(1, rows, CHUNK),) if t_saved is None else outs


def _make_rot_fn(cfg):
    hr = cfg.hr
    half = RET_DK // 2

    def fn(pids, rqk, cos, sin):
        rq, rk = rqk[:, :cfg.d], rqk[:, cfg.d:]

        def rot(t, scale):
            outs = []
            for h in range(hr):
                x1 = t[:, h * RET_DK:h * RET_DK + half]
                x2 = t[:, h * RET_DK + half:(h + 1) * RET_DK]
                outs += [(x1 * cos - x2 * sin) * scale, (x2 * cos + x1 * sin) * scale]
            return jnp.concatenate(outs, axis=1)
        return rot(rq, 1.0), rot(rk, RET_DK ** -0.5)
    return fn


def _make_mix_fn(cfg):
    hg, hr = cfg.hg, cfg.hr

    def fn(pids, oa, ob, pm, gnorm):
        d = cfg.d
        gz, rg, gate_a, gate_b = pm[:, :d], pm[:, d:2 * d], pm[:, 2 * d:3 * d], pm[:, 3 * d:]
        oas = []
        for h in range(hg):
            oh = oa[:, h * GDN_DK:(h + 1) * GDN_DK]
            oas.append(oh * lax.rsqrt(jnp.mean(oh * oh, axis=-1, keepdims=True) + EPS) * gnorm)
        ya = jnp.concatenate(oas, axis=1) * _silu(gz)
        obs = []
        for h in range(hr):
            oh = ob[:, h * RET_DK:(h + 1) * RET_DK]
            obs.append(oh * lax.rsqrt(jnp.mean(oh * oh, axis=-1, keepdims=True) + EPS))
        yb = _silu(rg) * jnp.concatenate(obs, axis=1)
        return (_sigmoid(gate_a) * ya + _sigmoid(gate_b) * yb,)
    return fn


def _act_fn(pids, u):
    f = u.shape[1] // 2
    return (_silu(u[:, :f]) * u[:, f:],)


def _gdn_step(s, u, w, a, qd, kd, gl):
    top = _dot(jnp.concatenate([w, qd], axis=0), s)
    v_new = u - top[:CHUNK]
    bot = _dot(jnp.concatenate([a, kd.T], axis=0), v_new)
    o = top[CHUNK:] + bot[:CHUNK]
    s2 = s * gl + bot[CHUNK:]
    return s2, o


def _ret_step(s, q, k, v, dm, qdc, kdc, g):
    att = _dot(q, k, False, True) * dm
    bot = _dot(jnp.concatenate([att, (k * kdc).T], axis=0), v)
    o = bot[:CHUNK] + _dot(q * qdc, s)
    s2 = s * g + bot[CHUNK:]
    return s2, o


class Part(NamedTuple):
    body: object
    args: list
    in_specs: list
    out_shape: list
    out_specs: list
    scratch: list
    aliases: dict = {}


def _run_parts(name, grid, parts):
    n_in = [len(p.args) for p in parts]
    n_out = [len(p.out_shape) for p in parts]
    n_sc = [len(p.scratch) for p in parts]
    off_in = [sum(n_in[:k]) for k in range(len(parts))]
    off_out = [sum(n_out[:k]) for k in range(len(parts))]
    off_sc = [sum(n_sc[:k]) for k in range(len(parts))]

    def body(*refs):
        ins, outs, scr = refs[:sum(n_in)], refs[sum(n_in):sum(n_in) + sum(n_out)], refs[sum(n_in) + sum(n_out):]
        for k, p in enumerate(parts):
            p.body(*ins[off_in[k]:off_in[k] + n_in[k]], *outs[off_out[k]:off_out[k] + n_out[k]],
                   *scr[off_sc[k]:off_sc[k] + n_sc[k]])

    aliases = {off_in[k] + i: off_out[k] + o for k, p in enumerate(parts) for i, o in p.aliases.items()}
    res = _pallas(
        body, out_shape=sum((p.out_shape for p in parts), []), grid=grid, in_specs=sum((p.in_specs for p in parts), []),
        out_specs=sum((p.out_specs for p in parts), []), scratch_shapes=sum((p.scratch for p in parts), []),
        input_output_aliases=aliases, name=name, compiler_params=_params(grid))(*sum((p.args for p in parts), []))
    return [res[off_out[k]:off_out[k] + n_out[k]] for k in range(len(parts))]


def _gdn_scan_fwd(cfg, u, w, attn, qd, kd, gl):
    d, hg, nch, sc = cfg.d, cfg.hg, cfg.nch, cfg.sc
    nst = nch // sc

    def body(u_ref, w_ref, a_ref, qd_ref, kd_ref, gl_ref, o_ref, ss_ref, s_ref):
        @pl.when(pl.program_id(0) == 0)
        def _():
            s_ref[...] = jnp.zeros(s_ref.shape, F32)

        states = [s_ref[h] for h in range(hg)]
        for j in range(sc):
            rows = slice(j * CHUNK, (j + 1) * CHUNK)
            outs = []
            for h in range(hg):
                sl = slice(h * GDN_DK, (h + 1) * GDN_DK)
                ss_ref[j, h] = states[h]
                states[h], o = _gdn_step(states[h], u_ref[rows, sl], w_ref[rows, sl], a_ref[h, rows, :],
                                         qd_ref[rows, sl], kd_ref[rows, sl], gl_ref[h, j])
                outs.append(o)
            o_ref[rows, :] = jnp.concatenate(outs, axis=1)
        for h in range(hg):
            s_ref[h] = states[h]

    row = pl.BlockSpec((sc * CHUNK, d), lambda n: (n, 0))
    return Part(
        body, [u, w, attn, qd, kd, gl],
        [row, row, pl.BlockSpec((hg, sc * CHUNK, CHUNK), lambda n: (0, n, 0)), row, row,
         pl.BlockSpec((hg, sc, 1, GDN_DK), lambda n: (0, n, 0, 0))],
        [jax.ShapeDtypeStruct((cfg.rp, d), F32), jax.ShapeDtypeStruct((nch, hg, GDN_DK, GDN_DK), F32)],
        [row, pl.BlockSpec((sc, hg, GDN_DK, GDN_DK), lambda n: (n, 0, 0, 0))],
        [pltpu.VMEM((hg, GDN_DK, GDN_DK), F32)])


def _gdn_scan_bwd(cfg, do, u, w, attn, qd, kd, gl, ss):
    d, hg, nch, sc = cfg.d, cfg.hg, cfg.nch, cfg.sc
    nst = nch // sc

    def body(do_ref, u_ref, w_ref, a_ref, qd_ref, kd_ref, gl_ref, ss_ref,
             du_ref, dw_ref, da_ref, dqd_ref, dkd_ref, dgl_ref, ds_ref):
        @pl.when(pl.program_id(0) == 0)
        def _():
            ds_ref[...] = jnp.zeros(ds_ref.shape, F32)

        dstates = [ds_ref[h] for h in range(hg)]
        for j in reversed(range(sc)):
            rows = slice(j * CHUNK, (j + 1) * CHUNK)
            dus, dws, dqds, dkds = [], [], [], []
            for h in range(hg):
                sl = slice(h * GDN_DK, (h + 1) * GDN_DK)
                args = (ss_ref[j, h], u_ref[rows, sl], w_ref[rows, sl], a_ref[h, rows, :], qd_ref[rows, sl],
                        kd_ref[rows, sl], gl_ref[h, j])
                _, vjp_fn = jax.vjp(_gdn_step, *args)
                dstates[h], du, dw, da, dqd, dkd, dgl = vjp_fn((dstates[h], do_ref[rows, sl]))
                da_ref[h, rows, :] = da
                dgl_ref[h, j] = dgl
                dus.append(du)
                dws.append(dw)
                dqds.append(dqd)
                dkds.append(dkd)
            du_ref[rows, :] = jnp.concatenate(dus, axis=1)
            dw_ref[rows, :] = jnp.concatenate(dws, axis=1)
            dqd_ref[rows, :] = jnp.concatenate(dqds, axis=1)
            dkd_ref[rows, :] = jnp.concatenate(dkds, axis=1)
        for h in range(hg):
            ds_ref[h] = dstates[h]

    row = pl.BlockSpec((sc * CHUNK, d), lambda n: (nst - 1 - n, 0))
    aspec = pl.BlockSpec((hg, sc * CHUNK, CHUNK), lambda n: (0, nst - 1 - n, 0))
    gspec = pl.BlockSpec((hg, sc, 1, GDN_DK), lambda n: (0, nst - 1 - n, 0, 0))
    rowshape = jax.ShapeDtypeStruct((cfg.rp, d), F32)
    return Part(
        body, [do, u, w, attn, qd, kd, gl, ss],
        [row, row, row, aspec, row, row, gspec, pl.BlockSpec((sc, hg, GDN_DK, GDN_DK), lambda n: (nst - 1 - n, 0, 0, 0))],
        [rowshape, rowshape, jax.ShapeDtypeStruct(attn.shape, F32), rowshape, rowshape, jax.ShapeDtypeStruct(gl.shape, F32)],
        [row, row, aspec, row, row, gspec],
        [pltpu.VMEM((hg, GDN_DK, GDN_DK), F32)])


def _ret_consts(cfg):
    hr = cfg.hr
    lg = np.log(1.0 - 2.0 ** (-5.0 - np.arange(hr, dtype=np.float64)))
    idx = np.arange(CHUNK, dtype=np.float64)
    tril = np.tril(np.ones((CHUNK, CHUNK), dtype=bool))
    dm = np.where(tril[None], np.exp((idx[:, None] - idx[None, :])[None] * lg[:, None, None]), 0.0)
    qdc = np.exp((idx[None, :] + 1.0) * lg[:, None])
    kdc = np.exp((CHUNK - 1.0 - idx[None, :]) * lg[:, None])
    gch = np.exp(CHUNK * lg)
    qdc = np.broadcast_to(qdc[:, :, None], (hr, CHUNK, RET_DK))
    kdc = np.broadcast_to(kdc[:, :, None], (hr, CHUNK, RET_DK))
    gch = np.broadcast_to(gch[:, None, None], (hr, 1, RET_DK))
    return tuple(jnp.asarray(np.ascontiguousarray(t), F32) for t in (dm, qdc, kdc, gch))


def _ret_scan_fwd(cfg, qr, kr, proj, consts):
    d, hr, nch, sc = cfg.d, cfg.hr, cfg.nch, cfg.sc
    nst = nch // sc
    dm, qdc, kdc, gch = consts

    def body(q_ref, k_ref, v_ref, dm_ref, qdc_ref, kdc_ref, g_ref, o_ref, ss_ref, s_ref):
        @pl.when(pl.program_id(0) == 0)
        def _():
            s_ref[...] = jnp.zeros(s_ref.shape, F32)

        states = [s_ref[h] for h in range(hr)]
        for j in range(sc):
            rows = slice(j * CHUNK, (j + 1) * CHUNK)
            outs = []
            for h in range(hr):
                sl = slice(h * RET_DK, (h + 1) * RET_DK)
                ss_ref[j, h] = states[h]
                states[h], o = _ret_step(states[h], q_ref[rows, sl], k_ref[rows, sl], v_ref[rows, sl], dm_ref[h],
                                         qdc_ref[h], kdc_ref[h], g_ref[h])
                outs.append(o)
            o_ref[rows, :] = jnp.concatenate(outs, axis=1)
        for h in range(hr):
            s_ref[h] = states[h]

    row = pl.BlockSpec((sc * CHUNK, d), lambda n: (n, 0))
    return Part(
        body, [qr, kr, proj, dm, qdc, kdc, gch],
        [row, row, pl.BlockSpec((sc * CHUNK, d), lambda n: (n, RV_BLOCK)), _full(dm), _full(qdc), _full(kdc), _full(gch)],
        [jax.ShapeDtypeStruct((cfg.rp, d), F32), jax.ShapeDtypeStruct((nch, hr, RET_DK, RET_DK), F32)],
        [row, pl.BlockSpec((sc, hr, RET_DK, RET_DK), lambda n: (n, 0, 0, 0))],
        [pltpu.VMEM((hr, RET_DK, RET_DK), F32)])


def _ret_scan_bwd(cfg, do, qr, kr, proj, consts, ss, dproj):
    d, hr, nch, sc = cfg.d, cfg.hr, cfg.nch, cfg.sc
    nst = nch // sc
    dm, qdc, kdc, gch = consts

    def body(do_ref, q_ref, k_ref, v_ref, dm_ref, qdc_ref, kdc_ref, g_ref, ss_ref, _, dq_ref, dk_ref, dv_ref, ds_ref):
        @pl.when(pl.program_id(0) == 0)
        def _():
            ds_ref[...] = jnp.zeros(ds_ref.shape, F32)

        dstates = [ds_ref[h] for h in range(hr)]
        for j in reversed(range(sc)):
            rows = slice(j * CHUNK, (j + 1) * CHUNK)
            dqs, dks, dvs = [], [], []
            for h in range(hr):
                sl = slice(h * RET_DK, (h + 1) * RET_DK)
                cs = (dm_ref[h], qdc_ref[h], kdc_ref[h], g_ref[h])
                _, vjp_fn = jax.vjp(lambda s, q, k, v, cs=cs: _ret_step(s, q, k, v, *cs),
                                    ss_ref[j, h], q_ref[rows, sl], k_ref[rows, sl], v_ref[rows, sl])
                dstates[h], dq, dk, dv = vjp_fn((dstates[h], do_ref[rows, sl]))
                dqs.append(dq)
                dks.append(dk)
                dvs.append(dv)
            dq_ref[rows, :] = jnp.concatenate(dqs, axis=1)
            dk_ref[rows, :] = jnp.concatenate(dks, axis=1)
            dv_ref[rows, :] = jnp.concatenate(dvs, axis=1).astype(dv_ref.dtype)
        for h in range(hr):
            ds_ref[h] = dstates[h]

    row = pl.BlockSpec((sc * CHUNK, d), lambda n: (nst - 1 - n, 0))
    rowshape = jax.ShapeDtypeStruct((cfg.rp, d), F32)
    vspec = pl.BlockSpec((sc * CHUNK, d), lambda n: (nst - 1 - n, RV_BLOCK))
    return Part(
        body, [do, qr, kr, proj, dm, qdc, kdc, gch, ss, dproj],
        [row, row, row, vspec, _full(dm), _full(qdc), _full(kdc), _full(gch),
         pl.BlockSpec((sc, hr, RET_DK, RET_DK), lambda n: (nst - 1 - n, 0, 0, 0)), ANY],
        [rowshape, rowshape, jax.ShapeDtypeStruct(dproj.shape, dproj.dtype)],
        [row, row, vspec],
        [pltpu.VMEM((hr, RET_DK, RET_DK), F32)], {9: 2})


def _final(cfg, h2, normf, tgt):
    d, tr = cfg.d, cfg.xrow
    nr = cfg.rp // tr

    def body(h_ref, g_ref, t_ref, dh_ref, dg_ref, loss_ref):
        i = pl.program_id(0)
        y, vjp_fn = jax.vjp(_rms, h_ref[...], g_ref[...])
        err = jnp.where(i >= 1, y - t_ref[...], 0.0)
        dh, dg = vjp_fn(err * (1.0 / d))
        dh_ref[...] = dh
        part = jnp.zeros((8, LANES), F32) + 0.5 * jnp.sum(err * err) * (1.0 / d)

        @pl.when(i == 0)
        def _():
            dg_ref[...] = dg
            loss_ref[...] = part

        @pl.when(i > 0)
        def _():
            dg_ref[...] += dg
            loss_ref[...] += part

    return _pallas(
        body,
        out_shape=[jax.ShapeDtypeStruct((cfg.rp, d), F32), jax.ShapeDtypeStruct((1, d), F32),
                   jax.ShapeDtypeStruct((8, LANES), F32)],
        grid=(nr,),
        in_specs=[_rows(tr, d), _full(normf), pl.BlockSpec((tr, d), lambda i: (jnp.maximum(i - 1, 0), 0))],
        out_specs=[_rows(tr, d), pl.BlockSpec((1, d), lambda i: (0, 0)), pl.BlockSpec((8, LANES), lambda i: (0, 0))],
        name="final_loss", compiler_params=_params((nr,)))(h2, normf, tgt)


ANY = pl.BlockSpec(memory_space=pl.ANY)


def _place():
    x, y, c = lax.axis_index("x"), lax.axis_index("y"), lax.axis_index("c")
    others = [(1 - x, y), (x, 1 - y), (1 - x, 1 - y)]
    return x, y, c, others


def _row_tile(rows, cap=256):
    return max(t for t in range(16, min(rows, cap) + 1, 16) if rows % t == 0)


class Comm(NamedTuple):
    ins: list
    outs: list
    sems: list
    start: object
    finish: object


def _run_comm(name, comm):
    n_in, n_out = len(comm.ins), len(comm.outs)

    def body(*refs):
        ins, outs, sems = refs[:n_in], refs[n_in:n_in + n_out], refs[n_in + n_out:]
        comm.start(ins, outs, sems)
        comm.finish(ins, outs, sems)

    return _pallas(body, out_shape=comm.outs, in_specs=[ANY] * n_in, out_specs=[ANY] * n_out,
                   scratch_shapes=comm.sems, name=name)(*comm.ins)


def _gather_comm(ws):
    n = len(ws)
    halves = [w.shape[0] // 2 for w in ws]

    def copies(w_refs, o_refs, sems):
        send_sems, recv_sems = sems
        x, y, c, others = _place()
        me = 2 * x + y
        chips = [2 * px + py for px, py in others]

        def piece(a, chip, core):
            return o_refs[a].at[chip, pl.ds(core * halves[a], halves[a]), :]

        def copy(a, k, src, chip, core, to):
            return pltpu.make_async_remote_copy(src_ref=src, dst_ref=piece(a, chip, core), send_sem=send_sems.at[6 * a + k],
                                                recv_sem=recv_sems.at[6 * a + k], device_id=to, device_id_type=MESH)

        def first(j, a):
            return copy(a, j, w_refs[a].at[pl.ds(c * halves[a], halves[a]), :], me, c, (*others[j], c))

        def landed(j, a):
            return copy(a, j, piece(a, chips[j], c), chips[j], c, (x, y, c))

        def passed(j, a):
            return copy(a, 3 + j, piece(a, chips[j], c), chips[j], c, (x, y, 1 - c))

        def from_sibling(j, a):
            return copy(a, 3 + j, piece(a, chips[j], 1 - c), chips[j], 1 - c, (x, y, c))

        return first, landed, passed, from_sibling

    pairs = [(j, a) for j in range(3) for a in range(n)]

    def start(w_refs, o_refs, sems):
        first, _, _, _ = copies(w_refs, o_refs, sems)
        for j, a in pairs:
            first(j, a).start()

    def finish(w_refs, o_refs, sems):
        first, landed, passed, from_sibling = copies(w_refs, o_refs, sems)
        for j, a in pairs:
            landed(j, a).wait_recv()
            passed(j, a).start()
        for j, a in pairs:
            from_sibling(j, a).wait_recv()
        for j, a in pairs:
            first(j, a).wait_send()
            passed(j, a).wait_send()

    return Comm(list(ws), [jax.ShapeDtypeStruct((N_CHIPS,) + w.shape, w.dtype) for w in ws],
                [pltpu.SemaphoreType.DMA((6 * n,)), pltpu.SemaphoreType.DMA((6 * n,))], start, finish)


def _pair_exchange(name, gs):
    n = len(gs)

    def body(*refs):
        g_refs, o_refs = refs[:n], refs[n:2 * n]
        send_sems, recv_sems = refs[2 * n:]
        x, y, c, _ = _place()
        cps = []
        for a in range(n):
            half = gs[a].shape[1] // 2
            cp = pltpu.make_async_remote_copy(
                src_ref=g_refs[a].at[:, pl.ds((1 - c) * half, half), :], dst_ref=o_refs[a], send_sem=send_sems.at[a],
                recv_sem=recv_sems.at[a], device_id=(x, y, 1 - c), device_id_type=MESH)
            cp.start()
            cps.append(cp)
        for cp in cps:
            cp.wait()

    return _pallas(
        body, out_shape=[jax.ShapeDtypeStruct((N_CHIPS, g.shape[1] // 2, g.shape[2]), g.dtype) for g in gs],
        in_specs=[ANY] * n, out_specs=[ANY] * n,
        scratch_shapes=[pltpu.SemaphoreType.DMA((n,)), pltpu.SemaphoreType.DMA((n,))], name=name)(*gs)


def _pair_sum(name, g, recv, cidx):
    half, cols = recv.shape[1], recv.shape[2]
    tr = _row_tile(half)
    nblk = half // tr

    def body(c_ref, g_ref, r_ref, o_ref):
        o_ref[...] = (g_ref[...] + r_ref[...]).astype(o_ref.dtype)

    grid_spec = pltpu.PrefetchScalarGridSpec(
        num_scalar_prefetch=1, grid=(N_CHIPS, nblk),
        in_specs=[pl.BlockSpec((1, tr, cols), lambda s, i, c: (s, c[0] * nblk + i, 0)),
                  pl.BlockSpec((1, tr, cols), lambda s, i, c: (s, i, 0))],
        out_specs=pl.BlockSpec((1, tr, cols), lambda s, i, c: (s, i, 0)))
    return _pallas(
        body, out_shape=jax.ShapeDtypeStruct((N_CHIPS, half, cols), BF16), grid_spec=grid_spec,
        name=name, compiler_params=_params((N_CHIPS, nblk)))(cidx, g, recv)


def _exchange_comm(parts):
    n = len(parts)

    def copies(p_refs, o_refs, sems):
        send_sems, recv_sems = sems
        x, y, c, others = _place()
        me = 2 * x + y

        def copy(a, j, src_chip, dst_chip):
            px, py = others[j]
            return pltpu.make_async_remote_copy(
                src_ref=p_refs[a].at[src_chip], dst_ref=o_refs[a].at[dst_chip], send_sem=send_sems.at[3 * a + j],
                recv_sem=recv_sems.at[3 * a + j], device_id=(px, py, c), device_id_type=MESH)

        def send(j, a):
            return copy(a, j, 2 * others[j][0] + others[j][1], me)

        def arrival(j, a):
            return copy(a, j, me, 2 * others[j][0] + others[j][1])

        return send, arrival

    pairs = [(j, a) for j in range(3) for a in range(n)]

    def start(p_refs, o_refs, sems):
        send, _ = copies(p_refs, o_refs, sems)
        for j, a in pairs:
            send(j, a).start()

    def finish(p_refs, o_refs, sems):
        send, arrival = copies(p_refs, o_refs, sems)
        for j, a in pairs:
            arrival(j, a).wait_recv()
        for j, a in pairs:
            send(j, a).wait_send()

    return Comm(list(parts), [jax.ShapeDtypeStruct(p.shape, p.dtype) for p in parts],
                [pltpu.SemaphoreType.DMA((3 * n,)), pltpu.SemaphoreType.DMA((3 * n,))], start, finish)


def _chip_sum(name, part, slots, chip):
    half, cols = slots.shape[1], slots.shape[2]
    tr = _row_tile(half)

    def body(me_ref, p_ref, *rest):
        s_refs, o_ref = rest[:N_CHIPS], rest[N_CHIPS]
        own = p_ref[...].astype(F32)
        v = [jnp.where(me_ref[0] == k, own, s_refs[k][...].astype(F32)) for k in range(N_CHIPS)]
        o_ref[...] = ((v[0] + v[1]) + v[2]) + v[3]

    def slot_spec(k):
        return pl.BlockSpec((None, tr, cols), lambda i, me: (jnp.where(me[0] == k, (k + 1) % N_CHIPS, k), i, 0))

    grid_spec = pltpu.PrefetchScalarGridSpec(
        num_scalar_prefetch=1, grid=(half // tr,),
        in_specs=[pl.BlockSpec((None, tr, cols), lambda i, me: (me[0], i, 0))] + [slot_spec(k) for k in range(N_CHIPS)],
        out_specs=pl.BlockSpec((tr, cols), lambda i, me: (i, 0)))
    return _pallas(
        body, out_shape=jax.ShapeDtypeStruct((half, cols), F32), grid_spec=grid_spec,
        name=name, compiler_params=_params((half // tr,)))(chip, part, *([slots] * N_CHIPS))


def _pair_swap(fins):
    n = len(fins)

    def body(*refs):
        f_refs, o_refs = refs[:n], refs[n:2 * n]
        send_sems, recv_sems = refs[2 * n:]
        x, y, c, _ = _place()
        cps = [pltpu.make_async_remote_copy(src_ref=f_refs[a], dst_ref=o_refs[a], send_sem=send_sems.at[a],
                                            recv_sem=recv_sems.at[a], device_id=(x, y, 1 - c), device_id_type=MESH)
               for a in range(n)]
        for cp in cps:
            cp.start()
        for cp in cps:
            cp.wait()

    return _pallas(
        body, out_shape=[jax.ShapeDtypeStruct(f.shape, f.dtype) for f in fins], in_specs=[ANY] * n, out_specs=[ANY] * n,
        scratch_shapes=[pltpu.SemaphoreType.DMA((n,)), pltpu.SemaphoreType.DMA((n,))], name="grad_pair_swap")(*fins)


def _adamw(name, w, g_own, g_other, m, v, cidx):
    R, cols = w.shape[-2:]
    lead = (None,) * (w.ndim - 2)
    zeros = (0,) * (w.ndim - 2)
    half = R // 2
    tr = _row_tile(half, 128)
    nblk = half // tr
    c1 = 1.0 - ADAM_B1 ** ADAM_STEP
    c2 = 1.0 - ADAM_B2 ** ADAM_STEP

    def body(c_ref, w_ref, go_ref, gs_ref, m_ref, v_ref, g_ref, d_ref, nm_ref, nv_ref):
        mine = (pl.program_id(0) // nblk) == c_ref[0]
        gv = jnp.where(mine, go_ref[...], gs_ref[...])
        nm = ADAM_B1 * m_ref[...] + (1.0 - ADAM_B1) * gv
        nv = ADAM_B2 * v_ref[...] + (1.0 - ADAM_B2) * (gv * gv)
        g_ref[...] = gv
        d_ref[...] = -ADAM_LR * ((nm / c1) / (jnp.sqrt(nv / c2) + ADAM_EPS) + ADAM_WD * w_ref[...])
        nm_ref[...] = nm
        nv_ref[...] = nv

    spec = pl.BlockSpec(lead + (tr, cols), lambda i, c: zeros + (i, 0))
    hspec = pl.BlockSpec((tr, cols), lambda i, c: (i % nblk, 0))
    shape = jax.ShapeDtypeStruct(w.shape, F32)
    grid_spec = pltpu.PrefetchScalarGridSpec(num_scalar_prefetch=1, grid=(R // tr,),
                                             in_specs=[spec, hspec, hspec, spec, spec], out_specs=[spec] * 4)
    return _pallas(
        body, out_shape=[shape] * 4, grid_spec=grid_spec,
        name=name, compiler_params=_params((R // tr,)))(cidx, w, g_own, g_other, m, v)


PARAMS = (("meta", 1), ("norm1", None), ("w_in", 2), ("gdn_conv_w", 2), ("gdn_a_log", None), ("gdn_dt_bias", None),
          ("gdn_norm", None), ("w_out", 1), ("norm2", None), ("w_ffn_up", 2), ("ffn_conv_w", 2), ("ffn_conv_b", None),
          ("w_ffn_down", 1), ("norm_f", None))
BIG = ("w_in", "w_out", "w_ffn_up", "w_ffn_down")
PACK_ALIGN = 1024
PACK_ROWS_ALIGN = 32


def _pack(arrs, dtype):
    parts, total = [], 0
    for a in arrs:
        f = a.reshape(-1).astype(dtype)
        pad = (-f.shape[0]) % PACK_ALIGN
        parts.append(jnp.pad(f, (0, pad)) if pad else f)
        total += f.shape[0] + pad
    rows = total // LANES
    rpad = (-rows) % PACK_ROWS_ALIGN
    if rpad:
        parts.append(jnp.zeros((rpad * LANES,), dtype))
    return jnp.concatenate(parts).reshape(rows + rpad, LANES)


def _unpack(buf, shapes):
    flat = buf.reshape(-1)
    outs, off = [], 0
    for s in shapes:
        n = int(np.prod(s))
        outs.append(flat[off:off + n].reshape(s))
        off += n + (-n) % PACK_ALIGN
    return outs


def _split4(a, axis):
    n = a.shape[axis] // N_CHIPS
    return [lax.slice_in_dim(a, s * n, (s + 1) * n, axis=axis) for s in range(N_CHIPS)]


PROJ_ORDER = (3, 7, 8, 9, 0, 1, 2, 6, 4, 5)


def _reorder_w_in(w, cfg):
    d, hg = cfg.d, cfg.hg

    def block(k):
        off = k * d + (2 * hg if k >= 4 else 0)
        return w[:, off:off + d]

    tail = jnp.pad(w[:, 4 * d:4 * d + 2 * hg], ((0, 0), (0, LANES - 2 * hg)))
    return jnp.concatenate([block(k) for k in PROJ_ORDER] + [tail], axis=1)


def _restore_w_in(wr, cfg):
    d, hg = cfg.d, cfg.hg
    at = {k: i for i, k in enumerate(PROJ_ORDER)}
    block = lambda k: wr[:, at[k] * d:(at[k] + 1) * d]
    return jnp.concatenate([block(k) for k in range(4)] + [wr[:, 10 * d:10 * d + 2 * hg]] +
                           [block(k) for k in range(4, 10)], axis=1)


def _step(cfg, x, tgt, shard, m_shard, v_shard):
    d, hg, dff, rp, tr, tm = cfg.d, cfg.hg, cfg.dff, cfg.rp, cfg.tr, cfg.tm
    nrow = rp // tr
    assert cfg.tf * N_CHIPS == 2 * dff and cfg.din % N_CHIPS == 0
    cidx = lax.axis_index("c").astype(jnp.int32).reshape(1)
    chip = (2 * lax.axis_index("x") + lax.axis_index("y")).astype(jnp.int32).reshape(1)

    axis = dict(PARAMS)
    small = ("meta", "gdn_conv_w", "ffn_conv_w")
    small_shapes = [shard[n].shape for n in small]
    mine = [shard[n][0].astype(BF16) for n in BIG] + [_pack([shard[n] for n in small], F32)]

    def with_own(gathered, own):
        return [lax.dynamic_update_slice(g, w[None], (chip[0], 0, 0)) for g, w in zip(gathered, own)]

    g_in, g_small = with_own(_run_comm("weights_gather_first", _gather_comm([mine[0], mine[4]])), [mine[0], mine[4]])
    w_in_r = _reorder_w_in(jnp.concatenate([g_in[s] for s in range(N_CHIPS)], axis=1), cfg)
    per_chip = [_unpack(g_small[s], small_shapes) for s in range(N_CHIPS)]
    full = {n: jnp.concatenate([per_chip[s][k] for s in range(N_CHIPS)], axis=axis[n]) for k, n in enumerate(small)}
    meta = full["meta"]
    gconv_w = full["gdn_conv_w"][0]
    fconv_w = full["ffn_conv_w"][0]
    norm1, norm2, gnorm = shard["norm1"], shard["norm2"], shard["gdn_norm"]
    normf = shard["norm_f"].reshape(1, d)
    fconv_b = shard["ffn_conv_b"]
    alog = jnp.pad(shard["gdn_a_log"], ((0, 7), (0, LANES - hg)))
    dtb = jnp.pad(shard["gdn_dt_bias"], ((0, 7), (0, LANES - hg)))

    h0 = jnp.concatenate([jnp.zeros((cfg.front, d), F32), meta, x], axis=0)
    half = RET_DK // 2
    pos = np.arange(rp, dtype=np.float32) - np.float32(cfg.front)
    inv = (np.float32(1.0) / np.float32(ROPE_BASE) ** (np.arange(half, dtype=np.float32) / np.float32(half))).astype(np.float32)
    ang = pos[:, None] * inv[None, :]
    cos, sin = jnp.asarray(np.cos(ang), F32), jnp.asarray(np.sin(ang), F32)
    rconsts = _ret_consts(cfg)

    tr_n = 3 * tr if rp % (3 * tr) == 0 else tr
    rms_f = _make_rms_fn(cfg, tr_n, False)
    rms_b = _make_rms_fn(cfg, tr_n, True)
    rowshape = jax.ShapeDtypeStruct((rp, d), F32)
    rspec = _rows(tr, d)
    nspec = _rows(tr_n, d)

    def rms_fwd(name, h, g):
        return _stage_fwd(name, rms_f, (rp // tr_n,), [In(h, nspec), In(g, _full(g))],
                          [jax.ShapeDtypeStruct((rp, d), BF16)], [nspec])[0]

    wm = 10 * d
    w_main, w_tail = w_in_r[:, :wm], w_in_r[:, wm:]
    tn_in = 2560 if wm % 2560 == 0 else LANES
    hn1 = rms_fwd("rms1_fwd", h0, norm1)
    proj, rest = _mm("proj_fwd", hn1, w_main, tm=tm, tn=tn_in, tk=d, out_dtype=BF16, comm=_gather_comm(mine[1:4]))
    ptail = _mm("proj_tail_fwd", hn1, w_tail, tm=tm, tn=LANES, tk=d)
    g_out, g_up, g_down = with_own(rest, mine[1:4])
    w_out = g_out.reshape(d, d)
    w_up = g_up
    w_up_t = jnp.swapaxes(g_up, 1, 2).reshape(2 * dff, d)
    w_down = g_down.reshape(dff, d)
    cqkv = _conv_fwd("gdn_conv_fwd", proj, CONV_COL * d, gconv_w, None, taps=GDN_CONV, width=3 * d, tr=tr, tc=d)
    prep_fn = _make_gdn_prep_fn(cfg, tr)
    prep_ins = [In(cqkv, _rows(tr, 3 * d), BF16), In(ptail, _rows(tr, LANES), BF16),
                In(alog, _full(alog), F32, True), In(dtb, _full(dtb), F32, True)]
    qn, kn, vv, bB, lB = _stage_fwd("gdn_prep_fwd", prep_fn, (nrow,), prep_ins, [rowshape] * 5, [rspec] * 5)

    trg = cfg.nb * CHUNK
    gi_grid = (rp // trg, hg)
    hspec = pl.BlockSpec((trg, GDN_DK), lambda i, h: (i, h))
    aspec = pl.BlockSpec((1, trg, CHUNK), lambda i, h: (h, i, 0))
    gspec = pl.BlockSpec((1, cfg.nb, 1, GDN_DK), lambda i, h: (h, i, 0, 0))
    intra_ins = [In(t, hspec, F32) for t in (qn, kn, vv, bB, lB)]
    ashape = jax.ShapeDtypeStruct((hg, rp, CHUNK), F32)
    intra_shapes = [rowshape, rowshape, ashape, rowshape, rowshape, jax.ShapeDtypeStruct((hg, cfg.nch, 1, GDN_DK), F32), ashape]
    intra_specs = [hspec, hspec, aspec, hspec, hspec, gspec, aspec]
    gu, gw, gattn, gqd, gkd, ggl, gtinv = _stage_fwd("gdn_intra_fwd", _gdn_intra_fn, gi_grid, intra_ins, intra_shapes,
                                                     intra_specs)
    rot_fn = _make_rot_fn(cfg)

    def rot_ins(dproj=None):
        return [In(proj, _rows(tr_n, 2 * d, ROT_COL // 2), BF16, galias=dproj, gshape=(rp, wm)),
                In(cos, _rows(tr_n, half)), In(sin, _rows(tr_n, half))]

    qr, kr = _stage_fwd("rot_fwd", rot_fn, (rp // tr_n,), rot_ins(), [rowshape] * 2, [nspec] * 2)
    nst = cfg.nch // cfg.sc
    oa, gss = _run_parts("gdn_scan_fwd", (nst,), [_gdn_scan_fwd(cfg, gu, gw, gattn, gqd, gkd, ggl)])[0]
    ob, rss = _run_parts("ret_scan_fwd", (nst,), [_ret_scan_fwd(cfg, qr, kr, proj, rconsts)])[0]

    mix_fn = _make_mix_fn(cfg)
    mix_ins = [In(oa, rspec, F32), In(ob, rspec, F32), In(proj, _rows(tr, 4 * d, MIX_COL // 4), BF16, gshape=(rp, wm)),
               In(gnorm, _full(gnorm), F32, True)]
    ymix = _stage_fwd("mix_fwd", mix_fn, (nrow,), mix_ins, [jax.ShapeDtypeStruct((rp, d), BF16)], [rspec])[0]
    h1 = _mm("out_proj_fwd", ymix, w_out, tm=tm, tn=d, tk=d, add=h0)

    hn2 = rms_fwd("rms2_fwd", h1, norm2)
    up = _mm("ffn_up_fwd", hn2, w_up, tm=tm, tn=cfg.tf, tk=d, out_dtype=BF16)
    uc = _conv_fwd("ffn_conv_fwd", up, 0, fconv_w, fconv_b, taps=FFN_CONV, width=2 * dff, tr=tr, tc=cfg.tf)
    tra = tr
    act_ins = [In(uc, _rows(tra, 2 * dff), BF16)]
    act_spec = _rows(tra, dff)
    act = _stage_fwd("ffn_act_fwd", _act_fn, (rp // tra,), act_ins, [jax.ShapeDtypeStruct((rp, dff), BF16)], [act_spec])[0]
    h2 = _mm("ffn_down_fwd", act, w_down, tm=tm, tn=d, tk=cfg.tf, add=h1)

    dh2, g_normf, loss_blk = _final(cfg, h2, normf, tgt)
    loss = lax.psum(loss_blk[0, 0], ("x", "y", "c"))

    g_w_down = _mm_tn("ffn_down_dw", act, dh2, tr=tm, tka=cfg.tf, tn=d)
    dact = _mm("ffn_down_dx", dh2, w_down.T, tm=tm, tn=cfg.tf, tk=d)
    duc, = _stage_bwd("ffn_act_bwd", _act_fn, (rp // tra,), act_ins, [(dact, act_spec)])
    dup, g_fconv_w, g_fconv_b = _conv_bwd("ffn_conv_bwd", up, 0, fconv_w, duc, taps=FFN_CONV, width=2 * dff,
                                          tr=tr, tc=cfg.tf, with_bias=True)
    g_w_up = _mm_tn("ffn_up_dw", hn2, dup, tr=tm, tka=d, tn=cfg.tf, blocked=True)

    def pair_reduce(tag, names, arrs):
        recvs = _pair_exchange("grad_pair_exchange_" + tag, arrs)
        return [_pair_sum("grad_pair_sum_" + n, g, r, cidx) for n, g, r in zip(names, arrs, recvs)]

    parts_ffn = pair_reduce("ffn", ["w_ffn_down", "w_ffn_up"], [g_w_down.reshape(N_CHIPS, dff // N_CHIPS, d), g_w_up])
    dhn2, slots_ffn = _mm("ffn_up_dx", dup, w_up_t, tm=tm, tn=d, tk=2 * cfg.tf, comm=_exchange_comm(parts_ffn))

    def rms_bwd(name, h, g, dhn, dres):
        ins = [In(h, nspec, F32), In(g, _full(g), F32, True)]
        return _stage_bwd(name, rms_b, (rp // tr_n,), ins, [(dhn, nspec), (dres, nspec)])

    dh1, g_norm2 = rms_bwd("rms2_bwd", h1, norm2, dhn2, dh2)
    g_w_out = _mm_tn("out_proj_dw", ymix, dh1, tr=tm, tka=d, tn=d)
    dymix = _mm("out_proj_dx", dh1, w_out.T, tm=tm, tn=d, tk=d)
    doa, dob, dproj, g_gnorm = _stage_bwd("mix_bwd", mix_fn, (nrow,), mix_ins, [(dymix, rspec)])

    dqr, dkr, dproj = _run_parts("ret_scan_bwd", (nst,), [_ret_scan_bwd(cfg, dob, qr, kr, proj, rconsts, rss, dproj)])[0]
    dproj, = _stage_bwd("rot_bwd", rot_fn, (rp // tr_n,), rot_ins(dproj), [(dqr, nspec), (dkr, nspec)])
    dgu, dgw, dgattn, dgqd, dgkd, dggl = _run_parts(
        "gdn_scan_bwd", (nst,), [_gdn_scan_bwd(cfg, doa, gu, gw, gattn, gqd, gkd, ggl, gss)])[0]

    intra_cots = [(dgu, hspec), (dgw, hspec), (dgattn, aspec), (dgqd, hspec), (dgkd, hspec), (dggl, gspec)]
    dqn, dkn, dvv, dbB, dlB = _stage_bwd("gdn_intra_bwd", _gdn_intra_fn, gi_grid, intra_ins + [In(gtinv, aspec)], intra_cots)
    dcqkv, dtail, g_alog, g_dtb = _stage_bwd(
        "gdn_prep_bwd", prep_fn, (nrow,), prep_ins, [(t, rspec) for t in (dqn, dkn, dvv, dbB, dlB)])
    dproj, g_gconv_w = _conv_bwd("gdn_conv_bwd", proj, CONV_COL * d, gconv_w, dcqkv, taps=GDN_CONV, width=3 * d,
                                 tr=tr, tc=d, with_bias=False, dx_into=dproj)
    g_w_in_r = jnp.concatenate([_mm_tn("proj_dw", hn1, dproj, tr=tm, tka=d, tn=tn_in),
                                _mm_tn("proj_tail_dw", hn1, dtail, tr=tm, tka=d, tn=LANES)], axis=1)
    g_in4 = jnp.stack(_split4(_restore_w_in(g_w_in_r, cfg), 1))
    parts_mix = pair_reduce("mix", ["w_out", "w_in"], [g_w_out.reshape(N_CHIPS, d // N_CHIPS, d), g_in4])
    dhn1_tail = _mm("proj_tail_dx", dtail, w_tail.T, tm=tm, tn=d, tk=LANES)
    dhn1, slots_mix = _mm("proj_dx", dproj, w_main.T, tm=tm, tn=d, tk=tn_in // 2 if tn_in > LANES else LANES, add=dhn1_tail,
                          comm=_exchange_comm(parts_mix))
    dh0, g_norm1 = rms_bwd("rms1_bwd", h0, norm1, dhn1, dh1)

    grad_x = dh0[cfg.xrow:]
    small_grads = {
        "meta": dh0[cfg.front:cfg.xrow], "norm1": g_norm1, "gdn_conv_w": g_gconv_w[None],
        "gdn_a_log": g_alog[0:1, :hg], "gdn_dt_bias": g_dtb[0:1, :hg], "gdn_norm": g_gnorm, "norm2": g_norm2,
        "ffn_conv_w": g_fconv_w[None], "ffn_conv_b": g_fconv_b, "norm_f": g_normf.reshape(d),
    }

    small_names = [n for n, _ in PARAMS if n not in BIG]
    g_small = jnp.stack([_pack([small_grads[n] if axis[n] is None else _split4(small_grads[n], axis[n])[s]
                                for n in small_names], F32) for s in range(N_CHIPS)])
    parts_small = pair_reduce("small", ["small"], [g_small])
    slots_small = _run_comm("grad_exchange_small", _exchange_comm(parts_small))
    tags = ["w_in", "w_out", "w_ffn_up", "w_ffn_down", "small"]
    parts = [parts_mix[1], parts_mix[0], parts_ffn[1], parts_ffn[0], parts_small[0]]
    slots = [slots_mix[1], slots_mix[0], slots_ffn[1], slots_ffn[0], slots_small[0]]
    fins = [_chip_sum("grad_chip_sum_" + t, p, s, chip) for t, p, s in zip(tags, parts, slots)]
    sibs = _pair_swap(fins)

    def flat2(a):
        return a.reshape(-1, a.shape[-1])

    outs = {}
    for k, t in enumerate(BIG):
        res = _adamw("adamw_" + t, flat2(shard[t]), fins[k], sibs[k], flat2(m_shard[t]), flat2(v_shard[t]), cidx)
        outs[t] = [r.reshape(shard[t].shape) for r in res]
    small_shapes_all = [shard[n].shape for n in small_names]
    pk = lambda src: _pack([src[n] for n in small_names], F32)
    res = _adamw("adamw_small", pk(shard), fins[4], sibs[4], pk(m_shard), pk(v_shard), cidx)
    for k, r in enumerate(res):
        for n, a in zip(small_names, _unpack(r, small_shapes_all)):
            outs.setdefault(n, [None] * 4)[k] = a
    names = [n for n, _ in PARAMS]
    return (loss, grad_x[None], *[outs[n][k] for k in range(4) for n in names])


def kernel(x, meta, norm1, w_in, gdn_conv_w, gdn_a_log, gdn_dt_bias, gdn_norm, w_out, norm2, w_ffn_up, ffn_conv_w, ffn_conv_b, w_ffn_down, norm_f, loss_target, m_meta, m_norm1, m_w_in, m_gdn_conv_w, m_gdn_a_log, m_gdn_dt_bias, m_gdn_norm, m_w_out, m_norm2, m_w_ffn_up, m_ffn_conv_w, m_ffn_conv_b, m_w_ffn_down, m_norm_f, v_meta, v_norm1, v_w_in, v_gdn_conv_w, v_gdn_a_log, v_gdn_dt_bias, v_gdn_norm, v_w_out, v_norm2, v_w_ffn_up, v_ffn_conv_w, v_ffn_conv_b, v_w_ffn_down, v_norm_f):
    names = [n for n, _ in PARAMS]
    shard = dict(zip(names, (meta, norm1, w_in, gdn_conv_w, gdn_a_log, gdn_dt_bias, gdn_norm, w_out, norm2, w_ffn_up,
                             ffn_conv_w, ffn_conv_b, w_ffn_down, norm_f)))
    m_shard = dict(zip(names, (m_meta, m_norm1, m_w_in, m_gdn_conv_w, m_gdn_a_log, m_gdn_dt_bias, m_gdn_norm, m_w_out,
                               m_norm2, m_w_ffn_up, m_ffn_conv_w, m_ffn_conv_b, m_w_ffn_down, m_norm_f)))
    v_shard = dict(zip(names, (v_meta, v_norm1, v_w_in, v_gdn_conv_w, v_gdn_a_log, v_gdn_dt_bias, v_gdn_norm, v_w_out,
                               v_norm2, v_w_ffn_up, v_ffn_conv_w, v_ffn_conv_b, v_w_ffn_down, v_norm_f)))
    return _step(REAL, x[0], loss_target[0], shard, m_shard, v_shard)
```

```python
import functools
from typing import NamedTuple

import numpy as np
import jax
import jax.numpy as jnp
from jax import lax
from jax.experimental import pallas as pl
from jax.experimental.pallas import tpu as pltpu

F32 = jnp.float32
BF16 = jnp.bfloat16
EPS = 1e-6
CHUNK = 64
GDN_DK = 128
RET_DK = 256
GDN_CONV = 4
FFN_CONV = 3
ROPE_BASE = 10000.0
LANES = 128
N_CHIPS = 4
ADAM_LR, ADAM_B1, ADAM_B2, ADAM_EPS, ADAM_WD, ADAM_STEP = 0.001, 0.9, 0.999, 1e-08, 0.01, 10
MIX_COL, CONV_COL, RV_BLOCK, ROT_COL = 0, 4, 7, 8
MESH = pl.DeviceIdType.MESH
VMEM_LIMIT = 56 * 1024 * 1024


class Cfg(NamedTuple):
    d: int
    seq: int
    n_meta: int
    dff: int
    tr: int
    nb: int
    tm: int
    tf: int
    sc: int

    @property
    def hg(self): return self.d // GDN_DK
    @property
    def hr(self): return self.d // RET_DK
    @property
    def L(self): return self.n_meta + self.seq
    @property
    def rp(self): return -(-self.L // 256) * 256
    @property
    def front(self): return self.rp - self.L
    @property
    def xrow(self): return self.rp - self.seq
    @property
    def nch(self): return self.rp // CHUNK
    @property
    def din(self): return 10 * self.d + 2 * self.hg


REAL = Cfg(d=1024, seq=8192, n_meta=16, dff=2816, tr=256, nb=12, tm=1408, tf=1408, sc=4)


def _pallas(body, **kw):
    return pl.pallas_call(body, **kw)


def _sigmoid(x):
    return 1.0 / (1.0 + jnp.exp(-x))


def _silu(x):
    return x * _sigmoid(x)


def _softplus(x):
    return jnp.maximum(x, 0.0) + jnp.log(1.0 + jnp.exp(-jnp.abs(x)))


def _raw_dot(a, b, ta, tb, hi):
    if not hi:
        a = a.astype(BF16)
        b = b.astype(BF16)
    nbatch = a.ndim - 2
    ca = a.ndim - 2 if ta else a.ndim - 1
    cb = b.ndim - 1 if tb else b.ndim - 2
    batch = tuple(range(nbatch))
    return lax.dot_general(a, b, (((ca,), (cb,)), (batch, batch)),
                           precision=lax.Precision.HIGHEST if hi else None,
                           preferred_element_type=F32)


@functools.partial(jax.custom_vjp, nondiff_argnums=(2, 3, 4))
def _dot_p(a, b, ta, tb, hi):
    return _raw_dot(a, b, ta, tb, hi)


def _dot(a, b, ta=False, tb=False, hi=False):
    return _dot_p(a, b, ta, tb, hi)


def _dot_fwd(a, b, ta, tb, hi):
    return _raw_dot(a, b, ta, tb, hi), (a, b)


def _dot_bwd(ta, tb, hi, res, g):
    a, b = res
    if not ta and not tb:
        da, db = _dot(g, b, False, True, hi), _dot(a, g, True, False, hi)
    elif not ta and tb:
        da, db = _dot(g, b, False, False, hi), _dot(g, a, True, False, hi)
    elif ta and not tb:
        da, db = _dot(b, g, False, True, hi), _dot(a, g, False, False, hi)
    else:
        raise NotImplementedError
    return da.astype(a.dtype), db.astype(b.dtype)


_dot_p.defvjp(_dot_fwd, _dot_bwd)


def _iota2(n, m, axis):
    return lax.broadcasted_iota(jnp.int32, (n, m), axis)


def _bcast(mat, nb):
    return jnp.broadcast_to(mat[None], (nb,) + mat.shape)


def _split3(a):
    a0 = a.astype(BF16)
    r1 = a - a0.astype(F32)
    a1 = r1.astype(BF16)
    return a0, a1, (r1 - a1.astype(F32)).astype(BF16)


@functools.partial(jax.custom_vjp, nondiff_argnums=(2,))
def _dot_sel(a, e, te):
    eb = e.astype(BF16)
    p0, p1, p2 = (_raw_dot(p, eb, False, te, False) for p in _split3(a))
    return p0 + (p1 + p2)


def _dot_sel_fwd(a, e, te):
    return _dot_sel(a, e, te), e


def _dot_sel_bwd(te, e, g):
    return _dot_sel(g, e, not te), jnp.zeros_like(e)


_dot_sel.defvjp(_dot_sel_fwd, _dot_sel_bwd)


@jax.custom_vjp
def _sel_dot(e, x):
    eb = e.astype(BF16)
    p0, p1, p2 = (_raw_dot(eb, p, False, False, False) for p in _split3(x))
    return p0 + (p1 + p2)


def _sel_dot_fwd(e, x):
    return _sel_dot(e, x), e


def _sel_dot_bwd(e, g):
    eb = e.astype(BF16)
    p0, p1, p2 = (_raw_dot(eb, p, True, False, False) for p in _split3(g))
    return jnp.zeros_like(e), p0 + (p1 + p2)


_sel_dot.defvjp(_sel_dot_fwd, _sel_dot_bwd)


def _tri_inv_raw(m):
    nb = m.shape[0]
    r, c = _iota2(CHUNK, CHUNK, 0), _iota2(CHUNK, CHUNK, 1)
    t = _bcast((r == c).astype(F32), nb)
    b = 1
    while b < CHUNK:
        sh = b.bit_length() - 1
        off = ((r >> (sh + 1)) == (c >> (sh + 1))) & ((r >> sh) != (c >> sh)) & (r > c)
        cl = jnp.where(off[None], m, 0.0)
        t = t - _raw_dot(_raw_dot(t, cl, False, False, False), t, False, False, False)
        b *= 2
    return t


@jax.custom_vjp
def _tri_inv_given(m, t):
    return t


def _tri_inv_fwd(m, t):
    return t, t


def _tri_inv_bwd(t, g):
    return -_raw_dot(_raw_dot(t, g, True, False, False), t, False, True, False), jnp.zeros_like(t)


_tri_inv_given.defvjp(_tri_inv_fwd, _tri_inv_bwd)


def _rms(h, g):
    return h * lax.rsqrt(jnp.mean(h * h, axis=-1, keepdims=True) + EPS) * g


class In(NamedTuple):
    arr: jax.Array
    spec: pl.BlockSpec
    grad: object = None
    acc: bool = False
    gshape: object = None
    gspec: object = None
    galias: object = None


def _params(grid):
    sem = ("arbitrary",) * len(grid)
    return pltpu.CompilerParams(dimension_semantics=sem, vmem_limit_bytes=VMEM_LIMIT)


def _stage_fwd(name, fn, grid, ins, out_shapes, out_specs):
    n_in = len(ins)

    def body(*refs):
        pids = tuple(pl.program_id(k) for k in range(len(grid)))
        vals = [r[...].astype(F32) for r in refs[:n_in]]
        outs = fn(pids, *vals)
        for o_ref, o in zip(refs[n_in:], outs):
            o_ref[...] = o.reshape(o_ref.shape).astype(o_ref.dtype)

    return _pallas(
        body, out_shape=out_shapes, grid=grid, in_specs=[i.spec for i in ins],
        out_specs=out_specs, name=name, compiler_params=_params(grid))(*[i.arr for i in ins])


def _stage_bwd(name, fn, grid, ins, cots):
    n_in, n_ct = len(ins), len(cots)
    didx = [k for k, i in enumerate(ins) if i.grad is not None]
    aliased = [(o, ins[k].galias) for o, k in enumerate(didx) if ins[k].galias is not None]
    n_al = len(aliased)

    def body(*refs):
        pids = tuple(pl.program_id(k) for k in range(len(grid)))
        vals = [r[...].astype(F32) for r in refs[:n_in]]
        ct_refs = refs[n_in:n_in + n_ct]
        g_refs = refs[n_in + n_ct + n_al:]

        def f(*dv):
            merged = list(vals)
            for k, v in zip(didx, dv):
                merged[k] = v
            return tuple(fn(pids, *merged))

        outs, vjp_fn = jax.vjp(f, *[vals[k].astype(F32) for k in didx])
        cts = tuple(c[...].reshape(o.shape).astype(F32) for c, o in zip(ct_refs, outs))
        grads = vjp_fn(cts)
        first = functools.reduce(jnp.logical_and, [p == 0 for p in pids])
        for k, g_ref, g in zip(didx, g_refs, grads):
            if ins[k].acc:
                @pl.when(first)
                def _(g_ref=g_ref):
                    g_ref[...] = jnp.zeros(g_ref.shape, g_ref.dtype)
                g_ref[...] += g.reshape(g_ref.shape).astype(g_ref.dtype)
            else:
                g_ref[...] = g.reshape(g_ref.shape).astype(g_ref.dtype)

    out_shapes = [jax.ShapeDtypeStruct(ins[k].gshape or ins[k].arr.shape, ins[k].grad) for k in didx]
    out_specs = [ins[k].gspec or ins[k].spec for k in didx]
    return _pallas(
        body, out_shape=out_shapes, grid=grid,
        in_specs=[i.spec for i in ins] + [c[1] for c in cots] + [ANY] * n_al, out_specs=out_specs,
        input_output_aliases={n_in + n_ct + a: o for a, (o, _) in enumerate(aliased)},
        name=name, compiler_params=_params(grid))(*[i.arr for i in ins], *[c[0] for c in cots], *[a for _, a in aliased])


def _full(arr):
    nd = arr.ndim
    return pl.BlockSpec(arr.shape, lambda *p: (0,) * nd)


def _rows(tr, width, blk=0):
    return pl.BlockSpec((tr, width), lambda i: (i, blk))


def _mm(name, a, b, *, tm, tn, tk, out_dtype=F32, add=None, comm=None):
    M, K = a.shape
    N = b.shape[1] if b.ndim == 2 else b.shape[0] * b.shape[2]
    nk = K // tk
    grid = (M // tm, N // tn, nk)
    n_in = 3 if add is not None else 2
    n_ci, n_co = (len(comm.ins), len(comm.outs)) if comm is not None else (0, 0)

    def body(*refs):
        a_ref, b_ref = refs[0], refs[1]
        add_ref = refs[2] if add is not None else None
        c_ins = refs[n_in:n_in + n_ci]
        o_ref = refs[n_in + n_ci]
        c_outs = refs[n_in + n_ci + 1:n_in + n_ci + 1 + n_co]
        scratch = refs[n_in + n_ci + 1 + n_co:]
        acc_ref = scratch[0] if nk > 1 else None
        sems = scratch[1 if nk > 1 else 0:]
        step = (pl.program_id(0) * grid[1] + pl.program_id(1)) * nk + pl.program_id(2)
        if comm is not None:
            @pl.when(step == 0)
            def _():
                comm.start(c_ins, c_outs, sems)

        part = _raw_dot(a_ref[...], b_ref[...], False, False, False)

        def finish(total):
            if add_ref is not None:
                total = total + add_ref[...]
            o_ref[...] = total.astype(o_ref.dtype)

        if nk == 1:
            finish(part)
        else:
            k = pl.program_id(2)

            @pl.when(k == 0)
            def _():
                acc_ref[...] = part

            @pl.when(k > 0)
            def _():
                acc_ref[...] += part

            @pl.when(k == nk - 1)
            def _():
                finish(acc_ref[...])

        if comm is not None:
            @pl.when(step == grid[0] * grid[1] * nk - 1)
            def _():
                comm.finish(c_ins, c_outs, sems)

    b_spec = (pl.BlockSpec((tk, tn), lambda i, j, k: (k, j)) if b.ndim == 2 else
              pl.BlockSpec((None, tk, tn), lambda i, j, k: (j, k, 0)))
    in_specs = [pl.BlockSpec((tm, tk), lambda i, j, k: (i, k)), b_spec]
    args = [a, b]
    if add is not None:
        in_specs.append(pl.BlockSpec((tm, tn), lambda i, j, k: (i, j)))
        args.append(add)
    out_shape = jax.ShapeDtypeStruct((M, N), out_dtype)
    out_spec = pl.BlockSpec((tm, tn), lambda i, j, k: (i, j))
    scratch = [pltpu.VMEM((tm, tn), F32)] if nk > 1 else []
    if comm is None:
        return _pallas(body, out_shape=out_shape, grid=grid, in_specs=in_specs, out_specs=out_spec,
                       scratch_shapes=scratch, name=name, compiler_params=_params(grid))(*args)
    res = _pallas(body, out_shape=[out_shape] + comm.outs, grid=grid, in_specs=in_specs + [ANY] * n_ci,
                  out_specs=[out_spec] + [ANY] * n_co, scratch_shapes=scratch + comm.sems, name=name,
                  compiler_params=_params(grid))(*args, *comm.ins)
    return res[0], res[1:]


def _mm_tn(name, a, b, *, tr, tka, tn, blocked=False):
    R, Ka = a.shape
    N = b.shape[1]
    nr = R // tr
    grid = (Ka // tka, N // tn, nr)
    if blocked:
        out_shape = jax.ShapeDtypeStruct((N // tn, Ka, tn), F32)
        out_spec = pl.BlockSpec((None, tka, tn), lambda i, j, r: (j, i, 0))
    else:
        out_shape = jax.ShapeDtypeStruct((Ka, N), F32)
        out_spec = pl.BlockSpec((tka, tn), lambda i, j, r: (i, j))

    def body(a_ref, b_ref, o_ref):
        r = pl.program_id(2)
        part = _raw_dot(a_ref[...], b_ref[...], True, False, False)

        @pl.when(r == 0)
        def _():
            o_ref[...] = part

        @pl.when(r > 0)
        def _():
            o_ref[...] += part

    return _pallas(
        body, out_shape=out_shape, grid=grid,
        in_specs=[pl.BlockSpec((tr, tka), lambda i, j, r: (r, i)),
                  pl.BlockSpec((tr, tn), lambda i, j, r: (r, j))],
        out_specs=out_spec, name=name, compiler_params=_params(grid))(a, b)


def _conv_fwd(name, x, xcol0, w, b, *, taps, width, tr, tc):
    R = x.shape[0]
    grid = (width // tc, R // tr)
    cb0 = xcol0 // tc
    hrows = 16 if x.dtype == BF16 else 8
    hb = tr // hrows

    def body(*refs):
        x_ref, xp_ref, w_ref = refs[:3]
        b_ref = refs[3] if b is not None else None
        o_ref = refs[-1]
        i = pl.program_id(1)
        xv = x_ref[...].astype(F32)
        prev = jnp.where(i > 0, xp_ref[...].astype(F32)[hrows - 8:, :], 0.0)
        ext = jnp.concatenate([prev, xv], axis=0)
        acc = xv * w_ref[taps - 1:taps, :]
        for s in range(1, taps):
            acc = acc + pltpu.roll(ext, s, 0)[8:, :] * w_ref[taps - 1 - s:taps - s, :]
        if b_ref is not None:
            acc = acc + b_ref[...]
        o_ref[...] = acc.astype(o_ref.dtype)

    in_specs = [pl.BlockSpec((tr, tc), lambda j, i: (i, cb0 + j)),
                pl.BlockSpec((hrows, tc), lambda j, i: (jnp.maximum(i * hb - 1, 0), cb0 + j)),
                pl.BlockSpec((taps, tc), lambda j, i: (0, j))]
    args = [x, x, w]
    if b is not None:
        in_specs.append(pl.BlockSpec((1, tc), lambda j, i: (0, j)))
        args.append(b)
    return _pallas(
        body, out_shape=jax.ShapeDtypeStruct((R, width), BF16), grid=grid, in_specs=in_specs,
        out_specs=pl.BlockSpec((tr, tc), lambda j, i: (i, j)),
        name=name, compiler_params=_params(grid))(*args)


def _conv_bwd(name, x, xcol0, w, dy, *, taps, width, tr, tc, with_bias, dx_into=None):
    R = x.shape[0]
    nr = R // tr
    grid = (width // tc, nr)
    cb0 = xcol0 // tc
    hrows = 16 if dy.dtype == BF16 else 8
    hb = tr // hrows
    n_ext = tr + 8
    n_al = 0 if dx_into is None else 1

    def body(*refs):
        x_ref, w_ref, dy_ref, dyn_ref = refs[:4]
        dx_ref, dw_ref = refs[4 + n_al], refs[5 + n_al]
        db_ref = refs[6 + n_al] if with_bias else None
        i = pl.program_id(1)
        xv = x_ref[...].astype(F32)
        dyv = dy_ref[...].astype(F32)
        nxt = dyn_ref[...].astype(F32)[:8, :]
        dext = jnp.concatenate([dyv, jnp.where(i < nr - 1, nxt, 0.0)], axis=0)
        dx = dyv * w_ref[taps - 1:taps, :]
        dws = [None] * taps
        dws[taps - 1] = jnp.sum(xv * dyv, axis=0, keepdims=True)
        for s in range(1, taps):
            ahead = pltpu.roll(dext, n_ext - s, 0)[:tr, :]
            dx = dx + ahead * w_ref[taps - 1 - s:taps - s, :]
            dws[taps - 1 - s] = jnp.sum(xv * ahead, axis=0, keepdims=True)
        dx_ref[...] = dx.astype(dx_ref.dtype)

        @pl.when(i == 0)
        def _():
            for k in range(taps):
                dw_ref[k:k + 1, :] = dws[k]
            if db_ref is not None:
                db_ref[...] = jnp.sum(dyv, axis=0, keepdims=True)

        @pl.when(i > 0)
        def _():
            for k in range(taps):
                dw_ref[k:k + 1, :] += dws[k]
            if db_ref is not None:
                db_ref[...] += jnp.sum(dyv, axis=0, keepdims=True)

    in_specs = [pl.BlockSpec((tr, tc), lambda j, i: (i, cb0 + j)),
                pl.BlockSpec((taps, tc), lambda j, i: (0, j)),
                pl.BlockSpec((tr, tc), lambda j, i: (i, j)),
                pl.BlockSpec((hrows, tc), lambda j, i: (jnp.minimum((i + 1) * hb, R // hrows - 1), j))]
    args = [x, w, dy, dy]
    if dx_into is None:
        dx_shape, dx_spec, aliases = jax.ShapeDtypeStruct((R, width), BF16), pl.BlockSpec((tr, tc), lambda j, i: (i, j)), {}
    else:
        dx_shape = jax.ShapeDtypeStruct(dx_into.shape, dx_into.dtype)
        dx_spec, aliases = pl.BlockSpec((tr, tc), lambda j, i: (i, cb0 + j)), {4: 0}
        in_specs.append(ANY)
        args.append(dx_into)
    out_shape = [dx_shape, jax.ShapeDtypeStruct((taps, width), F32)]
    out_specs = [dx_spec, pl.BlockSpec((taps, tc), lambda j, i: (0, j))]
    if with_bias:
        out_shape.append(jax.ShapeDtypeStruct((1, width), F32))
        out_specs.append(pl.BlockSpec((1, tc), lambda j, i: (0, j)))
    return _pallas(
        body, out_shape=out_shape, grid=grid, in_specs=in_specs, out_specs=out_specs, input_output_aliases=aliases,
        name=name, compiler_params=_params(grid))(*args)


def _row_mask(cfg, i, tr):
    rows = i * tr + lax.broadcasted_iota(jnp.int32, (tr, 1), 0)
    return (rows >= cfg.front).astype(F32)


def _make_rms_fn(cfg, tr, with_residual):
    def fn(pids, h, g):
        hm = h * _row_mask(cfg, pids[0], tr)
        if with_residual:
            return _rms(hm, g), hm
        return (_rms(hm, g),)
    return fn


def _make_gdn_prep_fn(cfg, tr):
    d, hg = cfg.d, cfg.hg

    def fn(pids, c, tail, alog, dtb):
        cq, ck, cv = c[:, :d], c[:, d:2 * d], c[:, 2 * d:]
        mask = _row_mask(cfg, pids[0], tr)
        j, col = _iota2(LANES, d, 0), _iota2(LANES, d, 1)
        ea = ((col >> 7) == j).astype(F32)
        eb = ((col >> 7) + hg == j).astype(F32)
        al = jnp.sum(alog, axis=0, keepdims=True)
        db = jnp.sum(dtb, axis=0, keepdims=True)
        lg = _dot_sel(-jnp.exp(al) * _softplus(tail + db) * mask, ea, False)
        beta = _dot_sel(_sigmoid(tail) * mask, eb, False)
        sq, sk, sv = _silu(cq), _silu(ck), _silu(cv)
        qs, ks = [], []
        for h in range(hg):
            sl = slice(h * GDN_DK, (h + 1) * GDN_DK)
            qh, kh = sq[:, sl], sk[:, sl]
            qs.append(qh * lax.rsqrt(jnp.sum(qh * qh, axis=-1, keepdims=True) + EPS) * (GDN_DK ** -0.5))
            ks.append(kh * lax.rsqrt(jnp.sum(kh * kh, axis=-1, keepdims=True) + EPS))
        return jnp.concatenate(qs, axis=1), jnp.concatenate(ks, axis=1), sv, beta, lg
    return fn


def _gdn_intra_fn(pids, q, k, v, bB, lB, t_saved=None):
    rows = q.shape[0]
    nb = rows // CHUNK
    q3, k3, v3, b3, l3 = [t.reshape(nb, CHUNK, GDN_DK) for t in (q, k, v, bB, lB)]
    r, c = _iota2(CHUNK, CHUNK, 0), _iota2(CHUNK, CHUNK, 1)
    tril = (r >= c)
    strict = (r > c)
    gcol = _sel_dot(_bcast(tril.astype(F32), nb), l3)
    grow = jnp.swapaxes(gcol, 1, 2)[:, :CHUNK, :]
    diff = gcol[:, :, :CHUNK] - grow
    decay = jnp.where(tril[None], jnp.exp(jnp.where(tril[None], diff, 0.0)), 0.0)
    kb = k3 * b3
    m = jnp.where(strict[None], _dot(kb, k3, False, True) * decay, 0.0)
    t = _tri_inv_raw(m) if t_saved is None else _tri_inv_given(m, t_saved.reshape(nb, CHUNK, CHUNK))
    eg = jnp.exp(gcol)
    u = _dot(t, v3 * b3)
    w = _dot(t, kb * eg)
    attn = _dot(q3, k3, False, True) * decay
    qd = q3 * eg
    glast = jnp.sum(l3, axis=1, keepdims=True)
    kd = k3 * jnp.exp(glast - gcol)
    gl = jnp.exp(glast)
    outs = (u.reshape(rows, GDN_DK), w.reshape(rows, GDN_DK), attn.reshape(1, rows, CHUNK),
            qd.reshape(rows, GDN_DK), kd.reshape(rows, GDN_DK), gl.reshape(1, nb, 1, GDN_DK))
    return outs + (t.reshape(1, rows, CHUNK),) if t_saved is None else outs


def _make_rot_fn(cfg):
    hr = cfg.hr
    half = RET_DK // 2

    def fn(pids, rqk, cos, sin):
        rq, rk = rqk[:, :cfg.d], rqk[:, cfg.d:]

        def rot(t, scale):
            outs = []
            for h in range(hr):
                x1 = t[:, h * RET_DK:h * RET_DK + half]
                x2 = t[:, h * RET_DK + half:(h + 1) * RET_DK]
                outs += [(x1 * cos - x2 * sin) * scale, (x2 * cos + x1 * sin) * scale]
            return jnp.concatenate(outs, axis=1)
        return rot(rq, 1.0), rot(rk, RET_DK ** -0.5)
    return fn


def _make_mix_fn(cfg):
    hg, hr = cfg.hg, cfg.hr

    def fn(pids, oa, ob, pm, gnorm):
        d = cfg.d
        gz, rg, gate_a, gate_b = pm[:, :d], pm[:, d:2 * d], pm[:, 2 * d:3 * d], pm[:, 3 * d:]
        oas = []
        for h in range(hg):
            oh = oa[:, h * GDN_DK:(h + 1) * GDN_DK]
            oas.append(oh * lax.rsqrt(jnp.mean(oh * oh, axis=-1, keepdims=True) + EPS) * gnorm)
        ya = jnp.concatenate(oas, axis=1) * _silu(gz)
        obs = []
        for h in range(hr):
            oh = ob[:, h * RET_DK:(h + 1) * RET_DK]
            obs.append(oh * lax.rsqrt(jnp.mean(oh * oh, axis=-1, keepdims=True) + EPS))
        yb = _silu(rg) * jnp.concatenate(obs, axis=1)
        return (_sigmoid(gate_a) * ya + _sigmoid(gate_b) * yb,)
    return fn


def _act_fn(pids, u):
    f = u.shape[1] // 2
    return (_silu(u[:, :f]) * u[:, f:],)


def _gdn_step(s, u, w, a, qd, kd, gl):
    top = _dot(jnp.concatenate([w, qd], axis=0), s)
    v_new = u - top[:CHUNK]
    bot = _dot(jnp.concatenate([a, kd.T], axis=0), v_new)
    o = top[CHUNK:] + bot[:CHUNK]
    s2 = s * gl + bot[CHUNK:]
    return s2, o


def _ret_step(s, q, k, v, dm, qdc, kdc, g):
    att = _dot(q, k, False, True) * dm
    bot = _dot(jnp.concatenate([att, (k * kdc).T], axis=0), v)
    o = bot[:CHUNK] + _dot(q * qdc, s)
    s2 = s * g + bot[CHUNK:]
    return s2, o


class Part(NamedTuple):
    body: object
    args: list
    in_specs: list
    out_shape: list
    out_specs: list
    scratch: list
    aliases: dict = {}


def _run_parts(name, grid, parts):
    n_in = [len(p.args) for p in parts]
    n_out = [len(p.out_shape) for p in parts]
    n_sc = [len(p.scratch) for p in parts]
    off_in = [sum(n_in[:k]) for k in range(len(parts))]
    off_out = [sum(n_out[:k]) for k in range(len(parts))]
    off_sc = [sum(n_sc[:k]) for k in range(len(parts))]

    def body(*refs):
        ins, outs, scr = refs[:sum(n_in)], refs[sum(n_in):sum(n_in) + sum(n_out)], refs[sum(n_in) + sum(n_out):]
        for k, p in enumerate(parts):
            p.body(*ins[off_in[k]:off_in[k] + n_in[k]], *outs[off_out[k]:off_out[k] + n_out[k]],
                   *scr[off_sc[k]:off_sc[k] + n_sc[k]])

    aliases = {off_in[k] + i: off_out[k] + o for k, p in enumerate(parts) for i, o in p.aliases.items()}
    res = _pallas(
        body, out_shape=sum((p.out_shape for p in parts), []), grid=grid, in_specs=sum((p.in_specs for p in parts), []),
        out_specs=sum((p.out_specs for p in parts), []), scratch_shapes=sum((p.scratch for p in parts), []),
        input_output_aliases=aliases, name=name, compiler_params=_params(grid))(*sum((p.args for p in parts), []))
    return [res[off_out[k]:off_out[k] + n_out[k]] for k in range(len(parts))]


def _gdn_scan_fwd(cfg, u, w, attn, qd, kd, gl):
    d, hg, nch, sc = cfg.d, cfg.hg, cfg.nch, cfg.sc
    nst = nch // sc

    def body(u_ref, w_ref, a_ref, qd_ref, kd_ref, gl_ref, o_ref, ss_ref, s_ref):
        @pl.when(pl.program_id(0) == 0)
        def _():
            s_ref[...] = jnp.zeros(s_ref.shape, F32)

        states = [s_ref[h] for h in range(hg)]
        for j in range(sc):
            rows = slice(j * CHUNK, (j + 1) * CHUNK)
            outs = []
            for h in range(hg):
                sl = slice(h * GDN_DK, (h + 1) * GDN_DK)
                ss_ref[j, h] = states[h]
                states[h], o = _gdn_step(states[h], u_ref[rows, sl], w_ref[rows, sl], a_ref[h, rows, :],
                                         qd_ref[rows, sl], kd_ref[rows, sl], gl_ref[h, j])
                outs.append(o)
            o_ref[rows, :] = jnp.concatenate(outs, axis=1)
        for h in range(hg):
            s_ref[h] = states[h]

    row = pl.BlockSpec((sc * CHUNK, d), lambda n: (n, 0))
    return Part(
        body, [u, w, attn, qd, kd, gl],
        [row, row, pl.BlockSpec((hg, sc * CHUNK, CHUNK), lambda n: (0, n, 0)), row, row,
         pl.BlockSpec((hg, sc, 1, GDN_DK), lambda n: (0, n, 0, 0))],
        [jax.ShapeDtypeStruct((cfg.rp, d), F32), jax.ShapeDtypeStruct((nch, hg, GDN_DK, GDN_DK), F32)],
        [row, pl.BlockSpec((sc, hg, GDN_DK, GDN_DK), lambda n: (n, 0, 0, 0))],
        [pltpu.VMEM((hg, GDN_DK, GDN_DK), F32)])


def _gdn_scan_bwd(cfg, do, u, w, attn, qd, kd, gl, ss):
    d, hg, nch, sc = cfg.d, cfg.hg, cfg.nch, cfg.sc
    nst = nch // sc

    def body(do_ref, u_ref, w_ref, a_ref, qd_ref, kd_ref, gl_ref, ss_ref,
             du_ref, dw_ref, da_ref, dqd_ref, dkd_ref, dgl_ref, ds_ref):
        @pl.when(pl.program_id(0) == 0)
        def _():
            ds_ref[...] = jnp.zeros(ds_ref.shape, F32)

        dstates = [ds_ref[h] for h in range(hg)]
        for j in reversed(range(sc)):
            rows = slice(j * CHUNK, (j + 1) * CHUNK)
            dus, dws, dqds, dkds = [], [], [], []
            for h in range(hg):
                sl = slice(h * GDN_DK, (h + 1) * GDN_DK)
                args = (ss_ref[j, h], u_ref[rows, sl], w_ref[rows, sl], a_ref[h, rows, :], qd_ref[rows, sl],
                        kd_ref[rows, sl], gl_ref[h, j])
                _, vjp_fn = jax.vjp(_gdn_step, *args)
                dstates[h], du, dw, da, dqd, dkd, dgl = vjp_fn((dstates[h], do_ref[rows, sl]))
                da_ref[h, rows, :] = da
                dgl_ref[h, j] = dgl
                dus.append(du)
                dws.append(dw)
                dqds.append(dqd)
                dkds.append(dkd)
            du_ref[rows, :] = jnp.concatenate(dus, axis=1)
            dw_ref[rows, :] = jnp.concatenate(dws, axis=1)
            dqd_ref[rows, :] = jnp.concatenate(dqds, axis=1)
            dkd_ref[rows, :] = jnp.concatenate(dkds, axis=1)
        for h in range(hg):
            ds_ref[h] = dstates[h]

    row = pl.BlockSpec((sc * CHUNK, d), lambda n: (nst - 1 - n, 0))
    aspec = pl.BlockSpec((hg, sc * CHUNK, CHUNK), lambda n: (0, nst - 1 - n, 0))
    gspec = pl.BlockSpec((hg, sc, 1, GDN_DK), lambda n: (0, nst - 1 - n, 0, 0))
    rowshape = jax.ShapeDtypeStruct((cfg.rp, d), F32)
    return Part(
        body, [do, u, w, attn, qd, kd, gl, ss],
        [row, row, row, aspec, row, row, gspec, pl.BlockSpec((sc, hg, GDN_DK, GDN_DK), lambda n: (nst - 1 - n, 0, 0, 0))],
        [rowshape, rowshape, jax.ShapeDtypeStruct(attn.shape, F32), rowshape, rowshape, jax.ShapeDtypeStruct(gl.shape, F32)],
        [row, row, aspec, row, row, gspec],
        [pltpu.VMEM((hg, GDN_DK, GDN_DK), F32)])


def _ret_consts(cfg):
    hr = cfg.hr
    lg = np.log(1.0 - 2.0 ** (-5.0 - np.arange(hr, dtype=np.float64)))
    idx = np.arange(CHUNK, dtype=np.float64)
    tril = np.tril(np.ones((CHUNK, CHUNK), dtype=bool))
    dm = np.where(tril[None], np.exp((idx[:, None] - idx[None, :])[None] * lg[:, None, None]), 0.0)
    qdc = np.exp((idx[None, :] + 1.0) * lg[:, None])
    kdc = np.exp((CHUNK - 1.0 - idx[None, :]) * lg[:, None])
    gch = np.exp(CHUNK * lg)
    qdc = np.broadcast_to(qdc[:, :, None], (hr, CHUNK, RET_DK))
    kdc = np.broadcast_to(kdc[:, :, None], (hr, CHUNK, RET_DK))
    gch = np.broadcast_to(gch[:, None, None], (hr, 1, RET_DK))
    return tuple(jnp.asarray(np.ascontiguousarray(t), F32) for t in (dm, qdc, kdc, gch))


def _ret_scan_fwd(cfg, qr, kr, proj, consts):
    d, hr, nch, sc = cfg.d, cfg.hr, cfg.nch, cfg.sc
    nst = nch // sc
    dm, qdc, kdc, gch = consts

    def body(q_ref, k_ref, v_ref, dm_ref, qdc_ref, kdc_ref, g_ref, o_ref, ss_ref, s_ref):
        @pl.when(pl.program_id(0) == 0)
        def _():
            s_ref[...] = jnp.zeros(s_ref.shape, F32)

        states = [s_ref[h] for h in range(hr)]
        for j in range(sc):
            rows = slice(j * CHUNK, (j + 1) * CHUNK)
            outs = []
            for h in range(hr):
                sl = slice(h * RET_DK, (h + 1) * RET_DK)
                ss_ref[j, h] = states[h]
                states[h], o = _ret_step(states[h], q_ref[rows, sl], k_ref[rows, sl], v_ref[rows, sl], dm_ref[h],
                                         qdc_ref[h], kdc_ref[h], g_ref[h])
                outs.append(o)
            o_ref[rows, :] = jnp.concatenate(outs, axis=1)
        for h in range(hr):
            s_ref[h] = states[h]

    row = pl.BlockSpec((sc * CHUNK, d), lambda n: (n, 0))
    return Part(
        body, [qr, kr, proj, dm, qdc, kdc, gch],
        [row, row, pl.BlockSpec((sc * CHUNK, d), lambda n: (n, RV_BLOCK)), _full(dm), _full(qdc), _full(kdc), _full(gch)],
        [jax.ShapeDtypeStruct((cfg.rp, d), F32), jax.ShapeDtypeStruct((nch, hr, RET_DK, RET_DK), F32)],
        [row, pl.BlockSpec((sc, hr, RET_DK, RET_DK), lambda n: (n, 0, 0, 0))],
        [pltpu.VMEM((hr, RET_DK, RET_DK), F32)])


def _ret_scan_bwd(cfg, do, qr, kr, proj, consts, ss, dproj):
    d, hr, nch, sc = cfg.d, cfg.hr, cfg.nch, cfg.sc
    nst = nch // sc
    dm, qdc, kdc, gch = consts

    def body(do_ref, q_ref, k_ref, v_ref, dm_ref, qdc_ref, kdc_ref, g_ref, ss_ref, _, dq_ref, dk_ref, dv_ref, ds_ref):
        @pl.when(pl.program_id(0) == 0)
        def _():
            ds_ref[...] = jnp.zeros(ds_ref.shape, F32)

        dstates = [ds_ref[h] for h in range(hr)]
        for j in reversed(range(sc)):
            rows = slice(j * CHUNK, (j + 1) * CHUNK)
            dqs, dks, dvs = [], [], []
            for h in range(hr):
                sl = slice(h * RET_DK, (h + 1) * RET_DK)
                cs = (dm_ref[h], qdc_ref[h], kdc_ref[h], g_ref[h])
                _, vjp_fn = jax.vjp(lambda s, q, k, v, cs=cs: _ret_step(s, q, k, v, *cs),
                                    ss_ref[j, h], q_ref[rows, sl], k_ref[rows, sl], v_ref[rows, sl])
                dstates[h], dq, dk, dv = vjp_fn((dstates[h], do_ref[rows, sl]))
                dqs.append(dq)
                dks.append(dk)
                dvs.append(dv)
            dq_ref[rows, :] = jnp.concatenate(dqs, axis=1)
            dk_ref[rows, :] = jnp.concatenate(dks, axis=1)
            dv_ref[rows, :] = jnp.concatenate(dvs, axis=1).astype(dv_ref.dtype)
        for h in range(hr):
            ds_ref[h] = dstates[h]

    row = pl.BlockSpec((sc * CHUNK, d), lambda n: (nst - 1 - n, 0))
    rowshape = jax.ShapeDtypeStruct((cfg.rp, d), F32)
    vspec = pl.BlockSpec((sc * CHUNK, d), lambda n: (nst - 1 - n, RV_BLOCK))
    return Part(
        body, [do, qr, kr, proj, dm, qdc, kdc, gch, ss, dproj],
        [row, row, row, vspec, _full(dm), _full(qdc), _full(kdc), _full(gch),
         pl.BlockSpec((sc, hr, RET_DK, RET_DK), lambda n: (nst - 1 - n, 0, 0, 0)), ANY],
        [rowshape, rowshape, jax.ShapeDtypeStruct(dproj.shape, dproj.dtype)],
        [row, row, vspec],
        [pltpu.VMEM((hr, RET_DK, RET_DK), F32)], {9: 2})


def _final(cfg, h2, normf, tgt):
    d, tr = cfg.d, cfg.xrow
    nr = cfg.rp // tr

    def body(h_ref, g_ref, t_ref, dh_ref, dg_ref, loss_ref):
        i = pl.program_id(0)
        y, vjp_fn = jax.vjp(_rms, h_ref[...], g_ref[...])
        err = jnp.where(i >= 1, y - t_ref[...], 0.0)
        dh, dg = vjp_fn(err * (1.0 / d))
        dh_ref[...] = dh
        part = jnp.zeros((8, LANES), F32) + 0.5 * jnp.sum(err * err) * (1.0 / d)

        @pl.when(i == 0)
        def _():
            dg_ref[...] = dg
            loss_ref[...] = part

        @pl.when(i > 0)
        def _():
            dg_ref[...] += dg
            loss_ref[...] += part

    return _pallas(
        body,
        out_shape=[jax.ShapeDtypeStruct((cfg.rp, d), F32), jax.ShapeDtypeStruct((1, d), F32),
                   jax.ShapeDtypeStruct((8, LANES), F32)],
        grid=(nr,),
        in_specs=[_rows(tr, d), _full(normf), pl.BlockSpec((tr, d), lambda i: (jnp.maximum(i - 1, 0), 0))],
        out_specs=[_rows(tr, d), pl.BlockSpec((1, d), lambda i: (0, 0)), pl.BlockSpec((8, LANES), lambda i: (0, 0))],
        name="final_loss", compiler_params=_params((nr,)))(h2, normf, tgt)


ANY = pl.BlockSpec(memory_space=pl.ANY)


def _place():
    x, y, c = lax.axis_index("x"), lax.axis_index("y"), lax.axis_index("c")
    others = [(1 - x, y), (x, 1 - y), (1 - x, 1 - y)]
    return x, y, c, others


def _row_tile(rows, cap=256):
    return max(t for t in range(16, min(rows, cap) + 1, 16) if rows % t == 0)


class Comm(NamedTuple):
    ins: list
    outs: list
    sems: list
    start: object
    finish: object


def _run_comm(name, comm):
    n_in, n_out = len(comm.ins), len(comm.outs)

    def body(*refs):
        ins, outs, sems = refs[:n_in], refs[n_in:n_in + n_out], refs[n_in + n_out:]
        comm.start(ins, outs, sems)
        comm.finish(ins, outs, sems)

    return _pallas(body, out_shape=comm.outs, in_specs=[ANY] * n_in, out_specs=[ANY] * n_out,
                   scratch_shapes=comm.sems, name=name)(*comm.ins)


def _gather_comm(ws):
    n = len(ws)
    halves = [w.shape[0] // 2 for w in ws]

    def copies(w_refs, o_refs, sems):
        send_sems, recv_sems = sems
        x, y, c, others = _place()
        me = 2 * x + y
        chips = [2 * px + py for px, py in others]

        def piece(a, chip, core):
            return o_refs[a].at[chip, pl.ds(core * halves[a], halves[a]), :]

        def copy(a, k, src, chip, core, to):
            return pltpu.make_async_remote_copy(src_ref=src, dst_ref=piece(a, chip, core), send_sem=send_sems.at[6 * a + k],
                                                recv_sem=recv_sems.at[6 * a + k], device_id=to, device_id_type=MESH)

        def first(j, a):
            return copy(a, j, w_refs[a].at[pl.ds(c * halves[a], halves[a]), :], me, c, (*others[j], c))

        def landed(j, a):
            return copy(a, j, piece(a, chips[j], c), chips[j], c, (x, y, c))

        def passed(j, a):
            return copy(a, 3 + j, piece(a, chips[j], c), chips[j], c, (x, y, 1 - c))

        def from_sibling(j, a):
            return copy(a, 3 + j, piece(a, chips[j], 1 - c), chips[j], 1 - c, (x, y, c))

        return first, landed, passed, from_sibling

    pairs = [(j, a) for j in range(3) for a in range(n)]

    def start(w_refs, o_refs, sems):
        first, _, _, _ = copies(w_refs, o_refs, sems)
        for j, a in pairs:
            first(j, a).start()

    def finish(w_refs, o_refs, sems):
        first, landed, passed, from_sibling = copies(w_refs, o_refs, sems)
        for j, a in pairs:
            landed(j, a).wait_recv()
            passed(j, a).start()
        for j, a in pairs:
            from_sibling(j, a).wait_recv()
        for j, a in pairs:
            first(j, a).wait_send()
            passed(j, a).wait_send()

    return Comm(list(ws), [jax.ShapeDtypeStruct((N_CHIPS,) + w.shape, w.dtype) for w in ws],
                [pltpu.SemaphoreType.DMA((6 * n,)), pltpu.SemaphoreType.DMA((6 * n,))], start, finish)


def _pair_exchange(name, gs):
    n = len(gs)

    def body(*refs):
        g_refs, o_refs = refs[:n], refs[n:2 * n]
        send_sems, recv_sems = refs[2 * n:]
        x, y, c, _ = _place()
        cps = []
        for a in range(n):
            half = gs[a].shape[1] // 2
            cp = pltpu.make_async_remote_copy(
                src_ref=g_refs[a].at[:, pl.ds((1 - c) * half, half), :], dst_ref=o_refs[a], send_sem=send_sems.at[a],
                recv_sem=recv_sems.at[a], device_id=(x, y, 1 - c), device_id_type=MESH)
            cp.start()
            cps.append(cp)
        for cp in cps:
            cp.wait()

    return _pallas(
        body, out_shape=[jax.ShapeDtypeStruct((N_CHIPS, g.shape[1] // 2, g.shape[2]), g.dtype) for g in gs],
        in_specs=[ANY] * n, out_specs=[ANY] * n,
        scratch_shapes=[pltpu.SemaphoreType.DMA((n,)), pltpu.SemaphoreType.DMA((n,))], name=name)(*gs)


def _pair_sum(name, g, recv, cidx):
    half, cols = recv.shape[1], recv.shape[2]
    tr = _row_tile(half)
    nblk = half // tr

    def body(c_ref, g_ref, r_ref, o_ref):
        o_ref[...] = (g_ref[...] + r_ref[...]).astype(o_ref.dtype)

    grid_spec = pltpu.PrefetchScalarGridSpec(
        num_scalar_prefetch=1, grid=(N_CHIPS, nblk),
        in_specs=[pl.BlockSpec((1, tr, cols), lambda s, i, c: (s, c[0] * nblk + i, 0)),
                  pl.BlockSpec((1, tr, cols), lambda s, i, c: (s, i, 0))],
        out_specs=pl.BlockSpec((1, tr, cols), lambda s, i, c: (s, i, 0)))
    return _pallas(
        body, out_shape=jax.ShapeDtypeStruct((N_CHIPS, half, cols), BF16), grid_spec=grid_spec,
        name=name, compiler_params=_params((N_CHIPS, nblk)))(cidx, g, recv)


def _exchange_comm(parts):
    n = len(parts)

    def copies(p_refs, o_refs, sems):
        send_sems, recv_sems = sems
        x, y, c, others = _place()
        me = 2 * x + y

        def copy(a, j, src_chip, dst_chip):
            px, py = others[j]
            return pltpu.make_async_remote_copy(
                src_ref=p_refs[a].at[src_chip], dst_ref=o_refs[a].at[dst_chip], send_sem=send_sems.at[3 * a + j],
                recv_sem=recv_sems.at[3 * a + j], device_id=(px, py, c), device_id_type=MESH)

        def send(j, a):
            return copy(a, j, 2 * others[j][0] + others[j][1], me)

        def arrival(j, a):
            return copy(a, j, me, 2 * others[j][0] + others[j][1])

        return send, arrival

    pairs = [(j, a) for j in range(3) for a in range(n)]

    def start(p_refs, o_refs, sems):
        send, _ = copies(p_refs, o_refs, sems)
        for j, a in pairs:
            send(j, a).start()

    def finish(p_refs, o_refs, sems):
        send, arrival = copies(p_refs, o_refs, sems)
        for j, a in pairs:
            arrival(j, a).wait_recv()
        for j, a in pairs:
            send(j, a).wait_send()

    return Comm(list(parts), [jax.ShapeDtypeStruct(p.shape, p.dtype) for p in parts],
                [pltpu.SemaphoreType.DMA((3 * n,)), pltpu.SemaphoreType.DMA((3 * n,))], start, finish)


def _chip_sum(name, part, slots, chip):
    half, cols = slots.shape[1], slots.shape[2]
    tr = _row_tile(half)

    def body(me_ref, p_ref, *rest):
        s_refs, o_ref = rest[:N_CHIPS], rest[N_CHIPS]
        own = p_ref[...].astype(F32)
        v = [jnp.where(me_ref[0] == k, own, s_refs[k][...].astype(F32)) for k in range(N_CHIPS)]
        o_ref[...] = ((v[0] + v[1]) + v[2]) + v[3]

    def slot_spec(k):
        return pl.BlockSpec((None, tr, cols), lambda i, me: (jnp.where(me[0] == k, (k + 1) % N_CHIPS, k), i, 0))

    grid_spec = pltpu.PrefetchScalarGridSpec(
        num_scalar_prefetch=1, grid=(half // tr,),
        in_specs=[pl.BlockSpec((None, tr, cols), lambda i, me: (me[0], i, 0))] + [slot_spec(k) for k in range(N_CHIPS)],
        out_specs=pl.BlockSpec((tr, cols), lambda i, me: (i, 0)))
    return _pallas(
        body, out_shape=jax.ShapeDtypeStruct((half, cols), F32), grid_spec=grid_spec,
        name=name, compiler_params=_params((half // tr,)))(chip, part, *([slots] * N_CHIPS))


def _pair_swap(fins):
    n = len(fins)

    def body(*refs):
        f_refs, o_refs = refs[:n], refs[n:2 * n]
        send_sems, recv_sems = refs[2 * n:]
        x, y, c, _ = _place()
        cps = [pltpu.make_async_remote_copy(src_ref=f_refs[a], dst_ref=o_refs[a], send_sem=send_sems.at[a],
                                            recv_sem=recv_sems.at[a], device_id=(x, y, 1 - c), device_id_type=MESH)
               for a in range(n)]
        for cp in cps:
            cp.start()
        for cp in cps:
            cp.wait()

    return _pallas(
        body, out_shape=[jax.ShapeDtypeStruct(f.shape, f.dtype) for f in fins], in_specs=[ANY] * n, out_specs=[ANY] * n,
        scratch_shapes=[pltpu.SemaphoreType.DMA((n,)), pltpu.SemaphoreType.DMA((n,))], name="grad_pair_swap")(*fins)


def _adamw(name, w, g_own, g_other, m, v, cidx):
    R, cols = w.shape[-2:]
    lead = (None,) * (w.ndim - 2)
    zeros = (0,) * (w.ndim - 2)
    half = R // 2
    tr = _row_tile(half, 128)
    nblk = half // tr
    c1 = 1.0 - ADAM_B1 ** ADAM_STEP
    c2 = 1.0 - ADAM_B2 ** ADAM_STEP

    def body(c_ref, w_ref, go_ref, gs_ref, m_ref, v_ref, g_ref, d_ref, nm_ref, nv_ref):
        mine = (pl.program_id(0) // nblk) == c_ref[0]
        gv = jnp.where(mine, go_ref[...], gs_ref[...])
        nm = ADAM_B1 * m_ref[...] + (1.0 - ADAM_B1) * gv
        nv = ADAM_B2 * v_ref[...] + (1.0 - ADAM_B2) * (gv * gv)
        g_ref[...] = gv
        d_ref[...] = -ADAM_LR * ((nm / c1) / (jnp.sqrt(nv / c2) + ADAM_EPS) + ADAM_WD * w_ref[...])
        nm_ref[...] = nm
        nv_ref[...] = nv

    spec = pl.BlockSpec(lead + (tr, cols), lambda i, c: zeros + (i, 0))
    hspec = pl.BlockSpec((tr, cols), lambda i, c: (i % nblk, 0))
    shape = jax.ShapeDtypeStruct(w.shape, F32)
    grid_spec = pltpu.PrefetchScalarGridSpec(num_scalar_prefetch=1, grid=(R // tr,),
                                             in_specs=[spec, hspec, hspec, spec, spec], out_specs=[spec] * 4)
    return _pallas(
        body, out_shape=[shape] * 4, grid_spec=grid_spec,
        name=name, compiler_params=_params((R // tr,)))(cidx, w, g_own, g_other, m, v)


PARAMS = (("meta", 1), ("norm1", None), ("w_in", 2), ("gdn_conv_w", 2), ("gdn_a_log", None), ("gdn_dt_bias", None),
          ("gdn_norm", None), ("w_out", 1), ("norm2", None), ("w_ffn_up", 2), ("ffn_conv_w", 2), ("ffn_conv_b", None),
          ("w_ffn_down", 1), ("norm_f", None))
BIG = ("w_in", "w_out", "w_ffn_up", "w_ffn_down")
PACK_ALIGN = 1024
PACK_ROWS_ALIGN = 32


def _pack(arrs, dtype):
    parts, total = [], 0
    for a in arrs:
        f = a.reshape(-1).astype(dtype)
        pad = (-f.shape[0]) % PACK_ALIGN
        parts.append(jnp.pad(f, (0, pad)) if pad else f)
        total += f.shape[0] + pad
    rows = total // LANES
    rpad = (-rows) % PACK_ROWS_ALIGN
    if rpad:
        parts.append(jnp.zeros((rpad * LANES,), dtype))
    return jnp.concatenate(parts).reshape(rows + rpad, LANES)


def _unpack(buf, shapes):
    flat = buf.reshape(-1)
    outs, off = [], 0
    for s in shapes:
        n = int(np.prod(s))
        outs.append(flat[off:off + n].reshape(s))
        off += n + (-n) % PACK_ALIGN
    return outs


def _split4(a, axis):
    n = a.shape[axis] // N_CHIPS
    return [lax.slice_in_dim(a, s * n, (s + 1) * n, axis=axis) for s in range(N_CHIPS)]


PROJ_ORDER = (3, 7, 8, 9, 0, 1, 2, 6, 4, 5)


def _reorder_w_in(w, cfg):
    d, hg = cfg.d, cfg.hg

    def block(k):
        off = k * d + (2 * hg if k >= 4 else 0)
        return w[:, off:off + d]

    tail = jnp.pad(w[:, 4 * d:4 * d + 2 * hg], ((0, 0), (0, LANES - 2 * hg)))
    return jnp.concatenate([block(k) for k in PROJ_ORDER] + [tail], axis=1)


def _restore_w_in(wr, cfg):
    d, hg = cfg.d, cfg.hg
    at = {k: i for i, k in enumerate(PROJ_ORDER)}
    block = lambda k: wr[:, at[k] * d:(at[k] + 1) * d]
    return jnp.concatenate([block(k) for k in range(4)] + [wr[:, 10 * d:10 * d + 2 * hg]] +
                           [block(k) for k in range(4, 10)], axis=1)


def _step(cfg, x, tgt, shard, m_shard, v_shard):
    d, hg, dff, rp, tr, tm = cfg.d, cfg.hg, cfg.dff, cfg.rp, cfg.tr, cfg.tm
    nrow = rp // tr
    assert cfg.tf * N_CHIPS == 2 * dff and cfg.din % N_CHIPS == 0
    cidx = lax.axis_index("c").astype(jnp.int32).reshape(1)
    chip = (2 * lax.axis_index("x") + lax.axis_index("y")).astype(jnp.int32).reshape(1)

    axis = dict(PARAMS)
    small = ("meta", "gdn_conv_w", "ffn_conv_w")
    small_shapes = [shard[n].shape for n in small]
    mine = [shard[n][0].astype(BF16) for n in BIG] + [_pack([shard[n] for n in small], F32)]

    def with_own(gathered, own):
        return [lax.dynamic_update_slice(g, w[None], (chip[0], 0, 0)) for g, w in zip(gathered, own)]

    g_in, g_small = _run_comm("weights_gather_first", _gather_comm([mine[0], mine[4]]))
    g_small, = with_own([g_small], [mine[4]])
    w_in_r = _reorder_w_in(jnp.concatenate([jnp.where(chip[0] == s, mine[0], g_in[s]) for s in range(N_CHIPS)], axis=1),
                           cfg)
    per_chip = [_unpack(g_small[s], small_shapes) for s in range(N_CHIPS)]
    full = {n: jnp.concatenate([per_chip[s][k] for s in range(N_CHIPS)], axis=axis[n]) for k, n in enumerate(small)}
    meta = full["meta"]
    gconv_w = full["gdn_conv_w"][0]
    fconv_w = full["ffn_conv_w"][0]
    norm1, norm2, gnorm = shard["norm1"], shard["norm2"], shard["gdn_norm"]
    normf = shard["norm_f"].reshape(1, d)
    fconv_b = shard["ffn_conv_b"]
    alog = jnp.pad(shard["gdn_a_log"], ((0, 7), (0, LANES - hg)))
    dtb = jnp.pad(shard["gdn_dt_bias"], ((0, 7), (0, LANES - hg)))

    h0 = jnp.concatenate([jnp.zeros((cfg.front, d), F32), meta, x], axis=0)
    half = RET_DK // 2
    pos = np.arange(rp, dtype=np.float32) - np.float32(cfg.front)
    inv = (np.float32(1.0) / np.float32(ROPE_BASE) ** (np.arange(half, dtype=np.float32) / np.float32(half))).astype(np.float32)
    ang = pos[:, None] * inv[None, :]
    cos, sin = jnp.asarray(np.cos(ang), F32), jnp.asarray(np.sin(ang), F32)
    rconsts = _ret_consts(cfg)

    tr_n = 3 * tr if rp % (3 * tr) == 0 else tr
    rms_f = _make_rms_fn(cfg, tr_n, False)
    rms_b = _make_rms_fn(cfg, tr_n, True)
    rowshape = jax.ShapeDtypeStruct((rp, d), F32)
    rspec = _rows(tr, d)
    nspec = _rows(tr_n, d)

    def rms_fwd(name, h, g):
        return _stage_fwd(name, rms_f, (rp // tr_n,), [In(h, nspec), In(g, _full(g))],
                          [jax.ShapeDtypeStruct((rp, d), BF16)], [nspec])[0]

    wm = 10 * d
    w_main, w_tail = w_in_r[:, :wm], w_in_r[:, wm:]
    tn_in = 2560 if wm % 2560 == 0 else LANES
    hn1 = rms_fwd("rms1_fwd", h0, norm1)
    proj, rest = _mm("proj_fwd", hn1, w_main, tm=tm, tn=tn_in, tk=d, out_dtype=BF16, comm=_gather_comm(mine[1:4]))
    ptail = _mm("proj_tail_fwd", hn1, w_tail, tm=tm, tn=LANES, tk=d)
    g_out, g_up, g_down = with_own(rest, mine[1:4])
    w_out = g_out.reshape(d, d)
    w_up = g_up
    w_up_t = jnp.swapaxes(g_up, 1, 2).reshape(2 * dff, d)
    w_down = g_down.reshape(dff, d)
    cqkv = _conv_fwd("gdn_conv_fwd", proj, CONV_COL * d, gconv_w, None, taps=GDN_CONV, width=3 * d, tr=tr, tc=d)
    prep_fn = _make_gdn_prep_fn(cfg, tr)
    prep_ins = [In(cqkv, _rows(tr, 3 * d), BF16), In(ptail, _rows(tr, LANES), BF16),
                In(alog, _full(alog), F32, True), In(dtb, _full(dtb), F32, True)]
    qn, kn, vv, bB, lB = _stage_fwd("gdn_prep_fwd", prep_fn, (nrow,), prep_ins, [rowshape] * 5, [rspec] * 5)

    trg = cfg.nb * CHUNK
    gi_grid = (rp // trg, hg)
    hspec = pl.BlockSpec((trg, GDN_DK), lambda i, h: (i, h))
    aspec = pl.BlockSpec((1, trg, CHUNK), lambda i, h: (h, i, 0))
    gspec = pl.BlockSpec((1, cfg.nb, 1, GDN_DK), lambda i, h: (h, i, 0, 0))
    intra_ins = [In(t, hspec, F32) for t in (qn, kn, vv, bB, lB)]
    ashape = jax.ShapeDtypeStruct((hg, rp, CHUNK), F32)
    intra_shapes = [rowshape, rowshape, ashape, rowshape, rowshape, jax.ShapeDtypeStruct((hg, cfg.nch, 1, GDN_DK), F32), ashape]
    intra_specs = [hspec, hspec, aspec, hspec, hspec, gspec, aspec]
    gu, gw, gattn, gqd, gkd, ggl, gtinv = _stage_fwd("gdn_intra_fwd", _gdn_intra_fn, gi_grid, intra_ins, intra_shapes,
                                                     intra_specs)
    rot_fn = _make_rot_fn(cfg)

    def rot_ins(dproj=None):
        return [In(proj, _rows(tr_n, 2 * d, ROT_COL // 2), BF16, galias=dproj, gshape=(rp, wm)),
                In(cos, _rows(tr_n, half)), In(sin, _rows(tr_n, half))]

    qr, kr = _stage_fwd("rot_fwd", rot_fn, (rp // tr_n,), rot_ins(), [rowshape] * 2, [nspec] * 2)
    nst = cfg.nch // cfg.sc
    oa, gss = _run_parts("gdn_scan_fwd", (nst,), [_gdn_scan_fwd(cfg, gu, gw, gattn, gqd, gkd, ggl)])[0]
    ob, rss = _run_parts("ret_scan_fwd", (nst,), [_ret_scan_fwd(cfg, qr, kr, proj, rconsts)])[0]

    mix_fn = _make_mix_fn(cfg)
    mix_ins = [In(oa, rspec, F32), In(ob, rspec, F32), In(proj, _rows(tr, 4 * d, MIX_COL // 4), BF16, gshape=(rp, wm)),
               In(gnorm, _full(gnorm), F32, True)]
    ymix = _stage_fwd("mix_fwd", mix_fn, (nrow,), mix_ins, [jax.ShapeDtypeStruct((rp, d), BF16)], [rspec])[0]
    h1 = _mm("out_proj_fwd", ymix, w_out, tm=tm, tn=d, tk=d, add=h0)

    hn2 = rms_fwd("rms2_fwd", h1, norm2)
    up = _mm("ffn_up_fwd", hn2, w_up, tm=tm, tn=cfg.tf, tk=d, out_dtype=BF16)
    uc = _conv_fwd("ffn_conv_fwd", up, 0, fconv_w, fconv_b, taps=FFN_CONV, width=2 * dff, tr=tr, tc=cfg.tf)
    tra = tr
    act_ins = [In(uc, _rows(tra, 2 * dff), BF16)]
    act_spec = _rows(tra, dff)
    act = _stage_fwd("ffn_act_fwd", _act_fn, (rp // tra,), act_ins, [jax.ShapeDtypeStruct((rp, dff), BF16)], [act_spec])[0]
    h2 = _mm("ffn_down_fwd", act, w_down, tm=tm, tn=d, tk=cfg.tf, add=h1)

    dh2, g_normf, loss_blk = _final(cfg, h2, normf, tgt)
    loss = lax.psum(loss_blk[0, 0], ("x", "y", "c"))

    g_w_down = _mm_tn("ffn_down_dw", act, dh2, tr=tm, tka=cfg.tf, tn=d)
    dact = _mm("ffn_down_dx", dh2, w_down.T, tm=tm, tn=cfg.tf, tk=d)
    duc, = _stage_bwd("ffn_act_bwd", _act_fn, (rp // tra,), act_ins, [(dact, act_spec)])
    dup, g_fconv_w, g_fconv_b = _conv_bwd("ffn_conv_bwd", up, 0, fconv_w, duc, taps=FFN_CONV, width=2 * dff,
                                          tr=tr, tc=cfg.tf, with_bias=True)
    g_w_up = _mm_tn("ffn_up_dw", hn2, dup, tr=tm, tka=d, tn=cfg.tf, blocked=True)

    def pair_reduce(tag, names, arrs):
        recvs = _pair_exchange("grad_pair_exchange_" + tag, arrs)
        return [_pair_sum("grad_pair_sum_" + n, g, r, cidx) for n, g, r in zip(names, arrs, recvs)]

    parts_ffn = pair_reduce("ffn", ["w_ffn_down", "w_ffn_up"], [g_w_down.reshape(N_CHIPS, dff // N_CHIPS, d), g_w_up])
    dhn2, slots_ffn = _mm("ffn_up_dx", dup, w_up_t, tm=tm, tn=d, tk=2 * cfg.tf, comm=_exchange_comm(parts_ffn))

    def rms_bwd(name, h, g, dhn, dres):
        ins = [In(h, nspec, F32), In(g, _full(g), F32, True)]
        return _stage_bwd(name, rms_b, (rp // tr_n,), ins, [(dhn, nspec), (dres, nspec)])

    dh1, g_norm2 = rms_bwd("rms2_bwd", h1, norm2, dhn2, dh2)
    g_w_out = _mm_tn("out_proj_dw", ymix, dh1, tr=tm, tka=d, tn=d)
    dymix = _mm("out_proj_dx", dh1, w_out.T, tm=tm, tn=d, tk=d)
    doa, dob, dproj, g_gnorm = _stage_bwd("mix_bwd", mix_fn, (nrow,), mix_ins, [(dymix, rspec)])

    dqr, dkr, dproj = _run_parts("ret_scan_bwd", (nst,), [_ret_scan_bwd(cfg, dob, qr, kr, proj, rconsts, rss, dproj)])[0]
    dproj, = _stage_bwd("rot_bwd", rot_fn, (rp // tr_n,), rot_ins(dproj), [(dqr, nspec), (dkr, nspec)])
    dgu, dgw, dgattn, dgqd, dgkd, dggl = _run_parts(
        "gdn_scan_bwd", (nst,), [_gdn_scan_bwd(cfg, doa, gu, gw, gattn, gqd, gkd, ggl, gss)])[0]

    intra_cots = [(dgu, hspec), (dgw, hspec), (dgattn, aspec), (dgqd, hspec), (dgkd, hspec), (dggl, gspec)]
    dqn, dkn, dvv, dbB, dlB = _stage_bwd("gdn_intra_bwd", _gdn_intra_fn, gi_grid, intra_ins + [In(gtinv, aspec)], intra_cots)
    dcqkv, dtail, g_alog, g_dtb = _stage_bwd(
        "gdn_prep_bwd", prep_fn, (nrow,), prep_ins, [(t, rspec) for t in (dqn, dkn, dvv, dbB, dlB)])
    dproj, g_gconv_w = _conv_bwd("gdn_conv_bwd", proj, CONV_COL * d, gconv_w, dcqkv, taps=GDN_CONV, width=3 * d,
                                 tr=tr, tc=d, with_bias=False, dx_into=dproj)
    g_w_in_r = jnp.concatenate([_mm_tn("proj_dw", hn1, dproj, tr=tm, tka=d, tn=tn_in),
                                _mm_tn("proj_tail_dw", hn1, dtail, tr=tm, tka=d, tn=LANES)], axis=1)
    g_in4 = jnp.stack(_split4(_restore_w_in(g_w_in_r, cfg), 1))
    parts_mix = pair_reduce("mix", ["w_out", "w_in"], [g_w_out.reshape(N_CHIPS, d // N_CHIPS, d), g_in4])
    dhn1_tail = _mm("proj_tail_dx", dtail, w_tail.T, tm=tm, tn=d, tk=LANES)
    dhn1, slots_mix = _mm("proj_dx", dproj, w_main.T, tm=tm, tn=d, tk=tn_in // 2 if tn_in > LANES else LANES, add=dhn1_tail,
                          comm=_exchange_comm(parts_mix))
    dh0, g_norm1 = rms_bwd("rms1_bwd", h0, norm1, dhn1, dh1)

    grad_x = dh0[cfg.xrow:]
    small_grads = {
        "meta": dh0[cfg.front:cfg.xrow], "norm1": g_norm1, "gdn_conv_w": g_gconv_w[None],
        "gdn_a_log": g_alog[0:1, :hg], "gdn_dt_bias": g_dtb[0:1, :hg], "gdn_norm": g_gnorm, "norm2": g_norm2,
        "ffn_conv_w": g_fconv_w[None], "ffn_conv_b": g_fconv_b, "norm_f": g_normf.reshape(d),
    }

    small_names = [n for n, _ in PARAMS if n not in BIG]
    g_small = jnp.stack([_pack([small_grads[n] if axis[n] is None else _split4(small_grads[n], axis[n])[s]
                                for n in small_names], F32) for s in range(N_CHIPS)])
    parts_small = pair_reduce("small", ["small"], [g_small])
    slots_small = _run_comm("grad_exchange_small", _exchange_comm(parts_small))
    tags = ["w_in", "w_out", "w_ffn_up", "w_ffn_down", "small"]
    parts = [parts_mix[1], parts_mix[0], parts_ffn[1], parts_ffn[0], parts_small[0]]
    slots = [slots_mix[1], slots_mix[0], slots_ffn[1], slots_ffn[0], slots_small[0]]
    fins = [_chip_sum("grad_chip_sum_" + t, p, s, chip) for t, p, s in zip(tags, parts, slots)]
    sibs = _pair_swap(fins)

    def flat2(a):
        return a.reshape(-1, a.shape[-1])

    outs = {}
    for k, t in enumerate(BIG):
        res = _adamw("adamw_" + t, flat2(shard[t]), fins[k], sibs[k], flat2(m_shard[t]), flat2(v_shard[t]), cidx)
        outs[t] = [r.reshape(shard[t].shape) for r in res]
    small_shapes_all = [shard[n].shape for n in small_names]
    pk = lambda src: _pack([src[n] for n in small_names], F32)
    res = _adamw("adamw_small", pk(shard), fins[4], sibs[4], pk(m_shard), pk(v_shard), cidx)
    for k, r in enumerate(res):
        for n, a in zip(small_names, _unpack(r, small_shapes_all)):
            outs.setdefault(n, [None] * 4)[k] = a
    names = [n for n, _ in PARAMS]
    return (loss, grad_x[None], *[outs[n][k] for k in range(4) for n in names])


def kernel(x, meta, norm1, w_in, gdn_conv_w, gdn_a_log, gdn_dt_bias, gdn_norm, w_out, norm2, w_ffn_up, ffn_conv_w, ffn_conv_b, w_ffn_down, norm_f, loss_target, m_meta, m_norm1, m_w_in, m_gdn_conv_w, m_gdn_a_log, m_gdn_dt_bias, m_gdn_norm, m_w_out, m_norm2, m_w_ffn_up, m_ffn_conv_w, m_ffn_conv_b, m_w_ffn_down, m_norm_f, v_meta, v_norm1, v_w_in, v_gdn_conv_w, v_gdn_a_log, v_gdn_dt_bias, v_gdn_norm, v_w_out, v_norm2, v_w_ffn_up, v_ffn_conv_w, v_ffn_conv_b, v_w_ffn_down, v_norm_f):
    names = [n for n, _ in PARAMS]
    shard = dict(zip(names, (meta, norm1, w_in, gdn_conv_w, gdn_a_log, gdn_dt_bias, gdn_norm, w_out, norm2, w_ffn_up,
                             ffn_conv_w, ffn_conv_b, w_ffn_down, norm_f)))
    m_shard = dict(zip(names, (m_meta, m_norm1, m_w_in, m_gdn_conv_w, m_gdn_a_log, m_gdn_dt_bias, m_gdn_norm, m_w_out,
                               m_norm2, m_w_ffn_up, m_ffn_conv_w, m_ffn_conv_b, m_w_ffn_down, m_norm_f)))
    v_shard = dict(zip(names, (v_meta, v_norm1, v_w_in, v_gdn_conv_w, v_gdn_a_log, v_gdn_dt_bias, v_gdn_norm, v_w_out,
                               v_norm2, v_w_ffn_up, v_ffn_conv_w, v_ffn_conv_b, v_w_ffn_down, v_norm_f)))
    return _step(REAL, x[0], loss_target[0], shard, m_shard, v_shard)
```

```python
import functools
from typing import NamedTuple

import numpy as np
import jax
import jax.numpy as jnp
from jax import lax
from jax.experimental import pallas as pl
from jax.experimental.pallas import tpu as pltpu
from jax.experimental.pallas import tpu_sc as plsc

F32 = jnp.float32
BF16 = jnp.bfloat16
EPS = 1e-6
CHUNK = 64
GDN_DK = 128
RET_DK = 256
GDN_CONV = 4
FFN_CONV = 3
ROPE_BASE = 10000.0
LANES = 128
N_CHIPS = 4
ADAM_LR, ADAM_B1, ADAM_B2, ADAM_EPS, ADAM_WD, ADAM_STEP = 0.001, 0.9, 0.999, 1e-08, 0.01, 10
MIX_COL, CONV_COL, RV_BLOCK, ROT_COL = 0, 4, 7, 8
MESH = pl.DeviceIdType.MESH
VMEM_LIMIT = 56 * 1024 * 1024


class Cfg(NamedTuple):
    d: int
    seq: int
    n_meta: int
    dff: int
    tr: int
    nb: int
    tm: int
    tf: int
    sc: int

    @property
    def hg(self): return self.d // GDN_DK
    @property
    def hr(self): return self.d // RET_DK
    @property
    def L(self): return self.n_meta + self.seq
    @property
    def rp(self): return -(-self.L // 256) * 256
    @property
    def front(self): return self.rp - self.L
    @property
    def xrow(self): return self.rp - self.seq
    @property
    def nch(self): return self.rp // CHUNK
    @property
    def din(self): return 10 * self.d + 2 * self.hg


REAL = Cfg(d=1024, seq=8192, n_meta=16, dff=2816, tr=256, nb=12, tm=1408, tf=1408, sc=4)


def _pallas(body, **kw):
    return pl.pallas_call(body, **kw)


def _sigmoid(x):
    return 1.0 / (1.0 + jnp.exp(-x))


def _silu(x):
    return x * _sigmoid(x)


def _softplus(x):
    return jnp.maximum(x, 0.0) + jnp.log(1.0 + jnp.exp(-jnp.abs(x)))


def _raw_dot(a, b, ta, tb, hi):
    if not hi:
        a = a.astype(BF16)
        b = b.astype(BF16)
    nbatch = a.ndim - 2
    ca = a.ndim - 2 if ta else a.ndim - 1
    cb = b.ndim - 1 if tb else b.ndim - 2
    batch = tuple(range(nbatch))
    return lax.dot_general(a, b, (((ca,), (cb,)), (batch, batch)),
                           precision=lax.Precision.HIGHEST if hi else None,
                           preferred_element_type=F32)


@functools.partial(jax.custom_vjp, nondiff_argnums=(2, 3, 4))
def _dot_p(a, b, ta, tb, hi):
    return _raw_dot(a, b, ta, tb, hi)


def _dot(a, b, ta=False, tb=False, hi=False):
    return _dot_p(a, b, ta, tb, hi)


def _dot_fwd(a, b, ta, tb, hi):
    return _raw_dot(a, b, ta, tb, hi), (a, b)


def _dot_bwd(ta, tb, hi, res, g):
    a, b = res
    if not ta and not tb:
        da, db = _dot(g, b, False, True, hi), _dot(a, g, True, False, hi)
    elif not ta and tb:
        da, db = _dot(g, b, False, False, hi), _dot(g, a, True, False, hi)
    elif ta and not tb:
        da, db = _dot(b, g, False, True, hi), _dot(a, g, False, False, hi)
    else:
        raise NotImplementedError
    return da.astype(a.dtype), db.astype(b.dtype)


_dot_p.defvjp(_dot_fwd, _dot_bwd)


def _iota2(n, m, axis):
    return lax.broadcasted_iota(jnp.int32, (n, m), axis)


def _bcast(mat, nb):
    return jnp.broadcast_to(mat[None], (nb,) + mat.shape)


def _split3(a):
    a0 = a.astype(BF16)
    r1 = a - a0.astype(F32)
    a1 = r1.astype(BF16)
    return a0, a1, (r1 - a1.astype(F32)).astype(BF16)


@functools.partial(jax.custom_vjp, nondiff_argnums=(2,))
def _dot_sel(a, e, te):
    eb = e.astype(BF16)
    p0, p1, p2 = (_raw_dot(p, eb, False, te, False) for p in _split3(a))
    return p0 + (p1 + p2)


def _dot_sel_fwd(a, e, te):
    return _dot_sel(a, e, te), e


def _dot_sel_bwd(te, e, g):
    return _dot_sel(g, e, not te), jnp.zeros_like(e)


_dot_sel.defvjp(_dot_sel_fwd, _dot_sel_bwd)


@jax.custom_vjp
def _sel_dot(e, x):
    eb = e.astype(BF16)
    p0, p1, p2 = (_raw_dot(eb, p, False, False, False) for p in _split3(x))
    return p0 + (p1 + p2)


def _sel_dot_fwd(e, x):
    return _sel_dot(e, x), e


def _sel_dot_bwd(e, g):
    eb = e.astype(BF16)
    p0, p1, p2 = (_raw_dot(eb, p, True, False, False) for p in _split3(g))
    return jnp.zeros_like(e), p0 + (p1 + p2)


_sel_dot.defvjp(_sel_dot_fwd, _sel_dot_bwd)


def _tri_inv_raw(m):
    nb = m.shape[0]
    r, c = _iota2(CHUNK, CHUNK, 0), _iota2(CHUNK, CHUNK, 1)
    t = _bcast((r == c).astype(F32), nb)
    b = 1
    while b < CHUNK:
        sh = b.bit_length() - 1
        off = ((r >> (sh + 1)) == (c >> (sh + 1))) & ((r >> sh) != (c >> sh)) & (r > c)
        cl = jnp.where(off[None], m, 0.0)
        t = t - _raw_dot(_raw_dot(t, cl, False, False, False), t, False, False, False)
        b *= 2
    return t


@jax.custom_vjp
def _tri_inv_given(m, t):
    return t


def _tri_inv_fwd(m, t):
    return t, t


def _tri_inv_bwd(t, g):
    return -_raw_dot(_raw_dot(t, g, True, False, False), t, False, True, False), jnp.zeros_like(t)


_tri_inv_given.defvjp(_tri_inv_fwd, _tri_inv_bwd)


def _rms(h, g):
    return h * lax.rsqrt(jnp.mean(h * h, axis=-1, keepdims=True) + EPS) * g


class In(NamedTuple):
    arr: jax.Array
    spec: pl.BlockSpec
    grad: object = None
    acc: bool = False
    gshape: object = None
    gspec: object = None
    galias: object = None


def _params(grid):
    sem = ("arbitrary",) * len(grid)
    return pltpu.CompilerParams(dimension_semantics=sem, vmem_limit_bytes=VMEM_LIMIT)


def _stage_fwd(name, fn, grid, ins, out_shapes, out_specs):
    n_in = len(ins)

    def body(*refs):
        pids = tuple(pl.program_id(k) for k in range(len(grid)))
        vals = [r[...].astype(F32) for r in refs[:n_in]]
        outs = fn(pids, *vals)
        for o_ref, o in zip(refs[n_in:], outs):
            o_ref[...] = o.reshape(o_ref.shape).astype(o_ref.dtype)

    return _pallas(
        body, out_shape=out_shapes, grid=grid, in_specs=[i.spec for i in ins],
        out_specs=out_specs, name=name, compiler_params=_params(grid))(*[i.arr for i in ins])


def _stage_bwd(name, fn, grid, ins, cots):
    n_in, n_ct = len(ins), len(cots)
    didx = [k for k, i in enumerate(ins) if i.grad is not None]
    aliased = [(o, ins[k].galias) for o, k in enumerate(didx) if ins[k].galias is not None]
    n_al = len(aliased)

    def body(*refs):
        pids = tuple(pl.program_id(k) for k in range(len(grid)))
        vals = [r[...].astype(F32) for r in refs[:n_in]]
        ct_refs = refs[n_in:n_in + n_ct]
        g_refs = refs[n_in + n_ct + n_al:]

        def f(*dv):
            merged = list(vals)
            for k, v in zip(didx, dv):
                merged[k] = v
            return tuple(fn(pids, *merged))

        outs, vjp_fn = jax.vjp(f, *[vals[k].astype(F32) for k in didx])
        cts = tuple(c[...].reshape(o.shape).astype(F32) for c, o in zip(ct_refs, outs))
        grads = vjp_fn(cts)
        first = functools.reduce(jnp.logical_and, [p == 0 for p in pids])
        for k, g_ref, g in zip(didx, g_refs, grads):
            if ins[k].acc:
                @pl.when(first)
                def _(g_ref=g_ref):
                    g_ref[...] = jnp.zeros(g_ref.shape, g_ref.dtype)
                g_ref[...] += g.reshape(g_ref.shape).astype(g_ref.dtype)
            else:
                g_ref[...] = g.reshape(g_ref.shape).astype(g_ref.dtype)

    out_shapes = [jax.ShapeDtypeStruct(ins[k].gshape or ins[k].arr.shape, ins[k].grad) for k in didx]
    out_specs = [ins[k].gspec or ins[k].spec for k in didx]
    return _pallas(
        body, out_shape=out_shapes, grid=grid,
        in_specs=[i.spec for i in ins] + [c[1] for c in cots] + [ANY] * n_al, out_specs=out_specs,
        input_output_aliases={n_in + n_ct + a: o for a, (o, _) in enumerate(aliased)},
        name=name, compiler_params=_params(grid))(*[i.arr for i in ins], *[c[0] for c in cots], *[a for _, a in aliased])


def _full(arr):
    nd = arr.ndim
    return pl.BlockSpec(arr.shape, lambda *p: (0,) * nd)


def _rows(tr, width, blk=0):
    return pl.BlockSpec((tr, width), lambda i: (i, blk))


def _mm(name, a, b, *, tm, tn, tk, out_dtype=F32, add=None, comm=None):
    M, K = a.shape
    N = b.shape[1] if b.ndim == 2 else b.shape[0] * b.shape[2]
    nk = K // tk
    grid = (M // tm, N // tn, nk)
    n_in = 3 if add is not None else 2
    n_ci, n_co = (len(comm.ins), len(comm.outs)) if comm is not None else (0, 0)

    def body(*refs):
        a_ref, b_ref = refs[0], refs[1]
        add_ref = refs[2] if add is not None else None
        c_ins = refs[n_in:n_in + n_ci]
        o_ref = refs[n_in + n_ci]
        c_outs = refs[n_in + n_ci + 1:n_in + n_ci + 1 + n_co]
        scratch = refs[n_in + n_ci + 1 + n_co:]
        acc_ref = scratch[0] if nk > 1 else None
        sems = scratch[1 if nk > 1 else 0:]
        step = (pl.program_id(0) * grid[1] + pl.program_id(1)) * nk + pl.program_id(2)
        if comm is not None:
            @pl.when(step == 0)
            def _():
                comm.start(c_ins, c_outs, sems)

        part = _raw_dot(a_ref[...], b_ref[...], False, False, False)

        def finish(total):
            if add_ref is not None:
                total = total + add_ref[...]
            o_ref[...] = total.astype(o_ref.dtype)

        if nk == 1:
            finish(part)
        else:
            k = pl.program_id(2)

            @pl.when(k == 0)
            def _():
                acc_ref[...] = part

            @pl.when(k > 0)
            def _():
                acc_ref[...] += part

            @pl.when(k == nk - 1)
            def _():
                finish(acc_ref[...])

        if comm is not None:
            @pl.when(step == grid[0] * grid[1] * nk - 1)
            def _():
                comm.finish(c_ins, c_outs, sems)

    b_spec = (pl.BlockSpec((tk, tn), lambda i, j, k: (k, j)) if b.ndim == 2 else
              pl.BlockSpec((None, tk, tn), lambda i, j, k: (j, k, 0)))
    in_specs = [pl.BlockSpec((tm, tk), lambda i, j, k: (i, k)), b_spec]
    args = [a, b]
    if add is not None:
        in_specs.append(pl.BlockSpec((tm, tn), lambda i, j, k: (i, j)))
        args.append(add)
    out_shape = jax.ShapeDtypeStruct((M, N), out_dtype)
    out_spec = pl.BlockSpec((tm, tn), lambda i, j, k: (i, j))
    scratch = [pltpu.VMEM((tm, tn), F32)] if nk > 1 else []
    if comm is None:
        return _pallas(body, out_shape=out_shape, grid=grid, in_specs=in_specs, out_specs=out_spec,
                       scratch_shapes=scratch, name=name, compiler_params=_params(grid))(*args)
    res = _pallas(body, out_shape=[out_shape] + comm.outs, grid=grid, in_specs=in_specs + [ANY] * n_ci,
                  out_specs=[out_spec] + [ANY] * n_co, scratch_shapes=scratch + comm.sems, name=name,
                  compiler_params=_params(grid))(*args, *comm.ins)
    return res[0], res[1:]


def _mm_tn(name, a, b, *, tr, tka, tn, blocked=False):
    R, Ka = a.shape
    N = b.shape[1]
    nr = R // tr
    grid = (Ka // tka, N // tn, nr)
    if blocked:
        out_shape = jax.ShapeDtypeStruct((N // tn, Ka, tn), F32)
        out_spec = pl.BlockSpec((None, tka, tn), lambda i, j, r: (j, i, 0))
    else:
        out_shape = jax.ShapeDtypeStruct((Ka, N), F32)
        out_spec = pl.BlockSpec((tka, tn), lambda i, j, r: (i, j))

    def body(a_ref, b_ref, o_ref):
        r = pl.program_id(2)
        part = _raw_dot(a_ref[...], b_ref[...], True, False, False)

        @pl.when(r == 0)
        def _():
            o_ref[...] = part

        @pl.when(r > 0)
        def _():
            o_ref[...] += part

    return _pallas(
        body, out_shape=out_shape, grid=grid,
        in_specs=[pl.BlockSpec((tr, tka), lambda i, j, r: (r, i)),
                  pl.BlockSpec((tr, tn), lambda i, j, r: (r, j))],
        out_specs=out_spec, name=name, compiler_params=_params(grid))(a, b)


def _conv_fwd(name, x, xcol0, w, b, *, taps, width, tr, tc):
    R = x.shape[0]
    grid = (width // tc, R // tr)
    cb0 = xcol0 // tc
    hrows = 16 if x.dtype == BF16 else 8
    hb = tr // hrows

    def body(*refs):
        x_ref, xp_ref, w_ref = refs[:3]
        b_ref = refs[3] if b is not None else None
        o_ref = refs[-1]
        i = pl.program_id(1)
        xv = x_ref[...].astype(F32)
        prev = jnp.where(i > 0, xp_ref[...].astype(F32)[hrows - 8:, :], 0.0)
        ext = jnp.concatenate([prev, xv], axis=0)
        acc = xv * w_ref[taps - 1:taps, :]
        for s in range(1, taps):
            acc = acc + pltpu.roll(ext, s, 0)[8:, :] * w_ref[taps - 1 - s:taps - s, :]
        if b_ref is not None:
            acc = acc + b_ref[...]
        o_ref[...] = acc.astype(o_ref.dtype)

    in_specs = [pl.BlockSpec((tr, tc), lambda j, i: (i, cb0 + j)),
                pl.BlockSpec((hrows, tc), lambda j, i: (jnp.maximum(i * hb - 1, 0), cb0 + j)),
                pl.BlockSpec((taps, tc), lambda j, i: (0, j))]
    args = [x, x, w]
    if b is not None:
        in_specs.append(pl.BlockSpec((1, tc), lambda j, i: (0, j)))
        args.append(b)
    return _pallas(
        body, out_shape=jax.ShapeDtypeStruct((R, width), BF16), grid=grid, in_specs=in_specs,
        out_specs=pl.BlockSpec((tr, tc), lambda j, i: (i, j)),
        name=name, compiler_params=_params(grid))(*args)


def _conv_bwd(name, x, xcol0, w, dy, *, taps, width, tr, tc, with_bias, dx_into=None):
    R = x.shape[0]
    nr = R // tr
    grid = (width // tc, nr)
    cb0 = xcol0 // tc
    hrows = 16 if dy.dtype == BF16 else 8
    hb = tr // hrows
    n_ext = tr + 8
    n_al = 0 if dx_into is None else 1

    def body(*refs):
        x_ref, w_ref, dy_ref, dyn_ref = refs[:4]
        dx_ref, dw_ref = refs[4 + n_al], refs[5 + n_al]
        db_ref = refs[6 + n_al] if with_bias else None
        i = pl.program_id(1)
        xv = x_ref[...].astype(F32)
        dyv = dy_ref[...].astype(F32)
        nxt = dyn_ref[...].astype(F32)[:8, :]
        dext = jnp.concatenate([dyv, jnp.where(i < nr - 1, nxt, 0.0)], axis=0)
        dx = dyv * w_ref[taps - 1:taps, :]
        dws = [None] * taps
        dws[taps - 1] = jnp.sum(xv * dyv, axis=0, keepdims=True)
        for s in range(1, taps):
            ahead = pltpu.roll(dext, n_ext - s, 0)[:tr, :]
            dx = dx + ahead * w_ref[taps - 1 - s:taps - s, :]
            dws[taps - 1 - s] = jnp.sum(xv * ahead, axis=0, keepdims=True)
        dx_ref[...] = dx.astype(dx_ref.dtype)

        @pl.when(i == 0)
        def _():
            for k in range(taps):
                dw_ref[k:k + 1, :] = dws[k]
            if db_ref is not None:
                db_ref[...] = jnp.sum(dyv, axis=0, keepdims=True)

        @pl.when(i > 0)
        def _():
            for k in range(taps):
                dw_ref[k:k + 1, :] += dws[k]
            if db_ref is not None:
                db_ref[...] += jnp.sum(dyv, axis=0, keepdims=True)

    in_specs = [pl.BlockSpec((tr, tc), lambda j, i: (i, cb0 + j)),
                pl.BlockSpec((taps, tc), lambda j, i: (0, j)),
                pl.BlockSpec((tr, tc), lambda j, i: (i, j)),
                pl.BlockSpec((hrows, tc), lambda j, i: (jnp.minimum((i + 1) * hb, R // hrows - 1), j))]
    args = [x, w, dy, dy]
    if dx_into is None:
        dx_shape, dx_spec, aliases = jax.ShapeDtypeStruct((R, width), BF16), pl.BlockSpec((tr, tc), lambda j, i: (i, j)), {}
    else:
        dx_shape = jax.ShapeDtypeStruct(dx_into.shape, dx_into.dtype)
        dx_spec, aliases = pl.BlockSpec((tr, tc), lambda j, i: (i, cb0 + j)), {4: 0}
        in_specs.append(ANY)
        args.append(dx_into)
    out_shape = [dx_shape, jax.ShapeDtypeStruct((taps, width), F32)]
    out_specs = [dx_spec, pl.BlockSpec((taps, tc), lambda j, i: (0, j))]
    if with_bias:
        out_shape.append(jax.ShapeDtypeStruct((1, width), F32))
        out_specs.append(pl.BlockSpec((1, tc), lambda j, i: (0, j)))
    return _pallas(
        body, out_shape=out_shape, grid=grid, in_specs=in_specs, out_specs=out_specs, input_output_aliases=aliases,
        name=name, compiler_params=_params(grid))(*args)


def _row_mask(cfg, i, tr):
    rows = i * tr + lax.broadcasted_iota(jnp.int32, (tr, 1), 0)
    return (rows >= cfg.front).astype(F32)


def _make_rms_fn(cfg, tr, with_residual):
    def fn(pids, h, g):
        hm = h * _row_mask(cfg, pids[0], tr)
        if with_residual:
            return _rms(hm, g), hm
        return (_rms(hm, g),)
    return fn


def _make_gdn_prep_fn(cfg, tr):
    d, hg = cfg.d, cfg.hg

    def fn(pids, c, tail, alog, dtb):
        cq, ck, cv = c[:, :d], c[:, d:2 * d], c[:, 2 * d:]
        mask = _row_mask(cfg, pids[0], tr)
        j, col = _iota2(LANES, d, 0), _iota2(LANES, d, 1)
        ea = ((col >> 7) == j).astype(F32)
        eb = ((col >> 7) + hg == j).astype(F32)
        al = jnp.sum(alog, axis=0, keepdims=True)
        db = jnp.sum(dtb, axis=0, keepdims=True)
        lg = _dot_sel(-jnp.exp(al) * _softplus(tail + db) * mask, ea, False)
        beta = _dot_sel(_sigmoid(tail) * mask, eb, False)
        sq, sk, sv = _silu(cq), _silu(ck), _silu(cv)
        qs, ks = [], []
        for h in range(hg):
            sl = slice(h * GDN_DK, (h + 1) * GDN_DK)
            qh, kh = sq[:, sl], sk[:, sl]
            qs.append(qh * lax.rsqrt(jnp.sum(qh * qh, axis=-1, keepdims=True) + EPS) * (GDN_DK ** -0.5))
            ks.append(kh * lax.rsqrt(jnp.sum(kh * kh, axis=-1, keepdims=True) + EPS))
        return jnp.concatenate(qs, axis=1), jnp.concatenate(ks, axis=1), sv, beta, lg
    return fn


def _gdn_intra_fn(pids, q, k, v, bB, lB, t_saved=None):
    rows = q.shape[0]
    nb = rows // CHUNK
    q3, k3, v3, b3, l3 = [t.reshape(nb, CHUNK, GDN_DK) for t in (q, k, v, bB, lB)]
    r, c = _iota2(CHUNK, CHUNK, 0), _iota2(CHUNK, CHUNK, 1)
    tril = (r >= c)
    strict = (r > c)
    gcol = _sel_dot(_bcast(tril.astype(F32), nb), l3)
    grow = jnp.swapaxes(gcol, 1, 2)[:, :CHUNK, :]
    diff = gcol[:, :, :CHUNK] - grow
    decay = jnp.where(tril[None], jnp.exp(jnp.where(tril[None], diff, 0.0)), 0.0)
    kb = k3 * b3
    m = jnp.where(strict[None], _dot(kb, k3, False, True) * decay, 0.0)
    t = _tri_inv_raw(m) if t_saved is None else _tri_inv_given(m, t_saved.reshape(nb, CHUNK, CHUNK))
    eg = jnp.exp(gcol)
    u = _dot(t, v3 * b3)
    w = _dot(t, kb * eg)
    attn = _dot(q3, k3, False, True) * decay
    qd = q3 * eg
    glast = jnp.sum(l3, axis=1, keepdims=True)
    kd = k3 * jnp.exp(glast - gcol)
    gl = jnp.exp(glast)
    outs = (u.reshape(rows, GDN_DK), w.reshape(rows, GDN_DK), attn.reshape(1, rows, CHUNK),
            qd.reshape(rows, GDN_DK), kd.reshape(rows, GDN_DK), gl.reshape(1, nb, 1, GDN_DK))
    return outs + (t.reshape(1, rows, CHUNK),) if t_saved is None else outs


def _make_rot_fn(cfg):
    hr = cfg.hr
    half = RET_DK // 2

    def fn(pids, rqk, cos, sin):
        rq, rk = rqk[:, :cfg.d], rqk[:, cfg.d:]

        def rot(t, scale):
            outs = []
            for h in range(hr):
                x1 = t[:, h * RET_DK:h * RET_DK + half]
                x2 = t[:, h * RET_DK + half:(h + 1) * RET_DK]
                outs += [(x1 * cos - x2 * sin) * scale, (x2 * cos + x1 * sin) * scale]
            return jnp.concatenate(outs, axis=1)
        return rot(rq, 1.0), rot(rk, RET_DK ** -0.5)
    return fn


def _make_mix_fn(cfg):
    hg, hr = cfg.hg, cfg.hr

    def fn(pids, oa, ob, pm, gnorm):
        d = cfg.d
        gz, rg, gate_a, gate_b = pm[:, :d], pm[:, d:2 * d], pm[:, 2 * d:3 * d], pm[:, 3 * d:]
        oas = []
        for h in range(hg):
            oh = oa[:, h * GDN_DK:(h + 1) * GDN_DK]
            oas.append(oh * lax.rsqrt(jnp.mean(oh * oh, axis=-1, keepdims=True) + EPS) * gnorm)
        ya = jnp.concatenate(oas, axis=1) * _silu(gz)
        obs = []
        for h in range(hr):
            oh = ob[:, h * RET_DK:(h + 1) * RET_DK]
            obs.append(oh * lax.rsqrt(jnp.mean(oh * oh, axis=-1, keepdims=True) + EPS))
        yb = _silu(rg) * jnp.concatenate(obs, axis=1)
        return (_sigmoid(gate_a) * ya + _sigmoid(gate_b) * yb,)
    return fn


def _act_fn(pids, u):
    f = u.shape[1] // 2
    return (_silu(u[:, :f]) * u[:, f:],)


def _gdn_step(s, u, w, a, qd, kd, gl):
    top = _dot(jnp.concatenate([w, qd], axis=0), s)
    v_new = u - top[:CHUNK]
    bot = _dot(jnp.concatenate([a, kd.T], axis=0), v_new)
    o = top[CHUNK:] + bot[:CHUNK]
    s2 = s * gl + bot[CHUNK:]
    return s2, o


def _ret_step(s, q, k, v, dm, qdc, kdc, g):
    att = _dot(q, k, False, True) * dm
    bot = _dot(jnp.concatenate([att, (k * kdc).T], axis=0), v)
    o = bot[:CHUNK] + _dot(q * qdc, s)
    s2 = s * g + bot[CHUNK:]
    return s2, o


class Part(NamedTuple):
    body: object
    args: list
    in_specs: list
    out_shape: list
    out_specs: list
    scratch: list
    aliases: dict = {}


def _run_parts(name, grid, parts):
    n_in = [len(p.args) for p in parts]
    n_out = [len(p.out_shape) for p in parts]
    n_sc = [len(p.scratch) for p in parts]
    off_in = [sum(n_in[:k]) for k in range(len(parts))]
    off_out = [sum(n_out[:k]) for k in range(len(parts))]
    off_sc = [sum(n_sc[:k]) for k in range(len(parts))]

    def body(*refs):
        ins, outs, scr = refs[:sum(n_in)], refs[sum(n_in):sum(n_in) + sum(n_out)], refs[sum(n_in) + sum(n_out):]
        for k, p in enumerate(parts):
            p.body(*ins[off_in[k]:off_in[k] + n_in[k]], *outs[off_out[k]:off_out[k] + n_out[k]],
                   *scr[off_sc[k]:off_sc[k] + n_sc[k]])

    aliases = {off_in[k] + i: off_out[k] + o for k, p in enumerate(parts) for i, o in p.aliases.items()}
    res = _pallas(
        body, out_shape=sum((p.out_shape for p in parts), []), grid=grid, in_specs=sum((p.in_specs for p in parts), []),
        out_specs=sum((p.out_specs for p in parts), []), scratch_shapes=sum((p.scratch for p in parts), []),
        input_output_aliases=aliases, name=name, compiler_params=_params(grid))(*sum((p.args for p in parts), []))
    return [res[off_out[k]:off_out[k] + n_out[k]] for k in range(len(parts))]


def _gdn_scan_fwd(cfg, u, w, attn, qd, kd, gl):
    d, hg, nch, sc = cfg.d, cfg.hg, cfg.nch, cfg.sc
    nst = nch // sc

    def body(u_ref, w_ref, a_ref, qd_ref, kd_ref, gl_ref, o_ref, ss_ref, s_ref):
        @pl.when(pl.program_id(0) == 0)
        def _():
            s_ref[...] = jnp.zeros(s_ref.shape, F32)

        states = [s_ref[h] for h in range(hg)]
        for j in range(sc):
            rows = slice(j * CHUNK, (j + 1) * CHUNK)
            outs = []
            for h in range(hg):
                sl = slice(h * GDN_DK, (h + 1) * GDN_DK)
                ss_ref[j, h] = states[h]
                states[h], o = _gdn_step(states[h], u_ref[rows, sl], w_ref[rows, sl], a_ref[h, rows, :],
                                         qd_ref[rows, sl], kd_ref[rows, sl], gl_ref[h, j])
                outs.append(o)
            o_ref[rows, :] = jnp.concatenate(outs, axis=1)
        for h in range(hg):
            s_ref[h] = states[h]

    row = pl.BlockSpec((sc * CHUNK, d), lambda n: (n, 0))
    return Part(
        body, [u, w, attn, qd, kd, gl],
        [row, row, pl.BlockSpec((hg, sc * CHUNK, CHUNK), lambda n: (0, n, 0)), row, row,
         pl.BlockSpec((hg, sc, 1, GDN_DK), lambda n: (0, n, 0, 0))],
        [jax.ShapeDtypeStruct((cfg.rp, d), F32), jax.ShapeDtypeStruct((nch, hg, GDN_DK, GDN_DK), F32)],
        [row, pl.BlockSpec((sc, hg, GDN_DK, GDN_DK), lambda n: (n, 0, 0, 0))],
        [pltpu.VMEM((hg, GDN_DK, GDN_DK), F32)])


def _gdn_scan_bwd(cfg, do, u, w, attn, qd, kd, gl, ss):
    d, hg, nch, sc = cfg.d, cfg.hg, cfg.nch, cfg.sc
    nst = nch // sc

    def body(do_ref, u_ref, w_ref, a_ref, qd_ref, kd_ref, gl_ref, ss_ref,
             du_ref, dw_ref, da_ref, dqd_ref, dkd_ref, dgl_ref, ds_ref):
        @pl.when(pl.program_id(0) == 0)
        def _():
            ds_ref[...] = jnp.zeros(ds_ref.shape, F32)

        dstates = [ds_ref[h] for h in range(hg)]
        for j in reversed(range(sc)):
            rows = slice(j * CHUNK, (j + 1) * CHUNK)
            dus, dws, dqds, dkds = [], [], [], []
            for h in range(hg):
                sl = slice(h * GDN_DK, (h + 1) * GDN_DK)
                args = (ss_ref[j, h], u_ref[rows, sl], w_ref[rows, sl], a_ref[h, rows, :], qd_ref[rows, sl],
                        kd_ref[rows, sl], gl_ref[h, j])
                _, vjp_fn = jax.vjp(_gdn_step, *args)
                dstates[h], du, dw, da, dqd, dkd, dgl = vjp_fn((dstates[h], do_ref[rows, sl]))
                da_ref[h, rows, :] = da
                dgl_ref[h, j] = dgl
                dus.append(du)
                dws.append(dw)
                dqds.append(dqd)
                dkds.append(dkd)
            du_ref[rows, :] = jnp.concatenate(dus, axis=1)
            dw_ref[rows, :] = jnp.concatenate(dws, axis=1)
            dqd_ref[rows, :] = jnp.concatenate(dqds, axis=1)
            dkd_ref[rows, :] = jnp.concatenate(dkds, axis=1)
        for h in range(hg):
            ds_ref[h] = dstates[h]

    row = pl.BlockSpec((sc * CHUNK, d), lambda n: (nst - 1 - n, 0))
    aspec = pl.BlockSpec((hg, sc * CHUNK, CHUNK), lambda n: (0, nst - 1 - n, 0))
    gspec = pl.BlockSpec((hg, sc, 1, GDN_DK), lambda n: (0, nst - 1 - n, 0, 0))
    rowshape = jax.ShapeDtypeStruct((cfg.rp, d), F32)
    return Part(
        body, [do, u, w, attn, qd, kd, gl, ss],
        [row, row, row, aspec, row, row, gspec, pl.BlockSpec((sc, hg, GDN_DK, GDN_DK), lambda n: (nst - 1 - n, 0, 0, 0))],
        [rowshape, rowshape, jax.ShapeDtypeStruct(attn.shape, F32), rowshape, rowshape, jax.ShapeDtypeStruct(gl.shape, F32)],
        [row, row, aspec, row, row, gspec],
        [pltpu.VMEM((hg, GDN_DK, GDN_DK), F32)])


def _ret_consts(cfg):
    hr = cfg.hr
    lg = np.log(1.0 - 2.0 ** (-5.0 - np.arange(hr, dtype=np.float64)))
    idx = np.arange(CHUNK, dtype=np.float64)
    tril = np.tril(np.ones((CHUNK, CHUNK), dtype=bool))
    dm = np.where(tril[None], np.exp((idx[:, None] - idx[None, :])[None] * lg[:, None, None]), 0.0)
    qdc = np.exp((idx[None, :] + 1.0) * lg[:, None])
    kdc = np.exp((CHUNK - 1.0 - idx[None, :]) * lg[:, None])
    gch = np.exp(CHUNK * lg)
    qdc = np.broadcast_to(qdc[:, :, None], (hr, CHUNK, RET_DK))
    kdc = np.broadcast_to(kdc[:, :, None], (hr, CHUNK, RET_DK))
    gch = np.broadcast_to(gch[:, None, None], (hr, 1, RET_DK))
    return tuple(jnp.asarray(np.ascontiguousarray(t), F32) for t in (dm, qdc, kdc, gch))


def _ret_scan_fwd(cfg, qr, kr, proj, consts):
    d, hr, nch, sc = cfg.d, cfg.hr, cfg.nch, cfg.sc
    nst = nch // sc
    dm, qdc, kdc, gch = consts

    def body(q_ref, k_ref, v_ref, dm_ref, qdc_ref, kdc_ref, g_ref, o_ref, ss_ref, s_ref):
        @pl.when(pl.program_id(0) == 0)
        def _():
            s_ref[...] = jnp.zeros(s_ref.shape, F32)

        states = [s_ref[h] for h in range(hr)]
        for j in range(sc):
            rows = slice(j * CHUNK, (j + 1) * CHUNK)
            outs = []
            for h in range(hr):
                sl = slice(h * RET_DK, (h + 1) * RET_DK)
                ss_ref[j, h] = states[h]
                states[h], o = _ret_step(states[h], q_ref[rows, sl], k_ref[rows, sl], v_ref[rows, sl], dm_ref[h],
                                         qdc_ref[h], kdc_ref[h], g_ref[h])
                outs.append(o)
            o_ref[rows, :] = jnp.concatenate(outs, axis=1)
        for h in range(hr):
            s_ref[h] = states[h]

    row = pl.BlockSpec((sc * CHUNK, d), lambda n: (n, 0))
    return Part(
        body, [qr, kr, proj, dm, qdc, kdc, gch],
        [row, row, pl.BlockSpec((sc * CHUNK, d), lambda n: (n, RV_BLOCK)), _full(dm), _full(qdc), _full(kdc), _full(gch)],
        [jax.ShapeDtypeStruct((cfg.rp, d), F32), jax.ShapeDtypeStruct((nch, hr, RET_DK, RET_DK), F32)],
        [row, pl.BlockSpec((sc, hr, RET_DK, RET_DK), lambda n: (n, 0, 0, 0))],
        [pltpu.VMEM((hr, RET_DK, RET_DK), F32)])


def _ret_scan_bwd(cfg, do, qr, kr, proj, consts, ss, dproj):
    d, hr, nch, sc = cfg.d, cfg.hr, cfg.nch, cfg.sc
    nst = nch // sc
    dm, qdc, kdc, gch = consts

    def body(do_ref, q_ref, k_ref, v_ref, dm_ref, qdc_ref, kdc_ref, g_ref, ss_ref, _, dq_ref, dk_ref, dv_ref, ds_ref):
        @pl.when(pl.program_id(0) == 0)
        def _():
            ds_ref[...] = jnp.zeros(ds_ref.shape, F32)

        dstates = [ds_ref[h] for h in range(hr)]
        for j in reversed(range(sc)):
            rows = slice(j * CHUNK, (j + 1) * CHUNK)
            dqs, dks, dvs = [], [], []
            for h in range(hr):
                sl = slice(h * RET_DK, (h + 1) * RET_DK)
                cs = (dm_ref[h], qdc_ref[h], kdc_ref[h], g_ref[h])
                _, vjp_fn = jax.vjp(lambda s, q, k, v, cs=cs: _ret_step(s, q, k, v, *cs),
                                    ss_ref[j, h], q_ref[rows, sl], k_ref[rows, sl], v_ref[rows, sl])
                dstates[h], dq, dk, dv = vjp_fn((dstates[h], do_ref[rows, sl]))
                dqs.append(dq)
                dks.append(dk)
                dvs.append(dv)
            dq_ref[rows, :] = jnp.concatenate(dqs, axis=1)
            dk_ref[rows, :] = jnp.concatenate(dks, axis=1)
            dv_ref[rows, :] = jnp.concatenate(dvs, axis=1).astype(dv_ref.dtype)
        for h in range(hr):
            ds_ref[h] = dstates[h]

    row = pl.BlockSpec((sc * CHUNK, d), lambda n: (nst - 1 - n, 0))
    rowshape = jax.ShapeDtypeStruct((cfg.rp, d), F32)
    vspec = pl.BlockSpec((sc * CHUNK, d), lambda n: (nst - 1 - n, RV_BLOCK))
    return Part(
        body, [do, qr, kr, proj, dm, qdc, kdc, gch, ss, dproj],
        [row, row, row, vspec, _full(dm), _full(qdc), _full(kdc), _full(gch),
         pl.BlockSpec((sc, hr, RET_DK, RET_DK), lambda n: (nst - 1 - n, 0, 0, 0)), ANY],
        [rowshape, rowshape, jax.ShapeDtypeStruct(dproj.shape, dproj.dtype)],
        [row, row, vspec],
        [pltpu.VMEM((hr, RET_DK, RET_DK), F32)], {9: 2})


def _final(cfg, h2, normf, tgt):
    d, tr = cfg.d, cfg.xrow
    nr = cfg.rp // tr

    def body(h_ref, g_ref, t_ref, dh_ref, dg_ref, loss_ref):
        i = pl.program_id(0)
        y, vjp_fn = jax.vjp(_rms, h_ref[...], g_ref[...])
        err = jnp.where(i >= 1, y - t_ref[...], 0.0)
        dh, dg = vjp_fn(err * (1.0 / d))
        dh_ref[...] = dh
        part = jnp.zeros((8, LANES), F32) + 0.5 * jnp.sum(err * err) * (1.0 / d)

        @pl.when(i == 0)
        def _():
            dg_ref[...] = dg
            loss_ref[...] = part

        @pl.when(i > 0)
        def _():
            dg_ref[...] += dg
            loss_ref[...] += part

    return _pallas(
        body,
        out_shape=[jax.ShapeDtypeStruct((cfg.rp, d), F32), jax.ShapeDtypeStruct((1, d), F32),
                   jax.ShapeDtypeStruct((8, LANES), F32)],
        grid=(nr,),
        in_specs=[_rows(tr, d), _full(normf), pl.BlockSpec((tr, d), lambda i: (jnp.maximum(i - 1, 0), 0))],
        out_specs=[_rows(tr, d), pl.BlockSpec((1, d), lambda i: (0, 0)), pl.BlockSpec((8, LANES), lambda i: (0, 0))],
        name="final_loss", compiler_params=_params((nr,)))(h2, normf, tgt)


ANY = pl.BlockSpec(memory_space=pl.ANY)


def _place():
    x, y, c = lax.axis_index("x"), lax.axis_index("y"), lax.axis_index("c")
    others = [(1 - x, y), (x, 1 - y), (1 - x, 1 - y)]
    return x, y, c, others


def _row_tile(rows, cap=256):
    return max(t for t in range(16, min(rows, cap) + 1, 16) if rows % t == 0)


class Comm(NamedTuple):
    ins: list
    outs: list
    sems: list
    start: object
    finish: object


def _run_comm(name, comm):
    n_in, n_out = len(comm.ins), len(comm.outs)

    def body(*refs):
        ins, outs, sems = refs[:n_in], refs[n_in:n_in + n_out], refs[n_in + n_out:]
        comm.start(ins, outs, sems)
        comm.finish(ins, outs, sems)

    return _pallas(body, out_shape=comm.outs, in_specs=[ANY] * n_in, out_specs=[ANY] * n_out,
                   scratch_shapes=comm.sems, name=name)(*comm.ins)


def _gather_comm(ws):
    n = len(ws)
    halves = [w.shape[0] // 2 for w in ws]

    def copies(w_refs, o_refs, sems):
        send_sems, recv_sems = sems
        x, y, c, others = _place()
        me = 2 * x + y
        chips = [2 * px + py for px, py in others]

        def piece(a, chip, core):
            return o_refs[a].at[chip, pl.ds(core * halves[a], halves[a]), :]

        def copy(a, k, src, chip, core, to):
            return pltpu.make_async_remote_copy(src_ref=src, dst_ref=piece(a, chip, core), send_sem=send_sems.at[6 * a + k],
                                                recv_sem=recv_sems.at[6 * a + k], device_id=to, device_id_type=MESH)

        def first(j, a):
            return copy(a, j, w_refs[a].at[pl.ds(c * halves[a], halves[a]), :], me, c, (*others[j], c))

        def landed(j, a):
            return copy(a, j, piece(a, chips[j], c), chips[j], c, (x, y, c))

        def passed(j, a):
            return copy(a, 3 + j, piece(a, chips[j], c), chips[j], c, (x, y, 1 - c))

        def from_sibling(j, a):
            return copy(a, 3 + j, piece(a, chips[j], 1 - c), chips[j], 1 - c, (x, y, c))

        return first, landed, passed, from_sibling

    pairs = [(j, a) for j in range(3) for a in range(n)]

    def start(w_refs, o_refs, sems):
        first, _, _, _ = copies(w_refs, o_refs, sems)
        for j, a in pairs:
            first(j, a).start()

    def finish(w_refs, o_refs, sems):
        first, landed, passed, from_sibling = copies(w_refs, o_refs, sems)
        for j, a in pairs:
            landed(j, a).wait_recv()
            passed(j, a).start()
        for j, a in pairs:
            from_sibling(j, a).wait_recv()
        for j, a in pairs:
            first(j, a).wait_send()
            passed(j, a).wait_send()

    return Comm(list(ws), [jax.ShapeDtypeStruct((N_CHIPS,) + w.shape, w.dtype) for w in ws],
                [pltpu.SemaphoreType.DMA((6 * n,)), pltpu.SemaphoreType.DMA((6 * n,))], start, finish)


def _pair_exchange(name, gs):
    n = len(gs)

    def body(*refs):
        g_refs, o_refs = refs[:n], refs[n:2 * n]
        send_sems, recv_sems = refs[2 * n:]
        x, y, c, _ = _place()
        cps = []
        for a in range(n):
            half = gs[a].shape[1] // 2
            cp = pltpu.make_async_remote_copy(
                src_ref=g_refs[a].at[:, pl.ds((1 - c) * half, half), :], dst_ref=o_refs[a], send_sem=send_sems.at[a],
                recv_sem=recv_sems.at[a], device_id=(x, y, 1 - c), device_id_type=MESH)
            cp.start()
            cps.append(cp)
        for cp in cps:
            cp.wait()

    return _pallas(
        body, out_shape=[jax.ShapeDtypeStruct((N_CHIPS, g.shape[1] // 2, g.shape[2]), g.dtype) for g in gs],
        in_specs=[ANY] * n, out_specs=[ANY] * n,
        scratch_shapes=[pltpu.SemaphoreType.DMA((n,)), pltpu.SemaphoreType.DMA((n,))], name=name)(*gs)


def _pair_sum(name, g, recv, cidx):
    half, cols = recv.shape[1], recv.shape[2]
    tr = _row_tile(half)
    nblk = half // tr

    def body(c_ref, g_ref, r_ref, o_ref):
        o_ref[...] = (g_ref[...] + r_ref[...]).astype(o_ref.dtype)

    grid_spec = pltpu.PrefetchScalarGridSpec(
        num_scalar_prefetch=1, grid=(N_CHIPS, nblk),
        in_specs=[pl.BlockSpec((1, tr, cols), lambda s, i, c: (s, c[0] * nblk + i, 0)),
                  pl.BlockSpec((1, tr, cols), lambda s, i, c: (s, i, 0))],
        out_specs=pl.BlockSpec((1, tr, cols), lambda s, i, c: (s, i, 0)))
    return _pallas(
        body, out_shape=jax.ShapeDtypeStruct((N_CHIPS, half, cols), BF16), grid_spec=grid_spec,
        name=name, compiler_params=_params((N_CHIPS, nblk)))(cidx, g, recv)


def _exchange_comm(parts):
    n = len(parts)

    def copies(p_refs, o_refs, sems):
        send_sems, recv_sems = sems
        x, y, c, others = _place()
        me = 2 * x + y

        def copy(a, j, src_chip, dst_chip):
            px, py = others[j]
            return pltpu.make_async_remote_copy(
                src_ref=p_refs[a].at[src_chip], dst_ref=o_refs[a].at[dst_chip], send_sem=send_sems.at[3 * a + j],
                recv_sem=recv_sems.at[3 * a + j], device_id=(px, py, c), device_id_type=MESH)

        def send(j, a):
            return copy(a, j, 2 * others[j][0] + others[j][1], me)

        def arrival(j, a):
            return copy(a, j, me, 2 * others[j][0] + others[j][1])

        return send, arrival

    pairs = [(j, a) for j in range(3) for a in range(n)]

    def start(p_refs, o_refs, sems):
        send, _ = copies(p_refs, o_refs, sems)
        for j, a in pairs:
            send(j, a).start()

    def finish(p_refs, o_refs, sems):
        send, arrival = copies(p_refs, o_refs, sems)
        for j, a in pairs:
            arrival(j, a).wait_recv()
        for j, a in pairs:
            send(j, a).wait_send()

    return Comm(list(parts), [jax.ShapeDtypeStruct(p.shape, p.dtype) for p in parts],
                [pltpu.SemaphoreType.DMA((3 * n,)), pltpu.SemaphoreType.DMA((3 * n,))], start, finish)


def _chip_sum(name, part, slots, chip):
    half, cols = slots.shape[1], slots.shape[2]
    tr = _row_tile(half)

    def body(me_ref, p_ref, *rest):
        s_refs, o_ref = rest[:N_CHIPS], rest[N_CHIPS]
        own = p_ref[...].astype(F32)
        v = [jnp.where(me_ref[0] == k, own, s_refs[k][...].astype(F32)) for k in range(N_CHIPS)]
        o_ref[...] = ((v[0] + v[1]) + v[2]) + v[3]

    def slot_spec(k):
        return pl.BlockSpec((None, tr, cols), lambda i, me: (jnp.where(me[0] == k, (k + 1) % N_CHIPS, k), i, 0))

    grid_spec = pltpu.PrefetchScalarGridSpec(
        num_scalar_prefetch=1, grid=(half // tr,),
        in_specs=[pl.BlockSpec((None, tr, cols), lambda i, me: (me[0], i, 0))] + [slot_spec(k) for k in range(N_CHIPS)],
        out_specs=pl.BlockSpec((tr, cols), lambda i, me: (i, 0)))
    return _pallas(
        body, out_shape=jax.ShapeDtypeStruct((half, cols), F32), grid_spec=grid_spec,
        name=name, compiler_params=_params((half // tr,)))(chip, part, *([slots] * N_CHIPS))


def _pair_swap(name, fins):
    n = len(fins)

    def body(*refs):
        f_refs, o_refs = refs[:n], refs[n:2 * n]
        send_sems, recv_sems = refs[2 * n:]
        x, y, c, _ = _place()
        cps = [pltpu.make_async_remote_copy(src_ref=f_refs[a], dst_ref=o_refs[a], send_sem=send_sems.at[a],
                                            recv_sem=recv_sems.at[a], device_id=(x, y, 1 - c), device_id_type=MESH)
               for a in range(n)]
        for cp in cps:
            cp.start()
        for cp in cps:
            cp.wait()

    return _pallas(
        body, out_shape=[jax.ShapeDtypeStruct(f.shape, f.dtype) for f in fins], in_specs=[ANY] * n, out_specs=[ANY] * n,
        scratch_shapes=[pltpu.SemaphoreType.DMA((n,)), pltpu.SemaphoreType.DMA((n,))], name=name)(*fins)


SC_TILES = 32
SC_CHUNK = 2048


def _adamw_sc(name, w, g_own, g_other, m, v, cidx):
    del cidx
    n = w.size
    per = n // SC_TILES
    assert n % (SC_TILES * SC_CHUNK) == 0, (name, n)
    c1 = 1.0 - ADAM_B1 ** ADAM_STEP
    c2 = 1.0 - ADAM_B2 ** ADAM_STEP

    def body(w_hbm, go_hbm, gs_hbm, m_hbm, v_hbm, g_out, d_out, nm_out, nv_out, wb, gb, mb, vb, db):
        tile = lax.axis_index("sc_tile") * 2 + lax.axis_index("sc_core")
        core = lax.axis_index("c")
        own = (tile < SC_TILES // 2) == (core == 0)
        base = tile * per
        hbase = (tile % (SC_TILES // 2)) * per

        @pl.loop(0, per, step=SC_CHUNK)
        def _(o):
            sl = pl.ds(base + o, SC_CHUNK)
            hs = pl.ds(hbase + o, SC_CHUNK)
            pltpu.sync_copy(w_hbm.at[sl], wb)
            pltpu.sync_copy(m_hbm.at[sl], mb)
            pltpu.sync_copy(v_hbm.at[sl], vb)

            @pl.when(own)
            def _():
                pltpu.sync_copy(go_hbm.at[hs], gb)

            @pl.when(jnp.logical_not(own))
            def _():
                pltpu.sync_copy(gs_hbm.at[hs], gb)

            @pl.loop(0, SC_CHUNK, step=16)
            def _(i):
                s = pl.ds(i, 16)
                gv = gb[s]
                nm = ADAM_B1 * mb[s] + (1.0 - ADAM_B1) * gv
                nv = ADAM_B2 * vb[s] + (1.0 - ADAM_B2) * (gv * gv)
                db[s] = -ADAM_LR * ((nm / c1) / (jnp.sqrt(nv / c2) + ADAM_EPS) + ADAM_WD * wb[s])
                mb[s] = nm
                vb[s] = nv

            pltpu.sync_copy(gb, g_out.at[sl])
            pltpu.sync_copy(db, d_out.at[sl])
            pltpu.sync_copy(mb, nm_out.at[sl])
            pltpu.sync_copy(vb, nv_out.at[sl])

    flat = jax.ShapeDtypeStruct((n,), F32)
    res = pl.kernel(
        body, name=name, out_type=[flat] * 4,
        mesh=plsc.VectorSubcoreMesh(core_axis_name="sc_core", subcore_axis_name="sc_tile"),
        scratch_types=[pltpu.VMEM((SC_CHUNK,), F32)] * 5,
    )(w.reshape(-1), g_own.reshape(-1), g_other.reshape(-1), m.reshape(-1), v.reshape(-1))
    return [r.reshape(w.shape) for r in res]


def _adamw(name, w, g_own, g_other, m, v, cidx):
    R, cols = w.shape[-2:]
    lead = (None,) * (w.ndim - 2)
    zeros = (0,) * (w.ndim - 2)
    half = R // 2
    tr = _row_tile(half, 128)
    nblk = half // tr
    c1 = 1.0 - ADAM_B1 ** ADAM_STEP
    c2 = 1.0 - ADAM_B2 ** ADAM_STEP

    def body(c_ref, w_ref, go_ref, gs_ref, m_ref, v_ref, g_ref, d_ref, nm_ref, nv_ref):
        mine = (pl.program_id(0) // nblk) == c_ref[0]
        gv = jnp.where(mine, go_ref[...], gs_ref[...])
        nm = ADAM_B1 * m_ref[...] + (1.0 - ADAM_B1) * gv
        nv = ADAM_B2 * v_ref[...] + (1.0 - ADAM_B2) * (gv * gv)
        g_ref[...] = gv
        d_ref[...] = -ADAM_LR * ((nm / c1) / (jnp.sqrt(nv / c2) + ADAM_EPS) + ADAM_WD * w_ref[...])
        nm_ref[...] = nm
        nv_ref[...] = nv

    spec = pl.BlockSpec(lead + (tr, cols), lambda i, c: zeros + (i, 0))
    hspec = pl.BlockSpec((tr, cols), lambda i, c: (i % nblk, 0))
    shape = jax.ShapeDtypeStruct(w.shape, F32)
    grid_spec = pltpu.PrefetchScalarGridSpec(num_scalar_prefetch=1, grid=(R // tr,),
                                             in_specs=[spec, hspec, hspec, spec, spec], out_specs=[spec] * 4)
    return _pallas(
        body, out_shape=[shape] * 4, grid_spec=grid_spec,
        name=name, compiler_params=_params((R // tr,)))(cidx, w, g_own, g_other, m, v)


PARAMS = (("meta", 1), ("norm1", None), ("w_in", 2), ("gdn_conv_w", 2), ("gdn_a_log", None), ("gdn_dt_bias", None),
          ("gdn_norm", None), ("w_out", 1), ("norm2", None), ("w_ffn_up", 2), ("ffn_conv_w", 2), ("ffn_conv_b", None),
          ("w_ffn_down", 1), ("norm_f", None))
BIG = ("w_in", "w_out", "w_ffn_up", "w_ffn_down")
PACK_ALIGN = 1024
PACK_ROWS_ALIGN = 32


def _pack(arrs, dtype):
    parts, total = [], 0
    for a in arrs:
        f = a.reshape(-1).astype(dtype)
        pad = (-f.shape[0]) % PACK_ALIGN
        parts.append(jnp.pad(f, (0, pad)) if pad else f)
        total += f.shape[0] + pad
    rows = total // LANES
    rpad = (-rows) % PACK_ROWS_ALIGN
    if rpad:
        parts.append(jnp.zeros((rpad * LANES,), dtype))
    return jnp.concatenate(parts).reshape(rows + rpad, LANES)


def _unpack(buf, shapes):
    flat = buf.reshape(-1)
    outs, off = [], 0
    for s in shapes:
        n = int(np.prod(s))
        outs.append(flat[off:off + n].reshape(s))
        off += n + (-n) % PACK_ALIGN
    return outs


def _split4(a, axis):
    n = a.shape[axis] // N_CHIPS
    return [lax.slice_in_dim(a, s * n, (s + 1) * n, axis=axis) for s in range(N_CHIPS)]


PROJ_ORDER = (3, 7, 8, 9, 0, 1, 2, 6, 4, 5)


def _reorder_w_in(w, cfg):
    d, hg = cfg.d, cfg.hg

    def block(k):
        off = k * d + (2 * hg if k >= 4 else 0)
        return w[:, off:off + d]

    tail = jnp.pad(w[:, 4 * d:4 * d + 2 * hg], ((0, 0), (0, LANES - 2 * hg)))
    return jnp.concatenate([block(k) for k in PROJ_ORDER] + [tail], axis=1)


def _restore_w_in(wr, cfg):
    d, hg = cfg.d, cfg.hg
    at = {k: i for i, k in enumerate(PROJ_ORDER)}
    block = lambda k: wr[:, at[k] * d:(at[k] + 1) * d]
    return jnp.concatenate([block(k) for k in range(4)] + [wr[:, 10 * d:10 * d + 2 * hg]] +
                           [block(k) for k in range(4, 10)], axis=1)


def _step(cfg, x, tgt, shard, m_shard, v_shard):
    d, hg, dff, rp, tr, tm = cfg.d, cfg.hg, cfg.dff, cfg.rp, cfg.tr, cfg.tm
    nrow = rp // tr
    assert cfg.tf * N_CHIPS == 2 * dff and cfg.din % N_CHIPS == 0
    cidx = lax.axis_index("c").astype(jnp.int32).reshape(1)
    chip = (2 * lax.axis_index("x") + lax.axis_index("y")).astype(jnp.int32).reshape(1)

    axis = dict(PARAMS)
    small = ("meta", "gdn_conv_w", "ffn_conv_w")
    small_shapes = [shard[n].shape for n in small]
    mine = [shard[n][0].astype(BF16) for n in BIG] + [_pack([shard[n] for n in small], F32)]

    def with_own(gathered, own):
        return [lax.dynamic_update_slice(g, w[None], (chip[0], 0, 0)) for g, w in zip(gathered, own)]

    g_in, g_small = _run_comm("weights_gather_first", _gather_comm([mine[0], mine[4]]))
    g_small, = with_own([g_small], [mine[4]])
    w_in_r = _reorder_w_in(jnp.concatenate([jnp.where(chip[0] == s, mine[0], g_in[s]) for s in range(N_CHIPS)], axis=1),
                           cfg)
    per_chip = [_unpack(g_small[s], small_shapes) for s in range(N_CHIPS)]
    full = {n: jnp.concatenate([per_chip[s][k] for s in range(N_CHIPS)], axis=axis[n]) for k, n in enumerate(small)}
    meta = full["meta"]
    gconv_w = full["gdn_conv_w"][0]
    fconv_w = full["ffn_conv_w"][0]
    norm1, norm2, gnorm = shard["norm1"], shard["norm2"], shard["gdn_norm"]
    normf = shard["norm_f"].reshape(1, d)
    fconv_b = shard["ffn_conv_b"]
    alog = jnp.pad(shard["gdn_a_log"], ((0, 7), (0, LANES - hg)))
    dtb = jnp.pad(shard["gdn_dt_bias"], ((0, 7), (0, LANES - hg)))

    h0 = jnp.concatenate([jnp.zeros((cfg.front, d), F32), meta, x], axis=0)
    half = RET_DK // 2
    pos = np.arange(rp, dtype=np.float32) - np.float32(cfg.front)
    inv = (np.float32(1.0) / np.float32(ROPE_BASE) ** (np.arange(half, dtype=np.float32) / np.float32(half))).astype(np.float32)
    ang = pos[:, None] * inv[None, :]
    cos, sin = jnp.asarray(np.cos(ang), F32), jnp.asarray(np.sin(ang), F32)
    rconsts = _ret_consts(cfg)

    tr_n = 3 * tr if rp % (3 * tr) == 0 else tr
    rms_f = _make_rms_fn(cfg, tr_n, False)
    rms_b = _make_rms_fn(cfg, tr_n, True)
    rowshape = jax.ShapeDtypeStruct((rp, d), F32)
    rspec = _rows(tr, d)
    nspec = _rows(tr_n, d)

    def rms_fwd(name, h, g):
        return _stage_fwd(name, rms_f, (rp // tr_n,), [In(h, nspec), In(g, _full(g))],
                          [jax.ShapeDtypeStruct((rp, d), BF16)], [nspec])[0]

    wm = 10 * d
    w_main, w_tail = w_in_r[:, :wm], w_in_r[:, wm:]
    tn_in = 2560 if wm % 2560 == 0 else LANES
    hn1 = rms_fwd("rms1_fwd", h0, norm1)
    proj, rest = _mm("proj_fwd", hn1, w_main, tm=tm, tn=tn_in, tk=d, out_dtype=BF16, comm=_gather_comm(mine[1:4]))
    ptail = _mm("proj_tail_fwd", hn1, w_tail, tm=tm, tn=LANES, tk=d)
    g_out, g_up, g_down = with_own(rest, mine[1:4])
    w_out = g_out.reshape(d, d)
    w_up = g_up
    w_up_t = jnp.swapaxes(g_up, 1, 2).reshape(2 * dff, d)
    w_down = g_down.reshape(dff, d)
    cqkv = _conv_fwd("gdn_conv_fwd", proj, CONV_COL * d, gconv_w, None, taps=GDN_CONV, width=3 * d, tr=tr, tc=d)
    prep_fn = _make_gdn_prep_fn(cfg, tr)
    prep_ins = [In(cqkv, _rows(tr, 3 * d), BF16), In(ptail, _rows(tr, LANES), BF16),
                In(alog, _full(alog), F32, True), In(dtb, _full(dtb), F32, True)]
    qn, kn, vv, bB, lB = _stage_fwd("gdn_prep_fwd", prep_fn, (nrow,), prep_ins, [rowshape] * 5, [rspec] * 5)

    trg = cfg.nb * CHUNK
    gi_grid = (rp // trg, hg)
    hspec = pl.BlockSpec((trg, GDN_DK), lambda i, h: (i, h))
    aspec = pl.BlockSpec((1, trg, CHUNK), lambda i, h: (h, i, 0))
    gspec = pl.BlockSpec((1, cfg.nb, 1, GDN_DK), lambda i, h: (h, i, 0, 0))
    intra_ins = [In(t, hspec, F32) for t in (qn, kn, vv, bB, lB)]
    ashape = jax.ShapeDtypeStruct((hg, rp, CHUNK), F32)
    intra_shapes = [rowshape, rowshape, ashape, rowshape, rowshape, jax.ShapeDtypeStruct((hg, cfg.nch, 1, GDN_DK), F32), ashape]
    intra_specs = [hspec, hspec, aspec, hspec, hspec, gspec, aspec]
    gu, gw, gattn, gqd, gkd, ggl, gtinv = _stage_fwd("gdn_intra_fwd", _gdn_intra_fn, gi_grid, intra_ins, intra_shapes,
                                                     intra_specs)
    rot_fn = _make_rot_fn(cfg)

    def rot_ins(dproj=None):
        return [In(proj, _rows(tr_n, 2 * d, ROT_COL // 2), BF16, galias=dproj, gshape=(rp, wm)),
                In(cos, _rows(tr_n, half)), In(sin, _rows(tr_n, half))]

    qr, kr = _stage_fwd("rot_fwd", rot_fn, (rp // tr_n,), rot_ins(), [rowshape] * 2, [nspec] * 2)
    nst = cfg.nch // cfg.sc
    oa, gss = _run_parts("gdn_scan_fwd", (nst,), [_gdn_scan_fwd(cfg, gu, gw, gattn, gqd, gkd, ggl)])[0]
    ob, rss = _run_parts("ret_scan_fwd", (nst,), [_ret_scan_fwd(cfg, qr, kr, proj, rconsts)])[0]

    mix_fn = _make_mix_fn(cfg)
    mix_ins = [In(oa, rspec, F32), In(ob, rspec, F32), In(proj, _rows(tr, 4 * d, MIX_COL // 4), BF16, gshape=(rp, wm)),
               In(gnorm, _full(gnorm), F32, True)]
    ymix = _stage_fwd("mix_fwd", mix_fn, (nrow,), mix_ins, [jax.ShapeDtypeStruct((rp, d), BF16)], [rspec])[0]
    h1 = _mm("out_proj_fwd", ymix, w_out, tm=tm, tn=d, tk=d, add=h0)

    hn2 = rms_fwd("rms2_fwd", h1, norm2)
    up = _mm("ffn_up_fwd", hn2, w_up, tm=tm, tn=cfg.tf, tk=d, out_dtype=BF16)
    uc = _conv_fwd("ffn_conv_fwd", up, 0, fconv_w, fconv_b, taps=FFN_CONV, width=2 * dff, tr=tr, tc=cfg.tf)
    tra = tr
    act_ins = [In(uc, _rows(tra, 2 * dff), BF16)]
    act_spec = _rows(tra, dff)
    act = _stage_fwd("ffn_act_fwd", _act_fn, (rp // tra,), act_ins, [jax.ShapeDtypeStruct((rp, dff), BF16)], [act_spec])[0]
    h2 = _mm("ffn_down_fwd", act, w_down, tm=tm, tn=d, tk=cfg.tf, add=h1)

    dh2, g_normf, loss_blk = _final(cfg, h2, normf, tgt)
    loss = lax.psum(loss_blk[0, 0], ("x", "y", "c"))

    g_w_down = _mm_tn("ffn_down_dw", act, dh2, tr=tm, tka=cfg.tf, tn=d)
    dact = _mm("ffn_down_dx", dh2, w_down.T, tm=tm, tn=cfg.tf, tk=d)
    duc, = _stage_bwd("ffn_act_bwd", _act_fn, (rp // tra,), act_ins, [(dact, act_spec)])
    dup, g_fconv_w, g_fconv_b = _conv_bwd("ffn_conv_bwd", up, 0, fconv_w, duc, taps=FFN_CONV, width=2 * dff,
                                          tr=tr, tc=cfg.tf, with_bias=True)
    g_w_up = _mm_tn("ffn_up_dw", hn2, dup, tr=tm, tka=d, tn=cfg.tf, blocked=True)

    def pair_reduce(tag, names, arrs):
        recvs = _pair_exchange("grad_pair_exchange_" + tag, arrs)
        return [_pair_sum("grad_pair_sum_" + n, g, r, cidx) for n, g, r in zip(names, arrs, recvs)]

    parts_ffn = pair_reduce("ffn", ["w_ffn_down", "w_ffn_up"], [g_w_down.reshape(N_CHIPS, dff // N_CHIPS, d), g_w_up])
    dhn2, slots_ffn = _mm("ffn_up_dx", dup, w_up_t, tm=tm, tn=d, tk=2 * cfg.tf, comm=_exchange_comm(parts_ffn))
    ffn_names = ["w_ffn_down", "w_ffn_up"]
    fins_ffn = [_chip_sum("grad_chip_sum_" + t, p, s, chip) for t, p, s in zip(ffn_names, parts_ffn, slots_ffn)]
    sibs_ffn = _pair_swap("grad_pair_swap_ffn", fins_ffn)
    outs = {t: _adamw_sc("adamw_" + t, shard[t], fins_ffn[k], sibs_ffn[k], m_shard[t], v_shard[t], cidx)
            for k, t in enumerate(ffn_names)}

    def rms_bwd(name, h, g, dhn, dres):
        ins = [In(h, nspec, F32), In(g, _full(g), F32, True)]
        return _stage_bwd(name, rms_b, (rp // tr_n,), ins, [(dhn, nspec), (dres, nspec)])

    dh1, g_norm2 = rms_bwd("rms2_bwd", h1, norm2, dhn2, dh2)
    g_w_out = _mm_tn("out_proj_dw", ymix, dh1, tr=tm, tka=d, tn=d)
    dymix = _mm("out_proj_dx", dh1, w_out.T, tm=tm, tn=d, tk=d)
    doa, dob, dproj, g_gnorm = _stage_bwd("mix_bwd", mix_fn, (nrow,), mix_ins, [(dymix, rspec)])

    dqr, dkr, dproj = _run_parts("ret_scan_bwd", (nst,), [_ret_scan_bwd(cfg, dob, qr, kr, proj, rconsts, rss, dproj)])[0]
    dproj, = _stage_bwd("rot_bwd", rot_fn, (rp // tr_n,), rot_ins(dproj), [(dqr, nspec), (dkr, nspec)])
    dgu, dgw, dgattn, dgqd, dgkd, dggl = _run_parts(
        "gdn_scan_bwd", (nst,), [_gdn_scan_bwd(cfg, doa, gu, gw, gattn, gqd, gkd, ggl, gss)])[0]

    intra_cots = [(dgu, hspec), (dgw, hspec), (dgattn, aspec), (dgqd, hspec), (dgkd, hspec), (dggl, gspec)]
    dqn, dkn, dvv, dbB, dlB = _stage_bwd("gdn_intra_bwd", _gdn_intra_fn, gi_grid, intra_ins + [In(gtinv, aspec)], intra_cots)
    dcqkv, dtail, g_alog, g_dtb = _stage_bwd(
        "gdn_prep_bwd", prep_fn, (nrow,), prep_ins, [(t, rspec) for t in (dqn, dkn, dvv, dbB, dlB)])
    dproj, g_gconv_w = _conv_bwd("gdn_conv_bwd", proj, CONV_COL * d, gconv_w, dcqkv, taps=GDN_CONV, width=3 * d,
                                 tr=tr, tc=d, with_bias=False, dx_into=dproj)
    g_w_in_r = jnp.concatenate([_mm_tn("proj_dw", hn1, dproj, tr=tm, tka=d, tn=tn_in),
                                _mm_tn("proj_tail_dw", hn1, dtail, tr=tm, tka=d, tn=LANES)], axis=1)
    g_in4 = jnp.stack(_split4(_restore_w_in(g_w_in_r, cfg), 1))
    parts_mix = pair_reduce("mix", ["w_out", "w_in"], [g_w_out.reshape(N_CHIPS, d // N_CHIPS, d), g_in4])
    dhn1_tail = _mm("proj_tail_dx", dtail, w_tail.T, tm=tm, tn=d, tk=LANES)
    dhn1, slots_mix = _mm("proj_dx", dproj, w_main.T, tm=tm, tn=d, tk=tn_in // 2 if tn_in > LANES else LANES, add=dhn1_tail,
                          comm=_exchange_comm(parts_mix))
    dh0, g_norm1 = rms_bwd("rms1_bwd", h0, norm1, dhn1, dh1)

    grad_x = dh0[cfg.xrow:]
    small_grads = {
        "meta": dh0[cfg.front:cfg.xrow], "norm1": g_norm1, "gdn_conv_w": g_gconv_w[None],
        "gdn_a_log": g_alog[0:1, :hg], "gdn_dt_bias": g_dtb[0:1, :hg], "gdn_norm": g_gnorm, "norm2": g_norm2,
        "ffn_conv_w": g_fconv_w[None], "ffn_conv_b": g_fconv_b, "norm_f": g_normf.reshape(d),
    }

    small_names = [n for n, _ in PARAMS if n not in BIG]
    g_small = jnp.stack([_pack([small_grads[n] if axis[n] is None else _split4(small_grads[n], axis[n])[s]
                                for n in small_names], F32) for s in range(N_CHIPS)])
    parts_small = pair_reduce("small", ["small"], [g_small])
    slots_small = _run_comm("grad_exchange_small", _exchange_comm(parts_small))
    tags = ["w_in", "w_out", "small"]
    parts = [parts_mix[1], parts_mix[0], parts_small[0]]
    slots = [slots_mix[1], slots_mix[0], slots_small[0]]
    fins = [_chip_sum("grad_chip_sum_" + t, p, s, chip) for t, p, s in zip(tags, parts, slots)]
    sibs = _pair_swap("grad_pair_swap", fins)

    def flat2(a):
        return a.reshape(-1, a.shape[-1])

    for k, t in enumerate(tags[:2]):
        res = _adamw("adamw_" + t, flat2(shard[t]), fins[k], sibs[k], flat2(m_shard[t]), flat2(v_shard[t]), cidx)
        outs[t] = [r.reshape(shard[t].shape) for r in res]
    small_shapes_all = [shard[n].shape for n in small_names]
    pk = lambda src: _pack([src[n] for n in small_names], F32)
    res = _adamw("adamw_small", pk(shard), fins[2], sibs[2], pk(m_shard), pk(v_shard), cidx)
    for k, r in enumerate(res):
        for n, a in zip(small_names, _unpack(r, small_shapes_all)):
            outs.setdefault(n, [None] * 4)[k] = a
    names = [n for n, _ in PARAMS]
    return (loss, grad_x[None], *[outs[n][k] for k in range(4) for n in names])


def kernel(x, meta, norm1, w_in, gdn_conv_w, gdn_a_log, gdn_dt_bias, gdn_norm, w_out, norm2, w_ffn_up, ffn_conv_w, ffn_conv_b, w_ffn_down, norm_f, loss_target, m_meta, m_norm1, m_w_in, m_gdn_conv_w, m_gdn_a_log, m_gdn_dt_bias, m_gdn_norm, m_w_out, m_norm2, m_w_ffn_up, m_ffn_conv_w, m_ffn_conv_b, m_w_ffn_down, m_norm_f, v_meta, v_norm1, v_w_in, v_gdn_conv_w, v_gdn_a_log, v_gdn_dt_bias, v_gdn_norm, v_w_out, v_norm2, v_w_ffn_up, v_ffn_conv_w, v_ffn_conv_b, v_w_ffn_down, v_norm_f):
    names = [n for n, _ in PARAMS]
    shard = dict(zip(names, (meta, norm1, w_in, gdn_conv_w, gdn_a_log, gdn_dt_bias, gdn_norm, w_out, norm2, w_ffn_up,
                             ffn_conv_w, ffn_conv_b, w_ffn_down, norm_f)))
    m_shard = dict(zip(names, (m_meta, m_norm1, m_w_in, m_gdn_conv_w, m_gdn_a_log, m_gdn_dt_bias, m_gdn_norm, m_w_out,
                               m_norm2, m_w_ffn_up, m_ffn_conv_w, m_ffn_conv_b, m_w_ffn_down, m_norm_f)))
    v_shard = dict(zip(names, (v_meta, v_norm1, v_w_in, v_gdn_conv_w, v_gdn_a_log, v_gdn_dt_bias, v_gdn_norm, v_w_out,
                               v_norm2, v_w_ffn_up, v_ffn_conv_w, v_ffn_conv_b, v_w_ffn_down, v_norm_f)))
    return _step(REAL, x[0], loss_target[0], shard, m_shard, v_shard)
```

```python
import functools
from typing import NamedTuple

import numpy as np
import jax
import jax.numpy as jnp
from jax import lax
from jax.experimental import pallas as pl
from jax.experimental.pallas import tpu as pltpu

F32 = jnp.float32
BF16 = jnp.bfloat16
EPS = 1e-6
CHUNK = 64
GDN_DK = 128
RET_DK = 256
GDN_CONV = 4
FFN_CONV = 3
ROPE_BASE = 10000.0
LANES = 128
N_CHIPS = 4
ADAM_LR, ADAM_B1, ADAM_B2, ADAM_EPS, ADAM_WD, ADAM_STEP = 0.001, 0.9, 0.999, 1e-08, 0.01, 10
MIX_COL, CONV_COL, RV_BLOCK, ROT_COL = 0, 4, 7, 8
MESH = pl.DeviceIdType.MESH
VMEM_LIMIT = 56 * 1024 * 1024


class Cfg(NamedTuple):
    d: int
    seq: int
    n_meta: int
    dff: int
    tr: int
    nb: int
    tm: int
    tf: int
    sc: int

    @property
    def hg(self): return self.d // GDN_DK
    @property
    def hr(self): return self.d // RET_DK
    @property
    def L(self): return self.n_meta + self.seq
    @property
    def rp(self): return -(-self.L // 256) * 256
    @property
    def front(self): return self.rp - self.L
    @property
    def xrow(self): return self.rp - self.seq
    @property
    def nch(self): return self.rp // CHUNK
    @property
    def din(self): return 10 * self.d + 2 * self.hg


REAL = Cfg(d=1024, seq=8192, n_meta=16, dff=2816, tr=256, nb=12, tm=1408, tf=1408, sc=4)


def _pallas(body, **kw):
    return pl.pallas_call(body, **kw)


def _sigmoid_raw(x):
    return 1.0 / (1.0 + jnp.exp(-x))


@jax.custom_vjp
def _sigmoid(x):
    return _sigmoid_raw(x)


def _sigmoid_fwd(x):
    s = _sigmoid_raw(x)
    return s, s


def _sigmoid_bwd(s, g):
    return (g * (s * (1.0 - s)),)


_sigmoid.defvjp(_sigmoid_fwd, _sigmoid_bwd)


@jax.custom_vjp
def _silu(x):
    return x * _sigmoid_raw(x)


def _silu_fwd(x):
    s = _sigmoid_raw(x)
    return x * s, (x, s)


def _silu_bwd(res, g):
    x, s = res
    return (g * (s * (1.0 + x * (1.0 - s))),)


_silu.defvjp(_silu_fwd, _silu_bwd)


def _softplus(x):
    return jnp.maximum(x, 0.0) + jnp.log(1.0 + jnp.exp(-jnp.abs(x)))


def _raw_dot(a, b, ta, tb, hi):
    if not hi:
        a = a.astype(BF16)
        b = b.astype(BF16)
    nbatch = a.ndim - 2
    ca = a.ndim - 2 if ta else a.ndim - 1
    cb = b.ndim - 1 if tb else b.ndim - 2
    batch = tuple(range(nbatch))
    return lax.dot_general(a, b, (((ca,), (cb,)), (batch, batch)),
                           precision=lax.Precision.HIGHEST if hi else None,
                           preferred_element_type=F32)


@functools.partial(jax.custom_vjp, nondiff_argnums=(2, 3, 4))
def _dot_p(a, b, ta, tb, hi):
    return _raw_dot(a, b, ta, tb, hi)


def _dot(a, b, ta=False, tb=False, hi=False):
    return _dot_p(a, b, ta, tb, hi)


def _dot_fwd(a, b, ta, tb, hi):
    return _raw_dot(a, b, ta, tb, hi), (a, b)


def _dot_bwd(ta, tb, hi, res, g):
    a, b = res
    if not ta and not tb:
        da, db = _dot(g, b, False, True, hi), _dot(a, g, True, False, hi)
    elif not ta and tb:
        da, db = _dot(g, b, False, False, hi), _dot(g, a, True, False, hi)
    elif ta and not tb:
        da, db = _dot(b, g, False, True, hi), _dot(a, g, False, False, hi)
    else:
        raise NotImplementedError
    return da.astype(a.dtype), db.astype(b.dtype)


_dot_p.defvjp(_dot_fwd, _dot_bwd)


def _iota2(n, m, axis):
    return lax.broadcasted_iota(jnp.int32, (n, m), axis)


def _bcast(mat, nb):
    return jnp.broadcast_to(mat[None], (nb,) + mat.shape)


def _split3(a):
    a0 = a.astype(BF16)
    r1 = a - a0.astype(F32)
    a1 = r1.astype(BF16)
    return a0, a1, (r1 - a1.astype(F32)).astype(BF16)


@functools.partial(jax.custom_vjp, nondiff_argnums=(2,))
def _dot_sel(a, e, te):
    eb = e.astype(BF16)
    p0, p1, p2 = (_raw_dot(p, eb, False, te, False) for p in _split3(a))
    return p0 + (p1 + p2)


def _dot_sel_fwd(a, e, te):
    return _dot_sel(a, e, te), e


def _dot_sel_bwd(te, e, g):
    return _dot_sel(g, e, not te), jnp.zeros_like(e)


_dot_sel.defvjp(_dot_sel_fwd, _dot_sel_bwd)


@jax.custom_vjp
def _sel_dot(e, x):
    eb = e.astype(BF16)
    p0, p1, p2 = (_raw_dot(eb, p, False, False, False) for p in _split3(x))
    return p0 + (p1 + p2)


def _sel_dot_fwd(e, x):
    return _sel_dot(e, x), e


def _sel_dot_bwd(e, g):
    eb = e.astype(BF16)
    p0, p1, p2 = (_raw_dot(eb, p, True, False, False) for p in _split3(g))
    return jnp.zeros_like(e), p0 + (p1 + p2)


_sel_dot.defvjp(_sel_dot_fwd, _sel_dot_bwd)


def _tri_inv_raw(m):
    nb = m.shape[0]
    r, c = _iota2(CHUNK, CHUNK, 0), _iota2(CHUNK, CHUNK, 1)
    t = _bcast((r == c).astype(F32), nb)
    b = 1
    while b < CHUNK:
        sh = b.bit_length() - 1
        off = ((r >> (sh + 1)) == (c >> (sh + 1))) & ((r >> sh) != (c >> sh)) & (r > c)
        cl = jnp.where(off[None], m, 0.0)
        t = t - _raw_dot(_raw_dot(t, cl, False, False, False), t, False, False, False)
        b *= 2
    return t


@jax.custom_vjp
def _tri_inv_given(m, t):
    return t


def _tri_inv_fwd(m, t):
    return t, t


def _tri_inv_bwd(t, g):
    return -_raw_dot(_raw_dot(t, g, True, False, False), t, False, True, False), jnp.zeros_like(t)


_tri_inv_given.defvjp(_tri_inv_fwd, _tri_inv_bwd)


def _rms(h, g):
    return h * lax.rsqrt(jnp.mean(h * h, axis=-1, keepdims=True) + EPS) * g


class In(NamedTuple):
    arr: jax.Array
    spec: pl.BlockSpec
    grad: object = None
    acc: bool = False
    gshape: object = None
    gspec: object = None
    galias: object = None


def _params(grid):
    sem = ("arbitrary",) * len(grid)
    return pltpu.CompilerParams(dimension_semantics=sem, vmem_limit_bytes=VMEM_LIMIT)


def _stage_fwd(name, fn, grid, ins, out_shapes, out_specs):
    n_in = len(ins)

    def body(*refs):
        pids = tuple(pl.program_id(k) for k in range(len(grid)))
        vals = [r[...].astype(F32) for r in refs[:n_in]]
        outs = fn(pids, *vals)
        for o_ref, o in zip(refs[n_in:], outs):
            o_ref[...] = o.reshape(o_ref.shape).astype(o_ref.dtype)

    return _pallas(
        body, out_shape=out_shapes, grid=grid, in_specs=[i.spec for i in ins],
        out_specs=out_specs, name=name, compiler_params=_params(grid))(*[i.arr for i in ins])


def _stage_bwd(name, fn, grid, ins, cots):
    n_in, n_ct = len(ins), len(cots)
    didx = [k for k, i in enumerate(ins) if i.grad is not None]
    aliased = [(o, ins[k].galias) for o, k in enumerate(didx) if ins[k].galias is not None]
    n_al = len(aliased)

    def body(*refs):
        pids = tuple(pl.program_id(k) for k in range(len(grid)))
        vals = [r[...].astype(F32) for r in refs[:n_in]]
        ct_refs = refs[n_in:n_in + n_ct]
        g_refs = refs[n_in + n_ct + n_al:]

        def f(*dv):
            merged = list(vals)
            for k, v in zip(didx, dv):
                merged[k] = v
            return tuple(fn(pids, *merged))

        outs, vjp_fn = jax.vjp(f, *[vals[k].astype(F32) for k in didx])
        cts = tuple(c[...].reshape(o.shape).astype(F32) for c, o in zip(ct_refs, outs))
        grads = vjp_fn(cts)
        first = functools.reduce(jnp.logical_and, [p == 0 for p in pids])
        for k, g_ref, g in zip(didx, g_refs, grads):
            if ins[k].acc:
                @pl.when(first)
                def _(g_ref=g_ref):
                    g_ref[...] = jnp.zeros(g_ref.shape, g_ref.dtype)
                g_ref[...] += g.reshape(g_ref.shape).astype(g_ref.dtype)
            else:
                g_ref[...] = g.reshape(g_ref.shape).astype(g_ref.dtype)

    out_shapes = [jax.ShapeDtypeStruct(ins[k].gshape or ins[k].arr.shape, ins[k].grad) for k in didx]
    out_specs = [ins[k].gspec or ins[k].spec for k in didx]
    return _pallas(
        body, out_shape=out_shapes, grid=grid,
        in_specs=[i.spec for i in ins] + [c[1] for c in cots] + [ANY] * n_al, out_specs=out_specs,
        input_output_aliases={n_in + n_ct + a: o for a, (o, _) in enumerate(aliased)},
        name=name, compiler_params=_params(grid))(*[i.arr for i in ins], *[c[0] for c in cots], *[a for _, a in aliased])


def _full(arr):
    nd = arr.ndim
    return pl.BlockSpec(arr.shape, lambda *p: (0,) * nd)


def _rows(tr, width, blk=0):
    return pl.BlockSpec((tr, width), lambda i: (i, blk))


def _mm(name, a, b, *, tm, tn, tk, out_dtype=F32, add=None, comm=None):
    M, K = a.shape
    N = b.shape[1] if b.ndim == 2 else b.shape[0] * b.shape[2]
    nk = K // tk
    grid = (M // tm, N // tn, nk)
    n_in = 3 if add is not None else 2
    n_ci, n_co = (len(comm.ins), len(comm.outs)) if comm is not None else (0, 0)

    def body(*refs):
        a_ref, b_ref = refs[0], refs[1]
        add_ref = refs[2] if add is not None else None
        c_ins = refs[n_in:n_in + n_ci]
        o_ref = refs[n_in + n_ci]
        c_outs = refs[n_in + n_ci + 1:n_in + n_ci + 1 + n_co]
        scratch = refs[n_in + n_ci + 1 + n_co:]
        acc_ref = scratch[0] if nk > 1 else None
        sems = scratch[1 if nk > 1 else 0:]
        step = (pl.program_id(0) * grid[1] + pl.program_id(1)) * nk + pl.program_id(2)
        if comm is not None:
            @pl.when(step == 0)
            def _():
                comm.start(c_ins, c_outs, sems)

        part = _raw_dot(a_ref[...], b_ref[...], False, False, False)

        def finish(total):
            if add_ref is not None:
                total = total + add_ref[...]
            o_ref[...] = total.astype(o_ref.dtype)

        if nk == 1:
            finish(part)
        else:
            k = pl.program_id(2)

            @pl.when(k == 0)
            def _():
                acc_ref[...] = part

            @pl.when(k > 0)
            def _():
                acc_ref[...] += part

            @pl.when(k == nk - 1)
            def _():
                finish(acc_ref[...])

        if comm is not None:
            @pl.when(step == grid[0] * grid[1] * nk - 1)
            def _():
                comm.finish(c_ins, c_outs, sems)

    b_spec = (pl.BlockSpec((tk, tn), lambda i, j, k: (k, j)) if b.ndim == 2 else
              pl.BlockSpec((None, tk, tn), lambda i, j, k: (j, k, 0)))
    in_specs = [pl.BlockSpec((tm, tk), lambda i, j, k: (i, k)), b_spec]
    args = [a, b]
    if add is not None:
        in_specs.append(pl.BlockSpec((tm, tn), lambda i, j, k: (i, j)))
        args.append(add)
    out_shape = jax.ShapeDtypeStruct((M, N), out_dtype)
    out_spec = pl.BlockSpec((tm, tn), lambda i, j, k: (i, j))
    scratch = [pltpu.VMEM((tm, tn), F32)] if nk > 1 else []
    if comm is None:
        return _pallas(body, out_shape=out_shape, grid=grid, in_specs=in_specs, out_specs=out_spec,
                       scratch_shapes=scratch, name=name, compiler_params=_params(grid))(*args)
    res = _pallas(body, out_shape=[out_shape] + comm.outs, grid=grid, in_specs=in_specs + [ANY] * n_ci,
                  out_specs=[out_spec] + [ANY] * n_co, scratch_shapes=scratch + comm.sems, name=name,
                  compiler_params=_params(grid))(*args, *comm.ins)
    return res[0], res[1:]


def _mm_tn(name, a, b, *, tr, tka, tn, blocked=False):
    R, Ka = a.shape
    N = b.shape[1]
    nr = R // tr
    grid = (Ka // tka, N // tn, nr)
    if blocked:
        out_shape = jax.ShapeDtypeStruct((N // tn, Ka, tn), F32)
        out_spec = pl.BlockSpec((None, tka, tn), lambda i, j, r: (j, i, 0))
    else:
        out_shape = jax.ShapeDtypeStruct((Ka, N), F32)
        out_spec = pl.BlockSpec((tka, tn), lambda i, j, r: (i, j))

    def body(a_ref, b_ref, o_ref):
        r = pl.program_id(2)
        part = _raw_dot(a_ref[...], b_ref[...], True, False, False)

        @pl.when(r == 0)
        def _():
            o_ref[...] = part

        @pl.when(r > 0)
        def _():
            o_ref[...] += part

    return _pallas(
        body, out_shape=out_shape, grid=grid,
        in_specs=[pl.BlockSpec((tr, tka), lambda i, j, r: (r, i)),
                  pl.BlockSpec((tr, tn), lambda i, j, r: (r, j))],
        out_specs=out_spec, name=name, compiler_params=_params(grid))(a, b)


def _conv_fwd(name, x, xcol0, w, b, *, taps, width, tr, tc):
    R = x.shape[0]
    grid = (width // tc, R // tr)
    cb0 = xcol0 // tc
    hrows = 16 if x.dtype == BF16 else 8
    hb = tr // hrows

    def body(*refs):
        x_ref, xp_ref, w_ref = refs[:3]
        b_ref = refs[3] if b is not None else None
        o_ref = refs[-1]
        i = pl.program_id(1)
        xv = x_ref[...].astype(F32)
        prev = jnp.where(i > 0, xp_ref[...].astype(F32)[hrows - 8:, :], 0.0)
        ext = jnp.concatenate([prev, xv], axis=0)
        acc = xv * w_ref[taps - 1:taps, :]
        for s in range(1, taps):
            acc = acc + pltpu.roll(ext, s, 0)[8:, :] * w_ref[taps - 1 - s:taps - s, :]
        if b_ref is not None:
            acc = acc + b_ref[...]
        o_ref[...] = acc.astype(o_ref.dtype)

    in_specs = [pl.BlockSpec((tr, tc), lambda j, i: (i, cb0 + j)),
                pl.BlockSpec((hrows, tc), lambda j, i: (jnp.maximum(i * hb - 1, 0), cb0 + j)),
                pl.BlockSpec((taps, tc), lambda j, i: (0, j))]
    args = [x, x, w]
    if b is not None:
        in_specs.append(pl.BlockSpec((1, tc), lambda j, i: (0, j)))
        args.append(b)
    return _pallas(
        body, out_shape=jax.ShapeDtypeStruct((R, width), BF16), grid=grid, in_specs=in_specs,
        out_specs=pl.BlockSpec((tr, tc), lambda j, i: (i, j)),
        name=name, compiler_params=_params(grid))(*args)


def _conv_bwd(name, x, xcol0, w, dy, *, taps, width, tr, tc, with_bias, dx_into=None):
    R = x.shape[0]
    nr = R // tr
    grid = (width // tc, nr)
    cb0 = xcol0 // tc
    hrows = 16 if dy.dtype == BF16 else 8
    hb = tr // hrows
    n_ext = tr + 8
    n_al = 0 if dx_into is None else 1

    def body(*refs):
        x_ref, w_ref, dy_ref, dyn_ref = refs[:4]
        dx_ref, dw_ref = refs[4 + n_al], refs[5 + n_al]
        db_ref = refs[6 + n_al] if with_bias else None
        i = pl.program_id(1)
        xv = x_ref[...].astype(F32)
        dyv = dy_ref[...].astype(F32)
        nxt = dyn_ref[...].astype(F32)[:8, :]
        dext = jnp.concatenate([dyv, jnp.where(i < nr - 1, nxt, 0.0)], axis=0)
        dx = dyv * w_ref[taps - 1:taps, :]
        dws = [None] * taps
        dws[taps - 1] = jnp.sum(xv * dyv, axis=0, keepdims=True)
        for s in range(1, taps):
            ahead = pltpu.roll(dext, n_ext - s, 0)[:tr, :]
            dx = dx + ahead * w_ref[taps - 1 - s:taps - s, :]
            dws[taps - 1 - s] = jnp.sum(xv * ahead, axis=0, keepdims=True)
        dx_ref[...] = dx.astype(dx_ref.dtype)

        @pl.when(i == 0)
        def _():
            for k in range(taps):
                dw_ref[k:k + 1, :] = dws[k]
            if db_ref is not None:
                db_ref[...] = jnp.sum(dyv, axis=0, keepdims=True)

        @pl.when(i > 0)
        def _():
            for k in range(taps):
                dw_ref[k:k + 1, :] += dws[k]
            if db_ref is not None:
                db_ref[...] += jnp.sum(dyv, axis=0, keepdims=True)

    in_specs = [pl.BlockSpec((tr, tc), lambda j, i: (i, cb0 + j)),
                pl.BlockSpec((taps, tc), lambda j, i: (0, j)),
                pl.BlockSpec((tr, tc), lambda j, i: (i, j)),
                pl.BlockSpec((hrows, tc), lambda j, i: (jnp.minimum((i + 1) * hb, R // hrows - 1), j))]
    args = [x, w, dy, dy]
    if dx_into is None:
        dx_shape, dx_spec, aliases = jax.ShapeDtypeStruct((R, width), BF16), pl.BlockSpec((tr, tc), lambda j, i: (i, j)), {}
    else:
        dx_shape = jax.ShapeDtypeStruct(dx_into.shape, dx_into.dtype)
        dx_spec, aliases = pl.BlockSpec((tr, tc), lambda j, i: (i, cb0 + j)), {4: 0}
        in_specs.append(ANY)
        args.append(dx_into)
    out_shape = [dx_shape, jax.ShapeDtypeStruct((taps, width), F32)]
    out_specs = [dx_spec, pl.BlockSpec((taps, tc), lambda j, i: (0, j))]
    if with_bias:
        out_shape.append(jax.ShapeDtypeStruct((1, width), F32))
        out_specs.append(pl.BlockSpec((1, tc), lambda j, i: (0, j)))
    return _pallas(
        body, out_shape=out_shape, grid=grid, in_specs=in_specs, out_specs=out_specs, input_output_aliases=aliases,
        name=name, compiler_params=_params(grid))(*args)


def _row_mask(cfg, i, tr):
    rows = i * tr + lax.broadcasted_iota(jnp.int32, (tr, 1), 0)
    return (rows >= cfg.front).astype(F32)


def _make_rms_fn(cfg, tr, with_residual):
    def fn(pids, h, g):
        hm = h * _row_mask(cfg, pids[0], tr)
        if with_residual:
            return _rms(hm, g), hm
        return (_rms(hm, g),)
    return fn


def _make_gdn_prep_fn(cfg, tr):
    d, hg = cfg.d, cfg.hg

    def fn(pids, c, tail, alog, dtb):
        cq, ck, cv = c[:, :d], c[:, d:2 * d], c[:, 2 * d:]
        mask = _row_mask(cfg, pids[0], tr)
        j, col = _iota2(LANES, d, 0), _iota2(LANES, d, 1)
        ea = ((col >> 7) == j).astype(F32)
        eb = ((col >> 7) + hg == j).astype(F32)
        al = jnp.sum(alog, axis=0, keepdims=True)
        db = jnp.sum(dtb, axis=0, keepdims=True)
        lg = _dot_sel(-jnp.exp(al) * _softplus(tail + db) * mask, ea, False)
        beta = _dot_sel(_sigmoid(tail) * mask, eb, False)
        sq, sk, sv = _silu(cq), _silu(ck), _silu(cv)
        qs, ks = [], []
        for h in range(hg):
            sl = slice(h * GDN_DK, (h + 1) * GDN_DK)
            qh, kh = sq[:, sl], sk[:, sl]
            qs.append(qh * lax.rsqrt(jnp.sum(qh * qh, axis=-1, keepdims=True) + EPS) * (GDN_DK ** -0.5))
            ks.append(kh * lax.rsqrt(jnp.sum(kh * kh, axis=-1, keepdims=True) + EPS))
        return jnp.concatenate(qs, axis=1), jnp.concatenate(ks, axis=1), sv, beta, lg
    return fn


def _gdn_intra_fn(pids, q, k, v, bB, lB, t_saved=None):
    rows = q.shape[0]
    nb = rows // CHUNK
    q3, k3, v3, b3, l3 = [t.reshape(nb, CHUNK, GDN_DK) for t in (q, k, v, bB, lB)]
    r, c = _iota2(CHUNK, CHUNK, 0), _iota2(CHUNK, CHUNK, 1)
    tril = (r >= c)
    strict = (r > c)
    gcol = _sel_dot(_bcast(tril.astype(F32), nb), l3)
    grow = jnp.swapaxes(gcol, 1, 2)[:, :CHUNK, :]
    diff = gcol[:, :, :CHUNK] - grow
    decay = jnp.where(tril[None], jnp.exp(jnp.where(tril[None], diff, 0.0)), 0.0)
    kb = k3 * b3
    m = jnp.where(strict[None], _dot(kb, k3, False, True) * decay, 0.0)
    t = _tri_inv_raw(m) if t_saved is None else _tri_inv_given(m, t_saved.reshape(nb, CHUNK, CHUNK))
    eg = jnp.exp(gcol)
    u = _dot(t, v3 * b3)
    w = _dot(t, kb * eg)
    attn = _dot(q3, k3, False, True) * decay
    qd = q3 * eg
    glast = jnp.sum(l3, axis=1, keepdims=True)
    kd = k3 * jnp.exp(glast - gcol)
    gl = jnp.exp(glast)
    outs = (u.reshape(rows, GDN_DK), w.reshape(rows, GDN_DK), attn.reshape(1, rows, CHUNK),
            qd.reshape(rows, GDN_DK), kd.reshape(rows, GDN_DK), gl.reshape(1, nb, 1, GDN_DK))
    return outs + (t.reshape(1, rows, CHUNK),) if t_saved is None else outs


def _make_rot_fn(cfg):
    hr = cfg.hr
    half = RET_DK // 2

    def fn(pids, rqk, cos, sin):
        rq, rk = rqk[:, :cfg.d], rqk[:, cfg.d:]

        def rot(t, scale):
            outs = []
            for h in range(hr):
                x1 = t[:, h * RET_DK:h * RET_DK + half]
                x2 = t[:, h * RET_DK + half:(h + 1) * RET_DK]
                outs += [(x1 * cos - x2 * sin) * scale, (x2 * cos + x1 * sin) * scale]
            return jnp.concatenate(outs, axis=1)
        return rot(rq, 1.0), rot(rk, RET_DK ** -0.5)
    return fn


def _make_mix_fn(cfg):
    hg, hr = cfg.hg, cfg.hr

    def fn(pids, oa, ob, pm, gnorm):
        d = cfg.d
        gz, rg, gate_a, gate_b = pm[:, :d], pm[:, d:2 * d], pm[:, 2 * d:3 * d], pm[:, 3 * d:]
        oas = []
        for h in range(hg):
            oh = oa[:, h * GDN_DK:(h + 1) * GDN_DK]
            oas.append(oh * lax.rsqrt(jnp.mean(oh * oh, axis=-1, keepdims=True) + EPS) * gnorm)
        ya = jnp.concatenate(oas, axis=1) * _silu(gz)
        obs = []
        for h in range(hr):
            oh = ob[:, h * RET_DK:(h + 1) * RET_DK]
            obs.append(oh * lax.rsqrt(jnp.mean(oh * oh, axis=-1, keepdims=True) + EPS))
        yb = _silu(rg) * jnp.concatenate(obs, axis=1)
        return (_sigmoid(gate_a) * ya + _sigmoid(gate_b) * yb,)
    return fn


def _act_fn(pids, u):
    f = u.shape[1] // 2
    return (_silu(u[:, :f]) * u[:, f:],)


def _gdn_step(s, u, w, a, qd, kd, gl):
    top = _dot(jnp.concatenate([w, qd], axis=0), s)
    v_new = u - top[:CHUNK]
    bot = _dot(jnp.concatenate([a, kd.T], axis=0), v_new)
    o = top[CHUNK:] + bot[:CHUNK]
    s2 = s * gl + bot[CHUNK:]
    return s2, o


def _ret_step(s, q, k, v, dm, qdc, kdc, g):
    att = _dot(q, k, False, True) * dm
    bot = _dot(jnp.concatenate([att, (k * kdc).T], axis=0), v)
    o = bot[:CHUNK] + _dot(q * qdc, s)
    s2 = s * g + bot[CHUNK:]
    return s2, o


class Part(NamedTuple):
    body: object
    args: list
    in_specs: list
    out_shape: list
    out_specs: list
    scratch: list
    aliases: dict = {}


def _run_parts(name, grid, parts):
    n_in = [len(p.args) for p in parts]
    n_out = [len(p.out_shape) for p in parts]
    n_sc = [len(p.scratch) for p in parts]
    off_in = [sum(n_in[:k]) for k in range(len(parts))]
    off_out = [sum(n_out[:k]) for k in range(len(parts))]
    off_sc = [sum(n_sc[:k]) for k in range(len(parts))]

    def body(*refs):
        ins, outs, scr = refs[:sum(n_in)], refs[sum(n_in):sum(n_in) + sum(n_out)], refs[sum(n_in) + sum(n_out):]
        for k, p in enumerate(parts):
            p.body(*ins[off_in[k]:off_in[k] + n_in[k]], *outs[off_out[k]:off_out[k] + n_out[k]],
                   *scr[off_sc[k]:off_sc[k] + n_sc[k]])

    aliases = {off_in[k] + i: off_out[k] + o for k, p in enumerate(parts) for i, o in p.aliases.items()}
    res = _pallas(
        body, out_shape=sum((p.out_shape for p in parts), []), grid=grid, in_specs=sum((p.in_specs for p in parts), []),
        out_specs=sum((p.out_specs for p in parts), []), scratch_shapes=sum((p.scratch for p in parts), []),
        input_output_aliases=aliases, name=name, compiler_params=_params(grid))(*sum((p.args for p in parts), []))
    return [res[off_out[k]:off_out[k] + n_out[k]] for k in range(len(parts))]


def _gdn_scan_fwd(cfg, u, w, attn, qd, kd, gl):
    d, hg, nch, sc = cfg.d, cfg.hg, cfg.nch, cfg.sc
    nst = nch // sc

    def body(u_ref, w_ref, a_ref, qd_ref, kd_ref, gl_ref, o_ref, ss_ref, s_ref):
        @pl.when(pl.program_id(0) == 0)
        def _():
            s_ref[...] = jnp.zeros(s_ref.shape, F32)

        states = [s_ref[h] for h in range(hg)]
        for j in range(sc):
            rows = slice(j * CHUNK, (j + 1) * CHUNK)
            outs = []
            for h in range(hg):
                sl = slice(h * GDN_DK, (h + 1) * GDN_DK)
                ss_ref[j, h] = states[h]
                states[h], o = _gdn_step(states[h], u_ref[rows, sl], w_ref[rows, sl], a_ref[h, rows, :],
                                         qd_ref[rows, sl], kd_ref[rows, sl], gl_ref[h, j])
                outs.append(o)
            o_ref[rows, :] = jnp.concatenate(outs, axis=1)
        for h in range(hg):
            s_ref[h] = states[h]

    row = pl.BlockSpec((sc * CHUNK, d), lambda n: (n, 0))
    return Part(
        body, [u, w, attn, qd, kd, gl],
        [row, row, pl.BlockSpec((hg, sc * CHUNK, CHUNK), lambda n: (0, n, 0)), row, row,
         pl.BlockSpec((hg, sc, 1, GDN_DK), lambda n: (0, n, 0, 0))],
        [jax.ShapeDtypeStruct((cfg.rp, d), F32), jax.ShapeDtypeStruct((nch, hg, GDN_DK, GDN_DK), F32)],
        [row, pl.BlockSpec((sc, hg, GDN_DK, GDN_DK), lambda n: (n, 0, 0, 0))],
        [pltpu.VMEM((hg, GDN_DK, GDN_DK), F32)])


def _gdn_scan_bwd(cfg, do, u, w, attn, qd, kd, gl, ss):
    d, hg, nch, sc = cfg.d, cfg.hg, cfg.nch, cfg.sc
    nst = nch // sc

    def body(do_ref, u_ref, w_ref, a_ref, qd_ref, kd_ref, gl_ref, ss_ref,
             du_ref, dw_ref, da_ref, dqd_ref, dkd_ref, dgl_ref, ds_ref):
        @pl.when(pl.program_id(0) == 0)
        def _():
            ds_ref[...] = jnp.zeros(ds_ref.shape, F32)

        dstates = [ds_ref[h] for h in range(hg)]
        for j in reversed(range(sc)):
            rows = slice(j * CHUNK, (j + 1) * CHUNK)
            dus, dws, dqds, dkds = [], [], [], []
            for h in range(hg):
                sl = slice(h * GDN_DK, (h + 1) * GDN_DK)
                args = (ss_ref[j, h], u_ref[rows, sl], w_ref[rows, sl], a_ref[h, rows, :], qd_ref[rows, sl],
                        kd_ref[rows, sl], gl_ref[h, j])
                _, vjp_fn = jax.vjp(_gdn_step, *args)
                dstates[h], du, dw, da, dqd, dkd, dgl = vjp_fn((dstates[h], do_ref[rows, sl]))
                da_ref[h, rows, :] = da
                dgl_ref[h, j] = dgl
                dus.append(du)
                dws.append(dw)
                dqds.append(dqd)
                dkds.append(dkd)
            du_ref[rows, :] = jnp.concatenate(dus, axis=1)
            dw_ref[rows, :] = jnp.concatenate(dws, axis=1)
            dqd_ref[rows, :] = jnp.concatenate(dqds, axis=1)
            dkd_ref[rows, :] = jnp.concatenate(dkds, axis=1)
        for h in range(hg):
            ds_ref[h] = dstates[h]

    row = pl.BlockSpec((sc * CHUNK, d), lambda n: (nst - 1 - n, 0))
    aspec = pl.BlockSpec((hg, sc * CHUNK, CHUNK), lambda n: (0, nst - 1 - n, 0))
    gspec = pl.BlockSpec((hg, sc, 1, GDN_DK), lambda n: (0, nst - 1 - n, 0, 0))
    rowshape = jax.ShapeDtypeStruct((cfg.rp, d), F32)
    return Part(
        body, [do, u, w, attn, qd, kd, gl, ss],
        [row, row, row, aspec, row, row, gspec, pl.BlockSpec((sc, hg, GDN_DK, GDN_DK), lambda n: (nst - 1 - n, 0, 0, 0))],
        [rowshape, rowshape, jax.ShapeDtypeStruct(attn.shape, F32), rowshape, rowshape, jax.ShapeDtypeStruct(gl.shape, F32)],
        [row, row, aspec, row, row, gspec],
        [pltpu.VMEM((hg, GDN_DK, GDN_DK), F32)])


def _ret_consts(cfg):
    hr = cfg.hr
    lg = np.log(1.0 - 2.0 ** (-5.0 - np.arange(hr, dtype=np.float64)))
    idx = np.arange(CHUNK, dtype=np.float64)
    tril = np.tril(np.ones((CHUNK, CHUNK), dtype=bool))
    dm = np.where(tril[None], np.exp((idx[:, None] - idx[None, :])[None] * lg[:, None, None]), 0.0)
    qdc = np.exp((idx[None, :] + 1.0) * lg[:, None])
    kdc = np.exp((CHUNK - 1.0 - idx[None, :]) * lg[:, None])
    gch = np.exp(CHUNK * lg)
    qdc = np.broadcast_to(qdc[:, :, None], (hr, CHUNK, RET_DK))
    kdc = np.broadcast_to(kdc[:, :, None], (hr, CHUNK, RET_DK))
    gch = np.broadcast_to(gch[:, None, None], (hr, 1, RET_DK))
    return tuple(jnp.asarray(np.ascontiguousarray(t), F32) for t in (dm, qdc, kdc, gch))


def _ret_scan_fwd(cfg, qr, kr, proj, consts):
    d, hr, nch, sc = cfg.d, cfg.hr, cfg.nch, cfg.sc
    nst = nch // sc
    dm, qdc, kdc, gch = consts

    def body(q_ref, k_ref, v_ref, dm_ref, qdc_ref, kdc_ref, g_ref, o_ref, ss_ref, s_ref):
        @pl.when(pl.program_id(0) == 0)
        def _():
            s_ref[...] = jnp.zeros(s_ref.shape, F32)

        states = [s_ref[h] for h in range(hr)]
        for j in range(sc):
            rows = slice(j * CHUNK, (j + 1) * CHUNK)
            outs = []
            for h in range(hr):
                sl = slice(h * RET_DK, (h + 1) * RET_DK)
                ss_ref[j, h] = states[h]
                states[h], o = _ret_step(states[h], q_ref[rows, sl], k_ref[rows, sl], v_ref[rows, sl], dm_ref[h],
                                         qdc_ref[h], kdc_ref[h], g_ref[h])
                outs.append(o)
            o_ref[rows, :] = jnp.concatenate(outs, axis=1)
        for h in range(hr):
            s_ref[h] = states[h]

    row = pl.BlockSpec((sc * CHUNK, d), lambda n: (n, 0))
    return Part(
        body, [qr, kr, proj, dm, qdc, kdc, gch],
        [row, row, pl.BlockSpec((sc * CHUNK, d), lambda n: (n, RV_BLOCK)), _full(dm), _full(qdc), _full(kdc), _full(gch)],
        [jax.ShapeDtypeStruct((cfg.rp, d), F32), jax.ShapeDtypeStruct((nch, hr, RET_DK, RET_DK), F32)],
        [row, pl.BlockSpec((sc, hr, RET_DK, RET_DK), lambda n: (n, 0, 0, 0))],
        [pltpu.VMEM((hr, RET_DK, RET_DK), F32)])


def _ret_scan_bwd(cfg, do, qr, kr, proj, consts, ss, dproj):
    d, hr, nch, sc = cfg.d, cfg.hr, cfg.nch, cfg.sc
    nst = nch // sc
    dm, qdc, kdc, gch = consts

    def body(do_ref, q_ref, k_ref, v_ref, dm_ref, qdc_ref, kdc_ref, g_ref, ss_ref, _, dq_ref, dk_ref, dv_ref, ds_ref):
        @pl.when(pl.program_id(0) == 0)
        def _():
            ds_ref[...] = jnp.zeros(ds_ref.shape, F32)

        dstates = [ds_ref[h] for h in range(hr)]
        for j in reversed(range(sc)):
            rows = slice(j * CHUNK, (j + 1) * CHUNK)
            dqs, dks, dvs = [], [], []
            for h in range(hr):
                sl = slice(h * RET_DK, (h + 1) * RET_DK)
                cs = (dm_ref[h], qdc_ref[h], kdc_ref[h], g_ref[h])
                _, vjp_fn = jax.vjp(lambda s, q, k, v, cs=cs: _ret_step(s, q, k, v, *cs),
                                    ss_ref[j, h], q_ref[rows, sl], k_ref[rows, sl], v_ref[rows, sl])
                dstates[h], dq, dk, dv = vjp_fn((dstates[h], do_ref[rows, sl]))
                dqs.append(dq)
                dks.append(dk)
                dvs.append(dv)
            dq_ref[rows, :] = jnp.concatenate(dqs, axis=1)
            dk_ref[rows, :] = jnp.concatenate(dks, axis=1)
            dv_ref[rows, :] = jnp.concatenate(dvs, axis=1).astype(dv_ref.dtype)
        for h in range(hr):
            ds_ref[h] = dstates[h]

    row = pl.BlockSpec((sc * CHUNK, d), lambda n: (nst - 1 - n, 0))
    rowshape = jax.ShapeDtypeStruct((cfg.rp, d), F32)
    vspec = pl.BlockSpec((sc * CHUNK, d), lambda n: (nst - 1 - n, RV_BLOCK))
    return Part(
        body, [do, qr, kr, proj, dm, qdc, kdc, gch, ss, dproj],
        [row, row, row, vspec, _full(dm), _full(qdc), _full(kdc), _full(gch),
         pl.BlockSpec((sc, hr, RET_DK, RET_DK), lambda n: (nst - 1 - n, 0, 0, 0)), ANY],
        [rowshape, rowshape, jax.ShapeDtypeStruct(dproj.shape, dproj.dtype)],
        [row, row, vspec],
        [pltpu.VMEM((hr, RET_DK, RET_DK), F32)], {9: 2})


def _final(cfg, h2, normf, tgt):
    d, tr = cfg.d, cfg.xrow
    nr = cfg.rp // tr

    def body(h_ref, g_ref, t_ref, dh_ref, dg_ref, loss_ref):
        i = pl.program_id(0)
        y, vjp_fn = jax.vjp(_rms, h_ref[...], g_ref[...])
        err = jnp.where(i >= 1, y - t_ref[...], 0.0)
        dh, dg = vjp_fn(err * (1.0 / d))
        dh_ref[...] = dh
        part = jnp.zeros((8, LANES), F32) + 0.5 * jnp.sum(err * err) * (1.0 / d)

        @pl.when(i == 0)
        def _():
            dg_ref[...] = dg
            loss_ref[...] = part

        @pl.when(i > 0)
        def _():
            dg_ref[...] += dg
            loss_ref[...] += part

    return _pallas(
        body,
        out_shape=[jax.ShapeDtypeStruct((cfg.rp, d), F32), jax.ShapeDtypeStruct((1, d), F32),
                   jax.ShapeDtypeStruct((8, LANES), F32)],
        grid=(nr,),
        in_specs=[_rows(tr, d), _full(normf), pl.BlockSpec((tr, d), lambda i: (jnp.maximum(i - 1, 0), 0))],
        out_specs=[_rows(tr, d), pl.BlockSpec((1, d), lambda i: (0, 0)), pl.BlockSpec((8, LANES), lambda i: (0, 0))],
        name="final_loss", compiler_params=_params((nr,)))(h2, normf, tgt)


ANY = pl.BlockSpec(memory_space=pl.ANY)


def _place():
    x, y, c = lax.axis_index("x"), lax.axis_index("y"), lax.axis_index("c")
    others = [(1 - x, y), (x, 1 - y), (1 - x, 1 - y)]
    return x, y, c, others


def _row_tile(rows, cap=256):
    return max(t for t in range(16, min(rows, cap) + 1, 16) if rows % t == 0)


class Comm(NamedTuple):
    ins: list
    outs: list
    sems: list
    start: object
    finish: object


def _run_comm(name, comm):
    n_in, n_out = len(comm.ins), len(comm.outs)

    def body(*refs):
        ins, outs, sems = refs[:n_in], refs[n_in:n_in + n_out], refs[n_in + n_out:]
        comm.start(ins, outs, sems)
        comm.finish(ins, outs, sems)

    return _pallas(body, out_shape=comm.outs, in_specs=[ANY] * n_in, out_specs=[ANY] * n_out,
                   scratch_shapes=comm.sems, name=name)(*comm.ins)


def _gather_comm(ws):
    n = len(ws)
    halves = [w.shape[0] // 2 for w in ws]

    def copies(w_refs, o_refs, sems):
        send_sems, recv_sems = sems
        x, y, c, others = _place()
        me = 2 * x + y
        chips = [2 * px + py for px, py in others]

        def piece(a, chip, core):
            return o_refs[a].at[chip, pl.ds(core * halves[a], halves[a]), :]

        def copy(a, k, src, chip, core, to):
            return pltpu.make_async_remote_copy(src_ref=src, dst_ref=piece(a, chip, core), send_sem=send_sems.at[6 * a + k],
                                                recv_sem=recv_sems.at[6 * a + k], device_id=to, device_id_type=MESH)

        def first(j, a):
            return copy(a, j, w_refs[a].at[pl.ds(c * halves[a], halves[a]), :], me, c, (*others[j], c))

        def landed(j, a):
            return copy(a, j, piece(a, chips[j], c), chips[j], c, (x, y, c))

        def passed(j, a):
            return copy(a, 3 + j, piece(a, chips[j], c), chips[j], c, (x, y, 1 - c))

        def from_sibling(j, a):
            return copy(a, 3 + j, piece(a, chips[j], 1 - c), chips[j], 1 - c, (x, y, c))

        return first, landed, passed, from_sibling

    pairs = [(j, a) for j in range(3) for a in range(n)]

    def start(w_refs, o_refs, sems):
        first, _, _, _ = copies(w_refs, o_refs, sems)
        for j, a in pairs:
            first(j, a).start()

    def finish(w_refs, o_refs, sems):
        first, landed, passed, from_sibling = copies(w_refs, o_refs, sems)
        for j, a in pairs:
            landed(j, a).wait_recv()
            passed(j, a).start()
        for j, a in pairs:
            from_sibling(j, a).wait_recv()
        for j, a in pairs:
            first(j, a).wait_send()
            passed(j, a).wait_send()

    return Comm(list(ws), [jax.ShapeDtypeStruct((N_CHIPS,) + w.shape, w.dtype) for w in ws],
                [pltpu.SemaphoreType.DMA((6 * n,)), pltpu.SemaphoreType.DMA((6 * n,))], start, finish)


def _pair_exchange(name, gs):
    n = len(gs)

    def body(*refs):
        g_refs, o_refs = refs[:n], refs[n:2 * n]
        send_sems, recv_sems = refs[2 * n:]
        x, y, c, _ = _place()
        cps = []
        for a in range(n):
            half = gs[a].shape[1] // 2
            cp = pltpu.make_async_remote_copy(
                src_ref=g_refs[a].at[:, pl.ds((1 - c) * half, half), :], dst_ref=o_refs[a], send_sem=send_sems.at[a],
                recv_sem=recv_sems.at[a], device_id=(x, y, 1 - c), device_id_type=MESH)
            cp.start()
            cps.append(cp)
        for cp in cps:
            cp.wait()

    return _pallas(
        body, out_shape=[jax.ShapeDtypeStruct((N_CHIPS, g.shape[1] // 2, g.shape[2]), g.dtype) for g in gs],
        in_specs=[ANY] * n, out_specs=[ANY] * n,
        scratch_shapes=[pltpu.SemaphoreType.DMA((n,)), pltpu.SemaphoreType.DMA((n,))], name=name)(*gs)


def _pair_sum(name, g, recv, cidx):
    half, cols = recv.shape[1], recv.shape[2]
    tr = _row_tile(half)
    nblk = half // tr

    def body(c_ref, g_ref, r_ref, o_ref):
        o_ref[...] = (g_ref[...] + r_ref[...]).astype(o_ref.dtype)

    grid_spec = pltpu.PrefetchScalarGridSpec(
        num_scalar_prefetch=1, grid=(N_CHIPS, nblk),
        in_specs=[pl.BlockSpec((1, tr, cols), lambda s, i, c: (s, c[0] * nblk + i, 0)),
                  pl.BlockSpec((1, tr, cols), lambda s, i, c: (s, i, 0))],
        out_specs=pl.BlockSpec((1, tr, cols), lambda s, i, c: (s, i, 0)))
    return _pallas(
        body, out_shape=jax.ShapeDtypeStruct((N_CHIPS, half, cols), BF16), grid_spec=grid_spec,
        name=name, compiler_params=_params((N_CHIPS, nblk)))(cidx, g, recv)


def _exchange_comm(parts):
    n = len(parts)

    def copies(p_refs, o_refs, sems):
        send_sems, recv_sems = sems
        x, y, c, others = _place()
        me = 2 * x + y

        def copy(a, j, src_chip, dst_chip):
            px, py = others[j]
            return pltpu.make_async_remote_copy(
                src_ref=p_refs[a].at[src_chip], dst_ref=o_refs[a].at[dst_chip], send_sem=send_sems.at[3 * a + j],
                recv_sem=recv_sems.at[3 * a + j], device_id=(px, py, c), device_id_type=MESH)

        def send(j, a):
            return copy(a, j, 2 * others[j][0] + others[j][1], me)

        def arrival(j, a):
            return copy(a, j, me, 2 * others[j][0] + others[j][1])

        return send, arrival

    pairs = [(j, a) for j in range(3) for a in range(n)]

    def start(p_refs, o_refs, sems):
        send, _ = copies(p_refs, o_refs, sems)
        for j, a in pairs:
            send(j, a).start()

    def finish(p_refs, o_refs, sems):
        send, arrival = copies(p_refs, o_refs, sems)
        for j, a in pairs:
            arrival(j, a).wait_recv()
        for j, a in pairs:
            send(j, a).wait_send()

    return Comm(list(parts), [jax.ShapeDtypeStruct(p.shape, p.dtype) for p in parts],
                [pltpu.SemaphoreType.DMA((3 * n,)), pltpu.SemaphoreType.DMA((3 * n,))], start, finish)


def _chip_sum(name, part, slots, chip):
    half, cols = slots.shape[1], slots.shape[2]
    tr = _row_tile(half)

    def body(me_ref, p_ref, *rest):
        s_refs, o_ref = rest[:N_CHIPS], rest[N_CHIPS]
        own = p_ref[...].astype(F32)
        v = [jnp.where(me_ref[0] == k, own, s_refs[k][...].astype(F32)) for k in range(N_CHIPS)]
        o_ref[...] = ((v[0] + v[1]) + v[2]) + v[3]

    def slot_spec(k):
        return pl.BlockSpec((None, tr, cols), lambda i, me: (jnp.where(me[0] == k, (k + 1) % N_CHIPS, k), i, 0))

    grid_spec = pltpu.PrefetchScalarGridSpec(
        num_scalar_prefetch=1, grid=(half // tr,),
        in_specs=[pl.BlockSpec((None, tr, cols), lambda i, me: (me[0], i, 0))] + [slot_spec(k) for k in range(N_CHIPS)],
        out_specs=pl.BlockSpec((tr, cols), lambda i, me: (i, 0)))
    return _pallas(
        body, out_shape=jax.ShapeDtypeStruct((half, cols), F32), grid_spec=grid_spec,
        name=name, compiler_params=_params((half // tr,)))(chip, part, *([slots] * N_CHIPS))


def _pair_swap(fins):
    n = len(fins)

    def body(*refs):
        f_refs, o_refs = refs[:n], refs[n:2 * n]
        send_sems, recv_sems = refs[2 * n:]
        x, y, c, _ = _place()
        cps = [pltpu.make_async_remote_copy(src_ref=f_refs[a], dst_ref=o_refs[a], send_sem=send_sems.at[a],
                                            recv_sem=recv_sems.at[a], device_id=(x, y, 1 - c), device_id_type=MESH)
               for a in range(n)]
        for cp in cps:
            cp.start()
        for cp in cps:
            cp.wait()

    return _pallas(
        body, out_shape=[jax.ShapeDtypeStruct(f.shape, f.dtype) for f in fins], in_specs=[ANY] * n, out_specs=[ANY] * n,
        scratch_shapes=[pltpu.SemaphoreType.DMA((n,)), pltpu.SemaphoreType.DMA((n,))], name="grad_pair_swap")(*fins)


def _adamw(name, w, g_own, g_other, m, v, cidx):
    R, cols = w.shape[-2:]
    lead = (None,) * (w.ndim - 2)
    zeros = (0,) * (w.ndim - 2)
    half = R // 2
    tr = _row_tile(half, 128)
    nblk = half // tr
    c1 = 1.0 - ADAM_B1 ** ADAM_STEP
    c2 = 1.0 - ADAM_B2 ** ADAM_STEP

    def body(c_ref, w_ref, go_ref, gs_ref, m_ref, v_ref, g_ref, d_ref, nm_ref, nv_ref):
        mine = (pl.program_id(0) // nblk) == c_ref[0]
        gv = jnp.where(mine, go_ref[...], gs_ref[...])
        nm = ADAM_B1 * m_ref[...] + (1.0 - ADAM_B1) * gv
        nv = ADAM_B2 * v_ref[...] + (1.0 - ADAM_B2) * (gv * gv)
        g_ref[...] = gv
        d_ref[...] = -ADAM_LR * ((nm / c1) / (jnp.sqrt(nv / c2) + ADAM_EPS) + ADAM_WD * w_ref[...])
        nm_ref[...] = nm
        nv_ref[...] = nv

    spec = pl.BlockSpec(lead + (tr, cols), lambda i, c: zeros + (i, 0))
    hspec = pl.BlockSpec((tr, cols), lambda i, c: (i % nblk, 0))
    shape = jax.ShapeDtypeStruct(w.shape, F32)
    grid_spec = pltpu.PrefetchScalarGridSpec(num_scalar_prefetch=1, grid=(R // tr,),
                                             in_specs=[spec, hspec, hspec, spec, spec], out_specs=[spec] * 4)
    return _pallas(
        body, out_shape=[shape] * 4, grid_spec=grid_spec,
        name=name, compiler_params=_params((R // tr,)))(cidx, w, g_own, g_other, m, v)


PARAMS = (("meta", 1), ("norm1", None), ("w_in", 2), ("gdn_conv_w", 2), ("gdn_a_log", None), ("gdn_dt_bias", None),
          ("gdn_norm", None), ("w_out", 1), ("norm2", None), ("w_ffn_up", 2), ("ffn_conv_w", 2), ("ffn_conv_b", None),
          ("w_ffn_down", 1), ("norm_f", None))
BIG = ("w_in", "w_out", "w_ffn_up", "w_ffn_down")
PACK_ALIGN = 1024
PACK_ROWS_ALIGN = 32


def _pack(arrs, dtype):
    parts, total = [], 0
    for a in arrs:
        f = a.reshape(-1).astype(dtype)
        pad = (-f.shape[0]) % PACK_ALIGN
        parts.append(jnp.pad(f, (0, pad)) if pad else f)
        total += f.shape[0] + pad
    rows = total // LANES
    rpad = (-rows) % PACK_ROWS_ALIGN
    if rpad:
        parts.append(jnp.zeros((rpad * LANES,), dtype))
    return jnp.concatenate(parts).reshape(rows + rpad, LANES)


def _unpack(buf, shapes):
    flat = buf.reshape(-1)
    outs, off = [], 0
    for s in shapes:
        n = int(np.prod(s))
        outs.append(flat[off:off + n].reshape(s))
        off += n + (-n) % PACK_ALIGN
    return outs


def _split4(a, axis):
    n = a.shape[axis] // N_CHIPS
    return [lax.slice_in_dim(a, s * n, (s + 1) * n, axis=axis) for s in range(N_CHIPS)]


PROJ_ORDER = (3, 7, 8, 9, 0, 1, 2, 6, 4, 5)


def _reorder_w_in(w, cfg):
    d, hg = cfg.d, cfg.hg

    def block(k):
        off = k * d + (2 * hg if k >= 4 else 0)
        return w[:, off:off + d]

    tail = jnp.pad(w[:, 4 * d:4 * d + 2 * hg], ((0, 0), (0, LANES - 2 * hg)))
    return jnp.concatenate([block(k) for k in PROJ_ORDER] + [tail], axis=1)


def _restore_w_in(wr, cfg):
    d, hg = cfg.d, cfg.hg
    at = {k: i for i, k in enumerate(PROJ_ORDER)}
    block = lambda k: wr[:, at[k] * d:(at[k] + 1) * d]
    return jnp.concatenate([block(k) for k in range(4)] + [wr[:, 10 * d:10 * d + 2 * hg]] +
                           [block(k) for k in range(4, 10)], axis=1)


def _step(cfg, x, tgt, shard, m_shard, v_shard):
    d, hg, dff, rp, tr, tm = cfg.d, cfg.hg, cfg.dff, cfg.rp, cfg.tr, cfg.tm
    nrow = rp // tr
    assert cfg.tf * N_CHIPS == 2 * dff and cfg.din % N_CHIPS == 0
    cidx = lax.axis_index("c").astype(jnp.int32).reshape(1)
    chip = (2 * lax.axis_index("x") + lax.axis_index("y")).astype(jnp.int32).reshape(1)

    axis = dict(PARAMS)
    small = ("meta", "gdn_conv_w", "ffn_conv_w")
    small_shapes = [shard[n].shape for n in small]
    mine = [shard[n][0].astype(BF16) for n in BIG] + [_pack([shard[n] for n in small], F32)]

    def with_own(gathered, own):
        return [lax.dynamic_update_slice(g, w[None], (chip[0], 0, 0)) for g, w in zip(gathered, own)]

    g_in, g_small = _run_comm("weights_gather_first", _gather_comm([mine[0], mine[4]]))
    g_small, = with_own([g_small], [mine[4]])
    w_in_r = _reorder_w_in(jnp.concatenate([jnp.where(chip[0] == s, mine[0], g_in[s]) for s in range(N_CHIPS)], axis=1),
                           cfg)
    per_chip = [_unpack(g_small[s], small_shapes) for s in range(N_CHIPS)]
    full = {n: jnp.concatenate([per_chip[s][k] for s in range(N_CHIPS)], axis=axis[n]) for k, n in enumerate(small)}
    meta = full["meta"]
    gconv_w = full["gdn_conv_w"][0]
    fconv_w = full["ffn_conv_w"][0]
    norm1, norm2, gnorm = shard["norm1"], shard["norm2"], shard["gdn_norm"]
    normf = shard["norm_f"].reshape(1, d)
    fconv_b = shard["ffn_conv_b"]
    alog = jnp.pad(shard["gdn_a_log"], ((0, 7), (0, LANES - hg)))
    dtb = jnp.pad(shard["gdn_dt_bias"], ((0, 7), (0, LANES - hg)))

    h0 = jnp.concatenate([jnp.zeros((cfg.front, d), F32), meta, x], axis=0)
    half = RET_DK // 2
    pos = np.arange(rp, dtype=np.float32) - np.float32(cfg.front)
    inv = (np.float32(1.0) / np.float32(ROPE_BASE) ** (np.arange(half, dtype=np.float32) / np.float32(half))).astype(np.float32)
    ang = pos[:, None] * inv[None, :]
    cos, sin = jnp.asarray(np.cos(ang), F32), jnp.asarray(np.sin(ang), F32)
    rconsts = _ret_consts(cfg)

    tr_n = 3 * tr if rp % (3 * tr) == 0 else tr
    rms_f = _make_rms_fn(cfg, tr_n, False)
    rms_b = _make_rms_fn(cfg, tr_n, True)
    rowshape = jax.ShapeDtypeStruct((rp, d), F32)
    rspec = _rows(tr, d)
    nspec = _rows(tr_n, d)

    def rms_fwd(name, h, g):
        return _stage_fwd(name, rms_f, (rp // tr_n,), [In(h, nspec), In(g, _full(g))],
                          [jax.ShapeDtypeStruct((rp, d), BF16)], [nspec])[0]

    wm = 10 * d
    w_main, w_tail = w_in_r[:, :wm], w_in_r[:, wm:]
    tn_in = 2560 if wm % 2560 == 0 else LANES
    hn1 = rms_fwd("rms1_fwd", h0, norm1)
    proj, rest = _mm("proj_fwd", hn1, w_main, tm=tm, tn=tn_in, tk=d, out_dtype=BF16, comm=_gather_comm(mine[1:4]))
    ptail = _mm("proj_tail_fwd", hn1, w_tail, tm=tm, tn=LANES, tk=d)
    g_out, g_up, g_down = with_own(rest, mine[1:4])
    w_out = g_out.reshape(d, d)
    w_up = g_up
    w_up_t = jnp.swapaxes(g_up, 1, 2).reshape(2 * dff, d)
    w_down = g_down.reshape(dff, d)
    cqkv = _conv_fwd("gdn_conv_fwd", proj, CONV_COL * d, gconv_w, None, taps=GDN_CONV, width=3 * d, tr=tr, tc=d)
    prep_fn = _make_gdn_prep_fn(cfg, tr)
    prep_ins = [In(cqkv, _rows(tr, 3 * d), BF16), In(ptail, _rows(tr, LANES), BF16),
                In(alog, _full(alog), F32, True), In(dtb, _full(dtb), F32, True)]
    qn, kn, vv, bB, lB = _stage_fwd("gdn_prep_fwd", prep_fn, (nrow,), prep_ins, [rowshape] * 5, [rspec] * 5)

    trg = cfg.nb * CHUNK
    gi_grid = (rp // trg, hg)
    hspec = pl.BlockSpec((trg, GDN_DK), lambda i, h: (i, h))
    aspec = pl.BlockSpec((1, trg, CHUNK), lambda i, h: (h, i, 0))
    gspec = pl.BlockSpec((1, cfg.nb, 1, GDN_DK), lambda i, h: (h, i, 0, 0))
    intra_ins = [In(t, hspec, F32) for t in (qn, kn, vv, bB, lB)]
    ashape = jax.ShapeDtypeStruct((hg, rp, CHUNK), F32)
    intra_shapes = [rowshape, rowshape, ashape, rowshape, rowshape, jax.ShapeDtypeStruct((hg, cfg.nch, 1, GDN_DK), F32), ashape]
    intra_specs = [hspec, hspec, aspec, hspec, hspec, gspec, aspec]
    gu, gw, gattn, gqd, gkd, ggl, gtinv = _stage_fwd("gdn_intra_fwd", _gdn_intra_fn, gi_grid, intra_ins, intra_shapes,
                                                     intra_specs)
    rot_fn = _make_rot_fn(cfg)

    def rot_ins(dproj=None):
        return [In(proj, _rows(tr_n, 2 * d, ROT_COL // 2), BF16, galias=dproj, gshape=(rp, wm)),
                In(cos, _rows(tr_n, half)), In(sin, _rows(tr_n, half))]

    qr, kr = _stage_fwd("rot_fwd", rot_fn, (rp // tr_n,), rot_ins(), [rowshape] * 2, [nspec] * 2)
    nst = cfg.nch // cfg.sc
    oa, gss = _run_parts("gdn_scan_fwd", (nst,), [_gdn_scan_fwd(cfg, gu, gw, gattn, gqd, gkd, ggl)])[0]
    ob, rss = _run_parts("ret_scan_fwd", (nst,), [_ret_scan_fwd(cfg, qr, kr, proj, rconsts)])[0]

    mix_fn = _make_mix_fn(cfg)
    mix_ins = [In(oa, rspec, F32), In(ob, rspec, F32), In(proj, _rows(tr, 4 * d, MIX_COL // 4), BF16, gshape=(rp, wm)),
               In(gnorm, _full(gnorm), F32, True)]
    ymix = _stage_fwd("mix_fwd", mix_fn, (nrow,), mix_ins, [jax.ShapeDtypeStruct((rp, d), BF16)], [rspec])[0]
    h1 = _mm("out_proj_fwd", ymix, w_out, tm=tm, tn=d, tk=d, add=h0)

    hn2 = rms_fwd("rms2_fwd", h1, norm2)
    up = _mm("ffn_up_fwd", hn2, w_up, tm=tm, tn=cfg.tf, tk=d, out_dtype=BF16)
    uc = _conv_fwd("ffn_conv_fwd", up, 0, fconv_w, fconv_b, taps=FFN_CONV, width=2 * dff, tr=tr, tc=cfg.tf)
    tra = tr
    act_ins = [In(uc, _rows(tra, 2 * dff), BF16)]
    act_spec = _rows(tra, dff)
    act = _stage_fwd("ffn_act_fwd", _act_fn, (rp // tra,), act_ins, [jax.ShapeDtypeStruct((rp, dff), BF16)], [act_spec])[0]
    h2 = _mm("ffn_down_fwd", act, w_down, tm=tm, tn=d, tk=cfg.tf, add=h1)

    dh2, g_normf, loss_blk = _final(cfg, h2, normf, tgt)
    loss = lax.psum(loss_blk[0, 0], ("x", "y", "c"))

    g_w_down = _mm_tn("ffn_down_dw", act, dh2, tr=tm, tka=cfg.tf, tn=d)
    dact = _mm("ffn_down_dx", dh2, w_down.T, tm=tm, tn=cfg.tf, tk=d)
    duc, = _stage_bwd("ffn_act_bwd", _act_fn, (rp // tra,), act_ins, [(dact, act_spec)])
    dup, g_fconv_w, g_fconv_b = _conv_bwd("ffn_conv_bwd", up, 0, fconv_w, duc, taps=FFN_CONV, width=2 * dff,
                                          tr=tr, tc=cfg.tf, with_bias=True)
    g_w_up = _mm_tn("ffn_up_dw", hn2, dup, tr=tm, tka=d, tn=cfg.tf, blocked=True)

    def pair_reduce(tag, names, arrs):
        recvs = _pair_exchange("grad_pair_exchange_" + tag, arrs)
        return [_pair_sum("grad_pair_sum_" + n, g, r, cidx) for n, g, r in zip(names, arrs, recvs)]

    parts_ffn = pair_reduce("ffn", ["w_ffn_down", "w_ffn_up"], [g_w_down.reshape(N_CHIPS, dff // N_CHIPS, d), g_w_up])
    dhn2, slots_ffn = _mm("ffn_up_dx", dup, w_up_t, tm=tm, tn=d, tk=2 * cfg.tf, comm=_exchange_comm(parts_ffn))

    def rms_bwd(name, h, g, dhn, dres):
        ins = [In(h, nspec, F32), In(g, _full(g), F32, True)]
        return _stage_bwd(name, rms_b, (rp // tr_n,), ins, [(dhn, nspec), (dres, nspec)])

    dh1, g_norm2 = rms_bwd("rms2_bwd", h1, norm2, dhn2, dh2)
    g_w_out = _mm_tn("out_proj_dw", ymix, dh1, tr=tm, tka=d, tn=d)
    dymix = _mm("out_proj_dx", dh1, w_out.T, tm=tm, tn=d, tk=d)
    doa, dob, dproj, g_gnorm = _stage_bwd("mix_bwd", mix_fn, (nrow,), mix_ins, [(dymix, rspec)])

    dqr, dkr, dproj = _run_parts("ret_scan_bwd", (nst,), [_ret_scan_bwd(cfg, dob, qr, kr, proj, rconsts, rss, dproj)])[0]
    dproj, = _stage_bwd("rot_bwd", rot_fn, (rp // tr_n,), rot_ins(dproj), [(dqr, nspec), (dkr, nspec)])
    dgu, dgw, dgattn, dgqd, dgkd, dggl = _run_parts(
        "gdn_scan_bwd", (nst,), [_gdn_scan_bwd(cfg, doa, gu, gw, gattn, gqd, gkd, ggl, gss)])[0]

    intra_cots = [(dgu, hspec), (dgw, hspec), (dgattn, aspec), (dgqd, hspec), (dgkd, hspec), (dggl, gspec)]
    dqn, dkn, dvv, dbB, dlB = _stage_bwd("gdn_intra_bwd", _gdn_intra_fn, gi_grid, intra_ins + [In(gtinv, aspec)], intra_cots)
    dcqkv, dtail, g_alog, g_dtb = _stage_bwd(
        "gdn_prep_bwd", prep_fn, (nrow,), prep_ins, [(t, rspec) for t in (dqn, dkn, dvv, dbB, dlB)])
    dproj, g_gconv_w = _conv_bwd("gdn_conv_bwd", proj, CONV_COL * d, gconv_w, dcqkv, taps=GDN_CONV, width=3 * d,
                                 tr=tr, tc=d, with_bias=False, dx_into=dproj)
    g_w_in_r = jnp.concatenate([_mm_tn("proj_dw", hn1, dproj, tr=tm, tka=d, tn=tn_in),
                                _mm_tn("proj_tail_dw", hn1, dtail, tr=tm, tka=d, tn=LANES)], axis=1)
    g_in4 = jnp.stack(_split4(_restore_w_in(g_w_in_r, cfg), 1))
    parts_mix = pair_reduce("mix", ["w_out", "w_in"], [g_w_out.reshape(N_CHIPS, d // N_CHIPS, d), g_in4])
    dhn1_tail = _mm("proj_tail_dx", dtail, w_tail.T, tm=tm, tn=d, tk=LANES)
    dhn1, slots_mix = _mm("proj_dx", dproj, w_main.T, tm=tm, tn=d, tk=tn_in // 2 if tn_in > LANES else LANES, add=dhn1_tail,
                          comm=_exchange_comm(parts_mix))
    dh0, g_norm1 = rms_bwd("rms1_bwd", h0, norm1, dhn1, dh1)

    grad_x = dh0[cfg.xrow:]
    small_grads = {
        "meta": dh0[cfg.front:cfg.xrow], "norm1": g_norm1, "gdn_conv_w": g_gconv_w[None],
        "gdn_a_log": g_alog[0:1, :hg], "gdn_dt_bias": g_dtb[0:1, :hg], "gdn_norm": g_gnorm, "norm2": g_norm2,
        "ffn_conv_w": g_fconv_w[None], "ffn_conv_b": g_fconv_b, "norm_f": g_normf.reshape(d),
    }

    small_names = [n for n, _ in PARAMS if n not in BIG]
    g_small = jnp.stack([_pack([small_grads[n] if axis[n] is None else _split4(small_grads[n], axis[n])[s]
                                for n in small_names], F32) for s in range(N_CHIPS)])
    parts_small = pair_reduce("small", ["small"], [g_small])
    slots_small = _run_comm("grad_exchange_small", _exchange_comm(parts_small))
    tags = ["w_in", "w_out", "w_ffn_up", "w_ffn_down", "small"]
    parts = [parts_mix[1], parts_mix[0], parts_ffn[1], parts_ffn[0], parts_small[0]]
    slots = [slots_mix[1], slots_mix[0], slots_ffn[1], slots_ffn[0], slots_small[0]]
    fins = [_chip_sum("grad_chip_sum_" + t, p, s, chip) for t, p, s in zip(tags, parts, slots)]
    sibs = _pair_swap(fins)

    def flat2(a):
        return a.reshape(-1, a.shape[-1])

    outs = {}
    for k, t in enumerate(BIG):
        res = _adamw("adamw_" + t, flat2(shard[t]), fins[k], sibs[k], flat2(m_shard[t]), flat2(v_shard[t]), cidx)
        outs[t] = [r.reshape(shard[t].shape) for r in res]
    small_shapes_all = [shard[n].shape for n in small_names]
    pk = lambda src: _pack([src[n] for n in small_names], F32)
    res = _adamw("adamw_small", pk(shard), fins[4], sibs[4], pk(m_shard), pk(v_shard), cidx)
    for k, r in enumerate(res):
        for n, a in zip(small_names, _unpack(r, small_shapes_all)):
            outs.setdefault(n, [None] * 4)[k] = a
    names = [n for n, _ in PARAMS]
    return (loss, grad_x[None], *[outs[n][k] for k in range(4) for n in names])


def kernel(x, meta, norm1, w_in, gdn_conv_w, gdn_a_log, gdn_dt_bias, gdn_norm, w_out, norm2, w_ffn_up, ffn_conv_w, ffn_conv_b, w_ffn_down, norm_f, loss_target, m_meta, m_norm1, m_w_in, m_gdn_conv_w, m_gdn_a_log, m_gdn_dt_bias, m_gdn_norm, m_w_out, m_norm2, m_w_ffn_up, m_ffn_conv_w, m_ffn_conv_b, m_w_ffn_down, m_norm_f, v_meta, v_norm1, v_w_in, v_gdn_conv_w, v_gdn_a_log, v_gdn_dt_bias, v_gdn_norm, v_w_out, v_norm2, v_w_ffn_up, v_ffn_conv_w, v_ffn_conv_b, v_w_ffn_down, v_norm_f):
    names = [n for n, _ in PARAMS]
    shard = dict(zip(names, (meta, norm1, w_in, gdn_conv_w, gdn_a_log, gdn_dt_bias, gdn_norm, w_out, norm2, w_ffn_up,
                             ffn_conv_w, ffn_conv_b, w_ffn_down, norm_f)))
    m_shard = dict(zip(names, (m_meta, m_norm1, m_w_in, m_gdn_conv_w, m_gdn_a_log, m_gdn_dt_bias, m_gdn_norm, m_w_out,
                               m_norm2, m_w_ffn_up, m_ffn_conv_w, m_ffn_conv_b, m_w_ffn_down, m_norm_f)))
    v_shard = dict(zip(names, (v_meta, v_norm1, v_w_in, v_gdn_conv_w, v_gdn_a_log, v_gdn_dt_bias, v_gdn_norm, v_w_out,
                               v_norm2, v_w_ffn_up, v_ffn_conv_w, v_ffn_conv_b, v_w_ffn_down, v_norm_f)))
    return _step(REAL, x[0], loss_target[0], shard, m_shard, v_shard)
```

```python
import functools
from typing import NamedTuple

import numpy as np
import jax
import jax.numpy as jnp
from jax import lax
from jax.experimental import pallas as pl
from jax.experimental.pallas import tpu as pltpu

F32 = jnp.float32
BF16 = jnp.bfloat16
EPS = 1e-6
CHUNK = 64
GDN_DK = 128
RET_DK = 256
GDN_CONV = 4
FFN_CONV = 3
ROPE_BASE = 10000.0
LANES = 128
N_CHIPS = 4
ADAM_LR, ADAM_B1, ADAM_B2, ADAM_EPS, ADAM_WD, ADAM_STEP = 0.001, 0.9, 0.999, 1e-08, 0.01, 10
MIX_COL, CONV_COL, RV_BLOCK, ROT_COL = 0, 4, 7, 8
MESH = pl.DeviceIdType.MESH
VMEM_LIMIT = 56 * 1024 * 1024


class Cfg(NamedTuple):
    d: int
    seq: int
    n_meta: int
    dff: int
    tr: int
    nb: int
    tm: int
    tf: int
    sc: int

    @property
    def hg(self): return self.d // GDN_DK
    @property
    def hr(self): return self.d // RET_DK
    @property
    def L(self): return self.n_meta + self.seq
    @property
    def rp(self): return -(-self.L // 256) * 256
    @property
    def front(self): return self.rp - self.L
    @property
    def xrow(self): return self.rp - self.seq
    @property
    def nch(self): return self.rp // CHUNK
    @property
    def din(self): return 10 * self.d + 2 * self.hg


REAL = Cfg(d=1024, seq=8192, n_meta=16, dff=2816, tr=256, nb=12, tm=1408, tf=1408, sc=4)


def _pallas(body, **kw):
    return pl.pallas_call(body, **kw)


def _sigmoid_raw(x):
    return 1.0 / (1.0 + jnp.exp(-x))


@jax.custom_vjp
def _sigmoid(x):
    return _sigmoid_raw(x)


def _sigmoid_fwd(x):
    s = _sigmoid_raw(x)
    return s, s


def _sigmoid_bwd(s, g):
    return (g * (s * (1.0 - s)),)


_sigmoid.defvjp(_sigmoid_fwd, _sigmoid_bwd)


@jax.custom_vjp
def _silu(x):
    return x * _sigmoid_raw(x)


def _silu_fwd(x):
    s = _sigmoid_raw(x)
    return x * s, (x, s)


def _silu_bwd(res, g):
    x, s = res
    return (g * (s * (1.0 + x * (1.0 - s))),)


_silu.defvjp(_silu_fwd, _silu_bwd)


def _softplus(x):
    return jnp.maximum(x, 0.0) + jnp.log(1.0 + jnp.exp(-jnp.abs(x)))


def _raw_dot(a, b, ta, tb, hi):
    if not hi:
        a = a.astype(BF16)
        b = b.astype(BF16)
    nbatch = a.ndim - 2
    ca = a.ndim - 2 if ta else a.ndim - 1
    cb = b.ndim - 1 if tb else b.ndim - 2
    batch = tuple(range(nbatch))
    return lax.dot_general(a, b, (((ca,), (cb,)), (batch, batch)),
                           precision=lax.Precision.HIGHEST if hi else None,
                           preferred_element_type=F32)


@functools.partial(jax.custom_vjp, nondiff_argnums=(2, 3, 4))
def _dot_p(a, b, ta, tb, hi):
    return _raw_dot(a, b, ta, tb, hi)


def _dot(a, b, ta=False, tb=False, hi=False):
    return _dot_p(a, b, ta, tb, hi)


def _dot_fwd(a, b, ta, tb, hi):
    return _raw_dot(a, b, ta, tb, hi), (a, b)


def _dot_bwd(ta, tb, hi, res, g):
    a, b = res
    if not ta and not tb:
        da, db = _dot(g, b, False, True, hi), _dot(a, g, True, False, hi)
    elif not ta and tb:
        da, db = _dot(g, b, False, False, hi), _dot(g, a, True, False, hi)
    elif ta and not tb:
        da, db = _dot(b, g, False, True, hi), _dot(a, g, False, False, hi)
    else:
        raise NotImplementedError
    return da.astype(a.dtype), db.astype(b.dtype)


_dot_p.defvjp(_dot_fwd, _dot_bwd)


def _iota2(n, m, axis):
    return lax.broadcasted_iota(jnp.int32, (n, m), axis)


def _bcast(mat, nb):
    return jnp.broadcast_to(mat[None], (nb,) + mat.shape)


def _split3(a):
    a0 = a.astype(BF16)
    r1 = a - a0.astype(F32)
    a1 = r1.astype(BF16)
    return a0, a1, (r1 - a1.astype(F32)).astype(BF16)


@functools.partial(jax.custom_vjp, nondiff_argnums=(2,))
def _dot_sel(a, e, te):
    eb = e.astype(BF16)
    p0, p1, p2 = (_raw_dot(p, eb, False, te, False) for p in _split3(a))
    return p0 + (p1 + p2)


def _dot_sel_fwd(a, e, te):
    return _dot_sel(a, e, te), e


def _dot_sel_bwd(te, e, g):
    return _dot_sel(g, e, not te), jnp.zeros_like(e)


_dot_sel.defvjp(_dot_sel_fwd, _dot_sel_bwd)


@jax.custom_vjp
def _sel_dot(e, x):
    eb = e.astype(BF16)
    p0, p1, p2 = (_raw_dot(eb, p, False, False, False) for p in _split3(x))
    return p0 + (p1 + p2)


def _sel_dot_fwd(e, x):
    return _sel_dot(e, x), e


def _sel_dot_bwd(e, g):
    eb = e.astype(BF16)
    p0, p1, p2 = (_raw_dot(eb, p, True, False, False) for p in _split3(g))
    return jnp.zeros_like(e), p0 + (p1 + p2)


_sel_dot.defvjp(_sel_dot_fwd, _sel_dot_bwd)


def _tri_inv_raw(m):
    nb = m.shape[0]
    r, c = _iota2(CHUNK, CHUNK, 0), _iota2(CHUNK, CHUNK, 1)
    t = _bcast((r == c).astype(F32), nb)
    b = 1
    while b < CHUNK:
        sh = b.bit_length() - 1
        off = ((r >> (sh + 1)) == (c >> (sh + 1))) & ((r >> sh) != (c >> sh)) & (r > c)
        cl = jnp.where(off[None], m, 0.0)
        t = t - _raw_dot(_raw_dot(t, cl, False, False, False), t, False, False, False)
        b *= 2
    return t


@jax.custom_vjp
def _tri_inv_given(m, t):
    return t


def _tri_inv_fwd(m, t):
    return t, t


def _tri_inv_bwd(t, g):
    return -_raw_dot(_raw_dot(t, g, True, False, False), t, False, True, False), jnp.zeros_like(t)


_tri_inv_given.defvjp(_tri_inv_fwd, _tri_inv_bwd)


def _rms(h, g):
    return h * lax.rsqrt(jnp.mean(h * h, axis=-1, keepdims=True) + EPS) * g


class In(NamedTuple):
    arr: jax.Array
    spec: pl.BlockSpec
    grad: object = None
    acc: bool = False
    gshape: object = None
    gspec: object = None
    galias: object = None


def _params(grid):
    sem = ("arbitrary",) * len(grid)
    return pltpu.CompilerParams(dimension_semantics=sem, vmem_limit_bytes=VMEM_LIMIT)


def _stage_fwd(name, fn, grid, ins, out_shapes, out_specs):
    n_in = len(ins)

    def body(*refs):
        pids = tuple(pl.program_id(k) for k in range(len(grid)))
        vals = [r[...].astype(F32) for r in refs[:n_in]]
        outs = fn(pids, *vals)
        for o_ref, o in zip(refs[n_in:], outs):
            o_ref[...] = o.reshape(o_ref.shape).astype(o_ref.dtype)

    return _pallas(
        body, out_shape=out_shapes, grid=grid, in_specs=[i.spec for i in ins],
        out_specs=out_specs, name=name, compiler_params=_params(grid))(*[i.arr for i in ins])


def _stage_bwd(name, fn, grid, ins, cots):
    n_in, n_ct = len(ins), len(cots)
    didx = [k for k, i in enumerate(ins) if i.grad is not None]
    aliased = [(o, ins[k].galias) for o, k in enumerate(didx) if ins[k].galias is not None]
    n_al = len(aliased)

    def body(*refs):
        pids = tuple(pl.program_id(k) for k in range(len(grid)))
        vals = [r[...].astype(F32) for r in refs[:n_in]]
        ct_refs = refs[n_in:n_in + n_ct]
        g_refs = refs[n_in + n_ct + n_al:]

        def f(*dv):
            merged = list(vals)
            for k, v in zip(didx, dv):
                merged[k] = v
            return tuple(fn(pids, *merged))

        outs, vjp_fn = jax.vjp(f, *[vals[k].astype(F32) for k in didx])
        cts = tuple(c[...].reshape(o.shape).astype(F32) for c, o in zip(ct_refs, outs))
        grads = vjp_fn(cts)
        first = functools.reduce(jnp.logical_and, [p == 0 for p in pids])
        for k, g_ref, g in zip(didx, g_refs, grads):
            if ins[k].acc:
                @pl.when(first)
                def _(g_ref=g_ref):
                    g_ref[...] = jnp.zeros(g_ref.shape, g_ref.dtype)
                g_ref[...] += g.reshape(g_ref.shape).astype(g_ref.dtype)
            else:
                g_ref[...] = g.reshape(g_ref.shape).astype(g_ref.dtype)

    out_shapes = [jax.ShapeDtypeStruct(ins[k].gshape or ins[k].arr.shape, ins[k].grad) for k in didx]
    out_specs = [ins[k].gspec or ins[k].spec for k in didx]
    return _pallas(
        body, out_shape=out_shapes, grid=grid,
        in_specs=[i.spec for i in ins] + [c[1] for c in cots] + [ANY] * n_al, out_specs=out_specs,
        input_output_aliases={n_in + n_ct + a: o for a, (o, _) in enumerate(aliased)},
        name=name, compiler_params=_params(grid))(*[i.arr for i in ins], *[c[0] for c in cots], *[a for _, a in aliased])


def _full(arr):
    nd = arr.ndim
    return pl.BlockSpec(arr.shape, lambda *p: (0,) * nd)


def _rows(tr, width, blk=0):
    return pl.BlockSpec((tr, width), lambda i: (i, blk))


def _mm(name, a, b, *, tm, tn, tk, out_dtype=F32, add=None, comm=None):
    M, K = a.shape
    N = b.shape[1] if b.ndim == 2 else b.shape[0] * b.shape[2]
    nk = K // tk
    grid = (M // tm, N // tn, nk)
    n_in = 3 if add is not None else 2
    n_ci, n_co = (len(comm.ins), len(comm.outs)) if comm is not None else (0, 0)

    def body(*refs):
        a_ref, b_ref = refs[0], refs[1]
        add_ref = refs[2] if add is not None else None
        c_ins = refs[n_in:n_in + n_ci]
        o_ref = refs[n_in + n_ci]
        c_outs = refs[n_in + n_ci + 1:n_in + n_ci + 1 + n_co]
        scratch = refs[n_in + n_ci + 1 + n_co:]
        acc_ref = scratch[0] if nk > 1 else None
        sems = scratch[1 if nk > 1 else 0:]
        step = (pl.program_id(0) * grid[1] + pl.program_id(1)) * nk + pl.program_id(2)
        if comm is not None:
            @pl.when(step == 0)
            def _():
                comm.start(c_ins, c_outs, sems)

        part = _raw_dot(a_ref[...], b_ref[...], False, False, False)

        def finish(total):
            if add_ref is not None:
                total = total + add_ref[...]
            o_ref[...] = total.astype(o_ref.dtype)

        if nk == 1:
            finish(part)
        else:
            k = pl.program_id(2)

            @pl.when(k == 0)
            def _():
                acc_ref[...] = part

            @pl.when(k > 0)
            def _():
                acc_ref[...] += part

            @pl.when(k == nk - 1)
            def _():
                finish(acc_ref[...])

        if comm is not None:
            @pl.when(step == grid[0] * grid[1] * nk - 1)
            def _():
                comm.finish(c_ins, c_outs, sems)

    b_spec = (pl.BlockSpec((tk, tn), lambda i, j, k: (k, j)) if b.ndim == 2 else
              pl.BlockSpec((None, tk, tn), lambda i, j, k: (j, k, 0)))
    in_specs = [pl.BlockSpec((tm, tk), lambda i, j, k: (i, k)), b_spec]
    args = [a, b]
    if add is not None:
        in_specs.append(pl.BlockSpec((tm, tn), lambda i, j, k: (i, j)))
        args.append(add)
    out_shape = jax.ShapeDtypeStruct((M, N), out_dtype)
    out_spec = pl.BlockSpec((tm, tn), lambda i, j, k: (i, j))
    scratch = [pltpu.VMEM((tm, tn), F32)] if nk > 1 else []
    if comm is None:
        return _pallas(body, out_shape=out_shape, grid=grid, in_specs=in_specs, out_specs=out_spec,
                       scratch_shapes=scratch, name=name, compiler_params=_params(grid))(*args)
    res = _pallas(body, out_shape=[out_shape] + comm.outs, grid=grid, in_specs=in_specs + [ANY] * n_ci,
                  out_specs=[out_spec] + [ANY] * n_co, scratch_shapes=scratch + comm.sems, name=name,
                  compiler_params=_params(grid))(*args, *comm.ins)
    return res[0], res[1:]


def _mm_tn(name, a, b, *, tr, tka, tn, blocked=False):
    R, Ka = a.shape
    N = b.shape[1]
    nr = R // tr
    grid = (Ka // tka, N // tn, nr)
    if blocked:
        out_shape = jax.ShapeDtypeStruct((N // tn, Ka, tn), F32)
        out_spec = pl.BlockSpec((None, tka, tn), lambda i, j, r: (j, i, 0))
    else:
        out_shape = jax.ShapeDtypeStruct((Ka, N), F32)
        out_spec = pl.BlockSpec((tka, tn), lambda i, j, r: (i, j))

    def body(a_ref, b_ref, o_ref):
        r = pl.program_id(2)
        part = _raw_dot(a_ref[...], b_ref[...], True, False, False)

        @pl.when(r == 0)
        def _():
            o_ref[...] = part

        @pl.when(r > 0)
        def _():
            o_ref[...] += part

    return _pallas(
        body, out_shape=out_shape, grid=grid,
        in_specs=[pl.BlockSpec((tr, tka), lambda i, j, r: (r, i)),
                  pl.BlockSpec((tr, tn), lambda i, j, r: (r, j))],
        out_specs=out_spec, name=name, compiler_params=_params(grid))(a, b)


def _conv_fwd(name, x, xcol0, w, b, *, taps, width, tr, tc):
    R = x.shape[0]
    grid = (width // tc, R // tr)
    cb0 = xcol0 // tc
    hrows = 16 if x.dtype == BF16 else 8
    hb = tr // hrows

    def body(*refs):
        x_ref, xp_ref, w_ref = refs[:3]
        b_ref = refs[3] if b is not None else None
        o_ref = refs[-1]
        i = pl.program_id(1)
        xv = x_ref[...].astype(F32)
        prev = jnp.where(i > 0, xp_ref[...].astype(F32)[hrows - 8:, :], 0.0)
        ext = jnp.concatenate([prev, xv], axis=0)
        acc = xv * w_ref[taps - 1:taps, :]
        for s in range(1, taps):
            acc = acc + pltpu.roll(ext, s, 0)[8:, :] * w_ref[taps - 1 - s:taps - s, :]
        if b_ref is not None:
            acc = acc + b_ref[...]
        o_ref[...] = acc.astype(o_ref.dtype)

    in_specs = [pl.BlockSpec((tr, tc), lambda j, i: (i, cb0 + j)),
                pl.BlockSpec((hrows, tc), lambda j, i: (jnp.maximum(i * hb - 1, 0), cb0 + j)),
                pl.BlockSpec((taps, tc), lambda j, i: (0, j))]
    args = [x, x, w]
    if b is not None:
        in_specs.append(pl.BlockSpec((1, tc), lambda j, i: (0, j)))
        args.append(b)
    return _pallas(
        body, out_shape=jax.ShapeDtypeStruct((R, width), BF16), grid=grid, in_specs=in_specs,
        out_specs=pl.BlockSpec((tr, tc), lambda j, i: (i, j)),
        name=name, compiler_params=_params(grid))(*args)


def _conv_bwd(name, x, xcol0, w, dy, *, taps, width, tr, tc, with_bias, dx_into=None):
    R = x.shape[0]
    nr = R // tr
    grid = (width // tc, nr)
    cb0 = xcol0 // tc
    hrows = 16 if dy.dtype == BF16 else 8
    hb = tr // hrows
    n_ext = tr + 8
    n_al = 0 if dx_into is None else 1

    def body(*refs):
        x_ref, w_ref, dy_ref, dyn_ref = refs[:4]
        dx_ref, dw_ref = refs[4 + n_al], refs[5 + n_al]
        db_ref = refs[6 + n_al] if with_bias else None
        i = pl.program_id(1)
        xv = x_ref[...].astype(F32)
        dyv = dy_ref[...].astype(F32)
        nxt = dyn_ref[...].astype(F32)[:8, :]
        dext = jnp.concatenate([dyv, jnp.where(i < nr - 1, nxt, 0.0)], axis=0)
        dx = dyv * w_ref[taps - 1:taps, :]
        dws = [None] * taps
        dws[taps - 1] = jnp.sum(xv * dyv, axis=0, keepdims=True)
        for s in range(1, taps):
            ahead = pltpu.roll(dext, n_ext - s, 0)[:tr, :]
            dx = dx + ahead * w_ref[taps - 1 - s:taps - s, :]
            dws[taps - 1 - s] = jnp.sum(xv * ahead, axis=0, keepdims=True)
        dx_ref[...] = dx.astype(dx_ref.dtype)

        @pl.when(i == 0)
        def _():
            for k in range(taps):
                dw_ref[k:k + 1, :] = dws[k]
            if db_ref is not None:
                db_ref[...] = jnp.sum(dyv, axis=0, keepdims=True)

        @pl.when(i > 0)
        def _():
            for k in range(taps):
                dw_ref[k:k + 1, :] += dws[k]
            if db_ref is not None:
                db_ref[...] += jnp.sum(dyv, axis=0, keepdims=True)

    in_specs = [pl.BlockSpec((tr, tc), lambda j, i: (i, cb0 + j)),
                pl.BlockSpec((taps, tc), lambda j, i: (0, j)),
                pl.BlockSpec((tr, tc), lambda j, i: (i, j)),
                pl.BlockSpec((hrows, tc), lambda j, i: (jnp.minimum((i + 1) * hb, R // hrows - 1), j))]
    args = [x, w, dy, dy]
    if dx_into is None:
        dx_shape, dx_spec, aliases = jax.ShapeDtypeStruct((R, width), BF16), pl.BlockSpec((tr, tc), lambda j, i: (i, j)), {}
    else:
        dx_shape = jax.ShapeDtypeStruct(dx_into.shape, dx_into.dtype)
        dx_spec, aliases = pl.BlockSpec((tr, tc), lambda j, i: (i, cb0 + j)), {4: 0}
        in_specs.append(ANY)
        args.append(dx_into)
    out_shape = [dx_shape, jax.ShapeDtypeStruct((taps, width), F32)]
    out_specs = [dx_spec, pl.BlockSpec((taps, tc), lambda j, i: (0, j))]
    if with_bias:
        out_shape.append(jax.ShapeDtypeStruct((1, width), F32))
        out_specs.append(pl.BlockSpec((1, tc), lambda j, i: (0, j)))
    return _pallas(
        body, out_shape=out_shape, grid=grid, in_specs=in_specs, out_specs=out_specs, input_output_aliases=aliases,
        name=name, compiler_params=_params(grid))(*args)


def _row_mask(cfg, i, tr):
    rows = i * tr + lax.broadcasted_iota(jnp.int32, (tr, 1), 0)
    return (rows >= cfg.front).astype(F32)


def _make_rms_fn(cfg, tr, with_residual):
    def fn(pids, h, g):
        hm = h * _row_mask(cfg, pids[0], tr)
        if with_residual:
            return _rms(hm, g), hm
        return (_rms(hm, g),)
    return fn


def _make_gdn_prep_fn(cfg, tr):
    d, hg = cfg.d, cfg.hg

    def fn(pids, c, tail, alog, dtb):
        cq, ck, cv = c[:, :d], c[:, d:2 * d], c[:, 2 * d:]
        mask = _row_mask(cfg, pids[0], tr)
        j, col = _iota2(LANES, d, 0), _iota2(LANES, d, 1)
        ea = ((col >> 7) == j).astype(F32)
        eb = ((col >> 7) + hg == j).astype(F32)
        al = jnp.sum(alog, axis=0, keepdims=True)
        db = jnp.sum(dtb, axis=0, keepdims=True)
        lg = _dot_sel(-jnp.exp(al) * _softplus(tail + db) * mask, ea, False)
        beta = _dot_sel(_sigmoid(tail) * mask, eb, False)
        sq, sk, sv = _silu(cq), _silu(ck), _silu(cv)
        qs, ks = [], []
        for h in range(hg):
            sl = slice(h * GDN_DK, (h + 1) * GDN_DK)
            qh, kh = sq[:, sl], sk[:, sl]
            qs.append(qh * lax.rsqrt(jnp.sum(qh * qh, axis=-1, keepdims=True) + EPS) * (GDN_DK ** -0.5))
            ks.append(kh * lax.rsqrt(jnp.sum(kh * kh, axis=-1, keepdims=True) + EPS))
        return jnp.concatenate(qs, axis=1), jnp.concatenate(ks, axis=1), sv, beta, lg
    return fn


def _gdn_intra_fn(pids, q, k, v, bB, lB, t_saved=None):
    rows = q.shape[0]
    nb = rows // CHUNK
    q3, k3, v3, b3, l3 = [t.reshape(nb, CHUNK, GDN_DK) for t in (q, k, v, bB, lB)]
    r, c = _iota2(CHUNK, CHUNK, 0), _iota2(CHUNK, CHUNK, 1)
    tril = (r >= c)
    strict = (r > c)
    gcol = _sel_dot(_bcast(tril.astype(F32), nb), l3)
    grow = jnp.swapaxes(gcol, 1, 2)[:, :CHUNK, :]
    diff = gcol[:, :, :CHUNK] - grow
    decay = jnp.where(tril[None], jnp.exp(jnp.where(tril[None], diff, 0.0)), 0.0)
    kb = k3 * b3
    m = jnp.where(strict[None], _dot(kb, k3, False, True) * decay, 0.0)
    t = _tri_inv_raw(m) if t_saved is None else _tri_inv_given(m, t_saved.reshape(nb, CHUNK, CHUNK))
    eg = jnp.exp(gcol)
    u = _dot(t, v3 * b3)
    w = _dot(t, kb * eg)
    attn = _dot(q3, k3, False, True) * decay
    qd = q3 * eg
    glast = jnp.sum(l3, axis=1, keepdims=True)
    kd = k3 * jnp.exp(glast - gcol)
    gl = jnp.exp(glast)
    outs = (u.reshape(rows, GDN_DK), w.reshape(rows, GDN_DK), attn.reshape(1, rows, CHUNK),
            qd.reshape(rows, GDN_DK), kd.reshape(rows, GDN_DK), gl.reshape(1, nb, 1, GDN_DK))
    return outs + (t.reshape(1, rows, CHUNK),) if t_saved is None else outs


def _make_rot_fn(cfg):
    hr = cfg.hr
    half = RET_DK // 2

    def fn(pids, rqk, cos, sin):
        rq, rk = rqk[:, :cfg.d], rqk[:, cfg.d:]

        def rot(t, scale):
            outs = []
            for h in range(hr):
                x1 = t[:, h * RET_DK:h * RET_DK + half]
                x2 = t[:, h * RET_DK + half:(h + 1) * RET_DK]
                outs += [(x1 * cos - x2 * sin) * scale, (x2 * cos + x1 * sin) * scale]
            return jnp.concatenate(outs, axis=1)
        return rot(rq, 1.0), rot(rk, RET_DK ** -0.5)
    return fn


def _make_mix_fn(cfg):
    hg, hr = cfg.hg, cfg.hr

    def fn(pids, oa, ob, pm, gnorm):
        d = cfg.d
        gz, rg, gate_a, gate_b = pm[:, :d], pm[:, d:2 * d], pm[:, 2 * d:3 * d], pm[:, 3 * d:]
        oas = []
        for h in range(hg):
            oh = oa[:, h * GDN_DK:(h + 1) * GDN_DK]
            oas.append(oh * lax.rsqrt(jnp.mean(oh * oh, axis=-1, keepdims=True) + EPS) * gnorm)
        ya = jnp.concatenate(oas, axis=1) * _silu(gz)
        obs = []
        for h in range(hr):
            oh = ob[:, h * RET_DK:(h + 1) * RET_DK]
            obs.append(oh * lax.rsqrt(jnp.mean(oh * oh, axis=-1, keepdims=True) + EPS))
        yb = _silu(rg) * jnp.concatenate(obs, axis=1)
        return (_sigmoid(gate_a) * ya + _sigmoid(gate_b) * yb,)
    return fn


def _act_fn(pids, u):
    f = u.shape[1] // 2
    return (_silu(u[:, :f]) * u[:, f:],)


def _gdn_step(s, u, w, a, qd, kd, gl):
    top = _dot(jnp.concatenate([w, qd], axis=0), s)
    v_new = u - top[:CHUNK]
    bot = _dot(jnp.concatenate([a, kd.T], axis=0), v_new)
    o = top[CHUNK:] + bot[:CHUNK]
    s2 = s * gl + bot[CHUNK:]
    return s2, o


def _gdn_step_bwd(s, u, w, a, qd, kd, gl, ds2, do):
    lw = jnp.concatenate([w, qd], axis=0)
    v_new = u - _raw_dot(w, s, False, False, False)
    dv = _raw_dot(a, do, True, False, False) + _raw_dot(kd, ds2, False, False, False)
    da = _raw_dot(do, v_new, False, True, False)
    dkd = _raw_dot(v_new, ds2, False, True, False)
    dtop = jnp.concatenate([-dv, do], axis=0)
    dlw = _raw_dot(dtop, s, False, True, False)
    ds = ds2 * gl + _raw_dot(lw, dtop, True, False, False)
    dgl = jnp.sum(ds2 * s, axis=0, keepdims=True)
    return ds, dv, dlw[:CHUNK], da, dlw[CHUNK:], dkd, dgl


def _ret_step(s, q, k, v, dm, qdc, kdc, g):
    att = _dot(q, k, False, True) * dm
    bot = _dot(jnp.concatenate([att, (k * kdc).T], axis=0), v)
    o = bot[:CHUNK] + _dot(q * qdc, s)
    s2 = s * g + bot[CHUNK:]
    return s2, o


class Part(NamedTuple):
    body: object
    args: list
    in_specs: list
    out_shape: list
    out_specs: list
    scratch: list
    aliases: dict = {}


def _run_parts(name, grid, parts):
    n_in = [len(p.args) for p in parts]
    n_out = [len(p.out_shape) for p in parts]
    n_sc = [len(p.scratch) for p in parts]
    off_in = [sum(n_in[:k]) for k in range(len(parts))]
    off_out = [sum(n_out[:k]) for k in range(len(parts))]
    off_sc = [sum(n_sc[:k]) for k in range(len(parts))]

    def body(*refs):
        ins, outs, scr = refs[:sum(n_in)], refs[sum(n_in):sum(n_in) + sum(n_out)], refs[sum(n_in) + sum(n_out):]
        for k, p in enumerate(parts):
            p.body(*ins[off_in[k]:off_in[k] + n_in[k]], *outs[off_out[k]:off_out[k] + n_out[k]],
                   *scr[off_sc[k]:off_sc[k] + n_sc[k]])

    aliases = {off_in[k] + i: off_out[k] + o for k, p in enumerate(parts) for i, o in p.aliases.items()}
    res = _pallas(
        body, out_shape=sum((p.out_shape for p in parts), []), grid=grid, in_specs=sum((p.in_specs for p in parts), []),
        out_specs=sum((p.out_specs for p in parts), []), scratch_shapes=sum((p.scratch for p in parts), []),
        input_output_aliases=aliases, name=name, compiler_params=_params(grid))(*sum((p.args for p in parts), []))
    return [res[off_out[k]:off_out[k] + n_out[k]] for k in range(len(parts))]


def _gdn_scan_fwd(cfg, u, w, attn, qd, kd, gl):
    d, hg, nch, sc = cfg.d, cfg.hg, cfg.nch, cfg.sc
    nst = nch // sc

    def body(u_ref, w_ref, a_ref, qd_ref, kd_ref, gl_ref, o_ref, ss_ref, s_ref):
        @pl.when(pl.program_id(0) == 0)
        def _():
            s_ref[...] = jnp.zeros(s_ref.shape, F32)

        states = [s_ref[h] for h in range(hg)]
        for j in range(sc):
            rows = slice(j * CHUNK, (j + 1) * CHUNK)
            outs = []
            for h in range(hg):
                sl = slice(h * GDN_DK, (h + 1) * GDN_DK)
                ss_ref[j, h] = states[h]
                states[h], o = _gdn_step(states[h], u_ref[rows, sl], w_ref[rows, sl], a_ref[h, rows, :],
                                         qd_ref[rows, sl], kd_ref[rows, sl], gl_ref[h, j])
                outs.append(o)
            o_ref[rows, :] = jnp.concatenate(outs, axis=1)
        for h in range(hg):
            s_ref[h] = states[h]

    row = pl.BlockSpec((sc * CHUNK, d), lambda n: (n, 0))
    return Part(
        body, [u, w, attn, qd, kd, gl],
        [row, row, pl.BlockSpec((hg, sc * CHUNK, CHUNK), lambda n: (0, n, 0)), row, row,
         pl.BlockSpec((hg, sc, 1, GDN_DK), lambda n: (0, n, 0, 0))],
        [jax.ShapeDtypeStruct((cfg.rp, d), F32), jax.ShapeDtypeStruct((nch, hg, GDN_DK, GDN_DK), F32)],
        [row, pl.BlockSpec((sc, hg, GDN_DK, GDN_DK), lambda n: (n, 0, 0, 0))],
        [pltpu.VMEM((hg, GDN_DK, GDN_DK), F32)])


def _gdn_scan_bwd(cfg, do, u, w, attn, qd, kd, gl, ss):
    d, hg, nch, sc = cfg.d, cfg.hg, cfg.nch, cfg.sc
    nst = nch // sc

    def body(do_ref, u_ref, w_ref, a_ref, qd_ref, kd_ref, gl_ref, ss_ref,
             du_ref, dw_ref, da_ref, dqd_ref, dkd_ref, dgl_ref, ds_ref):
        @pl.when(pl.program_id(0) == 0)
        def _():
            ds_ref[...] = jnp.zeros(ds_ref.shape, F32)

        dstates = [ds_ref[h] for h in range(hg)]
        for j in reversed(range(sc)):
            rows = slice(j * CHUNK, (j + 1) * CHUNK)
            dus, dws, dqds, dkds = [], [], [], []
            for h in range(hg):
                sl = slice(h * GDN_DK, (h + 1) * GDN_DK)
                args = (ss_ref[j, h], u_ref[rows, sl], w_ref[rows, sl], a_ref[h, rows, :], qd_ref[rows, sl],
                        kd_ref[rows, sl], gl_ref[h, j])
                dstates[h], du, dw, da, dqd, dkd, dgl = _gdn_step_bwd(*args, dstates[h], do_ref[rows, sl])
                da_ref[h, rows, :] = da
                dgl_ref[h, j] = dgl
                dus.append(du)
                dws.append(dw)
                dqds.append(dqd)
                dkds.append(dkd)
            du_ref[rows, :] = jnp.concatenate(dus, axis=1)
            dw_ref[rows, :] = jnp.concatenate(dws, axis=1)
            dqd_ref[rows, :] = jnp.concatenate(dqds, axis=1)
            dkd_ref[rows, :] = jnp.concatenate(dkds, axis=1)
        for h in range(hg):
            ds_ref[h] = dstates[h]

    row = pl.BlockSpec((sc * CHUNK, d), lambda n: (nst - 1 - n, 0))
    aspec = pl.BlockSpec((hg, sc * CHUNK, CHUNK), lambda n: (0, nst - 1 - n, 0))
    gspec = pl.BlockSpec((hg, sc, 1, GDN_DK), lambda n: (0, nst - 1 - n, 0, 0))
    rowshape = jax.ShapeDtypeStruct((cfg.rp, d), F32)
    return Part(
        body, [do, u, w, attn, qd, kd, gl, ss],
        [row, row, row, aspec, row, row, gspec, pl.BlockSpec((sc, hg, GDN_DK, GDN_DK), lambda n: (nst - 1 - n, 0, 0, 0))],
        [rowshape, rowshape, jax.ShapeDtypeStruct(attn.shape, F32), rowshape, rowshape, jax.ShapeDtypeStruct(gl.shape, F32)],
        [row, row, aspec, row, row, gspec],
        [pltpu.VMEM((hg, GDN_DK, GDN_DK), F32)])


def _ret_consts(cfg):
    hr = cfg.hr
    lg = np.log(1.0 - 2.0 ** (-5.0 - np.arange(hr, dtype=np.float64)))
    idx = np.arange(CHUNK, dtype=np.float64)
    tril = np.tril(np.ones((CHUNK, CHUNK), dtype=bool))
    dm = np.where(tril[None], np.exp((idx[:, None] - idx[None, :])[None] * lg[:, None, None]), 0.0)
    qdc = np.exp((idx[None, :] + 1.0) * lg[:, None])
    kdc = np.exp((CHUNK - 1.0 - idx[None, :]) * lg[:, None])
    gch = np.exp(CHUNK * lg)
    qdc = np.broadcast_to(qdc[:, :, None], (hr, CHUNK, RET_DK))
    kdc = np.broadcast_to(kdc[:, :, None], (hr, CHUNK, RET_DK))
    gch = np.broadcast_to(gch[:, None, None], (hr, 1, RET_DK))
    return tuple(jnp.asarray(np.ascontiguousarray(t), F32) for t in (dm, qdc, kdc, gch))


def _ret_scan_fwd(cfg, qr, kr, proj, consts):
    d, hr, nch, sc = cfg.d, cfg.hr, cfg.nch, cfg.sc
    nst = nch // sc
    dm, qdc, kdc, gch = consts

    def body(q_ref, k_ref, v_ref, dm_ref, qdc_ref, kdc_ref, g_ref, o_ref, ss_ref, s_ref):
        @pl.when(pl.program_id(0) == 0)
        def _():
            s_ref[...] = jnp.zeros(s_ref.shape, F32)

        states = [s_ref[h] for h in range(hr)]
        for j in range(sc):
            rows = slice(j * CHUNK, (j + 1) * CHUNK)
            outs = []
            for h in range(hr):
                sl = slice(h * RET_DK, (h + 1) * RET_DK)
                ss_ref[j, h] = states[h]
                states[h], o = _ret_step(states[h], q_ref[rows, sl], k_ref[rows, sl], v_ref[rows, sl], dm_ref[h],
                                         qdc_ref[h], kdc_ref[h], g_ref[h])
                outs.append(o)
            o_ref[rows, :] = jnp.concatenate(outs, axis=1)
        for h in range(hr):
            s_ref[h] = states[h]

    row = pl.BlockSpec((sc * CHUNK, d), lambda n: (n, 0))
    return Part(
        body, [qr, kr, proj, dm, qdc, kdc, gch],
        [row, row, pl.BlockSpec((sc * CHUNK, d), lambda n: (n, RV_BLOCK)), _full(dm), _full(qdc), _full(kdc), _full(gch)],
        [jax.ShapeDtypeStruct((cfg.rp, d), F32), jax.ShapeDtypeStruct((nch, hr, RET_DK, RET_DK), F32)],
        [row, pl.BlockSpec((sc, hr, RET_DK, RET_DK), lambda n: (n, 0, 0, 0))],
        [pltpu.VMEM((hr, RET_DK, RET_DK), F32)])


def _ret_scan_bwd(cfg, do, qr, kr, proj, consts, ss, dproj):
    d, hr, nch, sc = cfg.d, cfg.hr, cfg.nch, cfg.sc
    nst = nch // sc
    dm, qdc, kdc, gch = consts

    def body(do_ref, q_ref, k_ref, v_ref, dm_ref, qdc_ref, kdc_ref, g_ref, ss_ref, _, dq_ref, dk_ref, dv_ref, ds_ref):
        @pl.when(pl.program_id(0) == 0)
        def _():
            ds_ref[...] = jnp.zeros(ds_ref.shape, F32)

        dstates = [ds_ref[h] for h in range(hr)]
        for j in reversed(range(sc)):
            rows = slice(j * CHUNK, (j + 1) * CHUNK)
            dqs, dks, dvs = [], [], []
            for h in range(hr):
                sl = slice(h * RET_DK, (h + 1) * RET_DK)
                cs = (dm_ref[h], qdc_ref[h], kdc_ref[h], g_ref[h])
                _, vjp_fn = jax.vjp(lambda s, q, k, v, cs=cs: _ret_step(s, q, k, v, *cs),
                                    ss_ref[j, h], q_ref[rows, sl], k_ref[rows, sl], v_ref[rows, sl])
                dstates[h], dq, dk, dv = vjp_fn((dstates[h], do_ref[rows, sl]))
                dqs.append(dq)
                dks.append(dk)
                dvs.append(dv)
            dq_ref[rows, :] = jnp.concatenate(dqs, axis=1)
            dk_ref[rows, :] = jnp.concatenate(dks, axis=1)
            dv_ref[rows, :] = jnp.concatenate(dvs, axis=1).astype(dv_ref.dtype)
        for h in range(hr):
            ds_ref[h] = dstates[h]

    row = pl.BlockSpec((sc * CHUNK, d), lambda n: (nst - 1 - n, 0))
    rowshape = jax.ShapeDtypeStruct((cfg.rp, d), F32)
    vspec = pl.BlockSpec((sc * CHUNK, d), lambda n: (nst - 1 - n, RV_BLOCK))
    return Part(
        body, [do, qr, kr, proj, dm, qdc, kdc, gch, ss, dproj],
        [row, row, row, vspec, _full(dm), _full(qdc), _full(kdc), _full(gch),
         pl.BlockSpec((sc, hr, RET_DK, RET_DK), lambda n: (nst - 1 - n, 0, 0, 0)), ANY],
        [rowshape, rowshape, jax.ShapeDtypeStruct(dproj.shape, dproj.dtype)],
        [row, row, vspec],
        [pltpu.VMEM((hr, RET_DK, RET_DK), F32)], {9: 2})


def _final(cfg, h2, normf, tgt):
    d, tr = cfg.d, cfg.xrow
    nr = cfg.rp // tr

    def body(h_ref, g_ref, t_ref, dh_ref, dg_ref, loss_ref):
        i = pl.program_id(0)
        y, vjp_fn = jax.vjp(_rms, h_ref[...], g_ref[...])
        err = jnp.where(i >= 1, y - t_ref[...], 0.0)
        dh, dg = vjp_fn(err * (1.0 / d))
        dh_ref[...] = dh
        part = jnp.zeros((8, LANES), F32) + 0.5 * jnp.sum(err * err) * (1.0 / d)

        @pl.when(i == 0)
        def _():
            dg_ref[...] = dg
            loss_ref[...] = part

        @pl.when(i > 0)
        def _():
            dg_ref[...] += dg
            loss_ref[...] += part

    return _pallas(
        body,
        out_shape=[jax.ShapeDtypeStruct((cfg.rp, d), F32), jax.ShapeDtypeStruct((1, d), F32),
                   jax.ShapeDtypeStruct((8, LANES), F32)],
        grid=(nr,),
        in_specs=[_rows(tr, d), _full(normf), pl.BlockSpec((tr, d), lambda i: (jnp.maximum(i - 1, 0), 0))],
        out_specs=[_rows(tr, d), pl.BlockSpec((1, d), lambda i: (0, 0)), pl.BlockSpec((8, LANES), lambda i: (0, 0))],
        name="final_loss", compiler_params=_params((nr,)))(h2, normf, tgt)


ANY = pl.BlockSpec(memory_space=pl.ANY)


def _place():
    x, y, c = lax.axis_index("x"), lax.axis_index("y"), lax.axis_index("c")
    others = [(1 - x, y), (x, 1 - y), (1 - x, 1 - y)]
    return x, y, c, others


def _row_tile(rows, cap=256):
    return max(t for t in range(16, min(rows, cap) + 1, 16) if rows % t == 0)


class Comm(NamedTuple):
    ins: list
    outs: list
    sems: list
    start: object
    finish: object


def _run_comm(name, comm):
    n_in, n_out = len(comm.ins), len(comm.outs)

    def body(*refs):
        ins, outs, sems = refs[:n_in], refs[n_in:n_in + n_out], refs[n_in + n_out:]
        comm.start(ins, outs, sems)
        comm.finish(ins, outs, sems)

    return _pallas(body, out_shape=comm.outs, in_specs=[ANY] * n_in, out_specs=[ANY] * n_out,
                   scratch_shapes=comm.sems, name=name)(*comm.ins)


def _gather_comm(ws):
    n = len(ws)
    halves = [w.shape[0] // 2 for w in ws]

    def copies(w_refs, o_refs, sems):
        send_sems, recv_sems = sems
        x, y, c, others = _place()
        me = 2 * x + y
        chips = [2 * px + py for px, py in others]

        def piece(a, chip, core):
            return o_refs[a].at[chip, pl.ds(core * halves[a], halves[a]), :]

        def copy(a, k, src, chip, core, to):
            return pltpu.make_async_remote_copy(src_ref=src, dst_ref=piece(a, chip, core), send_sem=send_sems.at[6 * a + k],
                                                recv_sem=recv_sems.at[6 * a + k], device_id=to, device_id_type=MESH)

        def first(j, a):
            return copy(a, j, w_refs[a].at[pl.ds(c * halves[a], halves[a]), :], me, c, (*others[j], c))

        def landed(j, a):
            return copy(a, j, piece(a, chips[j], c), chips[j], c, (x, y, c))

        def passed(j, a):
            return copy(a, 3 + j, piece(a, chips[j], c), chips[j], c, (x, y, 1 - c))

        def from_sibling(j, a):
            return copy(a, 3 + j, piece(a, chips[j], 1 - c), chips[j], 1 - c, (x, y, c))

        return first, landed, passed, from_sibling

    pairs = [(j, a) for j in range(3) for a in range(n)]

    def start(w_refs, o_refs, sems):
        first, _, _, _ = copies(w_refs, o_refs, sems)
        for j, a in pairs:
            first(j, a).start()

    def finish(w_refs, o_refs, sems):
        first, landed, passed, from_sibling = copies(w_refs, o_refs, sems)
        for j, a in pairs:
            landed(j, a).wait_recv()
            passed(j, a).start()
        for j, a in pairs:
            from_sibling(j, a).wait_recv()
        for j, a in pairs:
            first(j, a).wait_send()
            passed(j, a).wait_send()

    return Comm(list(ws), [jax.ShapeDtypeStruct((N_CHIPS,) + w.shape, w.dtype) for w in ws],
                [pltpu.SemaphoreType.DMA((6 * n,)), pltpu.SemaphoreType.DMA((6 * n,))], start, finish)


def _pair_exchange(name, gs):
    n = len(gs)

    def body(*refs):
        g_refs, o_refs = refs[:n], refs[n:2 * n]
        send_sems, recv_sems = refs[2 * n:]
        x, y, c, _ = _place()
        cps = []
        for a in range(n):
            half = gs[a].shape[1] // 2
            cp = pltpu.make_async_remote_copy(
                src_ref=g_refs[a].at[:, pl.ds((1 - c) * half, half), :], dst_ref=o_refs[a], send_sem=send_sems.at[a],
                recv_sem=recv_sems.at[a], device_id=(x, y, 1 - c), device_id_type=MESH)
            cp.start()
            cps.append(cp)
        for cp in cps:
            cp.wait()

    return _pallas(
        body, out_shape=[jax.ShapeDtypeStruct((N_CHIPS, g.shape[1] // 2, g.shape[2]), g.dtype) for g in gs],
        in_specs=[ANY] * n, out_specs=[ANY] * n,
        scratch_shapes=[pltpu.SemaphoreType.DMA((n,)), pltpu.SemaphoreType.DMA((n,))], name=name)(*gs)


def _pair_sum(name, g, recv, cidx):
    half, cols = recv.shape[1], recv.shape[2]
    tr = _row_tile(half)
    nblk = half // tr

    def body(c_ref, g_ref, r_ref, o_ref):
        o_ref[...] = (g_ref[...] + r_ref[...]).astype(o_ref.dtype)

    grid_spec = pltpu.PrefetchScalarGridSpec(
        num_scalar_prefetch=1, grid=(N_CHIPS, nblk),
        in_specs=[pl.BlockSpec((1, tr, cols), lambda s, i, c: (s, c[0] * nblk + i, 0)),
                  pl.BlockSpec((1, tr, cols), lambda s, i, c: (s, i, 0))],
        out_specs=pl.BlockSpec((1, tr, cols), lambda s, i, c: (s, i, 0)))
    return _pallas(
        body, out_shape=jax.ShapeDtypeStruct((N_CHIPS, half, cols), BF16), grid_spec=grid_spec,
        name=name, compiler_params=_params((N_CHIPS, nblk)))(cidx, g, recv)


def _exchange_comm(parts):
    n = len(parts)

    def copies(p_refs, o_refs, sems):
        send_sems, recv_sems = sems
        x, y, c, others = _place()
        me = 2 * x + y

        def copy(a, j, src_chip, dst_chip):
            px, py = others[j]
            return pltpu.make_async_remote_copy(
                src_ref=p_refs[a].at[src_chip], dst_ref=o_refs[a].at[dst_chip], send_sem=send_sems.at[3 * a + j],
                recv_sem=recv_sems.at[3 * a + j], device_id=(px, py, c), device_id_type=MESH)

        def send(j, a):
            return copy(a, j, 2 * others[j][0] + others[j][1], me)

        def arrival(j, a):
            return copy(a, j, me, 2 * others[j][0] + others[j][1])

        return send, arrival

    pairs = [(j, a) for j in range(3) for a in range(n)]

    def start(p_refs, o_refs, sems):
        send, _ = copies(p_refs, o_refs, sems)
        for j, a in pairs:
            send(j, a).start()

    def finish(p_refs, o_refs, sems):
        send, arrival = copies(p_refs, o_refs, sems)
        for j, a in pairs:
            arrival(j, a).wait_recv()
        for j, a in pairs:
            send(j, a).wait_send()

    return Comm(list(parts), [jax.ShapeDtypeStruct(p.shape, p.dtype) for p in parts],
                [pltpu.SemaphoreType.DMA((3 * n,)), pltpu.SemaphoreType.DMA((3 * n,))], start, finish)


def _chip_sum(name, part, slots, chip):
    half, cols = slots.shape[1], slots.shape[2]
    tr = _row_tile(half)

    def body(me_ref, p_ref, *rest):
        s_refs, o_ref = rest[:N_CHIPS], rest[N_CHIPS]
        own = p_ref[...].astype(F32)
        v = [jnp.where(me_ref[0] == k, own, s_refs[k][...].astype(F32)) for k in range(N_CHIPS)]
        o_ref[...] = ((v[0] + v[1]) + v[2]) + v[3]

    def slot_spec(k):
        return pl.BlockSpec((None, tr, cols), lambda i, me: (jnp.where(me[0] == k, (k + 1) % N_CHIPS, k), i, 0))

    grid_spec = pltpu.PrefetchScalarGridSpec(
        num_scalar_prefetch=1, grid=(half // tr,),
        in_specs=[pl.BlockSpec((None, tr, cols), lambda i, me: (me[0], i, 0))] + [slot_spec(k) for k in range(N_CHIPS)],
        out_specs=pl.BlockSpec((tr, cols), lambda i, me: (i, 0)))
    return _pallas(
        body, out_shape=jax.ShapeDtypeStruct((half, cols), F32), grid_spec=grid_spec,
        name=name, compiler_params=_params((half // tr,)))(chip, part, *([slots] * N_CHIPS))


def _pair_swap(fins):
    n = len(fins)

    def body(*refs):
        f_refs, o_refs = refs[:n], refs[n:2 * n]
        send_sems, recv_sems = refs[2 * n:]
        x, y, c, _ = _place()
        cps = [pltpu.make_async_remote_copy(src_ref=f_refs[a], dst_ref=o_refs[a], send_sem=send_sems.at[a],
                                            recv_sem=recv_sems.at[a], device_id=(x, y, 1 - c), device_id_type=MESH)
               for a in range(n)]
        for cp in cps:
            cp.start()
        for cp in cps:
            cp.wait()

    return _pallas(
        body, out_shape=[jax.ShapeDtypeStruct(f.shape, f.dtype) for f in fins], in_specs=[ANY] * n, out_specs=[ANY] * n,
        scratch_shapes=[pltpu.SemaphoreType.DMA((n,)), pltpu.SemaphoreType.DMA((n,))], name="grad_pair_swap")(*fins)


def _adamw(name, w, g_own, g_other, m, v, cidx):
    R, cols = w.shape[-2:]
    lead = (None,) * (w.ndim - 2)
    zeros = (0,) * (w.ndim - 2)
    half = R // 2
    tr = _row_tile(half, 128)
    nblk = half // tr
    c1 = 1.0 - ADAM_B1 ** ADAM_STEP
    c2 = 1.0 - ADAM_B2 ** ADAM_STEP

    def body(c_ref, w_ref, go_ref, gs_ref, m_ref, v_ref, g_ref, d_ref, nm_ref, nv_ref):
        mine = (pl.program_id(0) // nblk) == c_ref[0]
        gv = jnp.where(mine, go_ref[...], gs_ref[...])
        nm = ADAM_B1 * m_ref[...] + (1.0 - ADAM_B1) * gv
        nv = ADAM_B2 * v_ref[...] + (1.0 - ADAM_B2) * (gv * gv)
        g_ref[...] = gv
        d_ref[...] = -ADAM_LR * ((nm / c1) / (jnp.sqrt(nv / c2) + ADAM_EPS) + ADAM_WD * w_ref[...])
        nm_ref[...] = nm
        nv_ref[...] = nv

    spec = pl.BlockSpec(lead + (tr, cols), lambda i, c: zeros + (i, 0))
    hspec = pl.BlockSpec((tr, cols), lambda i, c: (i % nblk, 0))
    shape = jax.ShapeDtypeStruct(w.shape, F32)
    grid_spec = pltpu.PrefetchScalarGridSpec(num_scalar_prefetch=1, grid=(R // tr,),
                                             in_specs=[spec, hspec, hspec, spec, spec], out_specs=[spec] * 4)
    return _pallas(
        body, out_shape=[shape] * 4, grid_spec=grid_spec,
        name=name, compiler_params=_params((R // tr,)))(cidx, w, g_own, g_other, m, v)


PARAMS = (("meta", 1), ("norm1", None), ("w_in", 2), ("gdn_conv_w", 2), ("gdn_a_log", None), ("gdn_dt_bias", None),
          ("gdn_norm", None), ("w_out", 1), ("norm2", None), ("w_ffn_up", 2), ("ffn_conv_w", 2), ("ffn_conv_b", None),
          ("w_ffn_down", 1), ("norm_f", None))
BIG = ("w_in", "w_out", "w_ffn_up", "w_ffn_down")
PACK_ALIGN = 1024
PACK_ROWS_ALIGN = 32


def _pack(arrs, dtype):
    parts, total = [], 0
    for a in arrs:
        f = a.reshape(-1).astype(dtype)
        pad = (-f.shape[0]) % PACK_ALIGN
        parts.append(jnp.pad(f, (0, pad)) if pad else f)
        total += f.shape[0] + pad
    rows = total // LANES
    rpad = (-rows) % PACK_ROWS_ALIGN
    if rpad:
        parts.append(jnp.zeros((rpad * LANES,), dtype))
    return jnp.concatenate(parts).reshape(rows + rpad, LANES)


def _unpack(buf, shapes):
    flat = buf.reshape(-1)
    outs, off = [], 0
    for s in shapes:
        n = int(np.prod(s))
        outs.append(flat[off:off + n].reshape(s))
        off += n + (-n) % PACK_ALIGN
    return outs


def _split4(a, axis):
    n = a.shape[axis] // N_CHIPS
    return [lax.slice_in_dim(a, s * n, (s + 1) * n, axis=axis) for s in range(N_CHIPS)]


PROJ_ORDER = (3, 7, 8, 9, 0, 1, 2, 6, 4, 5)


def _reorder_w_in(w, cfg):
    d, hg = cfg.d, cfg.hg

    def block(k):
        off = k * d + (2 * hg if k >= 4 else 0)
        return w[:, off:off + d]

    tail = jnp.pad(w[:, 4 * d:4 * d + 2 * hg], ((0, 0), (0, LANES - 2 * hg)))
    return jnp.concatenate([block(k) for k in PROJ_ORDER] + [tail], axis=1)


def _restore_w_in(wr, cfg):
    d, hg = cfg.d, cfg.hg
    at = {k: i for i, k in enumerate(PROJ_ORDER)}
    block = lambda k: wr[:, at[k] * d:(at[k] + 1) * d]
    return jnp.concatenate([block(k) for k in range(4)] + [wr[:, 10 * d:10 * d + 2 * hg]] +
                           [block(k) for k in range(4, 10)], axis=1)


def _step(cfg, x, tgt, shard, m_shard, v_shard):
    d, hg, dff, rp, tr, tm = cfg.d, cfg.hg, cfg.dff, cfg.rp, cfg.tr, cfg.tm
    nrow = rp // tr
    assert cfg.tf * N_CHIPS == 2 * dff and cfg.din % N_CHIPS == 0
    cidx = lax.axis_index("c").astype(jnp.int32).reshape(1)
    chip = (2 * lax.axis_index("x") + lax.axis_index("y")).astype(jnp.int32).reshape(1)

    axis = dict(PARAMS)
    small = ("meta", "gdn_conv_w", "ffn_conv_w")
    small_shapes = [shard[n].shape for n in small]
    mine = [shard[n][0].astype(BF16) for n in BIG] + [_pack([shard[n] for n in small], F32)]

    def with_own(gathered, own):
        return [lax.dynamic_update_slice(g, w[None], (chip[0], 0, 0)) for g, w in zip(gathered, own)]

    g_in, g_small = _run_comm("weights_gather_first", _gather_comm([mine[0], mine[4]]))
    g_small, = with_own([g_small], [mine[4]])
    w_in_r = _reorder_w_in(jnp.concatenate([jnp.where(chip[0] == s, mine[0], g_in[s]) for s in range(N_CHIPS)], axis=1),
                           cfg)
    per_chip = [_unpack(g_small[s], small_shapes) for s in range(N_CHIPS)]
    full = {n: jnp.concatenate([per_chip[s][k] for s in range(N_CHIPS)], axis=axis[n]) for k, n in enumerate(small)}
    meta = full["meta"]
    gconv_w = full["gdn_conv_w"][0]
    fconv_w = full["ffn_conv_w"][0]
    norm1, norm2, gnorm = shard["norm1"], shard["norm2"], shard["gdn_norm"]
    normf = shard["norm_f"].reshape(1, d)
    fconv_b = shard["ffn_conv_b"]
    alog = jnp.pad(shard["gdn_a_log"], ((0, 7), (0, LANES - hg)))
    dtb = jnp.pad(shard["gdn_dt_bias"], ((0, 7), (0, LANES - hg)))

    h0 = jnp.concatenate([jnp.zeros((cfg.front, d), F32), meta, x], axis=0)
    half = RET_DK // 2
    pos = np.arange(rp, dtype=np.float32) - np.float32(cfg.front)
    inv = (np.float32(1.0) / np.float32(ROPE_BASE) ** (np.arange(half, dtype=np.float32) / np.float32(half))).astype(np.float32)
    ang = pos[:, None] * inv[None, :]
    cos, sin = jnp.asarray(np.cos(ang), F32), jnp.asarray(np.sin(ang), F32)
    rconsts = _ret_consts(cfg)

    tr_n = 3 * tr if rp % (3 * tr) == 0 else tr
    rms_f = _make_rms_fn(cfg, tr_n, False)
    rms_b = _make_rms_fn(cfg, tr_n, True)
    rowshape = jax.ShapeDtypeStruct((rp, d), F32)
    rspec = _rows(tr, d)
    nspec = _rows(tr_n, d)

    def rms_fwd(name, h, g):
        return _stage_fwd(name, rms_f, (rp // tr_n,), [In(h, nspec), In(g, _full(g))],
                          [jax.ShapeDtypeStruct((rp, d), BF16)], [nspec])[0]

    wm = 10 * d
    w_main, w_tail = w_in_r[:, :wm], w_in_r[:, wm:]
    tn_in = 2560 if wm % 2560 == 0 else LANES
    hn1 = rms_fwd("rms1_fwd", h0, norm1)
    proj, rest = _mm("proj_fwd", hn1, w_main, tm=tm, tn=tn_in, tk=d, out_dtype=BF16, comm=_gather_comm(mine[1:4]))
    ptail = _mm("proj_tail_fwd", hn1, w_tail, tm=tm, tn=LANES, tk=d)
    g_out, g_up, g_down = with_own(rest, mine[1:4])
    w_out = g_out.reshape(d, d)
    w_up = g_up
    w_up_t = jnp.swapaxes(g_up, 1, 2).reshape(2 * dff, d)
    w_down = g_down.reshape(dff, d)
    cqkv = _conv_fwd("gdn_conv_fwd", proj, CONV_COL * d, gconv_w, None, taps=GDN_CONV, width=3 * d, tr=tr, tc=d)
    prep_fn = _make_gdn_prep_fn(cfg, tr)
    prep_ins = [In(cqkv, _rows(tr, 3 * d), BF16), In(ptail, _rows(tr, LANES), BF16),
                In(alog, _full(alog), F32, True), In(dtb, _full(dtb), F32, True)]
    qn, kn, vv, bB, lB = _stage_fwd("gdn_prep_fwd", prep_fn, (nrow,), prep_ins, [rowshape] * 5, [rspec] * 5)

    trg = cfg.nb * CHUNK
    gi_grid = (rp // trg, hg)
    hspec = pl.BlockSpec((trg, GDN_DK), lambda i, h: (i, h))
    aspec = pl.BlockSpec((1, trg, CHUNK), lambda i, h: (h, i, 0))
    gspec = pl.BlockSpec((1, cfg.nb, 1, GDN_DK), lambda i, h: (h, i, 0, 0))
    intra_ins = [In(t, hspec, F32) for t in (qn, kn, vv, bB, lB)]
    ashape = jax.ShapeDtypeStruct((hg, rp, CHUNK), F32)
    intra_shapes = [rowshape, rowshape, ashape, rowshape, rowshape, jax.ShapeDtypeStruct((hg, cfg.nch, 1, GDN_DK), F32), ashape]
    intra_specs = [hspec, hspec, aspec, hspec, hspec, gspec, aspec]
    gu, gw, gattn, gqd, gkd, ggl, gtinv = _stage_fwd("gdn_intra_fwd", _gdn_intra_fn, gi_grid, intra_ins, intra_shapes,
                                                     intra_specs)
    rot_fn = _make_rot_fn(cfg)

    def rot_ins(dproj=None):
        return [In(proj, _rows(tr_n, 2 * d, ROT_COL // 2), BF16, galias=dproj, gshape=(rp, wm)),
                In(cos, _rows(tr_n, half)), In(sin, _rows(tr_n, half))]

    qr, kr = _stage_fwd("rot_fwd", rot_fn, (rp // tr_n,), rot_ins(), [rowshape] * 2, [nspec] * 2)
    nst = cfg.nch // cfg.sc
    oa, gss = _run_parts("gdn_scan_fwd", (nst,), [_gdn_scan_fwd(cfg, gu, gw, gattn, gqd, gkd, ggl)])[0]
    ob, rss = _run_parts("ret_scan_fwd", (nst,), [_ret_scan_fwd(cfg, qr, kr, proj, rconsts)])[0]

    mix_fn = _make_mix_fn(cfg)
    mix_ins = [In(oa, rspec, F32), In(ob, rspec, F32), In(proj, _rows(tr, 4 * d, MIX_COL // 4), BF16, gshape=(rp, wm)),
               In(gnorm, _full(gnorm), F32, True)]
    ymix = _stage_fwd("mix_fwd", mix_fn, (nrow,), mix_ins, [jax.ShapeDtypeStruct((rp, d), BF16)], [rspec])[0]
    h1 = _mm("out_proj_fwd", ymix, w_out, tm=tm, tn=d, tk=d, add=h0)

    hn2 = rms_fwd("rms2_fwd", h1, norm2)
    up = _mm("ffn_up_fwd", hn2, w_up, tm=tm, tn=cfg.tf, tk=d, out_dtype=BF16)
    uc = _conv_fwd("ffn_conv_fwd", up, 0, fconv_w, fconv_b, taps=FFN_CONV, width=2 * dff, tr=tr, tc=cfg.tf)
    tra = tr
    act_ins = [In(uc, _rows(tra, 2 * dff), BF16)]
    act_spec = _rows(tra, dff)
    act = _stage_fwd("ffn_act_fwd", _act_fn, (rp // tra,), act_ins, [jax.ShapeDtypeStruct((rp, dff), BF16)], [act_spec])[0]
    h2 = _mm("ffn_down_fwd", act, w_down, tm=tm, tn=d, tk=cfg.tf, add=h1)

    dh2, g_normf, loss_blk = _final(cfg, h2, normf, tgt)
    loss = lax.psum(loss_blk[0, 0], ("x", "y", "c"))

    g_w_down = _mm_tn("ffn_down_dw", act, dh2, tr=tm, tka=cfg.tf, tn=d)
    dact = _mm("ffn_down_dx", dh2, w_down.T, tm=tm, tn=cfg.tf, tk=d)
    duc, = _stage_bwd("ffn_act_bwd", _act_fn, (rp // tra,), act_ins, [(dact, act_spec)])
    dup, g_fconv_w, g_fconv_b = _conv_bwd("ffn_conv_bwd", up, 0, fconv_w, duc, taps=FFN_CONV, width=2 * dff,
                                          tr=tr, tc=cfg.tf, with_bias=True)
    g_w_up = _mm_tn("ffn_up_dw", hn2, dup, tr=tm, tka=d, tn=cfg.tf, blocked=True)

    def pair_reduce(tag, names, arrs):
        recvs = _pair_exchange("grad_pair_exchange_" + tag, arrs)
        return [_pair_sum("grad_pair_sum_" + n, g, r, cidx) for n, g, r in zip(names, arrs, recvs)]

    parts_ffn = pair_reduce("ffn", ["w_ffn_down", "w_ffn_up"], [g_w_down.reshape(N_CHIPS, dff // N_CHIPS, d), g_w_up])
    dhn2, slots_ffn = _mm("ffn_up_dx", dup, w_up_t, tm=tm, tn=d, tk=2 * cfg.tf, comm=_exchange_comm(parts_ffn))

    def rms_bwd(name, h, g, dhn, dres):
        ins = [In(h, nspec, F32), In(g, _full(g), F32, True)]
        return _stage_bwd(name, rms_b, (rp // tr_n,), ins, [(dhn, nspec), (dres, nspec)])

    dh1, g_norm2 = rms_bwd("rms2_bwd", h1, norm2, dhn2, dh2)
    g_w_out = _mm_tn("out_proj_dw", ymix, dh1, tr=tm, tka=d, tn=d)
    dymix = _mm("out_proj_dx", dh1, w_out.T, tm=tm, tn=d, tk=d)
    doa, dob, dproj, g_gnorm = _stage_bwd("mix_bwd", mix_fn, (nrow,), mix_ins, [(dymix, rspec)])

    dqr, dkr, dproj = _run_parts("ret_scan_bwd", (nst,), [_ret_scan_bwd(cfg, dob, qr, kr, proj, rconsts, rss, dproj)])[0]
    dproj, = _stage_bwd("rot_bwd", rot_fn, (rp // tr_n,), rot_ins(dproj), [(dqr, nspec), (dkr, nspec)])
    dgu, dgw, dgattn, dgqd, dgkd, dggl = _run_parts(
        "gdn_scan_bwd", (nst,), [_gdn_scan_bwd(cfg, doa, gu, gw, gattn, gqd, gkd, ggl, gss)])[0]

    intra_cots = [(dgu, hspec), (dgw, hspec), (dgattn, aspec), (dgqd, hspec), (dgkd, hspec), (dggl, gspec)]
    dqn, dkn, dvv, dbB, dlB = _stage_bwd("gdn_intra_bwd", _gdn_intra_fn, gi_grid, intra_ins + [In(gtinv, aspec)], intra_cots)
    dcqkv, dtail, g_alog, g_dtb = _stage_bwd(
        "gdn_prep_bwd", prep_fn, (nrow,), prep_ins, [(t, rspec) for t in (dqn, dkn, dvv, dbB, dlB)])
    dproj, g_gconv_w = _conv_bwd("gdn_conv_bwd", proj, CONV_COL * d, gconv_w, dcqkv, taps=GDN_CONV, width=3 * d,
                                 tr=tr, tc=d, with_bias=False, dx_into=dproj)
    g_w_in_r = jnp.concatenate([_mm_tn("proj_dw", hn1, dproj, tr=tm, tka=d, tn=tn_in),
                                _mm_tn("proj_tail_dw", hn1, dtail, tr=tm, tka=d, tn=LANES)], axis=1)
    g_in4 = jnp.stack(_split4(_restore_w_in(g_w_in_r, cfg), 1))
    parts_mix = pair_reduce("mix", ["w_out", "w_in"], [g_w_out.reshape(N_CHIPS, d // N_CHIPS, d), g_in4])
    dhn1_tail = _mm("proj_tail_dx", dtail, w_tail.T, tm=tm, tn=d, tk=LANES)
    dhn1, slots_mix = _mm("proj_dx", dproj, w_main.T, tm=tm, tn=d, tk=tn_in // 2 if tn_in > LANES else LANES, add=dhn1_tail,
                          comm=_exchange_comm(parts_mix))
    dh0, g_norm1 = rms_bwd("rms1_bwd", h0, norm1, dhn1, dh1)

    grad_x = dh0[cfg.xrow:]
    small_grads = {
        "meta": dh0[cfg.front:cfg.xrow], "norm1": g_norm1, "gdn_conv_w": g_gconv_w[None],
        "gdn_a_log": g_alog[0:1, :hg], "gdn_dt_bias": g_dtb[0:1, :hg], "gdn_norm": g_gnorm, "norm2": g_norm2,
        "ffn_conv_w": g_fconv_w[None], "ffn_conv_b": g_fconv_b, "norm_f": g_normf.reshape(d),
    }

    small_names = [n for n, _ in PARAMS if n not in BIG]
    g_small = jnp.stack([_pack([small_grads[n] if axis[n] is None else _split4(small_grads[n], axis[n])[s]
                                for n in small_names], F32) for s in range(N_CHIPS)])
    parts_small = pair_reduce("small", ["small"], [g_small])
    slots_small = _run_comm("grad_exchange_small", _exchange_comm(parts_small))
    tags = ["w_in", "w_out", "w_ffn_up", "w_ffn_down", "small"]
    parts = [parts_mix[1], parts_mix[0], parts_ffn[1], parts_ffn[0], parts_small[0]]
    slots = [slots_mix[1], slots_mix[0], slots_ffn[1], slots_ffn[0], slots_small[0]]
    fins = [_chip_sum("grad_chip_sum_" + t, p, s, chip) for t, p, s in zip(tags, parts, slots)]
    sibs = _pair_swap(fins)

    def flat2(a):
        return a.reshape(-1, a.shape[-1])

    outs = {}
    for k, t in enumerate(BIG):
        res = _adamw("adamw_" + t, flat2(shard[t]), fins[k], sibs[k], flat2(m_shard[t]), flat2(v_shard[t]), cidx)
        outs[t] = [r.reshape(shard[t].shape) for r in res]
    small_shapes_all = [shard[n].shape for n in small_names]
    pk = lambda src: _pack([src[n] for n in small_names], F32)
    res = _adamw("adamw_small", pk(shard), fins[4], sibs[4], pk(m_shard), pk(v_shard), cidx)
    for k, r in enumerate(res):
        for n, a in zip(small_names, _unpack(r, small_shapes_all)):
            outs.setdefault(n, [None] * 4)[k] = a
    names = [n for n, _ in PARAMS]
    return (loss, grad_x[None], *[outs[n][k] for k in range(4) for n in names])


def kernel(x, meta, norm1, w_in, gdn_conv_w, gdn_a_log, gdn_dt_bias, gdn_norm, w_out, norm2, w_ffn_up, ffn_conv_w, ffn_conv_b, w_ffn_down, norm_f, loss_target, m_meta, m_norm1, m_w_in, m_gdn_conv_w, m_gdn_a_log, m_gdn_dt_bias, m_gdn_norm, m_w_out, m_norm2, m_w_ffn_up, m_ffn_conv_w, m_ffn_conv_b, m_w_ffn_down, m_norm_f, v_meta, v_norm1, v_w_in, v_gdn_conv_w, v_gdn_a_log, v_gdn_dt_bias, v_gdn_norm, v_w_out, v_norm2, v_w_ffn_up, v_ffn_conv_w, v_ffn_conv_b, v_w_ffn_down, v_norm_f):
    names = [n for n, _ in PARAMS]
    shard = dict(zip(names, (meta, norm1, w_in, gdn_conv_w, gdn_a_log, gdn_dt_bias, gdn_norm, w_out, norm2, w_ffn_up,
                             ffn_conv_w, ffn_conv_b, w_ffn_down, norm_f)))
    m_shard = dict(zip(names, (m_meta, m_norm1, m_w_in, m_gdn_conv_w, m_gdn_a_log, m_gdn_dt_bias, m_gdn_norm, m_w_out,
                               m_norm2, m_w_ffn_up, m_ffn_conv_w, m_ffn_conv_b, m_w_ffn_down, m_norm_f)))
    v_shard = dict(zip(names, (v_meta, v_norm1, v_w_in, v_gdn_conv_w, v_gdn_a_log, v_gdn_dt_bias, v_gdn_norm, v_w_out,
                               v_norm2, v_w_ffn_up, v_ffn_conv_w, v_ffn_conv_b, v_w_ffn_down, v_norm_f)))
    return _step(REAL, x[0], loss_target[0], shard, m_shard, v_shard)
```

```python
import functools
from typing import NamedTuple

import numpy as np
import jax
import jax.numpy as jnp
from jax import lax
from jax.experimental import pallas as pl
from jax.experimental.pallas import tpu as pltpu

F32 = jnp.float32
BF16 = jnp.bfloat16
EPS = 1e-6
CHUNK = 64
GDN_DK = 128
RET_DK = 256
GDN_CONV = 4
FFN_CONV = 3
ROPE_BASE = 10000.0
LANES = 128
N_CHIPS = 4
ADAM_LR, ADAM_B1, ADAM_B2, ADAM_EPS, ADAM_WD, ADAM_STEP = 0.001, 0.9, 0.999, 1e-08, 0.01, 10
MIX_COL, CONV_COL, RV_BLOCK, ROT_COL = 0, 4, 7, 8
MESH = pl.DeviceIdType.MESH
VMEM_LIMIT = 56 * 1024 * 1024


class Cfg(NamedTuple):
    d: int
    seq: int
    n_meta: int
    dff: int
    tr: int
    nb: int
    tm: int
    tf: int
    sc: int

    @property
    def hg(self): return self.d // GDN_DK
    @property
    def hr(self): return self.d // RET_DK
    @property
    def L(self): return self.n_meta + self.seq
    @property
    def rp(self): return -(-self.L // 256) * 256
    @property
    def front(self): return self.rp - self.L
    @property
    def xrow(self): return self.rp - self.seq
    @property
    def nch(self): return self.rp // CHUNK
    @property
    def din(self): return 10 * self.d + 2 * self.hg


REAL = Cfg(d=1024, seq=8192, n_meta=16, dff=2816, tr=256, nb=12, tm=1408, tf=1408, sc=4)


def _pallas(body, **kw):
    return pl.pallas_call(body, **kw)


def _sigmoid_raw(x):
    return 1.0 / (1.0 + jnp.exp(-x))


@jax.custom_vjp
def _sigmoid(x):
    return _sigmoid_raw(x)


def _sigmoid_fwd(x):
    s = _sigmoid_raw(x)
    return s, s


def _sigmoid_bwd(s, g):
    return (g * (s * (1.0 - s)),)


_sigmoid.defvjp(_sigmoid_fwd, _sigmoid_bwd)


@jax.custom_vjp
def _silu(x):
    return x * _sigmoid_raw(x)


def _silu_fwd(x):
    s = _sigmoid_raw(x)
    return x * s, (x, s)


def _silu_bwd(res, g):
    x, s = res
    return (g * (s * (1.0 + x * (1.0 - s))),)


_silu.defvjp(_silu_fwd, _silu_bwd)


def _softplus(x):
    return jnp.maximum(x, 0.0) + jnp.log(1.0 + jnp.exp(-jnp.abs(x)))


def _raw_dot(a, b, ta, tb, hi):
    if not hi:
        a = a.astype(BF16)
        b = b.astype(BF16)
    nbatch = a.ndim - 2
    ca = a.ndim - 2 if ta else a.ndim - 1
    cb = b.ndim - 1 if tb else b.ndim - 2
    batch = tuple(range(nbatch))
    return lax.dot_general(a, b, (((ca,), (cb,)), (batch, batch)),
                           precision=lax.Precision.HIGHEST if hi else None,
                           preferred_element_type=F32)


@functools.partial(jax.custom_vjp, nondiff_argnums=(2, 3, 4))
def _dot_p(a, b, ta, tb, hi):
    return _raw_dot(a, b, ta, tb, hi)


def _dot(a, b, ta=False, tb=False, hi=False):
    return _dot_p(a, b, ta, tb, hi)


def _dot_fwd(a, b, ta, tb, hi):
    return _raw_dot(a, b, ta, tb, hi), (a, b)


def _dot_bwd(ta, tb, hi, res, g):
    a, b = res
    if not ta and not tb:
        da, db = _dot(g, b, False, True, hi), _dot(a, g, True, False, hi)
    elif not ta and tb:
        da, db = _dot(g, b, False, False, hi), _dot(g, a, True, False, hi)
    elif ta and not tb:
        da, db = _dot(b, g, False, True, hi), _dot(a, g, False, False, hi)
    else:
        raise NotImplementedError
    return da.astype(a.dtype), db.astype(b.dtype)


_dot_p.defvjp(_dot_fwd, _dot_bwd)


def _iota2(n, m, axis):
    return lax.broadcasted_iota(jnp.int32, (n, m), axis)


def _bcast(mat, nb):
    return jnp.broadcast_to(mat[None], (nb,) + mat.shape)


def _split3(a):
    a0 = a.astype(BF16)
    r1 = a - a0.astype(F32)
    a1 = r1.astype(BF16)
    return a0, a1, (r1 - a1.astype(F32)).astype(BF16)


@functools.partial(jax.custom_vjp, nondiff_argnums=(2,))
def _dot_sel(a, e, te):
    eb = e.astype(BF16)
    p0, p1, p2 = (_raw_dot(p, eb, False, te, False) for p in _split3(a))
    return p0 + (p1 + p2)


def _dot_sel_fwd(a, e, te):
    return _dot_sel(a, e, te), e


def _dot_sel_bwd(te, e, g):
    return _dot_sel(g, e, not te), jnp.zeros_like(e)


_dot_sel.defvjp(_dot_sel_fwd, _dot_sel_bwd)


@jax.custom_vjp
def _sel_dot(e, x):
    eb = e.astype(BF16)
    p0, p1, p2 = (_raw_dot(eb, p, False, False, False) for p in _split3(x))
    return p0 + (p1 + p2)


def _sel_dot_fwd(e, x):
    return _sel_dot(e, x), e


def _sel_dot_bwd(e, g):
    eb = e.astype(BF16)
    p0, p1, p2 = (_raw_dot(eb, p, True, False, False) for p in _split3(g))
    return jnp.zeros_like(e), p0 + (p1 + p2)


_sel_dot.defvjp(_sel_dot_fwd, _sel_dot_bwd)


def _tri_inv_raw(m):
    nb = m.shape[0]
    r, c = _iota2(CHUNK, CHUNK, 0), _iota2(CHUNK, CHUNK, 1)
    t = _bcast((r == c).astype(F32), nb)
    b = 1
    while b < CHUNK:
        sh = b.bit_length() - 1
        off = ((r >> (sh + 1)) == (c >> (sh + 1))) & ((r >> sh) != (c >> sh)) & (r > c)
        cl = jnp.where(off[None], m, 0.0)
        t = t - _raw_dot(_raw_dot(t, cl, False, False, False), t, False, False, False)
        b *= 2
    return t


@jax.custom_vjp
def _tri_inv_given(m, t):
    return t


def _tri_inv_fwd(m, t):
    return t, t


def _tri_inv_bwd(t, g):
    return -_raw_dot(_raw_dot(t, g, True, False, False), t, False, True, False), jnp.zeros_like(t)


_tri_inv_given.defvjp(_tri_inv_fwd, _tri_inv_bwd)


def _rms(h, g):
    return h * lax.rsqrt(jnp.mean(h * h, axis=-1, keepdims=True) + EPS) * g


class In(NamedTuple):
    arr: jax.Array
    spec: pl.BlockSpec
    grad: object = None
    acc: bool = False
    gshape: object = None
    gspec: object = None
    galias: object = None


def _params(grid):
    sem = ("arbitrary",) * len(grid)
    return pltpu.CompilerParams(dimension_semantics=sem, vmem_limit_bytes=VMEM_LIMIT)


def _stage_fwd(name, fn, grid, ins, out_shapes, out_specs):
    n_in = len(ins)

    def body(*refs):
        pids = tuple(pl.program_id(k) for k in range(len(grid)))
        vals = [r[...].astype(F32) for r in refs[:n_in]]
        outs = fn(pids, *vals)
        for o_ref, o in zip(refs[n_in:], outs):
            o_ref[...] = o.reshape(o_ref.shape).astype(o_ref.dtype)

    return _pallas(
        body, out_shape=out_shapes, grid=grid, in_specs=[i.spec for i in ins],
        out_specs=out_specs, name=name, compiler_params=_params(grid))(*[i.arr for i in ins])


def _stage_bwd(name, fn, grid, ins, cots):
    n_in, n_ct = len(ins), len(cots)
    didx = [k for k, i in enumerate(ins) if i.grad is not None]
    aliased = [(o, ins[k].galias) for o, k in enumerate(didx) if ins[k].galias is not None]
    n_al = len(aliased)

    def body(*refs):
        pids = tuple(pl.program_id(k) for k in range(len(grid)))
        vals = [r[...].astype(F32) for r in refs[:n_in]]
        ct_refs = refs[n_in:n_in + n_ct]
        g_refs = refs[n_in + n_ct + n_al:]

        def f(*dv):
            merged = list(vals)
            for k, v in zip(didx, dv):
                merged[k] = v
            return tuple(fn(pids, *merged))

        outs, vjp_fn = jax.vjp(f, *[vals[k].astype(F32) for k in didx])
        cts = tuple(c[...].reshape(o.shape).astype(F32) for c, o in zip(ct_refs, outs))
        grads = vjp_fn(cts)
        first = functools.reduce(jnp.logical_and, [p == 0 for p in pids])
        for k, g_ref, g in zip(didx, g_refs, grads):
            if ins[k].acc:
                @pl.when(first)
                def _(g_ref=g_ref):
                    g_ref[...] = jnp.zeros(g_ref.shape, g_ref.dtype)
                g_ref[...] += g.reshape(g_ref.shape).astype(g_ref.dtype)
            else:
                g_ref[...] = g.reshape(g_ref.shape).astype(g_ref.dtype)

    out_shapes = [jax.ShapeDtypeStruct(ins[k].gshape or ins[k].arr.shape, ins[k].grad) for k in didx]
    out_specs = [ins[k].gspec or ins[k].spec for k in didx]
    return _pallas(
        body, out_shape=out_shapes, grid=grid,
        in_specs=[i.spec for i in ins] + [c[1] for c in cots] + [ANY] * n_al, out_specs=out_specs,
        input_output_aliases={n_in + n_ct + a: o for a, (o, _) in enumerate(aliased)},
        name=name, compiler_params=_params(grid))(*[i.arr for i in ins], *[c[0] for c in cots], *[a for _, a in aliased])


def _full(arr):
    nd = arr.ndim
    return pl.BlockSpec(arr.shape, lambda *p: (0,) * nd)


def _rows(tr, width, blk=0):
    return pl.BlockSpec((tr, width), lambda i: (i, blk))


def _mm(name, a, b, *, tm, tn, tk, out_dtype=F32, add=None, comm=None):
    M, K = a.shape
    N = b.shape[1] if b.ndim == 2 else b.shape[0] * b.shape[2]
    nk = K // tk
    grid = (M // tm, N // tn, nk)
    n_in = 3 if add is not None else 2
    n_ci, n_co = (len(comm.ins), len(comm.outs)) if comm is not None else (0, 0)

    def body(*refs):
        a_ref, b_ref = refs[0], refs[1]
        add_ref = refs[2] if add is not None else None
        c_ins = refs[n_in:n_in + n_ci]
        o_ref = refs[n_in + n_ci]
        c_outs = refs[n_in + n_ci + 1:n_in + n_ci + 1 + n_co]
        scratch = refs[n_in + n_ci + 1 + n_co:]
        acc_ref = scratch[0] if nk > 1 else None
        sems = scratch[1 if nk > 1 else 0:]
        step = (pl.program_id(0) * grid[1] + pl.program_id(1)) * nk + pl.program_id(2)
        if comm is not None:
            @pl.when(step == 0)
            def _():
                comm.start(c_ins, c_outs, sems)

        part = _raw_dot(a_ref[...], b_ref[...], False, False, False)

        def finish(total):
            if add_ref is not None:
                total = total + add_ref[...]
            o_ref[...] = total.astype(o_ref.dtype)

        if nk == 1:
            finish(part)
        else:
            k = pl.program_id(2)

            @pl.when(k == 0)
            def _():
                acc_ref[...] = part

            @pl.when(k > 0)
            def _():
                acc_ref[...] += part

            @pl.when(k == nk - 1)
            def _():
                finish(acc_ref[...])

        if comm is not None:
            @pl.when(step == grid[0] * grid[1] * nk - 1)
            def _():
                comm.finish(c_ins, c_outs, sems)

    b_spec = (pl.BlockSpec((tk, tn), lambda i, j, k: (k, j)) if b.ndim == 2 else
              pl.BlockSpec((None, tk, tn), lambda i, j, k: (j, k, 0)))
    in_specs = [pl.BlockSpec((tm, tk), lambda i, j, k: (i, k)), b_spec]
    args = [a, b]
    if add is not None:
        in_specs.append(pl.BlockSpec((tm, tn), lambda i, j, k: (i, j)))
        args.append(add)
    out_shape = jax.ShapeDtypeStruct((M, N), out_dtype)
    out_spec = pl.BlockSpec((tm, tn), lambda i, j, k: (i, j))
    scratch = [pltpu.VMEM((tm, tn), F32)] if nk > 1 else []
    if comm is None:
        return _pallas(body, out_shape=out_shape, grid=grid, in_specs=in_specs, out_specs=out_spec,
                       scratch_shapes=scratch, name=name, compiler_params=_params(grid))(*args)
    res = _pallas(body, out_shape=[out_shape] + comm.outs, grid=grid, in_specs=in_specs + [ANY] * n_ci,
                  out_specs=[out_spec] + [ANY] * n_co, scratch_shapes=scratch + comm.sems, name=name,
                  compiler_params=_params(grid))(*args, *comm.ins)
    return res[0], res[1:]


def _mm_tn(name, a, b, *, tr, tka, tn, blocked=False):
    R, Ka = a.shape
    N = b.shape[1]
    nr = R // tr
    grid = (Ka // tka, N // tn, nr)
    if blocked:
        out_shape = jax.ShapeDtypeStruct((N // tn, Ka, tn), F32)
        out_spec = pl.BlockSpec((None, tka, tn), lambda i, j, r: (j, i, 0))
    else:
        out_shape = jax.ShapeDtypeStruct((Ka, N), F32)
        out_spec = pl.BlockSpec((tka, tn), lambda i, j, r: (i, j))

    def body(a_ref, b_ref, o_ref):
        r = pl.program_id(2)
        part = _raw_dot(a_ref[...], b_ref[...], True, False, False)

        @pl.when(r == 0)
        def _():
            o_ref[...] = part

        @pl.when(r > 0)
        def _():
            o_ref[...] += part

    return _pallas(
        body, out_shape=out_shape, grid=grid,
        in_specs=[pl.BlockSpec((tr, tka), lambda i, j, r: (r, i)),
                  pl.BlockSpec((tr, tn), lambda i, j, r: (r, j))],
        out_specs=out_spec, name=name, compiler_params=_params(grid))(a, b)


def _conv_fwd(name, x, xcol0, w, b, *, taps, width, tr, tc):
    R = x.shape[0]
    grid = (width // tc, R // tr)
    cb0 = xcol0 // tc
    hrows = 16 if x.dtype == BF16 else 8
    hb = tr // hrows

    def body(*refs):
        x_ref, xp_ref, w_ref = refs[:3]
        b_ref = refs[3] if b is not None else None
        o_ref = refs[-1]
        i = pl.program_id(1)
        xv = x_ref[...].astype(F32)
        prev = jnp.where(i > 0, xp_ref[...].astype(F32)[hrows - 8:, :], 0.0)
        ext = jnp.concatenate([prev, xv], axis=0)
        acc = xv * w_ref[taps - 1:taps, :]
        for s in range(1, taps):
            acc = acc + pltpu.roll(ext, s, 0)[8:, :] * w_ref[taps - 1 - s:taps - s, :]
        if b_ref is not None:
            acc = acc + b_ref[...]
        o_ref[...] = acc.astype(o_ref.dtype)

    in_specs = [pl.BlockSpec((tr, tc), lambda j, i: (i, cb0 + j)),
                pl.BlockSpec((hrows, tc), lambda j, i: (jnp.maximum(i * hb - 1, 0), cb0 + j)),
                pl.BlockSpec((taps, tc), lambda j, i: (0, j))]
    args = [x, x, w]
    if b is not None:
        in_specs.append(pl.BlockSpec((1, tc), lambda j, i: (0, j)))
        args.append(b)
    return _pallas(
        body, out_shape=jax.ShapeDtypeStruct((R, width), BF16), grid=grid, in_specs=in_specs,
        out_specs=pl.BlockSpec((tr, tc), lambda j, i: (i, j)),
        name=name, compiler_params=_params(grid))(*args)


def _conv_bwd(name, x, xcol0, w, dy, *, taps, width, tr, tc, with_bias, dx_into=None):
    R = x.shape[0]
    nr = R // tr
    grid = (width // tc, nr)
    cb0 = xcol0 // tc
    hrows = 16 if dy.dtype == BF16 else 8
    hb = tr // hrows
    n_ext = tr + 8
    n_al = 0 if dx_into is None else 1

    def body(*refs):
        x_ref, w_ref, dy_ref, dyn_ref = refs[:4]
        dx_ref, dw_ref = refs[4 + n_al], refs[5 + n_al]
        db_ref = refs[6 + n_al] if with_bias else None
        i = pl.program_id(1)
        xv = x_ref[...].astype(F32)
        dyv = dy_ref[...].astype(F32)
        nxt = dyn_ref[...].astype(F32)[:8, :]
        dext = jnp.concatenate([dyv, jnp.where(i < nr - 1, nxt, 0.0)], axis=0)
        dx = dyv * w_ref[taps - 1:taps, :]
        dws = [None] * taps
        dws[taps - 1] = jnp.sum(xv * dyv, axis=0, keepdims=True)
        for s in range(1, taps):
            ahead = pltpu.roll(dext, n_ext - s, 0)[:tr, :]
            dx = dx + ahead * w_ref[taps - 1 - s:taps - s, :]
            dws[taps - 1 - s] = jnp.sum(xv * ahead, axis=0, keepdims=True)
        dx_ref[...] = dx.astype(dx_ref.dtype)

        @pl.when(i == 0)
        def _():
            for k in range(taps):
                dw_ref[k:k + 1, :] = dws[k]
            if db_ref is not None:
                db_ref[...] = jnp.sum(dyv, axis=0, keepdims=True)

        @pl.when(i > 0)
        def _():
            for k in range(taps):
                dw_ref[k:k + 1, :] += dws[k]
            if db_ref is not None:
                db_ref[...] += jnp.sum(dyv, axis=0, keepdims=True)

    in_specs = [pl.BlockSpec((tr, tc), lambda j, i: (i, cb0 + j)),
                pl.BlockSpec((taps, tc), lambda j, i: (0, j)),
                pl.BlockSpec((tr, tc), lambda j, i: (i, j)),
                pl.BlockSpec((hrows, tc), lambda j, i: (jnp.minimum((i + 1) * hb, R // hrows - 1), j))]
    args = [x, w, dy, dy]
    if dx_into is None:
        dx_shape, dx_spec, aliases = jax.ShapeDtypeStruct((R, width), BF16), pl.BlockSpec((tr, tc), lambda j, i: (i, j)), {}
    else:
        dx_shape = jax.ShapeDtypeStruct(dx_into.shape, dx_into.dtype)
        dx_spec, aliases = pl.BlockSpec((tr, tc), lambda j, i: (i, cb0 + j)), {4: 0}
        in_specs.append(ANY)
        args.append(dx_into)
    out_shape = [dx_shape, jax.ShapeDtypeStruct((taps, width), F32)]
    out_specs = [dx_spec, pl.BlockSpec((taps, tc), lambda j, i: (0, j))]
    if with_bias:
        out_shape.append(jax.ShapeDtypeStruct((1, width), F32))
        out_specs.append(pl.BlockSpec((1, tc), lambda j, i: (0, j)))
    return _pallas(
        body, out_shape=out_shape, grid=grid, in_specs=in_specs, out_specs=out_specs, input_output_aliases=aliases,
        name=name, compiler_params=_params(grid))(*args)


def _row_mask(cfg, i, tr):
    rows = i * tr + lax.broadcasted_iota(jnp.int32, (tr, 1), 0)
    return (rows >= cfg.front).astype(F32)


def _make_rms_fn(cfg, tr, with_residual):
    def fn(pids, h, g):
        hm = h * _row_mask(cfg, pids[0], tr)
        if with_residual:
            return _rms(hm, g), hm
        return (_rms(hm, g),)
    return fn


def _make_gdn_prep_fn(cfg, tr):
    d, hg = cfg.d, cfg.hg

    def fn(pids, c, tail, alog, dtb):
        cq, ck, cv = c[:, :d], c[:, d:2 * d], c[:, 2 * d:]
        mask = _row_mask(cfg, pids[0], tr)
        j, col = _iota2(LANES, d, 0), _iota2(LANES, d, 1)
        ea = ((col >> 7) == j).astype(F32)
        eb = ((col >> 7) + hg == j).astype(F32)
        al = jnp.sum(alog, axis=0, keepdims=True)
        db = jnp.sum(dtb, axis=0, keepdims=True)
        lg = _dot_sel(-jnp.exp(al) * _softplus(tail + db) * mask, ea, False)
        beta = _dot_sel(_sigmoid(tail) * mask, eb, False)
        sq, sk, sv = _silu(cq), _silu(ck), _silu(cv)
        qs, ks = [], []
        for h in range(hg):
            sl = slice(h * GDN_DK, (h + 1) * GDN_DK)
            qh, kh = sq[:, sl], sk[:, sl]
            qs.append(qh * lax.rsqrt(jnp.sum(qh * qh, axis=-1, keepdims=True) + EPS) * (GDN_DK ** -0.5))
            ks.append(kh * lax.rsqrt(jnp.sum(kh * kh, axis=-1, keepdims=True) + EPS))
        return jnp.concatenate(qs, axis=1), jnp.concatenate(ks, axis=1), sv, beta, lg
    return fn


def _gdn_intra_fn(pids, q, k, v, bB, lB, t_saved=None):
    rows = q.shape[0]
    nb = rows // CHUNK
    q3, k3, v3, b3, l3 = [t.reshape(nb, CHUNK, GDN_DK) for t in (q, k, v, bB, lB)]
    r, c = _iota2(CHUNK, CHUNK, 0), _iota2(CHUNK, CHUNK, 1)
    tril = (r >= c)
    strict = (r > c)
    gcol = _sel_dot(_bcast(tril.astype(F32), nb), l3)
    grow = jnp.swapaxes(gcol, 1, 2)[:, :CHUNK, :]
    diff = gcol[:, :, :CHUNK] - grow
    decay = jnp.where(tril[None], jnp.exp(jnp.where(tril[None], diff, 0.0)), 0.0)
    kb = k3 * b3
    m = jnp.where(strict[None], _dot(kb, k3, False, True) * decay, 0.0)
    t = _tri_inv_raw(m) if t_saved is None else _tri_inv_given(m, t_saved.reshape(nb, CHUNK, CHUNK))
    eg = jnp.exp(gcol)
    u = _dot(t, v3 * b3)
    w = _dot(t, kb * eg)
    attn = _dot(q3, k3, False, True) * decay
    qd = q3 * eg
    glast = jnp.sum(l3, axis=1, keepdims=True)
    kd = k3 * jnp.exp(glast - gcol)
    gl = jnp.exp(glast)
    outs = (u.reshape(rows, GDN_DK), w.reshape(rows, GDN_DK), attn.reshape(1, rows, CHUNK),
            qd.reshape(rows, GDN_DK), kd.reshape(rows, GDN_DK), gl.reshape(1, nb, 1, GDN_DK))
    return outs + (t.reshape(1, rows, CHUNK),) if t_saved is None else outs


def _make_rot_fn(cfg):
    hr = cfg.hr
    half = RET_DK // 2

    def fn(pids, rqk, cos, sin):
        rq, rk = rqk[:, :cfg.d], rqk[:, cfg.d:]

        def rot(t, scale):
            outs = []
            for h in range(hr):
                x1 = t[:, h * RET_DK:h * RET_DK + half]
                x2 = t[:, h * RET_DK + half:(h + 1) * RET_DK]
                outs += [(x1 * cos - x2 * sin) * scale, (x2 * cos + x1 * sin) * scale]
            return jnp.concatenate(outs, axis=1)
        return rot(rq, 1.0), rot(rk, RET_DK ** -0.5)
    return fn


def _make_mix_fn(cfg):
    hg, hr = cfg.hg, cfg.hr

    def fn(pids, oa, ob, pm, gnorm):
        d = cfg.d
        gz, rg, gate_a, gate_b = pm[:, :d], pm[:, d:2 * d], pm[:, 2 * d:3 * d], pm[:, 3 * d:]
        oas = []
        for h in range(hg):
            oh = oa[:, h * GDN_DK:(h + 1) * GDN_DK]
            oas.append(oh * lax.rsqrt(jnp.mean(oh * oh, axis=-1, keepdims=True) + EPS) * gnorm)
        ya = jnp.concatenate(oas, axis=1) * _silu(gz)
        obs = []
        for h in range(hr):
            oh = ob[:, h * RET_DK:(h + 1) * RET_DK]
            obs.append(oh * lax.rsqrt(jnp.mean(oh * oh, axis=-1, keepdims=True) + EPS))
        yb = _silu(rg) * jnp.concatenate(obs, axis=1)
        return (_sigmoid(gate_a) * ya + _sigmoid(gate_b) * yb,)
    return fn


def _act_fn(pids, u):
    f = u.shape[1] // 2
    return (_silu(u[:, :f]) * u[:, f:],)


def _gdn_step(s, u, w, a, qd, kd, gl):
    top = _dot(jnp.concatenate([w, qd], axis=0), s)
    v_new = u - top[:CHUNK]
    bot = _dot(jnp.concatenate([a, kd.T], axis=0), v_new)
    o = top[CHUNK:] + bot[:CHUNK]
    s2 = s * gl + bot[CHUNK:]
    return s2, o


def _gdn_step_bwd(s, u, w, a, qd, kd, gl, ds2, do):
    lw = jnp.concatenate([w, qd], axis=0)
    v_new = u - _raw_dot(w, s, False, False, False)
    dv = _raw_dot(a, do, True, False, False) + _raw_dot(kd, ds2, False, False, False)
    da = _raw_dot(do, v_new, False, True, False)
    dkd = _raw_dot(v_new, ds2, False, True, False)
    dtop = jnp.concatenate([-dv, do], axis=0)
    dlw = _raw_dot(dtop, s, False, True, False)
    ds = ds2 * gl + _raw_dot(lw, dtop, True, False, False)
    dgl = jnp.sum(ds2 * s, axis=0, keepdims=True)
    return ds, dv, dlw[:CHUNK], da, dlw[CHUNK:], dkd, dgl


def _ret_step(s, q, k, v, dm, qdc, kdc, g):
    att = _dot(q, k, False, True) * dm
    bot = _dot(jnp.concatenate([att, (k * kdc).T], axis=0), v)
    o = bot[:CHUNK] + _dot(q * qdc, s)
    s2 = s * g + bot[CHUNK:]
    return s2, o


class Part(NamedTuple):
    body: object
    args: list
    in_specs: list
    out_shape: list
    out_specs: list
    scratch: list
    aliases: dict = {}


def _run_parts(name, grid, parts):
    n_in = [len(p.args) for p in parts]
    n_out = [len(p.out_shape) for p in parts]
    n_sc = [len(p.scratch) for p in parts]
    off_in = [sum(n_in[:k]) for k in range(len(parts))]
    off_out = [sum(n_out[:k]) for k in range(len(parts))]
    off_sc = [sum(n_sc[:k]) for k in range(len(parts))]

    def body(*refs):
        ins, outs, scr = refs[:sum(n_in)], refs[sum(n_in):sum(n_in) + sum(n_out)], refs[sum(n_in) + sum(n_out):]
        for k, p in enumerate(parts):
            p.body(*ins[off_in[k]:off_in[k] + n_in[k]], *outs[off_out[k]:off_out[k] + n_out[k]],
                   *scr[off_sc[k]:off_sc[k] + n_sc[k]])

    aliases = {off_in[k] + i: off_out[k] + o for k, p in enumerate(parts) for i, o in p.aliases.items()}
    res = _pallas(
        body, out_shape=sum((p.out_shape for p in parts), []), grid=grid, in_specs=sum((p.in_specs for p in parts), []),
        out_specs=sum((p.out_specs for p in parts), []), scratch_shapes=sum((p.scratch for p in parts), []),
        input_output_aliases=aliases, name=name, compiler_params=_params(grid))(*sum((p.args for p in parts), []))
    return [res[off_out[k]:off_out[k] + n_out[k]] for k in range(len(parts))]


def _gdn_scan_fwd(cfg, u, w, attn, qd, kd, gl):
    d, hg, nch, sc = cfg.d, cfg.hg, cfg.nch, cfg.sc
    nst = nch // sc

    def body(u_ref, w_ref, a_ref, qd_ref, kd_ref, gl_ref, o_ref, ss_ref, s_ref):
        @pl.when(pl.program_id(0) == 0)
        def _():
            s_ref[...] = jnp.zeros(s_ref.shape, F32)

        states = [s_ref[h] for h in range(hg)]
        for j in range(sc):
            rows = slice(j * CHUNK, (j + 1) * CHUNK)
            outs = []
            for h in range(hg):
                sl = slice(h * GDN_DK, (h + 1) * GDN_DK)
                ss_ref[j, h] = states[h]
                states[h], o = _gdn_step(states[h], u_ref[rows, sl], w_ref[rows, sl], a_ref[h, rows, :],
                                         qd_ref[rows, sl], kd_ref[rows, sl], gl_ref[h, j])
                outs.append(o)
            o_ref[rows, :] = jnp.concatenate(outs, axis=1)
        for h in range(hg):
            s_ref[h] = states[h]

    row = pl.BlockSpec((sc * CHUNK, d), lambda n: (n, 0))
    return Part(
        body, [u, w, attn, qd, kd, gl],
        [row, row, pl.BlockSpec((hg, sc * CHUNK, CHUNK), lambda n: (0, n, 0)), row, row,
         pl.BlockSpec((hg, sc, 1, GDN_DK), lambda n: (0, n, 0, 0))],
        [jax.ShapeDtypeStruct((cfg.rp, d), F32), jax.ShapeDtypeStruct((nch, hg, GDN_DK, GDN_DK), F32)],
        [row, pl.BlockSpec((sc, hg, GDN_DK, GDN_DK), lambda n: (n, 0, 0, 0))],
        [pltpu.VMEM((hg, GDN_DK, GDN_DK), F32)])


def _gdn_scan_bwd(cfg, do, u, w, attn, qd, kd, gl, ss):
    d, hg, nch, sc = cfg.d, cfg.hg, cfg.nch, cfg.sc
    nst = nch // sc

    def body(do_ref, u_ref, w_ref, a_ref, qd_ref, kd_ref, gl_ref, ss_ref,
             du_ref, dw_ref, da_ref, dqd_ref, dkd_ref, dgl_ref, ds_ref):
        @pl.when(pl.program_id(0) == 0)
        def _():
            ds_ref[...] = jnp.zeros(ds_ref.shape, F32)

        dstates = [ds_ref[h] for h in range(hg)]
        for j in reversed(range(sc)):
            rows = slice(j * CHUNK, (j + 1) * CHUNK)
            dus, dws, dqds, dkds = [], [], [], []
            for h in range(hg):
                sl = slice(h * GDN_DK, (h + 1) * GDN_DK)
                args = (ss_ref[j, h], u_ref[rows, sl], w_ref[rows, sl], a_ref[h, rows, :], qd_ref[rows, sl],
                        kd_ref[rows, sl], gl_ref[h, j])
                dstates[h], du, dw, da, dqd, dkd, dgl = _gdn_step_bwd(*args, dstates[h], do_ref[rows, sl])
                da_ref[h, rows, :] = da
                dgl_ref[h, j] = dgl
                dus.append(du)
                dws.append(dw)
                dqds.append(dqd)
                dkds.append(dkd)
            du_ref[rows, :] = jnp.concatenate(dus, axis=1)
            dw_ref[rows, :] = jnp.concatenate(dws, axis=1)
            dqd_ref[rows, :] = jnp.concatenate(dqds, axis=1)
            dkd_ref[rows, :] = jnp.concatenate(dkds, axis=1)
        for h in range(hg):
            ds_ref[h] = dstates[h]

    row = pl.BlockSpec((sc * CHUNK, d), lambda n: (nst - 1 - n, 0))
    aspec = pl.BlockSpec((hg, sc * CHUNK, CHUNK), lambda n: (0, nst - 1 - n, 0))
    gspec = pl.BlockSpec((hg, sc, 1, GDN_DK), lambda n: (0, nst - 1 - n, 0, 0))
    rowshape = jax.ShapeDtypeStruct((cfg.rp, d), F32)
    return Part(
        body, [do, u, w, attn, qd, kd, gl, ss],
        [row, row, row, aspec, row, row, gspec, pl.BlockSpec((sc, hg, GDN_DK, GDN_DK), lambda n: (nst - 1 - n, 0, 0, 0))],
        [rowshape, rowshape, jax.ShapeDtypeStruct(attn.shape, F32), rowshape, rowshape, jax.ShapeDtypeStruct(gl.shape, F32)],
        [row, row, aspec, row, row, gspec],
        [pltpu.VMEM((hg, GDN_DK, GDN_DK), F32)])


def _ret_consts(cfg):
    hr = cfg.hr
    lg = np.log(1.0 - 2.0 ** (-5.0 - np.arange(hr, dtype=np.float64)))
    idx = np.arange(CHUNK, dtype=np.float64)
    tril = np.tril(np.ones((CHUNK, CHUNK), dtype=bool))
    dm = np.where(tril[None], np.exp((idx[:, None] - idx[None, :])[None] * lg[:, None, None]), 0.0)
    qdc = np.exp((idx[None, :] + 1.0) * lg[:, None])
    kdc = np.exp((CHUNK - 1.0 - idx[None, :]) * lg[:, None])
    gch = np.exp(CHUNK * lg)
    qdc = np.broadcast_to(qdc[:, :, None], (hr, CHUNK, RET_DK))
    kdc = np.broadcast_to(kdc[:, :, None], (hr, CHUNK, RET_DK))
    gch = np.broadcast_to(gch[:, None, None], (hr, 1, RET_DK))
    return tuple(jnp.asarray(np.ascontiguousarray(t), F32) for t in (dm, qdc, kdc, gch))


def _ret_scan_fwd(cfg, qr, kr, proj, consts):
    d, hr, nch, sc = cfg.d, cfg.hr, cfg.nch, cfg.sc
    nst = nch // sc
    dm, qdc, kdc, gch = consts

    def body(q_ref, k_ref, v_ref, dm_ref, qdc_ref, kdc_ref, g_ref, o_ref, ss_ref, s_ref):
        @pl.when(pl.program_id(0) == 0)
        def _():
            s_ref[...] = jnp.zeros(s_ref.shape, F32)

        states = [s_ref[h] for h in range(hr)]
        for j in range(sc):
            rows = slice(j * CHUNK, (j + 1) * CHUNK)
            outs = []
            for h in range(hr):
                sl = slice(h * RET_DK, (h + 1) * RET_DK)
                ss_ref[j, h] = states[h]
                states[h], o = _ret_step(states[h], q_ref[rows, sl], k_ref[rows, sl], v_ref[rows, sl], dm_ref[h],
                                         qdc_ref[h], kdc_ref[h], g_ref[h])
                outs.append(o)
            o_ref[rows, :] = jnp.concatenate(outs, axis=1)
        for h in range(hr):
            s_ref[h] = states[h]

    row = pl.BlockSpec((sc * CHUNK, d), lambda n: (n, 0))
    return Part(
        body, [qr, kr, proj, dm, qdc, kdc, gch],
        [row, row, pl.BlockSpec((sc * CHUNK, d), lambda n: (n, RV_BLOCK)), _full(dm), _full(qdc), _full(kdc), _full(gch)],
        [jax.ShapeDtypeStruct((cfg.rp, d), F32), jax.ShapeDtypeStruct((nch, hr, RET_DK, RET_DK), F32)],
        [row, pl.BlockSpec((sc, hr, RET_DK, RET_DK), lambda n: (n, 0, 0, 0))],
        [pltpu.VMEM((hr, RET_DK, RET_DK), F32)])


def _ret_scan_bwd(cfg, do, qr, kr, proj, consts, ss, dproj):
    d, hr, nch, sc = cfg.d, cfg.hr, cfg.nch, cfg.sc
    nst = nch // sc
    dm, qdc, kdc, gch = consts

    def body(do_ref, q_ref, k_ref, v_ref, dm_ref, qdc_ref, kdc_ref, g_ref, ss_ref, _, dq_ref, dk_ref, dv_ref, ds_ref):
        @pl.when(pl.program_id(0) == 0)
        def _():
            ds_ref[...] = jnp.zeros(ds_ref.shape, F32)

        dstates = [ds_ref[h] for h in range(hr)]
        for j in reversed(range(sc)):
            rows = slice(j * CHUNK, (j + 1) * CHUNK)
            dqs, dks, dvs = [], [], []
            for h in range(hr):
                sl = slice(h * RET_DK, (h + 1) * RET_DK)
                cs = (dm_ref[h], qdc_ref[h], kdc_ref[h], g_ref[h])
                _, vjp_fn = jax.vjp(lambda s, q, k, v, cs=cs: _ret_step(s, q, k, v, *cs),
                                    ss_ref[j, h], q_ref[rows, sl], k_ref[rows, sl], v_ref[rows, sl])
                dstates[h], dq, dk, dv = vjp_fn((dstates[h], do_ref[rows, sl]))
                dqs.append(dq)
                dks.append(dk)
                dvs.append(dv)
            dq_ref[rows, :] = jnp.concatenate(dqs, axis=1)
            dk_ref[rows, :] = jnp.concatenate(dks, axis=1)
            dv_ref[rows, :] = jnp.concatenate(dvs, axis=1).astype(dv_ref.dtype)
        for h in range(hr):
            ds_ref[h] = dstates[h]

    row = pl.BlockSpec((sc * CHUNK, d), lambda n: (nst - 1 - n, 0))
    rowshape = jax.ShapeDtypeStruct((cfg.rp, d), F32)
    vspec = pl.BlockSpec((sc * CHUNK, d), lambda n: (nst - 1 - n, RV_BLOCK))
    return Part(
        body, [do, qr, kr, proj, dm, qdc, kdc, gch, ss, dproj],
        [row, row, row, vspec, _full(dm), _full(qdc), _full(kdc), _full(gch),
         pl.BlockSpec((sc, hr, RET_DK, RET_DK), lambda n: (nst - 1 - n, 0, 0, 0)), ANY],
        [rowshape, rowshape, jax.ShapeDtypeStruct(dproj.shape, dproj.dtype)],
        [row, row, vspec],
        [pltpu.VMEM((hr, RET_DK, RET_DK), F32)], {9: 2})


def _final(cfg, h2, normf, tgt):
    d, tr = cfg.d, cfg.xrow
    nr = cfg.rp // tr

    def body(h_ref, g_ref, t_ref, dh_ref, dg_ref, loss_ref):
        i = pl.program_id(0)
        y, vjp_fn = jax.vjp(_rms, h_ref[...], g_ref[...])
        err = jnp.where(i >= 1, y - t_ref[...], 0.0)
        dh, dg = vjp_fn(err * (1.0 / d))
        dh_ref[...] = dh
        part = jnp.zeros((8, LANES), F32) + 0.5 * jnp.sum(err * err) * (1.0 / d)

        @pl.when(i == 0)
        def _():
            dg_ref[...] = dg
            loss_ref[...] = part

        @pl.when(i > 0)
        def _():
            dg_ref[...] += dg
            loss_ref[...] += part

    return _pallas(
        body,
        out_shape=[jax.ShapeDtypeStruct((cfg.rp, d), F32), jax.ShapeDtypeStruct((1, d), F32),
                   jax.ShapeDtypeStruct((8, LANES), F32)],
        grid=(nr,),
        in_specs=[_rows(tr, d), _full(normf), pl.BlockSpec((tr, d), lambda i: (jnp.maximum(i - 1, 0), 0))],
        out_specs=[_rows(tr, d), pl.BlockSpec((1, d), lambda i: (0, 0)), pl.BlockSpec((8, LANES), lambda i: (0, 0))],
        name="final_loss", compiler_params=_params((nr,)))(h2, normf, tgt)


ANY = pl.BlockSpec(memory_space=pl.ANY)


def _place():
    x, y, c = lax.axis_index("x"), lax.axis_index("y"), lax.axis_index("c")
    others = [(1 - x, y), (x, 1 - y), (1 - x, 1 - y)]
    return x, y, c, others


def _row_tile(rows, cap=256):
    return max(t for t in range(16, min(rows, cap) + 1, 16) if rows % t == 0)


class Comm(NamedTuple):
    ins: list
    outs: list
    sems: list
    start: object
    finish: object


def _run_comm(name, comm):
    n_in, n_out = len(comm.ins), len(comm.outs)

    def body(*refs):
        ins, outs, sems = refs[:n_in], refs[n_in:n_in + n_out], refs[n_in + n_out:]
        comm.start(ins, outs, sems)
        comm.finish(ins, outs, sems)

    return _pallas(body, out_shape=comm.outs, in_specs=[ANY] * n_in, out_specs=[ANY] * n_out,
                   scratch_shapes=comm.sems, name=name)(*comm.ins)


def _gather_comm(ws):
    n = len(ws)
    halves = [w.shape[0] // 2 for w in ws]

    def copies(w_refs, o_refs, sems):
        send_sems, recv_sems = sems
        x, y, c, others = _place()
        me = 2 * x + y
        chips = [2 * px + py for px, py in others]

        def piece(a, chip, core):
            return o_refs[a].at[chip, pl.ds(core * halves[a], halves[a]), :]

        def copy(a, k, src, chip, core, to):
            return pltpu.make_async_remote_copy(src_ref=src, dst_ref=piece(a, chip, core), send_sem=send_sems.at[6 * a + k],
                                                recv_sem=recv_sems.at[6 * a + k], device_id=to, device_id_type=MESH)

        def first(j, a):
            return copy(a, j, w_refs[a].at[pl.ds(c * halves[a], halves[a]), :], me, c, (*others[j], c))

        def landed(j, a):
            return copy(a, j, piece(a, chips[j], c), chips[j], c, (x, y, c))

        def passed(j, a):
            return copy(a, 3 + j, piece(a, chips[j], c), chips[j], c, (x, y, 1 - c))

        def from_sibling(j, a):
            return copy(a, 3 + j, piece(a, chips[j], 1 - c), chips[j], 1 - c, (x, y, c))

        return first, landed, passed, from_sibling

    pairs = [(j, a) for j in range(3) for a in range(n)]

    def start(w_refs, o_refs, sems):
        first, _, _, _ = copies(w_refs, o_refs, sems)
        for j, a in pairs:
            first(j, a).start()

    def finish(w_refs, o_refs, sems):
        first, landed, passed, from_sibling = copies(w_refs, o_refs, sems)
        for j, a in pairs:
            landed(j, a).wait_recv()
            passed(j, a).start()
        for j, a in pairs:
            from_sibling(j, a).wait_recv()
        for j, a in pairs:
            first(j, a).wait_send()
            passed(j, a).wait_send()

    return Comm(list(ws), [jax.ShapeDtypeStruct((N_CHIPS,) + w.shape, w.dtype) for w in ws],
                [pltpu.SemaphoreType.DMA((6 * n,)), pltpu.SemaphoreType.DMA((6 * n,))], start, finish)


def _pair_exchange(name, gs):
    n = len(gs)

    def body(*refs):
        g_refs, o_refs = refs[:n], refs[n:2 * n]
        send_sems, recv_sems = refs[2 * n:]
        x, y, c, _ = _place()
        cps = []
        for a in range(n):
            half = gs[a].shape[1] // 2
            cp = pltpu.make_async_remote_copy(
                src_ref=g_refs[a].at[:, pl.ds((1 - c) * half, half), :], dst_ref=o_refs[a], send_sem=send_sems.at[a],
                recv_sem=recv_sems.at[a], device_id=(x, y, 1 - c), device_id_type=MESH)
            cp.start()
            cps.append(cp)
        for cp in cps:
            cp.wait()

    return _pallas(
        body, out_shape=[jax.ShapeDtypeStruct((N_CHIPS, g.shape[1] // 2, g.shape[2]), g.dtype) for g in gs],
        in_specs=[ANY] * n, out_specs=[ANY] * n,
        scratch_shapes=[pltpu.SemaphoreType.DMA((n,)), pltpu.SemaphoreType.DMA((n,))], name=name)(*gs)


def _pair_sum(name, g, recv, cidx):
    half, cols = recv.shape[1], recv.shape[2]
    tr = _row_tile(half)
    nblk = half // tr

    def body(c_ref, g_ref, r_ref, o_ref):
        o_ref[...] = (g_ref[...] + r_ref[...]).astype(o_ref.dtype)

    grid_spec = pltpu.PrefetchScalarGridSpec(
        num_scalar_prefetch=1, grid=(N_CHIPS, nblk),
        in_specs=[pl.BlockSpec((1, tr, cols), lambda s, i, c: (s, c[0] * nblk + i, 0)),
                  pl.BlockSpec((1, tr, cols), lambda s, i, c: (s, i, 0))],
        out_specs=pl.BlockSpec((1, tr, cols), lambda s, i, c: (s, i, 0)))
    return _pallas(
        body, out_shape=jax.ShapeDtypeStruct((N_CHIPS, half, cols), BF16), grid_spec=grid_spec,
        name=name, compiler_params=_params((N_CHIPS, nblk)))(cidx, g, recv)


def _exchange_comm(parts):
    n = len(parts)

    def copies(p_refs, o_refs, sems):
        send_sems, recv_sems = sems
        x, y, c, others = _place()
        me = 2 * x + y

        def copy(a, j, src_chip, dst_chip):
            px, py = others[j]
            return pltpu.make_async_remote_copy(
                src_ref=p_refs[a].at[src_chip], dst_ref=o_refs[a].at[dst_chip], send_sem=send_sems.at[3 * a + j],
                recv_sem=recv_sems.at[3 * a + j], device_id=(px, py, c), device_id_type=MESH)

        def send(j, a):
            return copy(a, j, 2 * others[j][0] + others[j][1], me)

        def arrival(j, a):
            return copy(a, j, me, 2 * others[j][0] + others[j][1])

        return send, arrival

    pairs = [(j, a) for j in range(3) for a in range(n)]

    def start(p_refs, o_refs, sems):
        send, _ = copies(p_refs, o_refs, sems)
        for j, a in pairs:
            send(j, a).start()

    def finish(p_refs, o_refs, sems):
        send, arrival = copies(p_refs, o_refs, sems)
        for j, a in pairs:
            arrival(j, a).wait_recv()
        for j, a in pairs:
            send(j, a).wait_send()

    return Comm(list(parts), [jax.ShapeDtypeStruct(p.shape, p.dtype) for p in parts],
                [pltpu.SemaphoreType.DMA((3 * n,)), pltpu.SemaphoreType.DMA((3 * n,))], start, finish)


def _chip_sum(name, part, slots, chip):
    half, cols = slots.shape[1], slots.shape[2]
    tr = _row_tile(half)

    def body(me_ref, p_ref, *rest):
        s_refs, o_ref = rest[:N_CHIPS], rest[N_CHIPS]
        own = p_ref[...].astype(F32)
        v = [jnp.where(me_ref[0] == k, own, s_refs[k][...].astype(F32)) for k in range(N_CHIPS)]
        o_ref[...] = ((v[0] + v[1]) + v[2]) + v[3]

    def slot_spec(k):
        return pl.BlockSpec((None, tr, cols), lambda i, me: (jnp.where(me[0] == k, (k + 1) % N_CHIPS, k), i, 0))

    grid_spec = pltpu.PrefetchScalarGridSpec(
        num_scalar_prefetch=1, grid=(half // tr,),
        in_specs=[pl.BlockSpec((None, tr, cols), lambda i, me: (me[0], i, 0))] + [slot_spec(k) for k in range(N_CHIPS)],
        out_specs=pl.BlockSpec((tr, cols), lambda i, me: (i, 0)))
    return _pallas(
        body, out_shape=jax.ShapeDtypeStruct((half, cols), F32), grid_spec=grid_spec,
        name=name, compiler_params=_params((half // tr,)))(chip, part, *([slots] * N_CHIPS))


def _pair_swap(fins):
    n = len(fins)

    def body(*refs):
        f_refs, o_refs = refs[:n], refs[n:2 * n]
        send_sems, recv_sems = refs[2 * n:]
        x, y, c, _ = _place()
        cps = [pltpu.make_async_remote_copy(src_ref=f_refs[a], dst_ref=o_refs[a], send_sem=send_sems.at[a],
                                            recv_sem=recv_sems.at[a], device_id=(x, y, 1 - c), device_id_type=MESH)
               for a in range(n)]
        for cp in cps:
            cp.start()
        for cp in cps:
            cp.wait()

    return _pallas(
        body, out_shape=[jax.ShapeDtypeStruct(f.shape, f.dtype) for f in fins], in_specs=[ANY] * n, out_specs=[ANY] * n,
        scratch_shapes=[pltpu.SemaphoreType.DMA((n,)), pltpu.SemaphoreType.DMA((n,))], name="grad_pair_swap")(*fins)


def _adamw(name, w, g_own, g_other, m, v, cidx):
    R, cols = w.shape[-2:]
    lead = (None,) * (w.ndim - 2)
    zeros = (0,) * (w.ndim - 2)
    half = R // 2
    tr = _row_tile(half, 128)
    nblk = half // tr
    c1 = 1.0 - ADAM_B1 ** ADAM_STEP
    c2 = 1.0 - ADAM_B2 ** ADAM_STEP

    def body(c_ref, w_ref, go_ref, gs_ref, m_ref, v_ref, g_ref, d_ref, nm_ref, nv_ref):
        mine = (pl.program_id(0) // nblk) == c_ref[0]
        gv = jnp.where(mine, go_ref[...], gs_ref[...])
        nm = ADAM_B1 * m_ref[...] + (1.0 - ADAM_B1) * gv
        nv = ADAM_B2 * v_ref[...] + (1.0 - ADAM_B2) * (gv * gv)
        g_ref[...] = gv
        d_ref[...] = -ADAM_LR * ((nm / c1) / (jnp.sqrt(nv / c2) + ADAM_EPS) + ADAM_WD * w_ref[...])
        nm_ref[...] = nm
        nv_ref[...] = nv

    spec = pl.BlockSpec(lead + (tr, cols), lambda i, c: zeros + (i, 0))
    hspec = pl.BlockSpec((tr, cols), lambda i, c: (i % nblk, 0))
    shape = jax.ShapeDtypeStruct(w.shape, F32)
    grid_spec = pltpu.PrefetchScalarGridSpec(num_scalar_prefetch=1, grid=(R // tr,),
                                             in_specs=[spec, hspec, hspec, spec, spec], out_specs=[spec] * 4)
    return _pallas(
        body, out_shape=[shape] * 4, grid_spec=grid_spec,
        name=name, compiler_params=_params((R // tr,)))(cidx, w, g_own, g_other, m, v)


PARAMS = (("meta", 1), ("norm1", None), ("w_in", 2), ("gdn_conv_w", 2), ("gdn_a_log", None), ("gdn_dt_bias", None),
          ("gdn_norm", None), ("w_out", 1), ("norm2", None), ("w_ffn_up", 2), ("ffn_conv_w", 2), ("ffn_conv_b", None),
          ("w_ffn_down", 1), ("norm_f", None))
BIG = ("w_in", "w_out", "w_ffn_up", "w_ffn_down")
PACK_ALIGN = 1024
PACK_ROWS_ALIGN = 32


def _pack(arrs, dtype):
    parts, total = [], 0
    for a in arrs:
        f = a.reshape(-1).astype(dtype)
        pad = (-f.shape[0]) % PACK_ALIGN
        parts.append(jnp.pad(f, (0, pad)) if pad else f)
        total += f.shape[0] + pad
    rows = total // LANES
    rpad = (-rows) % PACK_ROWS_ALIGN
    if rpad:
        parts.append(jnp.zeros((rpad * LANES,), dtype))
    return jnp.concatenate(parts).reshape(rows + rpad, LANES)


def _unpack(buf, shapes):
    flat = buf.reshape(-1)
    outs, off = [], 0
    for s in shapes:
        n = int(np.prod(s))
        outs.append(flat[off:off + n].reshape(s))
        off += n + (-n) % PACK_ALIGN
    return outs


def _split4(a, axis):
    n = a.shape[axis] // N_CHIPS
    return [lax.slice_in_dim(a, s * n, (s + 1) * n, axis=axis) for s in range(N_CHIPS)]


PROJ_ORDER = (3, 7, 8, 9, 0, 1, 2, 6, 4, 5)


def _reorder_w_in(w, cfg):
    d, hg = cfg.d, cfg.hg

    def block(k):
        off = k * d + (2 * hg if k >= 4 else 0)
        return w[:, off:off + d]

    tail = jnp.pad(w[:, 4 * d:4 * d + 2 * hg], ((0, 0), (0, LANES - 2 * hg)))
    return jnp.concatenate([block(k) for k in PROJ_ORDER] + [tail], axis=1)


def _restore_w_in(wr, cfg):
    d, hg = cfg.d, cfg.hg
    at = {k: i for i, k in enumerate(PROJ_ORDER)}
    block = lambda k: wr[:, at[k] * d:(at[k] + 1) * d]
    return jnp.concatenate([block(k) for k in range(4)] + [wr[:, 10 * d:10 * d + 2 * hg]] +
                           [block(k) for k in range(4, 10)], axis=1)


def _step(cfg, x, tgt, shard, m_shard, v_shard):
    d, hg, dff, rp, tr, tm = cfg.d, cfg.hg, cfg.dff, cfg.rp, cfg.tr, cfg.tm
    nrow = rp // tr
    assert cfg.tf * N_CHIPS == 2 * dff and cfg.din % N_CHIPS == 0
    cidx = lax.axis_index("c").astype(jnp.int32).reshape(1)
    chip = (2 * lax.axis_index("x") + lax.axis_index("y")).astype(jnp.int32).reshape(1)

    axis = dict(PARAMS)
    small = ("meta", "gdn_conv_w", "ffn_conv_w")
    small_shapes = [shard[n].shape for n in small]
    mine = [shard[n][0].astype(BF16) for n in BIG] + [_pack([shard[n] for n in small], F32)]

    def with_own(gathered, own):
        return [lax.dynamic_update_slice(g, w[None], (chip[0], 0, 0)) for g, w in zip(gathered, own)]

    g_in, g_small = _run_comm("weights_gather_first", _gather_comm([mine[0], mine[4]]))
    g_small, = with_own([g_small], [mine[4]])
    w_in_r = _reorder_w_in(jnp.concatenate([jnp.where(chip[0] == s, mine[0], g_in[s]) for s in range(N_CHIPS)], axis=1),
                           cfg)
    per_chip = [_unpack(g_small[s], small_shapes) for s in range(N_CHIPS)]
    full = {n: jnp.concatenate([per_chip[s][k] for s in range(N_CHIPS)], axis=axis[n]) for k, n in enumerate(small)}
    meta = full["meta"]
    gconv_w = full["gdn_conv_w"][0]
    fconv_w = full["ffn_conv_w"][0]
    norm1, norm2, gnorm = shard["norm1"], shard["norm2"], shard["gdn_norm"]
    normf = shard["norm_f"].reshape(1, d)
    fconv_b = shard["ffn_conv_b"]
    alog = jnp.pad(shard["gdn_a_log"], ((0, 7), (0, LANES - hg)))
    dtb = jnp.pad(shard["gdn_dt_bias"], ((0, 7), (0, LANES - hg)))

    h0 = jnp.concatenate([jnp.zeros((cfg.front, d), F32), meta, x], axis=0)
    half = RET_DK // 2
    pos = np.arange(rp, dtype=np.float32) - np.float32(cfg.front)
    inv = (np.float32(1.0) / np.float32(ROPE_BASE) ** (np.arange(half, dtype=np.float32) / np.float32(half))).astype(np.float32)
    ang = pos[:, None] * inv[None, :]
    cos, sin = jnp.asarray(np.cos(ang), F32), jnp.asarray(np.sin(ang), F32)
    rconsts = _ret_consts(cfg)

    tr_n = 3 * tr if rp % (3 * tr) == 0 else tr
    rms_f = _make_rms_fn(cfg, tr_n, False)
    rms_b = _make_rms_fn(cfg, tr_n, True)
    rowshape = jax.ShapeDtypeStruct((rp, d), F32)
    rspec = _rows(tr, d)
    nspec = _rows(tr_n, d)

    def rms_fwd(name, h, g):
        return _stage_fwd(name, rms_f, (rp // tr_n,), [In(h, nspec), In(g, _full(g))],
                          [jax.ShapeDtypeStruct((rp, d), BF16)], [nspec])[0]

    wm = 10 * d
    w_main, w_tail = w_in_r[:, :wm], w_in_r[:, wm:]
    tn_in = 2560 if wm % 2560 == 0 else LANES
    hn1 = rms_fwd("rms1_fwd", h0, norm1)
    proj, rest = _mm("proj_fwd", hn1, w_main, tm=tm, tn=tn_in, tk=d, out_dtype=BF16, comm=_gather_comm(mine[1:4]))
    ptail = _mm("proj_tail_fwd", hn1, w_tail, tm=tm, tn=LANES, tk=d)
    g_out, g_up, g_down = with_own(rest, mine[1:4])
    w_out = g_out.reshape(d, d)
    w_up = g_up
    w_up_t = jnp.swapaxes(g_up, 1, 2).reshape(2 * dff, d)
    w_down = g_down.reshape(dff, d)
    cqkv = _conv_fwd("gdn_conv_fwd", proj, CONV_COL * d, gconv_w, None, taps=GDN_CONV, width=3 * d, tr=tr, tc=d)
    prep_fn = _make_gdn_prep_fn(cfg, tr)
    prep_ins = [In(cqkv, _rows(tr, 3 * d), BF16), In(ptail, _rows(tr, LANES), BF16),
                In(alog, _full(alog), F32, True), In(dtb, _full(dtb), F32, True)]
    qn, kn, vv, bB, lB = _stage_fwd("gdn_prep_fwd", prep_fn, (nrow,), prep_ins, [rowshape] * 5, [rspec] * 5)

    trg = cfg.nb * CHUNK
    gi_grid = (rp // trg, hg)
    hspec = pl.BlockSpec((trg, GDN_DK), lambda i, h: (i, h))
    aspec = pl.BlockSpec((1, trg, CHUNK), lambda i, h: (h, i, 0))
    gspec = pl.BlockSpec((1, cfg.nb, 1, GDN_DK), lambda i, h: (h, i, 0, 0))
    intra_ins = [In(t, hspec, F32) for t in (qn, kn, vv, bB, lB)]
    ashape = jax.ShapeDtypeStruct((hg, rp, CHUNK), F32)
    intra_shapes = [rowshape, rowshape, ashape, rowshape, rowshape, jax.ShapeDtypeStruct((hg, cfg.nch, 1, GDN_DK), F32), ashape]
    intra_specs = [hspec, hspec, aspec, hspec, hspec, gspec, aspec]
    gu, gw, gattn, gqd, gkd, ggl, gtinv = _stage_fwd("gdn_intra_fwd", _gdn_intra_fn, gi_grid, intra_ins, intra_shapes,
                                                     intra_specs)
    rot_fn = _make_rot_fn(cfg)

    def rot_ins(dproj=None):
        return [In(proj, _rows(tr_n, 2 * d, ROT_COL // 2), BF16, galias=dproj, gshape=(rp, wm)),
                In(cos, _rows(tr_n, half)), In(sin, _rows(tr_n, half))]

    qr, kr = _stage_fwd("rot_fwd", rot_fn, (rp // tr_n,), rot_ins(), [rowshape] * 2, [nspec] * 2)
    nst = cfg.nch // cfg.sc
    oa, gss = _run_parts("gdn_scan_fwd", (nst,), [_gdn_scan_fwd(cfg, gu, gw, gattn, gqd, gkd, ggl)])[0]
    ob, rss = _run_parts("ret_scan_fwd", (nst,), [_ret_scan_fwd(cfg, qr, kr, proj, rconsts)])[0]

    mix_fn = _make_mix_fn(cfg)
    mix_ins = [In(oa, rspec, F32), In(ob, rspec, F32), In(proj, _rows(tr, 4 * d, MIX_COL // 4), BF16, gshape=(rp, wm)),
               In(gnorm, _full(gnorm), F32, True)]
    ymix = _stage_fwd("mix_fwd", mix_fn, (nrow,), mix_ins, [jax.ShapeDtypeStruct((rp, d), BF16)], [rspec])[0]
    h1 = _mm("out_proj_fwd", ymix, w_out, tm=tm, tn=d, tk=d, add=h0)

    hn2 = rms_fwd("rms2_fwd", h1, norm2)
    up = _mm("ffn_up_fwd", hn2, w_up, tm=tm, tn=cfg.tf, tk=d, out_dtype=BF16)
    uc = _conv_fwd("ffn_conv_fwd", up, 0, fconv_w, fconv_b, taps=FFN_CONV, width=2 * dff, tr=tr, tc=cfg.tf)
    tra = tr
    act_ins = [In(uc, _rows(tra, 2 * dff), BF16)]
    act_spec = _rows(tra, dff)
    act = _stage_fwd("ffn_act_fwd", _act_fn, (rp // tra,), act_ins, [jax.ShapeDtypeStruct((rp, dff), BF16)], [act_spec])[0]
    h2 = _mm("ffn_down_fwd", act, w_down, tm=tm, tn=d, tk=cfg.tf, add=h1)

    dh2, g_normf, loss_blk = _final(cfg, h2, normf, tgt)
    loss = lax.psum(loss_blk[0, 0], ("x", "y", "c"))

    g_w_down = _mm_tn("ffn_down_dw", act, dh2, tr=tm, tka=cfg.tf, tn=d)
    dact = _mm("ffn_down_dx", dh2, w_down.T, tm=tm, tn=cfg.tf, tk=d, out_dtype=BF16)
    duc, = _stage_bwd("ffn_act_bwd", _act_fn, (rp // tra,), act_ins, [(dact, act_spec)])
    dup, g_fconv_w, g_fconv_b = _conv_bwd("ffn_conv_bwd", up, 0, fconv_w, duc, taps=FFN_CONV, width=2 * dff,
                                          tr=tr, tc=cfg.tf, with_bias=True)
    g_w_up = _mm_tn("ffn_up_dw", hn2, dup, tr=tm, tka=d, tn=cfg.tf, blocked=True)

    def pair_reduce(tag, names, arrs):
        recvs = _pair_exchange("grad_pair_exchange_" + tag, arrs)
        return [_pair_sum("grad_pair_sum_" + n, g, r, cidx) for n, g, r in zip(names, arrs, recvs)]

    parts_ffn = pair_reduce("ffn", ["w_ffn_down", "w_ffn_up"], [g_w_down.reshape(N_CHIPS, dff // N_CHIPS, d), g_w_up])
    dhn2, slots_ffn = _mm("ffn_up_dx", dup, w_up_t, tm=tm, tn=d, tk=2 * cfg.tf, comm=_exchange_comm(parts_ffn))

    def rms_bwd(name, h, g, dhn, dres):
        ins = [In(h, nspec, F32), In(g, _full(g), F32, True)]
        return _stage_bwd(name, rms_b, (rp // tr_n,), ins, [(dhn, nspec), (dres, nspec)])

    dh1, g_norm2 = rms_bwd("rms2_bwd", h1, norm2, dhn2, dh2)
    g_w_out = _mm_tn("out_proj_dw", ymix, dh1, tr=tm, tka=d, tn=d)
    dymix = _mm("out_proj_dx", dh1, w_out.T, tm=tm, tn=d, tk=d)
    doa, dob, dproj, g_gnorm = _stage_bwd("mix_bwd", mix_fn, (nrow,), mix_ins, [(dymix, rspec)])

    dqr, dkr, dproj = _run_parts("ret_scan_bwd", (nst,), [_ret_scan_bwd(cfg, dob, qr, kr, proj, rconsts, rss, dproj)])[0]
    dproj, = _stage_bwd("rot_bwd", rot_fn, (rp // tr_n,), rot_ins(dproj), [(dqr, nspec), (dkr, nspec)])
    dgu, dgw, dgattn, dgqd, dgkd, dggl = _run_parts(
        "gdn_scan_bwd", (nst,), [_gdn_scan_bwd(cfg, doa, gu, gw, gattn, gqd, gkd, ggl, gss)])[0]

    intra_cots = [(dgu, hspec), (dgw, hspec), (dgattn, aspec), (dgqd, hspec), (dgkd, hspec), (dggl, gspec)]
    dqn, dkn, dvv, dbB, dlB = _stage_bwd("gdn_intra_bwd", _gdn_intra_fn, gi_grid, intra_ins + [In(gtinv, aspec)], intra_cots)
    dcqkv, dtail, g_alog, g_dtb = _stage_bwd(
        "gdn_prep_bwd", prep_fn, (nrow,), prep_ins, [(t, rspec) for t in (dqn, dkn, dvv, dbB, dlB)])
    dproj, g_gconv_w = _conv_bwd("gdn_conv_bwd", proj, CONV_COL * d, gconv_w, dcqkv, taps=GDN_CONV, width=3 * d,
                                 tr=tr, tc=d, with_bias=False, dx_into=dproj)
    g_w_in_r = jnp.concatenate([_mm_tn("proj_dw", hn1, dproj, tr=tm, tka=d, tn=tn_in),
                                _mm_tn("proj_tail_dw", hn1, dtail, tr=tm, tka=d, tn=LANES)], axis=1)
    g_in4 = jnp.stack(_split4(_restore_w_in(g_w_in_r, cfg), 1))
    parts_mix = pair_reduce("mix", ["w_out", "w_in"], [g_w_out.reshape(N_CHIPS, d // N_CHIPS, d), g_in4])
    dhn1_tail = _mm("proj_tail_dx", dtail, w_tail.T, tm=tm, tn=d, tk=LANES)
    dhn1, slots_mix = _mm("proj_dx", dproj, w_main.T, tm=tm, tn=d, tk=tn_in // 2 if tn_in > LANES else LANES, add=dhn1_tail,
                          comm=_exchange_comm(parts_mix))
    dh0, g_norm1 = rms_bwd("rms1_bwd", h0, norm1, dhn1, dh1)

    grad_x = dh0[cfg.xrow:]
    small_grads = {
        "meta": dh0[cfg.front:cfg.xrow], "norm1": g_norm1, "gdn_conv_w": g_gconv_w[None],
        "gdn_a_log": g_alog[0:1, :hg], "gdn_dt_bias": g_dtb[0:1, :hg], "gdn_norm": g_gnorm, "norm2": g_norm2,
        "ffn_conv_w": g_fconv_w[None], "ffn_conv_b": g_fconv_b, "norm_f": g_normf.reshape(d),
    }

    small_names = [n for n, _ in PARAMS if n not in BIG]
    g_small = jnp.stack([_pack([small_grads[n] if axis[n] is None else _split4(small_grads[n], axis[n])[s]
                                for n in small_names], F32) for s in range(N_CHIPS)])
    parts_small = pair_reduce("small", ["small"], [g_small])
    slots_small = _run_comm("grad_exchange_small", _exchange_comm(parts_small))
    tags = ["w_in", "w_out", "w_ffn_up", "w_ffn_down", "small"]
    parts = [parts_mix[1], parts_mix[0], parts_ffn[1], parts_ffn[0], parts_small[0]]
    slots = [slots_mix[1], slots_mix[0], slots_ffn[1], slots_ffn[0], slots_small[0]]
    fins = [_chip_sum("grad_chip_sum_" + t, p, s, chip) for t, p, s in zip(tags, parts, slots)]
    sibs = _pair_swap(fins)

    def flat2(a):
        return a.reshape(-1, a.shape[-1])

    outs = {}
    for k, t in enumerate(BIG):
        res = _adamw("adamw_" + t, flat2(shard[t]), fins[k], sibs[k], flat2(m_shard[t]), flat2(v_shard[t]), cidx)
        outs[t] = [r.reshape(shard[t].shape) for r in res]
    small_shapes_all = [shard[n].shape for n in small_names]
    pk = lambda src: _pack([src[n] for n in small_names], F32)
    res = _adamw("adamw_small", pk(shard), fins[4], sibs[4], pk(m_shard), pk(v_shard), cidx)
    for k, r in enumerate(res):
        for n, a in zip(small_names, _unpack(r, small_shapes_all)):
            outs.setdefault(n, [None] * 4)[k] = a
    names = [n for n, _ in PARAMS]
    return (loss, grad_x[None], *[outs[n][k] for k in range(4) for n in names])


def kernel(x, meta, norm1, w_in, gdn_conv_w, gdn_a_log, gdn_dt_bias, gdn_norm, w_out, norm2, w_ffn_up, ffn_conv_w, ffn_conv_b, w_ffn_down, norm_f, loss_target, m_meta, m_norm1, m_w_in, m_gdn_conv_w, m_gdn_a_log, m_gdn_dt_bias, m_gdn_norm, m_w_out, m_norm2, m_w_ffn_up, m_ffn_conv_w, m_ffn_conv_b, m_w_ffn_down, m_norm_f, v_meta, v_norm1, v_w_in, v_gdn_conv_w, v_gdn_a_log, v_gdn_dt_bias, v_gdn_norm, v_w_out, v_norm2, v_w_ffn_up, v_ffn_conv_w, v_ffn_conv_b, v_w_ffn_down, v_norm_f):
    names = [n for n, _ in PARAMS]
    shard = dict(zip(names, (meta, norm1, w_in, gdn_conv_w, gdn_a_log, gdn_dt_bias, gdn_norm, w_out, norm2, w_ffn_up,
                             ffn_conv_w, ffn_conv_b, w_ffn_down, norm_f)))
    m_shard = dict(zip(names, (m_meta, m_norm1, m_w_in, m_gdn_conv_w, m_gdn_a_log, m_gdn_dt_bias, m_gdn_norm, m_w_out,
                               m_norm2, m_w_ffn_up, m_ffn_conv_w, m_ffn_conv_b, m_w_ffn_down, m_norm_f)))
    v_shard = dict(zip(names, (v_meta, v_norm1, v_w_in, v_gdn_conv_w, v_gdn_a_log, v_gdn_dt_bias, v_gdn_norm, v_w_out,
                               v_norm2, v_w_ffn_up, v_ffn_conv_w, v_ffn_conv_b, v_w_ffn_down, v_norm_f)))
    return _step(REAL, x[0], loss_target[0], shard, m_shard, v_shard)
```

```python
import functools
from typing import NamedTuple

import numpy as np
import jax
import jax.numpy as jnp
from jax import lax
from jax.experimental import pallas as pl
from jax.experimental.pallas import tpu as pltpu

F32 = jnp.float32
BF16 = jnp.bfloat16
EPS = 1e-6
CHUNK = 64
GDN_DK = 128
RET_DK = 256
GDN_CONV = 4
FFN_CONV = 3
ROPE_BASE = 10000.0
LANES = 128
N_CHIPS = 4
ADAM_LR, ADAM_B1, ADAM_B2, ADAM_EPS, ADAM_WD, ADAM_STEP = 0.001, 0.9, 0.999, 1e-08, 0.01, 10
MIX_COL, CONV_COL, RV_BLOCK, ROT_COL = 0, 4, 7, 8
MESH = pl.DeviceIdType.MESH
VMEM_LIMIT = 56 * 1024 * 1024


class Cfg(NamedTuple):
    d: int
    seq: int
    n_meta: int
    dff: int
    tr: int
    nb: int
    tm: int
    tf: int
    sc: int

    @property
    def hg(self): return self.d // GDN_DK
    @property
    def hr(self): return self.d // RET_DK
    @property
    def L(self): return self.n_meta + self.seq
    @property
    def rp(self): return -(-self.L // 256) * 256
    @property
    def front(self): return self.rp - self.L
    @property
    def xrow(self): return self.rp - self.seq
    @property
    def nch(self): return self.rp // CHUNK
    @property
    def din(self): return 10 * self.d + 2 * self.hg


REAL = Cfg(d=1024, seq=8192, n_meta=16, dff=2816, tr=256, nb=12, tm=1408, tf=1408, sc=4)


def _pallas(body, **kw):
    return pl.pallas_call(body, **kw)


def _sigmoid_raw(x):
    return 1.0 / (1.0 + jnp.exp(-x))


@jax.custom_vjp
def _sigmoid(x):
    return _sigmoid_raw(x)


def _sigmoid_fwd(x):
    s = _sigmoid_raw(x)
    return s, s


def _sigmoid_bwd(s, g):
    return (g * (s * (1.0 - s)),)


_sigmoid.defvjp(_sigmoid_fwd, _sigmoid_bwd)


@jax.custom_vjp
def _silu(x):
    return x * _sigmoid_raw(x)


def _silu_fwd(x):
    s = _sigmoid_raw(x)
    return x * s, (x, s)


def _silu_bwd(res, g):
    x, s = res
    return (g * (s * (1.0 + x * (1.0 - s))),)


_silu.defvjp(_silu_fwd, _silu_bwd)


def _softplus(x):
    return jnp.maximum(x, 0.0) + jnp.log(1.0 + jnp.exp(-jnp.abs(x)))


def _raw_dot(a, b, ta, tb, hi):
    if not hi:
        a = a.astype(BF16)
        b = b.astype(BF16)
    nbatch = a.ndim - 2
    ca = a.ndim - 2 if ta else a.ndim - 1
    cb = b.ndim - 1 if tb else b.ndim - 2
    batch = tuple(range(nbatch))
    return lax.dot_general(a, b, (((ca,), (cb,)), (batch, batch)),
                           precision=lax.Precision.HIGHEST if hi else None,
                           preferred_element_type=F32)


@functools.partial(jax.custom_vjp, nondiff_argnums=(2, 3, 4))
def _dot_p(a, b, ta, tb, hi):
    return _raw_dot(a, b, ta, tb, hi)


def _dot(a, b, ta=False, tb=False, hi=False):
    return _dot_p(a, b, ta, tb, hi)


def _dot_fwd(a, b, ta, tb, hi):
    return _raw_dot(a, b, ta, tb, hi), (a, b)


def _dot_bwd(ta, tb, hi, res, g):
    a, b = res
    if not ta and not tb:
        da, db = _dot(g, b, False, True, hi), _dot(a, g, True, False, hi)
    elif not ta and tb:
        da, db = _dot(g, b, False, False, hi), _dot(g, a, True, False, hi)
    elif ta and not tb:
        da, db = _dot(b, g, False, True, hi), _dot(a, g, False, False, hi)
    else:
        raise NotImplementedError
    return da.astype(a.dtype), db.astype(b.dtype)


_dot_p.defvjp(_dot_fwd, _dot_bwd)


def _iota2(n, m, axis):
    return lax.broadcasted_iota(jnp.int32, (n, m), axis)


def _bcast(mat, nb):
    return jnp.broadcast_to(mat[None], (nb,) + mat.shape)


def _split3(a):
    a0 = a.astype(BF16)
    r1 = a - a0.astype(F32)
    a1 = r1.astype(BF16)
    return a0, a1, (r1 - a1.astype(F32)).astype(BF16)


@functools.partial(jax.custom_vjp, nondiff_argnums=(2,))
def _dot_sel(a, e, te):
    eb = e.astype(BF16)
    p0, p1, p2 = (_raw_dot(p, eb, False, te, False) for p in _split3(a))
    return p0 + (p1 + p2)


def _dot_sel_fwd(a, e, te):
    return _dot_sel(a, e, te), e


def _dot_sel_bwd(te, e, g):
    return _dot_sel(g, e, not te), jnp.zeros_like(e)


_dot_sel.defvjp(_dot_sel_fwd, _dot_sel_bwd)


@jax.custom_vjp
def _sel_dot(e, x):
    eb = e.astype(BF16)
    p0, p1, p2 = (_raw_dot(eb, p, False, False, False) for p in _split3(x))
    return p0 + (p1 + p2)


def _sel_dot_fwd(e, x):
    return _sel_dot(e, x), e


def _sel_dot_bwd(e, g):
    eb = e.astype(BF16)
    p0, p1, p2 = (_raw_dot(eb, p, True, False, False) for p in _split3(g))
    return jnp.zeros_like(e), p0 + (p1 + p2)


_sel_dot.defvjp(_sel_dot_fwd, _sel_dot_bwd)


def _tri_inv_raw(m):
    nb = m.shape[0]
    r, c = _iota2(CHUNK, CHUNK, 0), _iota2(CHUNK, CHUNK, 1)
    t = _bcast((r == c).astype(F32), nb)
    b = 1
    while b < CHUNK:
        sh = b.bit_length() - 1
        off = ((r >> (sh + 1)) == (c >> (sh + 1))) & ((r >> sh) != (c >> sh)) & (r > c)
        cl = jnp.where(off[None], m, 0.0)
        t = t - _raw_dot(_raw_dot(t, cl, False, False, False), t, False, False, False)
        b *= 2
    return t


@jax.custom_vjp
def _tri_inv_given(m, t):
    return t


def _tri_inv_fwd(m, t):
    return t, t


def _tri_inv_bwd(t, g):
    return -_raw_dot(_raw_dot(t, g, True, False, False), t, False, True, False), jnp.zeros_like(t)


_tri_inv_given.defvjp(_tri_inv_fwd, _tri_inv_bwd)


def _rms(h, g):
    return h * lax.rsqrt(jnp.mean(h * h, axis=-1, keepdims=True) + EPS) * g


class In(NamedTuple):
    arr: jax.Array
    spec: pl.BlockSpec
    grad: object = None
    acc: bool = False
    gshape: object = None
    gspec: object = None
    galias: object = None


def _params(grid):
    sem = ("arbitrary",) * len(grid)
    return pltpu.CompilerParams(dimension_semantics=sem, vmem_limit_bytes=VMEM_LIMIT)


def _stage_fwd(name, fn, grid, ins, out_shapes, out_specs):
    n_in = len(ins)

    def body(*refs):
        pids = tuple(pl.program_id(k) for k in range(len(grid)))
        vals = [r[...].astype(F32) for r in refs[:n_in]]
        outs = fn(pids, *vals)
        for o_ref, o in zip(refs[n_in:], outs):
            o_ref[...] = o.reshape(o_ref.shape).astype(o_ref.dtype)

    return _pallas(
        body, out_shape=out_shapes, grid=grid, in_specs=[i.spec for i in ins],
        out_specs=out_specs, name=name, compiler_params=_params(grid))(*[i.arr for i in ins])


def _stage_bwd(name, fn, grid, ins, cots):
    n_in, n_ct = len(ins), len(cots)
    didx = [k for k, i in enumerate(ins) if i.grad is not None]
    aliased = [(o, ins[k].galias) for o, k in enumerate(didx) if ins[k].galias is not None]
    n_al = len(aliased)

    def body(*refs):
        pids = tuple(pl.program_id(k) for k in range(len(grid)))
        vals = [r[...].astype(F32) for r in refs[:n_in]]
        ct_refs = refs[n_in:n_in + n_ct]
        g_refs = refs[n_in + n_ct + n_al:]

        def f(*dv):
            merged = list(vals)
            for k, v in zip(didx, dv):
                merged[k] = v
            return tuple(fn(pids, *merged))

        outs, vjp_fn = jax.vjp(f, *[vals[k].astype(F32) for k in didx])
        cts = tuple(c[...].reshape(o.shape).astype(F32) for c, o in zip(ct_refs, outs))
        grads = vjp_fn(cts)
        first = functools.reduce(jnp.logical_and, [p == 0 for p in pids])
        for k, g_ref, g in zip(didx, g_refs, grads):
            if ins[k].acc:
                @pl.when(first)
                def _(g_ref=g_ref):
                    g_ref[...] = jnp.zeros(g_ref.shape, g_ref.dtype)
                g_ref[...] += g.reshape(g_ref.shape).astype(g_ref.dtype)
            else:
                g_ref[...] = g.reshape(g_ref.shape).astype(g_ref.dtype)

    out_shapes = [jax.ShapeDtypeStruct(ins[k].gshape or ins[k].arr.shape, ins[k].grad) for k in didx]
    out_specs = [ins[k].gspec or ins[k].spec for k in didx]
    return _pallas(
        body, out_shape=out_shapes, grid=grid,
        in_specs=[i.spec for i in ins] + [c[1] for c in cots] + [ANY] * n_al, out_specs=out_specs,
        input_output_aliases={n_in + n_ct + a: o for a, (o, _) in enumerate(aliased)},
        name=name, compiler_params=_params(grid))(*[i.arr for i in ins], *[c[0] for c in cots], *[a for _, a in aliased])


def _full(arr):
    nd = arr.ndim
    return pl.BlockSpec(arr.shape, lambda *p: (0,) * nd)


def _rows(tr, width, blk=0):
    return pl.BlockSpec((tr, width), lambda i: (i, blk))


def _mm(name, a, b, *, tm, tn, tk, out_dtype=F32, add=None, comm=None):
    M, K = a.shape
    N = b.shape[1] if b.ndim == 2 else b.shape[0] * b.shape[2]
    nk = K // tk
    grid = (M // tm, N // tn, nk)
    n_in = 3 if add is not None else 2
    n_ci, n_co = (len(comm.ins), len(comm.outs)) if comm is not None else (0, 0)

    def body(*refs):
        a_ref, b_ref = refs[0], refs[1]
        add_ref = refs[2] if add is not None else None
        c_ins = refs[n_in:n_in + n_ci]
        o_ref = refs[n_in + n_ci]
        c_outs = refs[n_in + n_ci + 1:n_in + n_ci + 1 + n_co]
        scratch = refs[n_in + n_ci + 1 + n_co:]
        acc_ref = scratch[0] if nk > 1 else None
        sems = scratch[1 if nk > 1 else 0:]
        step = (pl.program_id(0) * grid[1] + pl.program_id(1)) * nk + pl.program_id(2)
        if comm is not None:
            @pl.when(step == 0)
            def _():
                comm.start(c_ins, c_outs, sems)

        part = _raw_dot(a_ref[...], b_ref[...], False, False, False)

        def finish(total):
            if add_ref is not None:
                total = total + add_ref[...]
            o_ref[...] = total.astype(o_ref.dtype)

        if nk == 1:
            finish(part)
        else:
            k = pl.program_id(2)

            @pl.when(k == 0)
            def _():
                acc_ref[...] = part

            @pl.when(k > 0)
            def _():
                acc_ref[...] += part

            @pl.when(k == nk - 1)
            def _():
                finish(acc_ref[...])

        if comm is not None:
            @pl.when(step == grid[0] * grid[1] * nk - 1)
            def _():
                comm.finish(c_ins, c_outs, sems)

    b_spec = (pl.BlockSpec((tk, tn), lambda i, j, k: (k, j)) if b.ndim == 2 else
              pl.BlockSpec((None, tk, tn), lambda i, j, k: (j, k, 0)))
    in_specs = [pl.BlockSpec((tm, tk), lambda i, j, k: (i, k)), b_spec]
    args = [a, b]
    if add is not None:
        in_specs.append(pl.BlockSpec((tm, tn), lambda i, j, k: (i, j)))
        args.append(add)
    out_shape = jax.ShapeDtypeStruct((M, N), out_dtype)
    out_spec = pl.BlockSpec((tm, tn), lambda i, j, k: (i, j))
    scratch = [pltpu.VMEM((tm, tn), F32)] if nk > 1 else []
    if comm is None:
        return _pallas(body, out_shape=out_shape, grid=grid, in_specs=in_specs, out_specs=out_spec,
                       scratch_shapes=scratch, name=name, compiler_params=_params(grid))(*args)
    res = _pallas(body, out_shape=[out_shape] + comm.outs, grid=grid, in_specs=in_specs + [ANY] * n_ci,
                  out_specs=[out_spec] + [ANY] * n_co, scratch_shapes=scratch + comm.sems, name=name,
                  compiler_params=_params(grid))(*args, *comm.ins)
    return res[0], res[1:]


def _mm_tn(name, a, b, *, tr, tka, tn, blocked=False):
    R, Ka = a.shape
    N = b.shape[1]
    nr = R // tr
    grid = (Ka // tka, N // tn, nr)
    if blocked:
        out_shape = jax.ShapeDtypeStruct((N // tn, Ka, tn), F32)
        out_spec = pl.BlockSpec((None, tka, tn), lambda i, j, r: (j, i, 0))
    else:
        out_shape = jax.ShapeDtypeStruct((Ka, N), F32)
        out_spec = pl.BlockSpec((tka, tn), lambda i, j, r: (i, j))

    def body(a_ref, b_ref, o_ref):
        r = pl.program_id(2)
        part = _raw_dot(a_ref[...], b_ref[...], True, False, False)

        @pl.when(r == 0)
        def _():
            o_ref[...] = part

        @pl.when(r > 0)
        def _():
            o_ref[...] += part

    return _pallas(
        body, out_shape=out_shape, grid=grid,
        in_specs=[pl.BlockSpec((tr, tka), lambda i, j, r: (r, i)),
                  pl.BlockSpec((tr, tn), lambda i, j, r: (r, j))],
        out_specs=out_spec, name=name, compiler_params=_params(grid))(a, b)


def _conv_fwd(name, x, xcol0, w, b, *, taps, width, tr, tc):
    R = x.shape[0]
    grid = (width // tc, R // tr)
    cb0 = xcol0 // tc
    hrows = 16 if x.dtype == BF16 else 8
    hb = tr // hrows

    def body(*refs):
        x_ref, xp_ref, w_ref = refs[:3]
        b_ref = refs[3] if b is not None else None
        o_ref = refs[-1]
        i = pl.program_id(1)
        xv = x_ref[...].astype(F32)
        prev = jnp.where(i > 0, xp_ref[...].astype(F32)[hrows - 8:, :], 0.0)
        ext = jnp.concatenate([prev, xv], axis=0)
        acc = xv * w_ref[taps - 1:taps, :]
        for s in range(1, taps):
            acc = acc + pltpu.roll(ext, s, 0)[8:, :] * w_ref[taps - 1 - s:taps - s, :]
        if b_ref is not None:
            acc = acc + b_ref[...]
        o_ref[...] = acc.astype(o_ref.dtype)

    in_specs = [pl.BlockSpec((tr, tc), lambda j, i: (i, cb0 + j)),
                pl.BlockSpec((hrows, tc), lambda j, i: (jnp.maximum(i * hb - 1, 0), cb0 + j)),
                pl.BlockSpec((taps, tc), lambda j, i: (0, j))]
    args = [x, x, w]
    if b is not None:
        in_specs.append(pl.BlockSpec((1, tc), lambda j, i: (0, j)))
        args.append(b)
    return _pallas(
        body, out_shape=jax.ShapeDtypeStruct((R, width), BF16), grid=grid, in_specs=in_specs,
        out_specs=pl.BlockSpec((tr, tc), lambda j, i: (i, j)),
        name=name, compiler_params=_params(grid))(*args)


def _conv_bwd(name, x, xcol0, w, dy, *, taps, width, tr, tc, with_bias, dx_into=None):
    R = x.shape[0]
    nr = R // tr
    grid = (width // tc, nr)
    cb0 = xcol0 // tc
    hrows = 16 if dy.dtype == BF16 else 8
    hb = tr // hrows
    n_ext = tr + 8
    n_al = 0 if dx_into is None else 1

    def body(*refs):
        x_ref, w_ref, dy_ref, dyn_ref = refs[:4]
        dx_ref, dw_ref = refs[4 + n_al], refs[5 + n_al]
        db_ref = refs[6 + n_al] if with_bias else None
        i = pl.program_id(1)
        xv = x_ref[...].astype(F32)
        dyv = dy_ref[...].astype(F32)
        nxt = dyn_ref[...].astype(F32)[:8, :]
        dext = jnp.concatenate([dyv, jnp.where(i < nr - 1, nxt, 0.0)], axis=0)
        dx = dyv * w_ref[taps - 1:taps, :]
        dws = [None] * taps
        dws[taps - 1] = jnp.sum(xv * dyv, axis=0, keepdims=True)
        for s in range(1, taps):
            ahead = pltpu.roll(dext, n_ext - s, 0)[:tr, :]
            dx = dx + ahead * w_ref[taps - 1 - s:taps - s, :]
            dws[taps - 1 - s] = jnp.sum(xv * ahead, axis=0, keepdims=True)
        dx_ref[...] = dx.astype(dx_ref.dtype)

        @pl.when(i == 0)
        def _():
            for k in range(taps):
                dw_ref[k:k + 1, :] = dws[k]
            if db_ref is not None:
                db_ref[...] = jnp.sum(dyv, axis=0, keepdims=True)

        @pl.when(i > 0)
        def _():
            for k in range(taps):
                dw_ref[k:k + 1, :] += dws[k]
            if db_ref is not None:
                db_ref[...] += jnp.sum(dyv, axis=0, keepdims=True)

    in_specs = [pl.BlockSpec((tr, tc), lambda j, i: (i, cb0 + j)),
                pl.BlockSpec((taps, tc), lambda j, i: (0, j)),
                pl.BlockSpec((tr, tc), lambda j, i: (i, j)),
                pl.BlockSpec((hrows, tc), lambda j, i: (jnp.minimum((i + 1) * hb, R // hrows - 1), j))]
    args = [x, w, dy, dy]
    if dx_into is None:
        dx_shape, dx_spec, aliases = jax.ShapeDtypeStruct((R, width), BF16), pl.BlockSpec((tr, tc), lambda j, i: (i, j)), {}
    else:
        dx_shape = jax.ShapeDtypeStruct(dx_into.shape, dx_into.dtype)
        dx_spec, aliases = pl.BlockSpec((tr, tc), lambda j, i: (i, cb0 + j)), {4: 0}
        in_specs.append(ANY)
        args.append(dx_into)
    out_shape = [dx_shape, jax.ShapeDtypeStruct((taps, width), F32)]
    out_specs = [dx_spec, pl.BlockSpec((taps, tc), lambda j, i: (0, j))]
    if with_bias:
        out_shape.append(jax.ShapeDtypeStruct((1, width), F32))
        out_specs.append(pl.BlockSpec((1, tc), lambda j, i: (0, j)))
    return _pallas(
        body, out_shape=out_shape, grid=grid, in_specs=in_specs, out_specs=out_specs, input_output_aliases=aliases,
        name=name, compiler_params=_params(grid))(*args)


def _row_mask(cfg, i, tr):
    rows = i * tr + lax.broadcasted_iota(jnp.int32, (tr, 1), 0)
    return (rows >= cfg.front).astype(F32)


def _make_rms_fn(cfg, tr, with_residual):
    def fn(pids, h, g):
        hm = h * _row_mask(cfg, pids[0], tr)
        if with_residual:
            return _rms(hm, g), hm
        return (_rms(hm, g),)
    return fn


def _make_gdn_prep_fn(cfg, tr):
    d, hg = cfg.d, cfg.hg

    def fn(pids, c, tail, alog, dtb):
        cq, ck, cv = c[:, :d], c[:, d:2 * d], c[:, 2 * d:]
        mask = _row_mask(cfg, pids[0], tr)
        j, col = _iota2(LANES, d, 0), _iota2(LANES, d, 1)
        ea = ((col >> 7) == j).astype(F32)
        eb = ((col >> 7) + hg == j).astype(F32)
        al = jnp.sum(alog, axis=0, keepdims=True)
        db = jnp.sum(dtb, axis=0, keepdims=True)
        lg = _dot_sel(-jnp.exp(al) * _softplus(tail + db) * mask, ea, False)
        beta = _dot_sel(_sigmoid(tail) * mask, eb, False)
        sq, sk, sv = _silu(cq), _silu(ck), _silu(cv)
        qs, ks = [], []
        for h in range(hg):
            sl = slice(h * GDN_DK, (h + 1) * GDN_DK)
            qh, kh = sq[:, sl], sk[:, sl]
            qs.append(qh * lax.rsqrt(jnp.sum(qh * qh, axis=-1, keepdims=True) + EPS) * (GDN_DK ** -0.5))
            ks.append(kh * lax.rsqrt(jnp.sum(kh * kh, axis=-1, keepdims=True) + EPS))
        return jnp.concatenate(qs, axis=1), jnp.concatenate(ks, axis=1), sv, beta, lg
    return fn


def _gdn_intra_fn(pids, q, k, v, bB, lB, t_saved=None):
    rows = q.shape[0]
    nb = rows // CHUNK
    q3, k3, v3, b3, l3 = [t.reshape(nb, CHUNK, GDN_DK) for t in (q, k, v, bB, lB)]
    r, c = _iota2(CHUNK, CHUNK, 0), _iota2(CHUNK, CHUNK, 1)
    tril = (r >= c)
    strict = (r > c)
    gcol = _sel_dot(_bcast(tril.astype(F32), nb), l3)
    grow = jnp.swapaxes(gcol, 1, 2)[:, :CHUNK, :]
    diff = gcol[:, :, :CHUNK] - grow
    decay = jnp.where(tril[None], jnp.exp(jnp.where(tril[None], diff, 0.0)), 0.0)
    kb = k3 * b3
    m = jnp.where(strict[None], _dot(kb, k3, False, True) * decay, 0.0)
    t = _tri_inv_raw(m) if t_saved is None else _tri_inv_given(m, t_saved.reshape(nb, CHUNK, CHUNK))
    eg = jnp.exp(gcol)
    u = _dot(t, v3 * b3)
    w = _dot(t, kb * eg)
    attn = _dot(q3, k3, False, True) * decay
    qd = q3 * eg
    glast = jnp.sum(l3, axis=1, keepdims=True)
    kd = k3 * jnp.exp(glast - gcol)
    gl = jnp.exp(glast)
    outs = (u.reshape(rows, GDN_DK), w.reshape(rows, GDN_DK), attn.reshape(1, rows, CHUNK),
            qd.reshape(rows, GDN_DK), kd.reshape(rows, GDN_DK), gl.reshape(1, nb, 1, GDN_DK))
    return outs + (t.reshape(1, rows, CHUNK),) if t_saved is None else outs


def _make_rot_fn(cfg):
    hr = cfg.hr
    half = RET_DK // 2

    def fn(pids, rqk, cos, sin):
        rq, rk = rqk[:, :cfg.d], rqk[:, cfg.d:]

        def rot(t, scale):
            outs = []
            for h in range(hr):
                x1 = t[:, h * RET_DK:h * RET_DK + half]
                x2 = t[:, h * RET_DK + half:(h + 1) * RET_DK]
                outs += [(x1 * cos - x2 * sin) * scale, (x2 * cos + x1 * sin) * scale]
            return jnp.concatenate(outs, axis=1)
        return rot(rq, 1.0), rot(rk, RET_DK ** -0.5)
    return fn


def _make_mix_fn(cfg):
    hg, hr = cfg.hg, cfg.hr

    def fn(pids, oa, ob, pm, gnorm):
        d = cfg.d
        gz, rg, gate_a, gate_b = pm[:, :d], pm[:, d:2 * d], pm[:, 2 * d:3 * d], pm[:, 3 * d:]
        oas = []
        for h in range(hg):
            oh = oa[:, h * GDN_DK:(h + 1) * GDN_DK]
            oas.append(oh * lax.rsqrt(jnp.mean(oh * oh, axis=-1, keepdims=True) + EPS) * gnorm)
        ya = jnp.concatenate(oas, axis=1) * _silu(gz)
        obs = []
        for h in range(hr):
            oh = ob[:, h * RET_DK:(h + 1) * RET_DK]
            obs.append(oh * lax.rsqrt(jnp.mean(oh * oh, axis=-1, keepdims=True) + EPS))
        yb = _silu(rg) * jnp.concatenate(obs, axis=1)
        return (_sigmoid(gate_a) * ya + _sigmoid(gate_b) * yb,)
    return fn


def _act_fn(pids, u):
    f = u.shape[1] // 2
    return (_silu(u[:, :f]) * u[:, f:],)


def _gdn_step(s, u, w, a, qd, kd, gl):
    top = _dot(jnp.concatenate([w, qd], axis=0), s)
    v_new = u - top[:CHUNK]
    bot = _dot(jnp.concatenate([a, kd.T], axis=0), v_new)
    o = top[CHUNK:] + bot[:CHUNK]
    s2 = s * gl + bot[CHUNK:]
    return s2, o


def _gdn_step_bwd(s, u, w, a, qd, kd, gl, ds2, do):
    lw = jnp.concatenate([w, qd], axis=0)
    v_new = u - _raw_dot(w, s, False, False, False)
    dv = _raw_dot(a, do, True, False, False) + _raw_dot(kd, ds2, False, False, False)
    da = _raw_dot(do, v_new, False, True, False)
    dkd = _raw_dot(v_new, ds2, False, True, False)
    dtop = jnp.concatenate([-dv, do], axis=0)
    dlw = _raw_dot(dtop, s, False, True, False)
    ds = ds2 * gl + _raw_dot(lw, dtop, True, False, False)
    dgl = jnp.sum(ds2 * s, axis=0, keepdims=True)
    return ds, dv, dlw[:CHUNK], da, dlw[CHUNK:], dkd, dgl


def _ret_step(s, q, k, v, dm, qdc, kdc, g):
    att = _dot(q, k, False, True) * dm
    bot = _dot(jnp.concatenate([att, (k * kdc).T], axis=0), v)
    o = bot[:CHUNK] + _dot(q * qdc, s)
    s2 = s * g + bot[CHUNK:]
    return s2, o


class Part(NamedTuple):
    body: object
    args: list
    in_specs: list
    out_shape: list
    out_specs: list
    scratch: list
    aliases: dict = {}


def _run_parts(name, grid, parts):
    n_in = [len(p.args) for p in parts]
    n_out = [len(p.out_shape) for p in parts]
    n_sc = [len(p.scratch) for p in parts]
    off_in = [sum(n_in[:k]) for k in range(len(parts))]
    off_out = [sum(n_out[:k]) for k in range(len(parts))]
    off_sc = [sum(n_sc[:k]) for k in range(len(parts))]

    def body(*refs):
        ins, outs, scr = refs[:sum(n_in)], refs[sum(n_in):sum(n_in) + sum(n_out)], refs[sum(n_in) + sum(n_out):]
        for k, p in enumerate(parts):
            p.body(*ins[off_in[k]:off_in[k] + n_in[k]], *outs[off_out[k]:off_out[k] + n_out[k]],
                   *scr[off_sc[k]:off_sc[k] + n_sc[k]])

    aliases = {off_in[k] + i: off_out[k] + o for k, p in enumerate(parts) for i, o in p.aliases.items()}
    res = _pallas(
        body, out_shape=sum((p.out_shape for p in parts), []), grid=grid, in_specs=sum((p.in_specs for p in parts), []),
        out_specs=sum((p.out_specs for p in parts), []), scratch_shapes=sum((p.scratch for p in parts), []),
        input_output_aliases=aliases, name=name, compiler_params=_params(grid))(*sum((p.args for p in parts), []))
    return [res[off_out[k]:off_out[k] + n_out[k]] for k in range(len(parts))]


def _gdn_scan_fwd(cfg, u, w, attn, qd, kd, gl):
    d, hg, nch, sc = cfg.d, cfg.hg, cfg.nch, cfg.sc
    nst = nch // sc

    def body(u_ref, w_ref, a_ref, qd_ref, kd_ref, gl_ref, o_ref, ss_ref, s_ref):
        @pl.when(pl.program_id(0) == 0)
        def _():
            s_ref[...] = jnp.zeros(s_ref.shape, F32)

        states = [s_ref[h] for h in range(hg)]
        for j in range(sc):
            rows = slice(j * CHUNK, (j + 1) * CHUNK)
            outs = []
            for h in range(hg):
                sl = slice(h * GDN_DK, (h + 1) * GDN_DK)
                ss_ref[j, h] = states[h]
                states[h], o = _gdn_step(states[h], u_ref[rows, sl], w_ref[rows, sl], a_ref[h, rows, :],
                                         qd_ref[rows, sl], kd_ref[rows, sl], gl_ref[h, j])
                outs.append(o)
            o_ref[rows, :] = jnp.concatenate(outs, axis=1)
        for h in range(hg):
            s_ref[h] = states[h]

    row = pl.BlockSpec((sc * CHUNK, d), lambda n: (n, 0))
    return Part(
        body, [u, w, attn, qd, kd, gl],
        [row, row, pl.BlockSpec((hg, sc * CHUNK, CHUNK), lambda n: (0, n, 0)), row, row,
         pl.BlockSpec((hg, sc, 1, GDN_DK), lambda n: (0, n, 0, 0))],
        [jax.ShapeDtypeStruct((cfg.rp, d), F32), jax.ShapeDtypeStruct((nch, hg, GDN_DK, GDN_DK), F32)],
        [row, pl.BlockSpec((sc, hg, GDN_DK, GDN_DK), lambda n: (n, 0, 0, 0))],
        [pltpu.VMEM((hg, GDN_DK, GDN_DK), F32)])


def _gdn_scan_bwd(cfg, do, u, w, attn, qd, kd, gl, ss):
    d, hg, nch, sc = cfg.d, cfg.hg, cfg.nch, cfg.sc
    nst = nch // sc

    def body(do_ref, u_ref, w_ref, a_ref, qd_ref, kd_ref, gl_ref, ss_ref,
             du_ref, dw_ref, da_ref, dqd_ref, dkd_ref, dgl_ref, ds_ref):
        @pl.when(pl.program_id(0) == 0)
        def _():
            ds_ref[...] = jnp.zeros(ds_ref.shape, F32)

        dstates = [ds_ref[h] for h in range(hg)]
        for j in reversed(range(sc)):
            rows = slice(j * CHUNK, (j + 1) * CHUNK)
            dus, dws, dqds, dkds = [], [], [], []
            for h in range(hg):
                sl = slice(h * GDN_DK, (h + 1) * GDN_DK)
                args = (ss_ref[j, h], u_ref[rows, sl], w_ref[rows, sl], a_ref[h, rows, :], qd_ref[rows, sl],
                        kd_ref[rows, sl], gl_ref[h, j])
                dstates[h], du, dw, da, dqd, dkd, dgl = _gdn_step_bwd(*args, dstates[h], do_ref[rows, sl])
                da_ref[h, rows, :] = da
                dgl_ref[h, j] = dgl
                dus.append(du)
                dws.append(dw)
                dqds.append(dqd)
                dkds.append(dkd)
            du_ref[rows, :] = jnp.concatenate(dus, axis=1)
            dw_ref[rows, :] = jnp.concatenate(dws, axis=1)
            dqd_ref[rows, :] = jnp.concatenate(dqds, axis=1)
            dkd_ref[rows, :] = jnp.concatenate(dkds, axis=1)
        for h in range(hg):
            ds_ref[h] = dstates[h]

    row = pl.BlockSpec((sc * CHUNK, d), lambda n: (nst - 1 - n, 0))
    aspec = pl.BlockSpec((hg, sc * CHUNK, CHUNK), lambda n: (0, nst - 1 - n, 0))
    gspec = pl.BlockSpec((hg, sc, 1, GDN_DK), lambda n: (0, nst - 1 - n, 0, 0))
    rowshape = jax.ShapeDtypeStruct((cfg.rp, d), F32)
    return Part(
        body, [do, u, w, attn, qd, kd, gl, ss],
        [row, row, row, aspec, row, row, gspec, pl.BlockSpec((sc, hg, GDN_DK, GDN_DK), lambda n: (nst - 1 - n, 0, 0, 0))],
        [rowshape, rowshape, jax.ShapeDtypeStruct(attn.shape, F32), rowshape, rowshape, jax.ShapeDtypeStruct(gl.shape, F32)],
        [row, row, aspec, row, row, gspec],
        [pltpu.VMEM((hg, GDN_DK, GDN_DK), F32)])


def _ret_consts(cfg):
    hr = cfg.hr
    lg = np.log(1.0 - 2.0 ** (-5.0 - np.arange(hr, dtype=np.float64)))
    idx = np.arange(CHUNK, dtype=np.float64)
    tril = np.tril(np.ones((CHUNK, CHUNK), dtype=bool))
    dm = np.where(tril[None], np.exp((idx[:, None] - idx[None, :])[None] * lg[:, None, None]), 0.0)
    qdc = np.exp((idx[None, :] + 1.0) * lg[:, None])
    kdc = np.exp((CHUNK - 1.0 - idx[None, :]) * lg[:, None])
    gch = np.exp(CHUNK * lg)
    qdc = np.broadcast_to(qdc[:, :, None], (hr, CHUNK, RET_DK))
    kdc = np.broadcast_to(kdc[:, :, None], (hr, CHUNK, RET_DK))
    gch = np.broadcast_to(gch[:, None, None], (hr, 1, RET_DK))
    return tuple(jnp.asarray(np.ascontiguousarray(t), F32) for t in (dm, qdc, kdc, gch))


def _ret_scan_fwd(cfg, qr, kr, proj, consts):
    d, hr, nch, sc = cfg.d, cfg.hr, cfg.nch, cfg.sc
    nst = nch // sc
    dm, qdc, kdc, gch = consts

    def body(q_ref, k_ref, v_ref, dm_ref, qdc_ref, kdc_ref, g_ref, o_ref, ss_ref, s_ref):
        @pl.when(pl.program_id(0) == 0)
        def _():
            s_ref[...] = jnp.zeros(s_ref.shape, F32)

        states = [s_ref[h] for h in range(hr)]
        for j in range(sc):
            rows = slice(j * CHUNK, (j + 1) * CHUNK)
            outs = []
            for h in range(hr):
                sl = slice(h * RET_DK, (h + 1) * RET_DK)
                ss_ref[j, h] = states[h]
                states[h], o = _ret_step(states[h], q_ref[rows, sl], k_ref[rows, sl], v_ref[rows, sl], dm_ref[h],
                                         qdc_ref[h], kdc_ref[h], g_ref[h])
                outs.append(o)
            o_ref[rows, :] = jnp.concatenate(outs, axis=1)
        for h in range(hr):
            s_ref[h] = states[h]

    row = pl.BlockSpec((sc * CHUNK, d), lambda n: (n, 0))
    return Part(
        body, [qr, kr, proj, dm, qdc, kdc, gch],
        [row, row, pl.BlockSpec((sc * CHUNK, d), lambda n: (n, RV_BLOCK)), _full(dm), _full(qdc), _full(kdc), _full(gch)],
        [jax.ShapeDtypeStruct((cfg.rp, d), F32), jax.ShapeDtypeStruct((nch, hr, RET_DK, RET_DK), F32)],
        [row, pl.BlockSpec((sc, hr, RET_DK, RET_DK), lambda n: (n, 0, 0, 0))],
        [pltpu.VMEM((hr, RET_DK, RET_DK), F32)])


def _ret_scan_bwd(cfg, do, qr, kr, proj, consts, ss, dproj):
    d, hr, nch, sc = cfg.d, cfg.hr, cfg.nch, cfg.sc
    nst = nch // sc
    dm, qdc, kdc, gch = consts

    def body(do_ref, q_ref, k_ref, v_ref, dm_ref, qdc_ref, kdc_ref, g_ref, ss_ref, _, dq_ref, dk_ref, dv_ref, ds_ref):
        @pl.when(pl.program_id(0) == 0)
        def _():
            ds_ref[...] = jnp.zeros(ds_ref.shape, F32)

        dstates = [ds_ref[h] for h in range(hr)]
        for j in reversed(range(sc)):
            rows = slice(j * CHUNK, (j + 1) * CHUNK)
            dqs, dks, dvs = [], [], []
            for h in range(hr):
                sl = slice(h * RET_DK, (h + 1) * RET_DK)
                cs = (dm_ref[h], qdc_ref[h], kdc_ref[h], g_ref[h])
                _, vjp_fn = jax.vjp(lambda s, q, k, v, cs=cs: _ret_step(s, q, k, v, *cs),
                                    ss_ref[j, h], q_ref[rows, sl].astype(F32), k_ref[rows, sl].astype(F32), v_ref[rows, sl])
                dstates[h], dq, dk, dv = vjp_fn((dstates[h], do_ref[rows, sl]))
                dqs.append(dq)
                dks.append(dk)
                dvs.append(dv)
            dq_ref[rows, :] = jnp.concatenate(dqs, axis=1).astype(dq_ref.dtype)
            dk_ref[rows, :] = jnp.concatenate(dks, axis=1).astype(dk_ref.dtype)
            dv_ref[rows, :] = jnp.concatenate(dvs, axis=1).astype(dv_ref.dtype)
        for h in range(hr):
            ds_ref[h] = dstates[h]

    row = pl.BlockSpec((sc * CHUNK, d), lambda n: (nst - 1 - n, 0))
    rowshape = jax.ShapeDtypeStruct((cfg.rp, d), BF16)
    vspec = pl.BlockSpec((sc * CHUNK, d), lambda n: (nst - 1 - n, RV_BLOCK))
    return Part(
        body, [do, qr, kr, proj, dm, qdc, kdc, gch, ss, dproj],
        [row, row, row, vspec, _full(dm), _full(qdc), _full(kdc), _full(gch),
         pl.BlockSpec((sc, hr, RET_DK, RET_DK), lambda n: (nst - 1 - n, 0, 0, 0)), ANY],
        [rowshape, rowshape, jax.ShapeDtypeStruct(dproj.shape, dproj.dtype)],
        [row, row, vspec],
        [pltpu.VMEM((hr, RET_DK, RET_DK), F32)], {9: 2})


def _final(cfg, h2, normf, tgt):
    d, tr = cfg.d, cfg.xrow
    nr = cfg.rp // tr

    def body(h_ref, g_ref, t_ref, dh_ref, dg_ref, loss_ref):
        i = pl.program_id(0)
        y, vjp_fn = jax.vjp(_rms, h_ref[...], g_ref[...])
        err = jnp.where(i >= 1, y - t_ref[...], 0.0)
        dh, dg = vjp_fn(err * (1.0 / d))
        dh_ref[...] = dh
        part = jnp.zeros((8, LANES), F32) + 0.5 * jnp.sum(err * err) * (1.0 / d)

        @pl.when(i == 0)
        def _():
            dg_ref[...] = dg
            loss_ref[...] = part

        @pl.when(i > 0)
        def _():
            dg_ref[...] += dg
            loss_ref[...] += part

    return _pallas(
        body,
        out_shape=[jax.ShapeDtypeStruct((cfg.rp, d), F32), jax.ShapeDtypeStruct((1, d), F32),
                   jax.ShapeDtypeStruct((8, LANES), F32)],
        grid=(nr,),
        in_specs=[_rows(tr, d), _full(normf), pl.BlockSpec((tr, d), lambda i: (jnp.maximum(i - 1, 0), 0))],
        out_specs=[_rows(tr, d), pl.BlockSpec((1, d), lambda i: (0, 0)), pl.BlockSpec((8, LANES), lambda i: (0, 0))],
        name="final_loss", compiler_params=_params((nr,)))(h2, normf, tgt)


ANY = pl.BlockSpec(memory_space=pl.ANY)


def _place():
    x, y, c = lax.axis_index("x"), lax.axis_index("y"), lax.axis_index("c")
    others = [(1 - x, y), (x, 1 - y), (1 - x, 1 - y)]
    return x, y, c, others


def _row_tile(rows, cap=256):
    return max(t for t in range(16, min(rows, cap) + 1, 16) if rows % t == 0)


class Comm(NamedTuple):
    ins: list
    outs: list
    sems: list
    start: object
    finish: object


def _run_comm(name, comm):
    n_in, n_out = len(comm.ins), len(comm.outs)

    def body(*refs):
        ins, outs, sems = refs[:n_in], refs[n_in:n_in + n_out], refs[n_in + n_out:]
        comm.start(ins, outs, sems)
        comm.finish(ins, outs, sems)

    return _pallas(body, out_shape=comm.outs, in_specs=[ANY] * n_in, out_specs=[ANY] * n_out,
                   scratch_shapes=comm.sems, name=name)(*comm.ins)


def _gather_comm(ws):
    n = len(ws)
    halves = [w.shape[0] // 2 for w in ws]

    def copies(w_refs, o_refs, sems):
        send_sems, recv_sems = sems
        x, y, c, others = _place()
        me = 2 * x + y
        chips = [2 * px + py for px, py in others]

        def piece(a, chip, core):
            return o_refs[a].at[chip, pl.ds(core * halves[a], halves[a]), :]

        def copy(a, k, src, chip, core, to):
            return pltpu.make_async_remote_copy(src_ref=src, dst_ref=piece(a, chip, core), send_sem=send_sems.at[6 * a + k],
                                                recv_sem=recv_sems.at[6 * a + k], device_id=to, device_id_type=MESH)

        def first(j, a):
            return copy(a, j, w_refs[a].at[pl.ds(c * halves[a], halves[a]), :], me, c, (*others[j], c))

        def landed(j, a):
            return copy(a, j, piece(a, chips[j], c), chips[j], c, (x, y, c))

        def passed(j, a):
            return copy(a, 3 + j, piece(a, chips[j], c), chips[j], c, (x, y, 1 - c))

        def from_sibling(j, a):
            return copy(a, 3 + j, piece(a, chips[j], 1 - c), chips[j], 1 - c, (x, y, c))

        return first, landed, passed, from_sibling

    pairs = [(j, a) for j in range(3) for a in range(n)]

    def start(w_refs, o_refs, sems):
        first, _, _, _ = copies(w_refs, o_refs, sems)
        for j, a in pairs:
            first(j, a).start()

    def finish(w_refs, o_refs, sems):
        first, landed, passed, from_sibling = copies(w_refs, o_refs, sems)
        for j, a in pairs:
            landed(j, a).wait_recv()
            passed(j, a).start()
        for j, a in pairs:
            from_sibling(j, a).wait_recv()
        for j, a in pairs:
            first(j, a).wait_send()
            passed(j, a).wait_send()

    return Comm(list(ws), [jax.ShapeDtypeStruct((N_CHIPS,) + w.shape, w.dtype) for w in ws],
                [pltpu.SemaphoreType.DMA((6 * n,)), pltpu.SemaphoreType.DMA((6 * n,))], start, finish)


def _pair_exchange(name, gs):
    n = len(gs)

    def body(*refs):
        g_refs, o_refs = refs[:n], refs[n:2 * n]
        send_sems, recv_sems = refs[2 * n:]
        x, y, c, _ = _place()
        cps = []
        for a in range(n):
            half = gs[a].shape[1] // 2
            cp = pltpu.make_async_remote_copy(
                src_ref=g_refs[a].at[:, pl.ds((1 - c) * half, half), :], dst_ref=o_refs[a], send_sem=send_sems.at[a],
                recv_sem=recv_sems.at[a], device_id=(x, y, 1 - c), device_id_type=MESH)
            cp.start()
            cps.append(cp)
        for cp in cps:
            cp.wait()

    return _pallas(
        body, out_shape=[jax.ShapeDtypeStruct((N_CHIPS, g.shape[1] // 2, g.shape[2]), g.dtype) for g in gs],
        in_specs=[ANY] * n, out_specs=[ANY] * n,
        scratch_shapes=[pltpu.SemaphoreType.DMA((n,)), pltpu.SemaphoreType.DMA((n,))], name=name)(*gs)


def _pair_sum(name, g, recv, cidx):
    half, cols = recv.shape[1], recv.shape[2]
    tr = _row_tile(half)
    nblk = half // tr

    def body(c_ref, g_ref, r_ref, o_ref):
        o_ref[...] = (g_ref[...] + r_ref[...]).astype(o_ref.dtype)

    grid_spec = pltpu.PrefetchScalarGridSpec(
        num_scalar_prefetch=1, grid=(N_CHIPS, nblk),
        in_specs=[pl.BlockSpec((1, tr, cols), lambda s, i, c: (s, c[0] * nblk + i, 0)),
                  pl.BlockSpec((1, tr, cols), lambda s, i, c: (s, i, 0))],
        out_specs=pl.BlockSpec((1, tr, cols), lambda s, i, c: (s, i, 0)))
    return _pallas(
        body, out_shape=jax.ShapeDtypeStruct((N_CHIPS, half, cols), BF16), grid_spec=grid_spec,
        name=name, compiler_params=_params((N_CHIPS, nblk)))(cidx, g, recv)


def _exchange_comm(parts):
    n = len(parts)

    def copies(p_refs, o_refs, sems):
        send_sems, recv_sems = sems
        x, y, c, others = _place()
        me = 2 * x + y

        def copy(a, j, src_chip, dst_chip):
            px, py = others[j]
            return pltpu.make_async_remote_copy(
                src_ref=p_refs[a].at[src_chip], dst_ref=o_refs[a].at[dst_chip], send_sem=send_sems.at[3 * a + j],
                recv_sem=recv_sems.at[3 * a + j], device_id=(px, py, c), device_id_type=MESH)

        def send(j, a):
            return copy(a, j, 2 * others[j][0] + others[j][1], me)

        def arrival(j, a):
            return copy(a, j, me, 2 * others[j][0] + others[j][1])

        return send, arrival

    pairs = [(j, a) for j in range(3) for a in range(n)]

    def start(p_refs, o_refs, sems):
        send, _ = copies(p_refs, o_refs, sems)
        for j, a in pairs:
            send(j, a).start()

    def finish(p_refs, o_refs, sems):
        send, arrival = copies(p_refs, o_refs, sems)
        for j, a in pairs:
            arrival(j, a).wait_recv()
        for j, a in pairs:
            send(j, a).wait_send()

    return Comm(list(parts), [jax.ShapeDtypeStruct(p.shape, p.dtype) for p in parts],
                [pltpu.SemaphoreType.DMA((3 * n,)), pltpu.SemaphoreType.DMA((3 * n,))], start, finish)


def _chip_sum(name, part, slots, chip):
    half, cols = slots.shape[1], slots.shape[2]
    tr = _row_tile(half)

    def body(me_ref, p_ref, *rest):
        s_refs, o_ref = rest[:N_CHIPS], rest[N_CHIPS]
        own = p_ref[...].astype(F32)
        v = [jnp.where(me_ref[0] == k, own, s_refs[k][...].astype(F32)) for k in range(N_CHIPS)]
        o_ref[...] = ((v[0] + v[1]) + v[2]) + v[3]

    def slot_spec(k):
        return pl.BlockSpec((None, tr, cols), lambda i, me: (jnp.where(me[0] == k, (k + 1) % N_CHIPS, k), i, 0))

    grid_spec = pltpu.PrefetchScalarGridSpec(
        num_scalar_prefetch=1, grid=(half // tr,),
        in_specs=[pl.BlockSpec((None, tr, cols), lambda i, me: (me[0], i, 0))] + [slot_spec(k) for k in range(N_CHIPS)],
        out_specs=pl.BlockSpec((tr, cols), lambda i, me: (i, 0)))
    return _pallas(
        body, out_shape=jax.ShapeDtypeStruct((half, cols), F32), grid_spec=grid_spec,
        name=name, compiler_params=_params((half // tr,)))(chip, part, *([slots] * N_CHIPS))


def _pair_swap(fins):
    n = len(fins)

    def body(*refs):
        f_refs, o_refs = refs[:n], refs[n:2 * n]
        send_sems, recv_sems = refs[2 * n:]
        x, y, c, _ = _place()
        cps = [pltpu.make_async_remote_copy(src_ref=f_refs[a], dst_ref=o_refs[a], send_sem=send_sems.at[a],
                                            recv_sem=recv_sems.at[a], device_id=(x, y, 1 - c), device_id_type=MESH)
               for a in range(n)]
        for cp in cps:
            cp.start()
        for cp in cps:
            cp.wait()

    return _pallas(
        body, out_shape=[jax.ShapeDtypeStruct(f.shape, f.dtype) for f in fins], in_specs=[ANY] * n, out_specs=[ANY] * n,
        scratch_shapes=[pltpu.SemaphoreType.DMA((n,)), pltpu.SemaphoreType.DMA((n,))], name="grad_pair_swap")(*fins)


def _adamw(name, w, g_own, g_other, m, v, cidx):
    R, cols = w.shape[-2:]
    lead = (None,) * (w.ndim - 2)
    zeros = (0,) * (w.ndim - 2)
    half = R // 2
    tr = _row_tile(half, 128)
    nblk = half // tr
    c1 = 1.0 - ADAM_B1 ** ADAM_STEP
    c2 = 1.0 - ADAM_B2 ** ADAM_STEP

    def body(c_ref, w_ref, go_ref, gs_ref, m_ref, v_ref, g_ref, d_ref, nm_ref, nv_ref):
        mine = (pl.program_id(0) // nblk) == c_ref[0]
        gv = jnp.where(mine, go_ref[...], gs_ref[...])
        nm = ADAM_B1 * m_ref[...] + (1.0 - ADAM_B1) * gv
        nv = ADAM_B2 * v_ref[...] + (1.0 - ADAM_B2) * (gv * gv)
        g_ref[...] = gv
        d_ref[...] = -ADAM_LR * ((nm / c1) / (jnp.sqrt(nv / c2) + ADAM_EPS) + ADAM_WD * w_ref[...])
        nm_ref[...] = nm
        nv_ref[...] = nv

    spec = pl.BlockSpec(lead + (tr, cols), lambda i, c: zeros + (i, 0))
    hspec = pl.BlockSpec((tr, cols), lambda i, c: (i % nblk, 0))
    shape = jax.ShapeDtypeStruct(w.shape, F32)
    grid_spec = pltpu.PrefetchScalarGridSpec(num_scalar_prefetch=1, grid=(R // tr,),
                                             in_specs=[spec, hspec, hspec, spec, spec], out_specs=[spec] * 4)
    return _pallas(
        body, out_shape=[shape] * 4, grid_spec=grid_spec,
        name=name, compiler_params=_params((R // tr,)))(cidx, w, g_own, g_other, m, v)


PARAMS = (("meta", 1), ("norm1", None), ("w_in", 2), ("gdn_conv_w", 2), ("gdn_a_log", None), ("gdn_dt_bias", None),
          ("gdn_norm", None), ("w_out", 1), ("norm2", None), ("w_ffn_up", 2), ("ffn_conv_w", 2), ("ffn_conv_b", None),
          ("w_ffn_down", 1), ("norm_f", None))
BIG = ("w_in", "w_out", "w_ffn_up", "w_ffn_down")
PACK_ALIGN = 1024
PACK_ROWS_ALIGN = 32


def _pack(arrs, dtype):
    parts, total = [], 0
    for a in arrs:
        f = a.reshape(-1).astype(dtype)
        pad = (-f.shape[0]) % PACK_ALIGN
        parts.append(jnp.pad(f, (0, pad)) if pad else f)
        total += f.shape[0] + pad
    rows = total // LANES
    rpad = (-rows) % PACK_ROWS_ALIGN
    if rpad:
        parts.append(jnp.zeros((rpad * LANES,), dtype))
    return jnp.concatenate(parts).reshape(rows + rpad, LANES)


def _unpack(buf, shapes):
    flat = buf.reshape(-1)
    outs, off = [], 0
    for s in shapes:
        n = int(np.prod(s))
        outs.append(flat[off:off + n].reshape(s))
        off += n + (-n) % PACK_ALIGN
    return outs


def _split4(a, axis):
    n = a.shape[axis] // N_CHIPS
    return [lax.slice_in_dim(a, s * n, (s + 1) * n, axis=axis) for s in range(N_CHIPS)]


PROJ_ORDER = (3, 7, 8, 9, 0, 1, 2, 6, 4, 5)


def _reorder_w_in(w, cfg):
    d, hg = cfg.d, cfg.hg

    def block(k):
        off = k * d + (2 * hg if k >= 4 else 0)
        return w[:, off:off + d]

    tail = jnp.pad(w[:, 4 * d:4 * d + 2 * hg], ((0, 0), (0, LANES - 2 * hg)))
    return jnp.concatenate([block(k) for k in PROJ_ORDER] + [tail], axis=1)


def _restore_w_in(wr, cfg):
    d, hg = cfg.d, cfg.hg
    at = {k: i for i, k in enumerate(PROJ_ORDER)}
    block = lambda k: wr[:, at[k] * d:(at[k] + 1) * d]
    return jnp.concatenate([block(k) for k in range(4)] + [wr[:, 10 * d:10 * d + 2 * hg]] +
                           [block(k) for k in range(4, 10)], axis=1)


def _step(cfg, x, tgt, shard, m_shard, v_shard):
    d, hg, dff, rp, tr, tm = cfg.d, cfg.hg, cfg.dff, cfg.rp, cfg.tr, cfg.tm
    nrow = rp // tr
    assert cfg.tf * N_CHIPS == 2 * dff and cfg.din % N_CHIPS == 0
    cidx = lax.axis_index("c").astype(jnp.int32).reshape(1)
    chip = (2 * lax.axis_index("x") + lax.axis_index("y")).astype(jnp.int32).reshape(1)

    axis = dict(PARAMS)
    small = ("meta", "gdn_conv_w", "ffn_conv_w")
    small_shapes = [shard[n].shape for n in small]
    mine = [shard[n][0].astype(BF16) for n in BIG] + [_pack([shard[n] for n in small], F32)]

    def with_own(gathered, own):
        return [lax.dynamic_update_slice(g, w[None], (chip[0], 0, 0)) for g, w in zip(gathered, own)]

    g_in, g_small = _run_comm("weights_gather_first", _gather_comm([mine[0], mine[4]]))
    g_small, = with_own([g_small], [mine[4]])
    w_in_r = _reorder_w_in(jnp.concatenate([jnp.where(chip[0] == s, mine[0], g_in[s]) for s in range(N_CHIPS)], axis=1),
                           cfg)
    per_chip = [_unpack(g_small[s], small_shapes) for s in range(N_CHIPS)]
    full = {n: jnp.concatenate([per_chip[s][k] for s in range(N_CHIPS)], axis=axis[n]) for k, n in enumerate(small)}
    meta = full["meta"]
    gconv_w = full["gdn_conv_w"][0]
    fconv_w = full["ffn_conv_w"][0]
    norm1, norm2, gnorm = shard["norm1"], shard["norm2"], shard["gdn_norm"]
    normf = shard["norm_f"].reshape(1, d)
    fconv_b = shard["ffn_conv_b"]
    alog = jnp.pad(shard["gdn_a_log"], ((0, 7), (0, LANES - hg)))
    dtb = jnp.pad(shard["gdn_dt_bias"], ((0, 7), (0, LANES - hg)))

    h0 = jnp.concatenate([jnp.zeros((cfg.front, d), F32), meta, x], axis=0)
    half = RET_DK // 2
    pos = np.arange(rp, dtype=np.float32) - np.float32(cfg.front)
    inv = (np.float32(1.0) / np.float32(ROPE_BASE) ** (np.arange(half, dtype=np.float32) / np.float32(half))).astype(np.float32)
    ang = pos[:, None] * inv[None, :]
    cos, sin = jnp.asarray(np.cos(ang), F32), jnp.asarray(np.sin(ang), F32)
    rconsts = _ret_consts(cfg)

    tr_n = 3 * tr if rp % (3 * tr) == 0 else tr
    rms_f = _make_rms_fn(cfg, tr_n, False)
    rms_b = _make_rms_fn(cfg, tr_n, True)
    rowshape = jax.ShapeDtypeStruct((rp, d), F32)
    rspec = _rows(tr, d)
    nspec = _rows(tr_n, d)

    def rms_fwd(name, h, g):
        return _stage_fwd(name, rms_f, (rp // tr_n,), [In(h, nspec), In(g, _full(g))],
                          [jax.ShapeDtypeStruct((rp, d), BF16)], [nspec])[0]

    wm = 10 * d
    w_main, w_tail = w_in_r[:, :wm], w_in_r[:, wm:]
    tn_in = 2560 if wm % 2560 == 0 else LANES
    hn1 = rms_fwd("rms1_fwd", h0, norm1)
    proj, rest = _mm("proj_fwd", hn1, w_main, tm=tm, tn=tn_in, tk=d, out_dtype=BF16, comm=_gather_comm(mine[1:4]))
    ptail = _mm("proj_tail_fwd", hn1, w_tail, tm=tm, tn=LANES, tk=d)
    g_out, g_up, g_down = with_own(rest, mine[1:4])
    w_out = g_out.reshape(d, d)
    w_up = g_up
    w_up_t = jnp.swapaxes(g_up, 1, 2).reshape(2 * dff, d)
    w_down = g_down.reshape(dff, d)
    cqkv = _conv_fwd("gdn_conv_fwd", proj, CONV_COL * d, gconv_w, None, taps=GDN_CONV, width=3 * d, tr=tr, tc=d)
    prep_fn = _make_gdn_prep_fn(cfg, tr)
    prep_ins = [In(cqkv, _rows(tr, 3 * d), BF16), In(ptail, _rows(tr, LANES), BF16),
                In(alog, _full(alog), F32, True), In(dtb, _full(dtb), F32, True)]
    qn, kn, vv, bB, lB = _stage_fwd("gdn_prep_fwd", prep_fn, (nrow,), prep_ins, [rowshape] * 5, [rspec] * 5)

    trg = cfg.nb * CHUNK
    gi_grid = (rp // trg, hg)
    hspec = pl.BlockSpec((trg, GDN_DK), lambda i, h: (i, h))
    aspec = pl.BlockSpec((1, trg, CHUNK), lambda i, h: (h, i, 0))
    gspec = pl.BlockSpec((1, cfg.nb, 1, GDN_DK), lambda i, h: (h, i, 0, 0))
    intra_ins = [In(t, hspec, F32) for t in (qn, kn, vv, bB, lB)]
    ashape = jax.ShapeDtypeStruct((hg, rp, CHUNK), F32)
    intra_shapes = [rowshape, rowshape, ashape, rowshape, rowshape, jax.ShapeDtypeStruct((hg, cfg.nch, 1, GDN_DK), F32), ashape]
    intra_specs = [hspec, hspec, aspec, hspec, hspec, gspec, aspec]
    gu, gw, gattn, gqd, gkd, ggl, gtinv = _stage_fwd("gdn_intra_fwd", _gdn_intra_fn, gi_grid, intra_ins, intra_shapes,
                                                     intra_specs)
    rot_fn = _make_rot_fn(cfg)

    def rot_ins(dproj=None):
        return [In(proj, _rows(tr_n, 2 * d, ROT_COL // 2), BF16, galias=dproj, gshape=(rp, wm)),
                In(cos, _rows(tr_n, half)), In(sin, _rows(tr_n, half))]

    qr, kr = _stage_fwd("rot_fwd", rot_fn, (rp // tr_n,), rot_ins(), [jax.ShapeDtypeStruct((rp, d), BF16)] * 2, [nspec] * 2)
    nst = cfg.nch // cfg.sc
    oa, gss = _run_parts("gdn_scan_fwd", (nst,), [_gdn_scan_fwd(cfg, gu, gw, gattn, gqd, gkd, ggl)])[0]
    ob, rss = _run_parts("ret_scan_fwd", (nst,), [_ret_scan_fwd(cfg, qr, kr, proj, rconsts)])[0]

    mix_fn = _make_mix_fn(cfg)
    mix_ins = [In(oa, rspec, F32), In(ob, rspec, F32), In(proj, _rows(tr, 4 * d, MIX_COL // 4), BF16, gshape=(rp, wm)),
               In(gnorm, _full(gnorm), F32, True)]
    ymix = _stage_fwd("mix_fwd", mix_fn, (nrow,), mix_ins, [jax.ShapeDtypeStruct((rp, d), BF16)], [rspec])[0]
    h1 = _mm("out_proj_fwd", ymix, w_out, tm=tm, tn=d, tk=d, add=h0)

    hn2 = rms_fwd("rms2_fwd", h1, norm2)
    up = _mm("ffn_up_fwd", hn2, w_up, tm=tm, tn=cfg.tf, tk=d, out_dtype=BF16)
    uc = _conv_fwd("ffn_conv_fwd", up, 0, fconv_w, fconv_b, taps=FFN_CONV, width=2 * dff, tr=tr, tc=cfg.tf)
    tra = tr
    act_ins = [In(uc, _rows(tra, 2 * dff), BF16)]
    act_spec = _rows(tra, dff)
    act = _stage_fwd("ffn_act_fwd", _act_fn, (rp // tra,), act_ins, [jax.ShapeDtypeStruct((rp, dff), BF16)], [act_spec])[0]
    h2 = _mm("ffn_down_fwd", act, w_down, tm=tm, tn=d, tk=cfg.tf, add=h1)

    dh2, g_normf, loss_blk = _final(cfg, h2, normf, tgt)
    loss = lax.psum(loss_blk[0, 0], ("x", "y", "c"))

    g_w_down = _mm_tn("ffn_down_dw", act, dh2, tr=tm, tka=cfg.tf, tn=d)
    dact = _mm("ffn_down_dx", dh2, w_down.T, tm=tm, tn=cfg.tf, tk=d, out_dtype=BF16)
    duc, = _stage_bwd("ffn_act_bwd", _act_fn, (rp // tra,), act_ins, [(dact, act_spec)])
    dup, g_fconv_w, g_fconv_b = _conv_bwd("ffn_conv_bwd", up, 0, fconv_w, duc, taps=FFN_CONV, width=2 * dff,
                                          tr=tr, tc=cfg.tf, with_bias=True)
    g_w_up = _mm_tn("ffn_up_dw", hn2, dup, tr=tm, tka=d, tn=cfg.tf, blocked=True)

    def pair_reduce(tag, names, arrs):
        recvs = _pair_exchange("grad_pair_exchange_" + tag, arrs)
        return [_pair_sum("grad_pair_sum_" + n, g, r, cidx) for n, g, r in zip(names, arrs, recvs)]

    parts_ffn = pair_reduce("ffn", ["w_ffn_down", "w_ffn_up"], [g_w_down.reshape(N_CHIPS, dff // N_CHIPS, d), g_w_up])
    dhn2, slots_ffn = _mm("ffn_up_dx", dup, w_up_t, tm=tm, tn=d, tk=2 * cfg.tf, comm=_exchange_comm(parts_ffn))

    def rms_bwd(name, h, g, dhn, dres):
        ins = [In(h, nspec, F32), In(g, _full(g), F32, True)]
        return _stage_bwd(name, rms_b, (rp // tr_n,), ins, [(dhn, nspec), (dres, nspec)])

    dh1, g_norm2 = rms_bwd("rms2_bwd", h1, norm2, dhn2, dh2)
    g_w_out = _mm_tn("out_proj_dw", ymix, dh1, tr=tm, tka=d, tn=d)
    dymix = _mm("out_proj_dx", dh1, w_out.T, tm=tm, tn=d, tk=d)
    doa, dob, dproj, g_gnorm = _stage_bwd("mix_bwd", mix_fn, (nrow,), mix_ins, [(dymix, rspec)])

    dqr, dkr, dproj = _run_parts("ret_scan_bwd", (nst,), [_ret_scan_bwd(cfg, dob, qr, kr, proj, rconsts, rss, dproj)])[0]
    dproj, = _stage_bwd("rot_bwd", rot_fn, (rp // tr_n,), rot_ins(dproj), [(dqr, nspec), (dkr, nspec)])
    dgu, dgw, dgattn, dgqd, dgkd, dggl = _run_parts(
        "gdn_scan_bwd", (nst,), [_gdn_scan_bwd(cfg, doa, gu, gw, gattn, gqd, gkd, ggl, gss)])[0]

    intra_cots = [(dgu, hspec), (dgw, hspec), (dgattn, aspec), (dgqd, hspec), (dgkd, hspec), (dggl, gspec)]
    dqn, dkn, dvv, dbB, dlB = _stage_bwd("gdn_intra_bwd", _gdn_intra_fn, gi_grid, intra_ins + [In(gtinv, aspec)], intra_cots)
    dcqkv, dtail, g_alog, g_dtb = _stage_bwd(
        "gdn_prep_bwd", prep_fn, (nrow,), prep_ins, [(t, rspec) for t in (dqn, dkn, dvv, dbB, dlB)])
    dproj, g_gconv_w = _conv_bwd("gdn_conv_bwd", proj, CONV_COL * d, gconv_w, dcqkv, taps=GDN_CONV, width=3 * d,
                                 tr=tr, tc=d, with_bias=False, dx_into=dproj)
    g_w_in_r = jnp.concatenate([_mm_tn("proj_dw", hn1, dproj, tr=tm, tka=d, tn=tn_in),
                                _mm_tn("proj_tail_dw", hn1, dtail, tr=tm, tka=d, tn=LANES)], axis=1)
    g_in4 = jnp.stack(_split4(_restore_w_in(g_w_in_r, cfg), 1))
    parts_mix = pair_reduce("mix", ["w_out", "w_in"], [g_w_out.reshape(N_CHIPS, d // N_CHIPS, d), g_in4])
    dhn1_tail = _mm("proj_tail_dx", dtail, w_tail.T, tm=tm, tn=d, tk=LANES)
    dhn1, slots_mix = _mm("proj_dx", dproj, w_main.T, tm=tm, tn=d, tk=tn_in // 2 if tn_in > LANES else LANES, add=dhn1_tail,
                          comm=_exchange_comm(parts_mix))
    dh0, g_norm1 = rms_bwd("rms1_bwd", h0, norm1, dhn1, dh1)

    grad_x = dh0[cfg.xrow:]
    small_grads = {
        "meta": dh0[cfg.front:cfg.xrow], "norm1": g_norm1, "gdn_conv_w": g_gconv_w[None],
        "gdn_a_log": g_alog[0:1, :hg], "gdn_dt_bias": g_dtb[0:1, :hg], "gdn_norm": g_gnorm, "norm2": g_norm2,
        "ffn_conv_w": g_fconv_w[None], "ffn_conv_b": g_fconv_b, "norm_f": g_normf.reshape(d),
    }

    small_names = [n for n, _ in PARAMS if n not in BIG]
    g_small = jnp.stack([_pack([small_grads[n] if axis[n] is None else _split4(small_grads[n], axis[n])[s]
                                for n in small_names], F32) for s in range(N_CHIPS)])
    parts_small = pair_reduce("small", ["small"], [g_small])
    slots_small = _run_comm("grad_exchange_small", _exchange_comm(parts_small))
    tags = ["w_in", "w_out", "w_ffn_up", "w_ffn_down", "small"]
    parts = [parts_mix[1], parts_mix[0], parts_ffn[1], parts_ffn[0], parts_small[0]]
    slots = [slots_mix[1], slots_mix[0], slots_ffn[1], slots_ffn[0], slots_small[0]]
    fins = [_chip_sum("grad_chip_sum_" + t, p, s, chip) for t, p, s in zip(tags, parts, slots)]
    sibs = _pair_swap(fins)

    def flat2(a):
        return a.reshape(-1, a.shape[-1])

    outs = {}
    for k, t in enumerate(BIG):
        res = _adamw("adamw_" + t, flat2(shard[t]), fins[k], sibs[k], flat2(m_shard[t]), flat2(v_shard[t]), cidx)
        outs[t] = [r.reshape(shard[t].shape) for r in res]
    small_shapes_all = [shard[n].shape for n in small_names]
    pk = lambda src: _pack([src[n] for n in small_names], F32)
    res = _adamw("adamw_small", pk(shard), fins[4], sibs[4], pk(m_shard), pk(v_shard), cidx)
    for k, r in enumerate(res):
        for n, a in zip(small_names, _unpack(r, small_shapes_all)):
            outs.setdefault(n, [None] * 4)[k] = a
    names = [n for n, _ in PARAMS]
    return (loss, grad_x[None], *[outs[n][k] for k in range(4) for n in names])


def kernel(x, meta, norm1, w_in, gdn_conv_w, gdn_a_log, gdn_dt_bias, gdn_norm, w_out, norm2, w_ffn_up, ffn_conv_w, ffn_conv_b, w_ffn_down, norm_f, loss_target, m_meta, m_norm1, m_w_in, m_gdn_conv_w, m_gdn_a_log, m_gdn_dt_bias, m_gdn_norm, m_w_out, m_norm2, m_w_ffn_up, m_ffn_conv_w, m_ffn_conv_b, m_w_ffn_down, m_norm_f, v_meta, v_norm1, v_w_in, v_gdn_conv_w, v_gdn_a_log, v_gdn_dt_bias, v_gdn_norm, v_w_out, v_norm2, v_w_ffn_up, v_ffn_conv_w, v_ffn_conv_b, v_w_ffn_down, v_norm_f):
    names = [n for n, _ in PARAMS]
    shard = dict(zip(names, (meta, norm1, w_in, gdn_conv_w, gdn_a_log, gdn_dt_bias, gdn_norm, w_out, norm2, w_ffn_up,
                             ffn_conv_w, ffn_conv_b, w_ffn_down, norm_f)))
    m_shard = dict(zip(names, (m_meta, m_norm1, m_w_in, m_gdn_conv_w, m_gdn_a_log, m_gdn_dt_bias, m_gdn_norm, m_w_out,
                               m_norm2, m_w_ffn_up, m_ffn_conv_w, m_ffn_conv_b, m_w_ffn_down, m_norm_f)))
    v_shard = dict(zip(names, (v_meta, v_norm1, v_w_in, v_gdn_conv_w, v_gdn_a_log, v_gdn_dt_bias, v_gdn_norm, v_w_out,
                               v_norm2, v_w_ffn_up, v_ffn_conv_w, v_ffn_conv_b, v_w_ffn_down, v_norm_f)))
    return _step(REAL, x[0], loss_target[0], shard, m_shard, v_shard)
```

```python
import functools
from typing import NamedTuple

import numpy as np
import jax
import jax.numpy as jnp
from jax import lax
from jax.experimental import pallas as pl
from jax.experimental.pallas import tpu as pltpu

F32 = jnp.float32
BF16 = jnp.bfloat16
EPS = 1e-6
CHUNK = 64
GDN_DK = 128
RET_DK = 256
GDN_CONV = 4
FFN_CONV = 3
ROPE_BASE = 10000.0
LANES = 128
N_CHIPS = 4
ADAM_LR, ADAM_B1, ADAM_B2, ADAM_EPS, ADAM_WD, ADAM_STEP = 0.001, 0.9, 0.999, 1e-08, 0.01, 10
MIX_COL, CONV_COL, RV_BLOCK, ROT_COL = 0, 4, 7, 8
MESH = pl.DeviceIdType.MESH
VMEM_LIMIT = 56 * 1024 * 1024


class Cfg(NamedTuple):
    d: int
    seq: int
    n_meta: int
    dff: int
    tr: int
    nb: int
    tm: int
    tf: int
    sc: int

    @property
    def hg(self): return self.d // GDN_DK
    @property
    def hr(self): return self.d // RET_DK
    @property
    def L(self): return self.n_meta + self.seq
    @property
    def rp(self): return -(-self.L // 256) * 256
    @property
    def front(self): return self.rp - self.L
    @property
    def xrow(self): return self.rp - self.seq
    @property
    def nch(self): return self.rp // CHUNK
    @property
    def din(self): return 10 * self.d + 2 * self.hg


REAL = Cfg(d=1024, seq=8192, n_meta=16, dff=2816, tr=256, nb=12, tm=1408, tf=1408, sc=6)


def _pallas(body, **kw):
    return pl.pallas_call(body, **kw)


def _sigmoid_raw(x):
    return 1.0 / (1.0 + jnp.exp(-x))


@jax.custom_vjp
def _sigmoid(x):
    return _sigmoid_raw(x)


def _sigmoid_fwd(x):
    s = _sigmoid_raw(x)
    return s, s


def _sigmoid_bwd(s, g):
    return (g * (s * (1.0 - s)),)


_sigmoid.defvjp(_sigmoid_fwd, _sigmoid_bwd)


@jax.custom_vjp
def _silu(x):
    return x * _sigmoid_raw(x)


def _silu_fwd(x):
    s = _sigmoid_raw(x)
    return x * s, (x, s)


def _silu_bwd(res, g):
    x, s = res
    return (g * (s * (1.0 + x * (1.0 - s))),)


_silu.defvjp(_silu_fwd, _silu_bwd)


def _softplus(x):
    return jnp.maximum(x, 0.0) + jnp.log(1.0 + jnp.exp(-jnp.abs(x)))


def _raw_dot(a, b, ta, tb, hi):
    if not hi:
        a = a.astype(BF16)
        b = b.astype(BF16)
    nbatch = a.ndim - 2
    ca = a.ndim - 2 if ta else a.ndim - 1
    cb = b.ndim - 1 if tb else b.ndim - 2
    batch = tuple(range(nbatch))
    return lax.dot_general(a, b, (((ca,), (cb,)), (batch, batch)),
                           precision=lax.Precision.HIGHEST if hi else None,
                           preferred_element_type=F32)


@functools.partial(jax.custom_vjp, nondiff_argnums=(2, 3, 4))
def _dot_p(a, b, ta, tb, hi):
    return _raw_dot(a, b, ta, tb, hi)


def _dot(a, b, ta=False, tb=False, hi=False):
    return _dot_p(a, b, ta, tb, hi)


def _dot_fwd(a, b, ta, tb, hi):
    return _raw_dot(a, b, ta, tb, hi), (a, b)


def _dot_bwd(ta, tb, hi, res, g):
    a, b = res
    if not ta and not tb:
        da, db = _dot(g, b, False, True, hi), _dot(a, g, True, False, hi)
    elif not ta and tb:
        da, db = _dot(g, b, False, False, hi), _dot(g, a, True, False, hi)
    elif ta and not tb:
        da, db = _dot(b, g, False, True, hi), _dot(a, g, False, False, hi)
    else:
        raise NotImplementedError
    return da.astype(a.dtype), db.astype(b.dtype)


_dot_p.defvjp(_dot_fwd, _dot_bwd)


def _iota2(n, m, axis):
    return lax.broadcasted_iota(jnp.int32, (n, m), axis)


def _bcast(mat, nb):
    return jnp.broadcast_to(mat[None], (nb,) + mat.shape)


def _split3(a):
    a0 = a.astype(BF16)
    r1 = a - a0.astype(F32)
    a1 = r1.astype(BF16)
    return a0, a1, (r1 - a1.astype(F32)).astype(BF16)


@functools.partial(jax.custom_vjp, nondiff_argnums=(2,))
def _dot_sel(a, e, te):
    eb = e.astype(BF16)
    p0, p1, p2 = (_raw_dot(p, eb, False, te, False) for p in _split3(a))
    return p0 + (p1 + p2)


def _dot_sel_fwd(a, e, te):
    return _dot_sel(a, e, te), e


def _dot_sel_bwd(te, e, g):
    return _dot_sel(g, e, not te), jnp.zeros_like(e)


_dot_sel.defvjp(_dot_sel_fwd, _dot_sel_bwd)


@jax.custom_vjp
def _sel_dot(e, x):
    eb = e.astype(BF16)
    p0, p1, p2 = (_raw_dot(eb, p, False, False, False) for p in _split3(x))
    return p0 + (p1 + p2)


def _sel_dot_fwd(e, x):
    return _sel_dot(e, x), e


def _sel_dot_bwd(e, g):
    eb = e.astype(BF16)
    p0, p1, p2 = (_raw_dot(eb, p, True, False, False) for p in _split3(g))
    return jnp.zeros_like(e), p0 + (p1 + p2)


_sel_dot.defvjp(_sel_dot_fwd, _sel_dot_bwd)


def _tri_inv_raw(m):
    nb = m.shape[0]
    r, c = _iota2(CHUNK, CHUNK, 0), _iota2(CHUNK, CHUNK, 1)
    t = _bcast((r == c).astype(F32), nb)
    b = 1
    while b < CHUNK:
        sh = b.bit_length() - 1
        off = ((r >> (sh + 1)) == (c >> (sh + 1))) & ((r >> sh) != (c >> sh)) & (r > c)
        cl = jnp.where(off[None], m, 0.0)
        t = t - _raw_dot(_raw_dot(t, cl, False, False, False), t, False, False, False)
        b *= 2
    return t


@jax.custom_vjp
def _tri_inv_given(m, t):
    return t


def _tri_inv_fwd(m, t):
    return t, t


def _tri_inv_bwd(t, g):
    return -_raw_dot(_raw_dot(t, g, True, False, False), t, False, True, False), jnp.zeros_like(t)


_tri_inv_given.defvjp(_tri_inv_fwd, _tri_inv_bwd)


def _rms(h, g):
    return h * lax.rsqrt(jnp.mean(h * h, axis=-1, keepdims=True) + EPS) * g


class In(NamedTuple):
    arr: jax.Array
    spec: pl.BlockSpec
    grad: object = None
    acc: bool = False
    gshape: object = None
    gspec: object = None
    galias: object = None


def _params(grid):
    sem = ("arbitrary",) * len(grid)
    return pltpu.CompilerParams(dimension_semantics=sem, vmem_limit_bytes=VMEM_LIMIT)


def _stage_fwd(name, fn, grid, ins, out_shapes, out_specs):
    n_in = len(ins)

    def body(*refs):
        pids = tuple(pl.program_id(k) for k in range(len(grid)))
        vals = [r[...].astype(F32) for r in refs[:n_in]]
        outs = fn(pids, *vals)
        for o_ref, o in zip(refs[n_in:], outs):
            o_ref[...] = o.reshape(o_ref.shape).astype(o_ref.dtype)

    return _pallas(
        body, out_shape=out_shapes, grid=grid, in_specs=[i.spec for i in ins],
        out_specs=out_specs, name=name, compiler_params=_params(grid))(*[i.arr for i in ins])


def _stage_bwd(name, fn, grid, ins, cots):
    n_in, n_ct = len(ins), len(cots)
    didx = [k for k, i in enumerate(ins) if i.grad is not None]
    aliased = [(o, ins[k].galias) for o, k in enumerate(didx) if ins[k].galias is not None]
    n_al = len(aliased)

    def body(*refs):
        pids = tuple(pl.program_id(k) for k in range(len(grid)))
        vals = [r[...].astype(F32) for r in refs[:n_in]]
        ct_refs = refs[n_in:n_in + n_ct]
        g_refs = refs[n_in + n_ct + n_al:]

        def f(*dv):
            merged = list(vals)
            for k, v in zip(didx, dv):
                merged[k] = v
            return tuple(fn(pids, *merged))

        outs, vjp_fn = jax.vjp(f, *[vals[k].astype(F32) for k in didx])
        cts = tuple(c[...].reshape(o.shape).astype(F32) for c, o in zip(ct_refs, outs))
        grads = vjp_fn(cts)
        first = functools.reduce(jnp.logical_and, [p == 0 for p in pids])
        for k, g_ref, g in zip(didx, g_refs, grads):
            if ins[k].acc:
                @pl.when(first)
                def _(g_ref=g_ref):
                    g_ref[...] = jnp.zeros(g_ref.shape, g_ref.dtype)
                g_ref[...] += g.reshape(g_ref.shape).astype(g_ref.dtype)
            else:
                g_ref[...] = g.reshape(g_ref.shape).astype(g_ref.dtype)

    out_shapes = [jax.ShapeDtypeStruct(ins[k].gshape or ins[k].arr.shape, ins[k].grad) for k in didx]
    out_specs = [ins[k].gspec or ins[k].spec for k in didx]
    return _pallas(
        body, out_shape=out_shapes, grid=grid,
        in_specs=[i.spec for i in ins] + [c[1] for c in cots] + [ANY] * n_al, out_specs=out_specs,
        input_output_aliases={n_in + n_ct + a: o for a, (o, _) in enumerate(aliased)},
        name=name, compiler_params=_params(grid))(*[i.arr for i in ins], *[c[0] for c in cots], *[a for _, a in aliased])


def _full(arr):
    nd = arr.ndim
    return pl.BlockSpec(arr.shape, lambda *p: (0,) * nd)


def _rows(tr, width, blk=0):
    return pl.BlockSpec((tr, width), lambda i: (i, blk))


def _mm(name, a, b, *, tm, tn, tk, out_dtype=F32, add=None, comm=None):
    M, K = a.shape
    N = b.shape[1] if b.ndim == 2 else b.shape[0] * b.shape[2]
    nk = K // tk
    grid = (M // tm, N // tn, nk)
    n_in = 3 if add is not None else 2
    n_ci, n_co = (len(comm.ins), len(comm.outs)) if comm is not None else (0, 0)

    def body(*refs):
        a_ref, b_ref = refs[0], refs[1]
        add_ref = refs[2] if add is not None else None
        c_ins = refs[n_in:n_in + n_ci]
        o_ref = refs[n_in + n_ci]
        c_outs = refs[n_in + n_ci + 1:n_in + n_ci + 1 + n_co]
        scratch = refs[n_in + n_ci + 1 + n_co:]
        acc_ref = scratch[0] if nk > 1 else None
        sems = scratch[1 if nk > 1 else 0:]
        step = (pl.program_id(0) * grid[1] + pl.program_id(1)) * nk + pl.program_id(2)
        if comm is not None:
            @pl.when(step == 0)
            def _():
                comm.start(c_ins, c_outs, sems)

        part = _raw_dot(a_ref[...], b_ref[...], False, False, False)

        def finish(total):
            if add_ref is not None:
                total = total + add_ref[...]
            o_ref[...] = total.astype(o_ref.dtype)

        if nk == 1:
            finish(part)
        else:
            k = pl.program_id(2)

            @pl.when(k == 0)
            def _():
                acc_ref[...] = part

            @pl.when(k > 0)
            def _():
                acc_ref[...] += part

            @pl.when(k == nk - 1)
            def _():
                finish(acc_ref[...])

        if comm is not None:
            @pl.when(step == grid[0] * grid[1] * nk - 1)
            def _():
                comm.finish(c_ins, c_outs, sems)

    b_spec = (pl.BlockSpec((tk, tn), lambda i, j, k: (k, j)) if b.ndim == 2 else
              pl.BlockSpec((None, tk, tn), lambda i, j, k: (j, k, 0)))
    in_specs = [pl.BlockSpec((tm, tk), lambda i, j, k: (i, k)), b_spec]
    args = [a, b]
    if add is not None:
        in_specs.append(pl.BlockSpec((tm, tn), lambda i, j, k: (i, j)))
        args.append(add)
    out_shape = jax.ShapeDtypeStruct((M, N), out_dtype)
    out_spec = pl.BlockSpec((tm, tn), lambda i, j, k: (i, j))
    scratch = [pltpu.VMEM((tm, tn), F32)] if nk > 1 else []
    if comm is None:
        return _pallas(body, out_shape=out_shape, grid=grid, in_specs=in_specs, out_specs=out_spec,
                       scratch_shapes=scratch, name=name, compiler_params=_params(grid))(*args)
    res = _pallas(body, out_shape=[out_shape] + comm.outs, grid=grid, in_specs=in_specs + [ANY] * n_ci,
                  out_specs=[out_spec] + [ANY] * n_co, scratch_shapes=scratch + comm.sems, name=name,
                  compiler_params=_params(grid))(*args, *comm.ins)
    return res[0], res[1:]


def _mm_tn(name, a, b, *, tr, tka, tn, blocked=False):
    R, Ka = a.shape
    N = b.shape[1]
    nr = R // tr
    grid = (Ka // tka, N // tn, nr)
    if blocked:
        out_shape = jax.ShapeDtypeStruct((N // tn, Ka, tn), F32)
        out_spec = pl.BlockSpec((None, tka, tn), lambda i, j, r: (j, i, 0))
    else:
        out_shape = jax.ShapeDtypeStruct((Ka, N), F32)
        out_spec = pl.BlockSpec((tka, tn), lambda i, j, r: (i, j))

    def body(a_ref, b_ref, o_ref):
        r = pl.program_id(2)
        part = _raw_dot(a_ref[...], b_ref[...], True, False, False)

        @pl.when(r == 0)
        def _():
            o_ref[...] = part

        @pl.when(r > 0)
        def _():
            o_ref[...] += part

    return _pallas(
        body, out_shape=out_shape, grid=grid,
        in_specs=[pl.BlockSpec((tr, tka), lambda i, j, r: (r, i)),
                  pl.BlockSpec((tr, tn), lambda i, j, r: (r, j))],
        out_specs=out_spec, name=name, compiler_params=_params(grid))(a, b)


def _conv_fwd(name, x, xcol0, w, b, *, taps, width, tr, tc):
    R = x.shape[0]
    grid = (width // tc, R // tr)
    cb0 = xcol0 // tc
    hrows = 16 if x.dtype == BF16 else 8
    hb = tr // hrows

    def body(*refs):
        x_ref, xp_ref, w_ref = refs[:3]
        b_ref = refs[3] if b is not None else None
        o_ref = refs[-1]
        i = pl.program_id(1)
        xv = x_ref[...].astype(F32)
        prev = jnp.where(i > 0, xp_ref[...].astype(F32)[hrows - 8:, :], 0.0)
        ext = jnp.concatenate([prev, xv], axis=0)
        acc = xv * w_ref[taps - 1:taps, :]
        for s in range(1, taps):
            acc = acc + pltpu.roll(ext, s, 0)[8:, :] * w_ref[taps - 1 - s:taps - s, :]
        if b_ref is not None:
            acc = acc + b_ref[...]
        o_ref[...] = acc.astype(o_ref.dtype)

    in_specs = [pl.BlockSpec((tr, tc), lambda j, i: (i, cb0 + j)),
                pl.BlockSpec((hrows, tc), lambda j, i: (jnp.maximum(i * hb - 1, 0), cb0 + j)),
                pl.BlockSpec((taps, tc), lambda j, i: (0, j))]
    args = [x, x, w]
    if b is not None:
        in_specs.append(pl.BlockSpec((1, tc), lambda j, i: (0, j)))
        args.append(b)
    return _pallas(
        body, out_shape=jax.ShapeDtypeStruct((R, width), BF16), grid=grid, in_specs=in_specs,
        out_specs=pl.BlockSpec((tr, tc), lambda j, i: (i, j)),
        name=name, compiler_params=_params(grid))(*args)


def _conv_bwd(name, x, xcol0, w, dy, *, taps, width, tr, tc, with_bias, dx_into=None):
    R = x.shape[0]
    nr = R // tr
    grid = (width // tc, nr)
    cb0 = xcol0 // tc
    hrows = 16 if dy.dtype == BF16 else 8
    hb = tr // hrows
    n_ext = tr + 8
    n_al = 0 if dx_into is None else 1

    def body(*refs):
        x_ref, w_ref, dy_ref, dyn_ref = refs[:4]
        dx_ref, dw_ref = refs[4 + n_al], refs[5 + n_al]
        db_ref = refs[6 + n_al] if with_bias else None
        i = pl.program_id(1)
        xv = x_ref[...].astype(F32)
        dyv = dy_ref[...].astype(F32)
        nxt = dyn_ref[...].astype(F32)[:8, :]
        dext = jnp.concatenate([dyv, jnp.where(i < nr - 1, nxt, 0.0)], axis=0)
        dx = dyv * w_ref[taps - 1:taps, :]
        dws = [None] * taps
        dws[taps - 1] = jnp.sum(xv * dyv, axis=0, keepdims=True)
        for s in range(1, taps):
            ahead = pltpu.roll(dext, n_ext - s, 0)[:tr, :]
            dx = dx + ahead * w_ref[taps - 1 - s:taps - s, :]
            dws[taps - 1 - s] = jnp.sum(xv * ahead, axis=0, keepdims=True)
        dx_ref[...] = dx.astype(dx_ref.dtype)

        @pl.when(i == 0)
        def _():
            for k in range(taps):
                dw_ref[k:k + 1, :] = dws[k]
            if db_ref is not None:
                db_ref[...] = jnp.sum(dyv, axis=0, keepdims=True)

        @pl.when(i > 0)
        def _():
            for k in range(taps):
                dw_ref[k:k + 1, :] += dws[k]
            if db_ref is not None:
                db_ref[...] += jnp.sum(dyv, axis=0, keepdims=True)

    in_specs = [pl.BlockSpec((tr, tc), lambda j, i: (i, cb0 + j)),
                pl.BlockSpec((taps, tc), lambda j, i: (0, j)),
                pl.BlockSpec((tr, tc), lambda j, i: (i, j)),
                pl.BlockSpec((hrows, tc), lambda j, i: (jnp.minimum((i + 1) * hb, R // hrows - 1), j))]
    args = [x, w, dy, dy]
    if dx_into is None:
        dx_shape, dx_spec, aliases = jax.ShapeDtypeStruct((R, width), BF16), pl.BlockSpec((tr, tc), lambda j, i: (i, j)), {}
    else:
        dx_shape = jax.ShapeDtypeStruct(dx_into.shape, dx_into.dtype)
        dx_spec, aliases = pl.BlockSpec((tr, tc), lambda j, i: (i, cb0 + j)), {4: 0}
        in_specs.append(ANY)
        args.append(dx_into)
    out_shape = [dx_shape, jax.ShapeDtypeStruct((taps, width), F32)]
    out_specs = [dx_spec, pl.BlockSpec((taps, tc), lambda j, i: (0, j))]
    if with_bias:
        out_shape.append(jax.ShapeDtypeStruct((1, width), F32))
        out_specs.append(pl.BlockSpec((1, tc), lambda j, i: (0, j)))
    return _pallas(
        body, out_shape=out_shape, grid=grid, in_specs=in_specs, out_specs=out_specs, input_output_aliases=aliases,
        name=name, compiler_params=_params(grid))(*args)


def _row_mask(cfg, i, tr):
    rows = i * tr + lax.broadcasted_iota(jnp.int32, (tr, 1), 0)
    return (rows >= cfg.front).astype(F32)


def _make_rms_fn(cfg, tr, with_residual):
    def fn(pids, h, g):
        hm = h * _row_mask(cfg, pids[0], tr)
        if with_residual:
            return _rms(hm, g), hm
        return (_rms(hm, g),)
    return fn


def _make_gdn_prep_fn(cfg, tr):
    d, hg = cfg.d, cfg.hg

    def fn(pids, c, tail, alog, dtb):
        cq, ck, cv = c[:, :d], c[:, d:2 * d], c[:, 2 * d:]
        mask = _row_mask(cfg, pids[0], tr)
        j, col = _iota2(LANES, d, 0), _iota2(LANES, d, 1)
        ea = ((col >> 7) == j).astype(F32)
        eb = ((col >> 7) + hg == j).astype(F32)
        al = jnp.sum(alog, axis=0, keepdims=True)
        db = jnp.sum(dtb, axis=0, keepdims=True)
        lg = _dot_sel(-jnp.exp(al) * _softplus(tail + db) * mask, ea, False)
        beta = _dot_sel(_sigmoid(tail) * mask, eb, False)
        sq, sk, sv = _silu(cq), _silu(ck), _silu(cv)
        qs, ks = [], []
        for h in range(hg):
            sl = slice(h * GDN_DK, (h + 1) * GDN_DK)
            qh, kh = sq[:, sl], sk[:, sl]
            qs.append(qh * lax.rsqrt(jnp.sum(qh * qh, axis=-1, keepdims=True) + EPS) * (GDN_DK ** -0.5))
            ks.append(kh * lax.rsqrt(jnp.sum(kh * kh, axis=-1, keepdims=True) + EPS))
        return jnp.concatenate(qs, axis=1), jnp.concatenate(ks, axis=1), sv, beta, lg
    return fn


def _gdn_intra_fn(pids, q, k, v, bB, lB, t_saved=None):
    rows = q.shape[0]
    nb = rows // CHUNK
    q3, k3, v3, b3, l3 = [t.reshape(nb, CHUNK, GDN_DK) for t in (q, k, v, bB, lB)]
    r, c = _iota2(CHUNK, CHUNK, 0), _iota2(CHUNK, CHUNK, 1)
    tril = (r >= c)
    strict = (r > c)
    gcol = _sel_dot(_bcast(tril.astype(F32), nb), l3)
    grow = jnp.swapaxes(gcol, 1, 2)[:, :CHUNK, :]
    diff = gcol[:, :, :CHUNK] - grow
    decay = jnp.where(tril[None], jnp.exp(jnp.where(tril[None], diff, 0.0)), 0.0)
    kb = k3 * b3
    m = jnp.where(strict[None], _dot(kb, k3, False, True) * decay, 0.0)
    t = _tri_inv_raw(m) if t_saved is None else _tri_inv_given(m, t_saved.reshape(nb, CHUNK, CHUNK))
    eg = jnp.exp(gcol)
    u = _dot(t, v3 * b3)
    w = _dot(t, kb * eg)
    attn = _dot(q3, k3, False, True) * decay
    qd = q3 * eg
    glast = jnp.sum(l3, axis=1, keepdims=True)
    kd = k3 * jnp.exp(glast - gcol)
    gl = jnp.exp(glast)
    outs = (u.reshape(rows, GDN_DK), w.reshape(rows, GDN_DK), attn.reshape(1, rows, CHUNK),
            qd.reshape(rows, GDN_DK), kd.reshape(rows, GDN_DK), gl.reshape(1, nb, 1, GDN_DK))
    return outs + (t.reshape(1, rows, CHUNK),) if t_saved is None else outs


def _make_rot_fn(cfg):
    hr = cfg.hr
    half = RET_DK // 2

    def fn(pids, rqk, cos, sin):
        rq, rk = rqk[:, :cfg.d], rqk[:, cfg.d:]

        def rot(t, scale):
            outs = []
            for h in range(hr):
                x1 = t[:, h * RET_DK:h * RET_DK + half]
                x2 = t[:, h * RET_DK + half:(h + 1) * RET_DK]
                outs += [(x1 * cos - x2 * sin) * scale, (x2 * cos + x1 * sin) * scale]
            return jnp.concatenate(outs, axis=1)
        return rot(rq, 1.0), rot(rk, RET_DK ** -0.5)
    return fn


def _make_mix_fn(cfg):
    hg, hr = cfg.hg, cfg.hr

    def fn(pids, oa, ob, pm, gnorm):
        d = cfg.d
        gz, rg, gate_a, gate_b = pm[:, :d], pm[:, d:2 * d], pm[:, 2 * d:3 * d], pm[:, 3 * d:]
        oas = []
        for h in range(hg):
            oh = oa[:, h * GDN_DK:(h + 1) * GDN_DK]
            oas.append(oh * lax.rsqrt(jnp.mean(oh * oh, axis=-1, keepdims=True) + EPS) * gnorm)
        ya = jnp.concatenate(oas, axis=1) * _silu(gz)
        obs = []
        for h in range(hr):
            oh = ob[:, h * RET_DK:(h + 1) * RET_DK]
            obs.append(oh * lax.rsqrt(jnp.mean(oh * oh, axis=-1, keepdims=True) + EPS))
        yb = _silu(rg) * jnp.concatenate(obs, axis=1)
        return (_sigmoid(gate_a) * ya + _sigmoid(gate_b) * yb,)
    return fn


def _act_fn(pids, u):
    f = u.shape[1] // 2
    return (_silu(u[:, :f]) * u[:, f:],)


def _gdn_step(s, u, w, a, qd, kd, gl):
    top = _dot(jnp.concatenate([w, qd], axis=0), s)
    v_new = u - top[:CHUNK]
    bot = _dot(jnp.concatenate([a, kd.T], axis=0), v_new)
    o = top[CHUNK:] + bot[:CHUNK]
    s2 = s * gl + bot[CHUNK:]
    return s2, o


def _gdn_step_bwd(s, u, w, a, qd, kd, gl, ds2, do):
    lw = jnp.concatenate([w, qd], axis=0)
    v_new = u - _raw_dot(w, s, False, False, False)
    dv = _raw_dot(a, do, True, False, False) + _raw_dot(kd, ds2, False, False, False)
    da = _raw_dot(do, v_new, False, True, False)
    dkd = _raw_dot(v_new, ds2, False, True, False)
    dtop = jnp.concatenate([-dv, do], axis=0)
    dlw = _raw_dot(dtop, s, False, True, False)
    ds = ds2 * gl + _raw_dot(lw, dtop, True, False, False)
    dgl = jnp.sum(ds2 * s, axis=0, keepdims=True)
    return ds, dv, dlw[:CHUNK], da, dlw[CHUNK:], dkd, dgl


def _ret_step(s, q, k, v, dm, qdc, kdc, g):
    att = _dot(q, k, False, True) * dm
    bot = _dot(jnp.concatenate([att, (k * kdc).T], axis=0), v)
    o = bot[:CHUNK] + _dot(q * qdc, s)
    s2 = s * g + bot[CHUNK:]
    return s2, o


class Part(NamedTuple):
    body: object
    args: list
    in_specs: list
    out_shape: list
    out_specs: list
    scratch: list
    aliases: dict = {}


def _run_parts(name, grid, parts):
    n_in = [len(p.args) for p in parts]
    n_out = [len(p.out_shape) for p in parts]
    n_sc = [len(p.scratch) for p in parts]
    off_in = [sum(n_in[:k]) for k in range(len(parts))]
    off_out = [sum(n_out[:k]) for k in range(len(parts))]
    off_sc = [sum(n_sc[:k]) for k in range(len(parts))]

    def body(*refs):
        ins, outs, scr = refs[:sum(n_in)], refs[sum(n_in):sum(n_in) + sum(n_out)], refs[sum(n_in) + sum(n_out):]
        for k, p in enumerate(parts):
            p.body(*ins[off_in[k]:off_in[k] + n_in[k]], *outs[off_out[k]:off_out[k] + n_out[k]],
                   *scr[off_sc[k]:off_sc[k] + n_sc[k]])

    aliases = {off_in[k] + i: off_out[k] + o for k, p in enumerate(parts) for i, o in p.aliases.items()}
    res = _pallas(
        body, out_shape=sum((p.out_shape for p in parts), []), grid=grid, in_specs=sum((p.in_specs for p in parts), []),
        out_specs=sum((p.out_specs for p in parts), []), scratch_shapes=sum((p.scratch for p in parts), []),
        input_output_aliases=aliases, name=name, compiler_params=_params(grid))(*sum((p.args for p in parts), []))
    return [res[off_out[k]:off_out[k] + n_out[k]] for k in range(len(parts))]


def _gdn_scan_fwd(cfg, u, w, attn, qd, kd, gl):
    d, hg, nch, sc = cfg.d, cfg.hg, cfg.nch, cfg.sc
    nst = nch // sc

    def body(u_ref, w_ref, a_ref, qd_ref, kd_ref, gl_ref, o_ref, ss_ref, s_ref):
        @pl.when(pl.program_id(0) == 0)
        def _():
            s_ref[...] = jnp.zeros(s_ref.shape, F32)

        states = [s_ref[h] for h in range(hg)]
        for j in range(sc):
            rows = slice(j * CHUNK, (j + 1) * CHUNK)
            outs = []
            for h in range(hg):
                sl = slice(h * GDN_DK, (h + 1) * GDN_DK)
                ss_ref[j, h] = states[h]
                states[h], o = _gdn_step(states[h], u_ref[rows, sl], w_ref[rows, sl], a_ref[h, rows, :],
                                         qd_ref[rows, sl], kd_ref[rows, sl], gl_ref[h, j])
                outs.append(o)
            o_ref[rows, :] = jnp.concatenate(outs, axis=1)
        for h in range(hg):
            s_ref[h] = states[h]

    row = pl.BlockSpec((sc * CHUNK, d), lambda n: (n, 0))
    return Part(
        body, [u, w, attn, qd, kd, gl],
        [row, row, pl.BlockSpec((hg, sc * CHUNK, CHUNK), lambda n: (0, n, 0)), row, row,
         pl.BlockSpec((hg, sc, 1, GDN_DK), lambda n: (0, n, 0, 0))],
        [jax.ShapeDtypeStruct((cfg.rp, d), F32), jax.ShapeDtypeStruct((nch, hg, GDN_DK, GDN_DK), F32)],
        [row, pl.BlockSpec((sc, hg, GDN_DK, GDN_DK), lambda n: (n, 0, 0, 0))],
        [pltpu.VMEM((hg, GDN_DK, GDN_DK), F32)])


def _gdn_scan_bwd(cfg, do, u, w, attn, qd, kd, gl, ss):
    d, hg, nch, sc = cfg.d, cfg.hg, cfg.nch, cfg.sc
    nst = nch // sc

    def body(do_ref, u_ref, w_ref, a_ref, qd_ref, kd_ref, gl_ref, ss_ref,
             du_ref, dw_ref, da_ref, dqd_ref, dkd_ref, dgl_ref, ds_ref):
        @pl.when(pl.program_id(0) == 0)
        def _():
            ds_ref[...] = jnp.zeros(ds_ref.shape, F32)

        dstates = [ds_ref[h] for h in range(hg)]
        for j in reversed(range(sc)):
            rows = slice(j * CHUNK, (j + 1) * CHUNK)
            dus, dws, dqds, dkds = [], [], [], []
            for h in range(hg):
                sl = slice(h * GDN_DK, (h + 1) * GDN_DK)
                args = (ss_ref[j, h], u_ref[rows, sl], w_ref[rows, sl], a_ref[h, rows, :], qd_ref[rows, sl],
                        kd_ref[rows, sl], gl_ref[h, j])
                dstates[h], du, dw, da, dqd, dkd, dgl = _gdn_step_bwd(*args, dstates[h], do_ref[rows, sl])
                da_ref[h, rows, :] = da
                dgl_ref[h, j] = dgl
                dus.append(du)
                dws.append(dw)
                dqds.append(dqd)
                dkds.append(dkd)
            du_ref[rows, :] = jnp.concatenate(dus, axis=1)
            dw_ref[rows, :] = jnp.concatenate(dws, axis=1)
            dqd_ref[rows, :] = jnp.concatenate(dqds, axis=1)
            dkd_ref[rows, :] = jnp.concatenate(dkds, axis=1)
        for h in range(hg):
            ds_ref[h] = dstates[h]

    row = pl.BlockSpec((sc * CHUNK, d), lambda n: (nst - 1 - n, 0))
    aspec = pl.BlockSpec((hg, sc * CHUNK, CHUNK), lambda n: (0, nst - 1 - n, 0))
    gspec = pl.BlockSpec((hg, sc, 1, GDN_DK), lambda n: (0, nst - 1 - n, 0, 0))
    rowshape = jax.ShapeDtypeStruct((cfg.rp, d), F32)
    return Part(
        body, [do, u, w, attn, qd, kd, gl, ss],
        [row, row, row, aspec, row, row, gspec, pl.BlockSpec((sc, hg, GDN_DK, GDN_DK), lambda n: (nst - 1 - n, 0, 0, 0))],
        [rowshape, rowshape, jax.ShapeDtypeStruct(attn.shape, F32), rowshape, rowshape, jax.ShapeDtypeStruct(gl.shape, F32)],
        [row, row, aspec, row, row, gspec],
        [pltpu.VMEM((hg, GDN_DK, GDN_DK), F32)])


def _ret_consts(cfg):
    hr = cfg.hr
    lg = np.log(1.0 - 2.0 ** (-5.0 - np.arange(hr, dtype=np.float64)))
    idx = np.arange(CHUNK, dtype=np.float64)
    tril = np.tril(np.ones((CHUNK, CHUNK), dtype=bool))
    dm = np.where(tril[None], np.exp((idx[:, None] - idx[None, :])[None] * lg[:, None, None]), 0.0)
    qdc = np.exp((idx[None, :] + 1.0) * lg[:, None])
    kdc = np.exp((CHUNK - 1.0 - idx[None, :]) * lg[:, None])
    gch = np.exp(CHUNK * lg)
    qdc = np.broadcast_to(qdc[:, :, None], (hr, CHUNK, RET_DK))
    kdc = np.broadcast_to(kdc[:, :, None], (hr, CHUNK, RET_DK))
    gch = np.broadcast_to(gch[:, None, None], (hr, 1, RET_DK))
    return tuple(jnp.asarray(np.ascontiguousarray(t), F32) for t in (dm, qdc, kdc, gch))


def _ret_scan_fwd(cfg, qr, kr, proj, consts):
    d, hr, nch, sc = cfg.d, cfg.hr, cfg.nch, cfg.sc
    nst = nch // sc
    dm, qdc, kdc, gch = consts

    def body(q_ref, k_ref, v_ref, dm_ref, qdc_ref, kdc_ref, g_ref, o_ref, ss_ref, s_ref):
        @pl.when(pl.program_id(0) == 0)
        def _():
            s_ref[...] = jnp.zeros(s_ref.shape, F32)

        states = [s_ref[h] for h in range(hr)]
        for j in range(sc):
            rows = slice(j * CHUNK, (j + 1) * CHUNK)
            outs = []
            for h in range(hr):
                sl = slice(h * RET_DK, (h + 1) * RET_DK)
                ss_ref[j, h] = states[h]
                states[h], o = _ret_step(states[h], q_ref[rows, sl], k_ref[rows, sl], v_ref[rows, sl], dm_ref[h],
                                         qdc_ref[h], kdc_ref[h], g_ref[h])
                outs.append(o)
            o_ref[rows, :] = jnp.concatenate(outs, axis=1)
        for h in range(hr):
            s_ref[h] = states[h]

    row = pl.BlockSpec((sc * CHUNK, d), lambda n: (n, 0))
    return Part(
        body, [qr, kr, proj, dm, qdc, kdc, gch],
        [row, row, pl.BlockSpec((sc * CHUNK, d), lambda n: (n, RV_BLOCK)), _full(dm), _full(qdc), _full(kdc), _full(gch)],
        [jax.ShapeDtypeStruct((cfg.rp, d), F32), jax.ShapeDtypeStruct((nch, hr, RET_DK, RET_DK), F32)],
        [row, pl.BlockSpec((sc, hr, RET_DK, RET_DK), lambda n: (n, 0, 0, 0))],
        [pltpu.VMEM((hr, RET_DK, RET_DK), F32)])


def _ret_scan_bwd(cfg, do, qr, kr, proj, consts, ss, dproj):
    d, hr, nch, sc = cfg.d, cfg.hr, cfg.nch, cfg.sc
    nst = nch // sc
    dm, qdc, kdc, gch = consts

    def body(do_ref, q_ref, k_ref, v_ref, dm_ref, qdc_ref, kdc_ref, g_ref, ss_ref, _, dq_ref, dk_ref, dv_ref, ds_ref):
        @pl.when(pl.program_id(0) == 0)
        def _():
            ds_ref[...] = jnp.zeros(ds_ref.shape, F32)

        dstates = [ds_ref[h] for h in range(hr)]
        for j in reversed(range(sc)):
            rows = slice(j * CHUNK, (j + 1) * CHUNK)
            dqs, dks, dvs = [], [], []
            for h in range(hr):
                sl = slice(h * RET_DK, (h + 1) * RET_DK)
                cs = (dm_ref[h], qdc_ref[h], kdc_ref[h], g_ref[h])
                _, vjp_fn = jax.vjp(lambda s, q, k, v, cs=cs: _ret_step(s, q, k, v, *cs),
                                    ss_ref[j, h], q_ref[rows, sl].astype(F32), k_ref[rows, sl].astype(F32), v_ref[rows, sl])
                dstates[h], dq, dk, dv = vjp_fn((dstates[h], do_ref[rows, sl]))
                dqs.append(dq)
                dks.append(dk)
                dvs.append(dv)
            dq_ref[rows, :] = jnp.concatenate(dqs, axis=1).astype(dq_ref.dtype)
            dk_ref[rows, :] = jnp.concatenate(dks, axis=1).astype(dk_ref.dtype)
            dv_ref[rows, :] = jnp.concatenate(dvs, axis=1).astype(dv_ref.dtype)
        for h in range(hr):
            ds_ref[h] = dstates[h]

    row = pl.BlockSpec((sc * CHUNK, d), lambda n: (nst - 1 - n, 0))
    rowshape = jax.ShapeDtypeStruct((cfg.rp, d), BF16)
    vspec = pl.BlockSpec((sc * CHUNK, d), lambda n: (nst - 1 - n, RV_BLOCK))
    return Part(
        body, [do, qr, kr, proj, dm, qdc, kdc, gch, ss, dproj],
        [row, row, row, vspec, _full(dm), _full(qdc), _full(kdc), _full(gch),
         pl.BlockSpec((sc, hr, RET_DK, RET_DK), lambda n: (nst - 1 - n, 0, 0, 0)), ANY],
        [rowshape, rowshape, jax.ShapeDtypeStruct(dproj.shape, dproj.dtype)],
        [row, row, vspec],
        [pltpu.VMEM((hr, RET_DK, RET_DK), F32)], {9: 2})


def _final(cfg, h2, normf, tgt):
    d, tr = cfg.d, cfg.xrow
    nr = cfg.rp // tr

    def body(h_ref, g_ref, t_ref, dh_ref, dg_ref, loss_ref):
        i = pl.program_id(0)
        y, vjp_fn = jax.vjp(_rms, h_ref[...], g_ref[...])
        err = jnp.where(i >= 1, y - t_ref[...], 0.0)
        dh, dg = vjp_fn(err * (1.0 / d))
        dh_ref[...] = dh
        part = jnp.zeros((8, LANES), F32) + 0.5 * jnp.sum(err * err) * (1.0 / d)

        @pl.when(i == 0)
        def _():
            dg_ref[...] = dg
            loss_ref[...] = part

        @pl.when(i > 0)
        def _():
            dg_ref[...] += dg
            loss_ref[...] += part

    return _pallas(
        body,
        out_shape=[jax.ShapeDtypeStruct((cfg.rp, d), F32), jax.ShapeDtypeStruct((1, d), F32),
                   jax.ShapeDtypeStruct((8, LANES), F32)],
        grid=(nr,),
        in_specs=[_rows(tr, d), _full(normf), pl.BlockSpec((tr, d), lambda i: (jnp.maximum(i - 1, 0), 0))],
        out_specs=[_rows(tr, d), pl.BlockSpec((1, d), lambda i: (0, 0)), pl.BlockSpec((8, LANES), lambda i: (0, 0))],
        name="final_loss", compiler_params=_params((nr,)))(h2, normf, tgt)


ANY = pl.BlockSpec(memory_space=pl.ANY)


def _place():
    x, y, c = lax.axis_index("x"), lax.axis_index("y"), lax.axis_index("c")
    others = [(1 - x, y), (x, 1 - y), (1 - x, 1 - y)]
    return x, y, c, others


def _row_tile(rows, cap=256):
    return max(t for t in range(16, min(rows, cap) + 1, 16) if rows % t == 0)


class Comm(NamedTuple):
    ins: list
    outs: list
    sems: list
    start: object
    finish: object


def _run_comm(name, comm):
    n_in, n_out = len(comm.ins), len(comm.outs)

    def body(*refs):
        ins, outs, sems = refs[:n_in], refs[n_in:n_in + n_out], refs[n_in + n_out:]
        comm.start(ins, outs, sems)
        comm.finish(ins, outs, sems)

    return _pallas(body, out_shape=comm.outs, in_specs=[ANY] * n_in, out_specs=[ANY] * n_out,
                   scratch_shapes=comm.sems, name=name)(*comm.ins)


def _gather_comm(ws):
    n = len(ws)
    halves = [w.shape[0] // 2 for w in ws]

    def copies(w_refs, o_refs, sems):
        send_sems, recv_sems = sems
        x, y, c, others = _place()
        me = 2 * x + y
        chips = [2 * px + py for px, py in others]

        def piece(a, chip, core):
            return o_refs[a].at[chip, pl.ds(core * halves[a], halves[a]), :]

        def copy(a, k, src, chip, core, to):
            return pltpu.make_async_remote_copy(src_ref=src, dst_ref=piece(a, chip, core), send_sem=send_sems.at[6 * a + k],
                                                recv_sem=recv_sems.at[6 * a + k], device_id=to, device_id_type=MESH)

        def first(j, a):
            return copy(a, j, w_refs[a].at[pl.ds(c * halves[a], halves[a]), :], me, c, (*others[j], c))

        def landed(j, a):
            return copy(a, j, piece(a, chips[j], c), chips[j], c, (x, y, c))

        def passed(j, a):
            return copy(a, 3 + j, piece(a, chips[j], c), chips[j], c, (x, y, 1 - c))

        def from_sibling(j, a):
            return copy(a, 3 + j, piece(a, chips[j], 1 - c), chips[j], 1 - c, (x, y, c))

        return first, landed, passed, from_sibling

    pairs = [(j, a) for j in range(3) for a in range(n)]

    def start(w_refs, o_refs, sems):
        first, _, _, _ = copies(w_refs, o_refs, sems)
        for j, a in pairs:
            first(j, a).start()

    def finish(w_refs, o_refs, sems):
        first, landed, passed, from_sibling = copies(w_refs, o_refs, sems)
        for j, a in pairs:
            landed(j, a).wait_recv()
            passed(j, a).start()
        for j, a in pairs:
            from_sibling(j, a).wait_recv()
        for j, a in pairs:
            first(j, a).wait_send()
            passed(j, a).wait_send()

    return Comm(list(ws), [jax.ShapeDtypeStruct((N_CHIPS,) + w.shape, w.dtype) for w in ws],
                [pltpu.SemaphoreType.DMA((6 * n,)), pltpu.SemaphoreType.DMA((6 * n,))], start, finish)


def _pair_exchange(name, gs):
    n = len(gs)

    def body(*refs):
        g_refs, o_refs = refs[:n], refs[n:2 * n]
        send_sems, recv_sems = refs[2 * n:]
        x, y, c, _ = _place()
        cps = []
        for a in range(n):
            half = gs[a].shape[1] // 2
            cp = pltpu.make_async_remote_copy(
                src_ref=g_refs[a].at[:, pl.ds((1 - c) * half, half), :], dst_ref=o_refs[a], send_sem=send_sems.at[a],
                recv_sem=recv_sems.at[a], device_id=(x, y, 1 - c), device_id_type=MESH)
            cp.start()
            cps.append(cp)
        for cp in cps:
            cp.wait()

    return _pallas(
        body, out_shape=[jax.ShapeDtypeStruct((N_CHIPS, g.shape[1] // 2, g.shape[2]), g.dtype) for g in gs],
        in_specs=[ANY] * n, out_specs=[ANY] * n,
        scratch_shapes=[pltpu.SemaphoreType.DMA((n,)), pltpu.SemaphoreType.DMA((n,))], name=name)(*gs)


def _pair_sum(name, g, recv, cidx):
    half, cols = recv.shape[1], recv.shape[2]
    tr = _row_tile(half)
    nblk = half // tr

    def body(c_ref, g_ref, r_ref, o_ref):
        o_ref[...] = (g_ref[...] + r_ref[...]).astype(o_ref.dtype)

    grid_spec = pltpu.PrefetchScalarGridSpec(
        num_scalar_prefetch=1, grid=(N_CHIPS, nblk),
        in_specs=[pl.BlockSpec((1, tr, cols), lambda s, i, c: (s, c[0] * nblk + i, 0)),
                  pl.BlockSpec((1, tr, cols), lambda s, i, c: (s, i, 0))],
        out_specs=pl.BlockSpec((1, tr, cols), lambda s, i, c: (s, i, 0)))
    return _pallas(
        body, out_shape=jax.ShapeDtypeStruct((N_CHIPS, half, cols), BF16), grid_spec=grid_spec,
        name=name, compiler_params=_params((N_CHIPS, nblk)))(cidx, g, recv)


def _exchange_comm(parts):
    n = len(parts)

    def copies(p_refs, o_refs, sems):
        send_sems, recv_sems = sems
        x, y, c, others = _place()
        me = 2 * x + y

        def copy(a, j, src_chip, dst_chip):
            px, py = others[j]
            return pltpu.make_async_remote_copy(
                src_ref=p_refs[a].at[src_chip], dst_ref=o_refs[a].at[dst_chip], send_sem=send_sems.at[3 * a + j],
                recv_sem=recv_sems.at[3 * a + j], device_id=(px, py, c), device_id_type=MESH)

        def send(j, a):
            return copy(a, j, 2 * others[j][0] + others[j][1], me)

        def arrival(j, a):
            return copy(a, j, me, 2 * others[j][0] + others[j][1])

        return send, arrival

    pairs = [(j, a) for j in range(3) for a in range(n)]

    def start(p_refs, o_refs, sems):
        send, _ = copies(p_refs, o_refs, sems)
        for j, a in pairs:
            send(j, a).start()

    def finish(p_refs, o_refs, sems):
        send, arrival = copies(p_refs, o_refs, sems)
        for j, a in pairs:
            arrival(j, a).wait_recv()
        for j, a in pairs:
            send(j, a).wait_send()

    return Comm(list(parts), [jax.ShapeDtypeStruct(p.shape, p.dtype) for p in parts],
                [pltpu.SemaphoreType.DMA((3 * n,)), pltpu.SemaphoreType.DMA((3 * n,))], start, finish)


def _chip_sum(name, part, slots, chip):
    half, cols = slots.shape[1], slots.shape[2]
    tr = _row_tile(half)

    def body(me_ref, p_ref, *rest):
        s_refs, o_ref = rest[:N_CHIPS], rest[N_CHIPS]
        own = p_ref[...].astype(F32)
        v = [jnp.where(me_ref[0] == k, own, s_refs[k][...].astype(F32)) for k in range(N_CHIPS)]
        o_ref[...] = ((v[0] + v[1]) + v[2]) + v[3]

    def slot_spec(k):
        return pl.BlockSpec((None, tr, cols), lambda i, me: (jnp.where(me[0] == k, (k + 1) % N_CHIPS, k), i, 0))

    grid_spec = pltpu.PrefetchScalarGridSpec(
        num_scalar_prefetch=1, grid=(half // tr,),
        in_specs=[pl.BlockSpec((None, tr, cols), lambda i, me: (me[0], i, 0))] + [slot_spec(k) for k in range(N_CHIPS)],
        out_specs=pl.BlockSpec((tr, cols), lambda i, me: (i, 0)))
    return _pallas(
        body, out_shape=jax.ShapeDtypeStruct((half, cols), F32), grid_spec=grid_spec,
        name=name, compiler_params=_params((half // tr,)))(chip, part, *([slots] * N_CHIPS))


def _pair_swap(fins):
    n = len(fins)

    def body(*refs):
        f_refs, o_refs = refs[:n], refs[n:2 * n]
        send_sems, recv_sems = refs[2 * n:]
        x, y, c, _ = _place()
        cps = [pltpu.make_async_remote_copy(src_ref=f_refs[a], dst_ref=o_refs[a], send_sem=send_sems.at[a],
                                            recv_sem=recv_sems.at[a], device_id=(x, y, 1 - c), device_id_type=MESH)
               for a in range(n)]
        for cp in cps:
            cp.start()
        for cp in cps:
            cp.wait()

    return _pallas(
        body, out_shape=[jax.ShapeDtypeStruct(f.shape, f.dtype) for f in fins], in_specs=[ANY] * n, out_specs=[ANY] * n,
        scratch_shapes=[pltpu.SemaphoreType.DMA((n,)), pltpu.SemaphoreType.DMA((n,))], name="grad_pair_swap")(*fins)


def _adamw(name, w, g_own, g_other, m, v, cidx):
    R, cols = w.shape[-2:]
    lead = (None,) * (w.ndim - 2)
    zeros = (0,) * (w.ndim - 2)
    half = R // 2
    tr = _row_tile(half, 128)
    nblk = half // tr
    c1 = 1.0 - ADAM_B1 ** ADAM_STEP
    c2 = 1.0 - ADAM_B2 ** ADAM_STEP

    def body(c_ref, w_ref, go_ref, gs_ref, m_ref, v_ref, g_ref, d_ref, nm_ref, nv_ref):
        mine = (pl.program_id(0) // nblk) == c_ref[0]
        gv = jnp.where(mine, go_ref[...], gs_ref[...])
        nm = ADAM_B1 * m_ref[...] + (1.0 - ADAM_B1) * gv
        nv = ADAM_B2 * v_ref[...] + (1.0 - ADAM_B2) * (gv * gv)
        g_ref[...] = gv
        d_ref[...] = -ADAM_LR * ((nm / c1) / (jnp.sqrt(nv / c2) + ADAM_EPS) + ADAM_WD * w_ref[...])
        nm_ref[...] = nm
        nv_ref[...] = nv

    spec = pl.BlockSpec(lead + (tr, cols), lambda i, c: zeros + (i, 0))
    hspec = pl.BlockSpec((tr, cols), lambda i, c: (i % nblk, 0))
    shape = jax.ShapeDtypeStruct(w.shape, F32)
    grid_spec = pltpu.PrefetchScalarGridSpec(num_scalar_prefetch=1, grid=(R // tr,),
                                             in_specs=[spec, hspec, hspec, spec, spec], out_specs=[spec] * 4)
    return _pallas(
        body, out_shape=[shape] * 4, grid_spec=grid_spec,
        name=name, compiler_params=_params((R // tr,)))(cidx, w, g_own, g_other, m, v)


PARAMS = (("meta", 1), ("norm1", None), ("w_in", 2), ("gdn_conv_w", 2), ("gdn_a_log", None), ("gdn_dt_bias", None),
          ("gdn_norm", None), ("w_out", 1), ("norm2", None), ("w_ffn_up", 2), ("ffn_conv_w", 2), ("ffn_conv_b", None),
          ("w_ffn_down", 1), ("norm_f", None))
BIG = ("w_in", "w_out", "w_ffn_up", "w_ffn_down")
PACK_ALIGN = 1024
PACK_ROWS_ALIGN = 32


def _pack(arrs, dtype):
    parts, total = [], 0
    for a in arrs:
        f = a.reshape(-1).astype(dtype)
        pad = (-f.shape[0]) % PACK_ALIGN
        parts.append(jnp.pad(f, (0, pad)) if pad else f)
        total += f.shape[0] + pad
    rows = total // LANES
    rpad = (-rows) % PACK_ROWS_ALIGN
    if rpad:
        parts.append(jnp.zeros((rpad * LANES,), dtype))
    return jnp.concatenate(parts).reshape(rows + rpad, LANES)


def _unpack(buf, shapes):
    flat = buf.reshape(-1)
    outs, off = [], 0
    for s in shapes:
        n = int(np.prod(s))
        outs.append(flat[off:off + n].reshape(s))
        off += n + (-n) % PACK_ALIGN
    return outs


def _split4(a, axis):
    n = a.shape[axis] // N_CHIPS
    return [lax.slice_in_dim(a, s * n, (s + 1) * n, axis=axis) for s in range(N_CHIPS)]


PROJ_ORDER = (3, 7, 8, 9, 0, 1, 2, 6, 4, 5)


def _reorder_w_in(w, cfg):
    d, hg = cfg.d, cfg.hg

    def block(k):
        off = k * d + (2 * hg if k >= 4 else 0)
        return w[:, off:off + d]

    tail = jnp.pad(w[:, 4 * d:4 * d + 2 * hg], ((0, 0), (0, LANES - 2 * hg)))
    return jnp.concatenate([block(k) for k in PROJ_ORDER] + [tail], axis=1)


def _restore_w_in(wr, cfg):
    d, hg = cfg.d, cfg.hg
    at = {k: i for i, k in enumerate(PROJ_ORDER)}
    block = lambda k: wr[:, at[k] * d:(at[k] + 1) * d]
    return jnp.concatenate([block(k) for k in range(4)] + [wr[:, 10 * d:10 * d + 2 * hg]] +
                           [block(k) for k in range(4, 10)], axis=1)


def _step(cfg, x, tgt, shard, m_shard, v_shard):
    d, hg, dff, rp, tr, tm = cfg.d, cfg.hg, cfg.dff, cfg.rp, cfg.tr, cfg.tm
    nrow = rp // tr
    assert cfg.tf * N_CHIPS == 2 * dff and cfg.din % N_CHIPS == 0
    cidx = lax.axis_index("c").astype(jnp.int32).reshape(1)
    chip = (2 * lax.axis_index("x") + lax.axis_index("y")).astype(jnp.int32).reshape(1)

    axis = dict(PARAMS)
    small = ("meta", "gdn_conv_w", "ffn_conv_w")
    small_shapes = [shard[n].shape for n in small]
    mine = [shard[n][0].astype(BF16) for n in BIG] + [_pack([shard[n] for n in small], F32)]

    def with_own(gathered, own):
        return [lax.dynamic_update_slice(g, w[None], (chip[0], 0, 0)) for g, w in zip(gathered, own)]

    g_in, g_small = _run_comm("weights_gather_first", _gather_comm([mine[0], mine[4]]))
    g_small, = with_own([g_small], [mine[4]])
    w_in_r = _reorder_w_in(jnp.concatenate([jnp.where(chip[0] == s, mine[0], g_in[s]) for s in range(N_CHIPS)], axis=1),
                           cfg)
    per_chip = [_unpack(g_small[s], small_shapes) for s in range(N_CHIPS)]
    full = {n: jnp.concatenate([per_chip[s][k] for s in range(N_CHIPS)], axis=axis[n]) for k, n in enumerate(small)}
    meta = full["meta"]
    gconv_w = full["gdn_conv_w"][0]
    fconv_w = full["ffn_conv_w"][0]
    norm1, norm2, gnorm = shard["norm1"], shard["norm2"], shard["gdn_norm"]
    normf = shard["norm_f"].reshape(1, d)
    fconv_b = shard["ffn_conv_b"]
    alog = jnp.pad(shard["gdn_a_log"], ((0, 7), (0, LANES - hg)))
    dtb = jnp.pad(shard["gdn_dt_bias"], ((0, 7), (0, LANES - hg)))

    h0 = jnp.concatenate([jnp.zeros((cfg.front, d), F32), meta, x], axis=0)
    half = RET_DK // 2
    pos = np.arange(rp, dtype=np.float32) - np.float32(cfg.front)
    inv = (np.float32(1.0) / np.float32(ROPE_BASE) ** (np.arange(half, dtype=np.float32) / np.float32(half))).astype(np.float32)
    ang = pos[:, None] * inv[None, :]
    cos, sin = jnp.asarray(np.cos(ang), F32), jnp.asarray(np.sin(ang), F32)
    rconsts = _ret_consts(cfg)

    tr_n = 3 * tr if rp % (3 * tr) == 0 else tr
    rms_f = _make_rms_fn(cfg, tr_n, False)
    rms_b = _make_rms_fn(cfg, tr_n, True)
    rowshape = jax.ShapeDtypeStruct((rp, d), F32)
    rspec = _rows(tr, d)
    nspec = _rows(tr_n, d)

    def rms_fwd(name, h, g):
        return _stage_fwd(name, rms_f, (rp // tr_n,), [In(h, nspec), In(g, _full(g))],
                          [jax.ShapeDtypeStruct((rp, d), BF16)], [nspec])[0]

    wm = 10 * d
    w_main, w_tail = w_in_r[:, :wm], w_in_r[:, wm:]
    tn_in = 2560 if wm % 2560 == 0 else LANES
    hn1 = rms_fwd("rms1_fwd", h0, norm1)
    proj, rest = _mm("proj_fwd", hn1, w_main, tm=tm, tn=tn_in, tk=d, out_dtype=BF16, comm=_gather_comm(mine[1:4]))
    ptail = _mm("proj_tail_fwd", hn1, w_tail, tm=tm, tn=LANES, tk=d)
    g_out, g_up, g_down = with_own(rest, mine[1:4])
    w_out = g_out.reshape(d, d)
    w_up = g_up
    w_up_t = jnp.swapaxes(g_up, 1, 2).reshape(2 * dff, d)
    w_down = g_down.reshape(dff, d)
    cqkv = _conv_fwd("gdn_conv_fwd", proj, CONV_COL * d, gconv_w, None, taps=GDN_CONV, width=3 * d, tr=tr, tc=d)
    prep_fn = _make_gdn_prep_fn(cfg, tr)
    prep_ins = [In(cqkv, _rows(tr, 3 * d), BF16), In(ptail, _rows(tr, LANES), BF16),
                In(alog, _full(alog), F32, True), In(dtb, _full(dtb), F32, True)]
    qn, kn, vv, bB, lB = _stage_fwd("gdn_prep_fwd", prep_fn, (nrow,), prep_ins, [rowshape] * 5, [rspec] * 5)

    trg = cfg.nb * CHUNK
    gi_grid = (rp // trg, hg)
    hspec = pl.BlockSpec((trg, GDN_DK), lambda i, h: (i, h))
    aspec = pl.BlockSpec((1, trg, CHUNK), lambda i, h: (h, i, 0))
    gspec = pl.BlockSpec((1, cfg.nb, 1, GDN_DK), lambda i, h: (h, i, 0, 0))
    intra_ins = [In(t, hspec, F32) for t in (qn, kn, vv, bB, lB)]
    ashape = jax.ShapeDtypeStruct((hg, rp, CHUNK), F32)
    intra_shapes = [rowshape, rowshape, ashape, rowshape, rowshape, jax.ShapeDtypeStruct((hg, cfg.nch, 1, GDN_DK), F32), ashape]
    intra_specs = [hspec, hspec, aspec, hspec, hspec, gspec, aspec]
    gu, gw, gattn, gqd, gkd, ggl, gtinv = _stage_fwd("gdn_intra_fwd", _gdn_intra_fn, gi_grid, intra_ins, intra_shapes,
                                                     intra_specs)
    rot_fn = _make_rot_fn(cfg)

    def rot_ins(dproj=None):
        return [In(proj, _rows(tr_n, 2 * d, ROT_COL // 2), BF16, galias=dproj, gshape=(rp, wm)),
                In(cos, _rows(tr_n, half)), In(sin, _rows(tr_n, half))]

    qr, kr = _stage_fwd("rot_fwd", rot_fn, (rp // tr_n,), rot_ins(), [jax.ShapeDtypeStruct((rp, d), BF16)] * 2, [nspec] * 2)
    nst = cfg.nch // cfg.sc
    oa, gss = _run_parts("gdn_scan_fwd", (nst,), [_gdn_scan_fwd(cfg, gu, gw, gattn, gqd, gkd, ggl)])[0]
    ob, rss = _run_parts("ret_scan_fwd", (nst,), [_ret_scan_fwd(cfg, qr, kr, proj, rconsts)])[0]

    mix_fn = _make_mix_fn(cfg)
    mix_ins = [In(oa, rspec, F32), In(ob, rspec, F32), In(proj, _rows(tr, 4 * d, MIX_COL // 4), BF16, gshape=(rp, wm)),
               In(gnorm, _full(gnorm), F32, True)]
    ymix = _stage_fwd("mix_fwd", mix_fn, (nrow,), mix_ins, [jax.ShapeDtypeStruct((rp, d), BF16)], [rspec])[0]
    h1 = _mm("out_proj_fwd", ymix, w_out, tm=tm, tn=d, tk=d, add=h0)

    hn2 = rms_fwd("rms2_fwd", h1, norm2)
    up = _mm("ffn_up_fwd", hn2, w_up, tm=tm, tn=cfg.tf, tk=d, out_dtype=BF16)
    uc = _conv_fwd("ffn_conv_fwd", up, 0, fconv_w, fconv_b, taps=FFN_CONV, width=2 * dff, tr=tr, tc=cfg.tf)
    tra = tr
    act_ins = [In(uc, _rows(tra, 2 * dff), BF16)]
    act_spec = _rows(tra, dff)
    act = _stage_fwd("ffn_act_fwd", _act_fn, (rp // tra,), act_ins, [jax.ShapeDtypeStruct((rp, dff), BF16)], [act_spec])[0]
    h2 = _mm("ffn_down_fwd", act, w_down, tm=tm, tn=d, tk=cfg.tf, add=h1)

    dh2, g_normf, loss_blk = _final(cfg, h2, normf, tgt)
    loss = lax.psum(loss_blk[0, 0], ("x", "y", "c"))

    g_w_down = _mm_tn("ffn_down_dw", act, dh2, tr=tm, tka=cfg.tf, tn=d)
    dact = _mm("ffn_down_dx", dh2, w_down.T, tm=tm, tn=cfg.tf, tk=d, out_dtype=BF16)
    duc, = _stage_bwd("ffn_act_bwd", _act_fn, (rp // tra,), act_ins, [(dact, act_spec)])
    dup, g_fconv_w, g_fconv_b = _conv_bwd("ffn_conv_bwd", up, 0, fconv_w, duc, taps=FFN_CONV, width=2 * dff,
                                          tr=tr, tc=cfg.tf, with_bias=True)
    g_w_up = _mm_tn("ffn_up_dw", hn2, dup, tr=tm, tka=d, tn=cfg.tf, blocked=True)

    def pair_reduce(tag, names, arrs):
        recvs = _pair_exchange("grad_pair_exchange_" + tag, arrs)
        return [_pair_sum("grad_pair_sum_" + n, g, r, cidx) for n, g, r in zip(names, arrs, recvs)]

    parts_ffn = pair_reduce("ffn", ["w_ffn_down", "w_ffn_up"], [g_w_down.reshape(N_CHIPS, dff // N_CHIPS, d), g_w_up])
    dhn2, slots_ffn = _mm("ffn_up_dx", dup, w_up_t, tm=tm, tn=d, tk=2 * cfg.tf, comm=_exchange_comm(parts_ffn))

    def rms_bwd(name, h, g, dhn, dres):
        ins = [In(h, nspec, F32), In(g, _full(g), F32, True)]
        return _stage_bwd(name, rms_b, (rp // tr_n,), ins, [(dhn, nspec), (dres, nspec)])

    dh1, g_norm2 = rms_bwd("rms2_bwd", h1, norm2, dhn2, dh2)
    g_w_out = _mm_tn("out_proj_dw", ymix, dh1, tr=tm, tka=d, tn=d)
    dymix = _mm("out_proj_dx", dh1, w_out.T, tm=tm, tn=d, tk=d)
    doa, dob, dproj, g_gnorm = _stage_bwd("mix_bwd", mix_fn, (nrow,), mix_ins, [(dymix, rspec)])

    dqr, dkr, dproj = _run_parts("ret_scan_bwd", (nst,), [_ret_scan_bwd(cfg, dob, qr, kr, proj, rconsts, rss, dproj)])[0]
    dproj, = _stage_bwd("rot_bwd", rot_fn, (rp // tr_n,), rot_ins(dproj), [(dqr, nspec), (dkr, nspec)])
    dgu, dgw, dgattn, dgqd, dgkd, dggl = _run_parts(
        "gdn_scan_bwd", (nst,), [_gdn_scan_bwd(cfg, doa, gu, gw, gattn, gqd, gkd, ggl, gss)])[0]

    intra_cots = [(dgu, hspec), (dgw, hspec), (dgattn, aspec), (dgqd, hspec), (dgkd, hspec), (dggl, gspec)]
    dqn, dkn, dvv, dbB, dlB = _stage_bwd("gdn_intra_bwd", _gdn_intra_fn, gi_grid, intra_ins + [In(gtinv, aspec)], intra_cots)
    dcqkv, dtail, g_alog, g_dtb = _stage_bwd(
        "gdn_prep_bwd", prep_fn, (nrow,), prep_ins, [(t, rspec) for t in (dqn, dkn, dvv, dbB, dlB)])
    dproj, g_gconv_w = _conv_bwd("gdn_conv_bwd", proj, CONV_COL * d, gconv_w, dcqkv, taps=GDN_CONV, width=3 * d,
                                 tr=tr, tc=d, with_bias=False, dx_into=dproj)
    g_w_in_r = jnp.concatenate([_mm_tn("proj_dw", hn1, dproj, tr=tm, tka=d, tn=tn_in),
                                _mm_tn("proj_tail_dw", hn1, dtail, tr=tm, tka=d, tn=LANES)], axis=1)
    g_in4 = jnp.stack(_split4(_restore_w_in(g_w_in_r, cfg), 1))
    parts_mix = pair_reduce("mix", ["w_out", "w_in"], [g_w_out.reshape(N_CHIPS, d // N_CHIPS, d), g_in4])
    dhn1_tail = _mm("proj_tail_dx", dtail, w_tail.T, tm=tm, tn=d, tk=LANES)
    dhn1, slots_mix = _mm("proj_dx", dproj, w_main.T, tm=tm, tn=d, tk=tn_in // 2 if tn_in > LANES else LANES, add=dhn1_tail,
                          comm=_exchange_comm(parts_mix))
    dh0, g_norm1 = rms_bwd("rms1_bwd", h0, norm1, dhn1, dh1)

    grad_x = dh0[cfg.xrow:]
    small_grads = {
        "meta": dh0[cfg.front:cfg.xrow], "norm1": g_norm1, "gdn_conv_w": g_gconv_w[None],
        "gdn_a_log": g_alog[0:1, :hg], "gdn_dt_bias": g_dtb[0:1, :hg], "gdn_norm": g_gnorm, "norm2": g_norm2,
        "ffn_conv_w": g_fconv_w[None], "ffn_conv_b": g_fconv_b, "norm_f": g_normf.reshape(d),
    }

    small_names = [n for n, _ in PARAMS if n not in BIG]
    g_small = jnp.stack([_pack([small_grads[n] if axis[n] is None else _split4(small_grads[n], axis[n])[s]
                                for n in small_names], F32) for s in range(N_CHIPS)])
    parts_small = pair_reduce("small", ["small"], [g_small])
    slots_small = _run_comm("grad_exchange_small", _exchange_comm(parts_small))
    tags = ["w_in", "w_out", "w_ffn_up", "w_ffn_down", "small"]
    parts = [parts_mix[1], parts_mix[0], parts_ffn[1], parts_ffn[0], parts_small[0]]
    slots = [slots_mix[1], slots_mix[0], slots_ffn[1], slots_ffn[0], slots_small[0]]
    fins = [_chip_sum("grad_chip_sum_" + t, p, s, chip) for t, p, s in zip(tags, parts, slots)]
    sibs = _pair_swap(fins)

    def flat2(a):
        return a.reshape(-1, a.shape[-1])

    outs = {}
    for k, t in enumerate(BIG):
        res = _adamw("adamw_" + t, flat2(shard[t]), fins[k], sibs[k], flat2(m_shard[t]), flat2(v_shard[t]), cidx)
        outs[t] = [r.reshape(shard[t].shape) for r in res]
    small_shapes_all = [shard[n].shape for n in small_names]
    pk = lambda src: _pack([src[n] for n in small_names], F32)
    res = _adamw("adamw_small", pk(shard), fins[4], sibs[4], pk(m_shard), pk(v_shard), cidx)
    for k, r in enumerate(res):
        for n, a in zip(small_names, _unpack(r, small_shapes_all)):
            outs.setdefault(n, [None] * 4)[k] = a
    names = [n for n, _ in PARAMS]
    return (loss, grad_x[None], *[outs[n][k] for k in range(4) for n in names])


def kernel(x, meta, norm1, w_in, gdn_conv_w, gdn_a_log, gdn_dt_bias, gdn_norm, w_out, norm2, w_ffn_up, ffn_conv_w, ffn_conv_b, w_ffn_down, norm_f, loss_target, m_meta, m_norm1, m_w_in, m_gdn_conv_w, m_gdn_a_log, m_gdn_dt_bias, m_gdn_norm, m_w_out, m_norm2, m_w_ffn_up, m_ffn_conv_w, m_ffn_conv_b, m_w_ffn_down, m_norm_f, v_meta, v_norm1, v_w_in, v_gdn_conv_w, v_gdn_a_log, v_gdn_dt_bias, v_gdn_norm, v_w_out, v_norm2, v_w_ffn_up, v_ffn_conv_w, v_ffn_conv_b, v_w_ffn_down, v_norm_f):
    names = [n for n, _ in PARAMS]
    shard = dict(zip(names, (meta, norm1, w_in, gdn_conv_w, gdn_a_log, gdn_dt_bias, gdn_norm, w_out, norm2, w_ffn_up,
                             ffn_conv_w, ffn_conv_b, w_ffn_down, norm_f)))
    m_shard = dict(zip(names, (m_meta, m_norm1, m_w_in, m_gdn_conv_w, m_gdn_a_log, m_gdn_dt_bias, m_gdn_norm, m_w_out,
                               m_norm2, m_w_ffn_up, m_ffn_conv_w, m_ffn_conv_b, m_w_ffn_down, m_norm_f)))
    v_shard = dict(zip(names, (v_meta, v_norm1, v_w_in, v_gdn_conv_w, v_gdn_a_log, v_gdn_dt_bias, v_gdn_norm, v_w_out,
                               v_norm2, v_w_ffn_up, v_ffn_conv_w, v_ffn_conv_b, v_w_ffn_down, v_norm_f)))
    return _step(REAL, x[0], loss_target[0], shard, m_shard, v_shard)
```

```python
import functools
from typing import NamedTuple

import numpy as np
import jax
import jax.numpy as jnp
from jax import lax
from jax.experimental import pallas as pl
from jax.experimental.pallas import tpu as pltpu

F32 = jnp.float32
BF16 = jnp.bfloat16
EPS = 1e-6
CHUNK = 64
GDN_DK = 128
RET_DK = 256
GDN_CONV = 4
FFN_CONV = 3
ROPE_BASE = 10000.0
LANES = 128
N_CHIPS = 4
ADAM_LR, ADAM_B1, ADAM_B2, ADAM_EPS, ADAM_WD, ADAM_STEP = 0.001, 0.9, 0.999, 1e-08, 0.01, 10
MIX_COL, CONV_COL, RV_BLOCK, ROT_COL = 0, 4, 7, 8
MESH = pl.DeviceIdType.MESH
VMEM_LIMIT = 56 * 1024 * 1024


class Cfg(NamedTuple):
    d: int
    seq: int
    n_meta: int
    dff: int
    tr: int
    nb: int
    tm: int
    tf: int
    sc: int

    @property
    def hg(self): return self.d // GDN_DK
    @property
    def hr(self): return self.d // RET_DK
    @property
    def L(self): return self.n_meta + self.seq
    @property
    def rp(self): return -(-self.L // 256) * 256
    @property
    def front(self): return self.rp - self.L
    @property
    def xrow(self): return self.rp - self.seq
    @property
    def nch(self): return self.rp // CHUNK
    @property
    def din(self): return 10 * self.d + 2 * self.hg


REAL = Cfg(d=1024, seq=8192, n_meta=16, dff=2816, tr=256, nb=12, tm=1408, tf=1408, sc=6)


def _pallas(body, **kw):
    return pl.pallas_call(body, **kw)


def _sigmoid_raw(x):
    return 1.0 / (1.0 + jnp.exp(-x))


@jax.custom_vjp
def _sigmoid(x):
    return _sigmoid_raw(x)


def _sigmoid_fwd(x):
    s = _sigmoid_raw(x)
    return s, s


def _sigmoid_bwd(s, g):
    return (g * (s * (1.0 - s)),)


_sigmoid.defvjp(_sigmoid_fwd, _sigmoid_bwd)


@jax.custom_vjp
def _silu(x):
    return x * _sigmoid_raw(x)


def _silu_fwd(x):
    s = _sigmoid_raw(x)
    return x * s, (x, s)


def _silu_bwd(res, g):
    x, s = res
    return (g * (s * (1.0 + x * (1.0 - s))),)


_silu.defvjp(_silu_fwd, _silu_bwd)


def _softplus(x):
    return jnp.maximum(x, 0.0) + jnp.log(1.0 + jnp.exp(-jnp.abs(x)))


def _raw_dot(a, b, ta, tb, hi):
    if not hi:
        a = a.astype(BF16)
        b = b.astype(BF16)
    nbatch = a.ndim - 2
    ca = a.ndim - 2 if ta else a.ndim - 1
    cb = b.ndim - 1 if tb else b.ndim - 2
    batch = tuple(range(nbatch))
    return lax.dot_general(a, b, (((ca,), (cb,)), (batch, batch)),
                           precision=lax.Precision.HIGHEST if hi else None,
                           preferred_element_type=F32)


@functools.partial(jax.custom_vjp, nondiff_argnums=(2, 3, 4))
def _dot_p(a, b, ta, tb, hi):
    return _raw_dot(a, b, ta, tb, hi)


def _dot(a, b, ta=False, tb=False, hi=False):
    return _dot_p(a, b, ta, tb, hi)


def _dot_fwd(a, b, ta, tb, hi):
    return _raw_dot(a, b, ta, tb, hi), (a, b)


def _dot_bwd(ta, tb, hi, res, g):
    a, b = res
    if not ta and not tb:
        da, db = _dot(g, b, False, True, hi), _dot(a, g, True, False, hi)
    elif not ta and tb:
        da, db = _dot(g, b, False, False, hi), _dot(g, a, True, False, hi)
    elif ta and not tb:
        da, db = _dot(b, g, False, True, hi), _dot(a, g, False, False, hi)
    else:
        raise NotImplementedError
    return da.astype(a.dtype), db.astype(b.dtype)


_dot_p.defvjp(_dot_fwd, _dot_bwd)


def _iota2(n, m, axis):
    return lax.broadcasted_iota(jnp.int32, (n, m), axis)


def _bcast(mat, nb):
    return jnp.broadcast_to(mat[None], (nb,) + mat.shape)


def _split3(a):
    a0 = a.astype(BF16)
    r1 = a - a0.astype(F32)
    a1 = r1.astype(BF16)
    return a0, a1, (r1 - a1.astype(F32)).astype(BF16)


@functools.partial(jax.custom_vjp, nondiff_argnums=(2,))
def _dot_sel(a, e, te):
    eb = e.astype(BF16)
    p0, p1, p2 = (_raw_dot(p, eb, False, te, False) for p in _split3(a))
    return p0 + (p1 + p2)


def _dot_sel_fwd(a, e, te):
    return _dot_sel(a, e, te), e


def _dot_sel_bwd(te, e, g):
    return _dot_sel(g, e, not te), jnp.zeros_like(e)


_dot_sel.defvjp(_dot_sel_fwd, _dot_sel_bwd)


@jax.custom_vjp
def _sel_dot(e, x):
    eb = e.astype(BF16)
    p0, p1, p2 = (_raw_dot(eb, p, False, False, False) for p in _split3(x))
    return p0 + (p1 + p2)


def _sel_dot_fwd(e, x):
    return _sel_dot(e, x), e


def _sel_dot_bwd(e, g):
    eb = e.astype(BF16)
    p0, p1, p2 = (_raw_dot(eb, p, True, False, False) for p in _split3(g))
    return jnp.zeros_like(e), p0 + (p1 + p2)


_sel_dot.defvjp(_sel_dot_fwd, _sel_dot_bwd)


def _tri_inv_raw(m):
    nb = m.shape[0]
    r, c = _iota2(CHUNK, CHUNK, 0), _iota2(CHUNK, CHUNK, 1)
    t = _bcast((r == c).astype(F32), nb)
    b = 1
    while b < CHUNK:
        sh = b.bit_length() - 1
        off = ((r >> (sh + 1)) == (c >> (sh + 1))) & ((r >> sh) != (c >> sh)) & (r > c)
        cl = jnp.where(off[None], m, 0.0)
        t = t - _raw_dot(_raw_dot(t, cl, False, False, False), t, False, False, False)
        b *= 2
    return t


@jax.custom_vjp
def _tri_inv_given(m, t):
    return t


def _tri_inv_fwd(m, t):
    return t, t


def _tri_inv_bwd(t, g):
    return -_raw_dot(_raw_dot(t, g, True, False, False), t, False, True, False), jnp.zeros_like(t)


_tri_inv_given.defvjp(_tri_inv_fwd, _tri_inv_bwd)


def _rms(h, g):
    return h * lax.rsqrt(jnp.mean(h * h, axis=-1, keepdims=True) + EPS) * g


class In(NamedTuple):
    arr: jax.Array
    spec: pl.BlockSpec
    grad: object = None
    acc: bool = False
    gshape: object = None
    gspec: object = None
    galias: object = None


def _params(grid):
    sem = ("arbitrary",) * len(grid)
    return pltpu.CompilerParams(dimension_semantics=sem, vmem_limit_bytes=VMEM_LIMIT)


def _stage_fwd(name, fn, grid, ins, out_shapes, out_specs):
    n_in = len(ins)

    def body(*refs):
        pids = tuple(pl.program_id(k) for k in range(len(grid)))
        vals = [r[...].astype(F32) for r in refs[:n_in]]
        outs = fn(pids, *vals)
        for o_ref, o in zip(refs[n_in:], outs):
            o_ref[...] = o.reshape(o_ref.shape).astype(o_ref.dtype)

    return _pallas(
        body, out_shape=out_shapes, grid=grid, in_specs=[i.spec for i in ins],
        out_specs=out_specs, name=name, compiler_params=_params(grid))(*[i.arr for i in ins])


def _stage_bwd(name, fn, grid, ins, cots):
    n_in, n_ct = len(ins), len(cots)
    didx = [k for k, i in enumerate(ins) if i.grad is not None]
    aliased = [(o, ins[k].galias) for o, k in enumerate(didx) if ins[k].galias is not None]
    n_al = len(aliased)

    def body(*refs):
        pids = tuple(pl.program_id(k) for k in range(len(grid)))
        vals = [r[...].astype(F32) for r in refs[:n_in]]
        ct_refs = refs[n_in:n_in + n_ct]
        g_refs = refs[n_in + n_ct + n_al:]

        def f(*dv):
            merged = list(vals)
            for k, v in zip(didx, dv):
                merged[k] = v
            return tuple(fn(pids, *merged))

        outs, vjp_fn = jax.vjp(f, *[vals[k].astype(F32) for k in didx])
        cts = tuple(c[...].reshape(o.shape).astype(F32) for c, o in zip(ct_refs, outs))
        grads = vjp_fn(cts)
        first = functools.reduce(jnp.logical_and, [p == 0 for p in pids])
        for k, g_ref, g in zip(didx, g_refs, grads):
            if ins[k].acc:
                @pl.when(first)
                def _(g_ref=g_ref):
                    g_ref[...] = jnp.zeros(g_ref.shape, g_ref.dtype)
                g_ref[...] += g.reshape(g_ref.shape).astype(g_ref.dtype)
            else:
                g_ref[...] = g.reshape(g_ref.shape).astype(g_ref.dtype)

    out_shapes = [jax.ShapeDtypeStruct(ins[k].gshape or ins[k].arr.shape, ins[k].grad) for k in didx]
    out_specs = [ins[k].gspec or ins[k].spec for k in didx]
    return _pallas(
        body, out_shape=out_shapes, grid=grid,
        in_specs=[i.spec for i in ins] + [c[1] for c in cots] + [ANY] * n_al, out_specs=out_specs,
        input_output_aliases={n_in + n_ct + a: o for a, (o, _) in enumerate(aliased)},
        name=name, compiler_params=_params(grid))(*[i.arr for i in ins], *[c[0] for c in cots], *[a for _, a in aliased])


def _full(arr):
    nd = arr.ndim
    return pl.BlockSpec(arr.shape, lambda *p: (0,) * nd)


def _rows(tr, width, blk=0):
    return pl.BlockSpec((tr, width), lambda i: (i, blk))


def _mm(name, a, b, *, tm, tn, tk, out_dtype=F32, add=None, comm=None):
    M, K = a.shape
    N = b.shape[1] if b.ndim == 2 else b.shape[0] * b.shape[2]
    nk = K // tk
    grid = (M // tm, N // tn, nk)
    n_in = 3 if add is not None else 2
    n_ci, n_co = (len(comm.ins), len(comm.outs)) if comm is not None else (0, 0)

    def body(*refs):
        a_ref, b_ref = refs[0], refs[1]
        add_ref = refs[2] if add is not None else None
        c_ins = refs[n_in:n_in + n_ci]
        o_ref = refs[n_in + n_ci]
        c_outs = refs[n_in + n_ci + 1:n_in + n_ci + 1 + n_co]
        scratch = refs[n_in + n_ci + 1 + n_co:]
        acc_ref = scratch[0] if nk > 1 else None
        sems = scratch[1 if nk > 1 else 0:]
        step = (pl.program_id(0) * grid[1] + pl.program_id(1)) * nk + pl.program_id(2)
        if comm is not None:
            @pl.when(step == 0)
            def _():
                comm.start(c_ins, c_outs, sems)

        part = _raw_dot(a_ref[...], b_ref[...], False, False, False)

        def finish(total):
            if add_ref is not None:
                total = total + add_ref[...]
            o_ref[...] = total.astype(o_ref.dtype)

        if nk == 1:
            finish(part)
        else:
            k = pl.program_id(2)

            @pl.when(k == 0)
            def _():
                acc_ref[...] = part

            @pl.when(k > 0)
            def _():
                acc_ref[...] += part

            @pl.when(k == nk - 1)
            def _():
                finish(acc_ref[...])

        if comm is not None:
            @pl.when(step == grid[0] * grid[1] * nk - 1)
            def _():
                comm.finish(c_ins, c_outs, sems)

    b_spec = (pl.BlockSpec((tk, tn), lambda i, j, k: (k, j)) if b.ndim == 2 else
              pl.BlockSpec((None, tk, tn), lambda i, j, k: (j, k, 0)))
    in_specs = [pl.BlockSpec((tm, tk), lambda i, j, k: (i, k)), b_spec]
    args = [a, b]
    if add is not None:
        in_specs.append(pl.BlockSpec((tm, tn), lambda i, j, k: (i, j)))
        args.append(add)
    out_shape = jax.ShapeDtypeStruct((M, N), out_dtype)
    out_spec = pl.BlockSpec((tm, tn), lambda i, j, k: (i, j))
    scratch = [pltpu.VMEM((tm, tn), F32)] if nk > 1 else []
    if comm is None:
        return _pallas(body, out_shape=out_shape, grid=grid, in_specs=in_specs, out_specs=out_spec,
                       scratch_shapes=scratch, name=name, compiler_params=_params(grid))(*args)
    res = _pallas(body, out_shape=[out_shape] + comm.outs, grid=grid, in_specs=in_specs + [ANY] * n_ci,
                  out_specs=[out_spec] + [ANY] * n_co, scratch_shapes=scratch + comm.sems, name=name,
                  compiler_params=_params(grid))(*args, *comm.ins)
    return res[0], res[1:]


def _mm_tn(name, a, b, *, tr, tka, tn, blocked=False):
    R, Ka = a.shape
    N = b.shape[1]
    nr = R // tr
    grid = (Ka // tka, N // tn, nr)
    if blocked:
        out_shape = jax.ShapeDtypeStruct((N // tn, Ka, tn), F32)
        out_spec = pl.BlockSpec((None, tka, tn), lambda i, j, r: (j, i, 0))
    else:
        out_shape = jax.ShapeDtypeStruct((Ka, N), F32)
        out_spec = pl.BlockSpec((tka, tn), lambda i, j, r: (i, j))

    def body(a_ref, b_ref, o_ref):
        r = pl.program_id(2)
        part = _raw_dot(a_ref[...], b_ref[...], True, False, False)

        @pl.when(r == 0)
        def _():
            o_ref[...] = part

        @pl.when(r > 0)
        def _():
            o_ref[...] += part

    return _pallas(
        body, out_shape=out_shape, grid=grid,
        in_specs=[pl.BlockSpec((tr, tka), lambda i, j, r: (r, i)),
                  pl.BlockSpec((tr, tn), lambda i, j, r: (r, j))],
        out_specs=out_spec, name=name, compiler_params=_params(grid))(a, b)


def _conv_fwd(name, x, xcol0, w, b, *, taps, width, tr, tc):
    R = x.shape[0]
    grid = (width // tc, R // tr)
    cb0 = xcol0 // tc
    hrows = 16 if x.dtype == BF16 else 8
    hb = tr // hrows

    def body(*refs):
        x_ref, xp_ref, w_ref = refs[:3]
        b_ref = refs[3] if b is not None else None
        o_ref = refs[-1]
        i = pl.program_id(1)
        xv = x_ref[...].astype(F32)
        prev = jnp.where(i > 0, xp_ref[...].astype(F32)[hrows - 8:, :], 0.0)
        ext = jnp.concatenate([prev, xv], axis=0)
        acc = xv * w_ref[taps - 1:taps, :]
        for s in range(1, taps):
            acc = acc + pltpu.roll(ext, s, 0)[8:, :] * w_ref[taps - 1 - s:taps - s, :]
        if b_ref is not None:
            acc = acc + b_ref[...]
        o_ref[...] = acc.astype(o_ref.dtype)

    in_specs = [pl.BlockSpec((tr, tc), lambda j, i: (i, cb0 + j)),
                pl.BlockSpec((hrows, tc), lambda j, i: (jnp.maximum(i * hb - 1, 0), cb0 + j)),
                pl.BlockSpec((taps, tc), lambda j, i: (0, j))]
    args = [x, x, w]
    if b is not None:
        in_specs.append(pl.BlockSpec((1, tc), lambda j, i: (0, j)))
        args.append(b)
    return _pallas(
        body, out_shape=jax.ShapeDtypeStruct((R, width), BF16), grid=grid, in_specs=in_specs,
        out_specs=pl.BlockSpec((tr, tc), lambda j, i: (i, j)),
        name=name, compiler_params=_params(grid))(*args)


def _conv_bwd(name, x, xcol0, w, dy, *, taps, width, tr, tc, with_bias, dx_into=None):
    R = x.shape[0]
    nr = R // tr
    grid = (width // tc, nr)
    cb0 = xcol0 // tc
    hrows = 16 if dy.dtype == BF16 else 8
    hb = tr // hrows
    n_ext = tr + 8
    n_al = 0 if dx_into is None else 1

    def body(*refs):
        x_ref, w_ref, dy_ref, dyn_ref = refs[:4]
        dx_ref, dw_ref = refs[4 + n_al], refs[5 + n_al]
        db_ref = refs[6 + n_al] if with_bias else None
        i = pl.program_id(1)
        xv = x_ref[...].astype(F32)
        dyv = dy_ref[...].astype(F32)
        nxt = dyn_ref[...].astype(F32)[:8, :]
        dext = jnp.concatenate([dyv, jnp.where(i < nr - 1, nxt, 0.0)], axis=0)
        dx = dyv * w_ref[taps - 1:taps, :]
        dws = [None] * taps
        dws[taps - 1] = jnp.sum(xv * dyv, axis=0, keepdims=True)
        for s in range(1, taps):
            ahead = pltpu.roll(dext, n_ext - s, 0)[:tr, :]
            dx = dx + ahead * w_ref[taps - 1 - s:taps - s, :]
            dws[taps - 1 - s] = jnp.sum(xv * ahead, axis=0, keepdims=True)
        dx_ref[...] = dx.astype(dx_ref.dtype)

        @pl.when(i == 0)
        def _():
            for k in range(taps):
                dw_ref[k:k + 1, :] = dws[k]
            if db_ref is not None:
                db_ref[...] = jnp.sum(dyv, axis=0, keepdims=True)

        @pl.when(i > 0)
        def _():
            for k in range(taps):
                dw_ref[k:k + 1, :] += dws[k]
            if db_ref is not None:
                db_ref[...] += jnp.sum(dyv, axis=0, keepdims=True)

    in_specs = [pl.BlockSpec((tr, tc), lambda j, i: (i, cb0 + j)),
                pl.BlockSpec((taps, tc), lambda j, i: (0, j)),
                pl.BlockSpec((tr, tc), lambda j, i: (i, j)),
                pl.BlockSpec((hrows, tc), lambda j, i: (jnp.minimum((i + 1) * hb, R // hrows - 1), j))]
    args = [x, w, dy, dy]
    if dx_into is None:
        dx_shape, dx_spec, aliases = jax.ShapeDtypeStruct((R, width), BF16), pl.BlockSpec((tr, tc), lambda j, i: (i, j)), {}
    else:
        dx_shape = jax.ShapeDtypeStruct(dx_into.shape, dx_into.dtype)
        dx_spec, aliases = pl.BlockSpec((tr, tc), lambda j, i: (i, cb0 + j)), {4: 0}
        in_specs.append(ANY)
        args.append(dx_into)
    out_shape = [dx_shape, jax.ShapeDtypeStruct((taps, width), F32)]
    out_specs = [dx_spec, pl.BlockSpec((taps, tc), lambda j, i: (0, j))]
    if with_bias:
        out_shape.append(jax.ShapeDtypeStruct((1, width), F32))
        out_specs.append(pl.BlockSpec((1, tc), lambda j, i: (0, j)))
    return _pallas(
        body, out_shape=out_shape, grid=grid, in_specs=in_specs, out_specs=out_specs, input_output_aliases=aliases,
        name=name, compiler_params=_params(grid))(*args)


def _row_mask(cfg, i, tr):
    rows = i * tr + lax.broadcasted_iota(jnp.int32, (tr, 1), 0)
    return (rows >= cfg.front).astype(F32)


def _make_rms_fn(cfg, tr, with_residual):
    def fn(pids, h, g):
        hm = h * _row_mask(cfg, pids[0], tr)
        if with_residual:
            return _rms(hm, g), hm
        return (_rms(hm, g),)
    return fn


def _make_gdn_prep_fn(cfg, tr):
    d, hg = cfg.d, cfg.hg

    def fn(pids, c, tail, alog, dtb):
        cq, ck, cv = c[:, :d], c[:, d:2 * d], c[:, 2 * d:]
        mask = _row_mask(cfg, pids[0], tr)
        j, col = _iota2(LANES, d, 0), _iota2(LANES, d, 1)
        ea = ((col >> 7) == j).astype(F32)
        eb = ((col >> 7) + hg == j).astype(F32)
        al = jnp.sum(alog, axis=0, keepdims=True)
        db = jnp.sum(dtb, axis=0, keepdims=True)
        lg = _dot_sel(-jnp.exp(al) * _softplus(tail + db) * mask, ea, False)
        beta = _dot_sel(_sigmoid(tail) * mask, eb, False)
        sq, sk, sv = _silu(cq), _silu(ck), _silu(cv)
        qs, ks = [], []
        for h in range(hg):
            sl = slice(h * GDN_DK, (h + 1) * GDN_DK)
            qh, kh = sq[:, sl], sk[:, sl]
            qs.append(qh * lax.rsqrt(jnp.sum(qh * qh, axis=-1, keepdims=True) + EPS) * (GDN_DK ** -0.5))
            ks.append(kh * lax.rsqrt(jnp.sum(kh * kh, axis=-1, keepdims=True) + EPS))
        return jnp.concatenate(qs, axis=1), jnp.concatenate(ks, axis=1), sv, beta, lg
    return fn


def _gdn_intra_fn(pids, q, k, v, bB, lB, t_saved=None):
    rows = q.shape[0]
    nb = rows // CHUNK
    q3, k3, v3, b3, l3 = [t.reshape(nb, CHUNK, GDN_DK) for t in (q, k, v, bB, lB)]
    r, c = _iota2(CHUNK, CHUNK, 0), _iota2(CHUNK, CHUNK, 1)
    tril = (r >= c)
    strict = (r > c)
    gcol = _sel_dot(_bcast(tril.astype(F32), nb), l3)
    grow = jnp.swapaxes(gcol, 1, 2)[:, :CHUNK, :]
    diff = gcol[:, :, :CHUNK] - grow
    decay = jnp.where(tril[None], jnp.exp(jnp.where(tril[None], diff, 0.0)), 0.0)
    kb = k3 * b3
    m = jnp.where(strict[None], _dot(kb, k3, False, True) * decay, 0.0)
    t = _tri_inv_raw(m) if t_saved is None else _tri_inv_given(m, t_saved.reshape(nb, CHUNK, CHUNK))
    eg = jnp.exp(gcol)
    u = _dot(t, v3 * b3)
    w = _dot(t, kb * eg)
    attn = _dot(q3, k3, False, True) * decay
    qd = q3 * eg
    glast = jnp.sum(l3, axis=1, keepdims=True)
    kd = k3 * jnp.exp(glast - gcol)
    gl = jnp.exp(glast)
    outs = (u.reshape(rows, GDN_DK), w.reshape(rows, GDN_DK), attn.reshape(1, rows, CHUNK),
            qd.reshape(rows, GDN_DK), kd.reshape(rows, GDN_DK), gl.reshape(1, nb, 1, GDN_DK))
    return outs + (t.reshape(1, rows, CHUNK),) if t_saved is None else outs


def _make_rot_fn(cfg):
    hr = cfg.hr
    half = RET_DK // 2

    def fn(pids, rqk, cos, sin):
        rq, rk = rqk[:, :cfg.d], rqk[:, cfg.d:]

        def rot(t, scale):
            outs = []
            for h in range(hr):
                x1 = t[:, h * RET_DK:h * RET_DK + half]
                x2 = t[:, h * RET_DK + half:(h + 1) * RET_DK]
                outs += [(x1 * cos - x2 * sin) * scale, (x2 * cos + x1 * sin) * scale]
            return jnp.concatenate(outs, axis=1)
        return rot(rq, 1.0), rot(rk, RET_DK ** -0.5)
    return fn


def _make_mix_fn(cfg):
    hg, hr = cfg.hg, cfg.hr

    def fn(pids, oa, ob, pm, gnorm):
        d = cfg.d
        gz, rg, gate_a, gate_b = pm[:, :d], pm[:, d:2 * d], pm[:, 2 * d:3 * d], pm[:, 3 * d:]
        oas = []
        for h in range(hg):
            oh = oa[:, h * GDN_DK:(h + 1) * GDN_DK]
            oas.append(oh * lax.rsqrt(jnp.mean(oh * oh, axis=-1, keepdims=True) + EPS) * gnorm)
        ya = jnp.concatenate(oas, axis=1) * _silu(gz)
        obs = []
        for h in range(hr):
            oh = ob[:, h * RET_DK:(h + 1) * RET_DK]
            obs.append(oh * lax.rsqrt(jnp.mean(oh * oh, axis=-1, keepdims=True) + EPS))
        yb = _silu(rg) * jnp.concatenate(obs, axis=1)
        return (_sigmoid(gate_a) * ya + _sigmoid(gate_b) * yb,)
    return fn


def _act_fn(pids, u):
    f = u.shape[1] // 2
    return (_silu(u[:, :f]) * u[:, f:],)


def _gdn_step(s, u, w, a, qd, kd, gl):
    top = _dot(jnp.concatenate([w, qd], axis=0), s)
    v_new = u - top[:CHUNK]
    bot = _dot(jnp.concatenate([a, kd.T], axis=0), v_new)
    o = top[CHUNK:] + bot[:CHUNK]
    s2 = s * gl + bot[CHUNK:]
    return s2, o


def _gdn_step_bwd(s, u, w, a, qd, kd, gl, ds2, do):
    lw = jnp.concatenate([w, qd], axis=0)
    v_new = u - _raw_dot(w, s, False, False, False)
    dv = _raw_dot(a, do, True, False, False) + _raw_dot(kd, ds2, False, False, False)
    da = _raw_dot(do, v_new, False, True, False)
    dkd = _raw_dot(v_new, ds2, False, True, False)
    dtop = jnp.concatenate([-dv, do], axis=0)
    dlw = _raw_dot(dtop, s, False, True, False)
    ds = ds2 * gl + _raw_dot(lw, dtop, True, False, False)
    dgl = jnp.sum(ds2 * s, axis=0, keepdims=True)
    return ds, dv, dlw[:CHUNK], da, dlw[CHUNK:], dkd, dgl


def _ret_step(s, q, k, v, dm, qdc, kdc, g):
    att = _dot(q, k, False, True) * dm
    bot = _dot(jnp.concatenate([att, (k * kdc).T], axis=0), v)
    o = bot[:CHUNK] + _dot(q * qdc, s)
    s2 = s * g + bot[CHUNK:]
    return s2, o


class Part(NamedTuple):
    body: object
    args: list
    in_specs: list
    out_shape: list
    out_specs: list
    scratch: list
    aliases: dict = {}


def _run_parts(name, grid, parts):
    n_in = [len(p.args) for p in parts]
    n_out = [len(p.out_shape) for p in parts]
    n_sc = [len(p.scratch) for p in parts]
    off_in = [sum(n_in[:k]) for k in range(len(parts))]
    off_out = [sum(n_out[:k]) for k in range(len(parts))]
    off_sc = [sum(n_sc[:k]) for k in range(len(parts))]

    def body(*refs):
        ins, outs, scr = refs[:sum(n_in)], refs[sum(n_in):sum(n_in) + sum(n_out)], refs[sum(n_in) + sum(n_out):]
        for k, p in enumerate(parts):
            p.body(*ins[off_in[k]:off_in[k] + n_in[k]], *outs[off_out[k]:off_out[k] + n_out[k]],
                   *scr[off_sc[k]:off_sc[k] + n_sc[k]])

    aliases = {off_in[k] + i: off_out[k] + o for k, p in enumerate(parts) for i, o in p.aliases.items()}
    res = _pallas(
        body, out_shape=sum((p.out_shape for p in parts), []), grid=grid, in_specs=sum((p.in_specs for p in parts), []),
        out_specs=sum((p.out_specs for p in parts), []), scratch_shapes=sum((p.scratch for p in parts), []),
        input_output_aliases=aliases, name=name, compiler_params=_params(grid))(*sum((p.args for p in parts), []))
    return [res[off_out[k]:off_out[k] + n_out[k]] for k in range(len(parts))]


def _gdn_scan_fwd(cfg, u, w, attn, qd, kd, gl):
    d, hg, nch, sc = cfg.d, cfg.hg, cfg.nch, cfg.sc
    nst = nch // sc

    def body(u_ref, w_ref, a_ref, qd_ref, kd_ref, gl_ref, o_ref, ss_ref, s_ref):
        @pl.when(pl.program_id(0) == 0)
        def _():
            s_ref[...] = jnp.zeros(s_ref.shape, F32)

        states = [s_ref[h] for h in range(hg)]
        for j in range(sc):
            rows = slice(j * CHUNK, (j + 1) * CHUNK)
            outs = []
            for h in range(hg):
                sl = slice(h * GDN_DK, (h + 1) * GDN_DK)
                ss_ref[j, h] = states[h]
                states[h], o = _gdn_step(states[h], u_ref[rows, sl], w_ref[rows, sl], a_ref[h, rows, :],
                                         qd_ref[rows, sl], kd_ref[rows, sl], gl_ref[h, j])
                outs.append(o)
            o_ref[rows, :] = jnp.concatenate(outs, axis=1)
        for h in range(hg):
            s_ref[h] = states[h]

    row = pl.BlockSpec((sc * CHUNK, d), lambda n: (n, 0))
    return Part(
        body, [u, w, attn, qd, kd, gl],
        [row, row, pl.BlockSpec((hg, sc * CHUNK, CHUNK), lambda n: (0, n, 0)), row, row,
         pl.BlockSpec((hg, sc, 1, GDN_DK), lambda n: (0, n, 0, 0))],
        [jax.ShapeDtypeStruct((cfg.rp, d), F32), jax.ShapeDtypeStruct((nch, hg, GDN_DK, GDN_DK), F32)],
        [row, pl.BlockSpec((sc, hg, GDN_DK, GDN_DK), lambda n: (n, 0, 0, 0))],
        [pltpu.VMEM((hg, GDN_DK, GDN_DK), F32)])


def _gdn_scan_bwd(cfg, do, u, w, attn, qd, kd, gl, ss):
    d, hg, nch, sc = cfg.d, cfg.hg, cfg.nch, cfg.sc
    nst = nch // sc

    def body(do_ref, u_ref, w_ref, a_ref, qd_ref, kd_ref, gl_ref, ss_ref,
             du_ref, dw_ref, da_ref, dqd_ref, dkd_ref, dgl_ref, ds_ref):
        @pl.when(pl.program_id(0) == 0)
        def _():
            ds_ref[...] = jnp.zeros(ds_ref.shape, F32)

        dstates = [ds_ref[h] for h in range(hg)]
        for j in reversed(range(sc)):
            rows = slice(j * CHUNK, (j + 1) * CHUNK)
            dus, dws, dqds, dkds = [], [], [], []
            for h in range(hg):
                sl = slice(h * GDN_DK, (h + 1) * GDN_DK)
                args = (ss_ref[j, h], u_ref[rows, sl], w_ref[rows, sl], a_ref[h, rows, :], qd_ref[rows, sl],
                        kd_ref[rows, sl], gl_ref[h, j])
                dstates[h], du, dw, da, dqd, dkd, dgl = _gdn_step_bwd(*args, dstates[h], do_ref[rows, sl])
                da_ref[h, rows, :] = da
                dgl_ref[h, j] = dgl
                dus.append(du)
                dws.append(dw)
                dqds.append(dqd)
                dkds.append(dkd)
            du_ref[rows, :] = jnp.concatenate(dus, axis=1)
            dw_ref[rows, :] = jnp.concatenate(dws, axis=1)
            dqd_ref[rows, :] = jnp.concatenate(dqds, axis=1)
            dkd_ref[rows, :] = jnp.concatenate(dkds, axis=1)
        for h in range(hg):
            ds_ref[h] = dstates[h]

    row = pl.BlockSpec((sc * CHUNK, d), lambda n: (nst - 1 - n, 0))
    aspec = pl.BlockSpec((hg, sc * CHUNK, CHUNK), lambda n: (0, nst - 1 - n, 0))
    gspec = pl.BlockSpec((hg, sc, 1, GDN_DK), lambda n: (0, nst - 1 - n, 0, 0))
    rowshape = jax.ShapeDtypeStruct((cfg.rp, d), F32)
    return Part(
        body, [do, u, w, attn, qd, kd, gl, ss],
        [row, row, row, aspec, row, row, gspec, pl.BlockSpec((sc, hg, GDN_DK, GDN_DK), lambda n: (nst - 1 - n, 0, 0, 0))],
        [rowshape, rowshape, jax.ShapeDtypeStruct(attn.shape, F32), rowshape, rowshape, jax.ShapeDtypeStruct(gl.shape, F32)],
        [row, row, aspec, row, row, gspec],
        [pltpu.VMEM((hg, GDN_DK, GDN_DK), F32)])


def _ret_consts(cfg):
    hr = cfg.hr
    lg = np.log(1.0 - 2.0 ** (-5.0 - np.arange(hr, dtype=np.float64)))
    idx = np.arange(CHUNK, dtype=np.float64)
    tril = np.tril(np.ones((CHUNK, CHUNK), dtype=bool))
    dm = np.where(tril[None], np.exp((idx[:, None] - idx[None, :])[None] * lg[:, None, None]), 0.0)
    qdc = np.exp((idx[None, :] + 1.0) * lg[:, None])
    kdc = np.exp((CHUNK - 1.0 - idx[None, :]) * lg[:, None])
    gch = np.exp(CHUNK * lg)
    qdc = np.broadcast_to(qdc[:, :, None], (hr, CHUNK, RET_DK))
    kdc = np.broadcast_to(kdc[:, :, None], (hr, CHUNK, RET_DK))
    gch = np.broadcast_to(gch[:, None, None], (hr, 1, RET_DK))
    return tuple(jnp.asarray(np.ascontiguousarray(t), F32) for t in (dm, qdc, kdc, gch))


def _ret_scan_fwd(cfg, qr, kr, proj, consts):
    d, hr, nch, sc = cfg.d, cfg.hr, cfg.nch, cfg.sc
    nst = nch // sc
    dm, qdc, kdc, gch = consts

    def body(q_ref, k_ref, v_ref, dm_ref, qdc_ref, kdc_ref, g_ref, o_ref, ss_ref, s_ref):
        @pl.when(pl.program_id(0) == 0)
        def _():
            s_ref[...] = jnp.zeros(s_ref.shape, F32)

        states = [s_ref[h] for h in range(hr)]
        for j in range(sc):
            rows = slice(j * CHUNK, (j + 1) * CHUNK)
            outs = []
            for h in range(hr):
                sl = slice(h * RET_DK, (h + 1) * RET_DK)
                ss_ref[j, h] = states[h]
                states[h], o = _ret_step(states[h], q_ref[rows, sl], k_ref[rows, sl], v_ref[rows, sl], dm_ref[h],
                                         qdc_ref[h], kdc_ref[h], g_ref[h])
                outs.append(o)
            o_ref[rows, :] = jnp.concatenate(outs, axis=1)
        for h in range(hr):
            s_ref[h] = states[h]

    row = pl.BlockSpec((sc * CHUNK, d), lambda n: (n, 0))
    return Part(
        body, [qr, kr, proj, dm, qdc, kdc, gch],
        [row, row, pl.BlockSpec((sc * CHUNK, d), lambda n: (n, RV_BLOCK)), _full(dm), _full(qdc), _full(kdc), _full(gch)],
        [jax.ShapeDtypeStruct((cfg.rp, d), F32), jax.ShapeDtypeStruct((nch, hr, RET_DK, RET_DK), F32)],
        [row, pl.BlockSpec((sc, hr, RET_DK, RET_DK), lambda n: (n, 0, 0, 0))],
        [pltpu.VMEM((hr, RET_DK, RET_DK), F32)])


def _ret_scan_bwd(cfg, do, qr, kr, proj, consts, ss, dproj):
    d, hr, nch, sc = cfg.d, cfg.hr, cfg.nch, cfg.sc
    nst = nch // sc
    dm, qdc, kdc, gch = consts

    def body(do_ref, q_ref, k_ref, v_ref, dm_ref, qdc_ref, kdc_ref, g_ref, ss_ref, _, dq_ref, dk_ref, dv_ref, ds_ref):
        @pl.when(pl.program_id(0) == 0)
        def _():
            ds_ref[...] = jnp.zeros(ds_ref.shape, F32)

        dstates = [ds_ref[h] for h in range(hr)]
        for j in reversed(range(sc)):
            rows = slice(j * CHUNK, (j + 1) * CHUNK)
            dqs, dks, dvs = [], [], []
            for h in range(hr):
                sl = slice(h * RET_DK, (h + 1) * RET_DK)
                cs = (dm_ref[h], qdc_ref[h], kdc_ref[h], g_ref[h])
                _, vjp_fn = jax.vjp(lambda s, q, k, v, cs=cs: _ret_step(s, q, k, v, *cs),
                                    ss_ref[j, h], q_ref[rows, sl].astype(F32), k_ref[rows, sl].astype(F32), v_ref[rows, sl])
                dstates[h], dq, dk, dv = vjp_fn((dstates[h], do_ref[rows, sl]))
                dqs.append(dq)
                dks.append(dk)
                dvs.append(dv)
            dq_ref[rows, :] = jnp.concatenate(dqs, axis=1).astype(dq_ref.dtype)
            dk_ref[rows, :] = jnp.concatenate(dks, axis=1).astype(dk_ref.dtype)
            dv_ref[rows, :] = jnp.concatenate(dvs, axis=1).astype(dv_ref.dtype)
        for h in range(hr):
            ds_ref[h] = dstates[h]

    row = pl.BlockSpec((sc * CHUNK, d), lambda n: (nst - 1 - n, 0))
    rowshape = jax.ShapeDtypeStruct((cfg.rp, d), BF16)
    vspec = pl.BlockSpec((sc * CHUNK, d), lambda n: (nst - 1 - n, RV_BLOCK))
    return Part(
        body, [do, qr, kr, proj, dm, qdc, kdc, gch, ss, dproj],
        [row, row, row, vspec, _full(dm), _full(qdc), _full(kdc), _full(gch),
         pl.BlockSpec((sc, hr, RET_DK, RET_DK), lambda n: (nst - 1 - n, 0, 0, 0)), ANY],
        [rowshape, rowshape, jax.ShapeDtypeStruct(dproj.shape, dproj.dtype)],
        [row, row, vspec],
        [pltpu.VMEM((hr, RET_DK, RET_DK), F32)], {9: 2})


def _final(cfg, h2, normf, tgt):
    d, tr = cfg.d, cfg.xrow
    nr = cfg.rp // tr

    def body(h_ref, g_ref, t_ref, dh_ref, dg_ref, loss_ref):
        i = pl.program_id(0)
        y, vjp_fn = jax.vjp(_rms, h_ref[...], g_ref[...])
        err = jnp.where(i >= 1, y - t_ref[...], 0.0)
        dh, dg = vjp_fn(err * (1.0 / d))
        dh_ref[...] = dh
        part = jnp.zeros((8, LANES), F32) + 0.5 * jnp.sum(err * err) * (1.0 / d)

        @pl.when(i == 0)
        def _():
            dg_ref[...] = dg
            loss_ref[...] = part

        @pl.when(i > 0)
        def _():
            dg_ref[...] += dg
            loss_ref[...] += part

    return _pallas(
        body,
        out_shape=[jax.ShapeDtypeStruct((cfg.rp, d), F32), jax.ShapeDtypeStruct((1, d), F32),
                   jax.ShapeDtypeStruct((8, LANES), F32)],
        grid=(nr,),
        in_specs=[_rows(tr, d), _full(normf), pl.BlockSpec((tr, d), lambda i: (jnp.maximum(i - 1, 0), 0))],
        out_specs=[_rows(tr, d), pl.BlockSpec((1, d), lambda i: (0, 0)), pl.BlockSpec((8, LANES), lambda i: (0, 0))],
        name="final_loss", compiler_params=_params((nr,)))(h2, normf, tgt)


ANY = pl.BlockSpec(memory_space=pl.ANY)


def _place():
    x, y, c = lax.axis_index("x"), lax.axis_index("y"), lax.axis_index("c")
    others = [(1 - x, y), (x, 1 - y), (1 - x, 1 - y)]
    return x, y, c, others


def _row_tile(rows, cap=256):
    return max(t for t in range(16, min(rows, cap) + 1, 16) if rows % t == 0)


class Comm(NamedTuple):
    ins: list
    outs: list
    sems: list
    start: object
    finish: object


def _run_comm(name, comm):
    n_in, n_out = len(comm.ins), len(comm.outs)

    def body(*refs):
        ins, outs, sems = refs[:n_in], refs[n_in:n_in + n_out], refs[n_in + n_out:]
        comm.start(ins, outs, sems)
        comm.finish(ins, outs, sems)

    return _pallas(body, out_shape=comm.outs, in_specs=[ANY] * n_in, out_specs=[ANY] * n_out,
                   scratch_shapes=comm.sems, name=name)(*comm.ins)


def _gather_comm(ws):
    n = len(ws)
    halves = [w.shape[0] // 2 for w in ws]

    def copies(w_refs, o_refs, sems):
        send_sems, recv_sems = sems
        x, y, c, others = _place()
        me = 2 * x + y
        chips = [2 * px + py for px, py in others]

        def piece(a, chip, core):
            return o_refs[a].at[chip, pl.ds(core * halves[a], halves[a]), :]

        def copy(a, k, src, chip, core, to):
            return pltpu.make_async_remote_copy(src_ref=src, dst_ref=piece(a, chip, core), send_sem=send_sems.at[6 * a + k],
                                                recv_sem=recv_sems.at[6 * a + k], device_id=to, device_id_type=MESH)

        def first(j, a):
            return copy(a, j, w_refs[a].at[pl.ds(c * halves[a], halves[a]), :], me, c, (*others[j], c))

        def landed(j, a):
            return copy(a, j, piece(a, chips[j], c), chips[j], c, (x, y, c))

        def passed(j, a):
            return copy(a, 3 + j, piece(a, chips[j], c), chips[j], c, (x, y, 1 - c))

        def from_sibling(j, a):
            return copy(a, 3 + j, piece(a, chips[j], 1 - c), chips[j], 1 - c, (x, y, c))

        return first, landed, passed, from_sibling

    pairs = [(j, a) for j in range(3) for a in range(n)]

    def start(w_refs, o_refs, sems):
        first, _, _, _ = copies(w_refs, o_refs, sems)
        for j, a in pairs:
            first(j, a).start()

    def finish(w_refs, o_refs, sems):
        first, landed, passed, from_sibling = copies(w_refs, o_refs, sems)
        for j, a in pairs:
            landed(j, a).wait_recv()
            passed(j, a).start()
        for j, a in pairs:
            from_sibling(j, a).wait_recv()
        for j, a in pairs:
            first(j, a).wait_send()
            passed(j, a).wait_send()

    return Comm(list(ws), [jax.ShapeDtypeStruct((N_CHIPS,) + w.shape, w.dtype) for w in ws],
                [pltpu.SemaphoreType.DMA((6 * n,)), pltpu.SemaphoreType.DMA((6 * n,))], start, finish)


def _pair_exchange(name, gs):
    n = len(gs)

    def body(*refs):
        g_refs, o_refs = refs[:n], refs[n:2 * n]
        send_sems, recv_sems = refs[2 * n:]
        x, y, c, _ = _place()
        cps = []
        for a in range(n):
            half = gs[a].shape[1] // 2
            cp = pltpu.make_async_remote_copy(
                src_ref=g_refs[a].at[:, pl.ds((1 - c) * half, half), :], dst_ref=o_refs[a], send_sem=send_sems.at[a],
                recv_sem=recv_sems.at[a], device_id=(x, y, 1 - c), device_id_type=MESH)
            cp.start()
            cps.append(cp)
        for cp in cps:
            cp.wait()

    return _pallas(
        body, out_shape=[jax.ShapeDtypeStruct((N_CHIPS, g.shape[1] // 2, g.shape[2]), g.dtype) for g in gs],
        in_specs=[ANY] * n, out_specs=[ANY] * n,
        scratch_shapes=[pltpu.SemaphoreType.DMA((n,)), pltpu.SemaphoreType.DMA((n,))], name=name)(*gs)


def _pair_sum(name, g, recv, cidx):
    half, cols = recv.shape[1], recv.shape[2]
    tr = _row_tile(half)
    nblk = half // tr

    def body(c_ref, g_ref, r_ref, o_ref):
        o_ref[...] = (g_ref[...] + r_ref[...]).astype(o_ref.dtype)

    grid_spec = pltpu.PrefetchScalarGridSpec(
        num_scalar_prefetch=1, grid=(N_CHIPS, nblk),
        in_specs=[pl.BlockSpec((1, tr, cols), lambda s, i, c: (s, c[0] * nblk + i, 0)),
                  pl.BlockSpec((1, tr, cols), lambda s, i, c: (s, i, 0))],
        out_specs=pl.BlockSpec((1, tr, cols), lambda s, i, c: (s, i, 0)))
    return _pallas(
        body, out_shape=jax.ShapeDtypeStruct((N_CHIPS, half, cols), BF16), grid_spec=grid_spec,
        name=name, compiler_params=_params((N_CHIPS, nblk)))(cidx, g, recv)


def _exchange_comm(parts):
    n = len(parts)

    def copies(p_refs, o_refs, sems):
        send_sems, recv_sems = sems
        x, y, c, others = _place()
        me = 2 * x + y

        def copy(a, j, src_chip, dst_chip):
            px, py = others[j]
            return pltpu.make_async_remote_copy(
                src_ref=p_refs[a].at[src_chip], dst_ref=o_refs[a].at[dst_chip], send_sem=send_sems.at[3 * a + j],
                recv_sem=recv_sems.at[3 * a + j], device_id=(px, py, c), device_id_type=MESH)

        def send(j, a):
            return copy(a, j, 2 * others[j][0] + others[j][1], me)

        def arrival(j, a):
            return copy(a, j, me, 2 * others[j][0] + others[j][1])

        return send, arrival

    pairs = [(j, a) for j in range(3) for a in range(n)]

    def start(p_refs, o_refs, sems):
        send, _ = copies(p_refs, o_refs, sems)
        for j, a in pairs:
            send(j, a).start()

    def finish(p_refs, o_refs, sems):
        send, arrival = copies(p_refs, o_refs, sems)
        for j, a in pairs:
            arrival(j, a).wait_recv()
        for j, a in pairs:
            send(j, a).wait_send()

    return Comm(list(parts), [jax.ShapeDtypeStruct(p.shape, p.dtype) for p in parts],
                [pltpu.SemaphoreType.DMA((3 * n,)), pltpu.SemaphoreType.DMA((3 * n,))], start, finish)


def _chip_sum(name, part, slots, chip):
    half, cols = slots.shape[1], slots.shape[2]
    tr = _row_tile(half)

    def body(me_ref, p_ref, *rest):
        s_refs, o_ref = rest[:N_CHIPS], rest[N_CHIPS]
        own = p_ref[...].astype(F32)
        v = [jnp.where(me_ref[0] == k, own, s_refs[k][...].astype(F32)) for k in range(N_CHIPS)]
        o_ref[...] = ((v[0] + v[1]) + v[2]) + v[3]

    def slot_spec(k):
        return pl.BlockSpec((None, tr, cols), lambda i, me: (jnp.where(me[0] == k, (k + 1) % N_CHIPS, k), i, 0))

    grid_spec = pltpu.PrefetchScalarGridSpec(
        num_scalar_prefetch=1, grid=(half // tr,),
        in_specs=[pl.BlockSpec((None, tr, cols), lambda i, me: (me[0], i, 0))] + [slot_spec(k) for k in range(N_CHIPS)],
        out_specs=pl.BlockSpec((tr, cols), lambda i, me: (i, 0)))
    return _pallas(
        body, out_shape=jax.ShapeDtypeStruct((half, cols), F32), grid_spec=grid_spec,
        name=name, compiler_params=_params((half // tr,)))(chip, part, *([slots] * N_CHIPS))


def _pair_swap(fins):
    n = len(fins)

    def body(*refs):
        f_refs, o_refs = refs[:n], refs[n:2 * n]
        send_sems, recv_sems = refs[2 * n:]
        x, y, c, _ = _place()
        cps = [pltpu.make_async_remote_copy(src_ref=f_refs[a], dst_ref=o_refs[a], send_sem=send_sems.at[a],
                                            recv_sem=recv_sems.at[a], device_id=(x, y, 1 - c), device_id_type=MESH)
               for a in range(n)]
        for cp in cps:
            cp.start()
        for cp in cps:
            cp.wait()

    return _pallas(
        body, out_shape=[jax.ShapeDtypeStruct(f.shape, f.dtype) for f in fins], in_specs=[ANY] * n, out_specs=[ANY] * n,
        scratch_shapes=[pltpu.SemaphoreType.DMA((n,)), pltpu.SemaphoreType.DMA((n,))], name="grad_pair_swap")(*fins)


def _adamw(name, w, g_own, g_other, m, v, cidx):
    R, cols = w.shape[-2:]
    lead = (None,) * (w.ndim - 2)
    zeros = (0,) * (w.ndim - 2)
    half = R // 2
    tr = _row_tile(half, 128)
    nblk = half // tr
    c1 = 1.0 - ADAM_B1 ** ADAM_STEP
    c2 = 1.0 - ADAM_B2 ** ADAM_STEP

    def body(c_ref, w_ref, go_ref, gs_ref, m_ref, v_ref, g_ref, d_ref, nm_ref, nv_ref):
        mine = (pl.program_id(0) // nblk) == c_ref[0]
        gv = jnp.where(mine, go_ref[...], gs_ref[...])
        nm = ADAM_B1 * m_ref[...] + (1.0 - ADAM_B1) * gv
        nv = ADAM_B2 * v_ref[...] + (1.0 - ADAM_B2) * (gv * gv)
        g_ref[...] = gv
        d_ref[...] = -ADAM_LR * ((nm / c1) / (jnp.sqrt(nv / c2) + ADAM_EPS) + ADAM_WD * w_ref[...])
        nm_ref[...] = nm
        nv_ref[...] = nv

    spec = pl.BlockSpec(lead + (tr, cols), lambda i, c: zeros + (i, 0))
    hspec = pl.BlockSpec((tr, cols), lambda i, c: (i % nblk, 0))
    shape = jax.ShapeDtypeStruct(w.shape, F32)
    grid_spec = pltpu.PrefetchScalarGridSpec(num_scalar_prefetch=1, grid=(R // tr,),
                                             in_specs=[spec, hspec, hspec, spec, spec], out_specs=[spec] * 4)
    return _pallas(
        body, out_shape=[shape] * 4, grid_spec=grid_spec,
        name=name, compiler_params=_params((R // tr,)))(cidx, w, g_own, g_other, m, v)


PARAMS = (("meta", 1), ("norm1", None), ("w_in", 2), ("gdn_conv_w", 2), ("gdn_a_log", None), ("gdn_dt_bias", None),
          ("gdn_norm", None), ("w_out", 1), ("norm2", None), ("w_ffn_up", 2), ("ffn_conv_w", 2), ("ffn_conv_b", None),
          ("w_ffn_down", 1), ("norm_f", None))
BIG = ("w_in", "w_out", "w_ffn_up", "w_ffn_down")
PACK_ALIGN = 1024
PACK_ROWS_ALIGN = 32


def _pack(arrs, dtype):
    parts, total = [], 0
    for a in arrs:
        f = a.reshape(-1).astype(dtype)
        pad = (-f.shape[0]) % PACK_ALIGN
        parts.append(jnp.pad(f, (0, pad)) if pad else f)
        total += f.shape[0] + pad
    rows = total // LANES
    rpad = (-rows) % PACK_ROWS_ALIGN
    if rpad:
        parts.append(jnp.zeros((rpad * LANES,), dtype))
    return jnp.concatenate(parts).reshape(rows + rpad, LANES)


def _unpack(buf, shapes):
    flat = buf.reshape(-1)
    outs, off = [], 0
    for s in shapes:
        n = int(np.prod(s))
        outs.append(flat[off:off + n].reshape(s))
        off += n + (-n) % PACK_ALIGN
    return outs


def _split4(a, axis):
    n = a.shape[axis] // N_CHIPS
    return [lax.slice_in_dim(a, s * n, (s + 1) * n, axis=axis) for s in range(N_CHIPS)]


PROJ_ORDER = (3, 7, 8, 9, 0, 1, 2, 6, 4, 5)


def _reorder_w_in(w, cfg):
    d, hg = cfg.d, cfg.hg

    def block(k):
        off = k * d + (2 * hg if k >= 4 else 0)
        return w[:, off:off + d]

    tail = jnp.pad(w[:, 4 * d:4 * d + 2 * hg], ((0, 0), (0, LANES - 2 * hg)))
    return jnp.concatenate([block(k) for k in PROJ_ORDER] + [tail], axis=1)


def _restore_w_in(wr, cfg):
    d, hg = cfg.d, cfg.hg
    at = {k: i for i, k in enumerate(PROJ_ORDER)}
    block = lambda k: wr[:, at[k] * d:(at[k] + 1) * d]
    return jnp.concatenate([block(k) for k in range(4)] + [wr[:, 10 * d:10 * d + 2 * hg]] +
                           [block(k) for k in range(4, 10)], axis=1)


def _step(cfg, x, tgt, shard, m_shard, v_shard):
    d, hg, dff, rp, tr, tm = cfg.d, cfg.hg, cfg.dff, cfg.rp, cfg.tr, cfg.tm
    nrow = rp // tr
    assert cfg.tf * N_CHIPS == 2 * dff and cfg.din % N_CHIPS == 0
    cidx = lax.axis_index("c").astype(jnp.int32).reshape(1)
    chip = (2 * lax.axis_index("x") + lax.axis_index("y")).astype(jnp.int32).reshape(1)

    axis = dict(PARAMS)
    small = ("meta", "gdn_conv_w", "ffn_conv_w")
    small_shapes = [shard[n].shape for n in small]
    mine = [shard[n][0].astype(BF16) for n in BIG] + [_pack([shard[n] for n in small], F32)]

    def with_own(gathered, own):
        return [lax.dynamic_update_slice(g, w[None], (chip[0], 0, 0)) for g, w in zip(gathered, own)]

    g_in, g_small = _run_comm("weights_gather_first", _gather_comm([mine[0], mine[4]]))
    g_small, = with_own([g_small], [mine[4]])
    w_in_r = _reorder_w_in(jnp.concatenate([jnp.where(chip[0] == s, mine[0], g_in[s]) for s in range(N_CHIPS)], axis=1),
                           cfg)
    per_chip = [_unpack(g_small[s], small_shapes) for s in range(N_CHIPS)]
    full = {n: jnp.concatenate([per_chip[s][k] for s in range(N_CHIPS)], axis=axis[n]) for k, n in enumerate(small)}
    meta = full["meta"]
    gconv_w = full["gdn_conv_w"][0]
    fconv_w = full["ffn_conv_w"][0]
    norm1, norm2, gnorm = shard["norm1"], shard["norm2"], shard["gdn_norm"]
    normf = shard["norm_f"].reshape(1, d)
    fconv_b = shard["ffn_conv_b"]
    alog = jnp.pad(shard["gdn_a_log"], ((0, 7), (0, LANES - hg)))
    dtb = jnp.pad(shard["gdn_dt_bias"], ((0, 7), (0, LANES - hg)))

    h0 = jnp.concatenate([jnp.zeros((cfg.front, d), F32), meta, x], axis=0)
    half = RET_DK // 2
    pos = np.arange(rp, dtype=np.float32) - np.float32(cfg.front)
    inv = (np.float32(1.0) / np.float32(ROPE_BASE) ** (np.arange(half, dtype=np.float32) / np.float32(half))).astype(np.float32)
    ang = pos[:, None] * inv[None, :]
    cos, sin = jnp.asarray(np.cos(ang), F32), jnp.asarray(np.sin(ang), F32)
    rconsts = _ret_consts(cfg)

    tr_n = 3 * tr if rp % (3 * tr) == 0 else tr
    rms_f = _make_rms_fn(cfg, tr_n, False)
    rms_b = _make_rms_fn(cfg, tr_n, True)
    rowshape = jax.ShapeDtypeStruct((rp, d), F32)
    rspec = _rows(tr, d)
    nspec = _rows(tr_n, d)

    def rms_fwd(name, h, g):
        return _stage_fwd(name, rms_f, (rp // tr_n,), [In(h, nspec), In(g, _full(g))],
                          [jax.ShapeDtypeStruct((rp, d), BF16)], [nspec])[0]

    wm = 10 * d
    w_main, w_tail = w_in_r[:, :wm], w_in_r[:, wm:]
    tn_in = 2560 if wm % 2560 == 0 else LANES
    hn1 = rms_fwd("rms1_fwd", h0, norm1)
    proj, rest = _mm("proj_fwd", hn1, w_main, tm=tm, tn=tn_in, tk=d, out_dtype=BF16, comm=_gather_comm(mine[1:4]))
    ptail = _mm("proj_tail_fwd", hn1, w_tail, tm=tm, tn=LANES, tk=d)
    g_out, g_up, g_down = with_own(rest, mine[1:4])
    w_out = g_out.reshape(d, d)
    w_up = g_up
    w_up_t = jnp.swapaxes(g_up, 1, 2).reshape(2 * dff, d)
    w_down = g_down.reshape(dff, d)
    cqkv = _conv_fwd("gdn_conv_fwd", proj, CONV_COL * d, gconv_w, None, taps=GDN_CONV, width=3 * d, tr=tr, tc=d)
    prep_fn = _make_gdn_prep_fn(cfg, tr)
    prep_ins = [In(cqkv, _rows(tr, 3 * d), BF16), In(ptail, _rows(tr, LANES), BF16),
                In(alog, _full(alog), F32, True), In(dtb, _full(dtb), F32, True)]
    qn, kn, vv, bB, lB = _stage_fwd("gdn_prep_fwd", prep_fn, (nrow,), prep_ins, [rowshape] * 5, [rspec] * 5)

    trg = cfg.nb * CHUNK
    gi_grid = (rp // trg, hg)
    hspec = pl.BlockSpec((trg, GDN_DK), lambda i, h: (i, h))
    aspec = pl.BlockSpec((1, trg, CHUNK), lambda i, h: (h, i, 0))
    gspec = pl.BlockSpec((1, cfg.nb, 1, GDN_DK), lambda i, h: (h, i, 0, 0))
    intra_ins = [In(t, hspec, F32) for t in (qn, kn, vv, bB, lB)]
    ashape = jax.ShapeDtypeStruct((hg, rp, CHUNK), BF16)
    rowshape_b = jax.ShapeDtypeStruct((rp, d), BF16)
    intra_shapes = [rowshape, rowshape_b, ashape, rowshape_b, rowshape, jax.ShapeDtypeStruct((hg, cfg.nch, 1, GDN_DK), F32),
                    ashape]
    intra_specs = [hspec, hspec, aspec, hspec, hspec, gspec, aspec]
    gu, gw, gattn, gqd, gkd, ggl, gtinv = _stage_fwd("gdn_intra_fwd", _gdn_intra_fn, gi_grid, intra_ins, intra_shapes,
                                                     intra_specs)
    rot_fn = _make_rot_fn(cfg)

    def rot_ins(dproj=None):
        return [In(proj, _rows(tr_n, 2 * d, ROT_COL // 2), BF16, galias=dproj, gshape=(rp, wm)),
                In(cos, _rows(tr_n, half)), In(sin, _rows(tr_n, half))]

    qr, kr = _stage_fwd("rot_fwd", rot_fn, (rp // tr_n,), rot_ins(), [jax.ShapeDtypeStruct((rp, d), BF16)] * 2, [nspec] * 2)
    nst = cfg.nch // cfg.sc
    oa, gss = _run_parts("gdn_scan_fwd", (nst,), [_gdn_scan_fwd(cfg, gu, gw, gattn, gqd, gkd, ggl)])[0]
    ob, rss = _run_parts("ret_scan_fwd", (nst,), [_ret_scan_fwd(cfg, qr, kr, proj, rconsts)])[0]

    mix_fn = _make_mix_fn(cfg)
    mix_ins = [In(oa, rspec, F32), In(ob, rspec, F32), In(proj, _rows(tr, 4 * d, MIX_COL // 4), BF16, gshape=(rp, wm)),
               In(gnorm, _full(gnorm), F32, True)]
    ymix = _stage_fwd("mix_fwd", mix_fn, (nrow,), mix_ins, [jax.ShapeDtypeStruct((rp, d), BF16)], [rspec])[0]
    h1 = _mm("out_proj_fwd", ymix, w_out, tm=tm, tn=d, tk=d, add=h0)

    hn2 = rms_fwd("rms2_fwd", h1, norm2)
    up = _mm("ffn_up_fwd", hn2, w_up, tm=tm, tn=cfg.tf, tk=d, out_dtype=BF16)
    uc = _conv_fwd("ffn_conv_fwd", up, 0, fconv_w, fconv_b, taps=FFN_CONV, width=2 * dff, tr=tr, tc=cfg.tf)
    tra = tr
    act_ins = [In(uc, _rows(tra, 2 * dff), BF16)]
    act_spec = _rows(tra, dff)
    act = _stage_fwd("ffn_act_fwd", _act_fn, (rp // tra,), act_ins, [jax.ShapeDtypeStruct((rp, dff), BF16)], [act_spec])[0]
    h2 = _mm("ffn_down_fwd", act, w_down, tm=tm, tn=d, tk=cfg.tf, add=h1)

    dh2, g_normf, loss_blk = _final(cfg, h2, normf, tgt)
    loss = lax.psum(loss_blk[0, 0], ("x", "y", "c"))

    g_w_down = _mm_tn("ffn_down_dw", act, dh2, tr=tm, tka=cfg.tf, tn=d)
    dact = _mm("ffn_down_dx", dh2, w_down.T, tm=tm, tn=cfg.tf, tk=d, out_dtype=BF16)
    duc, = _stage_bwd("ffn_act_bwd", _act_fn, (rp // tra,), act_ins, [(dact, act_spec)])
    dup, g_fconv_w, g_fconv_b = _conv_bwd("ffn_conv_bwd", up, 0, fconv_w, duc, taps=FFN_CONV, width=2 * dff,
                                          tr=tr, tc=cfg.tf, with_bias=True)
    g_w_up = _mm_tn("ffn_up_dw", hn2, dup, tr=tm, tka=d, tn=cfg.tf, blocked=True)

    def pair_reduce(tag, names, arrs):
        recvs = _pair_exchange("grad_pair_exchange_" + tag, arrs)
        return [_pair_sum("grad_pair_sum_" + n, g, r, cidx) for n, g, r in zip(names, arrs, recvs)]

    parts_ffn = pair_reduce("ffn", ["w_ffn_down", "w_ffn_up"], [g_w_down.reshape(N_CHIPS, dff // N_CHIPS, d), g_w_up])
    dhn2, slots_ffn = _mm("ffn_up_dx", dup, w_up_t, tm=tm, tn=d, tk=2 * cfg.tf, comm=_exchange_comm(parts_ffn))

    def rms_bwd(name, h, g, dhn, dres):
        ins = [In(h, nspec, F32), In(g, _full(g), F32, True)]
        return _stage_bwd(name, rms_b, (rp // tr_n,), ins, [(dhn, nspec), (dres, nspec)])

    dh1, g_norm2 = rms_bwd("rms2_bwd", h1, norm2, dhn2, dh2)
    g_w_out = _mm_tn("out_proj_dw", ymix, dh1, tr=tm, tka=d, tn=d)
    dymix = _mm("out_proj_dx", dh1, w_out.T, tm=tm, tn=d, tk=d)
    doa, dob, dproj, g_gnorm = _stage_bwd("mix_bwd", mix_fn, (nrow,), mix_ins, [(dymix, rspec)])

    dqr, dkr, dproj = _run_parts("ret_scan_bwd", (nst,), [_ret_scan_bwd(cfg, dob, qr, kr, proj, rconsts, rss, dproj)])[0]
    dproj, = _stage_bwd("rot_bwd", rot_fn, (rp // tr_n,), rot_ins(dproj), [(dqr, nspec), (dkr, nspec)])
    dgu, dgw, dgattn, dgqd, dgkd, dggl = _run_parts(
        "gdn_scan_bwd", (nst,), [_gdn_scan_bwd(cfg, doa, gu, gw, gattn, gqd, gkd, ggl, gss)])[0]

    intra_cots = [(dgu, hspec), (dgw, hspec), (dgattn, aspec), (dgqd, hspec), (dgkd, hspec), (dggl, gspec)]
    dqn, dkn, dvv, dbB, dlB = _stage_bwd("gdn_intra_bwd", _gdn_intra_fn, gi_grid, intra_ins + [In(gtinv, aspec)], intra_cots)
    dcqkv, dtail, g_alog, g_dtb = _stage_bwd(
        "gdn_prep_bwd", prep_fn, (nrow,), prep_ins, [(t, rspec) for t in (dqn, dkn, dvv, dbB, dlB)])
    dproj, g_gconv_w = _conv_bwd("gdn_conv_bwd", proj, CONV_COL * d, gconv_w, dcqkv, taps=GDN_CONV, width=3 * d,
                                 tr=tr, tc=d, with_bias=False, dx_into=dproj)
    g_w_in_r = jnp.concatenate([_mm_tn("proj_dw", hn1, dproj, tr=tm, tka=d, tn=tn_in),
                                _mm_tn("proj_tail_dw", hn1, dtail, tr=tm, tka=d, tn=LANES)], axis=1)
    g_in4 = jnp.stack(_split4(_restore_w_in(g_w_in_r, cfg), 1))
    parts_mix = pair_reduce("mix", ["w_out", "w_in"], [g_w_out.reshape(N_CHIPS, d // N_CHIPS, d), g_in4])
    dhn1_tail = _mm("proj_tail_dx", dtail, w_tail.T, tm=tm, tn=d, tk=LANES)
    dhn1, slots_mix = _mm("proj_dx", dproj, w_main.T, tm=tm, tn=d, tk=tn_in // 2 if tn_in > LANES else LANES, add=dhn1_tail,
                          comm=_exchange_comm(parts_mix))
    dh0, g_norm1 = rms_bwd("rms1_bwd", h0, norm1, dhn1, dh1)

    grad_x = dh0[cfg.xrow:]
    small_grads = {
        "meta": dh0[cfg.front:cfg.xrow], "norm1": g_norm1, "gdn_conv_w": g_gconv_w[None],
        "gdn_a_log": g_alog[0:1, :hg], "gdn_dt_bias": g_dtb[0:1, :hg], "gdn_norm": g_gnorm, "norm2": g_norm2,
        "ffn_conv_w": g_fconv_w[None], "ffn_conv_b": g_fconv_b, "norm_f": g_normf.reshape(d),
    }

    small_names = [n for n, _ in PARAMS if n not in BIG]
    g_small = jnp.stack([_pack([small_grads[n] if axis[n] is None else _split4(small_grads[n], axis[n])[s]
                                for n in small_names], F32) for s in range(N_CHIPS)])
    parts_small = pair_reduce("small", ["small"], [g_small])
    slots_small = _run_comm("grad_exchange_small", _exchange_comm(parts_small))
    tags = ["w_in", "w_out", "w_ffn_up", "w_ffn_down", "small"]
    parts = [parts_mix[1], parts_mix[0], parts_ffn[1], parts_ffn[0], parts_small[0]]
    slots = [slots_mix[1], slots_mix[0], slots_ffn[1], slots_ffn[0], slots_small[0]]
    fins = [_chip_sum("grad_chip_sum_" + t, p, s, chip) for t, p, s in zip(tags, parts, slots)]
    sibs = _pair_swap(fins)

    def flat2(a):
        return a.reshape(-1, a.shape[-1])

    outs = {}
    for k, t in enumerate(BIG):
        res = _adamw("adamw_" + t, flat2(shard[t]), fins[k], sibs[k], flat2(m_shard[t]), flat2(v_shard[t]), cidx)
        outs[t] = [r.reshape(shard[t].shape) for r in res]
    small_shapes_all = [shard[n].shape for n in small_names]
    pk = lambda src: _pack([src[n] for n in small_names], F32)
    res = _adamw("adamw_small", pk(shard), fins[4], sibs[4], pk(m_shard), pk(v_shard), cidx)
    for k, r in enumerate(res):
        for n, a in zip(small_names, _unpack(r, small_shapes_all)):
            outs.setdefault(n, [None] * 4)[k] = a
    names = [n for n, _ in PARAMS]
    return (loss, grad_x[None], *[outs[n][k] for k in range(4) for n in names])


def kernel(x, meta, norm1, w_in, gdn_conv_w, gdn_a_log, gdn_dt_bias, gdn_norm, w_out, norm2, w_ffn_up, ffn_conv_w, ffn_conv_b, w_ffn_down, norm_f, loss_target, m_meta, m_norm1, m_w_in, m_gdn_conv_w, m_gdn_a_log, m_gdn_dt_bias, m_gdn_norm, m_w_out, m_norm2, m_w_ffn_up, m_ffn_conv_w, m_ffn_conv_b, m_w_ffn_down, m_norm_f, v_meta, v_norm1, v_w_in, v_gdn_conv_w, v_gdn_a_log, v_gdn_dt_bias, v_gdn_norm, v_w_out, v_norm2, v_w_ffn_up, v_ffn_conv_w, v_ffn_conv_b, v_w_ffn_down, v_norm_f):
    names = [n for n, _ in PARAMS]
    shard = dict(zip(names, (meta, norm1, w_in, gdn_conv_w, gdn_a_log, gdn_dt_bias, gdn_norm, w_out, norm2, w_ffn_up,
                             ffn_conv_w, ffn_conv_b, w_ffn_down, norm_f)))
    m_shard = dict(zip(names, (m_meta, m_norm1, m_w_in, m_gdn_conv_w, m_gdn_a_log, m_gdn_dt_bias, m_gdn_norm, m_w_out,
                               m_norm2, m_w_ffn_up, m_ffn_conv_w, m_ffn_conv_b, m_w_ffn_down, m_norm_f)))
    v_shard = dict(zip(names, (v_meta, v_norm1, v_w_in, v_gdn_conv_w, v_gdn_a_log, v_gdn_dt_bias, v_gdn_norm, v_w_out,
                               v_norm2, v_w_ffn_up, v_ffn_conv_w, v_ffn_conv_b, v_w_ffn_down, v_norm_f)))
    return _step(REAL, x[0], loss_target[0], shard, m_shard, v_shard)
```

```python
import functools
from typing import NamedTuple

import numpy as np
import jax
import jax.numpy as jnp
from jax import lax
from jax.experimental import pallas as pl
from jax.experimental.pallas import tpu as pltpu

F32 = jnp.float32
BF16 = jnp.bfloat16
EPS = 1e-6
CHUNK = 64
GDN_DK = 128
RET_DK = 256
GDN_CONV = 4
FFN_CONV = 3
ROPE_BASE = 10000.0
LANES = 128
N_CHIPS = 4
ADAM_LR, ADAM_B1, ADAM_B2, ADAM_EPS, ADAM_WD, ADAM_STEP = 0.001, 0.9, 0.999, 1e-08, 0.01, 10
MIX_COL, CONV_COL, RV_BLOCK, ROT_COL = 0, 4, 7, 8
MESH = pl.DeviceIdType.MESH
VMEM_LIMIT = 56 * 1024 * 1024


class Cfg(NamedTuple):
    d: int
    seq: int
    n_meta: int
    dff: int
    tr: int
    nb: int
    tm: int
    tf: int
    sc: int

    @property
    def hg(self): return self.d // GDN_DK
    @property
    def hr(self): return self.d // RET_DK
    @property
    def L(self): return self.n_meta + self.seq
    @property
    def rp(self): return -(-self.L // 256) * 256
    @property
    def front(self): return self.rp - self.L
    @property
    def xrow(self): return self.rp - self.seq
    @property
    def nch(self): return self.rp // CHUNK
    @property
    def din(self): return 10 * self.d + 2 * self.hg


REAL = Cfg(d=1024, seq=8192, n_meta=16, dff=2816, tr=256, nb=12, tm=1408, tf=1408, sc=6)


def _pallas(body, **kw):
    return pl.pallas_call(body, **kw)


def _sigmoid_raw(x):
    return 1.0 / (1.0 + jnp.exp(-x))


@jax.custom_vjp
def _sigmoid(x):
    return _sigmoid_raw(x)


def _sigmoid_fwd(x):
    s = _sigmoid_raw(x)
    return s, s


def _sigmoid_bwd(s, g):
    return (g * (s * (1.0 - s)),)


_sigmoid.defvjp(_sigmoid_fwd, _sigmoid_bwd)


@jax.custom_vjp
def _silu(x):
    return x * _sigmoid_raw(x)


def _silu_fwd(x):
    s = _sigmoid_raw(x)
    return x * s, (x, s)


def _silu_bwd(res, g):
    x, s = res
    return (g * (s * (1.0 + x * (1.0 - s))),)


_silu.defvjp(_silu_fwd, _silu_bwd)


def _softplus(x):
    return jnp.maximum(x, 0.0) + jnp.log(1.0 + jnp.exp(-jnp.abs(x)))


def _raw_dot(a, b, ta, tb, hi):
    if not hi:
        a = a.astype(BF16)
        b = b.astype(BF16)
    nbatch = a.ndim - 2
    ca = a.ndim - 2 if ta else a.ndim - 1
    cb = b.ndim - 1 if tb else b.ndim - 2
    batch = tuple(range(nbatch))
    return lax.dot_general(a, b, (((ca,), (cb,)), (batch, batch)),
                           precision=lax.Precision.HIGHEST if hi else None,
                           preferred_element_type=F32)


@functools.partial(jax.custom_vjp, nondiff_argnums=(2, 3, 4))
def _dot_p(a, b, ta, tb, hi):
    return _raw_dot(a, b, ta, tb, hi)


def _dot(a, b, ta=False, tb=False, hi=False):
    return _dot_p(a, b, ta, tb, hi)


def _dot_fwd(a, b, ta, tb, hi):
    return _raw_dot(a, b, ta, tb, hi), (a, b)


def _dot_bwd(ta, tb, hi, res, g):
    a, b = res
    if not ta and not tb:
        da, db = _dot(g, b, False, True, hi), _dot(a, g, True, False, hi)
    elif not ta and tb:
        da, db = _dot(g, b, False, False, hi), _dot(g, a, True, False, hi)
    elif ta and not tb:
        da, db = _dot(b, g, False, True, hi), _dot(a, g, False, False, hi)
    else:
        raise NotImplementedError
    return da.astype(a.dtype), db.astype(b.dtype)


_dot_p.defvjp(_dot_fwd, _dot_bwd)


def _iota2(n, m, axis):
    return lax.broadcasted_iota(jnp.int32, (n, m), axis)


def _bcast(mat, nb):
    return jnp.broadcast_to(mat[None], (nb,) + mat.shape)


def _split3(a):
    a0 = a.astype(BF16)
    r1 = a - a0.astype(F32)
    a1 = r1.astype(BF16)
    return a0, a1, (r1 - a1.astype(F32)).astype(BF16)


@functools.partial(jax.custom_vjp, nondiff_argnums=(2,))
def _dot_sel(a, e, te):
    eb = e.astype(BF16)
    p0, p1, p2 = (_raw_dot(p, eb, False, te, False) for p in _split3(a))
    return p0 + (p1 + p2)


def _dot_sel_fwd(a, e, te):
    return _dot_sel(a, e, te), e


def _dot_sel_bwd(te, e, g):
    return _dot_sel(g, e, not te), jnp.zeros_like(e)


_dot_sel.defvjp(_dot_sel_fwd, _dot_sel_bwd)


@jax.custom_vjp
def _sel_dot(e, x):
    eb = e.astype(BF16)
    p0, p1, p2 = (_raw_dot(eb, p, False, False, False) for p in _split3(x))
    return p0 + (p1 + p2)


def _sel_dot_fwd(e, x):
    return _sel_dot(e, x), e


def _sel_dot_bwd(e, g):
    eb = e.astype(BF16)
    p0, p1, p2 = (_raw_dot(eb, p, True, False, False) for p in _split3(g))
    return jnp.zeros_like(e), p0 + (p1 + p2)


_sel_dot.defvjp(_sel_dot_fwd, _sel_dot_bwd)


def _tri_inv_raw(m):
    nb = m.shape[0]
    r, c = _iota2(CHUNK, CHUNK, 0), _iota2(CHUNK, CHUNK, 1)
    t = _bcast((r == c).astype(F32), nb)
    b = 1
    while b < CHUNK:
        sh = b.bit_length() - 1
        off = ((r >> (sh + 1)) == (c >> (sh + 1))) & ((r >> sh) != (c >> sh)) & (r > c)
        cl = jnp.where(off[None], m, 0.0)
        t = t - _raw_dot(_raw_dot(t, cl, False, False, False), t, False, False, False)
        b *= 2
    return t


@jax.custom_vjp
def _tri_inv_given(m, t):
    return t


def _tri_inv_fwd(m, t):
    return t, t


def _tri_inv_bwd(t, g):
    return -_raw_dot(_raw_dot(t, g, True, False, False), t, False, True, False), jnp.zeros_like(t)


_tri_inv_given.defvjp(_tri_inv_fwd, _tri_inv_bwd)


def _rms(h, g):
    return h * lax.rsqrt(jnp.mean(h * h, axis=-1, keepdims=True) + EPS) * g


class In(NamedTuple):
    arr: jax.Array
    spec: pl.BlockSpec
    grad: object = None
    acc: bool = False
    gshape: object = None
    gspec: object = None
    galias: object = None


def _params(grid):
    sem = ("arbitrary",) * len(grid)
    return pltpu.CompilerParams(dimension_semantics=sem, vmem_limit_bytes=VMEM_LIMIT)


def _stage_fwd(name, fn, grid, ins, out_shapes, out_specs):
    n_in = len(ins)

    def body(*refs):
        pids = tuple(pl.program_id(k) for k in range(len(grid)))
        vals = [r[...].astype(F32) for r in refs[:n_in]]
        outs = fn(pids, *vals)
        for o_ref, o in zip(refs[n_in:], outs):
            o_ref[...] = o.reshape(o_ref.shape).astype(o_ref.dtype)

    return _pallas(
        body, out_shape=out_shapes, grid=grid, in_specs=[i.spec for i in ins],
        out_specs=out_specs, name=name, compiler_params=_params(grid))(*[i.arr for i in ins])


def _stage_bwd(name, fn, grid, ins, cots):
    n_in, n_ct = len(ins), len(cots)
    didx = [k for k, i in enumerate(ins) if i.grad is not None]
    aliased = [(o, ins[k].galias) for o, k in enumerate(didx) if ins[k].galias is not None]
    n_al = len(aliased)

    def body(*refs):
        pids = tuple(pl.program_id(k) for k in range(len(grid)))
        vals = [r[...].astype(F32) for r in refs[:n_in]]
        ct_refs = refs[n_in:n_in + n_ct]
        g_refs = refs[n_in + n_ct + n_al:]

        def f(*dv):
            merged = list(vals)
            for k, v in zip(didx, dv):
                merged[k] = v
            return tuple(fn(pids, *merged))

        outs, vjp_fn = jax.vjp(f, *[vals[k].astype(F32) for k in didx])
        cts = tuple(c[...].reshape(o.shape).astype(F32) for c, o in zip(ct_refs, outs))
        grads = vjp_fn(cts)
        first = functools.reduce(jnp.logical_and, [p == 0 for p in pids])
        for k, g_ref, g in zip(didx, g_refs, grads):
            if ins[k].acc:
                @pl.when(first)
                def _(g_ref=g_ref):
                    g_ref[...] = jnp.zeros(g_ref.shape, g_ref.dtype)
                g_ref[...] += g.reshape(g_ref.shape).astype(g_ref.dtype)
            else:
                g_ref[...] = g.reshape(g_ref.shape).astype(g_ref.dtype)

    out_shapes = [jax.ShapeDtypeStruct(ins[k].gshape or ins[k].arr.shape, ins[k].grad) for k in didx]
    out_specs = [ins[k].gspec or ins[k].spec for k in didx]
    return _pallas(
        body, out_shape=out_shapes, grid=grid,
        in_specs=[i.spec for i in ins] + [c[1] for c in cots] + [ANY] * n_al, out_specs=out_specs,
        input_output_aliases={n_in + n_ct + a: o for a, (o, _) in enumerate(aliased)},
        name=name, compiler_params=_params(grid))(*[i.arr for i in ins], *[c[0] for c in cots], *[a for _, a in aliased])


def _full(arr):
    nd = arr.ndim
    return pl.BlockSpec(arr.shape, lambda *p: (0,) * nd)


def _rows(tr, width, blk=0):
    return pl.BlockSpec((tr, width), lambda i: (i, blk))


def _mm(name, a, b, *, tm, tn, tk, out_dtype=F32, add=None, comm=None):
    M, K = a.shape
    N = b.shape[1] if b.ndim == 2 else b.shape[0] * b.shape[2]
    nk = K // tk
    grid = (M // tm, N // tn, nk)
    n_in = 3 if add is not None else 2
    n_ci, n_co = (len(comm.ins), len(comm.outs)) if comm is not None else (0, 0)

    def body(*refs):
        a_ref, b_ref = refs[0], refs[1]
        add_ref = refs[2] if add is not None else None
        c_ins = refs[n_in:n_in + n_ci]
        o_ref = refs[n_in + n_ci]
        c_outs = refs[n_in + n_ci + 1:n_in + n_ci + 1 + n_co]
        scratch = refs[n_in + n_ci + 1 + n_co:]
        acc_ref = scratch[0] if nk > 1 else None
        sems = scratch[1 if nk > 1 else 0:]
        step = (pl.program_id(0) * grid[1] + pl.program_id(1)) * nk + pl.program_id(2)
        if comm is not None:
            @pl.when(step == 0)
            def _():
                comm.start(c_ins, c_outs, sems)

        part = _raw_dot(a_ref[...], b_ref[...], False, False, False)

        def finish(total):
            if add_ref is not None:
                total = total + add_ref[...]
            o_ref[...] = total.astype(o_ref.dtype)

        if nk == 1:
            finish(part)
        else:
            k = pl.program_id(2)

            @pl.when(k == 0)
            def _():
                acc_ref[...] = part

            @pl.when(k > 0)
            def _():
                acc_ref[...] += part

            @pl.when(k == nk - 1)
            def _():
                finish(acc_ref[...])

        if comm is not None:
            @pl.when(step == grid[0] * grid[1] * nk - 1)
            def _():
                comm.finish(c_ins, c_outs, sems)

    b_spec = (pl.BlockSpec((tk, tn), lambda i, j, k: (k, j)) if b.ndim == 2 else
              pl.BlockSpec((None, tk, tn), lambda i, j, k: (j, k, 0)))
    in_specs = [pl.BlockSpec((tm, tk), lambda i, j, k: (i, k)), b_spec]
    args = [a, b]
    if add is not None:
        in_specs.append(pl.BlockSpec((tm, tn), lambda i, j, k: (i, j)))
        args.append(add)
    out_shape = jax.ShapeDtypeStruct((M, N), out_dtype)
    out_spec = pl.BlockSpec((tm, tn), lambda i, j, k: (i, j))
    scratch = [pltpu.VMEM((tm, tn), F32)] if nk > 1 else []
    if comm is None:
        return _pallas(body, out_shape=out_shape, grid=grid, in_specs=in_specs, out_specs=out_spec,
                       scratch_shapes=scratch, name=name, compiler_params=_params(grid))(*args)
    res = _pallas(body, out_shape=[out_shape] + comm.outs, grid=grid, in_specs=in_specs + [ANY] * n_ci,
                  out_specs=[out_spec] + [ANY] * n_co, scratch_shapes=scratch + comm.sems, name=name,
                  compiler_params=_params(grid))(*args, *comm.ins)
    return res[0], res[1:]


def _mm_tn(name, a, b, *, tr, tka, tn, blocked=False):
    R, Ka = a.shape
    N = b.shape[1]
    nr = R // tr
    grid = (Ka // tka, N // tn, nr)
    if blocked:
        out_shape = jax.ShapeDtypeStruct((N // tn, Ka, tn), F32)
        out_spec = pl.BlockSpec((None, tka, tn), lambda i, j, r: (j, i, 0))
    else:
        out_shape = jax.ShapeDtypeStruct((Ka, N), F32)
        out_spec = pl.BlockSpec((tka, tn), lambda i, j, r: (i, j))

    def body(a_ref, b_ref, o_ref):
        r = pl.program_id(2)
        part = _raw_dot(a_ref[...], b_ref[...], True, False, False)

        @pl.when(r == 0)
        def _():
            o_ref[...] = part

        @pl.when(r > 0)
        def _():
            o_ref[...] += part

    return _pallas(
        body, out_shape=out_shape, grid=grid,
        in_specs=[pl.BlockSpec((tr, tka), lambda i, j, r: (r, i)),
                  pl.BlockSpec((tr, tn), lambda i, j, r: (r, j))],
        out_specs=out_spec, name=name, compiler_params=_params(grid))(a, b)


def _conv_fwd(name, x, xcol0, w, b, *, taps, width, tr, tc):
    R = x.shape[0]
    grid = (width // tc, R // tr)
    cb0 = xcol0 // tc
    hrows = 16 if x.dtype == BF16 else 8
    hb = tr // hrows

    def body(*refs):
        x_ref, xp_ref, w_ref = refs[:3]
        b_ref = refs[3] if b is not None else None
        o_ref = refs[-1]
        i = pl.program_id(1)
        xv = x_ref[...].astype(F32)
        prev = jnp.where(i > 0, xp_ref[...].astype(F32)[hrows - 8:, :], 0.0)
        ext = jnp.concatenate([prev, xv], axis=0)
        acc = xv * w_ref[taps - 1:taps, :]
        for s in range(1, taps):
            acc = acc + pltpu.roll(ext, s, 0)[8:, :] * w_ref[taps - 1 - s:taps - s, :]
        if b_ref is not None:
            acc = acc + b_ref[...]
        o_ref[...] = acc.astype(o_ref.dtype)

    in_specs = [pl.BlockSpec((tr, tc), lambda j, i: (i, cb0 + j)),
                pl.BlockSpec((hrows, tc), lambda j, i: (jnp.maximum(i * hb - 1, 0), cb0 + j)),
                pl.BlockSpec((taps, tc), lambda j, i: (0, j))]
    args = [x, x, w]
    if b is not None:
        in_specs.append(pl.BlockSpec((1, tc), lambda j, i: (0, j)))
        args.append(b)
    return _pallas(
        body, out_shape=jax.ShapeDtypeStruct((R, width), BF16), grid=grid, in_specs=in_specs,
        out_specs=pl.BlockSpec((tr, tc), lambda j, i: (i, j)),
        name=name, compiler_params=_params(grid))(*args)


def _conv_bwd(name, x, xcol0, w, dy, *, taps, width, tr, tc, with_bias, dx_into=None):
    R = x.shape[0]
    nr = R // tr
    grid = (width // tc, nr)
    cb0 = xcol0 // tc
    hrows = 16 if dy.dtype == BF16 else 8
    hb = tr // hrows
    n_ext = tr + 8
    n_al = 0 if dx_into is None else 1

    def body(*refs):
        x_ref, w_ref, dy_ref, dyn_ref = refs[:4]
        dx_ref, dw_ref = refs[4 + n_al], refs[5 + n_al]
        db_ref = refs[6 + n_al] if with_bias else None
        i = pl.program_id(1)
        xv = x_ref[...].astype(F32)
        dyv = dy_ref[...].astype(F32)
        nxt = dyn_ref[...].astype(F32)[:8, :]
        dext = jnp.concatenate([dyv, jnp.where(i < nr - 1, nxt, 0.0)], axis=0)
        dx = dyv * w_ref[taps - 1:taps, :]
        dws = [None] * taps
        dws[taps - 1] = jnp.sum(xv * dyv, axis=0, keepdims=True)
        for s in range(1, taps):
            ahead = pltpu.roll(dext, n_ext - s, 0)[:tr, :]
            dx = dx + ahead * w_ref[taps - 1 - s:taps - s, :]
            dws[taps - 1 - s] = jnp.sum(xv * ahead, axis=0, keepdims=True)
        dx_ref[...] = dx.astype(dx_ref.dtype)

        @pl.when(i == 0)
        def _():
            for k in range(taps):
                dw_ref[k:k + 1, :] = dws[k]
            if db_ref is not None:
                db_ref[...] = jnp.sum(dyv, axis=0, keepdims=True)

        @pl.when(i > 0)
        def _():
            for k in range(taps):
                dw_ref[k:k + 1, :] += dws[k]
            if db_ref is not None:
                db_ref[...] += jnp.sum(dyv, axis=0, keepdims=True)

    in_specs = [pl.BlockSpec((tr, tc), lambda j, i: (i, cb0 + j)),
                pl.BlockSpec((taps, tc), lambda j, i: (0, j)),
                pl.BlockSpec((tr, tc), lambda j, i: (i, j)),
                pl.BlockSpec((hrows, tc), lambda j, i: (jnp.minimum((i + 1) * hb, R // hrows - 1), j))]
    args = [x, w, dy, dy]
    if dx_into is None:
        dx_shape, dx_spec, aliases = jax.ShapeDtypeStruct((R, width), BF16), pl.BlockSpec((tr, tc), lambda j, i: (i, j)), {}
    else:
        dx_shape = jax.ShapeDtypeStruct(dx_into.shape, dx_into.dtype)
        dx_spec, aliases = pl.BlockSpec((tr, tc), lambda j, i: (i, cb0 + j)), {4: 0}
        in_specs.append(ANY)
        args.append(dx_into)
    out_shape = [dx_shape, jax.ShapeDtypeStruct((taps, width), F32)]
    out_specs = [dx_spec, pl.BlockSpec((taps, tc), lambda j, i: (0, j))]
    if with_bias:
        out_shape.append(jax.ShapeDtypeStruct((1, width), F32))
        out_specs.append(pl.BlockSpec((1, tc), lambda j, i: (0, j)))
    return _pallas(
        body, out_shape=out_shape, grid=grid, in_specs=in_specs, out_specs=out_specs, input_output_aliases=aliases,
        name=name, compiler_params=_params(grid))(*args)


def _row_mask(cfg, i, tr):
    rows = i * tr + lax.broadcasted_iota(jnp.int32, (tr, 1), 0)
    return (rows >= cfg.front).astype(F32)


def _make_rms_fn(cfg, tr, with_residual):
    def fn(pids, h, g):
        hm = h * _row_mask(cfg, pids[0], tr)
        if with_residual:
            return _rms(hm, g), hm
        return (_rms(hm, g),)
    return fn


def _make_gdn_prep_fn(cfg, tr):
    d, hg = cfg.d, cfg.hg

    def fn(pids, c, tail, alog, dtb):
        cq, ck, cv = c[:, :d], c[:, d:2 * d], c[:, 2 * d:]
        mask = _row_mask(cfg, pids[0], tr)
        j, col = _iota2(LANES, d, 0), _iota2(LANES, d, 1)
        ea = ((col >> 7) == j).astype(F32)
        eb = ((col >> 7) + hg == j).astype(F32)
        al = jnp.sum(alog, axis=0, keepdims=True)
        db = jnp.sum(dtb, axis=0, keepdims=True)
        lg = _dot_sel(-jnp.exp(al) * _softplus(tail + db) * mask, ea, False)
        beta = _dot_sel(_sigmoid(tail) * mask, eb, False)
        sq, sk, sv = _silu(cq), _silu(ck), _silu(cv)
        qs, ks = [], []
        for h in range(hg):
            sl = slice(h * GDN_DK, (h + 1) * GDN_DK)
            qh, kh = sq[:, sl], sk[:, sl]
            qs.append(qh * lax.rsqrt(jnp.sum(qh * qh, axis=-1, keepdims=True) + EPS) * (GDN_DK ** -0.5))
            ks.append(kh * lax.rsqrt(jnp.sum(kh * kh, axis=-1, keepdims=True) + EPS))
        return jnp.concatenate(qs, axis=1), jnp.concatenate(ks, axis=1), sv, beta, lg
    return fn


def _gdn_intra_fn(pids, q, k, v, bB, lB, t_saved=None):
    rows = q.shape[0]
    nb = rows // CHUNK
    q3, k3, v3, b3, l3 = [t.reshape(nb, CHUNK, GDN_DK) for t in (q, k, v, bB, lB)]
    r, c = _iota2(CHUNK, CHUNK, 0), _iota2(CHUNK, CHUNK, 1)
    tril = (r >= c)
    strict = (r > c)
    gcol = _sel_dot(_bcast(tril.astype(F32), nb), l3)
    grow = jnp.swapaxes(gcol, 1, 2)[:, :CHUNK, :]
    diff = gcol[:, :, :CHUNK] - grow
    decay = jnp.where(tril[None], jnp.exp(jnp.where(tril[None], diff, 0.0)), 0.0)
    kb = k3 * b3
    m = jnp.where(strict[None], _dot(kb, k3, False, True) * decay, 0.0)
    t = _tri_inv_raw(m) if t_saved is None else _tri_inv_given(m, t_saved.reshape(nb, CHUNK, CHUNK))
    eg = jnp.exp(gcol)
    u = _dot(t, v3 * b3)
    w = _dot(t, kb * eg)
    attn = _dot(q3, k3, False, True) * decay
    qd = q3 * eg
    glast = jnp.sum(l3, axis=1, keepdims=True)
    kd = k3 * jnp.exp(glast - gcol)
    gl = jnp.exp(glast)
    outs = (u.reshape(rows, GDN_DK), w.reshape(rows, GDN_DK), attn.reshape(1, rows, CHUNK),
            qd.reshape(rows, GDN_DK), kd.reshape(rows, GDN_DK), gl.reshape(1, nb, 1, GDN_DK))
    return outs + (t.reshape(1, rows, CHUNK),) if t_saved is None else outs


def _make_rot_fn(cfg):
    hr = cfg.hr
    half = RET_DK // 2

    def fn(pids, rqk, cos, sin):
        rq, rk = rqk[:, :cfg.d], rqk[:, cfg.d:]

        def rot(t, scale):
            outs = []
            for h in range(hr):
                x1 = t[:, h * RET_DK:h * RET_DK + half]
                x2 = t[:, h * RET_DK + half:(h + 1) * RET_DK]
                outs += [(x1 * cos - x2 * sin) * scale, (x2 * cos + x1 * sin) * scale]
            return jnp.concatenate(outs, axis=1)
        return rot(rq, 1.0), rot(rk, RET_DK ** -0.5)
    return fn


def _make_mix_fn(cfg):
    hg, hr = cfg.hg, cfg.hr

    def fn(pids, oa, ob, pm, gnorm):
        d = cfg.d
        gz, rg, gate_a, gate_b = pm[:, :d], pm[:, d:2 * d], pm[:, 2 * d:3 * d], pm[:, 3 * d:]
        oas = []
        for h in range(hg):
            oh = oa[:, h * GDN_DK:(h + 1) * GDN_DK]
            oas.append(oh * lax.rsqrt(jnp.mean(oh * oh, axis=-1, keepdims=True) + EPS) * gnorm)
        ya = jnp.concatenate(oas, axis=1) * _silu(gz)
        obs = []
        for h in range(hr):
            oh = ob[:, h * RET_DK:(h + 1) * RET_DK]
            obs.append(oh * lax.rsqrt(jnp.mean(oh * oh, axis=-1, keepdims=True) + EPS))
        yb = _silu(rg) * jnp.concatenate(obs, axis=1)
        return (_sigmoid(gate_a) * ya + _sigmoid(gate_b) * yb,)
    return fn


def _act_fn(pids, u):
    f = u.shape[1] // 2
    return (_silu(u[:, :f]) * u[:, f:],)


def _gdn_step(s, u, w, a, qd, kd, gl):
    top = _dot(jnp.concatenate([w, qd], axis=0), s)
    v_new = u - top[:CHUNK]
    bot = _dot(jnp.concatenate([a, kd.T], axis=0), v_new)
    o = top[CHUNK:] + bot[:CHUNK]
    s2 = s * gl + bot[CHUNK:]
    return s2, o


def _gdn_step_bwd(s, u, w, a, qd, kd, gl, ds2, do):
    lw = jnp.concatenate([w, qd], axis=0)
    v_new = u - _raw_dot(w, s, False, False, False)
    dv = _raw_dot(a, do, True, False, False) + _raw_dot(kd, ds2, False, False, False)
    da = _raw_dot(do, v_new, False, True, False)
    dkd = _raw_dot(v_new, ds2, False, True, False)
    dtop = jnp.concatenate([-dv, do], axis=0)
    dlw = _raw_dot(dtop, s, False, True, False)
    ds = ds2 * gl + _raw_dot(lw, dtop, True, False, False)
    dgl = jnp.sum(ds2 * s, axis=0, keepdims=True)
    return ds, dv, dlw[:CHUNK], da, dlw[CHUNK:], dkd, dgl


def _ret_step(s, q, k, v, dm, qdc, kdc, g):
    att = _dot(q, k, False, True) * dm
    bot = _dot(jnp.concatenate([att, (k * kdc).T], axis=0), v)
    o = bot[:CHUNK] + _dot(q * qdc, s)
    s2 = s * g + bot[CHUNK:]
    return s2, o


class Part(NamedTuple):
    body: object
    args: list
    in_specs: list
    out_shape: list
    out_specs: list
    scratch: list
    aliases: dict = {}


def _run_parts(name, grid, parts):
    n_in = [len(p.args) for p in parts]
    n_out = [len(p.out_shape) for p in parts]
    n_sc = [len(p.scratch) for p in parts]
    off_in = [sum(n_in[:k]) for k in range(len(parts))]
    off_out = [sum(n_out[:k]) for k in range(len(parts))]
    off_sc = [sum(n_sc[:k]) for k in range(len(parts))]

    def body(*refs):
        ins, outs, scr = refs[:sum(n_in)], refs[sum(n_in):sum(n_in) + sum(n_out)], refs[sum(n_in) + sum(n_out):]
        for k, p in enumerate(parts):
            p.body(*ins[off_in[k]:off_in[k] + n_in[k]], *outs[off_out[k]:off_out[k] + n_out[k]],
                   *scr[off_sc[k]:off_sc[k] + n_sc[k]])

    aliases = {off_in[k] + i: off_out[k] + o for k, p in enumerate(parts) for i, o in p.aliases.items()}
    res = _pallas(
        body, out_shape=sum((p.out_shape for p in parts), []), grid=grid, in_specs=sum((p.in_specs for p in parts), []),
        out_specs=sum((p.out_specs for p in parts), []), scratch_shapes=sum((p.scratch for p in parts), []),
        input_output_aliases=aliases, name=name, compiler_params=_params(grid))(*sum((p.args for p in parts), []))
    return [res[off_out[k]:off_out[k] + n_out[k]] for k in range(len(parts))]


def _gdn_scan_fwd(cfg, u, w, attn, qd, kd, gl):
    d, hg, nch, sc = cfg.d, cfg.hg, cfg.nch, cfg.sc
    nst = nch // sc

    def body(u_ref, w_ref, a_ref, qd_ref, kd_ref, gl_ref, o_ref, ss_ref, s_ref):
        @pl.when(pl.program_id(0) == 0)
        def _():
            s_ref[...] = jnp.zeros(s_ref.shape, F32)

        states = [s_ref[h] for h in range(hg)]
        for j in range(sc):
            rows = slice(j * CHUNK, (j + 1) * CHUNK)
            outs = []
            for h in range(hg):
                sl = slice(h * GDN_DK, (h + 1) * GDN_DK)
                ss_ref[j, h] = states[h]
                states[h], o = _gdn_step(states[h], u_ref[rows, sl], w_ref[rows, sl], a_ref[h, rows, :],
                                         qd_ref[rows, sl], kd_ref[rows, sl], gl_ref[h, j])
                outs.append(o)
            o_ref[rows, :] = jnp.concatenate(outs, axis=1)
        for h in range(hg):
            s_ref[h] = states[h]

    row = pl.BlockSpec((sc * CHUNK, d), lambda n: (n, 0))
    return Part(
        body, [u, w, attn, qd, kd, gl],
        [row, row, pl.BlockSpec((hg, sc * CHUNK, CHUNK), lambda n: (0, n, 0)), row, row,
         pl.BlockSpec((hg, sc, 1, GDN_DK), lambda n: (0, n, 0, 0))],
        [jax.ShapeDtypeStruct((cfg.rp, d), F32), jax.ShapeDtypeStruct((nch, hg, GDN_DK, GDN_DK), F32)],
        [row, pl.BlockSpec((sc, hg, GDN_DK, GDN_DK), lambda n: (n, 0, 0, 0))],
        [pltpu.VMEM((hg, GDN_DK, GDN_DK), F32)])


def _gdn_scan_bwd(cfg, do, u, w, attn, qd, kd, gl, ss):
    d, hg, nch, sc = cfg.d, cfg.hg, cfg.nch, cfg.sc
    nst = nch // sc

    def body(do_ref, u_ref, w_ref, a_ref, qd_ref, kd_ref, gl_ref, ss_ref,
             du_ref, dw_ref, da_ref, dqd_ref, dkd_ref, dgl_ref, ds_ref):
        @pl.when(pl.program_id(0) == 0)
        def _():
            ds_ref[...] = jnp.zeros(ds_ref.shape, F32)

        dstates = [ds_ref[h] for h in range(hg)]
        for j in reversed(range(sc)):
            rows = slice(j * CHUNK, (j + 1) * CHUNK)
            dus, dws, dqds, dkds = [], [], [], []
            for h in range(hg):
                sl = slice(h * GDN_DK, (h + 1) * GDN_DK)
                args = (ss_ref[j, h], u_ref[rows, sl], w_ref[rows, sl], a_ref[h, rows, :], qd_ref[rows, sl],
                        kd_ref[rows, sl], gl_ref[h, j])
                dstates[h], du, dw, da, dqd, dkd, dgl = _gdn_step_bwd(*args, dstates[h], do_ref[rows, sl])
                da_ref[h, rows, :] = da
                dgl_ref[h, j] = dgl
                dus.append(du)
                dws.append(dw)
                dqds.append(dqd)
                dkds.append(dkd)
            du_ref[rows, :] = jnp.concatenate(dus, axis=1)
            dw_ref[rows, :] = jnp.concatenate(dws, axis=1)
            dqd_ref[rows, :] = jnp.concatenate(dqds, axis=1)
            dkd_ref[rows, :] = jnp.concatenate(dkds, axis=1)
        for h in range(hg):
            ds_ref[h] = dstates[h]

    row = pl.BlockSpec((sc * CHUNK, d), lambda n: (nst - 1 - n, 0))
    aspec = pl.BlockSpec((hg, sc * CHUNK, CHUNK), lambda n: (0, nst - 1 - n, 0))
    gspec = pl.BlockSpec((hg, sc, 1, GDN_DK), lambda n: (0, nst - 1 - n, 0, 0))
    rowshape = jax.ShapeDtypeStruct((cfg.rp, d), F32)
    return Part(
        body, [do, u, w, attn, qd, kd, gl, ss],
        [row, row, row, aspec, row, row, gspec, pl.BlockSpec((sc, hg, GDN_DK, GDN_DK), lambda n: (nst - 1 - n, 0, 0, 0))],
        [rowshape, rowshape, jax.ShapeDtypeStruct(attn.shape, F32), rowshape, rowshape, jax.ShapeDtypeStruct(gl.shape, F32)],
        [row, row, aspec, row, row, gspec],
        [pltpu.VMEM((hg, GDN_DK, GDN_DK), F32)])


def _ret_consts(cfg):
    hr = cfg.hr
    lg = np.log(1.0 - 2.0 ** (-5.0 - np.arange(hr, dtype=np.float64)))
    idx = np.arange(CHUNK, dtype=np.float64)
    tril = np.tril(np.ones((CHUNK, CHUNK), dtype=bool))
    dm = np.where(tril[None], np.exp((idx[:, None] - idx[None, :])[None] * lg[:, None, None]), 0.0)
    qdc = np.exp((idx[None, :] + 1.0) * lg[:, None])
    kdc = np.exp((CHUNK - 1.0 - idx[None, :]) * lg[:, None])
    gch = np.exp(CHUNK * lg)
    qdc = np.broadcast_to(qdc[:, :, None], (hr, CHUNK, RET_DK))
    kdc = np.broadcast_to(kdc[:, :, None], (hr, CHUNK, RET_DK))
    gch = np.broadcast_to(gch[:, None, None], (hr, 1, RET_DK))
    return tuple(jnp.asarray(np.ascontiguousarray(t), F32) for t in (dm, qdc, kdc, gch))


def _ret_scan_fwd(cfg, qr, kr, proj, consts):
    d, hr, nch, sc = cfg.d, cfg.hr, cfg.nch, cfg.sc
    nst = nch // sc
    dm, qdc, kdc, gch = consts

    def body(q_ref, k_ref, v_ref, dm_ref, qdc_ref, kdc_ref, g_ref, o_ref, ss_ref, s_ref):
        @pl.when(pl.program_id(0) == 0)
        def _():
            s_ref[...] = jnp.zeros(s_ref.shape, F32)

        states = [s_ref[h] for h in range(hr)]
        for j in range(sc):
            rows = slice(j * CHUNK, (j + 1) * CHUNK)
            outs = []
            for h in range(hr):
                sl = slice(h * RET_DK, (h + 1) * RET_DK)
                ss_ref[j, h] = states[h]
                states[h], o = _ret_step(states[h], q_ref[rows, sl], k_ref[rows, sl], v_ref[rows, sl], dm_ref[h],
                                         qdc_ref[h], kdc_ref[h], g_ref[h])
                outs.append(o)
            o_ref[rows, :] = jnp.concatenate(outs, axis=1)
        for h in range(hr):
            s_ref[h] = states[h]

    row = pl.BlockSpec((sc * CHUNK, d), lambda n: (n, 0))
    return Part(
        body, [qr, kr, proj, dm, qdc, kdc, gch],
        [row, row, pl.BlockSpec((sc * CHUNK, d), lambda n: (n, RV_BLOCK)), _full(dm), _full(qdc), _full(kdc), _full(gch)],
        [jax.ShapeDtypeStruct((cfg.rp, d), F32), jax.ShapeDtypeStruct((nch, hr, RET_DK, RET_DK), F32)],
        [row, pl.BlockSpec((sc, hr, RET_DK, RET_DK), lambda n: (n, 0, 0, 0))],
        [pltpu.VMEM((hr, RET_DK, RET_DK), F32)])


def _ret_scan_bwd(cfg, do, qr, kr, proj, consts, ss, dproj):
    d, hr, nch, sc = cfg.d, cfg.hr, cfg.nch, cfg.sc
    nst = nch // sc
    dm, qdc, kdc, gch = consts

    def body(do_ref, q_ref, k_ref, v_ref, dm_ref, qdc_ref, kdc_ref, g_ref, ss_ref, _, dq_ref, dk_ref, dv_ref, ds_ref):
        @pl.when(pl.program_id(0) == 0)
        def _():
            ds_ref[...] = jnp.zeros(ds_ref.shape, F32)

        dstates = [ds_ref[h] for h in range(hr)]
        for j in reversed(range(sc)):
            rows = slice(j * CHUNK, (j + 1) * CHUNK)
            dqs, dks, dvs = [], [], []
            for h in range(hr):
                sl = slice(h * RET_DK, (h + 1) * RET_DK)
                cs = (dm_ref[h], qdc_ref[h], kdc_ref[h], g_ref[h])
                _, vjp_fn = jax.vjp(lambda s, q, k, v, cs=cs: _ret_step(s, q, k, v, *cs),
                                    ss_ref[j, h], q_ref[rows, sl].astype(F32), k_ref[rows, sl].astype(F32), v_ref[rows, sl])
                dstates[h], dq, dk, dv = vjp_fn((dstates[h], do_ref[rows, sl]))
                dqs.append(dq)
                dks.append(dk)
                dvs.append(dv)
            dq_ref[rows, :] = jnp.concatenate(dqs, axis=1).astype(dq_ref.dtype)
            dk_ref[rows, :] = jnp.concatenate(dks, axis=1).astype(dk_ref.dtype)
            dv_ref[rows, :] = jnp.concatenate(dvs, axis=1).astype(dv_ref.dtype)
        for h in range(hr):
            ds_ref[h] = dstates[h]

    row = pl.BlockSpec((sc * CHUNK, d), lambda n: (nst - 1 - n, 0))
    rowshape = jax.ShapeDtypeStruct((cfg.rp, d), BF16)
    vspec = pl.BlockSpec((sc * CHUNK, d), lambda n: (nst - 1 - n, RV_BLOCK))
    return Part(
        body, [do, qr, kr, proj, dm, qdc, kdc, gch, ss, dproj],
        [row, row, row, vspec, _full(dm), _full(qdc), _full(kdc), _full(gch),
         pl.BlockSpec((sc, hr, RET_DK, RET_DK), lambda n: (nst - 1 - n, 0, 0, 0)), ANY],
        [rowshape, rowshape, jax.ShapeDtypeStruct(dproj.shape, dproj.dtype)],
        [row, row, vspec],
        [pltpu.VMEM((hr, RET_DK, RET_DK), F32)], {9: 2})


def _final(cfg, h2, normf, tgt):
    d, tr = cfg.d, cfg.xrow
    nr = cfg.rp // tr

    def body(h_ref, g_ref, t_ref, dh_ref, dg_ref, loss_ref):
        i = pl.program_id(0)
        y, vjp_fn = jax.vjp(_rms, h_ref[...], g_ref[...])
        err = jnp.where(i >= 1, y - t_ref[...], 0.0)
        dh, dg = vjp_fn(err * (1.0 / d))
        dh_ref[...] = dh
        part = jnp.zeros((8, LANES), F32) + 0.5 * jnp.sum(err * err) * (1.0 / d)

        @pl.when(i == 0)
        def _():
            dg_ref[...] = dg
            loss_ref[...] = part

        @pl.when(i > 0)
        def _():
            dg_ref[...] += dg
            loss_ref[...] += part

    return _pallas(
        body,
        out_shape=[jax.ShapeDtypeStruct((cfg.rp, d), F32), jax.ShapeDtypeStruct((1, d), F32),
                   jax.ShapeDtypeStruct((8, LANES), F32)],
        grid=(nr,),
        in_specs=[_rows(tr, d), _full(normf), pl.BlockSpec((tr, d), lambda i: (jnp.maximum(i - 1, 0), 0))],
        out_specs=[_rows(tr, d), pl.BlockSpec((1, d), lambda i: (0, 0)), pl.BlockSpec((8, LANES), lambda i: (0, 0))],
        name="final_loss", compiler_params=_params((nr,)))(h2, normf, tgt)


ANY = pl.BlockSpec(memory_space=pl.ANY)


def _place():
    x, y, c = lax.axis_index("x"), lax.axis_index("y"), lax.axis_index("c")
    others = [(1 - x, y), (x, 1 - y), (1 - x, 1 - y)]
    return x, y, c, others


def _row_tile(rows, cap=256):
    return max(t for t in range(16, min(rows, cap) + 1, 16) if rows % t == 0)


class Comm(NamedTuple):
    ins: list
    outs: list
    sems: list
    start: object
    finish: object


def _run_comm(name, comm):
    n_in, n_out = len(comm.ins), len(comm.outs)

    def body(*refs):
        ins, outs, sems = refs[:n_in], refs[n_in:n_in + n_out], refs[n_in + n_out:]
        comm.start(ins, outs, sems)
        comm.finish(ins, outs, sems)

    return _pallas(body, out_shape=comm.outs, in_specs=[ANY] * n_in, out_specs=[ANY] * n_out,
                   scratch_shapes=comm.sems, name=name)(*comm.ins)


def _gather_comm(ws):
    n = len(ws)
    halves = [w.shape[0] // 2 for w in ws]

    def copies(w_refs, o_refs, sems):
        send_sems, recv_sems = sems
        x, y, c, others = _place()
        me = 2 * x + y
        chips = [2 * px + py for px, py in others]

        def piece(a, chip, core):
            return o_refs[a].at[chip, pl.ds(core * halves[a], halves[a]), :]

        def copy(a, k, src, chip, core, to):
            return pltpu.make_async_remote_copy(src_ref=src, dst_ref=piece(a, chip, core), send_sem=send_sems.at[6 * a + k],
                                                recv_sem=recv_sems.at[6 * a + k], device_id=to, device_id_type=MESH)

        def first(j, a):
            return copy(a, j, w_refs[a].at[pl.ds(c * halves[a], halves[a]), :], me, c, (*others[j], c))

        def landed(j, a):
            return copy(a, j, piece(a, chips[j], c), chips[j], c, (x, y, c))

        def passed(j, a):
            return copy(a, 3 + j, piece(a, chips[j], c), chips[j], c, (x, y, 1 - c))

        def from_sibling(j, a):
            return copy(a, 3 + j, piece(a, chips[j], 1 - c), chips[j], 1 - c, (x, y, c))

        return first, landed, passed, from_sibling

    pairs = [(j, a) for j in range(3) for a in range(n)]

    def start(w_refs, o_refs, sems):
        first, _, _, _ = copies(w_refs, o_refs, sems)
        for j, a in pairs:
            first(j, a).start()

    def finish(w_refs, o_refs, sems):
        first, landed, passed, from_sibling = copies(w_refs, o_refs, sems)
        for j, a in pairs:
            landed(j, a).wait_recv()
            passed(j, a).start()
        for j, a in pairs:
            from_sibling(j, a).wait_recv()
        for j, a in pairs:
            first(j, a).wait_send()
            passed(j, a).wait_send()

    return Comm(list(ws), [jax.ShapeDtypeStruct((N_CHIPS,) + w.shape, w.dtype) for w in ws],
                [pltpu.SemaphoreType.DMA((6 * n,)), pltpu.SemaphoreType.DMA((6 * n,))], start, finish)


def _pair_exchange(name, gs):
    n = len(gs)

    def body(*refs):
        g_refs, o_refs = refs[:n], refs[n:2 * n]
        send_sems, recv_sems = refs[2 * n:]
        x, y, c, _ = _place()
        cps = []
        for a in range(n):
            half = gs[a].shape[1] // 2
            cp = pltpu.make_async_remote_copy(
                src_ref=g_refs[a].at[:, pl.ds((1 - c) * half, half), :], dst_ref=o_refs[a], send_sem=send_sems.at[a],
                recv_sem=recv_sems.at[a], device_id=(x, y, 1 - c), device_id_type=MESH)
            cp.start()
            cps.append(cp)
        for cp in cps:
            cp.wait()

    return _pallas(
        body, out_shape=[jax.ShapeDtypeStruct((N_CHIPS, g.shape[1] // 2, g.shape[2]), g.dtype) for g in gs],
        in_specs=[ANY] * n, out_specs=[ANY] * n,
        scratch_shapes=[pltpu.SemaphoreType.DMA((n,)), pltpu.SemaphoreType.DMA((n,))], name=name)(*gs)


def _pair_sum(name, g, recv, cidx):
    half, cols = recv.shape[1], recv.shape[2]
    tr = _row_tile(half)
    nblk = half // tr

    def body(c_ref, g_ref, r_ref, o_ref):
        o_ref[...] = (g_ref[...] + r_ref[...]).astype(o_ref.dtype)

    grid_spec = pltpu.PrefetchScalarGridSpec(
        num_scalar_prefetch=1, grid=(N_CHIPS, nblk),
        in_specs=[pl.BlockSpec((1, tr, cols), lambda s, i, c: (s, c[0] * nblk + i, 0)),
                  pl.BlockSpec((1, tr, cols), lambda s, i, c: (s, i, 0))],
        out_specs=pl.BlockSpec((1, tr, cols), lambda s, i, c: (s, i, 0)))
    return _pallas(
        body, out_shape=jax.ShapeDtypeStruct((N_CHIPS, half, cols), BF16), grid_spec=grid_spec,
        name=name, compiler_params=_params((N_CHIPS, nblk)))(cidx, g, recv)


def _exchange_comm(parts):
    n = len(parts)

    def copies(p_refs, o_refs, sems):
        send_sems, recv_sems = sems
        x, y, c, others = _place()
        me = 2 * x + y

        def copy(a, j, src_chip, dst_chip):
            px, py = others[j]
            return pltpu.make_async_remote_copy(
                src_ref=p_refs[a].at[src_chip], dst_ref=o_refs[a].at[dst_chip], send_sem=send_sems.at[3 * a + j],
                recv_sem=recv_sems.at[3 * a + j], device_id=(px, py, c), device_id_type=MESH)

        def send(j, a):
            return copy(a, j, 2 * others[j][0] + others[j][1], me)

        def arrival(j, a):
            return copy(a, j, me, 2 * others[j][0] + others[j][1])

        return send, arrival

    pairs = [(j, a) for j in range(3) for a in range(n)]

    def start(p_refs, o_refs, sems):
        send, _ = copies(p_refs, o_refs, sems)
        for j, a in pairs:
            send(j, a).start()

    def finish(p_refs, o_refs, sems):
        send, arrival = copies(p_refs, o_refs, sems)
        for j, a in pairs:
            arrival(j, a).wait_recv()
        for j, a in pairs:
            send(j, a).wait_send()

    return Comm(list(parts), [jax.ShapeDtypeStruct(p.shape, p.dtype) for p in parts],
                [pltpu.SemaphoreType.DMA((3 * n,)), pltpu.SemaphoreType.DMA((3 * n,))], start, finish)


def _chip_sum(name, part, slots, chip):
    half, cols = slots.shape[1], slots.shape[2]
    tr = _row_tile(half)

    def body(me_ref, p_ref, *rest):
        s_refs, o_ref = rest[:N_CHIPS], rest[N_CHIPS]
        own = p_ref[...].astype(F32)
        v = [jnp.where(me_ref[0] == k, own, s_refs[k][...].astype(F32)) for k in range(N_CHIPS)]
        o_ref[...] = ((v[0] + v[1]) + v[2]) + v[3]

    def slot_spec(k):
        return pl.BlockSpec((None, tr, cols), lambda i, me: (jnp.where(me[0] == k, (k + 1) % N_CHIPS, k), i, 0))

    grid_spec = pltpu.PrefetchScalarGridSpec(
        num_scalar_prefetch=1, grid=(half // tr,),
        in_specs=[pl.BlockSpec((None, tr, cols), lambda i, me: (me[0], i, 0))] + [slot_spec(k) for k in range(N_CHIPS)],
        out_specs=pl.BlockSpec((tr, cols), lambda i, me: (i, 0)))
    return _pallas(
        body, out_shape=jax.ShapeDtypeStruct((half, cols), F32), grid_spec=grid_spec,
        name=name, compiler_params=_params((half // tr,)))(chip, part, *([slots] * N_CHIPS))


def _pair_swap(fins):
    n = len(fins)

    def body(*refs):
        f_refs, o_refs = refs[:n], refs[n:2 * n]
        send_sems, recv_sems = refs[2 * n:]
        x, y, c, _ = _place()
        cps = [pltpu.make_async_remote_copy(src_ref=f_refs[a], dst_ref=o_refs[a], send_sem=send_sems.at[a],
                                            recv_sem=recv_sems.at[a], device_id=(x, y, 1 - c), device_id_type=MESH)
               for a in range(n)]
        for cp in cps:
            cp.start()
        for cp in cps:
            cp.wait()

    return _pallas(
        body, out_shape=[jax.ShapeDtypeStruct(f.shape, f.dtype) for f in fins], in_specs=[ANY] * n, out_specs=[ANY] * n,
        scratch_shapes=[pltpu.SemaphoreType.DMA((n,)), pltpu.SemaphoreType.DMA((n,))], name="grad_pair_swap")(*fins)


def _adamw(name, w, g_own, g_other, m, v, cidx):
    R, cols = w.shape[-2:]
    lead = (None,) * (w.ndim - 2)
    zeros = (0,) * (w.ndim - 2)
    half = R // 2
    tr = _row_tile(half, 128)
    nblk = half // tr
    c1 = 1.0 - ADAM_B1 ** ADAM_STEP
    c2 = 1.0 - ADAM_B2 ** ADAM_STEP

    def body(c_ref, w_ref, go_ref, gs_ref, m_ref, v_ref, g_ref, d_ref, nm_ref, nv_ref):
        mine = (pl.program_id(0) // nblk) == c_ref[0]
        gv = jnp.where(mine, go_ref[...], gs_ref[...])
        nm = ADAM_B1 * m_ref[...] + (1.0 - ADAM_B1) * gv
        nv = ADAM_B2 * v_ref[...] + (1.0 - ADAM_B2) * (gv * gv)
        g_ref[...] = gv
        d_ref[...] = -ADAM_LR * ((nm / c1) / (jnp.sqrt(nv / c2) + ADAM_EPS) + ADAM_WD * w_ref[...])
        nm_ref[...] = nm
        nv_ref[...] = nv

    spec = pl.BlockSpec(lead + (tr, cols), lambda i, c: zeros + (i, 0))
    hspec = pl.BlockSpec((tr, cols), lambda i, c: (i % nblk, 0))
    shape = jax.ShapeDtypeStruct(w.shape, F32)
    grid_spec = pltpu.PrefetchScalarGridSpec(num_scalar_prefetch=1, grid=(R // tr,),
                                             in_specs=[spec, hspec, hspec, spec, spec], out_specs=[spec] * 4)
    return _pallas(
        body, out_shape=[shape] * 4, grid_spec=grid_spec,
        name=name, compiler_params=_params((R // tr,)))(cidx, w, g_own, g_other, m, v)


PARAMS = (("meta", 1), ("norm1", None), ("w_in", 2), ("gdn_conv_w", 2), ("gdn_a_log", None), ("gdn_dt_bias", None),
          ("gdn_norm", None), ("w_out", 1), ("norm2", None), ("w_ffn_up", 2), ("ffn_conv_w", 2), ("ffn_conv_b", None),
          ("w_ffn_down", 1), ("norm_f", None))
BIG = ("w_in", "w_out", "w_ffn_up", "w_ffn_down")
PACK_ALIGN = 1024
PACK_ROWS_ALIGN = 32


def _pack(arrs, dtype):
    parts, total = [], 0
    for a in arrs:
        f = a.reshape(-1).astype(dtype)
        pad = (-f.shape[0]) % PACK_ALIGN
        parts.append(jnp.pad(f, (0, pad)) if pad else f)
        total += f.shape[0] + pad
    rows = total // LANES
    rpad = (-rows) % PACK_ROWS_ALIGN
    if rpad:
        parts.append(jnp.zeros((rpad * LANES,), dtype))
    return jnp.concatenate(parts).reshape(rows + rpad, LANES)


def _unpack(buf, shapes):
    flat = buf.reshape(-1)
    outs, off = [], 0
    for s in shapes:
        n = int(np.prod(s))
        outs.append(flat[off:off + n].reshape(s))
        off += n + (-n) % PACK_ALIGN
    return outs


def _split4(a, axis):
    n = a.shape[axis] // N_CHIPS
    return [lax.slice_in_dim(a, s * n, (s + 1) * n, axis=axis) for s in range(N_CHIPS)]


PROJ_ORDER = (3, 7, 8, 9, 0, 1, 2, 6, 4, 5)


def _reorder_w_in(w, cfg):
    d, hg = cfg.d, cfg.hg

    def block(k):
        off = k * d + (2 * hg if k >= 4 else 0)
        return w[:, off:off + d]

    tail = jnp.pad(w[:, 4 * d:4 * d + 2 * hg], ((0, 0), (0, LANES - 2 * hg)))
    return jnp.concatenate([block(k) for k in PROJ_ORDER] + [tail], axis=1)


def _restore_w_in(wr, cfg):
    d, hg = cfg.d, cfg.hg
    at = {k: i for i, k in enumerate(PROJ_ORDER)}
    block = lambda k: wr[:, at[k] * d:(at[k] + 1) * d]
    return jnp.concatenate([block(k) for k in range(4)] + [wr[:, 10 * d:10 * d + 2 * hg]] +
                           [block(k) for k in range(4, 10)], axis=1)


def _step(cfg, x, tgt, shard, m_shard, v_shard):
    d, hg, dff, rp, tr, tm = cfg.d, cfg.hg, cfg.dff, cfg.rp, cfg.tr, cfg.tm
    nrow = rp // tr
    assert cfg.tf * N_CHIPS == 2 * dff and cfg.din % N_CHIPS == 0
    cidx = lax.axis_index("c").astype(jnp.int32).reshape(1)
    chip = (2 * lax.axis_index("x") + lax.axis_index("y")).astype(jnp.int32).reshape(1)

    axis = dict(PARAMS)
    small = ("meta", "gdn_conv_w", "ffn_conv_w")
    small_shapes = [shard[n].shape for n in small]
    mine = [shard[n][0].astype(BF16) for n in BIG] + [_pack([shard[n] for n in small], F32)]

    def with_own(gathered, own):
        return [lax.dynamic_update_slice(g, w[None], (chip[0], 0, 0)) for g, w in zip(gathered, own)]

    g_in, g_small = _run_comm("weights_gather_first", _gather_comm([mine[0], mine[4]]))
    g_small, = with_own([g_small], [mine[4]])
    w_in_r = _reorder_w_in(jnp.concatenate([jnp.where(chip[0] == s, mine[0], g_in[s]) for s in range(N_CHIPS)], axis=1),
                           cfg)
    per_chip = [_unpack(g_small[s], small_shapes) for s in range(N_CHIPS)]
    full = {n: jnp.concatenate([per_chip[s][k] for s in range(N_CHIPS)], axis=axis[n]) for k, n in enumerate(small)}
    meta = full["meta"]
    gconv_w = full["gdn_conv_w"][0]
    fconv_w = full["ffn_conv_w"][0]
    norm1, norm2, gnorm = shard["norm1"], shard["norm2"], shard["gdn_norm"]
    normf = shard["norm_f"].reshape(1, d)
    fconv_b = shard["ffn_conv_b"]
    alog = jnp.pad(shard["gdn_a_log"], ((0, 7), (0, LANES - hg)))
    dtb = jnp.pad(shard["gdn_dt_bias"], ((0, 7), (0, LANES - hg)))

    h0 = jnp.concatenate([jnp.zeros((cfg.front, d), F32), meta, x], axis=0)
    half = RET_DK // 2
    pos = np.arange(rp, dtype=np.float32) - np.float32(cfg.front)
    inv = (np.float32(1.0) / np.float32(ROPE_BASE) ** (np.arange(half, dtype=np.float32) / np.float32(half))).astype(np.float32)
    ang = pos[:, None] * inv[None, :]
    cos, sin = jnp.asarray(np.cos(ang), F32), jnp.asarray(np.sin(ang), F32)
    rconsts = _ret_consts(cfg)

    tr_n = 3 * tr if rp % (3 * tr) == 0 else tr
    rms_f = _make_rms_fn(cfg, tr_n, False)
    rms_b = _make_rms_fn(cfg, tr_n, True)
    rowshape = jax.ShapeDtypeStruct((rp, d), F32)
    rspec = _rows(tr, d)
    nspec = _rows(tr_n, d)

    def rms_fwd(name, h, g):
        return _stage_fwd(name, rms_f, (rp // tr_n,), [In(h, nspec), In(g, _full(g))],
                          [jax.ShapeDtypeStruct((rp, d), BF16)], [nspec])[0]

    wm = 10 * d
    w_main, w_tail = w_in_r[:, :wm], w_in_r[:, wm:]
    tn_in = 2560 if wm % 2560 == 0 else LANES
    hn1 = rms_fwd("rms1_fwd", h0, norm1)
    proj, rest = _mm("proj_fwd", hn1, w_main, tm=tm, tn=tn_in, tk=d, out_dtype=BF16, comm=_gather_comm(mine[1:4]))
    ptail = _mm("proj_tail_fwd", hn1, w_tail, tm=tm, tn=LANES, tk=d)
    g_out, g_up, g_down = with_own(rest, mine[1:4])
    w_out = g_out.reshape(d, d)
    w_up = g_up
    w_up_t = jnp.swapaxes(g_up, 1, 2).reshape(2 * dff, d)
    w_down = g_down.reshape(dff, d)
    cqkv = _conv_fwd("gdn_conv_fwd", proj, CONV_COL * d, gconv_w, None, taps=GDN_CONV, width=3 * d, tr=tr, tc=d)
    prep_fn = _make_gdn_prep_fn(cfg, tr)
    prep_ins = [In(cqkv, _rows(tr, 3 * d), BF16), In(ptail, _rows(tr, LANES), BF16),
                In(alog, _full(alog), F32, True), In(dtb, _full(dtb), F32, True)]
    qn, kn, vv, bB, lB = _stage_fwd("gdn_prep_fwd", prep_fn, (nrow,), prep_ins, [rowshape] * 5, [rspec] * 5)

    trg = cfg.nb * CHUNK
    gi_grid = (rp // trg, hg)
    hspec = pl.BlockSpec((trg, GDN_DK), lambda i, h: (i, h))
    aspec = pl.BlockSpec((1, trg, CHUNK), lambda i, h: (h, i, 0))
    gspec = pl.BlockSpec((1, cfg.nb, 1, GDN_DK), lambda i, h: (h, i, 0, 0))
    intra_ins = [In(t, hspec, F32) for t in (qn, kn, vv, bB, lB)]
    ashape = jax.ShapeDtypeStruct((hg, rp, CHUNK), F32)
    intra_shapes = [rowshape, rowshape, ashape, rowshape, rowshape, jax.ShapeDtypeStruct((hg, cfg.nch, 1, GDN_DK), F32), ashape]
    intra_specs = [hspec, hspec, aspec, hspec, hspec, gspec, aspec]
    gu, gw, gattn, gqd, gkd, ggl, gtinv = _stage_fwd("gdn_intra_fwd", _gdn_intra_fn, gi_grid, intra_ins, intra_shapes,
                                                     intra_specs)
    rot_fn = _make_rot_fn(cfg)

    def rot_ins(dproj=None):
        return [In(proj, _rows(tr_n, 2 * d, ROT_COL // 2), BF16, galias=dproj, gshape=(rp, wm)),
                In(cos, _rows(tr_n, half)), In(sin, _rows(tr_n, half))]

    qr, kr = _stage_fwd("rot_fwd", rot_fn, (rp // tr_n,), rot_ins(), [jax.ShapeDtypeStruct((rp, d), BF16)] * 2, [nspec] * 2)
    nst = cfg.nch // cfg.sc
    oa, gss = _run_parts("gdn_scan_fwd", (nst,), [_gdn_scan_fwd(cfg, gu, gw, gattn, gqd, gkd, ggl)])[0]
    ob, rss = _run_parts("ret_scan_fwd", (nst,), [_ret_scan_fwd(cfg, qr, kr, proj, rconsts)])[0]

    mix_fn = _make_mix_fn(cfg)
    mix_ins = [In(oa, rspec, F32), In(ob, rspec, F32), In(proj, _rows(tr, 4 * d, MIX_COL // 4), BF16, gshape=(rp, wm)),
               In(gnorm, _full(gnorm), F32, True)]
    ymix = _stage_fwd("mix_fwd", mix_fn, (nrow,), mix_ins, [jax.ShapeDtypeStruct((rp, d), BF16)], [rspec])[0]
    h1 = _mm("out_proj_fwd", ymix, w_out, tm=tm, tn=d, tk=d, add=h0)

    hn2 = rms_fwd("rms2_fwd", h1, norm2)
    up = _mm("ffn_up_fwd", hn2, w_up, tm=tm, tn=cfg.tf, tk=d, out_dtype=BF16)
    uc = _conv_fwd("ffn_conv_fwd", up, 0, fconv_w, fconv_b, taps=FFN_CONV, width=2 * dff, tr=tr, tc=cfg.tf)
    tra = tr
    act_ins = [In(uc, _rows(tra, 2 * dff), BF16)]
    act_spec = _rows(tra, dff)
    act = _stage_fwd("ffn_act_fwd", _act_fn, (rp // tra,), act_ins, [jax.ShapeDtypeStruct((rp, dff), BF16)], [act_spec])[0]
    h2 = _mm("ffn_down_fwd", act, w_down, tm=tm, tn=d, tk=cfg.tf, add=h1)

    dh2, g_normf, loss_blk = _final(cfg, h2, normf, tgt)
    loss = lax.psum(loss_blk[0, 0], ("x", "y", "c"))

    g_w_down = _mm_tn("ffn_down_dw", act, dh2, tr=tm, tka=cfg.tf, tn=d)
    dact = _mm("ffn_down_dx", dh2, w_down.T, tm=tm, tn=cfg.tf, tk=d, out_dtype=BF16)
    duc, = _stage_bwd("ffn_act_bwd", _act_fn, (rp // tra,), act_ins, [(dact, act_spec)])
    dup, g_fconv_w, g_fconv_b = _conv_bwd("ffn_conv_bwd", up, 0, fconv_w, duc, taps=FFN_CONV, width=2 * dff,
                                          tr=tr, tc=cfg.tf, with_bias=True)
    g_w_up = _mm_tn("ffn_up_dw", hn2, dup, tr=tm, tka=d, tn=cfg.tf, blocked=True)

    def pair_reduce(tag, names, arrs):
        recvs = _pair_exchange("grad_pair_exchange_" + tag, arrs)
        return [_pair_sum("grad_pair_sum_" + n, g, r, cidx) for n, g, r in zip(names, arrs, recvs)]

    parts_ffn = pair_reduce("ffn", ["w_ffn_down", "w_ffn_up"], [g_w_down.reshape(N_CHIPS, dff // N_CHIPS, d), g_w_up])
    dhn2, slots_ffn = _mm("ffn_up_dx", dup, w_up_t, tm=tm, tn=d, tk=2 * cfg.tf, out_dtype=BF16,
                          comm=_exchange_comm(parts_ffn))

    def rms_bwd(name, h, g, dhn, dres):
        ins = [In(h, nspec, F32), In(g, _full(g), F32, True)]
        return _stage_bwd(name, rms_b, (rp // tr_n,), ins, [(dhn, nspec), (dres, nspec)])

    dh1, g_norm2 = rms_bwd("rms2_bwd", h1, norm2, dhn2, dh2)
    g_w_out = _mm_tn("out_proj_dw", ymix, dh1, tr=tm, tka=d, tn=d)
    dymix = _mm("out_proj_dx", dh1, w_out.T, tm=tm, tn=d, tk=d)
    doa, dob, dproj, g_gnorm = _stage_bwd("mix_bwd", mix_fn, (nrow,), mix_ins, [(dymix, rspec)])

    dqr, dkr, dproj = _run_parts("ret_scan_bwd", (nst,), [_ret_scan_bwd(cfg, dob, qr, kr, proj, rconsts, rss, dproj)])[0]
    dproj, = _stage_bwd("rot_bwd", rot_fn, (rp // tr_n,), rot_ins(dproj), [(dqr, nspec), (dkr, nspec)])
    dgu, dgw, dgattn, dgqd, dgkd, dggl = _run_parts(
        "gdn_scan_bwd", (nst,), [_gdn_scan_bwd(cfg, doa, gu, gw, gattn, gqd, gkd, ggl, gss)])[0]

    intra_cots = [(dgu, hspec), (dgw, hspec), (dgattn, aspec), (dgqd, hspec), (dgkd, hspec), (dggl, gspec)]
    dqn, dkn, dvv, dbB, dlB = _stage_bwd("gdn_intra_bwd", _gdn_intra_fn, gi_grid, intra_ins + [In(gtinv, aspec)], intra_cots)
    dcqkv, dtail, g_alog, g_dtb = _stage_bwd(
        "gdn_prep_bwd", prep_fn, (nrow,), prep_ins, [(t, rspec) for t in (dqn, dkn, dvv, dbB, dlB)])
    dproj, g_gconv_w = _conv_bwd("gdn_conv_bwd", proj, CONV_COL * d, gconv_w, dcqkv, taps=GDN_CONV, width=3 * d,
                                 tr=tr, tc=d, with_bias=False, dx_into=dproj)
    g_w_in_r = jnp.concatenate([_mm_tn("proj_dw", hn1, dproj, tr=tm, tka=d, tn=tn_in),
                                _mm_tn("proj_tail_dw", hn1, dtail, tr=tm, tka=d, tn=LANES)], axis=1)
    g_in4 = jnp.stack(_split4(_restore_w_in(g_w_in_r, cfg), 1))
    parts_mix = pair_reduce("mix", ["w_out", "w_in"], [g_w_out.reshape(N_CHIPS, d // N_CHIPS, d), g_in4])
    dhn1_tail = _mm("proj_tail_dx", dtail, w_tail.T, tm=tm, tn=d, tk=LANES)
    dhn1, slots_mix = _mm("proj_dx", dproj, w_main.T, tm=tm, tn=d, tk=tn_in // 2 if tn_in > LANES else LANES, add=dhn1_tail,
                          out_dtype=BF16, comm=_exchange_comm(parts_mix))
    dh0, g_norm1 = rms_bwd("rms1_bwd", h0, norm1, dhn1, dh1)

    grad_x = dh0[cfg.xrow:]
    small_grads = {
        "meta": dh0[cfg.front:cfg.xrow], "norm1": g_norm1, "gdn_conv_w": g_gconv_w[None],
        "gdn_a_log": g_alog[0:1, :hg], "gdn_dt_bias": g_dtb[0:1, :hg], "gdn_norm": g_gnorm, "norm2": g_norm2,
        "ffn_conv_w": g_fconv_w[None], "ffn_conv_b": g_fconv_b, "norm_f": g_normf.reshape(d),
    }

    small_names = [n for n, _ in PARAMS if n not in BIG]
    g_small = jnp.stack([_pack([small_grads[n] if axis[n] is None else _split4(small_grads[n], axis[n])[s]
                                for n in small_names], F32) for s in range(N_CHIPS)])
    parts_small = pair_reduce("small", ["small"], [g_small])
    slots_small = _run_comm("grad_exchange_small", _exchange_comm(parts_small))
    tags = ["w_in", "w_out", "w_ffn_up", "w_ffn_down", "small"]
    parts = [parts_mix[1], parts_mix[0], parts_ffn[1], parts_ffn[0], parts_small[0]]
    slots = [slots_mix[1], slots_mix[0], slots_ffn[1], slots_ffn[0], slots_small[0]]
    fins = [_chip_sum("grad_chip_sum_" + t, p, s, chip) for t, p, s in zip(tags, parts, slots)]
    sibs = _pair_swap(fins)

    def flat2(a):
        return a.reshape(-1, a.shape[-1])

    outs = {}
    for k, t in enumerate(BIG):
        res = _adamw("adamw_" + t, flat2(shard[t]), fins[k], sibs[k], flat2(m_shard[t]), flat2(v_shard[t]), cidx)
        outs[t] = [r.reshape(shard[t].shape) for r in res]
    small_shapes_all = [shard[n].shape for n in small_names]
    pk = lambda src: _pack([src[n] for n in small_names], F32)
    res = _adamw("adamw_small", pk(shard), fins[4], sibs[4], pk(m_shard), pk(v_shard), cidx)
    for k, r in enumerate(res):
        for n, a in zip(small_names, _unpack(r, small_shapes_all)):
            outs.setdefault(n, [None] * 4)[k] = a
    names = [n for n, _ in PARAMS]
    return (loss, grad_x[None], *[outs[n][k] for k in range(4) for n in names])


def kernel(x, meta, norm1, w_in, gdn_conv_w, gdn_a_log, gdn_dt_bias, gdn_norm, w_out, norm2, w_ffn_up, ffn_conv_w, ffn_conv_b, w_ffn_down, norm_f, loss_target, m_meta, m_norm1, m_w_in, m_gdn_conv_w, m_gdn_a_log, m_gdn_dt_bias, m_gdn_norm, m_w_out, m_norm2, m_w_ffn_up, m_ffn_conv_w, m_ffn_conv_b, m_w_ffn_down, m_norm_f, v_meta, v_norm1, v_w_in, v_gdn_conv_w, v_gdn_a_log, v_gdn_dt_bias, v_gdn_norm, v_w_out, v_norm2, v_w_ffn_up, v_ffn_conv_w, v_ffn_conv_b, v_w_ffn_down, v_norm_f):
    names = [n for n, _ in PARAMS]
    shard = dict(zip(names, (meta, norm1, w_in, gdn_conv_w, gdn_a_log, gdn_dt_bias, gdn_norm, w_out, norm2, w_ffn_up,
                             ffn_conv_w, ffn_conv_b, w_ffn_down, norm_f)))
    m_shard = dict(zip(names, (m_meta, m_norm1, m_w_in, m_gdn_conv_w, m_gdn_a_log, m_gdn_dt_bias, m_gdn_norm, m_w_out,
                               m_norm2, m_w_ffn_up, m_ffn_conv_w, m_ffn_conv_b, m_w_ffn_down, m_norm_f)))
    v_shard = dict(zip(names, (v_meta, v_norm1, v_w_in, v_gdn_conv_w, v_gdn_a_log, v_gdn_dt_bias, v_gdn_norm, v_w_out,
                               v_norm2, v_w_ffn_up, v_ffn_conv_w, v_ffn_conv_b, v_w_ffn_down, v_norm_f)))
    return _step(REAL, x[0], loss_target[0], shard, m_shard, v_shard)
```

```python
import functools
from typing import NamedTuple

import numpy as np
import jax
import jax.numpy as jnp
from jax import lax
from jax.experimental import pallas as pl
from jax.experimental.pallas import tpu as pltpu

F32 = jnp.float32
BF16 = jnp.bfloat16
EPS = 1e-6
CHUNK = 64
GDN_DK = 128
RET_DK = 256
GDN_CONV = 4
FFN_CONV = 3
ROPE_BASE = 10000.0
LANES = 128
N_CHIPS = 4
ADAM_LR, ADAM_B1, ADAM_B2, ADAM_EPS, ADAM_WD, ADAM_STEP = 0.001, 0.9, 0.999, 1e-08, 0.01, 10
MIX_COL, CONV_COL, RV_BLOCK, ROT_COL = 0, 4, 7, 8
MESH = pl.DeviceIdType.MESH
VMEM_LIMIT = 56 * 1024 * 1024


class Cfg(NamedTuple):
    d: int
    seq: int
    n_meta: int
    dff: int
    tr: int
    nb: int
    tm: int
    tf: int
    sc: int

    @property
    def hg(self): return self.d // GDN_DK
    @property
    def hr(self): return self.d // RET_DK
    @property
    def L(self): return self.n_meta + self.seq
    @property
    def rp(self): return -(-self.L // 256) * 256
    @property
    def front(self): return self.rp - self.L
    @property
    def xrow(self): return self.rp - self.seq
    @property
    def nch(self): return self.rp // CHUNK
    @property
    def din(self): return 10 * self.d + 2 * self.hg


REAL = Cfg(d=1024, seq=8192, n_meta=16, dff=2816, tr=256, nb=12, tm=1408, tf=1408, sc=6)


def _pallas(body, **kw):
    return pl.pallas_call(body, **kw)


def _sigmoid_raw(x):
    return 1.0 / (1.0 + jnp.exp(-x))


@jax.custom_vjp
def _sigmoid(x):
    return _sigmoid_raw(x)


def _sigmoid_fwd(x):
    s = _sigmoid_raw(x)
    return s, s


def _sigmoid_bwd(s, g):
    return (g * (s * (1.0 - s)),)


_sigmoid.defvjp(_sigmoid_fwd, _sigmoid_bwd)


@jax.custom_vjp
def _silu(x):
    return x * _sigmoid_raw(x)


def _silu_fwd(x):
    s = _sigmoid_raw(x)
    return x * s, (x, s)


def _silu_bwd(res, g):
    x, s = res
    return (g * (s * (1.0 + x * (1.0 - s))),)


_silu.defvjp(_silu_fwd, _silu_bwd)


def _softplus(x):
    return jnp.maximum(x, 0.0) + jnp.log(1.0 + jnp.exp(-jnp.abs(x)))


def _raw_dot(a, b, ta, tb, hi):
    if not hi:
        a = a.astype(BF16)
        b = b.astype(BF16)
    nbatch = a.ndim - 2
    ca = a.ndim - 2 if ta else a.ndim - 1
    cb = b.ndim - 1 if tb else b.ndim - 2
    batch = tuple(range(nbatch))
    return lax.dot_general(a, b, (((ca,), (cb,)), (batch, batch)),
                           precision=lax.Precision.HIGHEST if hi else None,
                           preferred_element_type=F32)


@functools.partial(jax.custom_vjp, nondiff_argnums=(2, 3, 4))
def _dot_p(a, b, ta, tb, hi):
    return _raw_dot(a, b, ta, tb, hi)


def _dot(a, b, ta=False, tb=False, hi=False):
    return _dot_p(a, b, ta, tb, hi)


def _dot_fwd(a, b, ta, tb, hi):
    return _raw_dot(a, b, ta, tb, hi), (a, b)


def _dot_bwd(ta, tb, hi, res, g):
    a, b = res
    if not ta and not tb:
        da, db = _dot(g, b, False, True, hi), _dot(a, g, True, False, hi)
    elif not ta and tb:
        da, db = _dot(g, b, False, False, hi), _dot(g, a, True, False, hi)
    elif ta and not tb:
        da, db = _dot(b, g, False, True, hi), _dot(a, g, False, False, hi)
    else:
        raise NotImplementedError
    return da.astype(a.dtype), db.astype(b.dtype)


_dot_p.defvjp(_dot_fwd, _dot_bwd)


def _iota2(n, m, axis):
    return lax.broadcasted_iota(jnp.int32, (n, m), axis)


def _bcast(mat, nb):
    return jnp.broadcast_to(mat[None], (nb,) + mat.shape)


def _split3(a):
    a0 = a.astype(BF16)
    r1 = a - a0.astype(F32)
    a1 = r1.astype(BF16)
    return a0, a1, (r1 - a1.astype(F32)).astype(BF16)


@functools.partial(jax.custom_vjp, nondiff_argnums=(2,))
def _dot_sel(a, e, te):
    eb = e.astype(BF16)
    p0, p1, p2 = (_raw_dot(p, eb, False, te, False) for p in _split3(a))
    return p0 + (p1 + p2)


def _dot_sel_fwd(a, e, te):
    return _dot_sel(a, e, te), e


def _dot_sel_bwd(te, e, g):
    return _dot_sel(g, e, not te), jnp.zeros_like(e)


_dot_sel.defvjp(_dot_sel_fwd, _dot_sel_bwd)


@jax.custom_vjp
def _sel_dot(e, x):
    eb = e.astype(BF16)
    p0, p1, p2 = (_raw_dot(eb, p, False, False, False) for p in _split3(x))
    return p0 + (p1 + p2)


def _sel_dot_fwd(e, x):
    return _sel_dot(e, x), e


def _sel_dot_bwd(e, g):
    eb = e.astype(BF16)
    p0, p1, p2 = (_raw_dot(eb, p, True, False, False) for p in _split3(g))
    return jnp.zeros_like(e), p0 + (p1 + p2)


_sel_dot.defvjp(_sel_dot_fwd, _sel_dot_bwd)


def _tri_inv_raw(m):
    nb = m.shape[0]
    r, c = _iota2(CHUNK, CHUNK, 0), _iota2(CHUNK, CHUNK, 1)
    t = _bcast((r == c).astype(F32), nb)
    b = 1
    while b < CHUNK:
        sh = b.bit_length() - 1
        off = ((r >> (sh + 1)) == (c >> (sh + 1))) & ((r >> sh) != (c >> sh)) & (r > c)
        cl = jnp.where(off[None], m, 0.0)
        t = t - _raw_dot(_raw_dot(t, cl, False, False, False), t, False, False, False)
        b *= 2
    return t


@jax.custom_vjp
def _tri_inv_given(m, t):
    return t


def _tri_inv_fwd(m, t):
    return t, t


def _tri_inv_bwd(t, g):
    return -_raw_dot(_raw_dot(t, g, True, False, False), t, False, True, False), jnp.zeros_like(t)


_tri_inv_given.defvjp(_tri_inv_fwd, _tri_inv_bwd)


def _rms(h, g):
    return h * lax.rsqrt(jnp.mean(h * h, axis=-1, keepdims=True) + EPS) * g


class In(NamedTuple):
    arr: jax.Array
    spec: pl.BlockSpec
    grad: object = None
    acc: bool = False
    gshape: object = None
    gspec: object = None
    galias: object = None


def _params(grid):
    sem = ("arbitrary",) * len(grid)
    return pltpu.CompilerParams(dimension_semantics=sem, vmem_limit_bytes=VMEM_LIMIT)


def _stage_fwd(name, fn, grid, ins, out_shapes, out_specs):
    n_in = len(ins)

    def body(*refs):
        pids = tuple(pl.program_id(k) for k in range(len(grid)))
        vals = [r[...].astype(F32) for r in refs[:n_in]]
        outs = fn(pids, *vals)
        for o_ref, o in zip(refs[n_in:], outs):
            o_ref[...] = o.reshape(o_ref.shape).astype(o_ref.dtype)

    return _pallas(
        body, out_shape=out_shapes, grid=grid, in_specs=[i.spec for i in ins],
        out_specs=out_specs, name=name, compiler_params=_params(grid))(*[i.arr for i in ins])


def _stage_bwd(name, fn, grid, ins, cots):
    n_in, n_ct = len(ins), len(cots)
    didx = [k for k, i in enumerate(ins) if i.grad is not None]
    aliased = [(o, ins[k].galias) for o, k in enumerate(didx) if ins[k].galias is not None]
    n_al = len(aliased)

    def body(*refs):
        pids = tuple(pl.program_id(k) for k in range(len(grid)))
        vals = [r[...].astype(F32) for r in refs[:n_in]]
        ct_refs = refs[n_in:n_in + n_ct]
        g_refs = refs[n_in + n_ct + n_al:]

        def f(*dv):
            merged = list(vals)
            for k, v in zip(didx, dv):
                merged[k] = v
            return tuple(fn(pids, *merged))

        outs, vjp_fn = jax.vjp(f, *[vals[k].astype(F32) for k in didx])
        cts = tuple(c[...].reshape(o.shape).astype(F32) for c, o in zip(ct_refs, outs))
        grads = vjp_fn(cts)
        first = functools.reduce(jnp.logical_and, [p == 0 for p in pids])
        for k, g_ref, g in zip(didx, g_refs, grads):
            if ins[k].acc:
                @pl.when(first)
                def _(g_ref=g_ref):
                    g_ref[...] = jnp.zeros(g_ref.shape, g_ref.dtype)
                g_ref[...] += g.reshape(g_ref.shape).astype(g_ref.dtype)
            else:
                g_ref[...] = g.reshape(g_ref.shape).astype(g_ref.dtype)

    out_shapes = [jax.ShapeDtypeStruct(ins[k].gshape or ins[k].arr.shape, ins[k].grad) for k in didx]
    out_specs = [ins[k].gspec or ins[k].spec for k in didx]
    return _pallas(
        body, out_shape=out_shapes, grid=grid,
        in_specs=[i.spec for i in ins] + [c[1] for c in cots] + [ANY] * n_al, out_specs=out_specs,
        input_output_aliases={n_in + n_ct + a: o for a, (o, _) in enumerate(aliased)},
        name=name, compiler_params=_params(grid))(*[i.arr for i in ins], *[c[0] for c in cots], *[a for _, a in aliased])


def _full(arr):
    nd = arr.ndim
    return pl.BlockSpec(arr.shape, lambda *p: (0,) * nd)


def _rows(tr, width, blk=0):
    return pl.BlockSpec((tr, width), lambda i: (i, blk))


def _mm(name, a, b, *, tm, tn, tk, out_dtype=F32, add=None, comm=None):
    M, K = a.shape
    N = b.shape[1] if b.ndim == 2 else b.shape[0] * b.shape[2]
    nk = K // tk
    grid = (M // tm, N // tn, nk)
    n_in = 3 if add is not None else 2
    n_ci, n_co = (len(comm.ins), len(comm.outs)) if comm is not None else (0, 0)

    def body(*refs):
        a_ref, b_ref = refs[0], refs[1]
        add_ref = refs[2] if add is not None else None
        c_ins = refs[n_in:n_in + n_ci]
        o_ref = refs[n_in + n_ci]
        c_outs = refs[n_in + n_ci + 1:n_in + n_ci + 1 + n_co]
        scratch = refs[n_in + n_ci + 1 + n_co:]
        acc_ref = scratch[0] if nk > 1 else None
        sems = scratch[1 if nk > 1 else 0:]
        step = (pl.program_id(0) * grid[1] + pl.program_id(1)) * nk + pl.program_id(2)
        if comm is not None:
            @pl.when(step == 0)
            def _():
                comm.start(c_ins, c_outs, sems)

        part = _raw_dot(a_ref[...], b_ref[...], False, False, False)

        def finish(total):
            if add_ref is not None:
                total = total + add_ref[...]
            o_ref[...] = total.astype(o_ref.dtype)

        if nk == 1:
            finish(part)
        else:
            k = pl.program_id(2)

            @pl.when(k == 0)
            def _():
                acc_ref[...] = part

            @pl.when(k > 0)
            def _():
                acc_ref[...] += part

            @pl.when(k == nk - 1)
            def _():
                finish(acc_ref[...])

        if comm is not None:
            @pl.when(step == grid[0] * grid[1] * nk - 1)
            def _():
                comm.finish(c_ins, c_outs, sems)

    b_spec = (pl.BlockSpec((tk, tn), lambda i, j, k: (k, j)) if b.ndim == 2 else
              pl.BlockSpec((None, tk, tn), lambda i, j, k: (j, k, 0)))
    in_specs = [pl.BlockSpec((tm, tk), lambda i, j, k: (i, k)), b_spec]
    args = [a, b]
    if add is not None:
        in_specs.append(pl.BlockSpec((tm, tn), lambda i, j, k: (i, j)))
        args.append(add)
    out_shape = jax.ShapeDtypeStruct((M, N), out_dtype)
    out_spec = pl.BlockSpec((tm, tn), lambda i, j, k: (i, j))
    scratch = [pltpu.VMEM((tm, tn), F32)] if nk > 1 else []
    if comm is None:
        return _pallas(body, out_shape=out_shape, grid=grid, in_specs=in_specs, out_specs=out_spec,
                       scratch_shapes=scratch, name=name, compiler_params=_params(grid))(*args)
    res = _pallas(body, out_shape=[out_shape] + comm.outs, grid=grid, in_specs=in_specs + [ANY] * n_ci,
                  out_specs=[out_spec] + [ANY] * n_co, scratch_shapes=scratch + comm.sems, name=name,
                  compiler_params=_params(grid))(*args, *comm.ins)
    return res[0], res[1:]


def _mm_tn(name, a, b, *, tr, tka, tn, blocked=False):
    R, Ka = a.shape
    N = b.shape[1]
    nr = R // tr
    grid = (Ka // tka, N // tn, nr)
    if blocked:
        out_shape = jax.ShapeDtypeStruct((N // tn, Ka, tn), F32)
        out_spec = pl.BlockSpec((None, tka, tn), lambda i, j, r: (j, i, 0))
    else:
        out_shape = jax.ShapeDtypeStruct((Ka, N), F32)
        out_spec = pl.BlockSpec((tka, tn), lambda i, j, r: (i, j))

    def body(a_ref, b_ref, o_ref):
        r = pl.program_id(2)
        part = _raw_dot(a_ref[...], b_ref[...], True, False, False)

        @pl.when(r == 0)
        def _():
            o_ref[...] = part

        @pl.when(r > 0)
        def _():
            o_ref[...] += part

    return _pallas(
        body, out_shape=out_shape, grid=grid,
        in_specs=[pl.BlockSpec((tr, tka), lambda i, j, r: (r, i)),
                  pl.BlockSpec((tr, tn), lambda i, j, r: (r, j))],
        out_specs=out_spec, name=name, compiler_params=_params(grid))(a, b)


def _conv_fwd(name, x, xcol0, w, b, *, taps, width, tr, tc):
    R = x.shape[0]
    grid = (width // tc, R // tr)
    cb0 = xcol0 // tc
    hrows = 16 if x.dtype == BF16 else 8
    hb = tr // hrows

    def body(*refs):
        x_ref, xp_ref, w_ref = refs[:3]
        b_ref = refs[3] if b is not None else None
        o_ref = refs[-1]
        i = pl.program_id(1)
        xv = x_ref[...].astype(F32)
        prev = jnp.where(i > 0, xp_ref[...].astype(F32)[hrows - 8:, :], 0.0)
        ext = jnp.concatenate([prev, xv], axis=0)
        acc = xv * w_ref[taps - 1:taps, :]
        for s in range(1, taps):
            acc = acc + pltpu.roll(ext, s, 0)[8:, :] * w_ref[taps - 1 - s:taps - s, :]
        if b_ref is not None:
            acc = acc + b_ref[...]
        o_ref[...] = acc.astype(o_ref.dtype)

    in_specs = [pl.BlockSpec((tr, tc), lambda j, i: (i, cb0 + j)),
                pl.BlockSpec((hrows, tc), lambda j, i: (jnp.maximum(i * hb - 1, 0), cb0 + j)),
                pl.BlockSpec((taps, tc), lambda j, i: (0, j))]
    args = [x, x, w]
    if b is not None:
        in_specs.append(pl.BlockSpec((1, tc), lambda j, i: (0, j)))
        args.append(b)
    return _pallas(
        body, out_shape=jax.ShapeDtypeStruct((R, width), BF16), grid=grid, in_specs=in_specs,
        out_specs=pl.BlockSpec((tr, tc), lambda j, i: (i, j)),
        name=name, compiler_params=_params(grid))(*args)


def _conv_bwd(name, x, xcol0, w, dy, *, taps, width, tr, tc, with_bias, dx_into=None):
    R = x.shape[0]
    nr = R // tr
    grid = (width // tc, nr)
    cb0 = xcol0 // tc
    hrows = 16 if dy.dtype == BF16 else 8
    hb = tr // hrows
    n_ext = tr + 8
    n_al = 0 if dx_into is None else 1

    def body(*refs):
        x_ref, w_ref, dy_ref, dyn_ref = refs[:4]
        dx_ref, dw_ref = refs[4 + n_al], refs[5 + n_al]
        db_ref = refs[6 + n_al] if with_bias else None
        i = pl.program_id(1)
        xv = x_ref[...].astype(F32)
        dyv = dy_ref[...].astype(F32)
        nxt = dyn_ref[...].astype(F32)[:8, :]
        dext = jnp.concatenate([dyv, jnp.where(i < nr - 1, nxt, 0.0)], axis=0)
        dx = dyv * w_ref[taps - 1:taps, :]
        dws = [None] * taps
        dws[taps - 1] = jnp.sum(xv * dyv, axis=0, keepdims=True)
        for s in range(1, taps):
            ahead = pltpu.roll(dext, n_ext - s, 0)[:tr, :]
            dx = dx + ahead * w_ref[taps - 1 - s:taps - s, :]
            dws[taps - 1 - s] = jnp.sum(xv * ahead, axis=0, keepdims=True)
        dx_ref[...] = dx.astype(dx_ref.dtype)

        @pl.when(i == 0)
        def _():
            for k in range(taps):
                dw_ref[k:k + 1, :] = dws[k]
            if db_ref is not None:
                db_ref[...] = jnp.sum(dyv, axis=0, keepdims=True)

        @pl.when(i > 0)
        def _():
            for k in range(taps):
                dw_ref[k:k + 1, :] += dws[k]
            if db_ref is not None:
                db_ref[...] += jnp.sum(dyv, axis=0, keepdims=True)

    in_specs = [pl.BlockSpec((tr, tc), lambda j, i: (i, cb0 + j)),
                pl.BlockSpec((taps, tc), lambda j, i: (0, j)),
                pl.BlockSpec((tr, tc), lambda j, i: (i, j)),
                pl.BlockSpec((hrows, tc), lambda j, i: (jnp.minimum((i + 1) * hb, R // hrows - 1), j))]
    args = [x, w, dy, dy]
    if dx_into is None:
        dx_shape, dx_spec, aliases = jax.ShapeDtypeStruct((R, width), BF16), pl.BlockSpec((tr, tc), lambda j, i: (i, j)), {}
    else:
        dx_shape = jax.ShapeDtypeStruct(dx_into.shape, dx_into.dtype)
        dx_spec, aliases = pl.BlockSpec((tr, tc), lambda j, i: (i, cb0 + j)), {4: 0}
        in_specs.append(ANY)
        args.append(dx_into)
    out_shape = [dx_shape, jax.ShapeDtypeStruct((taps, width), F32)]
    out_specs = [dx_spec, pl.BlockSpec((taps, tc), lambda j, i: (0, j))]
    if with_bias:
        out_shape.append(jax.ShapeDtypeStruct((1, width), F32))
        out_specs.append(pl.BlockSpec((1, tc), lambda j, i: (0, j)))
    return _pallas(
        body, out_shape=out_shape, grid=grid, in_specs=in_specs, out_specs=out_specs, input_output_aliases=aliases,
        name=name, compiler_params=_params(grid))(*args)


def _row_mask(cfg, i, tr):
    rows = i * tr + lax.broadcasted_iota(jnp.int32, (tr, 1), 0)
    return (rows >= cfg.front).astype(F32)


def _make_rms_fn(cfg, tr, with_residual):
    def fn(pids, h, g):
        hm = h * _row_mask(cfg, pids[0], tr)
        if with_residual:
            return _rms(hm, g), hm
        return (_rms(hm, g),)
    return fn


def _make_gdn_prep_fn(cfg, tr):
    d, hg = cfg.d, cfg.hg

    def fn(pids, c, tail, alog, dtb):
        cq, ck, cv = c[:, :d], c[:, d:2 * d], c[:, 2 * d:]
        mask = _row_mask(cfg, pids[0], tr)
        j, col = _iota2(LANES, d, 0), _iota2(LANES, d, 1)
        ea = ((col >> 7) == j).astype(F32)
        eb = ((col >> 7) + hg == j).astype(F32)
        al = jnp.sum(alog, axis=0, keepdims=True)
        db = jnp.sum(dtb, axis=0, keepdims=True)
        lg = _dot_sel(-jnp.exp(al) * _softplus(tail + db) * mask, ea, False)
        beta = _dot_sel(_sigmoid(tail) * mask, eb, False)
        sq, sk, sv = _silu(cq), _silu(ck), _silu(cv)
        qs, ks = [], []
        for h in range(hg):
            sl = slice(h * GDN_DK, (h + 1) * GDN_DK)
            qh, kh = sq[:, sl], sk[:, sl]
            qs.append(qh * lax.rsqrt(jnp.sum(qh * qh, axis=-1, keepdims=True) + EPS) * (GDN_DK ** -0.5))
            ks.append(kh * lax.rsqrt(jnp.sum(kh * kh, axis=-1, keepdims=True) + EPS))
        return jnp.concatenate(qs, axis=1), jnp.concatenate(ks, axis=1), sv, beta, lg
    return fn


def _gdn_intra_fn(pids, q, k, v, bB, lB, t_saved=None):
    rows = q.shape[0]
    nb = rows // CHUNK
    q3, k3, v3, b3, l3 = [t.reshape(nb, CHUNK, GDN_DK) for t in (q, k, v, bB, lB)]
    r, c = _iota2(CHUNK, CHUNK, 0), _iota2(CHUNK, CHUNK, 1)
    tril = (r >= c)
    strict = (r > c)
    gcol = _sel_dot(_bcast(tril.astype(F32), nb), l3)
    grow = jnp.swapaxes(gcol, 1, 2)[:, :CHUNK, :]
    diff = gcol[:, :, :CHUNK] - grow
    decay = jnp.where(tril[None], jnp.exp(jnp.where(tril[None], diff, 0.0)), 0.0)
    kb = k3 * b3
    m = jnp.where(strict[None], _dot(kb, k3, False, True) * decay, 0.0)
    t = _tri_inv_raw(m) if t_saved is None else _tri_inv_given(m, t_saved.reshape(nb, CHUNK, CHUNK))
    eg = jnp.exp(gcol)
    u = _dot(t, v3 * b3)
    w = _dot(t, kb * eg)
    attn = _dot(q3, k3, False, True) * decay
    qd = q3 * eg
    glast = jnp.sum(l3, axis=1, keepdims=True)
    kd = k3 * jnp.exp(glast - gcol)
    gl = jnp.exp(glast)
    outs = (u.reshape(rows, GDN_DK), w.reshape(rows, GDN_DK), attn.reshape(1, rows, CHUNK),
            qd.reshape(rows, GDN_DK), kd.reshape(rows, GDN_DK), gl.reshape(1, nb, 1, GDN_DK))
    return outs + (t.reshape(1, rows, CHUNK),) if t_saved is None else outs


def _make_rot_fn(cfg):
    hr = cfg.hr
    half = RET_DK // 2

    def fn(pids, rqk, cos, sin):
        rq, rk = rqk[:, :cfg.d], rqk[:, cfg.d:]

        def rot(t, scale):
            outs = []
            for h in range(hr):
                x1 = t[:, h * RET_DK:h * RET_DK + half]
                x2 = t[:, h * RET_DK + half:(h + 1) * RET_DK]
                outs += [(x1 * cos - x2 * sin) * scale, (x2 * cos + x1 * sin) * scale]
            return jnp.concatenate(outs, axis=1)
        return rot(rq, 1.0), rot(rk, RET_DK ** -0.5)
    return fn


def _make_mix_fn(cfg):
    hg, hr = cfg.hg, cfg.hr

    def fn(pids, oa, ob, pm, gnorm):
        d = cfg.d
        gz, rg, gate_a, gate_b = pm[:, :d], pm[:, d:2 * d], pm[:, 2 * d:3 * d], pm[:, 3 * d:]
        oas = []
        for h in range(hg):
            oh = oa[:, h * GDN_DK:(h + 1) * GDN_DK]
            oas.append(oh * lax.rsqrt(jnp.mean(oh * oh, axis=-1, keepdims=True) + EPS) * gnorm)
        ya = jnp.concatenate(oas, axis=1) * _silu(gz)
        obs = []
        for h in range(hr):
            oh = ob[:, h * RET_DK:(h + 1) * RET_DK]
            obs.append(oh * lax.rsqrt(jnp.mean(oh * oh, axis=-1, keepdims=True) + EPS))
        yb = _silu(rg) * jnp.concatenate(obs, axis=1)
        return (_sigmoid(gate_a) * ya + _sigmoid(gate_b) * yb,)
    return fn


def _act_fn(pids, u):
    f = u.shape[1] // 2
    return (_silu(u[:, :f]) * u[:, f:],)


def _gdn_step(s, u, w, a, qd, kd, gl):
    top = _dot(jnp.concatenate([w, qd], axis=0), s)
    v_new = u - top[:CHUNK]
    bot = _dot(jnp.concatenate([a, kd.T], axis=0), v_new)
    o = top[CHUNK:] + bot[:CHUNK]
    s2 = s * gl + bot[CHUNK:]
    return s2, o


def _gdn_step_bwd(s, u, w, a, qd, kd, gl, ds2, do):
    lw = jnp.concatenate([w, qd], axis=0)
    v_new = u - _raw_dot(w, s, False, False, False)
    dv = _raw_dot(a, do, True, False, False) + _raw_dot(kd, ds2, False, False, False)
    da = _raw_dot(do, v_new, False, True, False)
    dkd = _raw_dot(v_new, ds2, False, True, False)
    dtop = jnp.concatenate([-dv, do], axis=0)
    dlw = _raw_dot(dtop, s, False, True, False)
    ds = ds2 * gl + _raw_dot(lw, dtop, True, False, False)
    dgl = jnp.sum(ds2 * s, axis=0, keepdims=True)
    return ds, dv, dlw[:CHUNK], da, dlw[CHUNK:], dkd, dgl


def _ret_step(s, q, k, v, dm, qdc, kdc, g):
    att = _dot(q, k, False, True) * dm
    bot = _dot(jnp.concatenate([att, (k * kdc).T], axis=0), v)
    o = bot[:CHUNK] + _dot(q * qdc, s)
    s2 = s * g + bot[CHUNK:]
    return s2, o


class Part(NamedTuple):
    body: object
    args: list
    in_specs: list
    out_shape: list
    out_specs: list
    scratch: list
    aliases: dict = {}


def _run_parts(name, grid, parts):
    n_in = [len(p.args) for p in parts]
    n_out = [len(p.out_shape) for p in parts]
    n_sc = [len(p.scratch) for p in parts]
    off_in = [sum(n_in[:k]) for k in range(len(parts))]
    off_out = [sum(n_out[:k]) for k in range(len(parts))]
    off_sc = [sum(n_sc[:k]) for k in range(len(parts))]

    def body(*refs):
        ins, outs, scr = refs[:sum(n_in)], refs[sum(n_in):sum(n_in) + sum(n_out)], refs[sum(n_in) + sum(n_out):]
        for k, p in enumerate(parts):
            p.body(*ins[off_in[k]:off_in[k] + n_in[k]], *outs[off_out[k]:off_out[k] + n_out[k]],
                   *scr[off_sc[k]:off_sc[k] + n_sc[k]])

    aliases = {off_in[k] + i: off_out[k] + o for k, p in enumerate(parts) for i, o in p.aliases.items()}
    res = _pallas(
        body, out_shape=sum((p.out_shape for p in parts), []), grid=grid, in_specs=sum((p.in_specs for p in parts), []),
        out_specs=sum((p.out_specs for p in parts), []), scratch_shapes=sum((p.scratch for p in parts), []),
        input_output_aliases=aliases, name=name, compiler_params=_params(grid))(*sum((p.args for p in parts), []))
    return [res[off_out[k]:off_out[k] + n_out[k]] for k in range(len(parts))]


def _gdn_scan_fwd(cfg, u, w, attn, qd, kd, gl):
    d, hg, nch, sc = cfg.d, cfg.hg, cfg.nch, cfg.sc
    nst = nch // sc

    def body(u_ref, w_ref, a_ref, qd_ref, kd_ref, gl_ref, o_ref, ss_ref, s_ref):
        @pl.when(pl.program_id(0) == 0)
        def _():
            s_ref[...] = jnp.zeros(s_ref.shape, F32)

        states = [s_ref[h] for h in range(hg)]
        for j in range(sc):
            rows = slice(j * CHUNK, (j + 1) * CHUNK)
            outs = []
            for h in range(hg):
                sl = slice(h * GDN_DK, (h + 1) * GDN_DK)
                ss_ref[j, h] = states[h]
                states[h], o = _gdn_step(states[h], u_ref[rows, sl], w_ref[rows, sl], a_ref[h, rows, :],
                                         qd_ref[rows, sl], kd_ref[rows, sl], gl_ref[h, j])
                outs.append(o)
            o_ref[rows, :] = jnp.concatenate(outs, axis=1)
        for h in range(hg):
            s_ref[h] = states[h]

    row = pl.BlockSpec((sc * CHUNK, d), lambda n: (n, 0))
    return Part(
        body, [u, w, attn, qd, kd, gl],
        [row, row, pl.BlockSpec((hg, sc * CHUNK, CHUNK), lambda n: (0, n, 0)), row, row,
         pl.BlockSpec((hg, sc, 1, GDN_DK), lambda n: (0, n, 0, 0))],
        [jax.ShapeDtypeStruct((cfg.rp, d), F32), jax.ShapeDtypeStruct((nch, hg, GDN_DK, GDN_DK), F32)],
        [row, pl.BlockSpec((sc, hg, GDN_DK, GDN_DK), lambda n: (n, 0, 0, 0))],
        [pltpu.VMEM((hg, GDN_DK, GDN_DK), F32)])


def _gdn_scan_bwd(cfg, do, u, w, attn, qd, kd, gl, ss):
    d, hg, nch, sc = cfg.d, cfg.hg, cfg.nch, cfg.sc
    nst = nch // sc

    def body(do_ref, u_ref, w_ref, a_ref, qd_ref, kd_ref, gl_ref, ss_ref,
             du_ref, dw_ref, da_ref, dqd_ref, dkd_ref, dgl_ref, ds_ref):
        @pl.when(pl.program_id(0) == 0)
        def _():
            ds_ref[...] = jnp.zeros(ds_ref.shape, F32)

        dstates = [ds_ref[h] for h in range(hg)]
        for j in reversed(range(sc)):
            rows = slice(j * CHUNK, (j + 1) * CHUNK)
            dus, dws, dqds, dkds = [], [], [], []
            for h in range(hg):
                sl = slice(h * GDN_DK, (h + 1) * GDN_DK)
                args = (ss_ref[j, h], u_ref[rows, sl], w_ref[rows, sl], a_ref[h, rows, :], qd_ref[rows, sl],
                        kd_ref[rows, sl], gl_ref[h, j])
                dstates[h], du, dw, da, dqd, dkd, dgl = _gdn_step_bwd(*args, dstates[h], do_ref[rows, sl])
                da_ref[h, rows, :] = da
                dgl_ref[h, j] = dgl
                dus.append(du)
                dws.append(dw)
                dqds.append(dqd)
                dkds.append(dkd)
            du_ref[rows, :] = jnp.concatenate(dus, axis=1)
            dw_ref[rows, :] = jnp.concatenate(dws, axis=1)
            dqd_ref[rows, :] = jnp.concatenate(dqds, axis=1)
            dkd_ref[rows, :] = jnp.concatenate(dkds, axis=1)
        for h in range(hg):
            ds_ref[h] = dstates[h]

    row = pl.BlockSpec((sc * CHUNK, d), lambda n: (nst - 1 - n, 0))
    aspec = pl.BlockSpec((hg, sc * CHUNK, CHUNK), lambda n: (0, nst - 1 - n, 0))
    gspec = pl.BlockSpec((hg, sc, 1, GDN_DK), lambda n: (0, nst - 1 - n, 0, 0))
    rowshape = jax.ShapeDtypeStruct((cfg.rp, d), F32)
    return Part(
        body, [do, u, w, attn, qd, kd, gl, ss],
        [row, row, row, aspec, row, row, gspec, pl.BlockSpec((sc, hg, GDN_DK, GDN_DK), lambda n: (nst - 1 - n, 0, 0, 0))],
        [rowshape, rowshape, jax.ShapeDtypeStruct(attn.shape, F32), rowshape, rowshape, jax.ShapeDtypeStruct(gl.shape, F32)],
        [row, row, aspec, row, row, gspec],
        [pltpu.VMEM((hg, GDN_DK, GDN_DK), F32)])


def _ret_consts(cfg):
    hr = cfg.hr
    lg = np.log(1.0 - 2.0 ** (-5.0 - np.arange(hr, dtype=np.float64)))
    idx = np.arange(CHUNK, dtype=np.float64)
    tril = np.tril(np.ones((CHUNK, CHUNK), dtype=bool))
    dm = np.where(tril[None], np.exp((idx[:, None] - idx[None, :])[None] * lg[:, None, None]), 0.0)
    qdc = np.exp((idx[None, :] + 1.0) * lg[:, None])
    kdc = np.exp((CHUNK - 1.0 - idx[None, :]) * lg[:, None])
    gch = np.exp(CHUNK * lg)
    qdc = np.broadcast_to(qdc[:, :, None], (hr, CHUNK, RET_DK))
    kdc = np.broadcast_to(kdc[:, :, None], (hr, CHUNK, RET_DK))
    gch = np.broadcast_to(gch[:, None, None], (hr, 1, RET_DK))
    return tuple(jnp.asarray(np.ascontiguousarray(t), F32) for t in (dm, qdc, kdc, gch))


def _ret_scan_fwd(cfg, qr, kr, proj, consts):
    d, hr, nch, sc = cfg.d, cfg.hr, cfg.nch, cfg.sc
    nst = nch // sc
    dm, qdc, kdc, gch = consts

    def body(q_ref, k_ref, v_ref, dm_ref, qdc_ref, kdc_ref, g_ref, o_ref, ss_ref, s_ref):
        @pl.when(pl.program_id(0) == 0)
        def _():
            s_ref[...] = jnp.zeros(s_ref.shape, F32)

        states = [s_ref[h] for h in range(hr)]
        for j in range(sc):
            rows = slice(j * CHUNK, (j + 1) * CHUNK)
            outs = []
            for h in range(hr):
                sl = slice(h * RET_DK, (h + 1) * RET_DK)
                ss_ref[j, h] = states[h]
                states[h], o = _ret_step(states[h], q_ref[rows, sl], k_ref[rows, sl], v_ref[rows, sl], dm_ref[h],
                                         qdc_ref[h], kdc_ref[h], g_ref[h])
                outs.append(o)
            o_ref[rows, :] = jnp.concatenate(outs, axis=1)
        for h in range(hr):
            s_ref[h] = states[h]

    row = pl.BlockSpec((sc * CHUNK, d), lambda n: (n, 0))
    return Part(
        body, [qr, kr, proj, dm, qdc, kdc, gch],
        [row, row, pl.BlockSpec((sc * CHUNK, d), lambda n: (n, RV_BLOCK)), _full(dm), _full(qdc), _full(kdc), _full(gch)],
        [jax.ShapeDtypeStruct((cfg.rp, d), F32), jax.ShapeDtypeStruct((nch, hr, RET_DK, RET_DK), F32)],
        [row, pl.BlockSpec((sc, hr, RET_DK, RET_DK), lambda n: (n, 0, 0, 0))],
        [pltpu.VMEM((hr, RET_DK, RET_DK), F32)])


def _ret_scan_bwd(cfg, do, qr, kr, proj, consts, ss, dproj):
    d, hr, nch, sc = cfg.d, cfg.hr, cfg.nch, cfg.sc
    nst = nch // sc
    dm, qdc, kdc, gch = consts

    def body(do_ref, q_ref, k_ref, v_ref, dm_ref, qdc_ref, kdc_ref, g_ref, ss_ref, _, dq_ref, dk_ref, dv_ref, ds_ref):
        @pl.when(pl.program_id(0) == 0)
        def _():
            ds_ref[...] = jnp.zeros(ds_ref.shape, F32)

        dstates = [ds_ref[h] for h in range(hr)]
        for j in reversed(range(sc)):
            rows = slice(j * CHUNK, (j + 1) * CHUNK)
            dqs, dks, dvs = [], [], []
            for h in range(hr):
                sl = slice(h * RET_DK, (h + 1) * RET_DK)
                cs = (dm_ref[h], qdc_ref[h], kdc_ref[h], g_ref[h])
                _, vjp_fn = jax.vjp(lambda s, q, k, v, cs=cs: _ret_step(s, q, k, v, *cs),
                                    ss_ref[j, h], q_ref[rows, sl].astype(F32), k_ref[rows, sl].astype(F32), v_ref[rows, sl])
                dstates[h], dq, dk, dv = vjp_fn((dstates[h], do_ref[rows, sl]))
                dqs.append(dq)
                dks.append(dk)
                dvs.append(dv)
            dq_ref[rows, :] = jnp.concatenate(dqs, axis=1).astype(dq_ref.dtype)
            dk_ref[rows, :] = jnp.concatenate(dks, axis=1).astype(dk_ref.dtype)
            dv_ref[rows, :] = jnp.concatenate(dvs, axis=1).astype(dv_ref.dtype)
        for h in range(hr):
            ds_ref[h] = dstates[h]

    row = pl.BlockSpec((sc * CHUNK, d), lambda n: (nst - 1 - n, 0))
    rowshape = jax.ShapeDtypeStruct((cfg.rp, d), BF16)
    vspec = pl.BlockSpec((sc * CHUNK, d), lambda n: (nst - 1 - n, RV_BLOCK))
    return Part(
        body, [do, qr, kr, proj, dm, qdc, kdc, gch, ss, dproj],
        [row, row, row, vspec, _full(dm), _full(qdc), _full(kdc), _full(gch),
         pl.BlockSpec((sc, hr, RET_DK, RET_DK), lambda n: (nst - 1 - n, 0, 0, 0)), ANY],
        [rowshape, rowshape, jax.ShapeDtypeStruct(dproj.shape, dproj.dtype)],
        [row, row, vspec],
        [pltpu.VMEM((hr, RET_DK, RET_DK), F32)], {9: 2})


def _final(cfg, h2, normf, tgt):
    d, tr = cfg.d, cfg.xrow
    nr = cfg.rp // tr

    def body(h_ref, g_ref, t_ref, dh_ref, dg_ref, loss_ref):
        i = pl.program_id(0)
        y, vjp_fn = jax.vjp(_rms, h_ref[...], g_ref[...])
        err = jnp.where(i >= 1, y - t_ref[...], 0.0)
        dh, dg = vjp_fn(err * (1.0 / d))
        dh_ref[...] = dh
        part = jnp.zeros((8, LANES), F32) + 0.5 * jnp.sum(err * err) * (1.0 / d)

        @pl.when(i == 0)
        def _():
            dg_ref[...] = dg
            loss_ref[...] = part

        @pl.when(i > 0)
        def _():
            dg_ref[...] += dg
            loss_ref[...] += part

    return _pallas(
        body,
        out_shape=[jax.ShapeDtypeStruct((cfg.rp, d), F32), jax.ShapeDtypeStruct((1, d), F32),
                   jax.ShapeDtypeStruct((8, LANES), F32)],
        grid=(nr,),
        in_specs=[_rows(tr, d), _full(normf), pl.BlockSpec((tr, d), lambda i: (jnp.maximum(i - 1, 0), 0))],
        out_specs=[_rows(tr, d), pl.BlockSpec((1, d), lambda i: (0, 0)), pl.BlockSpec((8, LANES), lambda i: (0, 0))],
        name="final_loss", compiler_params=_params((nr,)))(h2, normf, tgt)


ANY = pl.BlockSpec(memory_space=pl.ANY)


def _place():
    x, y, c = lax.axis_index("x"), lax.axis_index("y"), lax.axis_index("c")
    others = [(1 - x, y), (x, 1 - y), (1 - x, 1 - y)]
    return x, y, c, others


def _row_tile(rows, cap=256):
    return max(t for t in range(16, min(rows, cap) + 1, 16) if rows % t == 0)


class Comm(NamedTuple):
    ins: list
    outs: list
    sems: list
    start: object
    finish: object


def _run_comm(name, comm):
    n_in, n_out = len(comm.ins), len(comm.outs)

    def body(*refs):
        ins, outs, sems = refs[:n_in], refs[n_in:n_in + n_out], refs[n_in + n_out:]
        comm.start(ins, outs, sems)
        comm.finish(ins, outs, sems)

    return _pallas(body, out_shape=comm.outs, in_specs=[ANY] * n_in, out_specs=[ANY] * n_out,
                   scratch_shapes=comm.sems, name=name)(*comm.ins)


def _gather_comm(ws):
    n = len(ws)
    halves = [w.shape[0] // 2 for w in ws]

    def copies(w_refs, o_refs, sems):
        send_sems, recv_sems = sems
        x, y, c, others = _place()
        me = 2 * x + y
        chips = [2 * px + py for px, py in others]

        def piece(a, chip, core):
            return o_refs[a].at[chip, pl.ds(core * halves[a], halves[a]), :]

        def copy(a, k, src, chip, core, to):
            return pltpu.make_async_remote_copy(src_ref=src, dst_ref=piece(a, chip, core), send_sem=send_sems.at[6 * a + k],
                                                recv_sem=recv_sems.at[6 * a + k], device_id=to, device_id_type=MESH)

        def first(j, a):
            return copy(a, j, w_refs[a].at[pl.ds(c * halves[a], halves[a]), :], me, c, (*others[j], c))

        def landed(j, a):
            return copy(a, j, piece(a, chips[j], c), chips[j], c, (x, y, c))

        def passed(j, a):
            return copy(a, 3 + j, piece(a, chips[j], c), chips[j], c, (x, y, 1 - c))

        def from_sibling(j, a):
            return copy(a, 3 + j, piece(a, chips[j], 1 - c), chips[j], 1 - c, (x, y, c))

        return first, landed, passed, from_sibling

    pairs = [(j, a) for j in range(3) for a in range(n)]

    def start(w_refs, o_refs, sems):
        first, _, _, _ = copies(w_refs, o_refs, sems)
        for j, a in pairs:
            first(j, a).start()

    def finish(w_refs, o_refs, sems):
        first, landed, passed, from_sibling = copies(w_refs, o_refs, sems)
        for j, a in pairs:
            landed(j, a).wait_recv()
            passed(j, a).start()
        for j, a in pairs:
            from_sibling(j, a).wait_recv()
        for j, a in pairs:
            first(j, a).wait_send()
            passed(j, a).wait_send()

    return Comm(list(ws), [jax.ShapeDtypeStruct((N_CHIPS,) + w.shape, w.dtype) for w in ws],
                [pltpu.SemaphoreType.DMA((6 * n,)), pltpu.SemaphoreType.DMA((6 * n,))], start, finish)


def _pair_comm(gs):
    n = len(gs)

    def copy(g_refs, o_refs, sems, a):
        x, y, c, _ = _place()
        half = gs[a].shape[1] // 2
        return pltpu.make_async_remote_copy(
            src_ref=g_refs[a].at[:, pl.ds((1 - c) * half, half), :], dst_ref=o_refs[a], send_sem=sems[0].at[a],
            recv_sem=sems[1].at[a], device_id=(x, y, 1 - c), device_id_type=MESH)

    def start(g_refs, o_refs, sems):
        for a in range(n):
            copy(g_refs, o_refs, sems, a).start()

    def finish(g_refs, o_refs, sems):
        for a in range(n):
            copy(g_refs, o_refs, sems, a).wait()

    return Comm(list(gs), [jax.ShapeDtypeStruct((N_CHIPS, g.shape[1] // 2, g.shape[2]), g.dtype) for g in gs],
                [pltpu.SemaphoreType.DMA((n,)), pltpu.SemaphoreType.DMA((n,))], start, finish)


def _pair_exchange(name, gs):
    n = len(gs)

    def body(*refs):
        g_refs, o_refs = refs[:n], refs[n:2 * n]
        send_sems, recv_sems = refs[2 * n:]
        x, y, c, _ = _place()
        cps = []
        for a in range(n):
            half = gs[a].shape[1] // 2
            cp = pltpu.make_async_remote_copy(
                src_ref=g_refs[a].at[:, pl.ds((1 - c) * half, half), :], dst_ref=o_refs[a], send_sem=send_sems.at[a],
                recv_sem=recv_sems.at[a], device_id=(x, y, 1 - c), device_id_type=MESH)
            cp.start()
            cps.append(cp)
        for cp in cps:
            cp.wait()

    return _pallas(
        body, out_shape=[jax.ShapeDtypeStruct((N_CHIPS, g.shape[1] // 2, g.shape[2]), g.dtype) for g in gs],
        in_specs=[ANY] * n, out_specs=[ANY] * n,
        scratch_shapes=[pltpu.SemaphoreType.DMA((n,)), pltpu.SemaphoreType.DMA((n,))], name=name)(*gs)


def _pair_sum(name, g, recv, cidx):
    half, cols = recv.shape[1], recv.shape[2]
    tr = _row_tile(half)
    nblk = half // tr

    def body(c_ref, g_ref, r_ref, o_ref):
        o_ref[...] = (g_ref[...] + r_ref[...]).astype(o_ref.dtype)

    grid_spec = pltpu.PrefetchScalarGridSpec(
        num_scalar_prefetch=1, grid=(N_CHIPS, nblk),
        in_specs=[pl.BlockSpec((1, tr, cols), lambda s, i, c: (s, c[0] * nblk + i, 0)),
                  pl.BlockSpec((1, tr, cols), lambda s, i, c: (s, i, 0))],
        out_specs=pl.BlockSpec((1, tr, cols), lambda s, i, c: (s, i, 0)))
    return _pallas(
        body, out_shape=jax.ShapeDtypeStruct((N_CHIPS, half, cols), BF16), grid_spec=grid_spec,
        name=name, compiler_params=_params((N_CHIPS, nblk)))(cidx, g, recv)


def _exchange_comm(parts):
    n = len(parts)

    def copies(p_refs, o_refs, sems):
        send_sems, recv_sems = sems
        x, y, c, others = _place()
        me = 2 * x + y

        def copy(a, j, src_chip, dst_chip):
            px, py = others[j]
            return pltpu.make_async_remote_copy(
                src_ref=p_refs[a].at[src_chip], dst_ref=o_refs[a].at[dst_chip], send_sem=send_sems.at[3 * a + j],
                recv_sem=recv_sems.at[3 * a + j], device_id=(px, py, c), device_id_type=MESH)

        def send(j, a):
            return copy(a, j, 2 * others[j][0] + others[j][1], me)

        def arrival(j, a):
            return copy(a, j, me, 2 * others[j][0] + others[j][1])

        return send, arrival

    pairs = [(j, a) for j in range(3) for a in range(n)]

    def start(p_refs, o_refs, sems):
        send, _ = copies(p_refs, o_refs, sems)
        for j, a in pairs:
            send(j, a).start()

    def finish(p_refs, o_refs, sems):
        send, arrival = copies(p_refs, o_refs, sems)
        for j, a in pairs:
            arrival(j, a).wait_recv()
        for j, a in pairs:
            send(j, a).wait_send()

    return Comm(list(parts), [jax.ShapeDtypeStruct(p.shape, p.dtype) for p in parts],
                [pltpu.SemaphoreType.DMA((3 * n,)), pltpu.SemaphoreType.DMA((3 * n,))], start, finish)


def _chip_sum(name, part, slots, chip):
    half, cols = slots.shape[1], slots.shape[2]
    tr = _row_tile(half)

    def body(me_ref, p_ref, *rest):
        s_refs, o_ref = rest[:N_CHIPS], rest[N_CHIPS]
        own = p_ref[...].astype(F32)
        v = [jnp.where(me_ref[0] == k, own, s_refs[k][...].astype(F32)) for k in range(N_CHIPS)]
        o_ref[...] = ((v[0] + v[1]) + v[2]) + v[3]

    def slot_spec(k):
        return pl.BlockSpec((None, tr, cols), lambda i, me: (jnp.where(me[0] == k, (k + 1) % N_CHIPS, k), i, 0))

    grid_spec = pltpu.PrefetchScalarGridSpec(
        num_scalar_prefetch=1, grid=(half // tr,),
        in_specs=[pl.BlockSpec((None, tr, cols), lambda i, me: (me[0], i, 0))] + [slot_spec(k) for k in range(N_CHIPS)],
        out_specs=pl.BlockSpec((tr, cols), lambda i, me: (i, 0)))
    return _pallas(
        body, out_shape=jax.ShapeDtypeStruct((half, cols), F32), grid_spec=grid_spec,
        name=name, compiler_params=_params((half // tr,)))(chip, part, *([slots] * N_CHIPS))


def _pair_swap(fins):
    n = len(fins)

    def body(*refs):
        f_refs, o_refs = refs[:n], refs[n:2 * n]
        send_sems, recv_sems = refs[2 * n:]
        x, y, c, _ = _place()
        cps = [pltpu.make_async_remote_copy(src_ref=f_refs[a], dst_ref=o_refs[a], send_sem=send_sems.at[a],
                                            recv_sem=recv_sems.at[a], device_id=(x, y, 1 - c), device_id_type=MESH)
               for a in range(n)]
        for cp in cps:
            cp.start()
        for cp in cps:
            cp.wait()

    return _pallas(
        body, out_shape=[jax.ShapeDtypeStruct(f.shape, f.dtype) for f in fins], in_specs=[ANY] * n, out_specs=[ANY] * n,
        scratch_shapes=[pltpu.SemaphoreType.DMA((n,)), pltpu.SemaphoreType.DMA((n,))], name="grad_pair_swap")(*fins)


def _adamw(name, w, g_own, g_other, m, v, cidx):
    R, cols = w.shape[-2:]
    lead = (None,) * (w.ndim - 2)
    zeros = (0,) * (w.ndim - 2)
    half = R // 2
    tr = _row_tile(half, 128)
    nblk = half // tr
    c1 = 1.0 - ADAM_B1 ** ADAM_STEP
    c2 = 1.0 - ADAM_B2 ** ADAM_STEP

    def body(c_ref, w_ref, go_ref, gs_ref, m_ref, v_ref, g_ref, d_ref, nm_ref, nv_ref):
        mine = (pl.program_id(0) // nblk) == c_ref[0]
        gv = jnp.where(mine, go_ref[...], gs_ref[...])
        nm = ADAM_B1 * m_ref[...] + (1.0 - ADAM_B1) * gv
        nv = ADAM_B2 * v_ref[...] + (1.0 - ADAM_B2) * (gv * gv)
        g_ref[...] = gv
        d_ref[...] = -ADAM_LR * ((nm / c1) / (jnp.sqrt(nv / c2) + ADAM_EPS) + ADAM_WD * w_ref[...])
        nm_ref[...] = nm
        nv_ref[...] = nv

    spec = pl.BlockSpec(lead + (tr, cols), lambda i, c: zeros + (i, 0))
    hspec = pl.BlockSpec((tr, cols), lambda i, c: (i % nblk, 0))
    shape = jax.ShapeDtypeStruct(w.shape, F32)
    grid_spec = pltpu.PrefetchScalarGridSpec(num_scalar_prefetch=1, grid=(R // tr,),
                                             in_specs=[spec, hspec, hspec, spec, spec], out_specs=[spec] * 4)
    return _pallas(
        body, out_shape=[shape] * 4, grid_spec=grid_spec,
        name=name, compiler_params=_params((R // tr,)))(cidx, w, g_own, g_other, m, v)


PARAMS = (("meta", 1), ("norm1", None), ("w_in", 2), ("gdn_conv_w", 2), ("gdn_a_log", None), ("gdn_dt_bias", None),
          ("gdn_norm", None), ("w_out", 1), ("norm2", None), ("w_ffn_up", 2), ("ffn_conv_w", 2), ("ffn_conv_b", None),
          ("w_ffn_down", 1), ("norm_f", None))
BIG = ("w_in", "w_out", "w_ffn_up", "w_ffn_down")
PACK_ALIGN = 1024
PACK_ROWS_ALIGN = 32


def _pack(arrs, dtype):
    parts, total = [], 0
    for a in arrs:
        f = a.reshape(-1).astype(dtype)
        pad = (-f.shape[0]) % PACK_ALIGN
        parts.append(jnp.pad(f, (0, pad)) if pad else f)
        total += f.shape[0] + pad
    rows = total // LANES
    rpad = (-rows) % PACK_ROWS_ALIGN
    if rpad:
        parts.append(jnp.zeros((rpad * LANES,), dtype))
    return jnp.concatenate(parts).reshape(rows + rpad, LANES)


def _unpack(buf, shapes):
    flat = buf.reshape(-1)
    outs, off = [], 0
    for s in shapes:
        n = int(np.prod(s))
        outs.append(flat[off:off + n].reshape(s))
        off += n + (-n) % PACK_ALIGN
    return outs


def _split4(a, axis):
    n = a.shape[axis] // N_CHIPS
    return [lax.slice_in_dim(a, s * n, (s + 1) * n, axis=axis) for s in range(N_CHIPS)]


PROJ_ORDER = (3, 7, 8, 9, 0, 1, 2, 6, 4, 5)


def _reorder_w_in(w, cfg):
    d, hg = cfg.d, cfg.hg

    def block(k):
        off = k * d + (2 * hg if k >= 4 else 0)
        return w[:, off:off + d]

    tail = jnp.pad(w[:, 4 * d:4 * d + 2 * hg], ((0, 0), (0, LANES - 2 * hg)))
    return jnp.concatenate([block(k) for k in PROJ_ORDER] + [tail], axis=1)


def _restore_w_in(wr, cfg):
    d, hg = cfg.d, cfg.hg
    at = {k: i for i, k in enumerate(PROJ_ORDER)}
    block = lambda k: wr[:, at[k] * d:(at[k] + 1) * d]
    return jnp.concatenate([block(k) for k in range(4)] + [wr[:, 10 * d:10 * d + 2 * hg]] +
                           [block(k) for k in range(4, 10)], axis=1)


def _step(cfg, x, tgt, shard, m_shard, v_shard):
    d, hg, dff, rp, tr, tm = cfg.d, cfg.hg, cfg.dff, cfg.rp, cfg.tr, cfg.tm
    nrow = rp // tr
    assert cfg.tf * N_CHIPS == 2 * dff and cfg.din % N_CHIPS == 0
    cidx = lax.axis_index("c").astype(jnp.int32).reshape(1)
    chip = (2 * lax.axis_index("x") + lax.axis_index("y")).astype(jnp.int32).reshape(1)

    axis = dict(PARAMS)
    small = ("meta", "gdn_conv_w", "ffn_conv_w")
    small_shapes = [shard[n].shape for n in small]
    mine = [shard[n][0].astype(BF16) for n in BIG] + [_pack([shard[n] for n in small], F32)]

    def with_own(gathered, own):
        return [lax.dynamic_update_slice(g, w[None], (chip[0], 0, 0)) for g, w in zip(gathered, own)]

    g_in, g_small = _run_comm("weights_gather_first", _gather_comm([mine[0], mine[4]]))
    g_small, = with_own([g_small], [mine[4]])
    w_in_r = _reorder_w_in(jnp.concatenate([jnp.where(chip[0] == s, mine[0], g_in[s]) for s in range(N_CHIPS)], axis=1),
                           cfg)
    per_chip = [_unpack(g_small[s], small_shapes) for s in range(N_CHIPS)]
    full = {n: jnp.concatenate([per_chip[s][k] for s in range(N_CHIPS)], axis=axis[n]) for k, n in enumerate(small)}
    meta = full["meta"]
    gconv_w = full["gdn_conv_w"][0]
    fconv_w = full["ffn_conv_w"][0]
    norm1, norm2, gnorm = shard["norm1"], shard["norm2"], shard["gdn_norm"]
    normf = shard["norm_f"].reshape(1, d)
    fconv_b = shard["ffn_conv_b"]
    alog = jnp.pad(shard["gdn_a_log"], ((0, 7), (0, LANES - hg)))
    dtb = jnp.pad(shard["gdn_dt_bias"], ((0, 7), (0, LANES - hg)))

    h0 = jnp.concatenate([jnp.zeros((cfg.front, d), F32), meta, x], axis=0)
    half = RET_DK // 2
    pos = np.arange(rp, dtype=np.float32) - np.float32(cfg.front)
    inv = (np.float32(1.0) / np.float32(ROPE_BASE) ** (np.arange(half, dtype=np.float32) / np.float32(half))).astype(np.float32)
    ang = pos[:, None] * inv[None, :]
    cos, sin = jnp.asarray(np.cos(ang), F32), jnp.asarray(np.sin(ang), F32)
    rconsts = _ret_consts(cfg)

    tr_n = 3 * tr if rp % (3 * tr) == 0 else tr
    rms_f = _make_rms_fn(cfg, tr_n, False)
    rms_b = _make_rms_fn(cfg, tr_n, True)
    rowshape = jax.ShapeDtypeStruct((rp, d), F32)
    rspec = _rows(tr, d)
    nspec = _rows(tr_n, d)

    def rms_fwd(name, h, g):
        return _stage_fwd(name, rms_f, (rp // tr_n,), [In(h, nspec), In(g, _full(g))],
                          [jax.ShapeDtypeStruct((rp, d), BF16)], [nspec])[0]

    wm = 10 * d
    w_main, w_tail = w_in_r[:, :wm], w_in_r[:, wm:]
    tn_in = 2560 if wm % 2560 == 0 else LANES
    hn1 = rms_fwd("rms1_fwd", h0, norm1)
    proj, rest = _mm("proj_fwd", hn1, w_main, tm=tm, tn=tn_in, tk=d, out_dtype=BF16, comm=_gather_comm(mine[1:4]))
    ptail = _mm("proj_tail_fwd", hn1, w_tail, tm=tm, tn=LANES, tk=d)
    g_out, g_up, g_down = with_own(rest, mine[1:4])
    w_out = g_out.reshape(d, d)
    w_up = g_up
    w_up_t = jnp.swapaxes(g_up, 1, 2).reshape(2 * dff, d)
    w_down = g_down.reshape(dff, d)
    cqkv = _conv_fwd("gdn_conv_fwd", proj, CONV_COL * d, gconv_w, None, taps=GDN_CONV, width=3 * d, tr=tr, tc=d)
    prep_fn = _make_gdn_prep_fn(cfg, tr)
    prep_ins = [In(cqkv, _rows(tr, 3 * d), BF16), In(ptail, _rows(tr, LANES), BF16),
                In(alog, _full(alog), F32, True), In(dtb, _full(dtb), F32, True)]
    qn, kn, vv, bB, lB = _stage_fwd("gdn_prep_fwd", prep_fn, (nrow,), prep_ins, [rowshape] * 5, [rspec] * 5)

    trg = cfg.nb * CHUNK
    gi_grid = (rp // trg, hg)
    hspec = pl.BlockSpec((trg, GDN_DK), lambda i, h: (i, h))
    aspec = pl.BlockSpec((1, trg, CHUNK), lambda i, h: (h, i, 0))
    gspec = pl.BlockSpec((1, cfg.nb, 1, GDN_DK), lambda i, h: (h, i, 0, 0))
    intra_ins = [In(t, hspec, F32) for t in (qn, kn, vv, bB, lB)]
    ashape = jax.ShapeDtypeStruct((hg, rp, CHUNK), F32)
    intra_shapes = [rowshape, rowshape, ashape, rowshape, rowshape, jax.ShapeDtypeStruct((hg, cfg.nch, 1, GDN_DK), F32), ashape]
    intra_specs = [hspec, hspec, aspec, hspec, hspec, gspec, aspec]
    gu, gw, gattn, gqd, gkd, ggl, gtinv = _stage_fwd("gdn_intra_fwd", _gdn_intra_fn, gi_grid, intra_ins, intra_shapes,
                                                     intra_specs)
    rot_fn = _make_rot_fn(cfg)

    def rot_ins(dproj=None):
        return [In(proj, _rows(tr_n, 2 * d, ROT_COL // 2), BF16, galias=dproj, gshape=(rp, wm)),
                In(cos, _rows(tr_n, half)), In(sin, _rows(tr_n, half))]

    qr, kr = _stage_fwd("rot_fwd", rot_fn, (rp // tr_n,), rot_ins(), [jax.ShapeDtypeStruct((rp, d), BF16)] * 2, [nspec] * 2)
    nst = cfg.nch // cfg.sc
    oa, gss = _run_parts("gdn_scan_fwd", (nst,), [_gdn_scan_fwd(cfg, gu, gw, gattn, gqd, gkd, ggl)])[0]
    ob, rss = _run_parts("ret_scan_fwd", (nst,), [_ret_scan_fwd(cfg, qr, kr, proj, rconsts)])[0]

    mix_fn = _make_mix_fn(cfg)
    mix_ins = [In(oa, rspec, F32), In(ob, rspec, F32), In(proj, _rows(tr, 4 * d, MIX_COL // 4), BF16, gshape=(rp, wm)),
               In(gnorm, _full(gnorm), F32, True)]
    ymix = _stage_fwd("mix_fwd", mix_fn, (nrow,), mix_ins, [jax.ShapeDtypeStruct((rp, d), BF16)], [rspec])[0]
    h1 = _mm("out_proj_fwd", ymix, w_out, tm=tm, tn=d, tk=d, add=h0)

    hn2 = rms_fwd("rms2_fwd", h1, norm2)
    up = _mm("ffn_up_fwd", hn2, w_up, tm=tm, tn=cfg.tf, tk=d, out_dtype=BF16)
    uc = _conv_fwd("ffn_conv_fwd", up, 0, fconv_w, fconv_b, taps=FFN_CONV, width=2 * dff, tr=tr, tc=cfg.tf)
    tra = tr
    act_ins = [In(uc, _rows(tra, 2 * dff), BF16)]
    act_spec = _rows(tra, dff)
    act = _stage_fwd("ffn_act_fwd", _act_fn, (rp // tra,), act_ins, [jax.ShapeDtypeStruct((rp, dff), BF16)], [act_spec])[0]
    h2 = _mm("ffn_down_fwd", act, w_down, tm=tm, tn=d, tk=cfg.tf, add=h1)

    dh2, g_normf, loss_blk = _final(cfg, h2, normf, tgt)
    loss = lax.psum(loss_blk[0, 0], ("x", "y", "c"))

    g_w_down = _mm_tn("ffn_down_dw", act, dh2, tr=tm, tka=cfg.tf, tn=d)
    g_w_down4 = g_w_down.reshape(N_CHIPS, dff // N_CHIPS, d)
    dact, (recv_down,) = _mm("ffn_down_dx", dh2, w_down.T, tm=tm, tn=cfg.tf, tk=d, out_dtype=BF16,
                             comm=_pair_comm([g_w_down4]))
    duc, = _stage_bwd("ffn_act_bwd", _act_fn, (rp // tra,), act_ins, [(dact, act_spec)])
    dup, g_fconv_w, g_fconv_b = _conv_bwd("ffn_conv_bwd", up, 0, fconv_w, duc, taps=FFN_CONV, width=2 * dff,
                                          tr=tr, tc=cfg.tf, with_bias=True)
    g_w_up = _mm_tn("ffn_up_dw", hn2, dup, tr=tm, tka=d, tn=cfg.tf, blocked=True)

    def pair_reduce(tag, names, arrs):
        recvs = _pair_exchange("grad_pair_exchange_" + tag, arrs)
        return [_pair_sum("grad_pair_sum_" + n, g, r, cidx) for n, g, r in zip(names, arrs, recvs)]

    parts_ffn = ([_pair_sum("grad_pair_sum_w_ffn_down", g_w_down4, recv_down, cidx)] +
                 pair_reduce("ffn", ["w_ffn_up"], [g_w_up]))
    dhn2, slots_ffn = _mm("ffn_up_dx", dup, w_up_t, tm=tm, tn=d, tk=2 * cfg.tf, out_dtype=BF16,
                          comm=_exchange_comm(parts_ffn))

    def rms_bwd(name, h, g, dhn, dres):
        ins = [In(h, nspec, F32), In(g, _full(g), F32, True)]
        return _stage_bwd(name, rms_b, (rp // tr_n,), ins, [(dhn, nspec), (dres, nspec)])

    dh1, g_norm2 = rms_bwd("rms2_bwd", h1, norm2, dhn2, dh2)
    g_w_out = _mm_tn("out_proj_dw", ymix, dh1, tr=tm, tka=d, tn=d)
    dymix = _mm("out_proj_dx", dh1, w_out.T, tm=tm, tn=d, tk=d)
    doa, dob, dproj, g_gnorm = _stage_bwd("mix_bwd", mix_fn, (nrow,), mix_ins, [(dymix, rspec)])

    dqr, dkr, dproj = _run_parts("ret_scan_bwd", (nst,), [_ret_scan_bwd(cfg, dob, qr, kr, proj, rconsts, rss, dproj)])[0]
    dproj, = _stage_bwd("rot_bwd", rot_fn, (rp // tr_n,), rot_ins(dproj), [(dqr, nspec), (dkr, nspec)])
    dgu, dgw, dgattn, dgqd, dgkd, dggl = _run_parts(
        "gdn_scan_bwd", (nst,), [_gdn_scan_bwd(cfg, doa, gu, gw, gattn, gqd, gkd, ggl, gss)])[0]

    intra_cots = [(dgu, hspec), (dgw, hspec), (dgattn, aspec), (dgqd, hspec), (dgkd, hspec), (dggl, gspec)]
    dqn, dkn, dvv, dbB, dlB = _stage_bwd("gdn_intra_bwd", _gdn_intra_fn, gi_grid, intra_ins + [In(gtinv, aspec)], intra_cots)
    dcqkv, dtail, g_alog, g_dtb = _stage_bwd(
        "gdn_prep_bwd", prep_fn, (nrow,), prep_ins, [(t, rspec) for t in (dqn, dkn, dvv, dbB, dlB)])
    dproj, g_gconv_w = _conv_bwd("gdn_conv_bwd", proj, CONV_COL * d, gconv_w, dcqkv, taps=GDN_CONV, width=3 * d,
                                 tr=tr, tc=d, with_bias=False, dx_into=dproj)
    g_w_in_r = jnp.concatenate([_mm_tn("proj_dw", hn1, dproj, tr=tm, tka=d, tn=tn_in),
                                _mm_tn("proj_tail_dw", hn1, dtail, tr=tm, tka=d, tn=LANES)], axis=1)
    g_in4 = jnp.stack(_split4(_restore_w_in(g_w_in_r, cfg), 1))
    parts_mix = pair_reduce("mix", ["w_out", "w_in"], [g_w_out.reshape(N_CHIPS, d // N_CHIPS, d), g_in4])
    dhn1_tail = _mm("proj_tail_dx", dtail, w_tail.T, tm=tm, tn=d, tk=LANES)
    dhn1, slots_mix = _mm("proj_dx", dproj, w_main.T, tm=tm, tn=d, tk=tn_in // 2 if tn_in > LANES else LANES, add=dhn1_tail,
                          out_dtype=BF16, comm=_exchange_comm(parts_mix))
    dh0, g_norm1 = rms_bwd("rms1_bwd", h0, norm1, dhn1, dh1)

    grad_x = dh0[cfg.xrow:]
    small_grads = {
        "meta": dh0[cfg.front:cfg.xrow], "norm1": g_norm1, "gdn_conv_w": g_gconv_w[None],
        "gdn_a_log": g_alog[0:1, :hg], "gdn_dt_bias": g_dtb[0:1, :hg], "gdn_norm": g_gnorm, "norm2": g_norm2,
        "ffn_conv_w": g_fconv_w[None], "ffn_conv_b": g_fconv_b, "norm_f": g_normf.reshape(d),
    }

    small_names = [n for n, _ in PARAMS if n not in BIG]
    g_small = jnp.stack([_pack([small_grads[n] if axis[n] is None else _split4(small_grads[n], axis[n])[s]
                                for n in small_names], F32) for s in range(N_CHIPS)])
    parts_small = pair_reduce("small", ["small"], [g_small])
    slots_small = _run_comm("grad_exchange_small", _exchange_comm(parts_small))
    tags = ["w_in", "w_out", "w_ffn_up", "w_ffn_down", "small"]
    parts = [parts_mix[1], parts_mix[0], parts_ffn[1], parts_ffn[0], parts_small[0]]
    slots = [slots_mix[1], slots_mix[0], slots_ffn[1], slots_ffn[0], slots_small[0]]
    fins = [_chip_sum("grad_chip_sum_" + t, p, s, chip) for t, p, s in zip(tags, parts, slots)]
    sibs = _pair_swap(fins)

    def flat2(a):
        return a.reshape(-1, a.shape[-1])

    outs = {}
    for k, t in enumerate(BIG):
        res = _adamw("adamw_" + t, flat2(shard[t]), fins[k], sibs[k], flat2(m_shard[t]), flat2(v_shard[t]), cidx)
        outs[t] = [r.reshape(shard[t].shape) for r in res]
    small_shapes_all = [shard[n].shape for n in small_names]
    pk = lambda src: _pack([src[n] for n in small_names], F32)
    res = _adamw("adamw_small", pk(shard), fins[4], sibs[4], pk(m_shard), pk(v_shard), cidx)
    for k, r in enumerate(res):
        for n, a in zip(small_names, _unpack(r, small_shapes_all)):
            outs.setdefault(n, [None] * 4)[k] = a
    names = [n for n, _ in PARAMS]
    return (loss, grad_x[None], *[outs[n][k] for k in range(4) for n in names])


def kernel(x, meta, norm1, w_in, gdn_conv_w, gdn_a_log, gdn_dt_bias, gdn_norm, w_out, norm2, w_ffn_up, ffn_conv_w, ffn_conv_b, w_ffn_down, norm_f, loss_target, m_meta, m_norm1, m_w_in, m_gdn_conv_w, m_gdn_a_log, m_gdn_dt_bias, m_gdn_norm, m_w_out, m_norm2, m_w_ffn_up, m_ffn_conv_w, m_ffn_conv_b, m_w_ffn_down, m_norm_f, v_meta, v_norm1, v_w_in, v_gdn_conv_w, v_gdn_a_log, v_gdn_dt_bias, v_gdn_norm, v_w_out, v_norm2, v_w_ffn_up, v_ffn_conv_w, v_ffn_conv_b, v_w_ffn_down, v_norm_f):
    names = [n for n, _ in PARAMS]
    shard = dict(zip(names, (meta, norm1, w_in, gdn_conv_w, gdn_a_log, gdn_dt_bias, gdn_norm, w_out, norm2, w_ffn_up,
                             ffn_conv_w, ffn_conv_b, w_ffn_down, norm_f)))
    m_shard = dict(zip(names, (m_meta, m_norm1, m_w_in, m_gdn_conv_w, m_gdn_a_log, m_gdn_dt_bias, m_gdn_norm, m_w_out,
                               m_norm2, m_w_ffn_up, m_ffn_conv_w, m_ffn_conv_b, m_w_ffn_down, m_norm_f)))
    v_shard = dict(zip(names, (v_meta, v_norm1, v_w_in, v_gdn_conv_w, v_gdn_a_log, v_gdn_dt_bias, v_gdn_norm, v_w_out,
                               v_norm2, v_w_ffn_up, v_ffn_conv_w, v_ffn_conv_b, v_w_ffn_down, v_norm_f)))
    return _step(REAL, x[0], loss_target[0], shard, m_shard, v_shard)
```

```python
import functools
from typing import NamedTuple

import numpy as np
import jax
import jax.numpy as jnp
from jax import lax
from jax.experimental import pallas as pl
from jax.experimental.pallas import tpu as pltpu

F32 = jnp.float32
BF16 = jnp.bfloat16
EPS = 1e-6
CHUNK = 64
GDN_DK = 128
RET_DK = 256
GDN_CONV = 4
FFN_CONV = 3
ROPE_BASE = 10000.0
LANES = 128
N_CHIPS = 4
ADAM_LR, ADAM_B1, ADAM_B2, ADAM_EPS, ADAM_WD, ADAM_STEP = 0.001, 0.9, 0.999, 1e-08, 0.01, 10
MIX_COL, CONV_COL, RV_BLOCK, ROT_COL = 0, 4, 7, 8
MESH = pl.DeviceIdType.MESH
VMEM_LIMIT = 56 * 1024 * 1024


class Cfg(NamedTuple):
    d: int
    seq: int
    n_meta: int
    dff: int
    tr: int
    nb: int
    tm: int
    tf: int
    sc: int

    @property
    def hg(self): return self.d // GDN_DK
    @property
    def hr(self): return self.d // RET_DK
    @property
    def L(self): return self.n_meta + self.seq
    @property
    def rp(self): return -(-self.L // 256) * 256
    @property
    def front(self): return self.rp - self.L
    @property
    def xrow(self): return self.rp - self.seq
    @property
    def nch(self): return self.rp // CHUNK
    @property
    def din(self): return 10 * self.d + 2 * self.hg


REAL = Cfg(d=1024, seq=8192, n_meta=16, dff=2816, tr=256, nb=12, tm=1408, tf=1408, sc=6)


def _pallas(body, **kw):
    return pl.pallas_call(body, **kw)


def _sigmoid_raw(x):
    return 1.0 / (1.0 + jnp.exp(-x))


@jax.custom_vjp
def _sigmoid(x):
    return _sigmoid_raw(x)


def _sigmoid_fwd(x):
    s = _sigmoid_raw(x)
    return s, s


def _sigmoid_bwd(s, g):
    return (g * (s * (1.0 - s)),)


_sigmoid.defvjp(_sigmoid_fwd, _sigmoid_bwd)


@jax.custom_vjp
def _silu(x):
    return x * _sigmoid_raw(x)


def _silu_fwd(x):
    s = _sigmoid_raw(x)
    return x * s, (x, s)


def _silu_bwd(res, g):
    x, s = res
    return (g * (s * (1.0 + x * (1.0 - s))),)


_silu.defvjp(_silu_fwd, _silu_bwd)


def _softplus(x):
    return jnp.maximum(x, 0.0) + jnp.log(1.0 + jnp.exp(-jnp.abs(x)))


def _raw_dot(a, b, ta, tb, hi):
    if not hi:
        a = a.astype(BF16)
        b = b.astype(BF16)
    nbatch = a.ndim - 2
    ca = a.ndim - 2 if ta else a.ndim - 1
    cb = b.ndim - 1 if tb else b.ndim - 2
    batch = tuple(range(nbatch))
    return lax.dot_general(a, b, (((ca,), (cb,)), (batch, batch)),
                           precision=lax.Precision.HIGHEST if hi else None,
                           preferred_element_type=F32)


@functools.partial(jax.custom_vjp, nondiff_argnums=(2, 3, 4))
def _dot_p(a, b, ta, tb, hi):
    return _raw_dot(a, b, ta, tb, hi)


def _dot(a, b, ta=False, tb=False, hi=False):
    return _dot_p(a, b, ta, tb, hi)


def _dot_fwd(a, b, ta, tb, hi):
    return _raw_dot(a, b, ta, tb, hi), (a, b)


def _dot_bwd(ta, tb, hi, res, g):
    a, b = res
    if not ta and not tb:
        da, db = _dot(g, b, False, True, hi), _dot(a, g, True, False, hi)
    elif not ta and tb:
        da, db = _dot(g, b, False, False, hi), _dot(g, a, True, False, hi)
    elif ta and not tb:
        da, db = _dot(b, g, False, True, hi), _dot(a, g, False, False, hi)
    else:
        raise NotImplementedError
    return da.astype(a.dtype), db.astype(b.dtype)


_dot_p.defvjp(_dot_fwd, _dot_bwd)


def _iota2(n, m, axis):
    return lax.broadcasted_iota(jnp.int32, (n, m), axis)


def _bcast(mat, nb):
    return jnp.broadcast_to(mat[None], (nb,) + mat.shape)


def _split3(a):
    a0 = a.astype(BF16)
    r1 = a - a0.astype(F32)
    a1 = r1.astype(BF16)
    return a0, a1, (r1 - a1.astype(F32)).astype(BF16)


@functools.partial(jax.custom_vjp, nondiff_argnums=(2,))
def _dot_sel(a, e, te):
    eb = e.astype(BF16)
    p0, p1, p2 = (_raw_dot(p, eb, False, te, False) for p in _split3(a))
    return p0 + (p1 + p2)


def _dot_sel_fwd(a, e, te):
    return _dot_sel(a, e, te), e


def _dot_sel_bwd(te, e, g):
    return _dot_sel(g, e, not te), jnp.zeros_like(e)


_dot_sel.defvjp(_dot_sel_fwd, _dot_sel_bwd)


@jax.custom_vjp
def _sel_dot(e, x):
    eb = e.astype(BF16)
    p0, p1, p2 = (_raw_dot(eb, p, False, False, False) for p in _split3(x))
    return p0 + (p1 + p2)


def _sel_dot_fwd(e, x):
    return _sel_dot(e, x), e


def _sel_dot_bwd(e, g):
    eb = e.astype(BF16)
    p0, p1, p2 = (_raw_dot(eb, p, True, False, False) for p in _split3(g))
    return jnp.zeros_like(e), p0 + (p1 + p2)


_sel_dot.defvjp(_sel_dot_fwd, _sel_dot_bwd)


def _tri_inv_raw(m):
    nb = m.shape[0]
    r, c = _iota2(CHUNK, CHUNK, 0), _iota2(CHUNK, CHUNK, 1)
    t = _bcast((r == c).astype(F32), nb)
    b = 1
    while b < CHUNK:
        sh = b.bit_length() - 1
        off = ((r >> (sh + 1)) == (c >> (sh + 1))) & ((r >> sh) != (c >> sh)) & (r > c)
        cl = jnp.where(off[None], m, 0.0)
        t = t - _raw_dot(_raw_dot(t, cl, False, False, False), t, False, False, False)
        b *= 2
    return t


@jax.custom_vjp
def _tri_inv_given(m, t):
    return t


def _tri_inv_fwd(m, t):
    return t, t


def _tri_inv_bwd(t, g):
    return -_raw_dot(_raw_dot(t, g, True, False, False), t, False, True, False), jnp.zeros_like(t)


_tri_inv_given.defvjp(_tri_inv_fwd, _tri_inv_bwd)


def _rms(h, g):
    return h * lax.rsqrt(jnp.mean(h * h, axis=-1, keepdims=True) + EPS) * g


class In(NamedTuple):
    arr: jax.Array
    spec: pl.BlockSpec
    grad: object = None
    acc: bool = False
    gshape: object = None
    gspec: object = None
    galias: object = None


def _params(grid):
    sem = ("arbitrary",) * len(grid)
    return pltpu.CompilerParams(dimension_semantics=sem, vmem_limit_bytes=VMEM_LIMIT)


def _stage_fwd(name, fn, grid, ins, out_shapes, out_specs):
    n_in = len(ins)

    def body(*refs):
        pids = tuple(pl.program_id(k) for k in range(len(grid)))
        vals = [r[...].astype(F32) for r in refs[:n_in]]
        outs = fn(pids, *vals)
        for o_ref, o in zip(refs[n_in:], outs):
            o_ref[...] = o.reshape(o_ref.shape).astype(o_ref.dtype)

    return _pallas(
        body, out_shape=out_shapes, grid=grid, in_specs=[i.spec for i in ins],
        out_specs=out_specs, name=name, compiler_params=_params(grid))(*[i.arr for i in ins])


def _stage_bwd(name, fn, grid, ins, cots):
    n_in, n_ct = len(ins), len(cots)
    didx = [k for k, i in enumerate(ins) if i.grad is not None]
    aliased = [(o, ins[k].galias) for o, k in enumerate(didx) if ins[k].galias is not None]
    n_al = len(aliased)

    def body(*refs):
        pids = tuple(pl.program_id(k) for k in range(len(grid)))
        vals = [r[...].astype(F32) for r in refs[:n_in]]
        ct_refs = refs[n_in:n_in + n_ct]
        g_refs = refs[n_in + n_ct + n_al:]

        def f(*dv):
            merged = list(vals)
            for k, v in zip(didx, dv):
                merged[k] = v
            return tuple(fn(pids, *merged))

        outs, vjp_fn = jax.vjp(f, *[vals[k].astype(F32) for k in didx])
        cts = tuple(c[...].reshape(o.shape).astype(F32) for c, o in zip(ct_refs, outs))
        grads = vjp_fn(cts)
        first = functools.reduce(jnp.logical_and, [p == 0 for p in pids])
        for k, g_ref, g in zip(didx, g_refs, grads):
            if ins[k].acc:
                @pl.when(first)
                def _(g_ref=g_ref):
                    g_ref[...] = jnp.zeros(g_ref.shape, g_ref.dtype)
                g_ref[...] += g.reshape(g_ref.shape).astype(g_ref.dtype)
            else:
                g_ref[...] = g.reshape(g_ref.shape).astype(g_ref.dtype)

    out_shapes = [jax.ShapeDtypeStruct(ins[k].gshape or ins[k].arr.shape, ins[k].grad) for k in didx]
    out_specs = [ins[k].gspec or ins[k].spec for k in didx]
    return _pallas(
        body, out_shape=out_shapes, grid=grid,
        in_specs=[i.spec for i in ins] + [c[1] for c in cots] + [ANY] * n_al, out_specs=out_specs,
        input_output_aliases={n_in + n_ct + a: o for a, (o, _) in enumerate(aliased)},
        name=name, compiler_params=_params(grid))(*[i.arr for i in ins], *[c[0] for c in cots], *[a for _, a in aliased])


def _full(arr):
    nd = arr.ndim
    return pl.BlockSpec(arr.shape, lambda *p: (0,) * nd)


def _rows(tr, width, blk=0):
    return pl.BlockSpec((tr, width), lambda i: (i, blk))


def _mm(name, a, b, *, tm, tn, tk, out_dtype=F32, add=None, comm=None):
    M, K = a.shape
    N = b.shape[1] if b.ndim == 2 else b.shape[0] * b.shape[2]
    nk = K // tk
    grid = (M // tm, N // tn, nk)
    n_in = 3 if add is not None else 2
    n_ci, n_co = (len(comm.ins), len(comm.outs)) if comm is not None else (0, 0)

    def body(*refs):
        a_ref, b_ref = refs[0], refs[1]
        add_ref = refs[2] if add is not None else None
        c_ins = refs[n_in:n_in + n_ci]
        o_ref = refs[n_in + n_ci]
        c_outs = refs[n_in + n_ci + 1:n_in + n_ci + 1 + n_co]
        scratch = refs[n_in + n_ci + 1 + n_co:]
        acc_ref = scratch[0] if nk > 1 else None
        sems = scratch[1 if nk > 1 else 0:]
        step = (pl.program_id(0) * grid[1] + pl.program_id(1)) * nk + pl.program_id(2)
        if comm is not None:
            @pl.when(step == 0)
            def _():
                comm.start(c_ins, c_outs, sems)

        part = _raw_dot(a_ref[...], b_ref[...], False, False, False)

        def finish(total):
            if add_ref is not None:
                total = total + add_ref[...]
            o_ref[...] = total.astype(o_ref.dtype)

        if nk == 1:
            finish(part)
        else:
            k = pl.program_id(2)

            @pl.when(k == 0)
            def _():
                acc_ref[...] = part

            @pl.when(k > 0)
            def _():
                acc_ref[...] += part

            @pl.when(k == nk - 1)
            def _():
                finish(acc_ref[...])

        if comm is not None:
            @pl.when(step == grid[0] * grid[1] * nk - 1)
            def _():
                comm.finish(c_ins, c_outs, sems)

    b_spec = (pl.BlockSpec((tk, tn), lambda i, j, k: (k, j)) if b.ndim == 2 else
              pl.BlockSpec((None, tk, tn), lambda i, j, k: (j, k, 0)))
    in_specs = [pl.BlockSpec((tm, tk), lambda i, j, k: (i, k)), b_spec]
    args = [a, b]
    if add is not None:
        in_specs.append(pl.BlockSpec((tm, tn), lambda i, j, k: (i, j)))
        args.append(add)
    out_shape = jax.ShapeDtypeStruct((M, N), out_dtype)
    out_spec = pl.BlockSpec((tm, tn), lambda i, j, k: (i, j))
    scratch = [pltpu.VMEM((tm, tn), F32)] if nk > 1 else []
    if comm is None:
        return _pallas(body, out_shape=out_shape, grid=grid, in_specs=in_specs, out_specs=out_spec,
                       scratch_shapes=scratch, name=name, compiler_params=_params(grid))(*args)
    res = _pallas(body, out_shape=[out_shape] + comm.outs, grid=grid, in_specs=in_specs + [ANY] * n_ci,
                  out_specs=[out_spec] + [ANY] * n_co, scratch_shapes=scratch + comm.sems, name=name,
                  compiler_params=_params(grid))(*args, *comm.ins)
    return res[0], res[1:]


def _mm_tn(name, a, b, *, tr, tka, tn, blocked=False):
    R, Ka = a.shape
    N = b.shape[1]
    nr = R // tr
    grid = (Ka // tka, N // tn, nr)
    if blocked:
        out_shape = jax.ShapeDtypeStruct((N // tn, Ka, tn), F32)
        out_spec = pl.BlockSpec((None, tka, tn), lambda i, j, r: (j, i, 0))
    else:
        out_shape = jax.ShapeDtypeStruct((Ka, N), F32)
        out_spec = pl.BlockSpec((tka, tn), lambda i, j, r: (i, j))

    def body(a_ref, b_ref, o_ref):
        r = pl.program_id(2)
        part = _raw_dot(a_ref[...], b_ref[...], True, False, False)

        @pl.when(r == 0)
        def _():
            o_ref[...] = part

        @pl.when(r > 0)
        def _():
            o_ref[...] += part

    return _pallas(
        body, out_shape=out_shape, grid=grid,
        in_specs=[pl.BlockSpec((tr, tka), lambda i, j, r: (r, i)),
                  pl.BlockSpec((tr, tn), lambda i, j, r: (r, j))],
        out_specs=out_spec, name=name, compiler_params=_params(grid))(a, b)


def _conv_fwd(name, x, xcol0, w, b, *, taps, width, tr, tc):
    R = x.shape[0]
    grid = (width // tc, R // tr)
    cb0 = xcol0 // tc
    hrows = 16 if x.dtype == BF16 else 8
    hb = tr // hrows

    def body(*refs):
        x_ref, xp_ref, w_ref = refs[:3]
        b_ref = refs[3] if b is not None else None
        o_ref = refs[-1]
        i = pl.program_id(1)
        xv = x_ref[...].astype(F32)
        prev = jnp.where(i > 0, xp_ref[...].astype(F32)[hrows - 8:, :], 0.0)
        ext = jnp.concatenate([prev, xv], axis=0)
        acc = xv * w_ref[taps - 1:taps, :]
        for s in range(1, taps):
            acc = acc + pltpu.roll(ext, s, 0)[8:, :] * w_ref[taps - 1 - s:taps - s, :]
        if b_ref is not None:
            acc = acc + b_ref[...]
        o_ref[...] = acc.astype(o_ref.dtype)

    in_specs = [pl.BlockSpec((tr, tc), lambda j, i: (i, cb0 + j)),
                pl.BlockSpec((hrows, tc), lambda j, i: (jnp.maximum(i * hb - 1, 0), cb0 + j)),
                pl.BlockSpec((taps, tc), lambda j, i: (0, j))]
    args = [x, x, w]
    if b is not None:
        in_specs.append(pl.BlockSpec((1, tc), lambda j, i: (0, j)))
        args.append(b)
    return _pallas(
        body, out_shape=jax.ShapeDtypeStruct((R, width), BF16), grid=grid, in_specs=in_specs,
        out_specs=pl.BlockSpec((tr, tc), lambda j, i: (i, j)),
        name=name, compiler_params=_params(grid))(*args)


def _conv_bwd(name, x, xcol0, w, dy, *, taps, width, tr, tc, with_bias, dx_into=None):
    R = x.shape[0]
    nr = R // tr
    grid = (width // tc, nr)
    cb0 = xcol0 // tc
    hrows = 16 if dy.dtype == BF16 else 8
    hb = tr // hrows
    n_ext = tr + 8
    n_al = 0 if dx_into is None else 1

    def body(*refs):
        x_ref, w_ref, dy_ref, dyn_ref = refs[:4]
        dx_ref, dw_ref = refs[4 + n_al], refs[5 + n_al]
        db_ref = refs[6 + n_al] if with_bias else None
        i = pl.program_id(1)
        xv = x_ref[...].astype(F32)
        dyv = dy_ref[...].astype(F32)
        nxt = dyn_ref[...].astype(F32)[:8, :]
        dext = jnp.concatenate([dyv, jnp.where(i < nr - 1, nxt, 0.0)], axis=0)
        dx = dyv * w_ref[taps - 1:taps, :]
        dws = [None] * taps
        dws[taps - 1] = jnp.sum(xv * dyv, axis=0, keepdims=True)
        for s in range(1, taps):
            ahead = pltpu.roll(dext, n_ext - s, 0)[:tr, :]
            dx = dx + ahead * w_ref[taps - 1 - s:taps - s, :]
            dws[taps - 1 - s] = jnp.sum(xv * ahead, axis=0, keepdims=True)
        dx_ref[...] = dx.astype(dx_ref.dtype)

        @pl.when(i == 0)
        def _():
            for k in range(taps):
                dw_ref[k:k + 1, :] = dws[k]
            if db_ref is not None:
                db_ref[...] = jnp.sum(dyv, axis=0, keepdims=True)

        @pl.when(i > 0)
        def _():
            for k in range(taps):
                dw_ref[k:k + 1, :] += dws[k]
            if db_ref is not None:
                db_ref[...] += jnp.sum(dyv, axis=0, keepdims=True)

    in_specs = [pl.BlockSpec((tr, tc), lambda j, i: (i, cb0 + j)),
                pl.BlockSpec((taps, tc), lambda j, i: (0, j)),
                pl.BlockSpec((tr, tc), lambda j, i: (i, j)),
                pl.BlockSpec((hrows, tc), lambda j, i: (jnp.minimum((i + 1) * hb, R // hrows - 1), j))]
    args = [x, w, dy, dy]
    if dx_into is None:
        dx_shape, dx_spec, aliases = jax.ShapeDtypeStruct((R, width), BF16), pl.BlockSpec((tr, tc), lambda j, i: (i, j)), {}
    else:
        dx_shape = jax.ShapeDtypeStruct(dx_into.shape, dx_into.dtype)
        dx_spec, aliases = pl.BlockSpec((tr, tc), lambda j, i: (i, cb0 + j)), {4: 0}
        in_specs.append(ANY)
        args.append(dx_into)
    out_shape = [dx_shape, jax.ShapeDtypeStruct((taps, width), F32)]
    out_specs = [dx_spec, pl.BlockSpec((taps, tc), lambda j, i: (0, j))]
    if with_bias:
        out_shape.append(jax.ShapeDtypeStruct((1, width), F32))
        out_specs.append(pl.BlockSpec((1, tc), lambda j, i: (0, j)))
    return _pallas(
        body, out_shape=out_shape, grid=grid, in_specs=in_specs, out_specs=out_specs, input_output_aliases=aliases,
        name=name, compiler_params=_params(grid))(*args)


def _row_mask(cfg, i, tr):
    rows = i * tr + lax.broadcasted_iota(jnp.int32, (tr, 1), 0)
    return (rows >= cfg.front).astype(F32)


def _make_rms_fn(cfg, tr, with_residual):
    def fn(pids, h, g):
        hm = h * _row_mask(cfg, pids[0], tr)
        if with_residual:
            return _rms(hm, g), hm
        return (_rms(hm, g),)
    return fn


def _make_gdn_prep_fn(cfg, tr):
    d, hg = cfg.d, cfg.hg

    def fn(pids, c, tail, alog, dtb):
        cq, ck, cv = c[:, :d], c[:, d:2 * d], c[:, 2 * d:]
        mask = _row_mask(cfg, pids[0], tr)
        j, col = _iota2(LANES, d, 0), _iota2(LANES, d, 1)
        ea = ((col >> 7) == j).astype(F32)
        eb = ((col >> 7) + hg == j).astype(F32)
        al = jnp.sum(alog, axis=0, keepdims=True)
        db = jnp.sum(dtb, axis=0, keepdims=True)
        lg = _dot_sel(-jnp.exp(al) * _softplus(tail + db) * mask, ea, False)
        beta = _dot_sel(_sigmoid(tail) * mask, eb, False)
        sq, sk, sv = _silu(cq), _silu(ck), _silu(cv)
        qs, ks = [], []
        for h in range(hg):
            sl = slice(h * GDN_DK, (h + 1) * GDN_DK)
            qh, kh = sq[:, sl], sk[:, sl]
            qs.append(qh * lax.rsqrt(jnp.sum(qh * qh, axis=-1, keepdims=True) + EPS) * (GDN_DK ** -0.5))
            ks.append(kh * lax.rsqrt(jnp.sum(kh * kh, axis=-1, keepdims=True) + EPS))
        return jnp.concatenate(qs, axis=1), jnp.concatenate(ks, axis=1), sv, beta, lg
    return fn


def _gdn_intra_fn(pids, q, k, v, bB, lB, t_saved=None):
    rows = q.shape[0]
    nb = rows // CHUNK
    q3, k3, v3, b3, l3 = [t.reshape(nb, CHUNK, GDN_DK) for t in (q, k, v, bB, lB)]
    r, c = _iota2(CHUNK, CHUNK, 0), _iota2(CHUNK, CHUNK, 1)
    tril = (r >= c)
    strict = (r > c)
    gcol = _sel_dot(_bcast(tril.astype(F32), nb), l3)
    grow = jnp.swapaxes(gcol, 1, 2)[:, :CHUNK, :]
    diff = gcol[:, :, :CHUNK] - grow
    decay = jnp.where(tril[None], jnp.exp(jnp.where(tril[None], diff, 0.0)), 0.0)
    kb = k3 * b3
    m = jnp.where(strict[None], _dot(kb, k3, False, True) * decay, 0.0)
    t = _tri_inv_raw(m) if t_saved is None else _tri_inv_given(m, t_saved.reshape(nb, CHUNK, CHUNK))
    eg = jnp.exp(gcol)
    u = _dot(t, v3 * b3)
    w = _dot(t, kb * eg)
    attn = _dot(q3, k3, False, True) * decay
    qd = q3 * eg
    glast = jnp.sum(l3, axis=1, keepdims=True)
    kd = k3 * jnp.exp(glast - gcol)
    gl = jnp.exp(glast)
    outs = (u.reshape(rows, GDN_DK), w.reshape(rows, GDN_DK), attn.reshape(1, rows, CHUNK),
            qd.reshape(rows, GDN_DK), kd.reshape(rows, GDN_DK), gl.reshape(1, nb, 1, GDN_DK))
    return outs + (t.reshape(1, rows, CHUNK),) if t_saved is None else outs


def _make_rot_fn(cfg):
    hr = cfg.hr
    half = RET_DK // 2

    def fn(pids, rqk, cos, sin):
        rq, rk = rqk[:, :cfg.d], rqk[:, cfg.d:]

        def rot(t, scale):
            outs = []
            for h in range(hr):
                x1 = t[:, h * RET_DK:h * RET_DK + half]
                x2 = t[:, h * RET_DK + half:(h + 1) * RET_DK]
                outs += [(x1 * cos - x2 * sin) * scale, (x2 * cos + x1 * sin) * scale]
            return jnp.concatenate(outs, axis=1)
        return rot(rq, 1.0), rot(rk, RET_DK ** -0.5)
    return fn


def _make_mix_fn(cfg):
    hg, hr = cfg.hg, cfg.hr

    def fn(pids, oa, ob, pm, gnorm):
        d = cfg.d
        gz, rg, gate_a, gate_b = pm[:, :d], pm[:, d:2 * d], pm[:, 2 * d:3 * d], pm[:, 3 * d:]
        oas = []
        for h in range(hg):
            oh = oa[:, h * GDN_DK:(h + 1) * GDN_DK]
            oas.append(oh * lax.rsqrt(jnp.mean(oh * oh, axis=-1, keepdims=True) + EPS) * gnorm)
        ya = jnp.concatenate(oas, axis=1) * _silu(gz)
        obs = []
        for h in range(hr):
            oh = ob[:, h * RET_DK:(h + 1) * RET_DK]
            obs.append(oh * lax.rsqrt(jnp.mean(oh * oh, axis=-1, keepdims=True) + EPS))
        yb = _silu(rg) * jnp.concatenate(obs, axis=1)
        return (_sigmoid(gate_a) * ya + _sigmoid(gate_b) * yb,)
    return fn


def _act_fn(pids, u):
    f = u.shape[1] // 2
    return (_silu(u[:, :f]) * u[:, f:],)


def _gdn_step(s, u, w, a, qd, kd, gl):
    top = _dot(jnp.concatenate([w, qd], axis=0), s)
    v_new = u - top[:CHUNK]
    bot = _dot(jnp.concatenate([a, kd.T], axis=0), v_new)
    o = top[CHUNK:] + bot[:CHUNK]
    s2 = s * gl + bot[CHUNK:]
    return s2, o


def _gdn_step_bwd(s, u, w, a, qd, kd, gl, ds2, do):
    lw = jnp.concatenate([w, qd], axis=0)
    v_new = u - _raw_dot(w, s, False, False, False)
    dv = _raw_dot(a, do, True, False, False) + _raw_dot(kd, ds2, False, False, False)
    da = _raw_dot(do, v_new, False, True, False)
    dkd = _raw_dot(v_new, ds2, False, True, False)
    dtop = jnp.concatenate([-dv, do], axis=0)
    dlw = _raw_dot(dtop, s, False, True, False)
    ds = ds2 * gl + _raw_dot(lw, dtop, True, False, False)
    dgl = jnp.sum(ds2 * s, axis=0, keepdims=True)
    return ds, dv, dlw[:CHUNK], da, dlw[CHUNK:], dkd, dgl


def _ret_step(s, q, k, v, dm, qdc, kdc, g):
    att = _dot(q, k, False, True) * dm
    bot = _dot(jnp.concatenate([att, (k * kdc).T], axis=0), v)
    o = bot[:CHUNK] + _dot(q * qdc, s)
    s2 = s * g + bot[CHUNK:]
    return s2, o


class Part(NamedTuple):
    body: object
    args: list
    in_specs: list
    out_shape: list
    out_specs: list
    scratch: list
    aliases: dict = {}


def _run_parts(name, grid, parts):
    n_in = [len(p.args) for p in parts]
    n_out = [len(p.out_shape) for p in parts]
    n_sc = [len(p.scratch) for p in parts]
    off_in = [sum(n_in[:k]) for k in range(len(parts))]
    off_out = [sum(n_out[:k]) for k in range(len(parts))]
    off_sc = [sum(n_sc[:k]) for k in range(len(parts))]

    def body(*refs):
        ins, outs, scr = refs[:sum(n_in)], refs[sum(n_in):sum(n_in) + sum(n_out)], refs[sum(n_in) + sum(n_out):]
        for k, p in enumerate(parts):
            p.body(*ins[off_in[k]:off_in[k] + n_in[k]], *outs[off_out[k]:off_out[k] + n_out[k]],
                   *scr[off_sc[k]:off_sc[k] + n_sc[k]])

    aliases = {off_in[k] + i: off_out[k] + o for k, p in enumerate(parts) for i, o in p.aliases.items()}
    res = _pallas(
        body, out_shape=sum((p.out_shape for p in parts), []), grid=grid, in_specs=sum((p.in_specs for p in parts), []),
        out_specs=sum((p.out_specs for p in parts), []), scratch_shapes=sum((p.scratch for p in parts), []),
        input_output_aliases=aliases, name=name, compiler_params=_params(grid))(*sum((p.args for p in parts), []))
    return [res[off_out[k]:off_out[k] + n_out[k]] for k in range(len(parts))]


def _gdn_scan_fwd(cfg, u, w, attn, qd, kd, gl):
    d, hg, nch, sc = cfg.d, cfg.hg, cfg.nch, cfg.sc
    nst = nch // sc

    def body(u_ref, w_ref, a_ref, qd_ref, kd_ref, gl_ref, o_ref, ss_ref, s_ref):
        @pl.when(pl.program_id(0) == 0)
        def _():
            s_ref[...] = jnp.zeros(s_ref.shape, F32)

        states = [s_ref[h] for h in range(hg)]
        for j in range(sc):
            rows = slice(j * CHUNK, (j + 1) * CHUNK)
            outs = []
            for h in range(hg):
                sl = slice(h * GDN_DK, (h + 1) * GDN_DK)
                ss_ref[j, h] = states[h]
                states[h], o = _gdn_step(states[h], u_ref[rows, sl], w_ref[rows, sl], a_ref[h, rows, :],
                                         qd_ref[rows, sl], kd_ref[rows, sl], gl_ref[h, j])
                outs.append(o)
            o_ref[rows, :] = jnp.concatenate(outs, axis=1)
        for h in range(hg):
            s_ref[h] = states[h]

    row = pl.BlockSpec((sc * CHUNK, d), lambda n: (n, 0))
    return Part(
        body, [u, w, attn, qd, kd, gl],
        [row, row, pl.BlockSpec((hg, sc * CHUNK, CHUNK), lambda n: (0, n, 0)), row, row,
         pl.BlockSpec((hg, sc, 1, GDN_DK), lambda n: (0, n, 0, 0))],
        [jax.ShapeDtypeStruct((cfg.rp, d), F32), jax.ShapeDtypeStruct((nch, hg, GDN_DK, GDN_DK), F32)],
        [row, pl.BlockSpec((sc, hg, GDN_DK, GDN_DK), lambda n: (n, 0, 0, 0))],
        [pltpu.VMEM((hg, GDN_DK, GDN_DK), F32)])


def _gdn_scan_bwd(cfg, do, u, w, attn, qd, kd, gl, ss):
    d, hg, nch, sc = cfg.d, cfg.hg, cfg.nch, cfg.sc
    nst = nch // sc

    def body(do_ref, u_ref, w_ref, a_ref, qd_ref, kd_ref, gl_ref, ss_ref,
             du_ref, dw_ref, da_ref, dqd_ref, dkd_ref, dgl_ref, ds_ref):
        @pl.when(pl.program_id(0) == 0)
        def _():
            ds_ref[...] = jnp.zeros(ds_ref.shape, F32)

        dstates = [ds_ref[h] for h in range(hg)]
        for j in reversed(range(sc)):
            rows = slice(j * CHUNK, (j + 1) * CHUNK)
            dus, dws, dqds, dkds = [], [], [], []
            for h in range(hg):
                sl = slice(h * GDN_DK, (h + 1) * GDN_DK)
                args = (ss_ref[j, h], u_ref[rows, sl], w_ref[rows, sl], a_ref[h, rows, :], qd_ref[rows, sl],
                        kd_ref[rows, sl], gl_ref[h, j])
                dstates[h], du, dw, da, dqd, dkd, dgl = _gdn_step_bwd(*args, dstates[h], do_ref[rows, sl])
                da_ref[h, rows, :] = da
                dgl_ref[h, j] = dgl
                dus.append(du)
                dws.append(dw)
                dqds.append(dqd)
                dkds.append(dkd)
            du_ref[rows, :] = jnp.concatenate(dus, axis=1)
            dw_ref[rows, :] = jnp.concatenate(dws, axis=1)
            dqd_ref[rows, :] = jnp.concatenate(dqds, axis=1)
            dkd_ref[rows, :] = jnp.concatenate(dkds, axis=1)
        for h in range(hg):
            ds_ref[h] = dstates[h]

    row = pl.BlockSpec((sc * CHUNK, d), lambda n: (nst - 1 - n, 0))
    aspec = pl.BlockSpec((hg, sc * CHUNK, CHUNK), lambda n: (0, nst - 1 - n, 0))
    gspec = pl.BlockSpec((hg, sc, 1, GDN_DK), lambda n: (0, nst - 1 - n, 0, 0))
    rowshape = jax.ShapeDtypeStruct((cfg.rp, d), F32)
    return Part(
        body, [do, u, w, attn, qd, kd, gl, ss],
        [row, row, row, aspec, row, row, gspec, pl.BlockSpec((sc, hg, GDN_DK, GDN_DK), lambda n: (nst - 1 - n, 0, 0, 0))],
        [rowshape, rowshape, jax.ShapeDtypeStruct(attn.shape, F32), rowshape, rowshape, jax.ShapeDtypeStruct(gl.shape, F32)],
        [row, row, aspec, row, row, gspec],
        [pltpu.VMEM((hg, GDN_DK, GDN_DK), F32)])


def _ret_consts(cfg):
    hr = cfg.hr
    lg = np.log(1.0 - 2.0 ** (-5.0 - np.arange(hr, dtype=np.float64)))
    idx = np.arange(CHUNK, dtype=np.float64)
    tril = np.tril(np.ones((CHUNK, CHUNK), dtype=bool))
    dm = np.where(tril[None], np.exp((idx[:, None] - idx[None, :])[None] * lg[:, None, None]), 0.0)
    qdc = np.exp((idx[None, :] + 1.0) * lg[:, None])
    kdc = np.exp((CHUNK - 1.0 - idx[None, :]) * lg[:, None])
    gch = np.exp(CHUNK * lg)
    qdc = np.broadcast_to(qdc[:, :, None], (hr, CHUNK, RET_DK))
    kdc = np.broadcast_to(kdc[:, :, None], (hr, CHUNK, RET_DK))
    gch = np.broadcast_to(gch[:, None, None], (hr, 1, RET_DK))
    return tuple(jnp.asarray(np.ascontiguousarray(t), F32) for t in (dm, qdc, kdc, gch))


def _ret_scan_fwd(cfg, qr, kr, proj, consts):
    d, hr, nch, sc = cfg.d, cfg.hr, cfg.nch, cfg.sc
    nst = nch // sc
    dm, qdc, kdc, gch = consts

    def body(q_ref, k_ref, v_ref, dm_ref, qdc_ref, kdc_ref, g_ref, o_ref, ss_ref, s_ref):
        @pl.when(pl.program_id(0) == 0)
        def _():
            s_ref[...] = jnp.zeros(s_ref.shape, F32)

        states = [s_ref[h] for h in range(hr)]
        for j in range(sc):
            rows = slice(j * CHUNK, (j + 1) * CHUNK)
            outs = []
            for h in range(hr):
                sl = slice(h * RET_DK, (h + 1) * RET_DK)
                ss_ref[j, h] = states[h]
                states[h], o = _ret_step(states[h], q_ref[rows, sl], k_ref[rows, sl], v_ref[rows, sl], dm_ref[h],
                                         qdc_ref[h], kdc_ref[h], g_ref[h])
                outs.append(o)
            o_ref[rows, :] = jnp.concatenate(outs, axis=1)
        for h in range(hr):
            s_ref[h] = states[h]

    row = pl.BlockSpec((sc * CHUNK, d), lambda n: (n, 0))
    return Part(
        body, [qr, kr, proj, dm, qdc, kdc, gch],
        [row, row, pl.BlockSpec((sc * CHUNK, d), lambda n: (n, RV_BLOCK)), _full(dm), _full(qdc), _full(kdc), _full(gch)],
        [jax.ShapeDtypeStruct((cfg.rp, d), F32), jax.ShapeDtypeStruct((nch, hr, RET_DK, RET_DK), F32)],
        [row, pl.BlockSpec((sc, hr, RET_DK, RET_DK), lambda n: (n, 0, 0, 0))],
        [pltpu.VMEM((hr, RET_DK, RET_DK), F32)])


def _ret_scan_bwd(cfg, do, qr, kr, proj, consts, ss, dproj):
    d, hr, nch, sc = cfg.d, cfg.hr, cfg.nch, cfg.sc
    nst = nch // sc
    dm, qdc, kdc, gch = consts

    def body(do_ref, q_ref, k_ref, v_ref, dm_ref, qdc_ref, kdc_ref, g_ref, ss_ref, _, dq_ref, dk_ref, dv_ref, ds_ref):
        @pl.when(pl.program_id(0) == 0)
        def _():
            ds_ref[...] = jnp.zeros(ds_ref.shape, F32)

        dstates = [ds_ref[h] for h in range(hr)]
        for j in reversed(range(sc)):
            rows = slice(j * CHUNK, (j + 1) * CHUNK)
            dqs, dks, dvs = [], [], []
            for h in range(hr):
                sl = slice(h * RET_DK, (h + 1) * RET_DK)
                cs = (dm_ref[h], qdc_ref[h], kdc_ref[h], g_ref[h])
                _, vjp_fn = jax.vjp(lambda s, q, k, v, cs=cs: _ret_step(s, q, k, v, *cs),
                                    ss_ref[j, h], q_ref[rows, sl].astype(F32), k_ref[rows, sl].astype(F32), v_ref[rows, sl])
                dstates[h], dq, dk, dv = vjp_fn((dstates[h], do_ref[rows, sl]))
                dqs.append(dq)
                dks.append(dk)
                dvs.append(dv)
            dq_ref[rows, :] = jnp.concatenate(dqs, axis=1).astype(dq_ref.dtype)
            dk_ref[rows, :] = jnp.concatenate(dks, axis=1).astype(dk_ref.dtype)
            dv_ref[rows, :] = jnp.concatenate(dvs, axis=1).astype(dv_ref.dtype)
        for h in range(hr):
            ds_ref[h] = dstates[h]

    row = pl.BlockSpec((sc * CHUNK, d), lambda n: (nst - 1 - n, 0))
    rowshape = jax.ShapeDtypeStruct((cfg.rp, d), BF16)
    vspec = pl.BlockSpec((sc * CHUNK, d), lambda n: (nst - 1 - n, RV_BLOCK))
    return Part(
        body, [do, qr, kr, proj, dm, qdc, kdc, gch, ss, dproj],
        [row, row, row, vspec, _full(dm), _full(qdc), _full(kdc), _full(gch),
         pl.BlockSpec((sc, hr, RET_DK, RET_DK), lambda n: (nst - 1 - n, 0, 0, 0)), ANY],
        [rowshape, rowshape, jax.ShapeDtypeStruct(dproj.shape, dproj.dtype)],
        [row, row, vspec],
        [pltpu.VMEM((hr, RET_DK, RET_DK), F32)], {9: 2})


def _final(cfg, h2, normf, tgt):
    d, tr = cfg.d, cfg.xrow
    nr = cfg.rp // tr

    def body(h_ref, g_ref, t_ref, dh_ref, dg_ref, loss_ref):
        i = pl.program_id(0)
        y, vjp_fn = jax.vjp(_rms, h_ref[...], g_ref[...])
        err = jnp.where(i >= 1, y - t_ref[...], 0.0)
        dh, dg = vjp_fn(err * (1.0 / d))
        dh_ref[...] = dh
        part = jnp.zeros((8, LANES), F32) + 0.5 * jnp.sum(err * err) * (1.0 / d)

        @pl.when(i == 0)
        def _():
            dg_ref[...] = dg
            loss_ref[...] = part

        @pl.when(i > 0)
        def _():
            dg_ref[...] += dg
            loss_ref[...] += part

    return _pallas(
        body,
        out_shape=[jax.ShapeDtypeStruct((cfg.rp, d), F32), jax.ShapeDtypeStruct((1, d), F32),
                   jax.ShapeDtypeStruct((8, LANES), F32)],
        grid=(nr,),
        in_specs=[_rows(tr, d), _full(normf), pl.BlockSpec((tr, d), lambda i: (jnp.maximum(i - 1, 0), 0))],
        out_specs=[_rows(tr, d), pl.BlockSpec((1, d), lambda i: (0, 0)), pl.BlockSpec((8, LANES), lambda i: (0, 0))],
        name="final_loss", compiler_params=_params((nr,)))(h2, normf, tgt)


ANY = pl.BlockSpec(memory_space=pl.ANY)


def _place():
    x, y, c = lax.axis_index("x"), lax.axis_index("y"), lax.axis_index("c")
    others = [(1 - x, y), (x, 1 - y), (1 - x, 1 - y)]
    return x, y, c, others


def _row_tile(rows, cap=256):
    return max(t for t in range(16, min(rows, cap) + 1, 16) if rows % t == 0)


class Comm(NamedTuple):
    ins: list
    outs: list
    sems: list
    start: object
    finish: object


def _run_comm(name, comm):
    n_in, n_out = len(comm.ins), len(comm.outs)

    def body(*refs):
        ins, outs, sems = refs[:n_in], refs[n_in:n_in + n_out], refs[n_in + n_out:]
        comm.start(ins, outs, sems)
        comm.finish(ins, outs, sems)

    return _pallas(body, out_shape=comm.outs, in_specs=[ANY] * n_in, out_specs=[ANY] * n_out,
                   scratch_shapes=comm.sems, name=name)(*comm.ins)


def _gather_comm(ws):
    n = len(ws)
    halves = [w.shape[0] // 2 for w in ws]

    def copies(w_refs, o_refs, sems):
        send_sems, recv_sems = sems
        x, y, c, others = _place()
        me = 2 * x + y
        chips = [2 * px + py for px, py in others]

        def piece(a, chip, core):
            return o_refs[a].at[chip, pl.ds(core * halves[a], halves[a]), :]

        def copy(a, k, src, chip, core, to):
            return pltpu.make_async_remote_copy(src_ref=src, dst_ref=piece(a, chip, core), send_sem=send_sems.at[6 * a + k],
                                                recv_sem=recv_sems.at[6 * a + k], device_id=to, device_id_type=MESH)

        def first(j, a):
            return copy(a, j, w_refs[a].at[pl.ds(c * halves[a], halves[a]), :], me, c, (*others[j], c))

        def landed(j, a):
            return copy(a, j, piece(a, chips[j], c), chips[j], c, (x, y, c))

        def passed(j, a):
            return copy(a, 3 + j, piece(a, chips[j], c), chips[j], c, (x, y, 1 - c))

        def from_sibling(j, a):
            return copy(a, 3 + j, piece(a, chips[j], 1 - c), chips[j], 1 - c, (x, y, c))

        return first, landed, passed, from_sibling

    pairs = [(j, a) for j in range(3) for a in range(n)]

    def start(w_refs, o_refs, sems):
        first, _, _, _ = copies(w_refs, o_refs, sems)
        for j, a in pairs:
            first(j, a).start()

    def finish(w_refs, o_refs, sems):
        first, landed, passed, from_sibling = copies(w_refs, o_refs, sems)
        for j, a in pairs:
            landed(j, a).wait_recv()
            passed(j, a).start()
        for j, a in pairs:
            from_sibling(j, a).wait_recv()
        for j, a in pairs:
            first(j, a).wait_send()
            passed(j, a).wait_send()

    return Comm(list(ws), [jax.ShapeDtypeStruct((N_CHIPS,) + w.shape, w.dtype) for w in ws],
                [pltpu.SemaphoreType.DMA((6 * n,)), pltpu.SemaphoreType.DMA((6 * n,))], start, finish)


def _pair_comm(gs):
    n = len(gs)

    def copy(g_refs, o_refs, sems, a):
        x, y, c, _ = _place()
        half = gs[a].shape[1] // 2
        return pltpu.make_async_remote_copy(
            src_ref=g_refs[a].at[:, pl.ds((1 - c) * half, half), :], dst_ref=o_refs[a], send_sem=sems[0].at[a],
            recv_sem=sems[1].at[a], device_id=(x, y, 1 - c), device_id_type=MESH)

    def start(g_refs, o_refs, sems):
        for a in range(n):
            copy(g_refs, o_refs, sems, a).start()

    def finish(g_refs, o_refs, sems):
        for a in range(n):
            copy(g_refs, o_refs, sems, a).wait()

    return Comm(list(gs), [jax.ShapeDtypeStruct((N_CHIPS, g.shape[1] // 2, g.shape[2]), g.dtype) for g in gs],
                [pltpu.SemaphoreType.DMA((n,)), pltpu.SemaphoreType.DMA((n,))], start, finish)


def _pair_exchange(name, gs):
    n = len(gs)

    def body(*refs):
        g_refs, o_refs = refs[:n], refs[n:2 * n]
        send_sems, recv_sems = refs[2 * n:]
        x, y, c, _ = _place()
        cps = []
        for a in range(n):
            half = gs[a].shape[1] // 2
            cp = pltpu.make_async_remote_copy(
                src_ref=g_refs[a].at[:, pl.ds((1 - c) * half, half), :], dst_ref=o_refs[a], send_sem=send_sems.at[a],
                recv_sem=recv_sems.at[a], device_id=(x, y, 1 - c), device_id_type=MESH)
            cp.start()
            cps.append(cp)
        for cp in cps:
            cp.wait()

    return _pallas(
        body, out_shape=[jax.ShapeDtypeStruct((N_CHIPS, g.shape[1] // 2, g.shape[2]), g.dtype) for g in gs],
        in_specs=[ANY] * n, out_specs=[ANY] * n,
        scratch_shapes=[pltpu.SemaphoreType.DMA((n,)), pltpu.SemaphoreType.DMA((n,))], name=name)(*gs)


def _pair_sum(name, g, recv, cidx):
    half, cols = recv.shape[1], recv.shape[2]
    tr = _row_tile(half)
    nblk = half // tr

    def body(c_ref, g_ref, r_ref, o_ref):
        o_ref[...] = (g_ref[...] + r_ref[...]).astype(o_ref.dtype)

    grid_spec = pltpu.PrefetchScalarGridSpec(
        num_scalar_prefetch=1, grid=(N_CHIPS, nblk),
        in_specs=[pl.BlockSpec((1, tr, cols), lambda s, i, c: (s, c[0] * nblk + i, 0)),
                  pl.BlockSpec((1, tr, cols), lambda s, i, c: (s, i, 0))],
        out_specs=pl.BlockSpec((1, tr, cols), lambda s, i, c: (s, i, 0)))
    return _pallas(
        body, out_shape=jax.ShapeDtypeStruct((N_CHIPS, half, cols), BF16), grid_spec=grid_spec,
        name=name, compiler_params=_params((N_CHIPS, nblk)))(cidx, g, recv)


def _exchange_comm(parts):
    n = len(parts)

    def copies(p_refs, o_refs, sems):
        send_sems, recv_sems = sems
        x, y, c, others = _place()
        me = 2 * x + y

        def copy(a, j, src_chip, dst_chip):
            px, py = others[j]
            return pltpu.make_async_remote_copy(
                src_ref=p_refs[a].at[src_chip], dst_ref=o_refs[a].at[dst_chip], send_sem=send_sems.at[3 * a + j],
                recv_sem=recv_sems.at[3 * a + j], device_id=(px, py, c), device_id_type=MESH)

        def send(j, a):
            return copy(a, j, 2 * others[j][0] + others[j][1], me)

        def arrival(j, a):
            return copy(a, j, me, 2 * others[j][0] + others[j][1])

        return send, arrival

    pairs = [(j, a) for j in range(3) for a in range(n)]

    def start(p_refs, o_refs, sems):
        send, _ = copies(p_refs, o_refs, sems)
        for j, a in pairs:
            send(j, a).start()

    def finish(p_refs, o_refs, sems):
        send, arrival = copies(p_refs, o_refs, sems)
        for j, a in pairs:
            arrival(j, a).wait_recv()
        for j, a in pairs:
            send(j, a).wait_send()

    return Comm(list(parts), [jax.ShapeDtypeStruct(p.shape, p.dtype) for p in parts],
                [pltpu.SemaphoreType.DMA((3 * n,)), pltpu.SemaphoreType.DMA((3 * n,))], start, finish)


def _chip_sum(name, part, slots, chip):
    half, cols = slots.shape[1], slots.shape[2]
    tr = _row_tile(half)

    def body(me_ref, p_ref, *rest):
        s_refs, o_ref = rest[:N_CHIPS], rest[N_CHIPS]
        own = p_ref[...].astype(F32)
        v = [jnp.where(me_ref[0] == k, own, s_refs[k][...].astype(F32)) for k in range(N_CHIPS)]
        o_ref[...] = ((v[0] + v[1]) + v[2]) + v[3]

    def slot_spec(k):
        return pl.BlockSpec((None, tr, cols), lambda i, me: (jnp.where(me[0] == k, (k + 1) % N_CHIPS, k), i, 0))

    grid_spec = pltpu.PrefetchScalarGridSpec(
        num_scalar_prefetch=1, grid=(half // tr,),
        in_specs=[pl.BlockSpec((None, tr, cols), lambda i, me: (me[0], i, 0))] + [slot_spec(k) for k in range(N_CHIPS)],
        out_specs=pl.BlockSpec((tr, cols), lambda i, me: (i, 0)))
    return _pallas(
        body, out_shape=jax.ShapeDtypeStruct((half, cols), F32), grid_spec=grid_spec,
        name=name, compiler_params=_params((half // tr,)))(chip, part, *([slots] * N_CHIPS))


def _pair_swap(fins):
    n = len(fins)

    def body(*refs):
        f_refs, o_refs = refs[:n], refs[n:2 * n]
        send_sems, recv_sems = refs[2 * n:]
        x, y, c, _ = _place()
        cps = [pltpu.make_async_remote_copy(src_ref=f_refs[a], dst_ref=o_refs[a], send_sem=send_sems.at[a],
                                            recv_sem=recv_sems.at[a], device_id=(x, y, 1 - c), device_id_type=MESH)
               for a in range(n)]
        for cp in cps:
            cp.start()
        for cp in cps:
            cp.wait()

    return _pallas(
        body, out_shape=[jax.ShapeDtypeStruct(f.shape, f.dtype) for f in fins], in_specs=[ANY] * n, out_specs=[ANY] * n,
        scratch_shapes=[pltpu.SemaphoreType.DMA((n,)), pltpu.SemaphoreType.DMA((n,))], name="grad_pair_swap")(*fins)


def _adamw(name, w, g_own, g_other, m, v, cidx):
    R, cols = w.shape[-2:]
    lead = (None,) * (w.ndim - 2)
    zeros = (0,) * (w.ndim - 2)
    half = R // 2
    tr = _row_tile(half, 128)
    nblk = half // tr
    c1 = 1.0 - ADAM_B1 ** ADAM_STEP
    c2 = 1.0 - ADAM_B2 ** ADAM_STEP

    def body(c_ref, w_ref, go_ref, gs_ref, m_ref, v_ref, g_ref, d_ref, nm_ref, nv_ref):
        mine = (pl.program_id(0) // nblk) == c_ref[0]
        gv = jnp.where(mine, go_ref[...], gs_ref[...])
        nm = ADAM_B1 * m_ref[...] + (1.0 - ADAM_B1) * gv
        nv = ADAM_B2 * v_ref[...] + (1.0 - ADAM_B2) * (gv * gv)
        g_ref[...] = gv
        d_ref[...] = -ADAM_LR * ((nm / c1) / (jnp.sqrt(nv / c2) + ADAM_EPS) + ADAM_WD * w_ref[...])
        nm_ref[...] = nm
        nv_ref[...] = nv

    spec = pl.BlockSpec(lead + (tr, cols), lambda i, c: zeros + (i, 0))
    hspec = pl.BlockSpec((tr, cols), lambda i, c: (i % nblk, 0))
    shape = jax.ShapeDtypeStruct(w.shape, F32)
    grid_spec = pltpu.PrefetchScalarGridSpec(num_scalar_prefetch=1, grid=(R // tr,),
                                             in_specs=[spec, hspec, hspec, spec, spec], out_specs=[spec] * 4)
    return _pallas(
        body, out_shape=[shape] * 4, grid_spec=grid_spec,
        name=name, compiler_params=_params((R // tr,)))(cidx, w, g_own, g_other, m, v)


PARAMS = (("meta", 1), ("norm1", None), ("w_in", 2), ("gdn_conv_w", 2), ("gdn_a_log", None), ("gdn_dt_bias", None),
          ("gdn_norm", None), ("w_out", 1), ("norm2", None), ("w_ffn_up", 2), ("ffn_conv_w", 2), ("ffn_conv_b", None),
          ("w_ffn_down", 1), ("norm_f", None))
BIG = ("w_in", "w_out", "w_ffn_up", "w_ffn_down")
PACK_ALIGN = 1024
PACK_ROWS_ALIGN = 32


def _pack(arrs, dtype):
    parts, total = [], 0
    for a in arrs:
        f = a.reshape(-1).astype(dtype)
        pad = (-f.shape[0]) % PACK_ALIGN
        parts.append(jnp.pad(f, (0, pad)) if pad else f)
        total += f.shape[0] + pad
    rows = total // LANES
    rpad = (-rows) % PACK_ROWS_ALIGN
    if rpad:
        parts.append(jnp.zeros((rpad * LANES,), dtype))
    return jnp.concatenate(parts).reshape(rows + rpad, LANES)


def _unpack(buf, shapes):
    flat = buf.reshape(-1)
    outs, off = [], 0
    for s in shapes:
        n = int(np.prod(s))
        outs.append(flat[off:off + n].reshape(s))
        off += n + (-n) % PACK_ALIGN
    return outs


def _split4(a, axis):
    n = a.shape[axis] // N_CHIPS
    return [lax.slice_in_dim(a, s * n, (s + 1) * n, axis=axis) for s in range(N_CHIPS)]


PROJ_ORDER = (3, 7, 8, 9, 0, 1, 2, 6, 4, 5)


def _reorder_w_in(w, cfg):
    d, hg = cfg.d, cfg.hg

    def block(k):
        off = k * d + (2 * hg if k >= 4 else 0)
        return w[:, off:off + d]

    tail = jnp.pad(w[:, 4 * d:4 * d + 2 * hg], ((0, 0), (0, LANES - 2 * hg)))
    return jnp.concatenate([block(k) for k in PROJ_ORDER] + [tail], axis=1)


def _restore_w_in(wr, cfg):
    d, hg = cfg.d, cfg.hg
    at = {k: i for i, k in enumerate(PROJ_ORDER)}
    block = lambda k: wr[:, at[k] * d:(at[k] + 1) * d]
    return jnp.concatenate([block(k) for k in range(4)] + [wr[:, 10 * d:10 * d + 2 * hg]] +
                           [block(k) for k in range(4, 10)], axis=1)


def _step(cfg, x, tgt, shard, m_shard, v_shard):
    d, hg, dff, rp, tr, tm = cfg.d, cfg.hg, cfg.dff, cfg.rp, cfg.tr, cfg.tm
    nrow = rp // tr
    assert cfg.tf * N_CHIPS == 2 * dff and cfg.din % N_CHIPS == 0
    cidx = lax.axis_index("c").astype(jnp.int32).reshape(1)
    chip = (2 * lax.axis_index("x") + lax.axis_index("y")).astype(jnp.int32).reshape(1)

    axis = dict(PARAMS)
    small = ("meta", "gdn_conv_w", "ffn_conv_w")
    small_shapes = [shard[n].shape for n in small]
    mine = [shard[n][0].astype(BF16) for n in BIG] + [_pack([shard[n] for n in small], F32)]

    def with_own(gathered, own):
        return [lax.dynamic_update_slice(g, w[None], (chip[0], 0, 0)) for g, w in zip(gathered, own)]

    g_in, g_small = _run_comm("weights_gather_first", _gather_comm([mine[0], mine[4]]))
    g_small, = with_own([g_small], [mine[4]])
    w_in_r = _reorder_w_in(jnp.concatenate([jnp.where(chip[0] == s, mine[0], g_in[s]) for s in range(N_CHIPS)], axis=1),
                           cfg)
    per_chip = [_unpack(g_small[s], small_shapes) for s in range(N_CHIPS)]
    full = {n: jnp.concatenate([per_chip[s][k] for s in range(N_CHIPS)], axis=axis[n]) for k, n in enumerate(small)}
    meta = full["meta"]
    gconv_w = full["gdn_conv_w"][0]
    fconv_w = full["ffn_conv_w"][0]
    norm1, norm2, gnorm = shard["norm1"], shard["norm2"], shard["gdn_norm"]
    normf = shard["norm_f"].reshape(1, d)
    fconv_b = shard["ffn_conv_b"]
    alog = jnp.pad(shard["gdn_a_log"], ((0, 7), (0, LANES - hg)))
    dtb = jnp.pad(shard["gdn_dt_bias"], ((0, 7), (0, LANES - hg)))

    h0 = jnp.concatenate([jnp.zeros((cfg.front, d), F32), meta, x], axis=0)
    half = RET_DK // 2
    pos = np.arange(rp, dtype=np.float32) - np.float32(cfg.front)
    inv = (np.float32(1.0) / np.float32(ROPE_BASE) ** (np.arange(half, dtype=np.float32) / np.float32(half))).astype(np.float32)
    ang = pos[:, None] * inv[None, :]
    cos, sin = jnp.asarray(np.cos(ang), F32), jnp.asarray(np.sin(ang), F32)
    rconsts = _ret_consts(cfg)

    tr_n = 3 * tr if rp % (3 * tr) == 0 else tr
    rms_f = _make_rms_fn(cfg, tr_n, False)
    rms_b = _make_rms_fn(cfg, tr_n, True)
    rowshape = jax.ShapeDtypeStruct((rp, d), F32)
    rspec = _rows(tr, d)
    nspec = _rows(tr_n, d)

    def rms_fwd(name, h, g):
        return _stage_fwd(name, rms_f, (rp // tr_n,), [In(h, nspec), In(g, _full(g))],
                          [jax.ShapeDtypeStruct((rp, d), BF16)], [nspec])[0]

    wm = 10 * d
    w_main, w_tail = w_in_r[:, :wm], w_in_r[:, wm:]
    tn_in = 2560 if wm % 2560 == 0 else LANES
    hn1 = rms_fwd("rms1_fwd", h0, norm1)
    proj, rest = _mm("proj_fwd", hn1, w_main, tm=tm, tn=tn_in, tk=d, out_dtype=BF16, comm=_gather_comm(mine[1:4]))
    ptail = _mm("proj_tail_fwd", hn1, w_tail, tm=tm, tn=LANES, tk=d)
    g_out, g_up, g_down = with_own(rest, mine[1:4])
    w_out = g_out.reshape(d, d)
    w_up = g_up
    w_up_t = jnp.swapaxes(g_up, 1, 2).reshape(2 * dff, d)
    w_down = g_down.reshape(dff, d)
    cqkv = _conv_fwd("gdn_conv_fwd", proj, CONV_COL * d, gconv_w, None, taps=GDN_CONV, width=3 * d, tr=tr, tc=d)
    prep_fn = _make_gdn_prep_fn(cfg, tr)
    prep_ins = [In(cqkv, _rows(tr, 3 * d), BF16), In(ptail, _rows(tr, LANES), BF16),
                In(alog, _full(alog), F32, True), In(dtb, _full(dtb), F32, True)]
    qn, kn, vv, bB, lB = _stage_fwd("gdn_prep_fwd", prep_fn, (nrow,), prep_ins, [rowshape] * 5, [rspec] * 5)

    trg = cfg.nb * CHUNK
    gi_grid = (rp // trg, hg)
    hspec = pl.BlockSpec((trg, GDN_DK), lambda i, h: (i, h))
    aspec = pl.BlockSpec((1, trg, CHUNK), lambda i, h: (h, i, 0))
    gspec = pl.BlockSpec((1, cfg.nb, 1, GDN_DK), lambda i, h: (h, i, 0, 0))
    intra_ins = [In(t, hspec, F32) for t in (qn, kn, vv, bB, lB)]
    ashape = jax.ShapeDtypeStruct((hg, rp, CHUNK), F32)
    intra_shapes = [rowshape, rowshape, ashape, rowshape, rowshape, jax.ShapeDtypeStruct((hg, cfg.nch, 1, GDN_DK), F32), ashape]
    intra_specs = [hspec, hspec, aspec, hspec, hspec, gspec, aspec]
    gu, gw, gattn, gqd, gkd, ggl, gtinv = _stage_fwd("gdn_intra_fwd", _gdn_intra_fn, gi_grid, intra_ins, intra_shapes,
                                                     intra_specs)
    rot_fn = _make_rot_fn(cfg)

    def rot_ins(dproj=None):
        return [In(proj, _rows(tr_n, 2 * d, ROT_COL // 2), BF16, galias=dproj, gshape=(rp, wm)),
                In(cos, _rows(tr_n, half)), In(sin, _rows(tr_n, half))]

    qr, kr = _stage_fwd("rot_fwd", rot_fn, (rp // tr_n,), rot_ins(), [jax.ShapeDtypeStruct((rp, d), BF16)] * 2, [nspec] * 2)
    nst = cfg.nch // cfg.sc
    oa, gss = _run_parts("gdn_scan_fwd", (nst,), [_gdn_scan_fwd(cfg, gu, gw, gattn, gqd, gkd, ggl)])[0]
    ob, rss = _run_parts("ret_scan_fwd", (nst,), [_ret_scan_fwd(cfg, qr, kr, proj, rconsts)])[0]

    mix_fn = _make_mix_fn(cfg)
    mix_ins = [In(oa, rspec, F32), In(ob, rspec, F32), In(proj, _rows(tr, 4 * d, MIX_COL // 4), BF16, gshape=(rp, wm)),
               In(gnorm, _full(gnorm), F32, True)]
    ymix = _stage_fwd("mix_fwd", mix_fn, (nrow,), mix_ins, [jax.ShapeDtypeStruct((rp, d), BF16)], [rspec])[0]
    h1 = _mm("out_proj_fwd", ymix, w_out, tm=tm, tn=d, tk=d, add=h0)

    hn2 = rms_fwd("rms2_fwd", h1, norm2)
    up = _mm("ffn_up_fwd", hn2, w_up, tm=tm, tn=cfg.tf, tk=d, out_dtype=BF16)
    uc = _conv_fwd("ffn_conv_fwd", up, 0, fconv_w, fconv_b, taps=FFN_CONV, width=2 * dff, tr=tr, tc=cfg.tf)
    tra = tr
    act_ins = [In(uc, _rows(tra, 2 * dff), BF16)]
    act_spec = _rows(tra, dff)
    act = _stage_fwd("ffn_act_fwd", _act_fn, (rp // tra,), act_ins, [jax.ShapeDtypeStruct((rp, dff), BF16)], [act_spec])[0]
    h2 = _mm("ffn_down_fwd", act, w_down, tm=tm, tn=d, tk=cfg.tf, add=h1)

    dh2, g_normf, loss_blk = _final(cfg, h2, normf, tgt)
    loss = lax.psum(loss_blk[0, 0], ("x", "y", "c"))

    g_w_down = _mm_tn("ffn_down_dw", act, dh2, tr=tm, tka=cfg.tf, tn=d)
    g_w_down4 = g_w_down.reshape(N_CHIPS, dff // N_CHIPS, d)
    dact, (recv_down,) = _mm("ffn_down_dx", dh2, w_down.T, tm=tm, tn=cfg.tf, tk=d, out_dtype=BF16,
                             comm=_pair_comm([g_w_down4]))
    duc, = _stage_bwd("ffn_act_bwd", _act_fn, (rp // tra,), act_ins, [(dact, act_spec)])
    dup, g_fconv_w, g_fconv_b = _conv_bwd("ffn_conv_bwd", up, 0, fconv_w, duc, taps=FFN_CONV, width=2 * dff,
                                          tr=tr, tc=cfg.tf, with_bias=True)
    g_w_up = _mm_tn("ffn_up_dw", hn2, dup, tr=tm, tka=d, tn=cfg.tf, blocked=True)

    def pair_reduce(tag, names, arrs):
        recvs = _pair_exchange("grad_pair_exchange_" + tag, arrs)
        return [_pair_sum("grad_pair_sum_" + n, g, r, cidx) for n, g, r in zip(names, arrs, recvs)]

    parts_ffn = ([_pair_sum("grad_pair_sum_w_ffn_down", g_w_down4, recv_down, cidx)] +
                 pair_reduce("ffn", ["w_ffn_up"], [g_w_up]))
    dhn2, slots_ffn = _mm("ffn_up_dx", dup, w_up_t, tm=tm, tn=d, tk=2 * cfg.tf, out_dtype=BF16,
                          comm=_exchange_comm(parts_ffn))

    def rms_bwd(name, h, g, dhn, dres):
        ins = [In(h, nspec, F32), In(g, _full(g), F32, True)]
        return _stage_bwd(name, rms_b, (rp // tr_n,), ins, [(dhn, nspec), (dres, nspec)])

    dh1, g_norm2 = rms_bwd("rms2_bwd", h1, norm2, dhn2, dh2)
    g_w_out = _mm_tn("out_proj_dw", ymix, dh1, tr=tm, tka=d, tn=d)
    g_w_out4 = g_w_out.reshape(N_CHIPS, d // N_CHIPS, d)
    dymix, (recv_out,) = _mm("out_proj_dx", dh1, w_out.T, tm=tm, tn=d, tk=d, comm=_pair_comm([g_w_out4]))
    doa, dob, dproj, g_gnorm = _stage_bwd("mix_bwd", mix_fn, (nrow,), mix_ins, [(dymix, rspec)])

    dqr, dkr, dproj = _run_parts("ret_scan_bwd", (nst,), [_ret_scan_bwd(cfg, dob, qr, kr, proj, rconsts, rss, dproj)])[0]
    dproj, = _stage_bwd("rot_bwd", rot_fn, (rp // tr_n,), rot_ins(dproj), [(dqr, nspec), (dkr, nspec)])
    dgu, dgw, dgattn, dgqd, dgkd, dggl = _run_parts(
        "gdn_scan_bwd", (nst,), [_gdn_scan_bwd(cfg, doa, gu, gw, gattn, gqd, gkd, ggl, gss)])[0]

    intra_cots = [(dgu, hspec), (dgw, hspec), (dgattn, aspec), (dgqd, hspec), (dgkd, hspec), (dggl, gspec)]
    dqn, dkn, dvv, dbB, dlB = _stage_bwd("gdn_intra_bwd", _gdn_intra_fn, gi_grid, intra_ins + [In(gtinv, aspec)], intra_cots)
    dcqkv, dtail, g_alog, g_dtb = _stage_bwd(
        "gdn_prep_bwd", prep_fn, (nrow,), prep_ins, [(t, rspec) for t in (dqn, dkn, dvv, dbB, dlB)])
    dproj, g_gconv_w = _conv_bwd("gdn_conv_bwd", proj, CONV_COL * d, gconv_w, dcqkv, taps=GDN_CONV, width=3 * d,
                                 tr=tr, tc=d, with_bias=False, dx_into=dproj)
    g_w_in_r = jnp.concatenate([_mm_tn("proj_dw", hn1, dproj, tr=tm, tka=d, tn=tn_in),
                                _mm_tn("proj_tail_dw", hn1, dtail, tr=tm, tka=d, tn=LANES)], axis=1)
    g_in4 = jnp.stack(_split4(_restore_w_in(g_w_in_r, cfg), 1))
    parts_mix = [_pair_sum("grad_pair_sum_w_out", g_w_out4, recv_out, cidx)] + pair_reduce("mix", ["w_in"], [g_in4])
    dhn1_tail = _mm("proj_tail_dx", dtail, w_tail.T, tm=tm, tn=d, tk=LANES)
    dhn1, slots_mix = _mm("proj_dx", dproj, w_main.T, tm=tm, tn=d, tk=tn_in // 2 if tn_in > LANES else LANES, add=dhn1_tail,
                          out_dtype=BF16, comm=_exchange_comm(parts_mix))
    dh0, g_norm1 = rms_bwd("rms1_bwd", h0, norm1, dhn1, dh1)

    grad_x = dh0[cfg.xrow:]
    small_grads = {
        "meta": dh0[cfg.front:cfg.xrow], "norm1": g_norm1, "gdn_conv_w": g_gconv_w[None],
        "gdn_a_log": g_alog[0:1, :hg], "gdn_dt_bias": g_dtb[0:1, :hg], "gdn_norm": g_gnorm, "norm2": g_norm2,
        "ffn_conv_w": g_fconv_w[None], "ffn_conv_b": g_fconv_b, "norm_f": g_normf.reshape(d),
    }

    small_names = [n for n, _ in PARAMS if n not in BIG]
    g_small = jnp.stack([_pack([small_grads[n] if axis[n] is None else _split4(small_grads[n], axis[n])[s]
                                for n in small_names], F32) for s in range(N_CHIPS)])
    parts_small = pair_reduce("small", ["small"], [g_small])
    slots_small = _run_comm("grad_exchange_small", _exchange_comm(parts_small))
    tags = ["w_in", "w_out", "w_ffn_up", "w_ffn_down", "small"]
    parts = [parts_mix[1], parts_mix[0], parts_ffn[1], parts_ffn[0], parts_small[0]]
    slots = [slots_mix[1], slots_mix[0], slots_ffn[1], slots_ffn[0], slots_small[0]]
    fins = [_chip_sum("grad_chip_sum_" + t, p, s, chip) for t, p, s in zip(tags, parts, slots)]
    sibs = _pair_swap(fins)

    def flat2(a):
        return a.reshape(-1, a.shape[-1])

    outs = {}
    for k, t in enumerate(BIG):
        res = _adamw("adamw_" + t, flat2(shard[t]), fins[k], sibs[k], flat2(m_shard[t]), flat2(v_shard[t]), cidx)
        outs[t] = [r.reshape(shard[t].shape) for r in res]
    small_shapes_all = [shard[n].shape for n in small_names]
    pk = lambda src: _pack([src[n] for n in small_names], F32)
    res = _adamw("adamw_small", pk(shard), fins[4], sibs[4], pk(m_shard), pk(v_shard), cidx)
    for k, r in enumerate(res):
        for n, a in zip(small_names, _unpack(r, small_shapes_all)):
            outs.setdefault(n, [None] * 4)[k] = a
    names = [n for n, _ in PARAMS]
    return (loss, grad_x[None], *[outs[n][k] for k in range(4) for n in names])


def kernel(x, meta, norm1, w_in, gdn_conv_w, gdn_a_log, gdn_dt_bias, gdn_norm, w_out, norm2, w_ffn_up, ffn_conv_w, ffn_conv_b, w_ffn_down, norm_f, loss_target, m_meta, m_norm1, m_w_in, m_gdn_conv_w, m_gdn_a_log, m_gdn_dt_bias, m_gdn_norm, m_w_out, m_norm2, m_w_ffn_up, m_ffn_conv_w, m_ffn_conv_b, m_w_ffn_down, m_norm_f, v_meta, v_norm1, v_w_in, v_gdn_conv_w, v_gdn_a_log, v_gdn_dt_bias, v_gdn_norm, v_w_out, v_norm2, v_w_ffn_up, v_ffn_conv_w, v_ffn_conv_b, v_w_ffn_down, v_norm_f):
    names = [n for n, _ in PARAMS]
    shard = dict(zip(names, (meta, norm1, w_in, gdn_conv_w, gdn_a_log, gdn_dt_bias, gdn_norm, w_out, norm2, w_ffn_up,
                             ffn_conv_w, ffn_conv_b, w_ffn_down, norm_f)))
    m_shard = dict(zip(names, (m_meta, m_norm1, m_w_in, m_gdn_conv_w, m_gdn_a_log, m_gdn_dt_bias, m_gdn_norm, m_w_out,
                               m_norm2, m_w_ffn_up, m_ffn_conv_w, m_ffn_conv_b, m_w_ffn_down, m_norm_f)))
    v_shard = dict(zip(names, (v_meta, v_norm1, v_w_in, v_gdn_conv_w, v_gdn_a_log, v_gdn_dt_bias, v_gdn_norm, v_w_out,
                               v_norm2, v_w_ffn_up, v_ffn_conv_w, v_ffn_conv_b, v_w_ffn_down, v_norm_f)))
    return _step(REAL, x[0], loss_target[0], shard, m_shard, v_shard)
```
